```python
import math
import jax, jax.numpy as jnp
from jax import lax
import numpy as np

D_MODEL = 1024
BATCH = 8
SEQ = 8192
DEPTH = 1

N_META = 16
GRID_W = 64
SSM_GROUP = 16
SSM_STATE = 64
SSM_WIDTH = D_MODEL // 2
SSM_GROUPS = SSM_WIDTH // SSM_GROUP
HEAD_DIM = 64
N_HEADS = D_MODEL // HEAD_DIM
N_KV_HEADS = N_HEADS // 4
Q_WIDTH = N_HEADS * HEAD_DIM
KV_WIDTH = N_KV_HEADS * HEAD_DIM
IN_WIDTH = SSM_WIDTH + Q_WIDTH + 2 * KV_WIDTH + 2 * D_MODEL
D_FF = 4 * D_MODEL
Q_BLOCK = 128
ROPE_THETA = 10000.0
NORM_EPS = 1e-6
DT_MIN = 1e-3
DT_MAX = 1e-1
EIG_RE_MAX = -1e-4

kernel_name = "hybrid_s5_gqa_axial_gated_encoder"


def rms_norm(x, g):
    xf = x.astype(jnp.float32)
    y = xf * lax.rsqrt(jnp.mean(xf * xf, axis=-1, keepdims=True) + NORM_EPS)
    return (y * g.astype(jnp.float32)).astype(x.dtype)


def axial_rope_tables(n_total):
    n_real = n_total - N_META
    rows = n_real // GRID_W
    row_id = jnp.repeat(jnp.arange(rows, dtype=jnp.float32), GRID_W)
    col_id = jnp.tile(jnp.arange(GRID_W, dtype=jnp.float32), rows)
    pairs_per_axis = HEAD_DIM // 4
    inv_freq = ROPE_THETA ** (-jnp.arange(pairs_per_axis, dtype=jnp.float32) / pairs_per_axis)
    ang = jnp.concatenate([row_id[:, None] * inv_freq, col_id[:, None] * inv_freq], axis=-1)
    ang = jnp.concatenate([jnp.zeros((N_META, HEAD_DIM // 2), jnp.float32), ang], axis=0)
    return jnp.cos(ang), jnp.sin(ang)


def apply_rope(t, cos, sin):
    tf = t.astype(jnp.float32).reshape(t.shape[:-1] + (HEAD_DIM // 2, 2))
    t0, t1 = tf[..., 0], tf[..., 1]
    c = cos[:, None, :]
    s = sin[:, None, :]
    out = jnp.stack([t0 * c - t1 * s, t0 * s + t1 * c], axis=-1)
    return out.reshape(t.shape)


def gqa_attention(q, k, v):
    b, l = q.shape[0], q.shape[1]
    rep = N_HEADS // N_KV_HEADS
    scale = HEAD_DIM ** -0.5
    qg = q.reshape(b, l, N_KV_HEADS, rep, HEAD_DIM)

    def attend(qb):
        s = jnp.einsum('bqgrd,bkgd->bgrqk', qb, k) * scale
        p = jax.nn.softmax(s, axis=-1)
        return jnp.einsum('bgrqk,bkgd->bqgrd', p, v)

    out_meta = attend(qg[:, :N_META]).reshape(b, N_META, Q_WIDTH)
    n_real = l - N_META
    n_blk = n_real // Q_BLOCK
    q_blocks = qg[:, N_META:].reshape(b, n_blk, Q_BLOCK, N_KV_HEADS, rep, HEAD_DIM)
    q_blocks = jnp.transpose(q_blocks, (1, 0, 2, 3, 4, 5))
    out_real = lax.map(attend, q_blocks)
    out_real = jnp.transpose(out_real, (1, 0, 2, 3, 4, 5)).reshape(b, n_real, Q_WIDTH)
    return jnp.concatenate([out_meta, out_real], axis=1)


def _complex_scan_combine(e1, e2):
    a1r, a1i, b1r, b1i = e1
    a2r, a2i, b2r, b2i = e2
    return (a1r * a2r - a1i * a2i,
            a1r * a2i + a1i * a2r,
            a2r * b1r - a2i * b1i + b2r,
            a2r * b1i + a2i * b1r + b2i)


def s5_direction(uf, a_re, a_im, log_dt, b_re, b_im, c_re, c_im, reverse):
    l = uf.shape[1]
    f32 = jnp.float32
    dt = jnp.exp(log_dt.astype(f32))[:, None]
    lam_re = jnp.minimum(a_re.astype(f32), EIG_RE_MAX)
    lam_im = a_im.astype(f32)
    mag = jnp.exp(lam_re * dt)
    ang = lam_im * dt
    lb_re = mag * jnp.cos(ang)
    lb_im = mag * jnp.sin(ang)
    num_re = lb_re - 1.0
    num_im = lb_im
    den = lam_re * lam_re + lam_im * lam_im
    f_re = (num_re * lam_re + num_im * lam_im) / den
    f_im = (num_im * lam_re - num_re * lam_im) / den
    br = b_re.astype(f32)
    bi = b_im.astype(f32)
    bb_re = f_re[..., None] * br - f_im[..., None] * bi
    bb_im = f_re[..., None] * bi + f_im[..., None] * br
    bu_re = jnp.einsum('blgp,gnp->blgn', uf, bb_re)
    bu_im = jnp.einsum('blgp,gnp->blgn', uf, bb_im)
    shape_a = (1, l) + lb_re.shape
    a_seq_re = jnp.broadcast_to(lb_re, shape_a)
    a_seq_im = jnp.broadcast_to(lb_im, shape_a)
    _, _, x_re, x_im = lax.associative_scan(
        _complex_scan_combine, (a_seq_re, a_seq_im, bu_re, bu_im), reverse=reverse, axis=1)
    return (jnp.einsum('blgn,gpn->blgp', x_re, c_re.astype(f32))
            - jnp.einsum('blgn,gpn->blgp', x_im, c_im.astype(f32)))


def s5_bidirectional(u, a_re, a_im, log_dt, b_re, b_im, c_re, c_im, d):
    bsz, l = u.shape[0], u.shape[1]
    uf = u.astype(jnp.float32).reshape(bsz, l, SSM_GROUPS, SSM_GROUP)
    y = uf * d.astype(jnp.float32).reshape(SSM_GROUPS, SSM_GROUP)
    for direction in range(2):
        y = y + s5_direction(uf, a_re[direction], a_im[direction], log_dt[direction],
                             b_re[direction], b_im[direction], c_re[direction], c_im[direction],
                             reverse=(direction == 1))
    return y.reshape(bsz, l, SSM_WIDTH)


def _fwd_setup_inputs(seed: int = 0) -> dict:
    key = jax.random.key(seed)
    ks = jax.random.split(key, 24)
    f32 = jnp.float32

    def nrm(k, shape, scale):
        return jax.random.normal(k, shape, f32) * scale

    def gain(k, shape):
        return 1.0 + 0.02 * jax.random.normal(k, shape, f32)

    n_idx = jnp.arange(SSM_STATE, dtype=f32)
    ssm_shape = (DEPTH, 2, SSM_GROUPS, SSM_STATE)
    return {
        "x": jax.random.normal(ks[0], (BATCH, SEQ, D_MODEL), f32),
        "meta_tokens": nrm(ks[1], (N_META, D_MODEL), 1.0),
        "norm_mix_g": gain(ks[2], (DEPTH, D_MODEL)),
        "w_in": nrm(ks[3], (DEPTH, D_MODEL, IN_WIDTH), D_MODEL ** -0.5),
        "ssm_a_re": -0.5 + 0.01 * jax.random.normal(ks[4], ssm_shape, f32),
        "ssm_a_im": jnp.pi * n_idx + 0.01 * jax.random.normal(ks[5], ssm_shape, f32),
        "ssm_log_dt": jax.random.uniform(ks[6], (DEPTH, 2, SSM_GROUPS), f32,
                                         minval=math.log(DT_MIN), maxval=math.log(DT_MAX)),
        "ssm_b_re": nrm(ks[7], (DEPTH, 2, SSM_GROUPS, SSM_STATE, SSM_GROUP), (2 * SSM_GROUP) ** -0.5),
        "ssm_b_im": nrm(ks[8], (DEPTH, 2, SSM_GROUPS, SSM_STATE, SSM_GROUP), (2 * SSM_GROUP) ** -0.5),
        "ssm_c_re": nrm(ks[9], (DEPTH, 2, SSM_GROUPS, SSM_GROUP, SSM_STATE), SSM_STATE ** -0.5),
        "ssm_c_im": nrm(ks[10], (DEPTH, 2, SSM_GROUPS, SSM_GROUP, SSM_STATE), SSM_STATE ** -0.5),
        "ssm_d": nrm(ks[11], (DEPTH, SSM_WIDTH), 1.0),
        "w_glu": nrm(ks[12], (DEPTH, SSM_WIDTH, SSM_WIDTH), SSM_WIDTH ** -0.5),
        "b_glu": nrm(ks[13], (DEPTH, SSM_WIDTH), 0.02),
        "q_norm_g": gain(ks[14], (DEPTH, HEAD_DIM)),
        "k_norm_g": gain(ks[15], (DEPTH, HEAD_DIM)),
        "w_ssm_proj": nrm(ks[16], (DEPTH, SSM_WIDTH, D_MODEL), SSM_WIDTH ** -0.5),
        "w_attn_proj": nrm(ks[17], (DEPTH, Q_WIDTH, D_MODEL), Q_WIDTH ** -0.5),
        "w_out": nrm(ks[18], (DEPTH, D_MODEL, D_MODEL), D_MODEL ** -0.5),
        "norm_mlp_g": gain(ks[19], (DEPTH, D_MODEL)),
        "w_mlp_in": nrm(ks[20], (DEPTH, D_MODEL, D_FF), D_MODEL ** -0.5),
        "w_mlp_out": nrm(ks[21], (DEPTH, D_FF, D_MODEL), D_FF ** -0.5),
        "norm_final_g": gain(ks[22], (D_MODEL,)),
    }


def _fwd_reference(x, meta_tokens, norm_mix_g, w_in, ssm_a_re, ssm_a_im, ssm_log_dt,
              ssm_b_re, ssm_b_im, ssm_c_re, ssm_c_im, ssm_d, w_glu, b_glu,
              q_norm_g, k_norm_g, w_ssm_proj, w_attn_proj, w_out,
              norm_mlp_g, w_mlp_in, w_mlp_out, norm_final_g):
    dtype = x.dtype
    bsz = x.shape[0]
    meta = jnp.broadcast_to(meta_tokens.astype(dtype)[None], (bsz, N_META, D_MODEL))
    h_res = jnp.concatenate([meta, x], axis=1)
    l = h_res.shape[1]
    cos, sin = axial_rope_tables(l)
    split_at = [SSM_WIDTH, SSM_WIDTH + Q_WIDTH, SSM_WIDTH + Q_WIDTH + KV_WIDTH,
                SSM_WIDTH + Q_WIDTH + 2 * KV_WIDTH, SSM_WIDTH + Q_WIDTH + 2 * KV_WIDTH + D_MODEL]

    for i in range(DEPTH):
        h = rms_norm(h_res, norm_mix_g[i])
        proj = h @ w_in[i]
        u, q, k, v, g_ssm, g_attn = jnp.split(proj, split_at, axis=-1)

        y = s5_bidirectional(u, ssm_a_re[i], ssm_a_im[i], ssm_log_dt[i], ssm_b_re[i], ssm_b_im[i],
                             ssm_c_re[i], ssm_c_im[i], ssm_d[i])
        z = jax.nn.gelu(y, approximate=False)
        y_ssm = z * jax.nn.sigmoid(z @ w_glu[i].astype(jnp.float32) + b_glu[i].astype(jnp.float32))

        q = rms_norm(q.reshape(bsz, l, N_HEADS, HEAD_DIM), q_norm_g[i])
        k = rms_norm(k.reshape(bsz, l, N_KV_HEADS, HEAD_DIM), k_norm_g[i])
        q = apply_rope(q, cos, sin)
        k = apply_rope(k, cos, sin)
        v = v.reshape(bsz, l, N_KV_HEADS, HEAD_DIM).astype(jnp.float32)
        y_attn = gqa_attention(q, k, v)

        merged = (jax.nn.sigmoid(g_ssm.astype(jnp.float32)) * (y_ssm.astype(dtype) @ w_ssm_proj[i])
                  + jax.nn.sigmoid(g_attn.astype(jnp.float32)) * (y_attn.astype(dtype) @ w_attn_proj[i]))
        h_res = h_res + (merged.astype(dtype) @ w_out[i]).astype(dtype)

        h2 = rms_norm(h_res, norm_mlp_g[i])
        h_res = h_res + (jnp.square(jax.nn.relu(h2 @ w_mlp_in[i])) @ w_mlp_out[i]).astype(dtype)

    out = rms_norm(h_res, norm_final_g)
    return out[:, N_META:]


import jax as _jax
import jax.numpy as _jnp

TWIN_FORMAT = 'train_step'
FWD_PARAMS = ['x', 'meta_tokens', 'norm_mix_g', 'w_in', 'ssm_a_re', 'ssm_a_im', 'ssm_log_dt', 'ssm_b_re', 'ssm_b_im', 'ssm_c_re', 'ssm_c_im', 'ssm_d', 'w_glu', 'b_glu', 'q_norm_g', 'k_norm_g', 'w_ssm_proj', 'w_attn_proj', 'w_out', 'norm_mlp_g', 'w_mlp_in', 'w_mlp_out', 'norm_final_g']
TWIN_WEIGHTS = ['meta_tokens', 'norm_mix_g', 'w_in', 'ssm_a_re', 'ssm_a_im', 'ssm_log_dt', 'ssm_b_re', 'ssm_b_im', 'ssm_c_re', 'ssm_c_im', 'ssm_d', 'w_glu', 'b_glu', 'q_norm_g', 'k_norm_g', 'w_ssm_proj', 'w_attn_proj', 'w_out', 'norm_mlp_g', 'w_mlp_in', 'w_mlp_out', 'norm_final_g']
TWIN_DIFF_INPUT = 'x'
TWIN_INPUTS = ['x', 'meta_tokens', 'norm_mix_g', 'w_in', 'ssm_a_re', 'ssm_a_im', 'ssm_log_dt', 'ssm_b_re', 'ssm_b_im', 'ssm_c_re', 'ssm_c_im', 'ssm_d', 'w_glu', 'b_glu', 'q_norm_g', 'k_norm_g', 'w_ssm_proj', 'w_attn_proj', 'w_out', 'norm_mlp_g', 'w_mlp_in', 'w_mlp_out', 'norm_final_g', 'loss_target', 'm_meta_tokens', 'm_norm_mix_g', 'm_w_in', 'm_ssm_a_re', 'm_ssm_a_im', 'm_ssm_log_dt', 'm_ssm_b_re', 'm_ssm_b_im', 'm_ssm_c_re', 'm_ssm_c_im', 'm_ssm_d', 'm_w_glu', 'm_b_glu', 'm_q_norm_g', 'm_k_norm_g', 'm_w_ssm_proj', 'm_w_attn_proj', 'm_w_out', 'm_norm_mlp_g', 'm_w_mlp_in', 'm_w_mlp_out', 'm_norm_final_g', 'v_meta_tokens', 'v_norm_mix_g', 'v_w_in', 'v_ssm_a_re', 'v_ssm_a_im', 'v_ssm_log_dt', 'v_ssm_b_re', 'v_ssm_b_im', 'v_ssm_c_re', 'v_ssm_c_im', 'v_ssm_d', 'v_w_glu', 'v_b_glu', 'v_q_norm_g', 'v_k_norm_g', 'v_w_ssm_proj', 'v_w_attn_proj', 'v_w_out', 'v_norm_mlp_g', 'v_w_mlp_in', 'v_w_mlp_out', 'v_norm_final_g']
TWIN_OUTPUTS = ['loss', 'grad_x', 'grad_meta_tokens', 'grad_norm_mix_g', 'grad_w_in', 'grad_ssm_a_re', 'grad_ssm_a_im', 'grad_ssm_log_dt', 'grad_ssm_b_re', 'grad_ssm_b_im', 'grad_ssm_c_re', 'grad_ssm_c_im', 'grad_ssm_d', 'grad_w_glu', 'grad_b_glu', 'grad_q_norm_g', 'grad_k_norm_g', 'grad_w_ssm_proj', 'grad_w_attn_proj', 'grad_w_out', 'grad_norm_mlp_g', 'grad_w_mlp_in', 'grad_w_mlp_out', 'grad_norm_final_g', 'delta_meta_tokens', 'delta_norm_mix_g', 'delta_w_in', 'delta_ssm_a_re', 'delta_ssm_a_im', 'delta_ssm_log_dt', 'delta_ssm_b_re', 'delta_ssm_b_im', 'delta_ssm_c_re', 'delta_ssm_c_im', 'delta_ssm_d', 'delta_w_glu', 'delta_b_glu', 'delta_q_norm_g', 'delta_k_norm_g', 'delta_w_ssm_proj', 'delta_w_attn_proj', 'delta_w_out', 'delta_norm_mlp_g', 'delta_w_mlp_in', 'delta_w_mlp_out', 'delta_norm_final_g', 'new_m_meta_tokens', 'new_m_norm_mix_g', 'new_m_w_in', 'new_m_ssm_a_re', 'new_m_ssm_a_im', 'new_m_ssm_log_dt', 'new_m_ssm_b_re', 'new_m_ssm_b_im', 'new_m_ssm_c_re', 'new_m_ssm_c_im', 'new_m_ssm_d', 'new_m_w_glu', 'new_m_b_glu', 'new_m_q_norm_g', 'new_m_k_norm_g', 'new_m_w_ssm_proj', 'new_m_w_attn_proj', 'new_m_w_out', 'new_m_norm_mlp_g', 'new_m_w_mlp_in', 'new_m_w_mlp_out', 'new_m_norm_final_g', 'new_v_meta_tokens', 'new_v_norm_mix_g', 'new_v_w_in', 'new_v_ssm_a_re', 'new_v_ssm_a_im', 'new_v_ssm_log_dt', 'new_v_ssm_b_re', 'new_v_ssm_b_im', 'new_v_ssm_c_re', 'new_v_ssm_c_im', 'new_v_ssm_d', 'new_v_w_glu', 'new_v_b_glu', 'new_v_q_norm_g', 'new_v_k_norm_g', 'new_v_w_ssm_proj', 'new_v_w_attn_proj', 'new_v_w_out', 'new_v_norm_mlp_g', 'new_v_w_mlp_in', 'new_v_w_mlp_out', 'new_v_norm_final_g']
TWIN_LEAF_KINDS = {'loss': 'loss', 'grad_x': 'grad_x', 'grad_meta_tokens': 'grad_w', 'grad_norm_mix_g': 'grad_w', 'grad_w_in': 'grad_w', 'grad_ssm_a_re': 'grad_w', 'grad_ssm_a_im': 'grad_w', 'grad_ssm_log_dt': 'grad_w', 'grad_ssm_b_re': 'grad_w', 'grad_ssm_b_im': 'grad_w', 'grad_ssm_c_re': 'grad_w', 'grad_ssm_c_im': 'grad_w', 'grad_ssm_d': 'grad_w', 'grad_w_glu': 'grad_w', 'grad_b_glu': 'grad_w', 'grad_q_norm_g': 'grad_w', 'grad_k_norm_g': 'grad_w', 'grad_w_ssm_proj': 'grad_w', 'grad_w_attn_proj': 'grad_w', 'grad_w_out': 'grad_w', 'grad_norm_mlp_g': 'grad_w', 'grad_w_mlp_in': 'grad_w', 'grad_w_mlp_out': 'grad_w', 'grad_norm_final_g': 'grad_w', 'delta_meta_tokens': 'delta_w', 'delta_norm_mix_g': 'delta_w', 'delta_w_in': 'delta_w', 'delta_ssm_a_re': 'delta_w', 'delta_ssm_a_im': 'delta_w', 'delta_ssm_log_dt': 'delta_w', 'delta_ssm_b_re': 'delta_w', 'delta_ssm_b_im': 'delta_w', 'delta_ssm_c_re': 'delta_w', 'delta_ssm_c_im': 'delta_w', 'delta_ssm_d': 'delta_w', 'delta_w_glu': 'delta_w', 'delta_b_glu': 'delta_w', 'delta_q_norm_g': 'delta_w', 'delta_k_norm_g': 'delta_w', 'delta_w_ssm_proj': 'delta_w', 'delta_w_attn_proj': 'delta_w', 'delta_w_out': 'delta_w', 'delta_norm_mlp_g': 'delta_w', 'delta_w_mlp_in': 'delta_w', 'delta_w_mlp_out': 'delta_w', 'delta_norm_final_g': 'delta_w', 'new_m_meta_tokens': 'new_m', 'new_m_norm_mix_g': 'new_m', 'new_m_w_in': 'new_m', 'new_m_ssm_a_re': 'new_m', 'new_m_ssm_a_im': 'new_m', 'new_m_ssm_log_dt': 'new_m', 'new_m_ssm_b_re': 'new_m', 'new_m_ssm_b_im': 'new_m', 'new_m_ssm_c_re': 'new_m', 'new_m_ssm_c_im': 'new_m', 'new_m_ssm_d': 'new_m', 'new_m_w_glu': 'new_m', 'new_m_b_glu': 'new_m', 'new_m_q_norm_g': 'new_m', 'new_m_k_norm_g': 'new_m', 'new_m_w_ssm_proj': 'new_m', 'new_m_w_attn_proj': 'new_m', 'new_m_w_out': 'new_m', 'new_m_norm_mlp_g': 'new_m', 'new_m_w_mlp_in': 'new_m', 'new_m_w_mlp_out': 'new_m', 'new_m_norm_final_g': 'new_m', 'new_v_meta_tokens': 'new_v', 'new_v_norm_mix_g': 'new_v', 'new_v_w_in': 'new_v', 'new_v_ssm_a_re': 'new_v', 'new_v_ssm_a_im': 'new_v', 'new_v_ssm_log_dt': 'new_v', 'new_v_ssm_b_re': 'new_v', 'new_v_ssm_b_im': 'new_v', 'new_v_ssm_c_re': 'new_v', 'new_v_ssm_c_im': 'new_v', 'new_v_ssm_d': 'new_v', 'new_v_w_glu': 'new_v', 'new_v_b_glu': 'new_v', 'new_v_q_norm_g': 'new_v', 'new_v_k_norm_g': 'new_v', 'new_v_w_ssm_proj': 'new_v', 'new_v_w_attn_proj': 'new_v', 'new_v_w_out': 'new_v', 'new_v_norm_mlp_g': 'new_v', 'new_v_w_mlp_in': 'new_v', 'new_v_w_mlp_out': 'new_v', 'new_v_norm_final_g': 'new_v'}


def _forward(args):
    return _fwd_reference(*[args[k] for k in FWD_PARAMS])


def _output_shape():
    out = _jax.eval_shape(lambda: _forward(_fwd_setup_inputs(0)))
    return out.shape, out.dtype

N_MICROBATCH = 1
ADAM_LR = 0.001
ADAM_B1 = 0.9
ADAM_B2 = 0.999
ADAM_EPS = 1e-08
ADAM_WD = 0.01
ADAM_STEP = 10
PER_EXAMPLE_BATCH_AXIS = {'x': 0, 'loss_target': 0}
SHARED_INPUTS = []
_WEIGHT_DTYPES = {'meta_tokens': _jnp.float32, 'norm_mix_g': _jnp.float32, 'w_in': _jnp.float32, 'ssm_a_re': _jnp.float32, 'ssm_a_im': _jnp.float32, 'ssm_log_dt': _jnp.float32, 'ssm_b_re': _jnp.float32, 'ssm_b_im': _jnp.float32, 'ssm_c_re': _jnp.float32, 'ssm_c_im': _jnp.float32, 'ssm_d': _jnp.float32, 'w_glu': _jnp.float32, 'b_glu': _jnp.float32, 'q_norm_g': _jnp.float32, 'k_norm_g': _jnp.float32, 'w_ssm_proj': _jnp.float32, 'w_attn_proj': _jnp.float32, 'w_out': _jnp.float32, 'norm_mlp_g': _jnp.float32, 'w_mlp_in': _jnp.float32, 'w_mlp_out': _jnp.float32, 'norm_final_g': _jnp.float32}
MOMENT_SCALE = {'meta_tokens': 1.206159e-03, 'norm_mix_g': 6.726748e-02, 'w_in': 3.233414e-02, 'ssm_a_re': 6.608966e-03, 'ssm_a_im': 6.769656e-03, 'ssm_log_dt': 4.634228e+00, 'ssm_b_re': 3.717692e-03, 'ssm_b_im': 3.698300e-03, 'ssm_c_re': 5.242598e-03, 'ssm_c_im': 5.354800e-03, 'ssm_d': 9.618468e-02, 'w_glu': 2.426309e-02, 'b_glu': 3.934123e-02, 'q_norm_g': 5.379917e-02, 'k_norm_g': 5.187453e-02, 'w_ssm_proj': 5.846999e-02, 'w_attn_proj': 1.480516e-02, 'w_out': 5.428878e-02, 'norm_mlp_g': 2.312170e-01, 'w_mlp_in': 1.112597e-01, 'w_mlp_out': 2.193305e-01, 'norm_final_g': 6.465106e+01}


def _to_microbatches(a, axis):
    t = _jnp.moveaxis(a, axis, 0)
    t = t.reshape((N_MICROBATCH, t.shape[0] // N_MICROBATCH) + t.shape[1:])
    return _jnp.moveaxis(t, 1, axis + 1)


def setup_inputs(seed: int = 0) -> dict:
    inp = _fwd_setup_inputs(seed)
    key = _jax.random.fold_in(_jax.random.key(seed), 7919)
    shape, _ = _output_shape()
    out = dict(inp)
    out["loss_target"] = _jax.random.normal(_jax.random.fold_in(key, 0), shape, _jnp.float32)
    for i, name in enumerate(TWIN_WEIGHTS):
        w = inp[name].astype(_jnp.float32)
        if MOMENT_SCALE is None:
            s = _jnp.sqrt(_jnp.mean(_jnp.square(w)) + 1e-30)
        else:
            s = MOMENT_SCALE[name]
        km, kv = _jax.random.split(_jax.random.fold_in(key, i + 1))
        out[name] = w
        out["m_" + name] = s * _jax.random.normal(km, w.shape, _jnp.float32)
        out["v_" + name] = (s * s) * _jax.random.uniform(kv, w.shape, _jnp.float32, 0.5, 1.5)
    if N_MICROBATCH > 1:
        for name, axis in PER_EXAMPLE_BATCH_AXIS.items():
            out[name] = _to_microbatches(out[name], axis)
    return {'x': out['x'], 'meta_tokens': out['meta_tokens'], 'norm_mix_g': out['norm_mix_g'], 'w_in': out['w_in'], 'ssm_a_re': out['ssm_a_re'], 'ssm_a_im': out['ssm_a_im'], 'ssm_log_dt': out['ssm_log_dt'], 'ssm_b_re': out['ssm_b_re'], 'ssm_b_im': out['ssm_b_im'], 'ssm_c_re': out['ssm_c_re'], 'ssm_c_im': out['ssm_c_im'], 'ssm_d': out['ssm_d'], 'w_glu': out['w_glu'], 'b_glu': out['b_glu'], 'q_norm_g': out['q_norm_g'], 'k_norm_g': out['k_norm_g'], 'w_ssm_proj': out['w_ssm_proj'], 'w_attn_proj': out['w_attn_proj'], 'w_out': out['w_out'], 'norm_mlp_g': out['norm_mlp_g'], 'w_mlp_in': out['w_mlp_in'], 'w_mlp_out': out['w_mlp_out'], 'norm_final_g': out['norm_final_g'], 'loss_target': out['loss_target'], 'm_meta_tokens': out['m_meta_tokens'], 'm_norm_mix_g': out['m_norm_mix_g'], 'm_w_in': out['m_w_in'], 'm_ssm_a_re': out['m_ssm_a_re'], 'm_ssm_a_im': out['m_ssm_a_im'], 'm_ssm_log_dt': out['m_ssm_log_dt'], 'm_ssm_b_re': out['m_ssm_b_re'], 'm_ssm_b_im': out['m_ssm_b_im'], 'm_ssm_c_re': out['m_ssm_c_re'], 'm_ssm_c_im': out['m_ssm_c_im'], 'm_ssm_d': out['m_ssm_d'], 'm_w_glu': out['m_w_glu'], 'm_b_glu': out['m_b_glu'], 'm_q_norm_g': out['m_q_norm_g'], 'm_k_norm_g': out['m_k_norm_g'], 'm_w_ssm_proj': out['m_w_ssm_proj'], 'm_w_attn_proj': out['m_w_attn_proj'], 'm_w_out': out['m_w_out'], 'm_norm_mlp_g': out['m_norm_mlp_g'], 'm_w_mlp_in': out['m_w_mlp_in'], 'm_w_mlp_out': out['m_w_mlp_out'], 'm_norm_final_g': out['m_norm_final_g'], 'v_meta_tokens': out['v_meta_tokens'], 'v_norm_mix_g': out['v_norm_mix_g'], 'v_w_in': out['v_w_in'], 'v_ssm_a_re': out['v_ssm_a_re'], 'v_ssm_a_im': out['v_ssm_a_im'], 'v_ssm_log_dt': out['v_ssm_log_dt'], 'v_ssm_b_re': out['v_ssm_b_re'], 'v_ssm_b_im': out['v_ssm_b_im'], 'v_ssm_c_re': out['v_ssm_c_re'], 'v_ssm_c_im': out['v_ssm_c_im'], 'v_ssm_d': out['v_ssm_d'], 'v_w_glu': out['v_w_glu'], 'v_b_glu': out['v_b_glu'], 'v_q_norm_g': out['v_q_norm_g'], 'v_k_norm_g': out['v_k_norm_g'], 'v_w_ssm_proj': out['v_w_ssm_proj'], 'v_w_attn_proj': out['v_w_attn_proj'], 'v_w_out': out['v_w_out'], 'v_norm_mlp_g': out['v_norm_mlp_g'], 'v_w_mlp_in': out['v_w_mlp_in'], 'v_w_mlp_out': out['v_w_mlp_out'], 'v_norm_final_g': out['v_norm_final_g']}


def _loss(weights, diff, rest, loss_target):
    with _jax.named_scope("forward"):
        args = {**rest, TWIN_DIFF_INPUT: diff, **{k: w.astype(_WEIGHT_DTYPES[k]) for k, w in weights.items()}}
        y = _forward(args)
    with _jax.named_scope("loss_head"):
        err = _jnp.square(y.astype(_jnp.float32) - loss_target)
        return 0.5 * _jnp.sum(_jnp.mean(err, axis=-1)) if err.ndim else 0.5 * err


def _adamw(w, g, m, v):
    m = ADAM_B1 * m + (1.0 - ADAM_B1) * g
    v = ADAM_B2 * v + (1.0 - ADAM_B2) * _jnp.square(g)
    m_hat = m / (1.0 - ADAM_B1 ** ADAM_STEP)
    v_hat = v / (1.0 - ADAM_B2 ** ADAM_STEP)
    delta = -ADAM_LR * (m_hat / (_jnp.sqrt(v_hat) + ADAM_EPS) + ADAM_WD * w)
    return delta, m, v


def reference(x, meta_tokens, norm_mix_g, w_in, ssm_a_re, ssm_a_im, ssm_log_dt, ssm_b_re, ssm_b_im, ssm_c_re, ssm_c_im, ssm_d, w_glu, b_glu, q_norm_g, k_norm_g, w_ssm_proj, w_attn_proj, w_out, norm_mlp_g, w_mlp_in, w_mlp_out, norm_final_g, loss_target, m_meta_tokens, m_norm_mix_g, m_w_in, m_ssm_a_re, m_ssm_a_im, m_ssm_log_dt, m_ssm_b_re, m_ssm_b_im, m_ssm_c_re, m_ssm_c_im, m_ssm_d, m_w_glu, m_b_glu, m_q_norm_g, m_k_norm_g, m_w_ssm_proj, m_w_attn_proj, m_w_out, m_norm_mlp_g, m_w_mlp_in, m_w_mlp_out, m_norm_final_g, v_meta_tokens, v_norm_mix_g, v_w_in, v_ssm_a_re, v_ssm_a_im, v_ssm_log_dt, v_ssm_b_re, v_ssm_b_im, v_ssm_c_re, v_ssm_c_im, v_ssm_d, v_w_glu, v_b_glu, v_q_norm_g, v_k_norm_g, v_w_ssm_proj, v_w_attn_proj, v_w_out, v_norm_mlp_g, v_w_mlp_in, v_w_mlp_out, v_norm_final_g):
    given = dict(x=x, meta_tokens=meta_tokens, norm_mix_g=norm_mix_g, w_in=w_in, ssm_a_re=ssm_a_re, ssm_a_im=ssm_a_im, ssm_log_dt=ssm_log_dt, ssm_b_re=ssm_b_re, ssm_b_im=ssm_b_im, ssm_c_re=ssm_c_re, ssm_c_im=ssm_c_im, ssm_d=ssm_d, w_glu=w_glu, b_glu=b_glu, q_norm_g=q_norm_g, k_norm_g=k_norm_g, w_ssm_proj=w_ssm_proj, w_attn_proj=w_attn_proj, w_out=w_out, norm_mlp_g=norm_mlp_g, w_mlp_in=w_mlp_in, w_mlp_out=w_mlp_out, norm_final_g=norm_final_g, loss_target=loss_target, m_meta_tokens=m_meta_tokens, m_norm_mix_g=m_norm_mix_g, m_w_in=m_w_in, m_ssm_a_re=m_ssm_a_re, m_ssm_a_im=m_ssm_a_im, m_ssm_log_dt=m_ssm_log_dt, m_ssm_b_re=m_ssm_b_re, m_ssm_b_im=m_ssm_b_im, m_ssm_c_re=m_ssm_c_re, m_ssm_c_im=m_ssm_c_im, m_ssm_d=m_ssm_d, m_w_glu=m_w_glu, m_b_glu=m_b_glu, m_q_norm_g=m_q_norm_g, m_k_norm_g=m_k_norm_g, m_w_ssm_proj=m_w_ssm_proj, m_w_attn_proj=m_w_attn_proj, m_w_out=m_w_out, m_norm_mlp_g=m_norm_mlp_g, m_w_mlp_in=m_w_mlp_in, m_w_mlp_out=m_w_mlp_out, m_norm_final_g=m_norm_final_g, v_meta_tokens=v_meta_tokens, v_norm_mix_g=v_norm_mix_g, v_w_in=v_w_in, v_ssm_a_re=v_ssm_a_re, v_ssm_a_im=v_ssm_a_im, v_ssm_log_dt=v_ssm_log_dt, v_ssm_b_re=v_ssm_b_re, v_ssm_b_im=v_ssm_b_im, v_ssm_c_re=v_ssm_c_re, v_ssm_c_im=v_ssm_c_im, v_ssm_d=v_ssm_d, v_w_glu=v_w_glu, v_b_glu=v_b_glu, v_q_norm_g=v_q_norm_g, v_k_norm_g=v_k_norm_g, v_w_ssm_proj=v_w_ssm_proj, v_w_attn_proj=v_w_attn_proj, v_w_out=v_w_out, v_norm_mlp_g=v_norm_mlp_g, v_w_mlp_in=v_w_mlp_in, v_w_mlp_out=v_w_mlp_out, v_norm_final_g=v_norm_final_g)
    weights = {n: given[n] for n in TWIN_WEIGHTS}
    shared = {n: given[n] for n in SHARED_INPUTS}
    per_example = {n: given[n] for n in ['x']}
    grad_fn = _jax.value_and_grad(_loss, argnums=(0, 1))

    def one_microbatch(ex, loss_target):
        ex = dict(ex)
        diff = ex.pop(TWIN_DIFF_INPUT)
        return grad_fn(weights, diff, {**shared, **ex}, loss_target)

    if N_MICROBATCH == 1:
        loss, (grad_w, grad_x) = one_microbatch(per_example, given["loss_target"])
    else:
        def body(carry, xs):
            loss_sum, grad_sum = carry
            l_k, (gw_k, gx_k) = one_microbatch(xs[0], xs[1])
            with _jax.named_scope("update"):
                return (loss_sum + l_k, _jax.tree.map(_jnp.add, grad_sum, gw_k)), gx_k

        init = (_jnp.zeros((), _jnp.float32), _jax.tree.map(_jnp.zeros_like, weights))
        (loss, grad_w), grad_x = _jax.lax.scan(body, init, (per_example, given["loss_target"]))
    with _jax.named_scope("update"):
        delta_w, new_m, new_v = {}, {}, {}
        for n in TWIN_WEIGHTS:
            delta_w[n], new_m[n], new_v[n] = _adamw(weights[n], grad_w[n], given["m_" + n], given["v_" + n])
    return (loss, grad_x, *[grad_w[n] for n in TWIN_WEIGHTS], *[delta_w[n] for n in TWIN_WEIGHTS],
            *[new_m[n] for n in TWIN_WEIGHTS], *[new_v[n] for n in TWIN_WEIGHTS])
```

```python
import functools
import math

import jax
import jax.numpy as jnp
from jax import lax
from jax.experimental import pallas as pl
from jax.experimental.pallas import tpu as pltpu

F32 = jnp.float32
BF16 = jnp.bfloat16

N_DEV = 8
N_META = 16
GRID_W = 64
SSM_GROUP = 16
SSM_STATE = 64
HEAD_DIM = 64
KV_REP = 4
ROPE_THETA = 10000.0
NORM_EPS = 1e-6
EIG_RE_MAX = -1e-4
ATTN_SCALE = HEAD_DIM ** -0.5

ADAM_LR = 0.001
ADAM_B1 = 0.9
ADAM_B2 = 0.999
ADAM_EPS = 1e-08
ADAM_WD = 0.01
ADAM_STEP = 10

ROW_TILE = 384
ROW_TILE_BWD = 192
SCAN_LANES = 1024
Q_TILE = 384
KV_TILE = 768
PACK_W = 1024
V7X_VMEM_LIMIT = 56 * 1024 * 1024
NEG_BIG = -1e30

BIG_WEIGHTS = ("meta_tokens", "w_in", "w_glu", "w_ssm_proj", "w_attn_proj", "w_out", "w_mlp_in", "w_mlp_out")
BIG_SHARD_AXIS = {"meta_tokens": 1, "w_in": 1, "w_glu": 0, "w_ssm_proj": 1, "w_attn_proj": 0, "w_out": 0,
                  "w_mlp_in": 1, "w_mlp_out": 0}
SMALL_WEIGHTS = ("norm_mix_g", "ssm_a_re", "ssm_a_im", "ssm_log_dt", "ssm_b_re", "ssm_b_im", "ssm_c_re",
                 "ssm_c_im", "ssm_d", "b_glu", "q_norm_g", "k_norm_g", "norm_mlp_g", "norm_final_g")
ALL_WEIGHTS = ("meta_tokens", "norm_mix_g", "w_in", "ssm_a_re", "ssm_a_im", "ssm_log_dt", "ssm_b_re", "ssm_b_im",
               "ssm_c_re", "ssm_c_im", "ssm_d", "w_glu", "b_glu", "q_norm_g", "k_norm_g", "w_ssm_proj",
               "w_attn_proj", "w_out", "norm_mlp_g", "w_mlp_in", "w_mlp_out", "norm_final_g")


def _round_up(n, m):
    return (n + m - 1) // m * m


def _pcall(body, *, name, grid, in_specs, out_specs, out_shape, scratch=(), vmem=None, **kw):
    params = pltpu.CompilerParams(dimension_semantics=("arbitrary",) * len(grid), vmem_limit_bytes=vmem)
    return pl.pallas_call(body, name=name, grid=grid, in_specs=in_specs, out_specs=out_specs, out_shape=out_shape,
                          scratch_shapes=list(scratch), compiler_params=params, **kw)


def _dot(a, b):
    return jnp.dot(a, b, preferred_element_type=F32)


def _dot_nt(a, b):
    return lax.dot_general(a, b, (((1,), (1,)), ((), ())), preferred_element_type=F32)


def _dot_tn(a, b):
    return lax.dot_general(a, b, (((0,), (0,)), ((), ())), preferred_element_type=F32)


def _full_spec(shape):
    nd = len(shape)
    return pl.BlockSpec(shape, lambda *_: (0,) * nd)


def _row_spec(tm, width):
    return pl.BlockSpec((tm, width), lambda i: (i, 0))


def _heads_spec(nh, tm):
    return pl.BlockSpec((nh, tm, HEAD_DIM), lambda i: (0, i, 0))


_ANY = pl.BlockSpec(memory_space=pl.ANY)


def _load_once(step, pairs, sem):
    @pl.when(step == 0)
    def _():
        copies = [pltpu.make_async_copy(src, dst, sem.at[k]) for k, (src, dst) in enumerate(pairs)]
        for cp in copies:
            cp.start()
        for cp in copies:
            cp.wait()


def _swap_pairs(x, even):
    n = x.shape[-1]
    return jnp.where(even, pltpu.roll(x, n - 1, 1), pltpu.roll(x, 1, 1))


def _gelu(y):
    return 0.5 * y * (1.0 + lax.erf(y * (1.0 / math.sqrt(2.0))))


def _gelu_grad(y):
    return 0.5 * (1.0 + lax.erf(y * (1.0 / math.sqrt(2.0)))) + y * jnp.exp(-0.5 * y * y) * (1.0 / math.sqrt(2.0 * math.pi))


def _in_proj_fwd(x0, g_mix, w_in, qg, kg, cos, sin):
    t, d = x0.shape
    tm = ROW_TILE
    du, dk = d // 2, d // 4
    nh, nkv = d // HEAD_DIM, d // HEAD_DIM // KV_REP
    o_q, o_k, o_v, o_g = du, du + d, du + d + dk, 2 * d

    def body(x_ref, g_ref, w_hbm, qg_ref, kg_ref, c_ref, s_ref,
             h_ref, u_ref, ub_ref, qraw_ref, kraw_ref, qr_ref, kr_ref, v_ref, gates_ref,
             w_ref, pair_ref, sem):
        _load_once(pl.program_id(0), [(w_hbm, w_ref)], sem)
        x = x_ref[...]
        r = lax.rsqrt(jnp.mean(x * x, axis=-1, keepdims=True) + NORM_EPS)
        h = ((x * r) * g_ref[...]).astype(BF16)
        h_ref[...] = h
        u = _dot(h, w_ref[:, 0:du])
        u_ref[...] = u
        ub_ref[...] = u.astype(BF16)
        lane = lax.broadcasted_iota(jnp.int32, (tm, 128), 1)
        lo = lane < HEAD_DIM
        even = (lane & 1) == 0
        c = c_ref[...]
        s = s_ref[...]

        def norm_rope(blk, g128):
            sq = blk * blk
            ms_lo = jnp.sum(jnp.where(lo, sq, 0.0), axis=-1, keepdims=True) * (1.0 / HEAD_DIM)
            ms_hi = jnp.sum(jnp.where(lo, 0.0, sq), axis=-1, keepdims=True) * (1.0 / HEAD_DIM)
            rr = jnp.where(lo, lax.rsqrt(ms_lo + NORM_EPS), lax.rsqrt(ms_hi + NORM_EPS))
            qn = (blk * rr) * g128
            return qn * c + _swap_pairs(qn, even) * s

        qfull = _dot(h, w_ref[:, o_q:o_k])
        qraw_ref[...] = qfull
        for a in range(nh // 2):
            pair_ref[...] = norm_rope(qfull[:, 128 * a:128 * (a + 1)], qg_ref[...]) * ATTN_SCALE
            qr_ref[2 * a] = pair_ref[:, 0:HEAD_DIM].astype(BF16)
            qr_ref[2 * a + 1] = pair_ref[:, HEAD_DIM:128].astype(BF16)
        kv = _dot(h, w_ref[:, o_k:o_g])
        kraw_ref[...] = kv[:, 0:dk]
        for a in range(nkv // 2):
            pair_ref[...] = norm_rope(kv[:, 128 * a:128 * (a + 1)], kg_ref[...])
            kr_ref[2 * a] = pair_ref[:, 0:HEAD_DIM].astype(BF16)
            kr_ref[2 * a + 1] = pair_ref[:, HEAD_DIM:128].astype(BF16)
        for a in range(nkv // 2):
            pair_ref[...] = kv[:, dk + 128 * a:dk + 128 * (a + 1)]
            v_ref[2 * a] = pair_ref[:, 0:HEAD_DIM].astype(BF16)
            v_ref[2 * a + 1] = pair_ref[:, HEAD_DIM:128].astype(BF16)
        gates_ref[...] = _dot(h, w_ref[:, o_g:4 * d])

    return _pcall(
        body, name="in_proj_fwd", grid=(t // tm,),
        in_specs=[_row_spec(tm, d), _full_spec((1, d)), _ANY, _full_spec((1, 128)), _full_spec((1, 128)),
                  _row_spec(tm, 128), _row_spec(tm, 128)],
        out_specs=[_row_spec(tm, d), _row_spec(tm, du), _row_spec(tm, du), _row_spec(tm, d), _row_spec(tm, dk),
                   _heads_spec(nh, tm), _heads_spec(nkv, tm), _heads_spec(nkv, tm), _row_spec(tm, 2 * d)],
        out_shape=[jax.ShapeDtypeStruct((t, d), BF16), jax.ShapeDtypeStruct((t, du), F32),
                   jax.ShapeDtypeStruct((t, du), BF16), jax.ShapeDtypeStruct((t, d), F32),
                   jax.ShapeDtypeStruct((t, dk), F32), jax.ShapeDtypeStruct((nh, t, HEAD_DIM), BF16),
                   jax.ShapeDtypeStruct((nkv, t, HEAD_DIM), BF16), jax.ShapeDtypeStruct((nkv, t, HEAD_DIM), BF16),
                   jax.ShapeDtypeStruct((t, 2 * d), F32)],
        scratch=[pltpu.VMEM((d, 4 * d), BF16), pltpu.VMEM((tm, 128), F32), pltpu.SemaphoreType.DMA((1,))],
        vmem=V7X_VMEM_LIMIT,
    )(x0, g_mix, w_in, qg, kg, cos, sin)


def _mixer_values(u, yf, yb, y_attn, gates, d_ref, wg_ref, bg_ref, ps_ref, pa_ref, d):
    y = (u * d_ref[...] + yf) + yb
    z = _gelu(y)
    sg = jax.nn.sigmoid(_dot(z.astype(BF16), wg_ref[...]) + bg_ref[...])
    y_ssm = z * sg
    a_ssm = _dot(y_ssm.astype(BF16), ps_ref[...])
    a_attn = _dot(y_attn.astype(BF16), pa_ref[...])
    s_ssm = jax.nn.sigmoid(gates[:, 0:d])
    s_attn = jax.nn.sigmoid(gates[:, d:2 * d])
    merged = s_ssm * a_ssm + s_attn * a_attn
    return y, z, sg, y_ssm, a_ssm, a_attn, s_ssm, s_attn, merged


def _mixer_out_fwd(x0, u, yf, yb, y_attn, gates, ssm_d, w_glu, b_glu, p_ssm, p_attn, w_out):
    t, d = x0.shape
    tm = ROW_TILE
    du = d // 2

    def body(x_ref, u_ref, yf_ref, yb_ref, ya_ref, gt_ref, d_ref, wg_ref, bg_ref, ps_ref, pa_ref, wo_ref, x1_ref):
        vals = _mixer_values(u_ref[...], yf_ref[...], yb_ref[...], ya_ref[...], gt_ref[...],
                             d_ref, wg_ref, bg_ref, ps_ref, pa_ref, d)
        merged = vals[-1]
        x1_ref[...] = x_ref[...] + _dot(merged.astype(BF16), wo_ref[...])

    return _pcall(
        body, name="mixer_out_fwd", grid=(t // tm,),
        in_specs=[_row_spec(tm, d), _row_spec(tm, du), _row_spec(tm, du), _row_spec(tm, du), _row_spec(tm, d),
                  _row_spec(tm, 2 * d), _full_spec((1, du)), _full_spec((du, du)), _full_spec((1, du)),
                  _full_spec((du, d)), _full_spec((d, d)), _full_spec((d, d))],
        out_specs=_row_spec(tm, d), out_shape=jax.ShapeDtypeStruct((t, d), F32), vmem=V7X_VMEM_LIMIT,
    )(x0, u, yf, yb, y_attn, gates, ssm_d, w_glu, b_glu, p_ssm, p_attn, w_out)


def _mlp_loss_fwd_bwd(x1, target, g_mlp, g_fin, w1, w2, n_valid):
    t, d = x1.shape
    tm = ROW_TILE_BWD
    dff = 4 * d
    fc = 512
    nfc = dff // fc

    def body(x_ref, tg_ref, gm_ref, gf_ref, w1_hbm, w2_hbm,
             dx1_ref, loss_ref, dgf_ref, dgm_ref, h2_ref, da_ref, hsq_ref, dx2b_ref,
             w1_ref, w2_ref, relu_ref, sem):
        i = pl.program_id(0)
        _load_once(i, [(w1_hbm, w1_ref), (w2_hbm, w2_ref)], sem)

        @pl.when(i == 0)
        def _():
            loss_ref[...] = jnp.zeros_like(loss_ref)
            dgf_ref[...] = jnp.zeros_like(dgf_ref)
            dgm_ref[...] = jnp.zeros_like(dgm_ref)

        x1v = x_ref[...]
        r1 = lax.rsqrt(jnp.mean(x1v * x1v, axis=-1, keepdims=True) + NORM_EPS)
        xh1 = x1v * r1
        h2b = (xh1 * gm_ref[...]).astype(BF16)
        h2_ref[...] = h2b
        acc = jnp.zeros((tm, d), F32)
        for c in range(nfc):
            a = jnp.maximum(_dot(h2b, w1_ref[:, fc * c:fc * (c + 1)]), 0.0)
            relu_ref[:, fc * c:fc * (c + 1)] = a
            hs = (a * a).astype(BF16)
            hsq_ref[:, fc * c:fc * (c + 1)] = hs
            acc = acc + _dot(hs, w2_ref[fc * c:fc * (c + 1), :])
        x2 = x1v + acc
        r2 = lax.rsqrt(jnp.mean(x2 * x2, axis=-1, keepdims=True) + NORM_EPS)
        xh2 = x2 * r2
        out = xh2 * gf_ref[...]
        row = i * tm + lax.broadcasted_iota(jnp.int32, (tm, 1), 0)
        valid = jnp.logical_and(row >= N_META, row < n_valid)
        diff = jnp.where(valid, out - tg_ref[...], 0.0)
        loss_ref[...] += 0.5 * jnp.sum(jnp.sum(diff * diff, axis=-1, keepdims=True) * (1.0 / d))
        dout = diff * (1.0 / d)
        dgf_ref[...] += jnp.sum(dout * xh2, axis=0, keepdims=True)
        dxh2 = dout * gf_ref[...]
        dx2 = r2 * (dxh2 - xh2 * jnp.mean(dxh2 * xh2, axis=-1, keepdims=True))
        dx2b = dx2.astype(BF16)
        dx2b_ref[...] = dx2b
        dh2 = jnp.zeros((tm, d), F32)
        for c in range(nfc):
            dhs = _dot_nt(dx2b, w2_ref[fc * c:fc * (c + 1), :])
            da = (dhs * (2.0 * relu_ref[:, fc * c:fc * (c + 1)])).astype(BF16)
            da_ref[:, fc * c:fc * (c + 1)] = da
            dh2 = dh2 + _dot_nt(da, w1_ref[:, fc * c:fc * (c + 1)])
        dgm_ref[...] += jnp.sum(dh2 * xh1, axis=0, keepdims=True)
        dxh1 = dh2 * gm_ref[...]
        dx1_ref[...] = dx2 + r1 * (dxh1 - xh1 * jnp.mean(dxh1 * xh1, axis=-1, keepdims=True))

    return _pcall(
        body, name="mlp_loss_fwd_bwd", grid=(t // tm,),
        in_specs=[_row_spec(tm, d), _row_spec(tm, d), _full_spec((1, d)), _full_spec((1, d)), _ANY, _ANY],
        out_specs=[_row_spec(tm, d), _full_spec((8, 128)), _full_spec((1, d)), _full_spec((1, d)),
                   _row_spec(tm, d), _row_spec(tm, dff), _row_spec(tm, dff), _row_spec(tm, d)],
        out_shape=[jax.ShapeDtypeStruct((t, d), F32), jax.ShapeDtypeStruct((8, 128), F32),
                   jax.ShapeDtypeStruct((1, d), F32), jax.ShapeDtypeStruct((1, d), F32),
                   jax.ShapeDtypeStruct((t, d), BF16), jax.ShapeDtypeStruct((t, dff), BF16),
                   jax.ShapeDtypeStruct((t, dff), BF16), jax.ShapeDtypeStruct((t, d), BF16)],
        scratch=[pltpu.VMEM((d, dff), BF16), pltpu.VMEM((dff, d), BF16), pltpu.VMEM((tm, dff), F32),
                 pltpu.SemaphoreType.DMA((2,))],
        vmem=V7X_VMEM_LIMIT,
    )(x1, target, g_mlp, g_fin, w1, w2)


def _mixer_out_bwd(dx1, u, yf, yb, y_attn, gates, ssm_d, w_glu, b_glu, p_ssm, p_attn, w_out):
    t, d = dx1.shape
    tm = ROW_TILE_BWD
    du = d // 2

    def body(dx_ref, u_ref, yf_ref, yb_ref, ya_ref, gt_ref, d_ref, wg_ref, bg_ref, ps_ref, pa_ref, wo_ref,
             dyb_ref, dud_ref, dya_ref, dgates_ref, zb_ref, dglb_ref, ysb_ref, dasb_ref, yab_ref, daab_ref,
             mgb_ref, dxb_ref, dd_ref, dbg_ref):
        i = pl.program_id(0)

        @pl.when(i == 0)
        def _():
            dd_ref[...] = jnp.zeros_like(dd_ref)
            dbg_ref[...] = jnp.zeros_like(dbg_ref)

        uv = u_ref[...]
        y_attn_v = ya_ref[...]
        y, z, sg, y_ssm, a_ssm, a_attn, s_ssm, s_attn, merged = _mixer_values(
            uv, yf_ref[...], yb_ref[...], y_attn_v, gt_ref[...], d_ref, wg_ref, bg_ref, ps_ref, pa_ref, d)
        dxb = dx_ref[...].astype(BF16)
        dxb_ref[...] = dxb
        mgb_ref[...] = merged.astype(BF16)
        dmerged = _dot_nt(dxb, wo_ref[...])
        dgates_ref[:, 0:d] = (dmerged * a_ssm * (s_ssm * (1.0 - s_ssm))).astype(BF16)
        dgates_ref[:, d:2 * d] = (dmerged * a_attn * (s_attn * (1.0 - s_attn))).astype(BF16)
        da_ssm = (dmerged * s_ssm).astype(BF16)
        da_attn = (dmerged * s_attn).astype(BF16)
        dasb_ref[...] = da_ssm
        daab_ref[...] = da_attn
        ysb_ref[...] = y_ssm.astype(BF16)
        yab_ref[...] = y_attn_v.astype(BF16)
        dy_ssm = _dot_nt(da_ssm, ps_ref[...])
        dya_ref[...] = _dot_nt(da_attn, pa_ref[...])
        dgl = dy_ssm * z * (sg * (1.0 - sg))
        dglb = dgl.astype(BF16)
        dglb_ref[...] = dglb
        zb_ref[...] = z.astype(BF16)
        dbg_ref[...] += jnp.sum(dgl, axis=0, keepdims=True)
        dz = dy_ssm * sg + _dot_nt(dglb, wg_ref[...])
        dy = dz * _gelu_grad(y)
        dyb_ref[...] = dy.astype(BF16)
        dd_ref[...] += jnp.sum(dy * uv, axis=0, keepdims=True)
        dud_ref[...] = dy * d_ref[...]

    bf = lambda w: jax.ShapeDtypeStruct((t, w), BF16)
    return _pcall(
        body, name="mixer_out_bwd", grid=(t // tm,),
        in_specs=[_row_spec(tm, d), _row_spec(tm, du), _row_spec(tm, du), _row_spec(tm, du), _row_spec(tm, d),
                  _row_spec(tm, 2 * d), _full_spec((1, du)), _full_spec((du, du)), _full_spec((1, du)),
                  _full_spec((du, d)), _full_spec((d, d)), _full_spec((d, d))],
        out_specs=[_row_spec(tm, du), _row_spec(tm, du), _row_spec(tm, d), _row_spec(tm, 2 * d),
                   _row_spec(tm, du), _row_spec(tm, du), _row_spec(tm, du), _row_spec(tm, d), _row_spec(tm, d),
                   _row_spec(tm, d), _row_spec(tm, d), _row_spec(tm, d), _full_spec((1, du)), _full_spec((1, du))],
        out_shape=[bf(du), jax.ShapeDtypeStruct((t, du), F32), jax.ShapeDtypeStruct((t, d), F32), bf(2 * d),
                   bf(du), bf(du), bf(du), bf(d), bf(d), bf(d), bf(d), bf(d),
                   jax.ShapeDtypeStruct((1, du), F32), jax.ShapeDtypeStruct((1, du), F32)],
        vmem=V7X_VMEM_LIMIT,
    )(dx1, u, yf, yb, y_attn, gates, ssm_d, w_glu, b_glu, p_ssm, p_attn, w_out)


def _in_proj_bwd(x0, dx1, dud, duf, dub, qraw, kraw, dq, dk, dv, dgates, g_mix, w_in, qg, kg, cos, sin):
    t, d = x0.shape
    tm = ROW_TILE_BWD
    du, dkw = d // 2, d // 4
    nh, nkv = d // HEAD_DIM, d // HEAD_DIM // KV_REP
    o_q, o_k, o_v, o_g = du, du + d, du + d + dkw, 2 * d

    def body(x_ref, dx1_ref, dud_ref, duf_ref, dub_ref, qraw_ref, kraw_ref, dq_ref, dk_ref, dv_ref, dgt_ref,
             g_ref, w_hbm, qg_ref, kg_ref, c_ref, s_ref,
             dx0_ref, dproj_ref, dgm_ref, dqg_ref, dkg_ref,
             w_ref, kv_ref, sem):
        i = pl.program_id(0)
        _load_once(i, [(w_hbm, w_ref)], sem)

        @pl.when(i == 0)
        def _():
            dgm_ref[...] = jnp.zeros_like(dgm_ref)
            dqg_ref[...] = jnp.zeros_like(dqg_ref)
            dkg_ref[...] = jnp.zeros_like(dkg_ref)

        lane = lax.broadcasted_iota(jnp.int32, (tm, 128), 1)
        lo = lane < HEAD_DIM
        even = (lane & 1) == 0
        c = c_ref[...]
        s = s_ref[...]

        def norm_rope_bwd(dout, raw, g128):
            sq = raw * raw
            ms_lo = jnp.sum(jnp.where(lo, sq, 0.0), axis=-1, keepdims=True) * (1.0 / HEAD_DIM)
            ms_hi = jnp.sum(jnp.where(lo, 0.0, sq), axis=-1, keepdims=True) * (1.0 / HEAD_DIM)
            rr = jnp.where(lo, lax.rsqrt(ms_lo + NORM_EPS), lax.rsqrt(ms_hi + NORM_EPS))
            xh = raw * rr
            dqn = dout * c + _swap_pairs(dout * s, even)
            dg = jnp.sum(dqn * xh, axis=0, keepdims=True)
            tt = dqn * g128
            pr = tt * xh
            mu_lo = jnp.sum(jnp.where(lo, pr, 0.0), axis=-1, keepdims=True) * (1.0 / HEAD_DIM)
            mu_hi = jnp.sum(jnp.where(lo, 0.0, pr), axis=-1, keepdims=True) * (1.0 / HEAD_DIM)
            return rr * (tt - xh * jnp.where(lo, mu_lo, mu_hi)), dg

        dub_tot = (dud_ref[...] + duf_ref[...]) + dub_ref[...]
        dproj_ref[:, 0:du] = dub_tot.astype(BF16)
        dqg = jnp.zeros((1, 128), F32)
        for a in range(nh // 2):
            sl = slice(128 * a, 128 * (a + 1))
            draw, dg = norm_rope_bwd(dq_ref[:, sl] * ATTN_SCALE, qraw_ref[:, sl], qg_ref[...])
            dqg = dqg + dg
            dproj_ref[:, o_q + 128 * a:o_q + 128 * (a + 1)] = draw.astype(BF16)
        dqg_ref[...] += dqg
        for hh in range(nkv):
            kv_ref[:, HEAD_DIM * hh:HEAD_DIM * (hh + 1)] = dk_ref[hh]
        dkg = jnp.zeros((1, 128), F32)
        for a in range(nkv // 2):
            sl = slice(128 * a, 128 * (a + 1))
            draw, dg = norm_rope_bwd(kv_ref[:, sl], kraw_ref[:, sl], kg_ref[...])
            dkg = dkg + dg
            dproj_ref[:, o_k + 128 * a:o_k + 128 * (a + 1)] = draw.astype(BF16)
        dkg_ref[...] += dkg
        for hh in range(nkv):
            kv_ref[:, HEAD_DIM * hh:HEAD_DIM * (hh + 1)] = dv_ref[hh]
        dproj_ref[:, o_v:o_g] = kv_ref[...].astype(BF16)
        dproj_ref[:, o_g:4 * d] = dgt_ref[...]
        dh = _dot_nt(dproj_ref[...], w_ref[...])
        x = x_ref[...]
        r = lax.rsqrt(jnp.mean(x * x, axis=-1, keepdims=True) + NORM_EPS)
        xh0 = x * r
        dgm_ref[...] += jnp.sum(dh * xh0, axis=0, keepdims=True)
        dxh = dh * g_ref[...]
        dx0_ref[...] = dx1_ref[...] + r * (dxh - xh0 * jnp.mean(dxh * xh0, axis=-1, keepdims=True))

    return _pcall(
        body, name="in_proj_bwd", grid=(t // tm,),
        in_specs=[_row_spec(tm, d), _row_spec(tm, d), _row_spec(tm, du), _row_spec(tm, du), _row_spec(tm, du),
                  _row_spec(tm, d), _row_spec(tm, dkw), _row_spec(tm, d), _heads_spec(nkv, tm), _heads_spec(nkv, tm),
                  _row_spec(tm, 2 * d), _full_spec((1, d)), _ANY, _full_spec((1, 128)), _full_spec((1, 128)),
                  _row_spec(tm, 128), _row_spec(tm, 128)],
        out_specs=[_row_spec(tm, d), _row_spec(tm, 4 * d), _full_spec((1, d)), _full_spec((1, 128)),
                   _full_spec((1, 128))],
        out_shape=[jax.ShapeDtypeStruct((t, d), F32), jax.ShapeDtypeStruct((t, 4 * d), BF16),
                   jax.ShapeDtypeStruct((1, d), F32), jax.ShapeDtypeStruct((1, 128), F32),
                   jax.ShapeDtypeStruct((1, 128), F32)],
        scratch=[pltpu.VMEM((d, 4 * d), BF16), pltpu.VMEM((tm, dkw), F32), pltpu.SemaphoreType.DMA((1,))],
        vmem=V7X_VMEM_LIMIT,
    )(x0, dx1, dud, duf, dub, qraw, kraw, dq, dk, dv, dgates, g_mix, w_in, qg, kg, cos, sin)


def _attn_fwd(qr, kr, vv, n_valid):
    nh, t, hd = qr.shape
    nkv = kr.shape[0]
    rep = nh // nkv
    tq, tk = Q_TILE, KV_TILE
    rows = rep * tq

    def body(q_ref, k_ref, v_ref, o_ref, lse_ref, m_scr, l_scr, acc_scr):
        j = pl.program_id(2)

        @pl.when(j == 0)
        def _():
            m_scr[...] = jnp.full(m_scr.shape, NEG_BIG, F32)
            l_scr[...] = jnp.zeros_like(l_scr)
            acc_scr[...] = jnp.zeros_like(acc_scr)

        q = q_ref[...].reshape(rows, hd)
        s = _dot_nt(q, k_ref[0])
        col = j * tk + lax.broadcasted_iota(jnp.int32, (1, tk), 1)
        s = jnp.where(col < n_valid, s, NEG_BIG)
        m_prev = m_scr[...]
        m_next = jnp.maximum(m_prev, jnp.max(s, axis=-1, keepdims=True))
        p = jnp.exp(s - m_next)
        alpha = jnp.exp(m_prev - m_next)
        l_scr[...] = alpha * l_scr[...] + jnp.sum(p, axis=-1, keepdims=True)
        acc_scr[...] = alpha * acc_scr[...] + _dot(p.astype(BF16), v_ref[0])
        m_scr[...] = m_next

        @pl.when(j == pl.num_programs(2) - 1)
        def _():
            o = acc_scr[...] / l_scr[...]
            for r in range(rep):
                o_ref[:, hd * r:hd * (r + 1)] = o[tq * r:tq * (r + 1), :]
            lse_ref[...] = (m_scr[...] + jnp.log(l_scr[...])).reshape(rep, tq, 1)

    return _pcall(
        body, name="attn_fwd", grid=(nkv, t // tq, t // tk),
        in_specs=[pl.BlockSpec((rep, tq, hd), lambda g, i, j: (g, i, 0)),
                  pl.BlockSpec((1, tk, hd), lambda g, i, j: (g, j, 0)),
                  pl.BlockSpec((1, tk, hd), lambda g, i, j: (g, j, 0))],
        out_specs=[pl.BlockSpec((tq, rep * hd), lambda g, i, j: (i, g)),
                   pl.BlockSpec((rep, tq, 1), lambda g, i, j: (g, i, 0))],
        out_shape=[jax.ShapeDtypeStruct((t, nh * hd), F32), jax.ShapeDtypeStruct((nh, t, 1), F32)],
        scratch=[pltpu.VMEM((rows, 1), F32), pltpu.VMEM((rows, 1), F32), pltpu.VMEM((rows, hd), F32)],
        vmem=V7X_VMEM_LIMIT,
    )(qr, kr, vv)


def _attn_bwd_dq(qr, kr, vv, o, do, lse, n_valid):
    nh, t, hd = qr.shape
    nkv = kr.shape[0]
    rep = nh // nkv
    tq, tk = Q_TILE, KV_TILE
    rows = rep * tq

    def body(q_ref, k_ref, v_ref, o_ref, do_ref, lse_ref, dq_ref, delta_ref, do_scr, delta_scr, acc_scr):
        j = pl.program_id(2)

        @pl.when(j == 0)
        def _():
            for r in range(rep):
                do_r = do_ref[:, hd * r:hd * (r + 1)]
                delta_scr[tq * r:tq * (r + 1), :] = jnp.sum(do_r * o_ref[:, hd * r:hd * (r + 1)], axis=-1, keepdims=True)
                do_scr[tq * r:tq * (r + 1), :] = do_r.astype(BF16)
            acc_scr[...] = jnp.zeros_like(acc_scr)

        q = q_ref[...].reshape(rows, hd)
        k = k_ref[0]
        s = _dot_nt(q, k)
        col = j * tk + lax.broadcasted_iota(jnp.int32, (1, tk), 1)
        s = jnp.where(col < n_valid, s, NEG_BIG)
        p = jnp.exp(s - lse_ref[...].reshape(rows, 1))
        dp = _dot_nt(do_scr[...], v_ref[0])
        ds = p * (dp - delta_scr[...])
        acc_scr[...] += _dot(ds.astype(BF16), k)

        @pl.when(j == pl.num_programs(2) - 1)
        def _():
            for r in range(rep):
                dq_ref[:, hd * r:hd * (r + 1)] = acc_scr[tq * r:tq * (r + 1), :]
            delta_ref[...] = delta_scr[...].reshape(rep, tq, 1)

    return _pcall(
        body, name="attn_bwd_dq", grid=(nkv, t // tq, t // tk),
        in_specs=[pl.BlockSpec((rep, tq, hd), lambda g, i, j: (g, i, 0)),
                  pl.BlockSpec((1, tk, hd), lambda g, i, j: (g, j, 0)),
                  pl.BlockSpec((1, tk, hd), lambda g, i, j: (g, j, 0)),
                  pl.BlockSpec((tq, rep * hd), lambda g, i, j: (i, g)),
                  pl.BlockSpec((tq, rep * hd), lambda g, i, j: (i, g)),
                  pl.BlockSpec((rep, tq, 1), lambda g, i, j: (g, i, 0))],
        out_specs=[pl.BlockSpec((tq, rep * hd), lambda g, i, j: (i, g)),
                   pl.BlockSpec((rep, tq, 1), lambda g, i, j: (g, i, 0))],
        out_shape=[jax.ShapeDtypeStruct((t, nh * hd), F32), jax.ShapeDtypeStruct((nh, t, 1), F32)],
        scratch=[pltpu.VMEM((rows, hd), BF16), pltpu.VMEM((rows, 1), F32), pltpu.VMEM((rows, hd), F32)],
        vmem=V7X_VMEM_LIMIT,
    )(qr, kr, vv, o, do, lse)


def _attn_bwd_dkv(qr, kr, vv, do, lse_row, delta_row, n_valid):
    nh, t, hd = qr.shape
    nkv = kr.shape[0]
    rep = nh // nkv
    tq = tk = KV_TILE

    def body(q_ref, k_ref, v_ref, do_ref, lse_ref, delta_ref, dk_ref, dv_ref):
        i = pl.program_id(2)

        @pl.when(i == 0)
        def _():
            dk_ref[...] = jnp.zeros_like(dk_ref)
            dv_ref[...] = jnp.zeros_like(dv_ref)

        k = k_ref[0]
        v = v_ref[0]
        key = pl.program_id(1) * tk + lax.broadcasted_iota(jnp.int32, (tk, 1), 0)
        key_ok = key < n_valid
        dk = jnp.zeros((tk, hd), F32)
        dv = jnp.zeros((tk, hd), F32)
        for r in range(rep):
            q = q_ref[r]
            do_r = do_ref[:, hd * r:hd * (r + 1)].astype(BF16)
            st = jnp.where(key_ok, _dot_nt(k, q), NEG_BIG)
            pt = jnp.exp(st - lse_ref[0, r:r + 1, :])
            dpt = _dot_nt(v, do_r)
            dst = pt * (dpt - delta_ref[0, r:r + 1, :])
            dv = dv + _dot(pt.astype(BF16), do_r)
            dk = dk + _dot(dst.astype(BF16), q)
        dk_ref[0] += dk
        dv_ref[0] += dv

    return _pcall(
        body, name="attn_bwd_dkv", grid=(nkv, t // tk, t // tq),
        in_specs=[pl.BlockSpec((rep, tq, hd), lambda g, j, i: (g, i, 0)),
                  pl.BlockSpec((1, tk, hd), lambda g, j, i: (g, j, 0)),
                  pl.BlockSpec((1, tk, hd), lambda g, j, i: (g, j, 0)),
                  pl.BlockSpec((tq, rep * hd), lambda g, j, i: (i, g)),
                  pl.BlockSpec((1, rep, tq), lambda g, j, i: (g, 0, i)),
                  pl.BlockSpec((1, rep, tq), lambda g, j, i: (g, 0, i))],
        out_specs=[pl.BlockSpec((1, tk, hd), lambda g, j, i: (g, j, 0)),
                   pl.BlockSpec((1, tk, hd), lambda g, j, i: (g, j, 0))],
        out_shape=[jax.ShapeDtypeStruct((nkv, t, hd), F32), jax.ShapeDtypeStruct((nkv, t, hd), F32)],
        vmem=V7X_VMEM_LIMIT,
    )(qr, kr, vv, do, lse_row, delta_row)


def _ssm_scan_fwd(u2, lam_re, lam_im, bb_re, bb_im, cc_re, cc_im):
    _, t, w = u2.shape
    gn = lam_re.shape[-1]
    tc = ROW_TILE
    cl = min(gn, SCAN_LANES)

    def body(u_ref, lr_ref, li_ref, br_ref, bi_ref, cr_ref, ci_ref, y_ref, xr_ref, xi_ref,
             bur_scr, bui_scr, cr_scr, ci_scr):
        @pl.when(pl.program_id(1) == 0)
        def _():
            cr_scr[...] = jnp.zeros_like(cr_scr)
            ci_scr[...] = jnp.zeros_like(ci_scr)

        ub = u_ref[0]
        bur_scr[...] = _dot(ub, br_ref[0])
        bui_scr[...] = _dot(ub, bi_ref[0])
        for c0 in range(0, gn, cl):
            lanes = pl.ds(c0, cl)
            lr = lr_ref[0, :, lanes]
            li = li_ref[0, :, lanes]

            def step(r, carry, lanes=lanes, lr=lr, li=li):
                xr, xi = carry
                nr = (lr * xr - li * xi) + bur_scr[pl.ds(r, 1), lanes]
                ni = (lr * xi + li * xr) + bui_scr[pl.ds(r, 1), lanes]
                xr_ref[0, pl.ds(r, 1), lanes] = nr
                xi_ref[0, pl.ds(r, 1), lanes] = ni
                return nr, ni

            xr, xi = lax.fori_loop(0, tc, step, (cr_scr[:, lanes], ci_scr[:, lanes]), unroll=8)
            cr_scr[:, lanes] = xr
            ci_scr[:, lanes] = xi
        y_ref[0] = _dot(xr_ref[0].astype(BF16), cr_ref[0]) - _dot(xi_ref[0].astype(BF16), ci_ref[0])

    row = lambda width: pl.BlockSpec((1, tc, width), lambda dd, i: (dd, i, 0))
    per_dir = lambda a, b: pl.BlockSpec((1, a, b), lambda dd, i: (dd, 0, 0))
    return _pcall(
        body, name="ssm_scan_fwd", grid=(2, t // tc),
        in_specs=[row(w), per_dir(1, gn), per_dir(1, gn), per_dir(w, gn), per_dir(w, gn), per_dir(gn, w),
                  per_dir(gn, w)],
        out_specs=[row(w), row(gn), row(gn)],
        out_shape=[jax.ShapeDtypeStruct((2, t, w), F32), jax.ShapeDtypeStruct((2, t, gn), F32),
                   jax.ShapeDtypeStruct((2, t, gn), F32)],
        scratch=[pltpu.VMEM((tc, gn), F32), pltpu.VMEM((tc, gn), F32), pltpu.VMEM((1, gn), F32),
                 pltpu.VMEM((1, gn), F32)],
        vmem=V7X_VMEM_LIMIT,
    )(u2, lam_re, lam_im, bb_re, bb_im, cc_re, cc_im)


def _ssm_scan_bwd(dy2, xs_re, xs_im, lam_re, lam_im, cct_re, cct_im, bbt_re, bbt_im):
    _, t, w = dy2.shape
    gn = lam_re.shape[-1]
    tc = ROW_TILE
    cl = min(gn, SCAN_LANES)
    nblk = t // tc

    def body(dy_ref, xr_ref, xi_ref, hr_ref, hi_ref, lr_ref, li_ref, ctr_ref, cti_ref, btr_ref, bti_ref,
             du_ref, gr_ref, gi_ref, dlr_ref, dli_ref,
             gxr_scr, gxi_scr, cr_scr, ci_scr, ar_scr, ai_scr):
        i = pl.program_id(1)

        @pl.when(i == 0)
        def _():
            cr_scr[...] = jnp.zeros_like(cr_scr)
            ci_scr[...] = jnp.zeros_like(ci_scr)
            ar_scr[...] = jnp.zeros_like(ar_scr)
            ai_scr[...] = jnp.zeros_like(ai_scr)

        dyb = dy_ref[0]
        gxr_scr[...] = _dot(dyb, ctr_ref[0])
        gxi_scr[...] = -_dot(dyb, cti_ref[0])
        first_block = i == nblk - 1
        row0 = lax.broadcasted_iota(jnp.int32, (tc, 1), 0) == 0
        for c0 in range(0, gn, cl):
            lanes = pl.ds(c0, cl)
            lr = lr_ref[0, :, lanes]
            li = li_ref[0, :, lanes]

            def step(k, carry, lanes=lanes, lr=lr, li=li):
                gr, gi = carry
                r = tc - 1 - k
                ngr = gxr_scr[pl.ds(r, 1), lanes] + (lr * gr + li * gi)
                ngi = gxi_scr[pl.ds(r, 1), lanes] + (lr * gi - li * gr)
                gxr_scr[pl.ds(r, 1), lanes] = ngr
                gxi_scr[pl.ds(r, 1), lanes] = ngi
                return ngr, ngi

            gr, gi = lax.fori_loop(0, tc, step, (cr_scr[:, lanes], ci_scr[:, lanes]), unroll=8)
            cr_scr[:, lanes] = gr
            ci_scr[:, lanes] = gi
        for c0 in range(0, gn, 512):
            lanes = pl.ds(c0, 512)
            halo_r = jnp.where(first_block, 0.0, hr_ref[0, 7:8, lanes])
            halo_i = jnp.where(first_block, 0.0, hi_ref[0, 7:8, lanes])
            xpr = jnp.where(row0, halo_r, pltpu.roll(xr_ref[0, :, lanes], 1, 0))
            xpi = jnp.where(row0, halo_i, pltpu.roll(xi_ref[0, :, lanes], 1, 0))
            gr = gxr_scr[:, lanes]
            gi = gxi_scr[:, lanes]
            ar_scr[:, lanes] += jnp.sum(gr * xpr + gi * xpi, axis=0, keepdims=True)
            ai_scr[:, lanes] += jnp.sum(gi * xpr - gr * xpi, axis=0, keepdims=True)
        grb = gxr_scr[...].astype(BF16)
        gib = gxi_scr[...].astype(BF16)
        gr_ref[0] = grb
        gi_ref[0] = gib
        du_ref[0] = _dot(grb, btr_ref[0]) + _dot(gib, bti_ref[0])
        dlr_ref[0] = ar_scr[...]
        dli_ref[0] = ai_scr[...]

    rev = lambda width: pl.BlockSpec((1, tc, width), lambda dd, i: (dd, nblk - 1 - i, 0))
    halo = pl.BlockSpec((1, 8, gn), lambda dd, i: (dd, jnp.maximum((nblk - 1 - i) * (tc // 8) - 1, 0), 0))
    per_dir = lambda a, b: pl.BlockSpec((1, a, b), lambda dd, i: (dd, 0, 0))
    return _pcall(
        body, name="ssm_scan_bwd", grid=(2, nblk),
        in_specs=[rev(w), rev(gn), rev(gn), halo, halo, per_dir(1, gn), per_dir(1, gn), per_dir(w, gn),
                  per_dir(w, gn), per_dir(gn, w), per_dir(gn, w)],
        out_specs=[rev(w), rev(gn), rev(gn), per_dir(1, gn), per_dir(1, gn)],
        out_shape=[jax.ShapeDtypeStruct((2, t, w), F32), jax.ShapeDtypeStruct((2, t, gn), BF16),
                   jax.ShapeDtypeStruct((2, t, gn), BF16), jax.ShapeDtypeStruct((2, 1, gn), F32),
                   jax.ShapeDtypeStruct((2, 1, gn), F32)],
        scratch=[pltpu.VMEM((tc, gn), F32), pltpu.VMEM((tc, gn), F32), pltpu.VMEM((1, gn), F32),
                 pltpu.VMEM((1, gn), F32), pltpu.VMEM((1, gn), F32), pltpu.VMEM((1, gn), F32)],
        vmem=V7X_VMEM_LIMIT,
    )(dy2, xs_re, xs_im, xs_re, xs_im, lam_re, lam_im, cct_re, cct_im, bbt_re, bbt_im)


def _matmul_tn(a, b, name):
    nb, t, m = a.shape
    n = b.shape[-1]
    bm, bn, tk = min(m, 1024), min(n, 1024), KV_TILE

    def body(a_ref, b_ref, o_ref):
        @pl.when(pl.program_id(3) == 0)
        def _():
            o_ref[...] = jnp.zeros_like(o_ref)

        o_ref[0] += _dot_tn(a_ref[0].astype(BF16), b_ref[0].astype(BF16))

    return _pcall(
        body, name=name, grid=(nb, m // bm, n // bn, t // tk),
        in_specs=[pl.BlockSpec((1, tk, bm), lambda z, i, j, k: (z, k, i)),
                  pl.BlockSpec((1, tk, bn), lambda z, i, j, k: (z, k, j))],
        out_specs=pl.BlockSpec((1, bm, bn), lambda z, i, j, k: (z, i, j)),
        out_shape=jax.ShapeDtypeStruct((nb, m, n), F32), vmem=V7X_VMEM_LIMIT,
    )(a, b)


def _reduce_adamw(gparts, p, m, v, name):
    rows, width = p.shape
    tr = max(k for k in range(16, 513, 16) if rows % k == 0)

    def body(g_ref, p_ref, m_ref, v_ref, go_ref, d_ref, mo_ref, vo_ref):
        g = g_ref[0].astype(F32)
        for k in range(1, N_DEV):
            g = g + g_ref[k].astype(F32)
        go_ref[...] = g
        mm = ADAM_B1 * m_ref[...] + (1.0 - ADAM_B1) * g
        vv = ADAM_B2 * v_ref[...] + (1.0 - ADAM_B2) * (g * g)
        m_hat = mm / (1.0 - ADAM_B1 ** ADAM_STEP)
        v_hat = vv / (1.0 - ADAM_B2 ** ADAM_STEP)
        d_ref[...] = -ADAM_LR * (m_hat / (jnp.sqrt(v_hat) + ADAM_EPS) + ADAM_WD * p_ref[...])
        mo_ref[...] = mm
        vo_ref[...] = vv

    spec = pl.BlockSpec((tr, width), lambda i: (i, 0))
    out = jax.ShapeDtypeStruct((rows, width), F32)
    return _pcall(
        body, name=name, grid=(rows // tr,),
        in_specs=[pl.BlockSpec((N_DEV, tr, width), lambda i: (0, i, 0)), spec, spec, spec],
        out_specs=[spec, spec, spec, spec], out_shape=[out, out, out, out], vmem=V7X_VMEM_LIMIT,
    )(gparts, p, m, v)


def _peer(k):
    x, y, c = lax.axis_index("x"), lax.axis_index("y"), lax.axis_index("c")
    return (x ^ ((k >> 2) & 1), y ^ ((k >> 1) & 1), c ^ (k & 1))


def _my_index():
    return 4 * lax.axis_index("x") + 2 * lax.axis_index("y") + lax.axis_index("c")


def _exchange(x, name, scatter):
    shape = x.shape[1:] if scatter else x.shape
    out_shape = (N_DEV,) + tuple(shape)

    def body(x_ref, out_ref, send_sems, recv_sems, local_sem):
        me = _my_index()
        mine = x_ref.at[me] if scatter else x_ref
        local = pltpu.make_async_copy(mine, out_ref.at[me], local_sem)
        local.start()
        copies = []
        for k in range(1, N_DEV):
            peer = _peer(k)
            peer_index = 4 * peer[0] + 2 * peer[1] + peer[2]
            src = x_ref.at[peer_index] if scatter else x_ref
            cp = pltpu.make_async_remote_copy(src_ref=src, dst_ref=out_ref.at[me], send_sem=send_sems.at[k - 1],
                                              recv_sem=recv_sems.at[k - 1], device_id=peer,
                                              device_id_type=pl.DeviceIdType.MESH)
            cp.start()
            copies.append(cp)
        for cp in copies:
            cp.wait_recv()
        for cp in copies:
            cp.wait_send()
        local.wait()

    return pl.pallas_call(
        body, name=name, in_specs=[_ANY], out_specs=_ANY, out_shape=jax.ShapeDtypeStruct(out_shape, x.dtype),
        scratch_shapes=[pltpu.SemaphoreType.DMA((N_DEV - 1,)), pltpu.SemaphoreType.DMA((N_DEV - 1,)),
                        pltpu.SemaphoreType.DMA(())],
    )(x)


def _to_shards(full, axis):
    r, c = full.shape
    if axis == 0:
        return full.reshape(N_DEV, r // N_DEV, c)
    return full.reshape(r, N_DEV, c // N_DEV).transpose(1, 0, 2)


def _from_shards(shards, axis):
    _, r, c = shards.shape
    if axis == 0:
        return shards.reshape(N_DEV * r, c)
    return shards.transpose(1, 0, 2).reshape(r, N_DEV * c)


def _pack_rows(parts, lead):
    flat = [p.reshape(p.shape[:lead] + (-1, PACK_W)) for p in parts]
    out = jnp.concatenate(flat, axis=lead)
    rows = out.shape[lead]
    pad = _round_up(rows, 16) - rows
    if pad:
        out = jnp.pad(out, [(0, 0)] * lead + [(0, pad), (0, 0)])
    return out


def _pack_flat(parts):
    flat = jnp.concatenate([p.reshape(-1) for p in parts])
    n = flat.shape[0]
    flat = jnp.pad(flat, (0, _round_up(n, 16 * PACK_W) - n))
    return flat.reshape(-1, PACK_W)


def _unpack(packed, shapes):
    flat = packed.reshape(-1)
    out, off = [], 0
    for shp in shapes:
        n = math.prod(shp)
        out.append(flat[off:off + n].reshape(shp))
        off += n
    return out


def _ssm_discretize(a_re, a_im, log_dt, b_re, b_im):
    dt = jnp.exp(log_dt)[..., None]
    lam_re = jnp.minimum(a_re, EIG_RE_MAX)
    lam_im = a_im
    mag = jnp.exp(lam_re * dt)
    ang = lam_im * dt
    lb_re = mag * jnp.cos(ang)
    lb_im = mag * jnp.sin(ang)
    num_re = lb_re - 1.0
    num_im = lb_im
    den = lam_re * lam_re + lam_im * lam_im
    f_re = (num_re * lam_re + num_im * lam_im) / den
    f_im = (num_im * lam_re - num_re * lam_im) / den
    bb_re = f_re[..., None] * b_re - f_im[..., None] * b_im
    bb_im = f_re[..., None] * b_im + f_im[..., None] * b_re
    return lb_re, lb_im, bb_re, bb_im


def _block_diag(blocks):
    two, g, a, b = blocks.shape
    eye = jnp.eye(g, dtype=blocks.dtype)
    return (blocks[:, :, :, None, :] * eye[None, :, None, :, None]).reshape(two, g * a, g * b)


def _diag_blocks(dense, g):
    two, ga, gb = dense.shape
    return jnp.einsum("zgagb->zgab", dense.reshape(two, g, ga // g, g, gb // g))


def _rope_tables(t, n_valid):
    pos = jnp.arange(t)
    real = jnp.logical_and(pos >= N_META, pos < n_valid)
    idx = jnp.where(real, pos - N_META, 0)
    row_id = (idx // GRID_W).astype(F32)
    col_id = (idx % GRID_W).astype(F32)
    pairs_per_axis = HEAD_DIM // 4
    inv_freq = ROPE_THETA ** (-jnp.arange(pairs_per_axis, dtype=F32) / pairs_per_axis)
    ang = jnp.concatenate([row_id[:, None] * inv_freq, col_id[:, None] * inv_freq], axis=-1)
    ang = jnp.where(real[:, None], ang, 0.0)
    cos = jnp.repeat(jnp.cos(ang), 2, axis=-1)
    sin = jnp.sin(ang)
    sin = jnp.stack([-sin, sin], axis=-1).reshape(t, HEAD_DIM)
    return jnp.tile(cos, (1, 2)), jnp.tile(sin, (1, 2))


def _local_step(x, loss_target, big, small):
    s_len, d = x.shape
    n_valid = s_len + N_META
    t = _round_up(n_valid, KV_TILE)
    du = d // 2
    groups = du // SSM_GROUP
    nh = d // HEAD_DIM
    nkv = nh // KV_REP
    pad = t - n_valid

    x0 = jnp.concatenate([big["meta_tokens"].astype(F32), x, jnp.zeros((pad, d), F32)], axis=0)
    tgt = jnp.concatenate([jnp.zeros((N_META, d), F32), loss_target, jnp.zeros((pad, d), F32)], axis=0)
    cos, sin = _rope_tables(t, n_valid)
    g_mix = small["norm_mix_g"].reshape(1, d)
    g_mlp = small["norm_mlp_g"].reshape(1, d)
    g_fin = small["norm_final_g"].reshape(1, d)
    qg = jnp.tile(small["q_norm_g"].reshape(1, HEAD_DIM), (1, 2))
    kg = jnp.tile(small["k_norm_g"].reshape(1, HEAD_DIM), (1, 2))
    ssm_d = small["ssm_d"].reshape(1, du)
    b_glu = small["b_glu"].reshape(1, du)

    ssm_in = tuple(small[n][0] for n in ("ssm_a_re", "ssm_a_im", "ssm_log_dt", "ssm_b_re", "ssm_b_im"))
    (lb_re, lb_im, bbar_re, bbar_im), disc_vjp = jax.vjp(_ssm_discretize, *ssm_in)
    lam_re = lb_re.reshape(2, 1, groups * SSM_STATE)
    lam_im = lb_im.reshape(2, 1, groups * SSM_STATE)
    bb_re = _block_diag(bbar_re.transpose(0, 1, 3, 2)).astype(BF16)
    bb_im = _block_diag(bbar_im.transpose(0, 1, 3, 2)).astype(BF16)
    c_re, c_im = small["ssm_c_re"][0], small["ssm_c_im"][0]
    cct_re = _block_diag(c_re).astype(BF16)
    cct_im = _block_diag(c_im).astype(BF16)
    cc_re = cct_re.transpose(0, 2, 1)
    cc_im = cct_im.transpose(0, 2, 1)
    bbt_re = bb_re.transpose(0, 2, 1)
    bbt_im = bb_im.transpose(0, 2, 1)

    h, u, ub, qraw, kraw, qr, kr, vv, gates = _in_proj_fwd(x0, g_mix, big["w_in"], qg, kg, cos, sin)
    u2 = jnp.stack([ub, ub[::-1]])
    y2, xs_re, xs_im = _ssm_scan_fwd(u2, lam_re, lam_im, bb_re, bb_im, cc_re, cc_im)
    yf, yb = y2[0], y2[1][::-1]
    y_attn, lse = _attn_fwd(qr, kr, vv, n_valid)
    mixer_w = (ssm_d, big["w_glu"], b_glu, big["w_ssm_proj"], big["w_attn_proj"], big["w_out"])
    x1 = _mixer_out_fwd(x0, u, yf, yb, y_attn, gates, *mixer_w)

    dx1, loss8, dg_fin, dg_mlp, h2b, dab, hsqb, dx2b = _mlp_loss_fwd_bwd(
        x1, tgt, g_mlp, g_fin, big["w_mlp_in"], big["w_mlp_out"], n_valid)

    (dyb, dud, dy_attn, dgates, zb, dglb, ysb, dasb, yab, daab, mgb, dxb, d_ssm_d, d_b_glu) = _mixer_out_bwd(
        dx1, u, yf, yb, y_attn, gates, *mixer_w)
    dq, delta = _attn_bwd_dq(qr, kr, vv, y_attn, dy_attn, lse, n_valid)
    dk, dv = _attn_bwd_dkv(qr, kr, vv, dy_attn, lse.reshape(nkv, KV_REP, t), delta.reshape(nkv, KV_REP, t), n_valid)
    dy2 = jnp.stack([dyb, dyb[::-1]])
    du2, g_re, g_im, dlam_re, dlam_im = _ssm_scan_bwd(dy2, xs_re, xs_im, lam_re, lam_im, cct_re, cct_im,
                                                      bbt_re, bbt_im)
    dx0, dproj, dg_mix, dqg, dkg = _in_proj_bwd(x0, dx1, dud, du2[0], du2[1][::-1], qraw, kraw, dq, dk, dv, dgates,
                                                g_mix, big["w_in"], qg, kg, cos, sin)

    one = lambda a: a[None]
    grads = {}
    grads["w_in"] = _matmul_tn(one(h), one(dproj), "grad_w_in")[0]
    grads["w_glu"] = _matmul_tn(one(zb), one(dglb), "grad_w_glu")[0]
    grads["w_ssm_proj"] = _matmul_tn(one(ysb), one(dasb), "grad_w_ssm_proj")[0]
    grads["w_attn_proj"] = _matmul_tn(one(yab), one(daab), "grad_w_attn_proj")[0]
    grads["w_out"] = _matmul_tn(one(mgb), one(dxb), "grad_w_out")[0]
    grads["w_mlp_in"] = _matmul_tn(one(h2b), one(dab), "grad_w_mlp_in")[0]
    grads["w_mlp_out"] = _matmul_tn(one(hsqb), one(dx2b), "grad_w_mlp_out")[0]
    grads["meta_tokens"] = dx0[0:N_META]
    dbb_re = _diag_blocks(_matmul_tn(u2, g_re, "grad_ssm_bbar_re"), groups).transpose(0, 1, 3, 2)
    dbb_im = _diag_blocks(_matmul_tn(u2, g_im, "grad_ssm_bbar_im"), groups).transpose(0, 1, 3, 2)
    dc_re = _diag_blocks(_matmul_tn(dy2, xs_re, "grad_ssm_c_re"), groups)
    dc_im = -_diag_blocks(_matmul_tn(dy2, xs_im, "grad_ssm_c_im"), groups)
    shape_gn = (2, groups, SSM_STATE)
    d_a_re, d_a_im, d_log_dt, d_b_re, d_b_im = disc_vjp(
        (dlam_re.reshape(shape_gn), dlam_im.reshape(shape_gn), dbb_re, dbb_im))
    grads.update({
        "norm_mix_g": dg_mix, "ssm_a_re": d_a_re[None], "ssm_a_im": d_a_im[None], "ssm_log_dt": d_log_dt[None],
        "ssm_b_re": d_b_re[None], "ssm_b_im": d_b_im[None], "ssm_c_re": dc_re[None], "ssm_c_im": dc_im[None],
        "ssm_d": d_ssm_d, "b_glu": d_b_glu,
        "q_norm_g": dqg[:, 0:HEAD_DIM] + dqg[:, HEAD_DIM:128], "k_norm_g": dkg[:, 0:HEAD_DIM] + dkg[:, HEAD_DIM:128],
        "norm_mlp_g": dg_mlp, "norm_final_g": dg_fin.reshape(d),
    })
    return loss8[0, 0], dx0[N_META:n_valid], grads


def kernel(x, meta_tokens, norm_mix_g, w_in, ssm_a_re, ssm_a_im, ssm_log_dt, ssm_b_re, ssm_b_im, ssm_c_re, ssm_c_im, ssm_d, w_glu, b_glu, q_norm_g, k_norm_g, w_ssm_proj, w_attn_proj, w_out, norm_mlp_g, w_mlp_in, w_mlp_out, norm_final_g, loss_target, m_meta_tokens, m_norm_mix_g, m_w_in, m_ssm_a_re, m_ssm_a_im, m_ssm_log_dt, m_ssm_b_re, m_ssm_b_im, m_ssm_c_re, m_ssm_c_im, m_ssm_d, m_w_glu, m_b_glu, m_q_norm_g, m_k_norm_g, m_w_ssm_proj, m_w_attn_proj, m_w_out, m_norm_mlp_g, m_w_mlp_in, m_w_mlp_out, m_norm_final_g, v_meta_tokens, v_norm_mix_g, v_w_in, v_ssm_a_re, v_ssm_a_im, v_ssm_log_dt, v_ssm_b_re, v_ssm_b_im, v_ssm_c_re, v_ssm_c_im, v_ssm_d, v_w_glu, v_b_glu, v_q_norm_g, v_k_norm_g, v_w_ssm_proj, v_w_attn_proj, v_w_out, v_norm_mlp_g, v_w_mlp_in, v_w_mlp_out, v_norm_final_g):
    w = dict(meta_tokens=meta_tokens, norm_mix_g=norm_mix_g, w_in=w_in, ssm_a_re=ssm_a_re, ssm_a_im=ssm_a_im, ssm_log_dt=ssm_log_dt, ssm_b_re=ssm_b_re, ssm_b_im=ssm_b_im, ssm_c_re=ssm_c_re, ssm_c_im=ssm_c_im, ssm_d=ssm_d, w_glu=w_glu, b_glu=b_glu, q_norm_g=q_norm_g, k_norm_g=k_norm_g, w_ssm_proj=w_ssm_proj, w_attn_proj=w_attn_proj, w_out=w_out, norm_mlp_g=norm_mlp_g, w_mlp_in=w_mlp_in, w_mlp_out=w_mlp_out, norm_final_g=norm_final_g)
    m = dict(meta_tokens=m_meta_tokens, norm_mix_g=m_norm_mix_g, w_in=m_w_in, ssm_a_re=m_ssm_a_re, ssm_a_im=m_ssm_a_im, ssm_log_dt=m_ssm_log_dt, ssm_b_re=m_ssm_b_re, ssm_b_im=m_ssm_b_im, ssm_c_re=m_ssm_c_re, ssm_c_im=m_ssm_c_im, ssm_d=m_ssm_d, w_glu=m_w_glu, b_glu=m_b_glu, q_norm_g=m_q_norm_g, k_norm_g=m_k_norm_g, w_ssm_proj=m_w_ssm_proj, w_attn_proj=m_w_attn_proj, w_out=m_w_out, norm_mlp_g=m_norm_mlp_g, w_mlp_in=m_w_mlp_in, w_mlp_out=m_w_mlp_out, norm_final_g=m_norm_final_g)
    v = dict(meta_tokens=v_meta_tokens, norm_mix_g=v_norm_mix_g, w_in=v_w_in, ssm_a_re=v_ssm_a_re, ssm_a_im=v_ssm_a_im, ssm_log_dt=v_ssm_log_dt, ssm_b_re=v_ssm_b_re, ssm_b_im=v_ssm_b_im, ssm_c_re=v_ssm_c_re, ssm_c_im=v_ssm_c_im, ssm_d=v_ssm_d, w_glu=v_w_glu, b_glu=v_b_glu, q_norm_g=v_q_norm_g, k_norm_g=v_k_norm_g, w_ssm_proj=v_w_ssm_proj, w_attn_proj=v_w_attn_proj, w_out=v_w_out, norm_mlp_g=v_norm_mlp_g, w_mlp_in=v_w_mlp_in, w_mlp_out=v_w_mlp_out, norm_final_g=v_norm_final_g)

    shard2d = {n: w[n].reshape(w[n].shape[-2:]) for n in BIG_WEIGHTS}
    big_shapes = [shard2d[n].shape for n in BIG_WEIGHTS]

    meta_hi = shard2d["meta_tokens"].astype(BF16)
    meta_res = shard2d["meta_tokens"] - meta_hi.astype(F32)
    meta_mid = meta_res.astype(BF16)
    meta_lo = (meta_res - meta_mid.astype(F32)).astype(BF16)
    to_gather = [meta_hi, meta_mid, meta_lo] + [shard2d[n].astype(BF16) for n in BIG_WEIGHTS[1:]]
    gather_shapes = [meta_hi.shape] * 3 + big_shapes[1:]
    gathered = _exchange(_pack_rows(to_gather, 0), "gather_weights", scatter=False)
    per_dev = [_unpack(gathered[j], gather_shapes) for j in range(N_DEV)]
    full = [_from_shards(jnp.stack([per_dev[j][k] for j in range(N_DEV)]), BIG_SHARD_AXIS[n])
            for k, n in enumerate(("meta_tokens",) * 2 + BIG_WEIGHTS)]
    big = dict(zip(BIG_WEIGHTS[1:], full[3:]))
    big["meta_tokens"] = (full[0].astype(F32) + full[1].astype(F32)) + full[2].astype(F32)
    small = {n: w[n] for n in SMALL_WEIGHTS}

    loss, grad_x, grads = _local_step(x[0], loss_target[0], big, small)
    loss = lax.psum(loss, ("x", "y", "c"))

    g_blocks = _pack_rows([_to_shards(grads[n], BIG_SHARD_AXIS[n]) for n in BIG_WEIGHTS], 1).astype(BF16)
    g_parts = _exchange(g_blocks, "scatter_grads", scatter=True)
    pk = lambda src: _pack_rows([src[n].reshape(shard2d[n].shape) for n in BIG_WEIGHTS], 0)
    big_out = _reduce_adamw(g_parts, pk(w), pk(m), pk(v), "adamw_sharded")
    small_shapes = [w[n].shape for n in SMALL_WEIGHTS]
    s_parts = _exchange(_pack_flat([grads[n] for n in SMALL_WEIGHTS]), "gather_small_grads", scatter=False)
    pf = lambda src: _pack_flat([src[n] for n in SMALL_WEIGHTS])
    small_out = _reduce_adamw(s_parts, pf(w), pf(m), pf(v), "adamw_replicated")

    results = []
    for kind in range(4):
        big_un = dict(zip(BIG_WEIGHTS, _unpack(big_out[kind], big_shapes)))
        small_un = dict(zip(SMALL_WEIGHTS, _unpack(small_out[kind], small_shapes)))
        for n in ALL_WEIGHTS:
            results.append(big_un[n].reshape(w[n].shape) if n in big_un else small_un[n])
    return (loss, grad_x[None], *results)
```

```python
import functools
import math

import jax
import jax.numpy as jnp
from jax import lax
from jax.experimental import pallas as pl
from jax.experimental.pallas import tpu as pltpu

F32 = jnp.float32
BF16 = jnp.bfloat16

N_DEV = 8
N_META = 16
GRID_W = 64
SSM_GROUP = 16
SSM_STATE = 64
HEAD_DIM = 64
KV_REP = 4
ROPE_THETA = 10000.0
NORM_EPS = 1e-6
EIG_RE_MAX = -1e-4
ATTN_SCALE = HEAD_DIM ** -0.5

ADAM_LR = 0.001
ADAM_B1 = 0.9
ADAM_B2 = 0.999
ADAM_EPS = 1e-08
ADAM_WD = 0.01
ADAM_STEP = 10

ROW_TILE = 384
ROW_TILE_BWD = 256
VT_ROWS = 80
MASK_BIAS = -1e30
SCAN_LANES = 1024
KV_TILE = 768
PACK_W = 1024
V7X_VMEM_LIMIT = 56 * 1024 * 1024
NEG_BIG = -1e30

BIG_WEIGHTS = ("meta_tokens", "w_in", "w_glu", "w_ssm_proj", "w_attn_proj", "w_out", "w_mlp_in", "w_mlp_out")
BIG_SHARD_AXIS = {"meta_tokens": 1, "w_in": 1, "w_glu": 0, "w_ssm_proj": 1, "w_attn_proj": 0, "w_out": 0,
                  "w_mlp_in": 1, "w_mlp_out": 0}
BLOCK_WEIGHTS = ("w_in", "w_mlp_in", "w_mlp_out")
SMALL_WEIGHTS = ("norm_mix_g", "ssm_a_re", "ssm_a_im", "ssm_log_dt", "ssm_b_re", "ssm_b_im", "ssm_c_re",
                 "ssm_c_im", "ssm_d", "b_glu", "q_norm_g", "k_norm_g", "norm_mlp_g", "norm_final_g")
ALL_WEIGHTS = ("meta_tokens", "norm_mix_g", "w_in", "ssm_a_re", "ssm_a_im", "ssm_log_dt", "ssm_b_re", "ssm_b_im",
               "ssm_c_re", "ssm_c_im", "ssm_d", "w_glu", "b_glu", "q_norm_g", "k_norm_g", "w_ssm_proj",
               "w_attn_proj", "w_out", "norm_mlp_g", "w_mlp_in", "w_mlp_out", "norm_final_g")


def _round_up(n, m):
    return (n + m - 1) // m * m


def _pcall(body, *, name, grid, in_specs, out_specs, out_shape, scratch=(), vmem=None, **kw):
    params = pltpu.CompilerParams(dimension_semantics=("arbitrary",) * len(grid), vmem_limit_bytes=vmem)
    return pl.pallas_call(body, name=name, grid=grid, in_specs=in_specs, out_specs=out_specs, out_shape=out_shape,
                          scratch_shapes=list(scratch), compiler_params=params, **kw)


def _dot(a, b):
    return jnp.dot(a, b, preferred_element_type=F32)


def _dot_nt(a, b):
    return lax.dot_general(a, b, (((1,), (1,)), ((), ())), preferred_element_type=F32)


def _dot_tn(a, b):
    return lax.dot_general(a, b, (((0,), (0,)), ((), ())), preferred_element_type=F32)


def _full_spec(shape):
    nd = len(shape)
    return pl.BlockSpec(shape, lambda *_: (0,) * nd)


def _row_spec(tm, width):
    return pl.BlockSpec((tm, width), lambda i: (i, 0))


def _heads_spec(nh, tm):
    return pl.BlockSpec((nh, tm, HEAD_DIM), lambda i: (0, i, 0))


_ANY = pl.BlockSpec(memory_space=pl.ANY)


def _load_once(step, pairs, sem):
    @pl.when(step == 0)
    def _():
        copies = [pltpu.make_async_copy(src, dst, sem.at[k]) for k, (src, dst) in enumerate(pairs)]
        for cp in copies:
            cp.start()
        for cp in copies:
            cp.wait()


def _swap_pairs(x, even):
    n = x.shape[-1]
    return jnp.where(even, pltpu.roll(x, n - 1, 1), pltpu.roll(x, 1, 1))


def _gelu(y):
    return 0.5 * y * (1.0 + lax.erf(y * (1.0 / math.sqrt(2.0))))


def _gelu_grad(y):
    return 0.5 * (1.0 + lax.erf(y * (1.0 / math.sqrt(2.0)))) + y * jnp.exp(-0.5 * y * y) * (1.0 / math.sqrt(2.0 * math.pi))


def _in_proj_fwd(x0, g_mix, w_in, qg, kg, cos, sin, n_valid):
    t, d = x0.shape
    tm = ROW_TILE
    du, dk = d // 2, d // 4
    nh, nkv = d // HEAD_DIM, d // HEAD_DIM // KV_REP
    bw = w_in.shape[-1]
    assert bw == du and dk * 2 == bw

    def body(x_ref, g_ref, w_hbm, qg_ref, kg_ref, c_ref, s_ref,
             h_ref, u_ref, ub_ref, qraw_ref, kraw_ref, qa_ref, ka_ref, va_ref, gates_ref,
             w_ref, sem):
        i = pl.program_id(0)
        _load_once(i, [(w_hbm, w_ref)], sem)
        x = x_ref[...]
        r = lax.rsqrt(jnp.mean(x * x, axis=-1, keepdims=True) + NORM_EPS)
        h = ((x * r) * g_ref[...]).astype(BF16)
        h_ref[...] = h
        u = _dot(h, w_ref[0])
        u_ref[...] = u
        ub_ref[...] = u.astype(BF16)
        lane = lax.broadcasted_iota(jnp.int32, (tm, 128), 1)
        lo = lane < HEAD_DIM
        even = (lane & 1) == 0
        aug = lane == HEAD_DIM
        c = c_ref[...]
        s = s_ref[...]
        row = i * tm + lax.broadcasted_iota(jnp.int32, (tm, 1), 0)
        one = jnp.where(aug, 1.0, 0.0)
        key_bias = jnp.where(jnp.logical_and(aug, row >= n_valid), MASK_BIAS, 0.0)

        def norm_rope(blk, g128):
            sq = blk * blk
            ms_lo = jnp.sum(jnp.where(lo, sq, 0.0), axis=-1, keepdims=True) * (1.0 / HEAD_DIM)
            ms_hi = jnp.sum(jnp.where(lo, 0.0, sq), axis=-1, keepdims=True) * (1.0 / HEAD_DIM)
            rr = jnp.where(lo, lax.rsqrt(ms_lo + NORM_EPS), lax.rsqrt(ms_hi + NORM_EPS))
            qn = (blk * rr) * g128
            return qn * c + _swap_pairs(qn, even) * s

        def put_heads(ref, first, pair, extra):
            ref[first] = jnp.where(lo, pair, extra).astype(BF16)
            ref[first + 1] = jnp.where(lo, pltpu.roll(pair, HEAD_DIM, 1), extra).astype(BF16)

        for blk in range(2):
            qb = _dot(h, w_ref[1 + blk])
            qraw_ref[:, bw * blk:bw * (blk + 1)] = qb
            for a in range(bw // 128):
                put_heads(qa_ref, (bw // HEAD_DIM) * blk + 2 * a,
                          norm_rope(qb[:, 128 * a:128 * (a + 1)], qg_ref[...]) * ATTN_SCALE, one)
        kv = _dot(h, w_ref[3])
        kraw_ref[...] = kv[:, 0:dk]
        for a in range(nkv // 2):
            put_heads(ka_ref, 2 * a, norm_rope(kv[:, 128 * a:128 * (a + 1)], kg_ref[...]), key_bias)
            put_heads(va_ref, 2 * a, kv[:, dk + 128 * a:dk + 128 * (a + 1)], one)
        for blk in range(4):
            gates_ref[:, bw * blk:bw * (blk + 1)] = _dot(h, w_ref[4 + blk])

    heads = lambda n: pl.BlockSpec((n, tm, 128), lambda i: (0, i, 0))
    return _pcall(
        body, name="in_proj_fwd", grid=(t // tm,),
        in_specs=[_row_spec(tm, d), _full_spec((1, d)), _ANY, _full_spec((1, 128)), _full_spec((1, 128)),
                  _row_spec(tm, 128), _row_spec(tm, 128)],
        out_specs=[_row_spec(tm, d), _row_spec(tm, du), _row_spec(tm, du), _row_spec(tm, d), _row_spec(tm, dk),
                   heads(nh), heads(nkv), heads(nkv), _row_spec(tm, 2 * d)],
        out_shape=[jax.ShapeDtypeStruct((t, d), BF16), jax.ShapeDtypeStruct((t, du), F32),
                   jax.ShapeDtypeStruct((t, du), BF16), jax.ShapeDtypeStruct((t, d), F32),
                   jax.ShapeDtypeStruct((t, dk), F32), jax.ShapeDtypeStruct((nh, t, 128), BF16),
                   jax.ShapeDtypeStruct((nkv, t, 128), BF16), jax.ShapeDtypeStruct((nkv, t, 128), BF16),
                   jax.ShapeDtypeStruct((t, 2 * d), F32)],
        scratch=[pltpu.VMEM((N_DEV, d, bw), BF16), pltpu.SemaphoreType.DMA((1,))],
        vmem=V7X_VMEM_LIMIT,
    )(x0, g_mix, w_in, qg, kg, cos, sin)


def _mixer_values(u, yf, yb, yt_attn, gates, d_ref, wg_ref, bg_ref, ps_ref, pa_ref, d):
    y = (u * d_ref[...] + yf) + yb
    z = _gelu(y)
    sg = jax.nn.sigmoid(_dot(z.astype(BF16), wg_ref[...]) + bg_ref[...])
    y_ssm = z * sg
    a_ssm = _dot(y_ssm.astype(BF16), ps_ref[...])
    a_attn = _dot_tn(yt_attn.astype(BF16), pa_ref[...])
    s_ssm = jax.nn.sigmoid(gates[:, 0:d])
    s_attn = jax.nn.sigmoid(gates[:, d:2 * d])
    merged = s_ssm * a_ssm + s_attn * a_attn
    return y, z, sg, y_ssm, a_ssm, a_attn, s_ssm, s_attn, merged


def _mixer_out_fwd(x0, u, yf, yb, y_attn, gates, ssm_d, w_glu, b_glu, p_ssm, p_attn, w_out):
    t, d = x0.shape
    tm = ROW_TILE
    du = d // 2

    def body(x_ref, u_ref, yf_ref, yb_ref, ya_ref, gt_ref, d_ref, wg_ref, bg_ref, ps_ref, pa_ref, wo_ref, x1_ref):
        vals = _mixer_values(u_ref[...], yf_ref[...], yb_ref[...], ya_ref[...], gt_ref[...],
                             d_ref, wg_ref, bg_ref, ps_ref, pa_ref, d)
        merged = vals[-1]
        x1_ref[...] = x_ref[...] + _dot(merged.astype(BF16), wo_ref[...])

    return _pcall(
        body, name="mixer_out_fwd", grid=(t // tm,),
        in_specs=[_row_spec(tm, d), _row_spec(tm, du), _row_spec(tm, du), _row_spec(tm, du),
                  pl.BlockSpec((d, tm), lambda i: (0, i)), _row_spec(tm, 2 * d), _full_spec((1, du)), _full_spec((du, du)), _full_spec((1, du)),
                  _full_spec((du, d)), _full_spec((d, d)), _full_spec((d, d))],
        out_specs=_row_spec(tm, d), out_shape=jax.ShapeDtypeStruct((t, d), F32), vmem=V7X_VMEM_LIMIT,
    )(x0, u, yf, yb, y_attn, gates, ssm_d, w_glu, b_glu, p_ssm, p_attn, w_out)


def _mlp_loss_fwd_bwd(x1, target, g_mlp, g_fin, w1, w2, n_valid):
    t, d = x1.shape
    tm = ROW_TILE_BWD
    dff = 4 * d
    nfc, _, fc = w1.shape

    def body(x_ref, tg_ref, gm_ref, gf_ref, w1_hbm, w2_hbm,
             dx1_ref, loss_ref, dgf_ref, dgm_ref, h2_ref, da_ref, hsq_ref, dx2b_ref,
             w1_ref, w2_ref, relu_ref, sem):
        i = pl.program_id(0)
        _load_once(i, [(w1_hbm, w1_ref), (w2_hbm, w2_ref)], sem)

        @pl.when(i == 0)
        def _():
            loss_ref[...] = jnp.zeros_like(loss_ref)
            dgf_ref[...] = jnp.zeros_like(dgf_ref)
            dgm_ref[...] = jnp.zeros_like(dgm_ref)

        x1v = x_ref[...]
        r1 = lax.rsqrt(jnp.mean(x1v * x1v, axis=-1, keepdims=True) + NORM_EPS)
        xh1 = x1v * r1
        h2b = (xh1 * gm_ref[...]).astype(BF16)
        h2_ref[...] = h2b
        acc = jnp.zeros((tm, d), F32)
        for c in range(nfc):
            a = jnp.maximum(_dot(h2b, w1_ref[c]), 0.0)
            relu_ref[:, fc * c:fc * (c + 1)] = a
            hs = (a * a).astype(BF16)
            hsq_ref[:, fc * c:fc * (c + 1)] = hs
            acc = acc + _dot(hs, w2_ref[c])
        x2 = x1v + acc
        r2 = lax.rsqrt(jnp.mean(x2 * x2, axis=-1, keepdims=True) + NORM_EPS)
        xh2 = x2 * r2
        out = xh2 * gf_ref[...]
        row = i * tm + lax.broadcasted_iota(jnp.int32, (tm, 1), 0)
        valid = jnp.logical_and(row >= N_META, row < n_valid)
        diff = jnp.where(valid, out - tg_ref[...], 0.0)
        loss_ref[...] += 0.5 * jnp.sum(jnp.sum(diff * diff, axis=-1, keepdims=True) * (1.0 / d))
        dout = diff * (1.0 / d)
        dgf_ref[...] += jnp.sum(dout * xh2, axis=0, keepdims=True)
        dxh2 = dout * gf_ref[...]
        dx2 = r2 * (dxh2 - xh2 * jnp.mean(dxh2 * xh2, axis=-1, keepdims=True))
        dx2b = dx2.astype(BF16)
        dx2b_ref[...] = dx2b
        dh2 = jnp.zeros((tm, d), F32)
        for c in range(nfc):
            dhs = _dot_nt(dx2b, w2_ref[c])
            da = (dhs * (2.0 * relu_ref[:, fc * c:fc * (c + 1)])).astype(BF16)
            da_ref[:, fc * c:fc * (c + 1)] = da
            dh2 = dh2 + _dot_nt(da, w1_ref[c])
        dgm_ref[...] += jnp.sum(dh2 * xh1, axis=0, keepdims=True)
        dxh1 = dh2 * gm_ref[...]
        dx1_ref[...] = dx2 + r1 * (dxh1 - xh1 * jnp.mean(dxh1 * xh1, axis=-1, keepdims=True))

    return _pcall(
        body, name="mlp_loss_fwd_bwd", grid=(t // tm,),
        in_specs=[_row_spec(tm, d), _row_spec(tm, d), _full_spec((1, d)), _full_spec((1, d)), _ANY, _ANY],
        out_specs=[_row_spec(tm, d), _full_spec((8, 128)), _full_spec((1, d)), _full_spec((1, d)),
                   _row_spec(tm, d), _row_spec(tm, dff), _row_spec(tm, dff), _row_spec(tm, d)],
        out_shape=[jax.ShapeDtypeStruct((t, d), F32), jax.ShapeDtypeStruct((8, 128), F32),
                   jax.ShapeDtypeStruct((1, d), F32), jax.ShapeDtypeStruct((1, d), F32),
                   jax.ShapeDtypeStruct((t, d), BF16), jax.ShapeDtypeStruct((t, dff), BF16),
                   jax.ShapeDtypeStruct((t, dff), BF16), jax.ShapeDtypeStruct((t, d), BF16)],
        scratch=[pltpu.VMEM((nfc, d, fc), BF16), pltpu.VMEM((nfc, fc, d), BF16), pltpu.VMEM((tm, dff), F32),
                 pltpu.SemaphoreType.DMA((2,))],
        vmem=V7X_VMEM_LIMIT,
    )(x1, target, g_mlp, g_fin, w1, w2)


def _mixer_out_bwd(dx1, u, yf, yb, yt_attn, gates, ssm_d, w_glu, b_glu, p_ssm, p_attn, w_out):
    t, d = dx1.shape
    tm = ROW_TILE_BWD
    du = d // 2

    def body(dx_ref, u_ref, yf_ref, yb_ref, yt_ref, gt_ref, d_ref, wg_ref, bg_ref, ps_ref, pa_ref, wo_ref,
             dyb_ref, dud_ref, dyat_ref, dgates_ref, zb_ref, dglb_ref, ysb_ref, dasb_ref, daab_ref,
             mgb_ref, dxb_ref, dd_ref, dbg_ref):
        i = pl.program_id(0)

        @pl.when(i == 0)
        def _():
            dd_ref[...] = jnp.zeros_like(dd_ref)
            dbg_ref[...] = jnp.zeros_like(dbg_ref)

        uv = u_ref[...]
        y, z, sg, y_ssm, a_ssm, a_attn, s_ssm, s_attn, merged = _mixer_values(
            uv, yf_ref[...], yb_ref[...], yt_ref[...], gt_ref[...], d_ref, wg_ref, bg_ref, ps_ref, pa_ref, d)
        dxb = dx_ref[...].astype(BF16)
        dxb_ref[...] = dxb
        mgb_ref[...] = merged.astype(BF16)
        dmerged = _dot_nt(dxb, wo_ref[...])
        dgates_ref[:, 0:d] = (dmerged * a_ssm * (s_ssm * (1.0 - s_ssm))).astype(BF16)
        dgates_ref[:, d:2 * d] = (dmerged * a_attn * (s_attn * (1.0 - s_attn))).astype(BF16)
        da_ssm = (dmerged * s_ssm).astype(BF16)
        da_attn = (dmerged * s_attn).astype(BF16)
        dasb_ref[...] = da_ssm
        daab_ref[...] = da_attn
        ysb_ref[...] = y_ssm.astype(BF16)
        dy_ssm = _dot_nt(da_ssm, ps_ref[...])
        dyat_ref[...] = _dot_nt(pa_ref[...], da_attn).astype(BF16)
        dgl = dy_ssm * z * (sg * (1.0 - sg))
        dglb = dgl.astype(BF16)
        dglb_ref[...] = dglb
        zb_ref[...] = z.astype(BF16)
        dbg_ref[...] += jnp.sum(dgl, axis=0, keepdims=True)
        dz = dy_ssm * sg + _dot_nt(dglb, wg_ref[...])
        dy = dz * _gelu_grad(y)
        dyb_ref[...] = dy.astype(BF16)
        dd_ref[...] += jnp.sum(dy * uv, axis=0, keepdims=True)
        dud_ref[...] = dy * d_ref[...]

    bf = lambda w: jax.ShapeDtypeStruct((t, w), BF16)
    return _pcall(
        body, name="mixer_out_bwd", grid=(t // tm,),
        in_specs=[_row_spec(tm, d), _row_spec(tm, du), _row_spec(tm, du), _row_spec(tm, du),
                  pl.BlockSpec((d, tm), lambda i: (0, i)),
                  _row_spec(tm, 2 * d), _full_spec((1, du)), _full_spec((du, du)), _full_spec((1, du)),
                  _full_spec((du, d)), _full_spec((d, d)), _full_spec((d, d))],
        out_specs=[_row_spec(tm, du), _row_spec(tm, du), pl.BlockSpec((d, tm), lambda i: (0, i)),
                   _row_spec(tm, 2 * d), _row_spec(tm, du), _row_spec(tm, du), _row_spec(tm, du), _row_spec(tm, d),
                   _row_spec(tm, d), _row_spec(tm, d), _row_spec(tm, d), _full_spec((1, du)), _full_spec((1, du))],
        out_shape=[bf(du), jax.ShapeDtypeStruct((t, du), F32), jax.ShapeDtypeStruct((d, t), BF16), bf(2 * d),
                   bf(du), bf(du), bf(du), bf(d), bf(d), bf(d), bf(d),
                   jax.ShapeDtypeStruct((1, du), F32), jax.ShapeDtypeStruct((1, du), F32)],
        vmem=V7X_VMEM_LIMIT,
    )(dx1, u, yf, yb, yt_attn, gates, ssm_d, w_glu, b_glu, p_ssm, p_attn, w_out)


def _in_proj_bwd(x0, dx1, dud, duf, dub, qraw, kraw, dq, dk, dv, dgates, g_mix, w_in, qg, kg, cos, sin):
    t, d = x0.shape
    tm = ROW_TILE_BWD
    du, dkw = d // 2, d // 4
    nh, nkv = d // HEAD_DIM, d // HEAD_DIM // KV_REP
    bw = w_in.shape[-1]
    o_q, o_k, o_v, o_g = du, du + d, du + d + dkw, 2 * d

    def body(x_ref, dx1_ref, dud_ref, duf_ref, dub_ref, qraw_ref, kraw_ref, dq_ref, dk_ref, dv_ref, dgt_ref,
             g_ref, w_hbm, qg_ref, kg_ref, c_ref, s_ref,
             dx0_ref, dproj_ref, dgm_ref, dqg_ref, dkg_ref,
             w_ref, kv_ref, sem):
        i = pl.program_id(0)
        _load_once(i, [(w_hbm, w_ref)], sem)

        @pl.when(i == 0)
        def _():
            dgm_ref[...] = jnp.zeros_like(dgm_ref)
            dqg_ref[...] = jnp.zeros_like(dqg_ref)
            dkg_ref[...] = jnp.zeros_like(dkg_ref)

        lane = lax.broadcasted_iota(jnp.int32, (tm, 128), 1)
        lo = lane < HEAD_DIM
        even = (lane & 1) == 0
        c = c_ref[...]
        s = s_ref[...]

        def norm_rope_bwd(dout, raw, g128):
            sq = raw * raw
            ms_lo = jnp.sum(jnp.where(lo, sq, 0.0), axis=-1, keepdims=True) * (1.0 / HEAD_DIM)
            ms_hi = jnp.sum(jnp.where(lo, 0.0, sq), axis=-1, keepdims=True) * (1.0 / HEAD_DIM)
            rr = jnp.where(lo, lax.rsqrt(ms_lo + NORM_EPS), lax.rsqrt(ms_hi + NORM_EPS))
            xh = raw * rr
            dqn = dout * c + _swap_pairs(dout * s, even)
            dg = jnp.sum(dqn * xh, axis=0, keepdims=True)
            tt = dqn * g128
            pr = tt * xh
            mu_lo = jnp.sum(jnp.where(lo, pr, 0.0), axis=-1, keepdims=True) * (1.0 / HEAD_DIM)
            mu_hi = jnp.sum(jnp.where(lo, 0.0, pr), axis=-1, keepdims=True) * (1.0 / HEAD_DIM)
            return rr * (tt - xh * jnp.where(lo, mu_lo, mu_hi)), dg

        dub_tot = (dud_ref[...] + duf_ref[...]) + dub_ref[...]
        dproj_ref[:, 0:du] = dub_tot.astype(BF16)
        dqg = jnp.zeros((1, 128), F32)
        for a in range(nh // 2):
            sl = slice(128 * a, 128 * (a + 1))
            draw, dg = norm_rope_bwd(dq_ref[:, sl] * ATTN_SCALE, qraw_ref[:, sl], qg_ref[...])
            dqg = dqg + dg
            dproj_ref[:, o_q + 128 * a:o_q + 128 * (a + 1)] = draw.astype(BF16)
        dqg_ref[...] += dqg
        for hh in range(nkv):
            kv_ref[:, HEAD_DIM * hh:HEAD_DIM * (hh + 1)] = dk_ref[hh, :, 0:HEAD_DIM]
        dkg = jnp.zeros((1, 128), F32)
        for a in range(nkv // 2):
            sl = slice(128 * a, 128 * (a + 1))
            draw, dg = norm_rope_bwd(kv_ref[:, sl], kraw_ref[:, sl], kg_ref[...])
            dkg = dkg + dg
            dproj_ref[:, o_k + 128 * a:o_k + 128 * (a + 1)] = draw.astype(BF16)
        dkg_ref[...] += dkg
        for hh in range(nkv):
            kv_ref[:, HEAD_DIM * hh:HEAD_DIM * (hh + 1)] = dv_ref[hh]
        dproj_ref[:, o_v:o_g] = kv_ref[...].astype(BF16)
        dproj_ref[:, o_g:4 * d] = dgt_ref[...]
        dh = jnp.zeros((tm, d), F32)
        for blk in range(N_DEV):
            dh = dh + _dot_nt(dproj_ref[:, bw * blk:bw * (blk + 1)], w_ref[blk])
        x = x_ref[...]
        r = lax.rsqrt(jnp.mean(x * x, axis=-1, keepdims=True) + NORM_EPS)
        xh0 = x * r
        dgm_ref[...] += jnp.sum(dh * xh0, axis=0, keepdims=True)
        dxh = dh * g_ref[...]
        dx0_ref[...] = dx1_ref[...] + r * (dxh - xh0 * jnp.mean(dxh * xh0, axis=-1, keepdims=True))

    return _pcall(
        body, name="in_proj_bwd", grid=(t // tm,),
        in_specs=[_row_spec(tm, d), _row_spec(tm, d), _row_spec(tm, du), _row_spec(tm, du), _row_spec(tm, du),
                  _row_spec(tm, d), _row_spec(tm, dkw), _row_spec(tm, d),
                  pl.BlockSpec((nkv, tm, 128), lambda i: (0, i, 0)), _heads_spec(nkv, tm),
                  _row_spec(tm, 2 * d), _full_spec((1, d)), _ANY, _full_spec((1, 128)), _full_spec((1, 128)),
                  _row_spec(tm, 128), _row_spec(tm, 128)],
        out_specs=[_row_spec(tm, d), _row_spec(tm, 4 * d), _full_spec((1, d)), _full_spec((1, 128)),
                   _full_spec((1, 128))],
        out_shape=[jax.ShapeDtypeStruct((t, d), F32), jax.ShapeDtypeStruct((t, 4 * d), BF16),
                   jax.ShapeDtypeStruct((1, d), F32), jax.ShapeDtypeStruct((1, 128), F32),
                   jax.ShapeDtypeStruct((1, 128), F32)],
        scratch=[pltpu.VMEM((N_DEV, d, bw), BF16), pltpu.VMEM((tm, dkw), F32), pltpu.SemaphoreType.DMA((1,))],
        vmem=V7X_VMEM_LIMIT,
    )(x0, dx1, dud, duf, dub, qraw, kraw, dq, dk, dv, dgates, g_mix, w_in, qg, kg, cos, sin)


def _attn_fwd(qat, ka, vta):
    nh, _, t = qat.shape
    nkv = ka.shape[0]
    rep = nh // nkv
    hd = HEAD_DIM
    vr = vta.shape[1]
    tq = tk = KV_TILE

    def body(qt_ref, k_ref, vt_ref, ot_ref, lse_ref, m_scr, acc_scr):
        j = pl.program_id(2)

        @pl.when(j == 0)
        def _():
            m_scr[...] = jnp.full(m_scr.shape, NEG_BIG, F32)
            acc_scr[...] = jnp.zeros_like(acc_scr)

        k = k_ref[0]
        vt = vt_ref[0]
        for r in range(rep):
            st = _dot(k, qt_ref[r])
            m_prev = m_scr[r]
            m_next = jnp.maximum(m_prev, jnp.max(st, axis=0, keepdims=True))
            pt = jnp.exp(st - m_next).astype(BF16)
            acc_scr[r] = jnp.exp(m_prev - m_next) * acc_scr[r] + _dot(vt, pt)
            m_scr[r] = m_next

        @pl.when(j == pl.num_programs(2) - 1)
        def _():
            for r in range(rep):
                l = acc_scr[r, hd:hd + 1, :]
                ot_ref[hd * r:hd * (r + 1), :] = acc_scr[r, 0:hd, :] / l
                lse_ref[0, r:r + 1, :] = m_scr[r] + jnp.log(l)

    return _pcall(
        body, name="attn_fwd", grid=(nkv, t // tq, t // tk),
        in_specs=[pl.BlockSpec((rep, 128, tq), lambda g, i, j: (g, 0, i)),
                  pl.BlockSpec((1, tk, 128), lambda g, i, j: (g, j, 0)),
                  pl.BlockSpec((1, vr, tk), lambda g, i, j: (g, 0, j))],
        out_specs=[pl.BlockSpec((rep * hd, tq), lambda g, i, j: (g, i)),
                   pl.BlockSpec((1, rep, tq), lambda g, i, j: (g, 0, i))],
        out_shape=[jax.ShapeDtypeStruct((nh * hd, t), F32), jax.ShapeDtypeStruct((nkv, rep, t), F32)],
        scratch=[pltpu.VMEM((rep, 1, tq), F32), pltpu.VMEM((rep, vr, tq), F32)],
        vmem=V7X_VMEM_LIMIT,
    )(qat, ka, vta)


def _attn_bwd(qat, ka, kt, va, dot, ot, lse_row):
    nh, _, t = qat.shape
    nkv = ka.shape[0]
    rep = nh // nkv
    hd = HEAD_DIM
    tq = tk = KV_TILE

    def body(qt_ref, k_ref, kt_ref, v_ref, dot_ref, ot_ref, lse_ref, dk_ref, dv_ref, dqt_ref):
        j = pl.program_id(1)
        i = pl.program_id(2)

        @pl.when(jnp.logical_and(j == 0, i == 0))
        def _():
            dqt_ref[...] = jnp.zeros_like(dqt_ref)

        @pl.when(i == 0)
        def _():
            dk_ref[...] = jnp.zeros_like(dk_ref)
            dv_ref[...] = jnp.zeros_like(dv_ref)

        k = k_ref[0]
        kt = kt_ref[0]
        v = v_ref[0, :, 0:hd]
        cols = pl.ds(pl.multiple_of(i * tq, tq), tq)
        dk = jnp.zeros((tk, 128), F32)
        dv = jnp.zeros((tk, hd), F32)
        for r in range(rep):
            heads = slice(hd * r, hd * (r + 1))
            qt = qt_ref[r]
            dot_r = dot_ref[heads, :]
            delta = jnp.sum(dot_r.astype(F32) * ot_ref[heads, :], axis=0, keepdims=True)
            pt = jnp.exp(_dot(k, qt) - lse_ref[0, r:r + 1, :])
            dst = (pt * (_dot(v, dot_r) - delta)).astype(BF16)
            dv = dv + _dot_nt(pt.astype(BF16), dot_r)
            dk = dk + _dot_nt(dst, qt)
            dqt_ref[heads, cols] += _dot(kt, dst)
        dk_ref[0] += dk
        dv_ref[0] += dv

    return _pcall(
        body, name="attn_bwd", grid=(nkv, t // tk, t // tq),
        in_specs=[pl.BlockSpec((rep, 128, tq), lambda g, j, i: (g, 0, i)),
                  pl.BlockSpec((1, tk, 128), lambda g, j, i: (g, j, 0)),
                  pl.BlockSpec((1, hd, tk), lambda g, j, i: (g, 0, j)),
                  pl.BlockSpec((1, tk, 128), lambda g, j, i: (g, j, 0)),
                  pl.BlockSpec((rep * hd, tq), lambda g, j, i: (g, i)),
                  pl.BlockSpec((rep * hd, tq), lambda g, j, i: (g, i)),
                  pl.BlockSpec((1, rep, tq), lambda g, j, i: (g, 0, i))],
        out_specs=[pl.BlockSpec((1, tk, 128), lambda g, j, i: (g, j, 0)),
                   pl.BlockSpec((1, tk, hd), lambda g, j, i: (g, j, 0)),
                   pl.BlockSpec((rep * hd, t), lambda g, j, i: (g, 0))],
        out_shape=[jax.ShapeDtypeStruct((nkv, t, 128), F32), jax.ShapeDtypeStruct((nkv, t, hd), F32),
                   jax.ShapeDtypeStruct((nh * hd, t), F32)],
        vmem=V7X_VMEM_LIMIT,
    )(qat, ka, kt, va, dot, ot, lse_row)


def _ssm_scan_fwd(ub, lam_re, lam_im, bb_re, bb_im, cc_re, cc_im):
    t, w = ub.shape
    gn = lam_re.shape[-1]
    tc = ROW_TILE
    cl = min(gn, SCAN_LANES)
    nblk = t // tc

    def body(u_ref, lr_ref, li_ref, br_ref, bi_ref, cr_ref, ci_ref, y_ref, xr_ref, xi_ref,
             bur_scr, bui_scr, cr_scr, ci_scr):
        descending = pl.program_id(0) == 1

        @pl.when(pl.program_id(1) == 0)
        def _():
            cr_scr[...] = jnp.zeros_like(cr_scr)
            ci_scr[...] = jnp.zeros_like(ci_scr)

        ub_t = u_ref[...]
        bur_scr[...] = _dot(ub_t, br_ref[0])
        bui_scr[...] = _dot(ub_t, bi_ref[0])
        for c0 in range(0, gn, cl):
            lanes = pl.ds(c0, cl)
            lr = lr_ref[0, :, lanes]
            li = li_ref[0, :, lanes]

            def step(k, carry, lanes=lanes, lr=lr, li=li):
                xr, xi = carry
                r = jnp.where(descending, tc - 1 - k, k)
                nr = (lr * xr - li * xi) + bur_scr[pl.ds(r, 1), lanes]
                ni = (lr * xi + li * xr) + bui_scr[pl.ds(r, 1), lanes]
                xr_ref[0, pl.ds(r, 1), lanes] = nr
                xi_ref[0, pl.ds(r, 1), lanes] = ni
                return nr, ni

            xr, xi = lax.fori_loop(0, tc, step, (cr_scr[:, lanes], ci_scr[:, lanes]), unroll=8)
            cr_scr[:, lanes] = xr
            ci_scr[:, lanes] = xi
        y_ref[0] = _dot(xr_ref[0].astype(BF16), cr_ref[0]) - _dot(xi_ref[0].astype(BF16), ci_ref[0])

    blk = lambda dd, i: jnp.where(dd == 0, i, nblk - 1 - i)
    row = lambda width: pl.BlockSpec((1, tc, width), lambda dd, i: (dd, blk(dd, i), 0))
    per_dir = lambda a, b: pl.BlockSpec((1, a, b), lambda dd, i: (dd, 0, 0))
    return _pcall(
        body, name="ssm_scan_fwd", grid=(2, nblk),
        in_specs=[pl.BlockSpec((tc, w), lambda dd, i: (blk(dd, i), 0)), per_dir(1, gn), per_dir(1, gn),
                  per_dir(w, gn), per_dir(w, gn), per_dir(gn, w), per_dir(gn, w)],
        out_specs=[row(w), row(gn), row(gn)],
        out_shape=[jax.ShapeDtypeStruct((2, t, w), F32), jax.ShapeDtypeStruct((2, t, gn), F32),
                   jax.ShapeDtypeStruct((2, t, gn), F32)],
        scratch=[pltpu.VMEM((tc, gn), F32), pltpu.VMEM((tc, gn), F32), pltpu.VMEM((1, gn), F32),
                 pltpu.VMEM((1, gn), F32)],
        vmem=V7X_VMEM_LIMIT,
    )(ub, lam_re, lam_im, bb_re, bb_im, cc_re, cc_im)


def _ssm_scan_bwd(dyb, xs_re, xs_im, lam_re, lam_im, cct_re, cct_im, bbt_re, bbt_im):
    t, w = dyb.shape
    gn = lam_re.shape[-1]
    tc = ROW_TILE
    cl = min(gn, SCAN_LANES)
    nblk = t // tc

    def body(dy_ref, xr_ref, xi_ref, hr_ref, hi_ref, lr_ref, li_ref, ctr_ref, cti_ref, btr_ref, bti_ref,
             du_ref, gr_ref, gi_ref, dlr_ref, dli_ref,
             gxr_scr, gxi_scr, cr_scr, ci_scr, ar_scr, ai_scr):
        i = pl.program_id(1)
        ascending = pl.program_id(0) == 1

        @pl.when(i == 0)
        def _():
            cr_scr[...] = jnp.zeros_like(cr_scr)
            ci_scr[...] = jnp.zeros_like(ci_scr)
            ar_scr[...] = jnp.zeros_like(ar_scr)
            ai_scr[...] = jnp.zeros_like(ai_scr)

        dyb = dy_ref[...]
        gxr_scr[...] = _dot(dyb, ctr_ref[0])
        gxi_scr[...] = -_dot(dyb, cti_ref[0])
        first_block = i == nblk - 1
        edge_row = lax.broadcasted_iota(jnp.int32, (tc, 1), 0) == jnp.where(ascending, tc - 1, 0)
        halo_row = jnp.where(ascending, 0, 7)
        shift = jnp.where(ascending, tc - 1, 1)
        for c0 in range(0, gn, cl):
            lanes = pl.ds(c0, cl)
            lr = lr_ref[0, :, lanes]
            li = li_ref[0, :, lanes]

            def step(k, carry, lanes=lanes, lr=lr, li=li):
                gr, gi = carry
                r = jnp.where(ascending, k, tc - 1 - k)
                ngr = gxr_scr[pl.ds(r, 1), lanes] + (lr * gr + li * gi)
                ngi = gxi_scr[pl.ds(r, 1), lanes] + (lr * gi - li * gr)
                gxr_scr[pl.ds(r, 1), lanes] = ngr
                gxi_scr[pl.ds(r, 1), lanes] = ngi
                return ngr, ngi

            gr, gi = lax.fori_loop(0, tc, step, (cr_scr[:, lanes], ci_scr[:, lanes]), unroll=8)
            cr_scr[:, lanes] = gr
            ci_scr[:, lanes] = gi
        for c0 in range(0, gn, 512):
            lanes = pl.ds(c0, 512)
            halo_r = jnp.where(first_block, 0.0, hr_ref[0, pl.ds(halo_row, 1), lanes])
            halo_i = jnp.where(first_block, 0.0, hi_ref[0, pl.ds(halo_row, 1), lanes])
            xpr = jnp.where(edge_row, halo_r, pltpu.roll(xr_ref[0, :, lanes], shift, 0))
            xpi = jnp.where(edge_row, halo_i, pltpu.roll(xi_ref[0, :, lanes], shift, 0))
            gr = gxr_scr[:, lanes]
            gi = gxi_scr[:, lanes]
            ar_scr[:, lanes] += jnp.sum(gr * xpr + gi * xpi, axis=0, keepdims=True)
            ai_scr[:, lanes] += jnp.sum(gi * xpr - gr * xpi, axis=0, keepdims=True)
        grb = gxr_scr[...].astype(BF16)
        gib = gxi_scr[...].astype(BF16)
        gr_ref[0] = grb
        gi_ref[0] = gib
        du_ref[0] = _dot(grb, btr_ref[0]) + _dot(gib, bti_ref[0])
        dlr_ref[0] = ar_scr[...]
        dli_ref[0] = ai_scr[...]

    blk = lambda dd, i: jnp.where(dd == 0, nblk - 1 - i, i)
    rev = lambda width: pl.BlockSpec((1, tc, width), lambda dd, i: (dd, blk(dd, i), 0))
    halo_blk = lambda dd, i: jnp.where(dd == 0, jnp.maximum(blk(dd, i) * (tc // 8) - 1, 0),
                                       jnp.minimum((blk(dd, i) + 1) * (tc // 8), t // 8 - 1))
    halo = pl.BlockSpec((1, 8, gn), lambda dd, i: (dd, halo_blk(dd, i), 0))
    per_dir = lambda a, b: pl.BlockSpec((1, a, b), lambda dd, i: (dd, 0, 0))
    return _pcall(
        body, name="ssm_scan_bwd", grid=(2, nblk),
        in_specs=[pl.BlockSpec((tc, w), lambda dd, i: (blk(dd, i), 0)), rev(gn), rev(gn), halo, halo,
                  per_dir(1, gn), per_dir(1, gn), per_dir(w, gn), per_dir(w, gn), per_dir(gn, w), per_dir(gn, w)],
        out_specs=[rev(w), rev(gn), rev(gn), per_dir(1, gn), per_dir(1, gn)],
        out_shape=[jax.ShapeDtypeStruct((2, t, w), F32), jax.ShapeDtypeStruct((2, t, gn), BF16),
                   jax.ShapeDtypeStruct((2, t, gn), BF16), jax.ShapeDtypeStruct((2, 1, gn), F32),
                   jax.ShapeDtypeStruct((2, 1, gn), F32)],
        scratch=[pltpu.VMEM((tc, gn), F32), pltpu.VMEM((tc, gn), F32), pltpu.VMEM((1, gn), F32),
                 pltpu.VMEM((1, gn), F32), pltpu.VMEM((1, gn), F32), pltpu.VMEM((1, gn), F32)],
        vmem=V7X_VMEM_LIMIT,
    )(dyb, xs_re, xs_im, xs_re, xs_im, lam_re, lam_im, cct_re, cct_im, bbt_re, bbt_im)


def _matmul_tn(a, b, name, a_is_transposed=False):
    nb, t, n = b.shape
    m = a.shape[1] if a_is_transposed else a.shape[2]
    shared = a.shape[0] == 1
    bm, bn, tk = min(m, 1024), min(n, 1024), KV_TILE

    def body(a_ref, b_ref, o_ref):
        @pl.when(pl.program_id(3) == 0)
        def _():
            o_ref[...] = jnp.zeros_like(o_ref)

        mul = _dot if a_is_transposed else _dot_tn
        o_ref[0] += mul(a_ref[0].astype(BF16), b_ref[0].astype(BF16))

    a_spec = (pl.BlockSpec((1, bm, tk), lambda z, i, j, k: (0 if shared else z, i, k)) if a_is_transposed else
              pl.BlockSpec((1, tk, bm), lambda z, i, j, k: (0 if shared else z, k, i)))
    return _pcall(
        body, name=name, grid=(nb, m // bm, n // bn, t // tk),
        in_specs=[a_spec,
                  pl.BlockSpec((1, tk, bn), lambda z, i, j, k: (z, k, j))],
        out_specs=pl.BlockSpec((1, bm, bn), lambda z, i, j, k: (z, i, j)),
        out_shape=jax.ShapeDtypeStruct((nb, m, n), F32), vmem=V7X_VMEM_LIMIT,
    )(a, b)


def _reduce_adamw(gparts, p, m, v, name):
    rows, width = p.shape
    tr = max(k for k in range(16, 513, 16) if rows % k == 0)

    def body(g_ref, p_ref, m_ref, v_ref, go_ref, d_ref, mo_ref, vo_ref):
        g = g_ref[0].astype(F32)
        for k in range(1, N_DEV):
            g = g + g_ref[k].astype(F32)
        go_ref[...] = g
        mm = ADAM_B1 * m_ref[...] + (1.0 - ADAM_B1) * g
        vv = ADAM_B2 * v_ref[...] + (1.0 - ADAM_B2) * (g * g)
        m_hat = mm / (1.0 - ADAM_B1 ** ADAM_STEP)
        v_hat = vv / (1.0 - ADAM_B2 ** ADAM_STEP)
        d_ref[...] = -ADAM_LR * (m_hat / (jnp.sqrt(v_hat) + ADAM_EPS) + ADAM_WD * p_ref[...])
        mo_ref[...] = mm
        vo_ref[...] = vv

    spec = pl.BlockSpec((tr, width), lambda i: (i, 0))
    out = jax.ShapeDtypeStruct((rows, width), F32)
    return _pcall(
        body, name=name, grid=(rows // tr,),
        in_specs=[pl.BlockSpec((N_DEV, tr, width), lambda i: (0, i, 0)), spec, spec, spec],
        out_specs=[spec, spec, spec, spec], out_shape=[out, out, out, out], vmem=V7X_VMEM_LIMIT,
    )(gparts, p, m, v)


def _peer(k):
    x, y, c = lax.axis_index("x"), lax.axis_index("y"), lax.axis_index("c")
    return (x ^ ((k >> 2) & 1), y ^ ((k >> 1) & 1), c ^ (k & 1))


def _my_index():
    return 4 * lax.axis_index("x") + 2 * lax.axis_index("y") + lax.axis_index("c")


def _exchange(x, name, scatter):
    shape = x.shape[1:] if scatter else x.shape
    out_shape = (N_DEV,) + tuple(shape)

    def body(x_ref, out_ref, send_sems, recv_sems, local_sem):
        me = _my_index()
        mine = x_ref.at[me] if scatter else x_ref
        local = pltpu.make_async_copy(mine, out_ref.at[me], local_sem)
        local.start()
        copies = []
        for k in range(1, N_DEV):
            peer = _peer(k)
            peer_index = 4 * peer[0] + 2 * peer[1] + peer[2]
            src = x_ref.at[peer_index] if scatter else x_ref
            cp = pltpu.make_async_remote_copy(src_ref=src, dst_ref=out_ref.at[me], send_sem=send_sems.at[k - 1],
                                              recv_sem=recv_sems.at[k - 1], device_id=peer,
                                              device_id_type=pl.DeviceIdType.MESH)
            cp.start()
            copies.append(cp)
        for cp in copies:
            cp.wait_recv()
        for cp in copies:
            cp.wait_send()
        local.wait()

    return pl.pallas_call(
        body, name=name, in_specs=[_ANY], out_specs=_ANY, out_shape=jax.ShapeDtypeStruct(out_shape, x.dtype),
        scratch_shapes=[pltpu.SemaphoreType.DMA((N_DEV - 1,)), pltpu.SemaphoreType.DMA((N_DEV - 1,)),
                        pltpu.SemaphoreType.DMA(())],
    )(x)


def _to_shards(full, axis):
    r, c = full.shape
    if axis == 0:
        return full.reshape(N_DEV, r // N_DEV, c)
    return full.reshape(r, N_DEV, c // N_DEV).transpose(1, 0, 2)


def _from_shards(shards, axis):
    _, r, c = shards.shape
    if axis == 0:
        return shards.reshape(N_DEV * r, c)
    return shards.transpose(1, 0, 2).reshape(r, N_DEV * c)


def _pack_rows(parts, lead):
    flat = []
    for p in parts:
        p = p.reshape(p.shape[:lead] + (-1, PACK_W))
        pad = _round_up(p.shape[lead], 16) - p.shape[lead]
        flat.append(jnp.pad(p, [(0, 0)] * lead + [(0, pad), (0, 0)]) if pad else p)
    return jnp.concatenate(flat, axis=lead)


def _unpack_rows(packed, shapes):
    lead = packed.shape[:-2]
    out, off = [], 0
    for shp in shapes:
        rows = math.prod(shp) // PACK_W
        out.append(packed[..., off:off + rows, :].reshape(lead + tuple(shp)))
        off += _round_up(rows, 16)
    return out


def _pack_flat(parts):
    flat = jnp.concatenate([p.reshape(-1) for p in parts])
    n = flat.shape[0]
    flat = jnp.pad(flat, (0, _round_up(n, 16 * PACK_W) - n))
    return flat.reshape(-1, PACK_W)


def _unpack(packed, shapes):
    flat = packed.reshape(-1)
    out, off = [], 0
    for shp in shapes:
        n = math.prod(shp)
        out.append(flat[off:off + n].reshape(shp))
        off += n
    return out


def _ssm_discretize(a_re, a_im, log_dt, b_re, b_im):
    dt = jnp.exp(log_dt)[..., None]
    lam_re = jnp.minimum(a_re, EIG_RE_MAX)
    lam_im = a_im
    mag = jnp.exp(lam_re * dt)
    ang = lam_im * dt
    lb_re = mag * jnp.cos(ang)
    lb_im = mag * jnp.sin(ang)
    num_re = lb_re - 1.0
    num_im = lb_im
    den = lam_re * lam_re + lam_im * lam_im
    f_re = (num_re * lam_re + num_im * lam_im) / den
    f_im = (num_im * lam_re - num_re * lam_im) / den
    bb_re = f_re[..., None] * b_re - f_im[..., None] * b_im
    bb_im = f_re[..., None] * b_im + f_im[..., None] * b_re
    return lb_re, lb_im, bb_re, bb_im


def _block_diag(blocks):
    two, g, a, b = blocks.shape
    eye = jnp.eye(g, dtype=blocks.dtype)
    return (blocks[:, :, :, None, :] * eye[None, :, None, :, None]).reshape(two, g * a, g * b)


def _diag_blocks(dense, g):
    two, ga, gb = dense.shape
    return jnp.einsum("zgagb->zgab", dense.reshape(two, g, ga // g, g, gb // g))


def _rope_tables(t, n_valid):
    pos = jnp.arange(t)
    real = jnp.logical_and(pos >= N_META, pos < n_valid)
    idx = jnp.where(real, pos - N_META, 0)
    row_id = (idx // GRID_W).astype(F32)
    col_id = (idx % GRID_W).astype(F32)
    pairs_per_axis = HEAD_DIM // 4
    inv_freq = ROPE_THETA ** (-jnp.arange(pairs_per_axis, dtype=F32) / pairs_per_axis)
    ang = jnp.concatenate([row_id[:, None] * inv_freq, col_id[:, None] * inv_freq], axis=-1)
    ang = jnp.where(real[:, None], ang, 0.0)
    cos = jnp.repeat(jnp.cos(ang), 2, axis=-1)
    sin = jnp.sin(ang)
    sin = jnp.stack([-sin, sin], axis=-1).reshape(t, HEAD_DIM)
    return jnp.tile(cos, (1, 2)), jnp.tile(sin, (1, 2))


def _local_step(x, loss_target, big, small):
    s_len, d = x.shape
    n_valid = s_len + N_META
    t = _round_up(n_valid, KV_TILE)
    du = d // 2
    groups = du // SSM_GROUP
    nh = d // HEAD_DIM
    nkv = nh // KV_REP
    pad = t - n_valid

    x0 = jnp.concatenate([big["meta_tokens"].astype(F32), x, jnp.zeros((pad, d), F32)], axis=0)
    tgt = jnp.concatenate([jnp.zeros((N_META, d), F32), loss_target, jnp.zeros((pad, d), F32)], axis=0)
    cos, sin = _rope_tables(t, n_valid)
    g_mix = small["norm_mix_g"].reshape(1, d)
    g_mlp = small["norm_mlp_g"].reshape(1, d)
    g_fin = small["norm_final_g"].reshape(1, d)
    qg = jnp.tile(small["q_norm_g"].reshape(1, HEAD_DIM), (1, 2))
    kg = jnp.tile(small["k_norm_g"].reshape(1, HEAD_DIM), (1, 2))
    ssm_d = small["ssm_d"].reshape(1, du)
    b_glu = small["b_glu"].reshape(1, du)

    ssm_in = tuple(small[n][0] for n in ("ssm_a_re", "ssm_a_im", "ssm_log_dt", "ssm_b_re", "ssm_b_im"))
    (lb_re, lb_im, bbar_re, bbar_im), disc_vjp = jax.vjp(_ssm_discretize, *ssm_in)
    lam_re = lb_re.reshape(2, 1, groups * SSM_STATE)
    lam_im = lb_im.reshape(2, 1, groups * SSM_STATE)
    bb_re = _block_diag(bbar_re.transpose(0, 1, 3, 2)).astype(BF16)
    bb_im = _block_diag(bbar_im.transpose(0, 1, 3, 2)).astype(BF16)
    c_re, c_im = small["ssm_c_re"][0], small["ssm_c_im"][0]
    cct_re = _block_diag(c_re).astype(BF16)
    cct_im = _block_diag(c_im).astype(BF16)
    cc_re = cct_re.transpose(0, 2, 1)
    cc_im = cct_im.transpose(0, 2, 1)
    bbt_re = bb_re.transpose(0, 2, 1)
    bbt_im = bb_im.transpose(0, 2, 1)

    h, u, ub, qraw, kraw, qa, ka, va, gates = _in_proj_fwd(x0, g_mix, big["w_in"], qg, kg, cos, sin, n_valid)
    y2, xs_re, xs_im = _ssm_scan_fwd(ub, lam_re, lam_im, bb_re, bb_im, cc_re, cc_im)
    yf, yb = y2[0], y2[1]
    qat = qa.transpose(0, 2, 1)
    kt = ka[:, :, 0:HEAD_DIM].transpose(0, 2, 1)
    vta = va[:, :, 0:VT_ROWS].transpose(0, 2, 1)
    yt_attn, lse = _attn_fwd(qat, ka, vta)
    mixer_w = (ssm_d, big["w_glu"], b_glu, big["w_ssm_proj"], big["w_attn_proj"], big["w_out"])
    x1 = _mixer_out_fwd(x0, u, yf, yb, yt_attn, gates, *mixer_w)

    dx1, loss8, dg_fin, dg_mlp, h2b, dab, hsqb, dx2b = _mlp_loss_fwd_bwd(
        x1, tgt, g_mlp, g_fin, big["w_mlp_in"], big["w_mlp_out"], n_valid)

    (dyb, dud, dyt_attn, dgates, zb, dglb, ysb, dasb, daab, mgb, dxb, d_ssm_d, d_b_glu) = _mixer_out_bwd(
        dx1, u, yf, yb, yt_attn, gates, *mixer_w)
    dk, dv, dqt = _attn_bwd(qat, ka, kt, va, dyt_attn, yt_attn, lse)
    du2, g_re, g_im, dlam_re, dlam_im = _ssm_scan_bwd(dyb, xs_re, xs_im, lam_re, lam_im, cct_re, cct_im,
                                                      bbt_re, bbt_im)
    dx0, dproj, dg_mix, dqg, dkg = _in_proj_bwd(x0, dx1, dud, du2[0], du2[1], qraw, kraw, dqt.T, dk, dv, dgates,
                                                g_mix, big["w_in"], qg, kg, cos, sin)

    one = lambda a: a[None]
    grads = {}
    u1, dy1 = one(ub), one(dyb)
    grads["w_in"] = _matmul_tn(one(h), one(dproj), "grad_w_in")[0]
    grads["w_glu"] = _matmul_tn(one(zb), one(dglb), "grad_w_glu")[0]
    grads["w_ssm_proj"] = _matmul_tn(one(ysb), one(dasb), "grad_w_ssm_proj")[0]
    grads["w_attn_proj"] = _matmul_tn(one(yt_attn), one(daab), "grad_w_attn_proj", a_is_transposed=True)[0]
    grads["w_out"] = _matmul_tn(one(mgb), one(dxb), "grad_w_out")[0]
    grads["w_mlp_in"] = _matmul_tn(one(h2b), one(dab), "grad_w_mlp_in")[0]
    grads["w_mlp_out"] = _matmul_tn(one(hsqb), one(dx2b), "grad_w_mlp_out")[0]
    grads["meta_tokens"] = dx0[0:N_META]
    dbb_re = _diag_blocks(_matmul_tn(u1, g_re, "grad_ssm_bbar_re"), groups).transpose(0, 1, 3, 2)
    dbb_im = _diag_blocks(_matmul_tn(u1, g_im, "grad_ssm_bbar_im"), groups).transpose(0, 1, 3, 2)
    dc_re = _diag_blocks(_matmul_tn(dy1, xs_re, "grad_ssm_c_re"), groups)
    dc_im = -_diag_blocks(_matmul_tn(dy1, xs_im, "grad_ssm_c_im"), groups)
    shape_gn = (2, groups, SSM_STATE)
    d_a_re, d_a_im, d_log_dt, d_b_re, d_b_im = disc_vjp(
        (dlam_re.reshape(shape_gn), dlam_im.reshape(shape_gn), dbb_re, dbb_im))
    grads.update({
        "norm_mix_g": dg_mix, "ssm_a_re": d_a_re[None], "ssm_a_im": d_a_im[None], "ssm_log_dt": d_log_dt[None],
        "ssm_b_re": d_b_re[None], "ssm_b_im": d_b_im[None], "ssm_c_re": dc_re[None], "ssm_c_im": dc_im[None],
        "ssm_d": d_ssm_d, "b_glu": d_b_glu,
        "q_norm_g": dqg[:, 0:HEAD_DIM] + dqg[:, HEAD_DIM:128], "k_norm_g": dkg[:, 0:HEAD_DIM] + dkg[:, HEAD_DIM:128],
        "norm_mlp_g": dg_mlp, "norm_final_g": dg_fin.reshape(d),
    })
    return loss8[0, 0], dx0[N_META:n_valid], grads


def kernel(x, meta_tokens, norm_mix_g, w_in, ssm_a_re, ssm_a_im, ssm_log_dt, ssm_b_re, ssm_b_im, ssm_c_re, ssm_c_im, ssm_d, w_glu, b_glu, q_norm_g, k_norm_g, w_ssm_proj, w_attn_proj, w_out, norm_mlp_g, w_mlp_in, w_mlp_out, norm_final_g, loss_target, m_meta_tokens, m_norm_mix_g, m_w_in, m_ssm_a_re, m_ssm_a_im, m_ssm_log_dt, m_ssm_b_re, m_ssm_b_im, m_ssm_c_re, m_ssm_c_im, m_ssm_d, m_w_glu, m_b_glu, m_q_norm_g, m_k_norm_g, m_w_ssm_proj, m_w_attn_proj, m_w_out, m_norm_mlp_g, m_w_mlp_in, m_w_mlp_out, m_norm_final_g, v_meta_tokens, v_norm_mix_g, v_w_in, v_ssm_a_re, v_ssm_a_im, v_ssm_log_dt, v_ssm_b_re, v_ssm_b_im, v_ssm_c_re, v_ssm_c_im, v_ssm_d, v_w_glu, v_b_glu, v_q_norm_g, v_k_norm_g, v_w_ssm_proj, v_w_attn_proj, v_w_out, v_norm_mlp_g, v_w_mlp_in, v_w_mlp_out, v_norm_final_g):
    w = dict(meta_tokens=meta_tokens, norm_mix_g=norm_mix_g, w_in=w_in, ssm_a_re=ssm_a_re, ssm_a_im=ssm_a_im, ssm_log_dt=ssm_log_dt, ssm_b_re=ssm_b_re, ssm_b_im=ssm_b_im, ssm_c_re=ssm_c_re, ssm_c_im=ssm_c_im, ssm_d=ssm_d, w_glu=w_glu, b_glu=b_glu, q_norm_g=q_norm_g, k_norm_g=k_norm_g, w_ssm_proj=w_ssm_proj, w_attn_proj=w_attn_proj, w_out=w_out, norm_mlp_g=norm_mlp_g, w_mlp_in=w_mlp_in, w_mlp_out=w_mlp_out, norm_final_g=norm_final_g)
    m = dict(meta_tokens=m_meta_tokens, norm_mix_g=m_norm_mix_g, w_in=m_w_in, ssm_a_re=m_ssm_a_re, ssm_a_im=m_ssm_a_im, ssm_log_dt=m_ssm_log_dt, ssm_b_re=m_ssm_b_re, ssm_b_im=m_ssm_b_im, ssm_c_re=m_ssm_c_re, ssm_c_im=m_ssm_c_im, ssm_d=m_ssm_d, w_glu=m_w_glu, b_glu=m_b_glu, q_norm_g=m_q_norm_g, k_norm_g=m_k_norm_g, w_ssm_proj=m_w_ssm_proj, w_attn_proj=m_w_attn_proj, w_out=m_w_out, norm_mlp_g=m_norm_mlp_g, w_mlp_in=m_w_mlp_in, w_mlp_out=m_w_mlp_out, norm_final_g=m_norm_final_g)
    v = dict(meta_tokens=v_meta_tokens, norm_mix_g=v_norm_mix_g, w_in=v_w_in, ssm_a_re=v_ssm_a_re, ssm_a_im=v_ssm_a_im, ssm_log_dt=v_ssm_log_dt, ssm_b_re=v_ssm_b_re, ssm_b_im=v_ssm_b_im, ssm_c_re=v_ssm_c_re, ssm_c_im=v_ssm_c_im, ssm_d=v_ssm_d, w_glu=v_w_glu, b_glu=v_b_glu, q_norm_g=v_q_norm_g, k_norm_g=v_k_norm_g, w_ssm_proj=v_w_ssm_proj, w_attn_proj=v_w_attn_proj, w_out=v_w_out, norm_mlp_g=v_norm_mlp_g, w_mlp_in=v_w_mlp_in, w_mlp_out=v_w_mlp_out, norm_final_g=v_norm_final_g)

    shard2d = {n: w[n].reshape(w[n].shape[-2:]) for n in BIG_WEIGHTS}
    big_shapes = [shard2d[n].shape for n in BIG_WEIGHTS]

    meta_hi = shard2d["meta_tokens"].astype(BF16)
    meta_res = shard2d["meta_tokens"] - meta_hi.astype(F32)
    meta_mid = meta_res.astype(BF16)
    meta_lo = (meta_res - meta_mid.astype(F32)).astype(BF16)
    to_gather = [meta_hi, meta_mid, meta_lo] + [shard2d[n].astype(BF16) for n in BIG_WEIGHTS[1:]]
    gather_shapes = [meta_hi.shape] * 3 + big_shapes[1:]
    gathered = _exchange(_pack_rows(to_gather, 0), "gather_weights", scatter=False)
    shards = _unpack_rows(gathered, gather_shapes)
    meta = [_from_shards(s, 1).astype(F32) for s in shards[0:3]]
    big = {"meta_tokens": (meta[0] + meta[1]) + meta[2]}
    for n, s in zip(BIG_WEIGHTS[1:], shards[3:]):
        big[n] = s if n in BLOCK_WEIGHTS else _from_shards(s, BIG_SHARD_AXIS[n])
    small = {n: w[n] for n in SMALL_WEIGHTS}

    loss, grad_x, grads = _local_step(x[0], loss_target[0], big, small)
    loss = lax.psum(loss, ("x", "y", "c"))

    g_blocks = _pack_rows([_to_shards(grads[n], BIG_SHARD_AXIS[n]) for n in BIG_WEIGHTS], 1).astype(BF16)
    g_parts = _exchange(g_blocks, "scatter_grads", scatter=True)
    pk = lambda src: _pack_rows([src[n].reshape(shard2d[n].shape) for n in BIG_WEIGHTS], 0)
    big_out = _reduce_adamw(g_parts, pk(w), pk(m), pk(v), "adamw_sharded")
    small_shapes = [w[n].shape for n in SMALL_WEIGHTS]
    s_parts = _exchange(_pack_flat([grads[n] for n in SMALL_WEIGHTS]), "gather_small_grads", scatter=False)
    pf = lambda src: _pack_flat([src[n] for n in SMALL_WEIGHTS])
    small_out = _reduce_adamw(s_parts, pf(w), pf(m), pf(v), "adamw_replicated")

    results = []
    for kind in range(4):
        big_un = dict(zip(BIG_WEIGHTS, _unpack_rows(big_out[kind], big_shapes)))
        small_un = dict(zip(SMALL_WEIGHTS, _unpack(small_out[kind], small_shapes)))
        for n in ALL_WEIGHTS:
            results.append(big_un[n].reshape(w[n].shape) if n in big_un else small_un[n])
    return (loss, grad_x[None], *results)
```

```python
import functools
import math

import jax
import jax.numpy as jnp
from jax import lax
from jax.experimental import pallas as pl
from jax.experimental.pallas import tpu as pltpu

F32 = jnp.float32
BF16 = jnp.bfloat16

N_DEV = 8
N_META = 16
GRID_W = 64
SSM_GROUP = 16
SSM_STATE = 64
HEAD_DIM = 64
KV_REP = 4
ROPE_THETA = 10000.0
NORM_EPS = 1e-6
EIG_RE_MAX = -1e-4
ATTN_SCALE = HEAD_DIM ** -0.5

ADAM_LR = 0.001
ADAM_B1 = 0.9
ADAM_B2 = 0.999
ADAM_EPS = 1e-08
ADAM_WD = 0.01
ADAM_STEP = 10

ROW_TILE = 384
ROW_TILE_BWD = 256
VT_ROWS = 80
MASK_BIAS = -1e30
SCAN_LANES = 1024
KV_TILE = 768
PACK_W = 1024
V7X_VMEM_LIMIT = 56 * 1024 * 1024
NEG_BIG = -1e30

BIG_WEIGHTS = ("meta_tokens", "w_in", "w_glu", "w_ssm_proj", "w_attn_proj", "w_out", "w_mlp_in", "w_mlp_out")
BIG_SHARD_AXIS = {"meta_tokens": 1, "w_in": 1, "w_glu": 0, "w_ssm_proj": 1, "w_attn_proj": 0, "w_out": 0,
                  "w_mlp_in": 1, "w_mlp_out": 0}
BLOCK_WEIGHTS = ("w_in", "w_mlp_in", "w_mlp_out")
EARLY_WEIGHTS = ("meta_tokens", "w_in")
LATE_WEIGHTS = ("w_glu", "w_ssm_proj", "w_attn_proj", "w_out", "w_mlp_in", "w_mlp_out")
SMALL_WEIGHTS = ("norm_mix_g", "ssm_a_re", "ssm_a_im", "ssm_log_dt", "ssm_b_re", "ssm_b_im", "ssm_c_re",
                 "ssm_c_im", "ssm_d", "b_glu", "q_norm_g", "k_norm_g", "norm_mlp_g", "norm_final_g")
ALL_WEIGHTS = ("meta_tokens", "norm_mix_g", "w_in", "ssm_a_re", "ssm_a_im", "ssm_log_dt", "ssm_b_re", "ssm_b_im",
               "ssm_c_re", "ssm_c_im", "ssm_d", "w_glu", "b_glu", "q_norm_g", "k_norm_g", "w_ssm_proj",
               "w_attn_proj", "w_out", "norm_mlp_g", "w_mlp_in", "w_mlp_out", "norm_final_g")


def _round_up(n, m):
    return (n + m - 1) // m * m


def _pcall(body, *, name, grid, in_specs, out_specs, out_shape, scratch=(), vmem=None, **kw):
    params = pltpu.CompilerParams(dimension_semantics=("arbitrary",) * len(grid), vmem_limit_bytes=vmem)
    return pl.pallas_call(body, name=name, grid=grid, in_specs=in_specs, out_specs=out_specs, out_shape=out_shape,
                          scratch_shapes=list(scratch), compiler_params=params, **kw)


def _dot(a, b):
    return jnp.dot(a, b, preferred_element_type=F32)


def _dot_nt(a, b):
    return lax.dot_general(a, b, (((1,), (1,)), ((), ())), preferred_element_type=F32)


def _dot_tn(a, b):
    return lax.dot_general(a, b, (((0,), (0,)), ((), ())), preferred_element_type=F32)


def _full_spec(shape):
    nd = len(shape)
    return pl.BlockSpec(shape, lambda *_: (0,) * nd)


def _row_spec(tm, width):
    return pl.BlockSpec((tm, width), lambda i: (i, 0))


def _heads_spec(nh, tm):
    return pl.BlockSpec((nh, tm, HEAD_DIM), lambda i: (0, i, 0))


_ANY = pl.BlockSpec(memory_space=pl.ANY)


def _load_once(step, pairs, sem):
    @pl.when(step == 0)
    def _():
        copies = [pltpu.make_async_copy(src, dst, sem.at[k]) for k, (src, dst) in enumerate(pairs)]
        for cp in copies:
            cp.start()
        for cp in copies:
            cp.wait()


def _swap_pairs(x, even):
    n = x.shape[-1]
    return jnp.where(even, pltpu.roll(x, n - 1, 1), pltpu.roll(x, 1, 1))


def _gelu(y):
    return 0.5 * y * (1.0 + lax.erf(y * (1.0 / math.sqrt(2.0))))


def _gelu_grad(y):
    return 0.5 * (1.0 + lax.erf(y * (1.0 / math.sqrt(2.0)))) + y * jnp.exp(-0.5 * y * y) * (1.0 / math.sqrt(2.0 * math.pi))


def _in_proj_fwd(x0, g_mix, w_in, qg, kg, cos, sin, n_valid):
    t, d = x0.shape
    tm = ROW_TILE
    du, dk = d // 2, d // 4
    nh, nkv = d // HEAD_DIM, d // HEAD_DIM // KV_REP
    bw = w_in.shape[-1]
    assert bw == du and dk * 2 == bw

    def body(x_ref, g_ref, w_hbm, qg_ref, kg_ref, c_ref, s_ref,
             h_ref, u_ref, ub_ref, qraw_ref, kraw_ref, qa_ref, ka_ref, va_ref, gates_ref,
             w_ref, sem):
        i = pl.program_id(0)
        _load_once(i, [(w_hbm, w_ref)], sem)
        x = x_ref[...]
        r = lax.rsqrt(jnp.mean(x * x, axis=-1, keepdims=True) + NORM_EPS)
        h = ((x * r) * g_ref[...]).astype(BF16)
        h_ref[...] = h
        u = _dot(h, w_ref[0])
        u_ref[...] = u
        ub_ref[...] = u.astype(BF16)
        lane = lax.broadcasted_iota(jnp.int32, (tm, 128), 1)
        lo = lane < HEAD_DIM
        even = (lane & 1) == 0
        aug = lane == HEAD_DIM
        c = c_ref[...]
        s = s_ref[...]
        row = i * tm + lax.broadcasted_iota(jnp.int32, (tm, 1), 0)
        one = jnp.where(aug, 1.0, 0.0)
        key_bias = jnp.where(jnp.logical_and(aug, row >= n_valid), MASK_BIAS, 0.0)

        def norm_rope(blk, g128):
            sq = blk * blk
            ms_lo = jnp.sum(jnp.where(lo, sq, 0.0), axis=-1, keepdims=True) * (1.0 / HEAD_DIM)
            ms_hi = jnp.sum(jnp.where(lo, 0.0, sq), axis=-1, keepdims=True) * (1.0 / HEAD_DIM)
            rr = jnp.where(lo, lax.rsqrt(ms_lo + NORM_EPS), lax.rsqrt(ms_hi + NORM_EPS))
            qn = (blk * rr) * g128
            return qn * c + _swap_pairs(qn, even) * s

        def put_heads(ref, first, pair, extra):
            ref[first] = jnp.where(lo, pair, extra).astype(BF16)
            ref[first + 1] = jnp.where(lo, pltpu.roll(pair, HEAD_DIM, 1), extra).astype(BF16)

        for blk in range(2):
            qb = _dot(h, w_ref[1 + blk])
            qraw_ref[:, bw * blk:bw * (blk + 1)] = qb
            for a in range(bw // 128):
                put_heads(qa_ref, (bw // HEAD_DIM) * blk + 2 * a,
                          norm_rope(qb[:, 128 * a:128 * (a + 1)], qg_ref[...]) * ATTN_SCALE, one)
        kv = _dot(h, w_ref[3])
        kraw_ref[...] = kv[:, 0:dk]
        for a in range(nkv // 2):
            put_heads(ka_ref, 2 * a, norm_rope(kv[:, 128 * a:128 * (a + 1)], kg_ref[...]), key_bias)
            put_heads(va_ref, 2 * a, kv[:, dk + 128 * a:dk + 128 * (a + 1)], one)
        for blk in range(4):
            gates_ref[:, bw * blk:bw * (blk + 1)] = _dot(h, w_ref[4 + blk])

    heads = lambda n: pl.BlockSpec((n, tm, 128), lambda i: (0, i, 0))
    return _pcall(
        body, name="in_proj_fwd", grid=(t // tm,),
        in_specs=[_row_spec(tm, d), _full_spec((1, d)), _ANY, _full_spec((1, 128)), _full_spec((1, 128)),
                  _row_spec(tm, 128), _row_spec(tm, 128)],
        out_specs=[_row_spec(tm, d), _row_spec(tm, du), _row_spec(tm, du), _row_spec(tm, d), _row_spec(tm, dk),
                   heads(nh), heads(nkv), heads(nkv), _row_spec(tm, 2 * d)],
        out_shape=[jax.ShapeDtypeStruct((t, d), BF16), jax.ShapeDtypeStruct((t, du), F32),
                   jax.ShapeDtypeStruct((t, du), BF16), jax.ShapeDtypeStruct((t, d), F32),
                   jax.ShapeDtypeStruct((t, dk), F32), jax.ShapeDtypeStruct((nh, t, 128), BF16),
                   jax.ShapeDtypeStruct((nkv, t, 128), BF16), jax.ShapeDtypeStruct((nkv, t, 128), BF16),
                   jax.ShapeDtypeStruct((t, 2 * d), F32)],
        scratch=[pltpu.VMEM((N_DEV, d, bw), BF16), pltpu.SemaphoreType.DMA((1,))],
        vmem=V7X_VMEM_LIMIT,
    )(x0, g_mix, w_in, qg, kg, cos, sin)


def _mixer_values(u, yf, yb, yt_attn, gates, d_ref, wg_ref, bg_ref, ps_ref, pa_ref, d):
    y = (u * d_ref[...] + yf) + yb
    z = _gelu(y)
    sg = jax.nn.sigmoid(_dot(z.astype(BF16), wg_ref[...]) + bg_ref[...])
    y_ssm = z * sg
    a_ssm = _dot(y_ssm.astype(BF16), ps_ref[...])
    a_attn = _dot_tn(yt_attn.astype(BF16), pa_ref[...])
    s_ssm = jax.nn.sigmoid(gates[:, 0:d])
    s_attn = jax.nn.sigmoid(gates[:, d:2 * d])
    merged = s_ssm * a_ssm + s_attn * a_attn
    return y, z, sg, y_ssm, a_ssm, a_attn, s_ssm, s_attn, merged


def _mixer_out_fwd(x0, u, yf, yb, y_attn, gates, ssm_d, w_glu, b_glu, p_ssm, p_attn, w_out):
    t, d = x0.shape
    tm = ROW_TILE
    du = d // 2

    def body(x_ref, u_ref, yf_ref, yb_ref, ya_ref, gt_ref, d_ref, wg_ref, bg_ref, ps_ref, pa_ref, wo_ref, x1_ref):
        vals = _mixer_values(u_ref[...], yf_ref[...], yb_ref[...], ya_ref[...], gt_ref[...],
                             d_ref, wg_ref, bg_ref, ps_ref, pa_ref, d)
        merged = vals[-1]
        x1_ref[...] = x_ref[...] + _dot(merged.astype(BF16), wo_ref[...])

    return _pcall(
        body, name="mixer_out_fwd", grid=(t // tm,),
        in_specs=[_row_spec(tm, d), _row_spec(tm, du), _row_spec(tm, du), _row_spec(tm, du),
                  pl.BlockSpec((d, tm), lambda i: (0, i)), _row_spec(tm, 2 * d), _full_spec((1, du)), _full_spec((du, du)), _full_spec((1, du)),
                  _full_spec((du, d)), _full_spec((d, d)), _full_spec((d, d))],
        out_specs=_row_spec(tm, d), out_shape=jax.ShapeDtypeStruct((t, d), F32), vmem=V7X_VMEM_LIMIT,
    )(x0, u, yf, yb, y_attn, gates, ssm_d, w_glu, b_glu, p_ssm, p_attn, w_out)


def _mlp_loss_fwd_bwd(x1, target, g_mlp, g_fin, w1, w2, n_valid):
    t, d = x1.shape
    tm = ROW_TILE_BWD
    dff = 4 * d
    nfc, _, fc = w1.shape

    def body(x_ref, tg_ref, gm_ref, gf_ref, w1_hbm, w2_hbm,
             dx1_ref, loss_ref, dgf_ref, dgm_ref, h2_ref, da_ref, hsq_ref, dx2b_ref,
             w1_ref, w2_ref, relu_ref, sem):
        i = pl.program_id(0)
        _load_once(i, [(w1_hbm, w1_ref), (w2_hbm, w2_ref)], sem)

        @pl.when(i == 0)
        def _():
            loss_ref[...] = jnp.zeros_like(loss_ref)
            dgf_ref[...] = jnp.zeros_like(dgf_ref)
            dgm_ref[...] = jnp.zeros_like(dgm_ref)

        x1v = x_ref[...]
        r1 = lax.rsqrt(jnp.mean(x1v * x1v, axis=-1, keepdims=True) + NORM_EPS)
        xh1 = x1v * r1
        h2b = (xh1 * gm_ref[...]).astype(BF16)
        h2_ref[...] = h2b
        acc = jnp.zeros((tm, d), F32)
        for c in range(nfc):
            a = jnp.maximum(_dot(h2b, w1_ref[c]), 0.0)
            relu_ref[:, fc * c:fc * (c + 1)] = a
            hs = (a * a).astype(BF16)
            hsq_ref[:, fc * c:fc * (c + 1)] = hs
            acc = acc + _dot(hs, w2_ref[c])
        x2 = x1v + acc
        r2 = lax.rsqrt(jnp.mean(x2 * x2, axis=-1, keepdims=True) + NORM_EPS)
        xh2 = x2 * r2
        out = xh2 * gf_ref[...]
        row = i * tm + lax.broadcasted_iota(jnp.int32, (tm, 1), 0)
        valid = jnp.logical_and(row >= N_META, row < n_valid)
        diff = jnp.where(valid, out - tg_ref[...], 0.0)
        loss_ref[...] += 0.5 * jnp.sum(jnp.sum(diff * diff, axis=-1, keepdims=True) * (1.0 / d))
        dout = diff * (1.0 / d)
        dgf_ref[...] += jnp.sum(dout * xh2, axis=0, keepdims=True)
        dxh2 = dout * gf_ref[...]
        dx2 = r2 * (dxh2 - xh2 * jnp.mean(dxh2 * xh2, axis=-1, keepdims=True))
        dx2b = dx2.astype(BF16)
        dx2b_ref[...] = dx2b
        dh2 = jnp.zeros((tm, d), F32)
        for c in range(nfc):
            dhs = _dot_nt(dx2b, w2_ref[c])
            da = (dhs * (2.0 * relu_ref[:, fc * c:fc * (c + 1)])).astype(BF16)
            da_ref[:, fc * c:fc * (c + 1)] = da
            dh2 = dh2 + _dot_nt(da, w1_ref[c])
        dgm_ref[...] += jnp.sum(dh2 * xh1, axis=0, keepdims=True)
        dxh1 = dh2 * gm_ref[...]
        dx1_ref[...] = dx2 + r1 * (dxh1 - xh1 * jnp.mean(dxh1 * xh1, axis=-1, keepdims=True))

    return _pcall(
        body, name="mlp_loss_fwd_bwd", grid=(t // tm,),
        in_specs=[_row_spec(tm, d), _row_spec(tm, d), _full_spec((1, d)), _full_spec((1, d)), _ANY, _ANY],
        out_specs=[_row_spec(tm, d), _full_spec((8, 128)), _full_spec((1, d)), _full_spec((1, d)),
                   _row_spec(tm, d), _row_spec(tm, dff), _row_spec(tm, dff), _row_spec(tm, d)],
        out_shape=[jax.ShapeDtypeStruct((t, d), F32), jax.ShapeDtypeStruct((8, 128), F32),
                   jax.ShapeDtypeStruct((1, d), F32), jax.ShapeDtypeStruct((1, d), F32),
                   jax.ShapeDtypeStruct((t, d), BF16), jax.ShapeDtypeStruct((t, dff), BF16),
                   jax.ShapeDtypeStruct((t, dff), BF16), jax.ShapeDtypeStruct((t, d), BF16)],
        scratch=[pltpu.VMEM((nfc, d, fc), BF16), pltpu.VMEM((nfc, fc, d), BF16), pltpu.VMEM((tm, dff), F32),
                 pltpu.SemaphoreType.DMA((2,))],
        vmem=V7X_VMEM_LIMIT,
    )(x1, target, g_mlp, g_fin, w1, w2)


def _mixer_out_bwd(dx1, u, yf, yb, yt_attn, gates, ssm_d, w_glu, b_glu, p_ssm, p_attn, w_out):
    t, d = dx1.shape
    tm = ROW_TILE_BWD
    du = d // 2

    def body(dx_ref, u_ref, yf_ref, yb_ref, yt_ref, gt_ref, d_ref, wg_ref, bg_ref, ps_ref, pa_ref, wo_ref,
             dyb_ref, dud_ref, dyat_ref, dgates_ref, zb_ref, dglb_ref, ysb_ref, dasb_ref, daab_ref,
             mgb_ref, dxb_ref, dd_ref, dbg_ref):
        i = pl.program_id(0)

        @pl.when(i == 0)
        def _():
            dd_ref[...] = jnp.zeros_like(dd_ref)
            dbg_ref[...] = jnp.zeros_like(dbg_ref)

        uv = u_ref[...]
        y, z, sg, y_ssm, a_ssm, a_attn, s_ssm, s_attn, merged = _mixer_values(
            uv, yf_ref[...], yb_ref[...], yt_ref[...], gt_ref[...], d_ref, wg_ref, bg_ref, ps_ref, pa_ref, d)
        dxb = dx_ref[...].astype(BF16)
        dxb_ref[...] = dxb
        mgb_ref[...] = merged.astype(BF16)
        dmerged = _dot_nt(dxb, wo_ref[...])
        dgates_ref[:, 0:d] = (dmerged * a_ssm * (s_ssm * (1.0 - s_ssm))).astype(BF16)
        dgates_ref[:, d:2 * d] = (dmerged * a_attn * (s_attn * (1.0 - s_attn))).astype(BF16)
        da_ssm = (dmerged * s_ssm).astype(BF16)
        da_attn = (dmerged * s_attn).astype(BF16)
        dasb_ref[...] = da_ssm
        daab_ref[...] = da_attn
        ysb_ref[...] = y_ssm.astype(BF16)
        dy_ssm = _dot_nt(da_ssm, ps_ref[...])
        dyat_ref[...] = _dot_nt(pa_ref[...], da_attn).astype(BF16)
        dgl = dy_ssm * z * (sg * (1.0 - sg))
        dglb = dgl.astype(BF16)
        dglb_ref[...] = dglb
        zb_ref[...] = z.astype(BF16)
        dbg_ref[...] += jnp.sum(dgl, axis=0, keepdims=True)
        dz = dy_ssm * sg + _dot_nt(dglb, wg_ref[...])
        dy = dz * _gelu_grad(y)
        dyb_ref[...] = dy.astype(BF16)
        dd_ref[...] += jnp.sum(dy * uv, axis=0, keepdims=True)
        dud_ref[...] = dy * d_ref[...]

    bf = lambda w: jax.ShapeDtypeStruct((t, w), BF16)
    return _pcall(
        body, name="mixer_out_bwd", grid=(t // tm,),
        in_specs=[_row_spec(tm, d), _row_spec(tm, du), _row_spec(tm, du), _row_spec(tm, du),
                  pl.BlockSpec((d, tm), lambda i: (0, i)),
                  _row_spec(tm, 2 * d), _full_spec((1, du)), _full_spec((du, du)), _full_spec((1, du)),
                  _full_spec((du, d)), _full_spec((d, d)), _full_spec((d, d))],
        out_specs=[_row_spec(tm, du), _row_spec(tm, du), pl.BlockSpec((d, tm), lambda i: (0, i)),
                   _row_spec(tm, 2 * d), _row_spec(tm, du), _row_spec(tm, du), _row_spec(tm, du), _row_spec(tm, d),
                   _row_spec(tm, d), _row_spec(tm, d), _row_spec(tm, d), _full_spec((1, du)), _full_spec((1, du))],
        out_shape=[bf(du), jax.ShapeDtypeStruct((t, du), F32), jax.ShapeDtypeStruct((d, t), BF16), bf(2 * d),
                   bf(du), bf(du), bf(du), bf(d), bf(d), bf(d), bf(d),
                   jax.ShapeDtypeStruct((1, du), F32), jax.ShapeDtypeStruct((1, du), F32)],
        vmem=V7X_VMEM_LIMIT,
    )(dx1, u, yf, yb, yt_attn, gates, ssm_d, w_glu, b_glu, p_ssm, p_attn, w_out)


def _in_proj_bwd(x0, dx1, dud, duf, dub, qraw, kraw, dq, dk, dv, dgates, g_mix, w_in, qg, kg, cos, sin):
    t, d = x0.shape
    tm = ROW_TILE_BWD
    du, dkw = d // 2, d // 4
    nh, nkv = d // HEAD_DIM, d // HEAD_DIM // KV_REP
    bw = w_in.shape[-1]
    o_q, o_k, o_v, o_g = du, du + d, du + d + dkw, 2 * d

    def body(x_ref, dx1_ref, dud_ref, duf_ref, dub_ref, qraw_ref, kraw_ref, dq_ref, dk_ref, dv_ref, dgt_ref,
             g_ref, w_hbm, qg_ref, kg_ref, c_ref, s_ref,
             dx0_ref, dproj_ref, dgm_ref, dqg_ref, dkg_ref,
             w_ref, kv_ref, sem):
        i = pl.program_id(0)
        _load_once(i, [(w_hbm, w_ref)], sem)

        @pl.when(i == 0)
        def _():
            dgm_ref[...] = jnp.zeros_like(dgm_ref)
            dqg_ref[...] = jnp.zeros_like(dqg_ref)
            dkg_ref[...] = jnp.zeros_like(dkg_ref)

        lane = lax.broadcasted_iota(jnp.int32, (tm, 128), 1)
        lo = lane < HEAD_DIM
        even = (lane & 1) == 0
        c = c_ref[...]
        s = s_ref[...]

        def norm_rope_bwd(dout, raw, g128):
            sq = raw * raw
            ms_lo = jnp.sum(jnp.where(lo, sq, 0.0), axis=-1, keepdims=True) * (1.0 / HEAD_DIM)
            ms_hi = jnp.sum(jnp.where(lo, 0.0, sq), axis=-1, keepdims=True) * (1.0 / HEAD_DIM)
            rr = jnp.where(lo, lax.rsqrt(ms_lo + NORM_EPS), lax.rsqrt(ms_hi + NORM_EPS))
            xh = raw * rr
            dqn = dout * c + _swap_pairs(dout * s, even)
            dg = jnp.sum(dqn * xh, axis=0, keepdims=True)
            tt = dqn * g128
            pr = tt * xh
            mu_lo = jnp.sum(jnp.where(lo, pr, 0.0), axis=-1, keepdims=True) * (1.0 / HEAD_DIM)
            mu_hi = jnp.sum(jnp.where(lo, 0.0, pr), axis=-1, keepdims=True) * (1.0 / HEAD_DIM)
            return rr * (tt - xh * jnp.where(lo, mu_lo, mu_hi)), dg

        dub_tot = (dud_ref[...] + duf_ref[...]) + dub_ref[...]
        dproj_ref[:, 0:du] = dub_tot.astype(BF16)
        dqg = jnp.zeros((1, 128), F32)
        for a in range(nh // 2):
            sl = slice(128 * a, 128 * (a + 1))
            draw, dg = norm_rope_bwd(dq_ref[:, sl] * ATTN_SCALE, qraw_ref[:, sl], qg_ref[...])
            dqg = dqg + dg
            dproj_ref[:, o_q + 128 * a:o_q + 128 * (a + 1)] = draw.astype(BF16)
        dqg_ref[...] += dqg
        for hh in range(nkv):
            kv_ref[:, HEAD_DIM * hh:HEAD_DIM * (hh + 1)] = dk_ref[hh, :, 0:HEAD_DIM]
        dkg = jnp.zeros((1, 128), F32)
        for a in range(nkv // 2):
            sl = slice(128 * a, 128 * (a + 1))
            draw, dg = norm_rope_bwd(kv_ref[:, sl], kraw_ref[:, sl], kg_ref[...])
            dkg = dkg + dg
            dproj_ref[:, o_k + 128 * a:o_k + 128 * (a + 1)] = draw.astype(BF16)
        dkg_ref[...] += dkg
        for hh in range(nkv):
            kv_ref[:, HEAD_DIM * hh:HEAD_DIM * (hh + 1)] = dv_ref[hh]
        dproj_ref[:, o_v:o_g] = kv_ref[...].astype(BF16)
        dproj_ref[:, o_g:4 * d] = dgt_ref[...]
        dh = jnp.zeros((tm, d), F32)
        for blk in range(N_DEV):
            dh = dh + _dot_nt(dproj_ref[:, bw * blk:bw * (blk + 1)], w_ref[blk])
        x = x_ref[...]
        r = lax.rsqrt(jnp.mean(x * x, axis=-1, keepdims=True) + NORM_EPS)
        xh0 = x * r
        dgm_ref[...] += jnp.sum(dh * xh0, axis=0, keepdims=True)
        dxh = dh * g_ref[...]
        dx0_ref[...] = dx1_ref[...] + r * (dxh - xh0 * jnp.mean(dxh * xh0, axis=-1, keepdims=True))

    return _pcall(
        body, name="in_proj_bwd", grid=(t // tm,),
        in_specs=[_row_spec(tm, d), _row_spec(tm, d), _row_spec(tm, du), _row_spec(tm, du), _row_spec(tm, du),
                  _row_spec(tm, d), _row_spec(tm, dkw), _row_spec(tm, d),
                  pl.BlockSpec((nkv, tm, 128), lambda i: (0, i, 0)), _heads_spec(nkv, tm),
                  _row_spec(tm, 2 * d), _full_spec((1, d)), _ANY, _full_spec((1, 128)), _full_spec((1, 128)),
                  _row_spec(tm, 128), _row_spec(tm, 128)],
        out_specs=[_row_spec(tm, d), _row_spec(tm, 4 * d), _full_spec((1, d)), _full_spec((1, 128)),
                   _full_spec((1, 128))],
        out_shape=[jax.ShapeDtypeStruct((t, d), F32), jax.ShapeDtypeStruct((t, 4 * d), BF16),
                   jax.ShapeDtypeStruct((1, d), F32), jax.ShapeDtypeStruct((1, 128), F32),
                   jax.ShapeDtypeStruct((1, 128), F32)],
        scratch=[pltpu.VMEM((N_DEV, d, bw), BF16), pltpu.VMEM((tm, dkw), F32), pltpu.SemaphoreType.DMA((1,))],
        vmem=V7X_VMEM_LIMIT,
    )(x0, dx1, dud, duf, dub, qraw, kraw, dq, dk, dv, dgates, g_mix, w_in, qg, kg, cos, sin)


def _attn_fwd(qat, ka, vta):
    nh, _, t = qat.shape
    nkv = ka.shape[0]
    rep = nh // nkv
    hd = HEAD_DIM
    vr = vta.shape[1]
    tq = tk = KV_TILE

    def body(qt_ref, k_ref, vt_ref, ot_ref, lse_ref, m_scr, acc_scr):
        j = pl.program_id(2)

        @pl.when(j == 0)
        def _():
            m_scr[...] = jnp.full(m_scr.shape, NEG_BIG, F32)
            acc_scr[...] = jnp.zeros_like(acc_scr)

        k = k_ref[0]
        vt = vt_ref[0]
        for r in range(rep):
            st = _dot(k, qt_ref[r])
            m_prev = m_scr[r]
            m_next = jnp.maximum(m_prev, jnp.max(st, axis=0, keepdims=True))
            pt = jnp.exp(st - m_next).astype(BF16)
            acc_scr[r] = jnp.exp(m_prev - m_next) * acc_scr[r] + _dot(vt, pt)
            m_scr[r] = m_next

        @pl.when(j == pl.num_programs(2) - 1)
        def _():
            for r in range(rep):
                l = acc_scr[r, hd:hd + 1, :]
                ot_ref[hd * r:hd * (r + 1), :] = acc_scr[r, 0:hd, :] / l
                lse_ref[0, r:r + 1, :] = m_scr[r] + jnp.log(l)

    return _pcall(
        body, name="attn_fwd", grid=(nkv, t // tq, t // tk),
        in_specs=[pl.BlockSpec((rep, 128, tq), lambda g, i, j: (g, 0, i)),
                  pl.BlockSpec((1, tk, 128), lambda g, i, j: (g, j, 0)),
                  pl.BlockSpec((1, vr, tk), lambda g, i, j: (g, 0, j))],
        out_specs=[pl.BlockSpec((rep * hd, tq), lambda g, i, j: (g, i)),
                   pl.BlockSpec((1, rep, tq), lambda g, i, j: (g, 0, i))],
        out_shape=[jax.ShapeDtypeStruct((nh * hd, t), F32), jax.ShapeDtypeStruct((nkv, rep, t), F32)],
        scratch=[pltpu.VMEM((rep, 1, tq), F32), pltpu.VMEM((rep, vr, tq), F32)],
        vmem=V7X_VMEM_LIMIT,
    )(qat, ka, vta)


def _attn_bwd(qat, ka, kt, va, dot, ot, lse_row):
    nh, _, t = qat.shape
    nkv = ka.shape[0]
    rep = nh // nkv
    hd = HEAD_DIM
    tq = tk = KV_TILE

    def body(qt_ref, k_ref, kt_ref, v_ref, dot_ref, ot_ref, lse_ref, dk_ref, dv_ref, dqt_ref):
        j = pl.program_id(1)
        i = pl.program_id(2)

        @pl.when(jnp.logical_and(j == 0, i == 0))
        def _():
            dqt_ref[...] = jnp.zeros_like(dqt_ref)

        @pl.when(i == 0)
        def _():
            dk_ref[...] = jnp.zeros_like(dk_ref)
            dv_ref[...] = jnp.zeros_like(dv_ref)

        k = k_ref[0]
        kt = kt_ref[0]
        v = v_ref[0, :, 0:hd]
        cols = pl.ds(pl.multiple_of(i * tq, tq), tq)
        dk = jnp.zeros((tk, 128), F32)
        dv = jnp.zeros((tk, hd), F32)
        for r in range(rep):
            heads = slice(hd * r, hd * (r + 1))
            qt = qt_ref[r]
            dot_r = dot_ref[heads, :]
            delta = jnp.sum(dot_r.astype(F32) * ot_ref[heads, :], axis=0, keepdims=True)
            pt = jnp.exp(_dot(k, qt) - lse_ref[0, r:r + 1, :])
            dst = (pt * (_dot(v, dot_r) - delta)).astype(BF16)
            dv = dv + _dot_nt(pt.astype(BF16), dot_r)
            dk = dk + _dot_nt(dst, qt)
            dqt_ref[heads, cols] += _dot(kt, dst)
        dk_ref[0] += dk
        dv_ref[0] += dv

    return _pcall(
        body, name="attn_bwd", grid=(nkv, t // tk, t // tq),
        in_specs=[pl.BlockSpec((rep, 128, tq), lambda g, j, i: (g, 0, i)),
                  pl.BlockSpec((1, tk, 128), lambda g, j, i: (g, j, 0)),
                  pl.BlockSpec((1, hd, tk), lambda g, j, i: (g, 0, j)),
                  pl.BlockSpec((1, tk, 128), lambda g, j, i: (g, j, 0)),
                  pl.BlockSpec((rep * hd, tq), lambda g, j, i: (g, i)),
                  pl.BlockSpec((rep * hd, tq), lambda g, j, i: (g, i)),
                  pl.BlockSpec((1, rep, tq), lambda g, j, i: (g, 0, i))],
        out_specs=[pl.BlockSpec((1, tk, 128), lambda g, j, i: (g, j, 0)),
                   pl.BlockSpec((1, tk, hd), lambda g, j, i: (g, j, 0)),
                   pl.BlockSpec((rep * hd, t), lambda g, j, i: (g, 0))],
        out_shape=[jax.ShapeDtypeStruct((nkv, t, 128), F32), jax.ShapeDtypeStruct((nkv, t, hd), F32),
                   jax.ShapeDtypeStruct((nh * hd, t), F32)],
        vmem=V7X_VMEM_LIMIT,
    )(qat, ka, kt, va, dot, ot, lse_row)


def _riding_exchange(refs, exchange, n_in, n_out, first_step, last_step):
    if exchange is None:
        return refs
    x_ref, out_ref = refs[n_in], refs[n_in + 1 + n_out]
    sems = refs[-3:]

    @pl.when(first_step)
    def _():
        _start_all(*_exchange_copies(x_ref, out_ref, *sems, exchange[1]))

    @pl.when(last_step)
    def _():
        _wait_all(*_exchange_copies(x_ref, out_ref, *sems, exchange[1]))

    return refs[:n_in] + refs[n_in + 1:n_in + 1 + n_out] + refs[n_in + 2 + n_out:-3]


def _ssm_scan_fwd(ub, lam_re, lam_im, bb_re, bb_im, cc_re, cc_im, exchange=None):
    t, w = ub.shape
    gn = lam_re.shape[-1]
    tc = ROW_TILE
    cl = min(gn, SCAN_LANES)
    nblk = t // tc

    def body(*refs):
        first = jnp.logical_and(pl.program_id(0) == 0, pl.program_id(1) == 0)
        last = jnp.logical_and(pl.program_id(0) == 1, pl.program_id(1) == nblk - 1)
        (u_ref, lr_ref, li_ref, br_ref, bi_ref, cr_ref, ci_ref, y_ref, xr_ref, xi_ref,
         bur_scr, bui_scr, cr_scr, ci_scr) = _riding_exchange(refs, exchange, 7, 3, first, last)
        descending = pl.program_id(0) == 1

        @pl.when(pl.program_id(1) == 0)
        def _():
            cr_scr[...] = jnp.zeros_like(cr_scr)
            ci_scr[...] = jnp.zeros_like(ci_scr)

        ub_t = u_ref[...]
        bur_scr[...] = _dot(ub_t, br_ref[0])
        bui_scr[...] = _dot(ub_t, bi_ref[0])
        for c0 in range(0, gn, cl):
            lanes = pl.ds(c0, cl)
            lr = lr_ref[0, :, lanes]
            li = li_ref[0, :, lanes]

            def step(k, carry, lanes=lanes, lr=lr, li=li):
                xr, xi = carry
                r = jnp.where(descending, tc - 1 - k, k)
                nr = (lr * xr - li * xi) + bur_scr[pl.ds(r, 1), lanes]
                ni = (lr * xi + li * xr) + bui_scr[pl.ds(r, 1), lanes]
                xr_ref[0, pl.ds(r, 1), lanes] = nr
                xi_ref[0, pl.ds(r, 1), lanes] = ni
                return nr, ni

            xr, xi = lax.fori_loop(0, tc, step, (cr_scr[:, lanes], ci_scr[:, lanes]), unroll=8)
            cr_scr[:, lanes] = xr
            ci_scr[:, lanes] = xi
        y_ref[0] = _dot(xr_ref[0].astype(BF16), cr_ref[0]) - _dot(xi_ref[0].astype(BF16), ci_ref[0])

    blk = lambda dd, i: jnp.where(dd == 0, i, nblk - 1 - i)
    row = lambda width: pl.BlockSpec((1, tc, width), lambda dd, i: (dd, blk(dd, i), 0))
    per_dir = lambda a, b: pl.BlockSpec((1, a, b), lambda dd, i: (dd, 0, 0))
    extra = exchange is not None
    return _pcall(
        body, name="ssm_scan_fwd", grid=(2, nblk),
        in_specs=[pl.BlockSpec((tc, w), lambda dd, i: (blk(dd, i), 0)), per_dir(1, gn), per_dir(1, gn),
                  per_dir(w, gn), per_dir(w, gn), per_dir(gn, w), per_dir(gn, w)] + [_ANY] * extra,
        out_specs=[row(w), row(gn), row(gn)] + [_ANY] * extra,
        out_shape=[jax.ShapeDtypeStruct((2, t, w), F32), jax.ShapeDtypeStruct((2, t, gn), F32),
                   jax.ShapeDtypeStruct((2, t, gn), F32)] + ([_exchange_out_shape(*exchange)] if extra else []),
        scratch=[pltpu.VMEM((tc, gn), F32), pltpu.VMEM((tc, gn), F32), pltpu.VMEM((1, gn), F32),
                 pltpu.VMEM((1, gn), F32)] + _EXCHANGE_SEMS * extra,
        vmem=V7X_VMEM_LIMIT,
    )(ub, lam_re, lam_im, bb_re, bb_im, cc_re, cc_im, *([exchange[0]] if extra else []))


def _ssm_scan_bwd(dyb, xs_re, xs_im, lam_re, lam_im, cct_re, cct_im, bbt_re, bbt_im, exchange=None):
    t, w = dyb.shape
    gn = lam_re.shape[-1]
    tc = ROW_TILE
    cl = min(gn, SCAN_LANES)
    nblk = t // tc

    def body(*refs):
        i = pl.program_id(1)
        first = jnp.logical_and(pl.program_id(0) == 0, i == 0)
        last = jnp.logical_and(pl.program_id(0) == 1, i == nblk - 1)
        (dy_ref, xr_ref, xi_ref, hr_ref, hi_ref, lr_ref, li_ref, ctr_ref, cti_ref, btr_ref, bti_ref,
         du_ref, gr_ref, gi_ref, dlr_ref, dli_ref,
         gxr_scr, gxi_scr, cr_scr, ci_scr, ar_scr, ai_scr) = _riding_exchange(refs, exchange, 11, 5, first, last)
        ascending = pl.program_id(0) == 1

        @pl.when(i == 0)
        def _():
            cr_scr[...] = jnp.zeros_like(cr_scr)
            ci_scr[...] = jnp.zeros_like(ci_scr)
            ar_scr[...] = jnp.zeros_like(ar_scr)
            ai_scr[...] = jnp.zeros_like(ai_scr)

        dyb = dy_ref[...]
        gxr_scr[...] = _dot(dyb, ctr_ref[0])
        gxi_scr[...] = -_dot(dyb, cti_ref[0])
        first_block = i == nblk - 1
        edge_row = lax.broadcasted_iota(jnp.int32, (tc, 1), 0) == jnp.where(ascending, tc - 1, 0)
        halo_row = jnp.where(ascending, 0, 7)
        shift = jnp.where(ascending, tc - 1, 1)
        for c0 in range(0, gn, cl):
            lanes = pl.ds(c0, cl)
            lr = lr_ref[0, :, lanes]
            li = li_ref[0, :, lanes]

            def step(k, carry, lanes=lanes, lr=lr, li=li):
                gr, gi = carry
                r = jnp.where(ascending, k, tc - 1 - k)
                ngr = gxr_scr[pl.ds(r, 1), lanes] + (lr * gr + li * gi)
                ngi = gxi_scr[pl.ds(r, 1), lanes] + (lr * gi - li * gr)
                gxr_scr[pl.ds(r, 1), lanes] = ngr
                gxi_scr[pl.ds(r, 1), lanes] = ngi
                return ngr, ngi

            gr, gi = lax.fori_loop(0, tc, step, (cr_scr[:, lanes], ci_scr[:, lanes]), unroll=8)
            cr_scr[:, lanes] = gr
            ci_scr[:, lanes] = gi
        for c0 in range(0, gn, 512):
            lanes = pl.ds(c0, 512)
            halo_r = jnp.where(first_block, 0.0, hr_ref[0, pl.ds(halo_row, 1), lanes])
            halo_i = jnp.where(first_block, 0.0, hi_ref[0, pl.ds(halo_row, 1), lanes])
            xpr = jnp.where(edge_row, halo_r, pltpu.roll(xr_ref[0, :, lanes], shift, 0))
            xpi = jnp.where(edge_row, halo_i, pltpu.roll(xi_ref[0, :, lanes], shift, 0))
            gr = gxr_scr[:, lanes]
            gi = gxi_scr[:, lanes]
            ar_scr[:, lanes] += jnp.sum(gr * xpr + gi * xpi, axis=0, keepdims=True)
            ai_scr[:, lanes] += jnp.sum(gi * xpr - gr * xpi, axis=0, keepdims=True)
        grb = gxr_scr[...].astype(BF16)
        gib = gxi_scr[...].astype(BF16)
        gr_ref[0] = grb
        gi_ref[0] = gib
        du_ref[0] = _dot(grb, btr_ref[0]) + _dot(gib, bti_ref[0])
        dlr_ref[0] = ar_scr[...]
        dli_ref[0] = ai_scr[...]

    blk = lambda dd, i: jnp.where(dd == 0, nblk - 1 - i, i)
    rev = lambda width: pl.BlockSpec((1, tc, width), lambda dd, i: (dd, blk(dd, i), 0))
    halo_blk = lambda dd, i: jnp.where(dd == 0, jnp.maximum(blk(dd, i) * (tc // 8) - 1, 0),
                                       jnp.minimum((blk(dd, i) + 1) * (tc // 8), t // 8 - 1))
    halo = pl.BlockSpec((1, 8, gn), lambda dd, i: (dd, halo_blk(dd, i), 0))
    per_dir = lambda a, b: pl.BlockSpec((1, a, b), lambda dd, i: (dd, 0, 0))
    extra = exchange is not None
    return _pcall(
        body, name="ssm_scan_bwd", grid=(2, nblk),
        in_specs=[pl.BlockSpec((tc, w), lambda dd, i: (blk(dd, i), 0)), rev(gn), rev(gn), halo, halo,
                  per_dir(1, gn), per_dir(1, gn), per_dir(w, gn), per_dir(w, gn), per_dir(gn, w), per_dir(gn, w)]
        + [_ANY] * extra,
        out_specs=[rev(w), rev(gn), rev(gn), per_dir(1, gn), per_dir(1, gn)] + [_ANY] * extra,
        out_shape=[jax.ShapeDtypeStruct((2, t, w), F32), jax.ShapeDtypeStruct((2, t, gn), BF16),
                   jax.ShapeDtypeStruct((2, t, gn), BF16), jax.ShapeDtypeStruct((2, 1, gn), F32),
                   jax.ShapeDtypeStruct((2, 1, gn), F32)] + ([_exchange_out_shape(*exchange)] if extra else []),
        scratch=[pltpu.VMEM((tc, gn), F32), pltpu.VMEM((tc, gn), F32), pltpu.VMEM((1, gn), F32),
                 pltpu.VMEM((1, gn), F32), pltpu.VMEM((1, gn), F32), pltpu.VMEM((1, gn), F32)]
        + _EXCHANGE_SEMS * extra,
        vmem=V7X_VMEM_LIMIT,
    )(dyb, xs_re, xs_im, xs_re, xs_im, lam_re, lam_im, cct_re, cct_im, bbt_re, bbt_im,
      *([exchange[0]] if extra else []))


def _matmul_tn(a, b, name, a_is_transposed=False):
    nb, t, n = b.shape
    m = a.shape[1] if a_is_transposed else a.shape[2]
    shared = a.shape[0] == 1
    bm, bn, tk = min(m, 1024), min(n, 1024), KV_TILE

    def body(a_ref, b_ref, o_ref):
        @pl.when(pl.program_id(3) == 0)
        def _():
            o_ref[...] = jnp.zeros_like(o_ref)

        mul = _dot if a_is_transposed else _dot_tn
        o_ref[0] += mul(a_ref[0].astype(BF16), b_ref[0].astype(BF16))

    a_spec = (pl.BlockSpec((1, bm, tk), lambda z, i, j, k: (0 if shared else z, i, k)) if a_is_transposed else
              pl.BlockSpec((1, tk, bm), lambda z, i, j, k: (0 if shared else z, k, i)))
    return _pcall(
        body, name=name, grid=(nb, m // bm, n // bn, t // tk),
        in_specs=[a_spec,
                  pl.BlockSpec((1, tk, bn), lambda z, i, j, k: (z, k, j))],
        out_specs=pl.BlockSpec((1, bm, bn), lambda z, i, j, k: (z, i, j)),
        out_shape=jax.ShapeDtypeStruct((nb, m, n), F32), vmem=V7X_VMEM_LIMIT,
    )(a, b)


def _reduce_adamw(gparts, p, m, v, name):
    rows, width = p.shape
    tr = max(k for k in range(16, 513, 16) if rows % k == 0)

    def body(g_ref, p_ref, m_ref, v_ref, go_ref, d_ref, mo_ref, vo_ref):
        g = g_ref[0].astype(F32)
        for k in range(1, N_DEV):
            g = g + g_ref[k].astype(F32)
        go_ref[...] = g
        mm = ADAM_B1 * m_ref[...] + (1.0 - ADAM_B1) * g
        vv = ADAM_B2 * v_ref[...] + (1.0 - ADAM_B2) * (g * g)
        m_hat = mm / (1.0 - ADAM_B1 ** ADAM_STEP)
        v_hat = vv / (1.0 - ADAM_B2 ** ADAM_STEP)
        d_ref[...] = -ADAM_LR * (m_hat / (jnp.sqrt(v_hat) + ADAM_EPS) + ADAM_WD * p_ref[...])
        mo_ref[...] = mm
        vo_ref[...] = vv

    spec = pl.BlockSpec((tr, width), lambda i: (i, 0))
    out = jax.ShapeDtypeStruct((rows, width), F32)
    return _pcall(
        body, name=name, grid=(rows // tr,),
        in_specs=[pl.BlockSpec((N_DEV, tr, width), lambda i: (0, i, 0)), spec, spec, spec],
        out_specs=[spec, spec, spec, spec], out_shape=[out, out, out, out], vmem=V7X_VMEM_LIMIT,
    )(gparts, p, m, v)


def _peer(k):
    x, y, c = lax.axis_index("x"), lax.axis_index("y"), lax.axis_index("c")
    return (x ^ ((k >> 2) & 1), y ^ ((k >> 1) & 1), c ^ (k & 1))


def _my_index():
    return 4 * lax.axis_index("x") + 2 * lax.axis_index("y") + lax.axis_index("c")


def _exchange_copies(x_ref, out_ref, send_sems, recv_sems, local_sem, scatter, first_sem=0):
    me = _my_index()
    local = pltpu.make_async_copy(x_ref.at[me] if scatter else x_ref, out_ref.at[me], local_sem)
    copies = []
    for k in range(1, N_DEV):
        peer = _peer(k)
        src = x_ref.at[4 * peer[0] + 2 * peer[1] + peer[2]] if scatter else x_ref
        copies.append(pltpu.make_async_remote_copy(
            src_ref=src, dst_ref=out_ref.at[me], send_sem=send_sems.at[first_sem + k - 1],
            recv_sem=recv_sems.at[first_sem + k - 1], device_id=peer, device_id_type=pl.DeviceIdType.MESH))
    return local, copies


def _start_all(local, copies):
    local.start()
    for cp in copies:
        cp.start()


def _wait_all(local, copies):
    for cp in copies:
        cp.wait_recv()
    for cp in copies:
        cp.wait_send()
    local.wait()


def _exchange_out_shape(x, scatter):
    return jax.ShapeDtypeStruct((N_DEV,) + tuple(x.shape[1:] if scatter else x.shape), x.dtype)


_EXCHANGE_SEMS = [pltpu.SemaphoreType.DMA((N_DEV - 1,)), pltpu.SemaphoreType.DMA((N_DEV - 1,)),
                  pltpu.SemaphoreType.DMA(())]


def _exchange(ops, name):
    n = len(ops)

    def body(*refs):
        x_refs, out_refs = refs[:n], refs[n:2 * n]
        send_sems, recv_sems, local_sems = refs[2 * n:]
        started = []
        for q, (_, scatter) in enumerate(ops):
            local, copies = _exchange_copies(x_refs[q], out_refs[q], send_sems, recv_sems, local_sems.at[q],
                                             scatter, first_sem=q * (N_DEV - 1))
            _start_all(local, copies)
            started.append((local, copies))
        for local, copies in started:
            _wait_all(local, copies)

    return pl.pallas_call(
        body, name=name, in_specs=[_ANY] * n, out_specs=[_ANY] * n,
        out_shape=[_exchange_out_shape(x, scatter) for x, scatter in ops],
        scratch_shapes=[pltpu.SemaphoreType.DMA((n * (N_DEV - 1),)), pltpu.SemaphoreType.DMA((n * (N_DEV - 1),)),
                        pltpu.SemaphoreType.DMA((n,))],
    )(*[x for x, _ in ops])


def _to_shards(full, axis):
    r, c = full.shape
    if axis == 0:
        return full.reshape(N_DEV, r // N_DEV, c)
    return full.reshape(r, N_DEV, c // N_DEV).transpose(1, 0, 2)


def _from_shards(shards, axis):
    _, r, c = shards.shape
    if axis == 0:
        return shards.reshape(N_DEV * r, c)
    return shards.transpose(1, 0, 2).reshape(r, N_DEV * c)


def _pack_rows(parts, lead):
    flat = []
    for p in parts:
        p = p.reshape(p.shape[:lead] + (-1, PACK_W))
        pad = _round_up(p.shape[lead], 16) - p.shape[lead]
        flat.append(jnp.pad(p, [(0, 0)] * lead + [(0, pad), (0, 0)]) if pad else p)
    return jnp.concatenate(flat, axis=lead)


def _unpack_rows(packed, shapes):
    lead = packed.shape[:-2]
    out, off = [], 0
    for shp in shapes:
        rows = math.prod(shp) // PACK_W
        out.append(packed[..., off:off + rows, :].reshape(lead + tuple(shp)))
        off += _round_up(rows, 16)
    return out


def _pack_flat(parts):
    flat = jnp.concatenate([p.reshape(-1) for p in parts])
    n = flat.shape[0]
    flat = jnp.pad(flat, (0, _round_up(n, 16 * PACK_W) - n))
    return flat.reshape(-1, PACK_W)


def _unpack(packed, shapes):
    flat = packed.reshape(-1)
    out, off = [], 0
    for shp in shapes:
        n = math.prod(shp)
        out.append(flat[off:off + n].reshape(shp))
        off += n
    return out


def _ssm_discretize(a_re, a_im, log_dt, b_re, b_im):
    dt = jnp.exp(log_dt)[..., None]
    lam_re = jnp.minimum(a_re, EIG_RE_MAX)
    lam_im = a_im
    mag = jnp.exp(lam_re * dt)
    ang = lam_im * dt
    lb_re = mag * jnp.cos(ang)
    lb_im = mag * jnp.sin(ang)
    num_re = lb_re - 1.0
    num_im = lb_im
    den = lam_re * lam_re + lam_im * lam_im
    f_re = (num_re * lam_re + num_im * lam_im) / den
    f_im = (num_im * lam_re - num_re * lam_im) / den
    bb_re = f_re[..., None] * b_re - f_im[..., None] * b_im
    bb_im = f_re[..., None] * b_im + f_im[..., None] * b_re
    return lb_re, lb_im, bb_re, bb_im


def _block_diag(blocks):
    two, g, a, b = blocks.shape
    eye = jnp.eye(g, dtype=blocks.dtype)
    return (blocks[:, :, :, None, :] * eye[None, :, None, :, None]).reshape(two, g * a, g * b)


def _diag_blocks(dense, g):
    two, ga, gb = dense.shape
    return jnp.einsum("zgagb->zgab", dense.reshape(two, g, ga // g, g, gb // g))


def _rope_tables(t, n_valid):
    pos = jnp.arange(t)
    real = jnp.logical_and(pos >= N_META, pos < n_valid)
    idx = jnp.where(real, pos - N_META, 0)
    row_id = (idx // GRID_W).astype(F32)
    col_id = (idx % GRID_W).astype(F32)
    pairs_per_axis = HEAD_DIM // 4
    inv_freq = ROPE_THETA ** (-jnp.arange(pairs_per_axis, dtype=F32) / pairs_per_axis)
    ang = jnp.concatenate([row_id[:, None] * inv_freq, col_id[:, None] * inv_freq], axis=-1)
    ang = jnp.where(real[:, None], ang, 0.0)
    cos = jnp.repeat(jnp.cos(ang), 2, axis=-1)
    sin = jnp.sin(ang)
    sin = jnp.stack([-sin, sin], axis=-1).reshape(t, HEAD_DIM)
    return jnp.tile(cos, (1, 2)), jnp.tile(sin, (1, 2))


def _local_step(x, loss_target, big, small, comm=None):
    s_len, d = x.shape
    n_valid = s_len + N_META
    t = _round_up(n_valid, KV_TILE)
    du = d // 2
    groups = du // SSM_GROUP
    nh = d // HEAD_DIM
    nkv = nh // KV_REP
    pad = t - n_valid

    x0 = jnp.concatenate([big["meta_tokens"].astype(F32), x, jnp.zeros((pad, d), F32)], axis=0)
    tgt = jnp.concatenate([jnp.zeros((N_META, d), F32), loss_target, jnp.zeros((pad, d), F32)], axis=0)
    cos, sin = _rope_tables(t, n_valid)
    g_mix = small["norm_mix_g"].reshape(1, d)
    g_mlp = small["norm_mlp_g"].reshape(1, d)
    g_fin = small["norm_final_g"].reshape(1, d)
    qg = jnp.tile(small["q_norm_g"].reshape(1, HEAD_DIM), (1, 2))
    kg = jnp.tile(small["k_norm_g"].reshape(1, HEAD_DIM), (1, 2))
    ssm_d = small["ssm_d"].reshape(1, du)
    b_glu = small["b_glu"].reshape(1, du)

    ssm_in = tuple(small[n][0] for n in ("ssm_a_re", "ssm_a_im", "ssm_log_dt", "ssm_b_re", "ssm_b_im"))
    (lb_re, lb_im, bbar_re, bbar_im), disc_vjp = jax.vjp(_ssm_discretize, *ssm_in)
    lam_re = lb_re.reshape(2, 1, groups * SSM_STATE)
    lam_im = lb_im.reshape(2, 1, groups * SSM_STATE)
    bb_re = _block_diag(bbar_re.transpose(0, 1, 3, 2)).astype(BF16)
    bb_im = _block_diag(bbar_im.transpose(0, 1, 3, 2)).astype(BF16)
    c_re, c_im = small["ssm_c_re"][0], small["ssm_c_im"][0]
    cct_re = _block_diag(c_re).astype(BF16)
    cct_im = _block_diag(c_im).astype(BF16)
    cc_re = cct_re.transpose(0, 2, 1)
    cc_im = cct_im.transpose(0, 2, 1)
    bbt_re = bb_re.transpose(0, 2, 1)
    bbt_im = bb_im.transpose(0, 2, 1)

    h, u, ub, qraw, kraw, qa, ka, va, gates = _in_proj_fwd(x0, g_mix, big["w_in"], qg, kg, cos, sin, n_valid)
    if comm is None:
        y2, xs_re, xs_im = _ssm_scan_fwd(ub, lam_re, lam_im, bb_re, bb_im, cc_re, cc_im)
    else:
        y2, xs_re, xs_im, late = _ssm_scan_fwd(ub, lam_re, lam_im, bb_re, bb_im, cc_re, cc_im,
                                               exchange=(comm["late_weights"], False))
        big = {**big, **comm["unpack_late_weights"](late)}
    yf, yb = y2[0], y2[1]
    qat = qa.transpose(0, 2, 1)
    kt = ka[:, :, 0:HEAD_DIM].transpose(0, 2, 1)
    vta = va[:, :, 0:VT_ROWS].transpose(0, 2, 1)
    yt_attn, lse = _attn_fwd(qat, ka, vta)
    mixer_w = (ssm_d, big["w_glu"], b_glu, big["w_ssm_proj"], big["w_attn_proj"], big["w_out"])
    x1 = _mixer_out_fwd(x0, u, yf, yb, yt_attn, gates, *mixer_w)

    dx1, loss8, dg_fin, dg_mlp, h2b, dab, hsqb, dx2b = _mlp_loss_fwd_bwd(
        x1, tgt, g_mlp, g_fin, big["w_mlp_in"], big["w_mlp_out"], n_valid)

    (dyb, dud, dyt_attn, dgates, zb, dglb, ysb, dasb, daab, mgb, dxb, d_ssm_d, d_b_glu) = _mixer_out_bwd(
        dx1, u, yf, yb, yt_attn, gates, *mixer_w)
    one = lambda a: a[None]
    grads = {}
    grads["w_glu"] = _matmul_tn(one(zb), one(dglb), "grad_w_glu")[0]
    grads["w_ssm_proj"] = _matmul_tn(one(ysb), one(dasb), "grad_w_ssm_proj")[0]
    grads["w_attn_proj"] = _matmul_tn(one(yt_attn), one(daab), "grad_w_attn_proj", a_is_transposed=True)[0]
    grads["w_out"] = _matmul_tn(one(mgb), one(dxb), "grad_w_out")[0]
    grads["w_mlp_in"] = _matmul_tn(one(h2b), one(dab), "grad_w_mlp_in")[0]
    grads["w_mlp_out"] = _matmul_tn(one(hsqb), one(dx2b), "grad_w_mlp_out")[0]
    dk, dv, dqt = _attn_bwd(qat, ka, kt, va, dyt_attn, yt_attn, lse)
    if comm is None:
        late_grad_parts = None
        du2, g_re, g_im, dlam_re, dlam_im = _ssm_scan_bwd(dyb, xs_re, xs_im, lam_re, lam_im, cct_re, cct_im,
                                                          bbt_re, bbt_im)
    else:
        du2, g_re, g_im, dlam_re, dlam_im, late_grad_parts = _ssm_scan_bwd(
            dyb, xs_re, xs_im, lam_re, lam_im, cct_re, cct_im, bbt_re, bbt_im,
            exchange=(comm["pack_late_grads"](grads), True))
    dx0, dproj, dg_mix, dqg, dkg = _in_proj_bwd(x0, dx1, dud, du2[0], du2[1], qraw, kraw, dqt.T, dk, dv, dgates,
                                                g_mix, big["w_in"], qg, kg, cos, sin)

    u1, dy1 = one(ub), one(dyb)
    grads["w_in"] = _matmul_tn(one(h), one(dproj), "grad_w_in")[0]
    grads["meta_tokens"] = dx0[0:N_META]
    dbb_re = _diag_blocks(_matmul_tn(u1, g_re, "grad_ssm_bbar_re"), groups).transpose(0, 1, 3, 2)
    dbb_im = _diag_blocks(_matmul_tn(u1, g_im, "grad_ssm_bbar_im"), groups).transpose(0, 1, 3, 2)
    dc_re = _diag_blocks(_matmul_tn(dy1, xs_re, "grad_ssm_c_re"), groups)
    dc_im = -_diag_blocks(_matmul_tn(dy1, xs_im, "grad_ssm_c_im"), groups)
    shape_gn = (2, groups, SSM_STATE)
    d_a_re, d_a_im, d_log_dt, d_b_re, d_b_im = disc_vjp(
        (dlam_re.reshape(shape_gn), dlam_im.reshape(shape_gn), dbb_re, dbb_im))
    grads.update({
        "norm_mix_g": dg_mix, "ssm_a_re": d_a_re[None], "ssm_a_im": d_a_im[None], "ssm_log_dt": d_log_dt[None],
        "ssm_b_re": d_b_re[None], "ssm_b_im": d_b_im[None], "ssm_c_re": dc_re[None], "ssm_c_im": dc_im[None],
        "ssm_d": d_ssm_d, "b_glu": d_b_glu,
        "q_norm_g": dqg[:, 0:HEAD_DIM] + dqg[:, HEAD_DIM:128], "k_norm_g": dkg[:, 0:HEAD_DIM] + dkg[:, HEAD_DIM:128],
        "norm_mlp_g": dg_mlp, "norm_final_g": dg_fin.reshape(d),
    })
    return loss8[0, 0], dx0[N_META:n_valid], grads, late_grad_parts


def kernel(x, meta_tokens, norm_mix_g, w_in, ssm_a_re, ssm_a_im, ssm_log_dt, ssm_b_re, ssm_b_im, ssm_c_re, ssm_c_im, ssm_d, w_glu, b_glu, q_norm_g, k_norm_g, w_ssm_proj, w_attn_proj, w_out, norm_mlp_g, w_mlp_in, w_mlp_out, norm_final_g, loss_target, m_meta_tokens, m_norm_mix_g, m_w_in, m_ssm_a_re, m_ssm_a_im, m_ssm_log_dt, m_ssm_b_re, m_ssm_b_im, m_ssm_c_re, m_ssm_c_im, m_ssm_d, m_w_glu, m_b_glu, m_q_norm_g, m_k_norm_g, m_w_ssm_proj, m_w_attn_proj, m_w_out, m_norm_mlp_g, m_w_mlp_in, m_w_mlp_out, m_norm_final_g, v_meta_tokens, v_norm_mix_g, v_w_in, v_ssm_a_re, v_ssm_a_im, v_ssm_log_dt, v_ssm_b_re, v_ssm_b_im, v_ssm_c_re, v_ssm_c_im, v_ssm_d, v_w_glu, v_b_glu, v_q_norm_g, v_k_norm_g, v_w_ssm_proj, v_w_attn_proj, v_w_out, v_norm_mlp_g, v_w_mlp_in, v_w_mlp_out, v_norm_final_g):
    w = dict(meta_tokens=meta_tokens, norm_mix_g=norm_mix_g, w_in=w_in, ssm_a_re=ssm_a_re, ssm_a_im=ssm_a_im, ssm_log_dt=ssm_log_dt, ssm_b_re=ssm_b_re, ssm_b_im=ssm_b_im, ssm_c_re=ssm_c_re, ssm_c_im=ssm_c_im, ssm_d=ssm_d, w_glu=w_glu, b_glu=b_glu, q_norm_g=q_norm_g, k_norm_g=k_norm_g, w_ssm_proj=w_ssm_proj, w_attn_proj=w_attn_proj, w_out=w_out, norm_mlp_g=norm_mlp_g, w_mlp_in=w_mlp_in, w_mlp_out=w_mlp_out, norm_final_g=norm_final_g)
    m = dict(meta_tokens=m_meta_tokens, norm_mix_g=m_norm_mix_g, w_in=m_w_in, ssm_a_re=m_ssm_a_re, ssm_a_im=m_ssm_a_im, ssm_log_dt=m_ssm_log_dt, ssm_b_re=m_ssm_b_re, ssm_b_im=m_ssm_b_im, ssm_c_re=m_ssm_c_re, ssm_c_im=m_ssm_c_im, ssm_d=m_ssm_d, w_glu=m_w_glu, b_glu=m_b_glu, q_norm_g=m_q_norm_g, k_norm_g=m_k_norm_g, w_ssm_proj=m_w_ssm_proj, w_attn_proj=m_w_attn_proj, w_out=m_w_out, norm_mlp_g=m_norm_mlp_g, w_mlp_in=m_w_mlp_in, w_mlp_out=m_w_mlp_out, norm_final_g=m_norm_final_g)
    v = dict(meta_tokens=v_meta_tokens, norm_mix_g=v_norm_mix_g, w_in=v_w_in, ssm_a_re=v_ssm_a_re, ssm_a_im=v_ssm_a_im, ssm_log_dt=v_ssm_log_dt, ssm_b_re=v_ssm_b_re, ssm_b_im=v_ssm_b_im, ssm_c_re=v_ssm_c_re, ssm_c_im=v_ssm_c_im, ssm_d=v_ssm_d, w_glu=v_w_glu, b_glu=v_b_glu, q_norm_g=v_q_norm_g, k_norm_g=v_k_norm_g, w_ssm_proj=v_w_ssm_proj, w_attn_proj=v_w_attn_proj, w_out=v_w_out, norm_mlp_g=v_norm_mlp_g, w_mlp_in=v_w_mlp_in, w_mlp_out=v_w_mlp_out, norm_final_g=v_norm_final_g)

    shard2d = {n: w[n].reshape(w[n].shape[-2:]) for n in BIG_WEIGHTS}
    big_shapes = [shard2d[n].shape for n in BIG_WEIGHTS]

    meta_hi = shard2d["meta_tokens"].astype(BF16)
    meta_res = shard2d["meta_tokens"] - meta_hi.astype(F32)
    meta_mid = meta_res.astype(BF16)
    meta_lo = (meta_res - meta_mid.astype(F32)).astype(BF16)
    shapes_of = lambda names: [shard2d[n].shape for n in names]

    def full_weights(names, shards):
        return {n: s if n in BLOCK_WEIGHTS else _from_shards(s, BIG_SHARD_AXIS[n]) for n, s in zip(names, shards)}

    early = _exchange([(_pack_rows([meta_hi, meta_mid, meta_lo, shard2d["w_in"].astype(BF16)], 0), False)],
                      "gather_early_weights")[0]
    shards = _unpack_rows(early, [meta_hi.shape] * 3 + shapes_of(EARLY_WEIGHTS[1:]))
    meta = [_from_shards(s, 1).astype(F32) for s in shards[0:3]]
    big = {"meta_tokens": (meta[0] + meta[1]) + meta[2], **full_weights(EARLY_WEIGHTS[1:], shards[3:])}
    small = {n: w[n] for n in SMALL_WEIGHTS}
    pack_grads = lambda names, grads: _pack_rows(
        [_to_shards(grads[n], BIG_SHARD_AXIS[n]) for n in names], 1).astype(BF16)
    comm = {
        "late_weights": _pack_rows([shard2d[n].astype(BF16) for n in LATE_WEIGHTS], 0),
        "unpack_late_weights": lambda g: full_weights(LATE_WEIGHTS, _unpack_rows(g, shapes_of(LATE_WEIGHTS))),
        "pack_late_grads": functools.partial(pack_grads, LATE_WEIGHTS),
    }

    loss, grad_x, grads, late_parts = _local_step(x[0], loss_target[0], big, small, comm)
    loss = lax.psum(loss, ("x", "y", "c"))

    early_parts, small_parts = _exchange([(pack_grads(EARLY_WEIGHTS, grads), True),
                                          (_pack_flat([grads[n] for n in SMALL_WEIGHTS]), False)],
                                         "exchange_last_grads")
    pk = lambda names, src: _pack_rows([src[n].reshape(shard2d[n].shape) for n in names], 0)
    pe, pl_ = functools.partial(pk, EARLY_WEIGHTS), functools.partial(pk, LATE_WEIGHTS)
    early_out = _reduce_adamw(early_parts, pe(w), pe(m), pe(v), "adamw_sharded_early")
    late_out = _reduce_adamw(late_parts, pl_(w), pl_(m), pl_(v), "adamw_sharded_late")
    small_shapes = [w[n].shape for n in SMALL_WEIGHTS]
    pf = lambda src: _pack_flat([src[n] for n in SMALL_WEIGHTS])
    small_out = _reduce_adamw(small_parts, pf(w), pf(m), pf(v), "adamw_replicated")

    results = []
    for kind in range(4):
        big_un = dict(zip(EARLY_WEIGHTS + LATE_WEIGHTS,
                          _unpack_rows(early_out[kind], shapes_of(EARLY_WEIGHTS))
                          + _unpack_rows(late_out[kind], shapes_of(LATE_WEIGHTS))))
        small_un = dict(zip(SMALL_WEIGHTS, _unpack(small_out[kind], small_shapes)))
        for n in ALL_WEIGHTS:
            results.append(big_un[n].reshape(w[n].shape) if n in big_un else small_un[n])
    return (loss, grad_x[None], *results)
```

```python
import functools
import math

import jax
import jax.numpy as jnp
from jax import lax
from jax.experimental import pallas as pl
from jax.experimental.pallas import tpu as pltpu

F32 = jnp.float32
BF16 = jnp.bfloat16

N_DEV = 8
N_META = 16
GRID_W = 64
SSM_GROUP = 16
SSM_STATE = 64
HEAD_DIM = 64
KV_REP = 4
ROPE_THETA = 10000.0
NORM_EPS = 1e-6
EIG_RE_MAX = -1e-4
ATTN_SCALE = HEAD_DIM ** -0.5

ADAM_LR = 0.001
ADAM_B1 = 0.9
ADAM_B2 = 0.999
ADAM_EPS = 1e-08
ADAM_WD = 0.01
ADAM_STEP = 10

ROW_TILE = 384
ROW_TILE_BWD = 256
QUERY_STRIP = 256
VT_ROWS = 80
MASK_BIAS = -1e30
SCAN_LANES = 512
DIAG_TILE = 256
KV_TILE = 768
PACK_W = 1024
V7X_VMEM_LIMIT = 56 * 1024 * 1024
NEG_BIG = -1e30

BIG_WEIGHTS = ("meta_tokens", "w_in", "w_glu", "w_ssm_proj", "w_attn_proj", "w_out", "w_mlp_in", "w_mlp_out")
BIG_SHARD_AXIS = {"meta_tokens": 1, "w_in": 1, "w_glu": 0, "w_ssm_proj": 1, "w_attn_proj": 0, "w_out": 0,
                  "w_mlp_in": 1, "w_mlp_out": 0}
BLOCK_WEIGHTS = ("w_in", "w_mlp_in", "w_mlp_out")
EARLY_WEIGHTS = ("meta_tokens", "w_in")
LATE_WEIGHTS = ("w_glu", "w_ssm_proj", "w_attn_proj", "w_out", "w_mlp_in", "w_mlp_out")
SMALL_WEIGHTS = ("norm_mix_g", "ssm_a_re", "ssm_a_im", "ssm_log_dt", "ssm_b_re", "ssm_b_im", "ssm_c_re",
                 "ssm_c_im", "ssm_d", "b_glu", "q_norm_g", "k_norm_g", "norm_mlp_g", "norm_final_g")
ALL_WEIGHTS = ("meta_tokens", "norm_mix_g", "w_in", "ssm_a_re", "ssm_a_im", "ssm_log_dt", "ssm_b_re", "ssm_b_im",
               "ssm_c_re", "ssm_c_im", "ssm_d", "w_glu", "b_glu", "q_norm_g", "k_norm_g", "w_ssm_proj",
               "w_attn_proj", "w_out", "norm_mlp_g", "w_mlp_in", "w_mlp_out", "norm_final_g")


def _round_up(n, m):
    return (n + m - 1) // m * m


def _pcall(body, *, name, grid, in_specs, out_specs, out_shape, scratch=(), vmem=None, **kw):
    params = pltpu.CompilerParams(dimension_semantics=("arbitrary",) * len(grid), vmem_limit_bytes=vmem)
    return pl.pallas_call(body, name=name, grid=grid, in_specs=in_specs, out_specs=out_specs, out_shape=out_shape,
                          scratch_shapes=list(scratch), compiler_params=params, **kw)


def _dot(a, b):
    return jnp.dot(a, b, preferred_element_type=F32)


def _dot_nt(a, b):
    return lax.dot_general(a, b, (((1,), (1,)), ((), ())), preferred_element_type=F32)


def _dot_tn(a, b):
    return lax.dot_general(a, b, (((0,), (0,)), ((), ())), preferred_element_type=F32)


def _full_spec(shape):
    nd = len(shape)
    return pl.BlockSpec(shape, lambda *_: (0,) * nd)


def _row_spec(tm, width):
    return pl.BlockSpec((tm, width), lambda i: (i, 0))


def _heads_spec(nh, tm):
    return pl.BlockSpec((nh, tm, HEAD_DIM), lambda i: (0, i, 0))


_ANY = pl.BlockSpec(memory_space=pl.ANY)


def _load_once(step, pairs, sem):
    @pl.when(step == 0)
    def _():
        copies = [pltpu.make_async_copy(src, dst, sem.at[k]) for k, (src, dst) in enumerate(pairs)]
        for cp in copies:
            cp.start()
        for cp in copies:
            cp.wait()


def _swap_pairs(x, even):
    n = x.shape[-1]
    return jnp.where(even, pltpu.roll(x, n - 1, 1), pltpu.roll(x, 1, 1))


def _gelu(y):
    return 0.5 * y * (1.0 + lax.erf(y * (1.0 / math.sqrt(2.0))))


def _gelu_grad(y):
    return 0.5 * (1.0 + lax.erf(y * (1.0 / math.sqrt(2.0)))) + y * jnp.exp(-0.5 * y * y) * (1.0 / math.sqrt(2.0 * math.pi))


def _in_proj_fwd(x0, g_mix, w_in, qg, kg, cos, sin, n_valid):
    t, d = x0.shape
    tm = ROW_TILE
    du, dk = d // 2, d // 4
    nh, nkv = d // HEAD_DIM, d // HEAD_DIM // KV_REP
    bw = w_in.shape[-1]
    assert bw == du and dk * 2 == bw

    def body(x_ref, g_ref, w_hbm, qg_ref, kg_ref, c_ref, s_ref,
             h_ref, u_ref, ub_ref, qraw_ref, kraw_ref, qa_ref, ka_ref, va_ref, gates_ref,
             w_ref, sem):
        i = pl.program_id(0)
        _load_once(i, [(w_hbm, w_ref)], sem)
        x = x_ref[...]
        r = lax.rsqrt(jnp.mean(x * x, axis=-1, keepdims=True) + NORM_EPS)
        h = ((x * r) * g_ref[...]).astype(BF16)
        h_ref[...] = h
        u = _dot(h, w_ref[0])
        u_ref[...] = u
        ub_ref[...] = u.astype(BF16)
        lane = lax.broadcasted_iota(jnp.int32, (tm, 128), 1)
        lo = lane < HEAD_DIM
        even = (lane & 1) == 0
        aug = lane == HEAD_DIM
        c = c_ref[...]
        s = s_ref[...]
        row = i * tm + lax.broadcasted_iota(jnp.int32, (tm, 1), 0)
        one = jnp.where(aug, 1.0, 0.0)
        key_bias = jnp.where(jnp.logical_and(aug, row >= n_valid), MASK_BIAS, 0.0)

        def norm_rope(blk, g128):
            sq = blk * blk
            ms_lo = jnp.sum(jnp.where(lo, sq, 0.0), axis=-1, keepdims=True) * (1.0 / HEAD_DIM)
            ms_hi = jnp.sum(jnp.where(lo, 0.0, sq), axis=-1, keepdims=True) * (1.0 / HEAD_DIM)
            rr = jnp.where(lo, lax.rsqrt(ms_lo + NORM_EPS), lax.rsqrt(ms_hi + NORM_EPS))
            qn = (blk * rr) * g128
            return qn * c + _swap_pairs(qn, even) * s

        def put_heads(ref, first, pair, extra):
            ref[first] = jnp.where(lo, pair, extra).astype(BF16)
            ref[first + 1] = jnp.where(lo, pltpu.roll(pair, HEAD_DIM, 1), extra).astype(BF16)

        for blk in range(2):
            qb = _dot(h, w_ref[1 + blk])
            qraw_ref[:, bw * blk:bw * (blk + 1)] = qb
            for a in range(bw // 128):
                put_heads(qa_ref, (bw // HEAD_DIM) * blk + 2 * a,
                          norm_rope(qb[:, 128 * a:128 * (a + 1)], qg_ref[...]) * ATTN_SCALE, one)
        kv = _dot(h, w_ref[3])
        kraw_ref[...] = kv[:, 0:dk]
        for a in range(nkv // 2):
            put_heads(ka_ref, 2 * a, norm_rope(kv[:, 128 * a:128 * (a + 1)], kg_ref[...]), key_bias)
            put_heads(va_ref, 2 * a, kv[:, dk + 128 * a:dk + 128 * (a + 1)], one)
        for blk in range(4):
            gates_ref[:, bw * blk:bw * (blk + 1)] = _dot(h, w_ref[4 + blk])

    heads = lambda n: pl.BlockSpec((n, tm, 128), lambda i: (0, i, 0))
    return _pcall(
        body, name="in_proj_fwd", grid=(t // tm,),
        in_specs=[_row_spec(tm, d), _full_spec((1, d)), _ANY, _full_spec((1, 128)), _full_spec((1, 128)),
                  _row_spec(tm, 128), _row_spec(tm, 128)],
        out_specs=[_row_spec(tm, d), _row_spec(tm, du), _row_spec(tm, du), _row_spec(tm, d), _row_spec(tm, dk),
                   heads(nh), heads(nkv), heads(nkv), _row_spec(tm, 2 * d)],
        out_shape=[jax.ShapeDtypeStruct((t, d), BF16), jax.ShapeDtypeStruct((t, du), F32),
                   jax.ShapeDtypeStruct((t, du), BF16), jax.ShapeDtypeStruct((t, d), F32),
                   jax.ShapeDtypeStruct((t, dk), F32), jax.ShapeDtypeStruct((nh, t, 128), BF16),
                   jax.ShapeDtypeStruct((nkv, t, 128), BF16), jax.ShapeDtypeStruct((nkv, t, 128), BF16),
                   jax.ShapeDtypeStruct((t, 2 * d), F32)],
        scratch=[pltpu.VMEM((N_DEV, d, bw), BF16), pltpu.SemaphoreType.DMA((1,))],
        vmem=V7X_VMEM_LIMIT,
    )(x0, g_mix, w_in, qg, kg, cos, sin)


def _mixer_values(u, yf, yb, yt_attn, gates, d_ref, wg_ref, bg_ref, ps_ref, pa_ref, d):
    y = (u * d_ref[...] + yf) + yb
    z = _gelu(y)
    sg = jax.nn.sigmoid(_dot(z.astype(BF16), wg_ref[...]) + bg_ref[...])
    y_ssm = z * sg
    a_ssm = _dot(y_ssm.astype(BF16), ps_ref[...])
    a_attn = _dot_tn(yt_attn.astype(BF16), pa_ref[...])
    s_ssm = jax.nn.sigmoid(gates[:, 0:d])
    s_attn = jax.nn.sigmoid(gates[:, d:2 * d])
    merged = s_ssm * a_ssm + s_attn * a_attn
    return y, z, sg, y_ssm, a_ssm, a_attn, s_ssm, s_attn, merged


def _mixer_out_fwd(x0, u, yf, yb, y_attn, gates, ssm_d, w_glu, b_glu, p_ssm, p_attn, w_out):
    t, d = x0.shape
    tm = ROW_TILE
    du = d // 2

    def body(x_ref, u_ref, yf_ref, yb_ref, ya_ref, gt_ref, d_ref, wg_ref, bg_ref, ps_ref, pa_ref, wo_ref, x1_ref):
        vals = _mixer_values(u_ref[...], yf_ref[...], yb_ref[...], ya_ref[...], gt_ref[...],
                             d_ref, wg_ref, bg_ref, ps_ref, pa_ref, d)
        merged = vals[-1]
        x1_ref[...] = x_ref[...] + _dot(merged.astype(BF16), wo_ref[...])

    return _pcall(
        body, name="mixer_out_fwd", grid=(t // tm,),
        in_specs=[_row_spec(tm, d), _row_spec(tm, du), _row_spec(tm, du), _row_spec(tm, du),
                  pl.BlockSpec((d, tm), lambda i: (0, i)), _row_spec(tm, 2 * d), _full_spec((1, du)), _full_spec((du, du)), _full_spec((1, du)),
                  _full_spec((du, d)), _full_spec((d, d)), _full_spec((d, d))],
        out_specs=_row_spec(tm, d), out_shape=jax.ShapeDtypeStruct((t, d), F32), vmem=V7X_VMEM_LIMIT,
    )(x0, u, yf, yb, y_attn, gates, ssm_d, w_glu, b_glu, p_ssm, p_attn, w_out)


def _mlp_loss_fwd_bwd(x1, target, g_mlp, g_fin, w1, w2, n_valid):
    t, d = x1.shape
    tm = ROW_TILE_BWD
    dff = 4 * d
    nfc, _, fc = w1.shape

    def body(x_ref, tg_ref, gm_ref, gf_ref, w1_hbm, w2_hbm,
             dx1_ref, loss_ref, dgf_ref, dgm_ref, h2_ref, da_ref, hsq_ref, dx2b_ref,
             w1_ref, w2_ref, relu_ref, sem):
        i = pl.program_id(0)
        _load_once(i, [(w1_hbm, w1_ref), (w2_hbm, w2_ref)], sem)

        @pl.when(i == 0)
        def _():
            loss_ref[...] = jnp.zeros_like(loss_ref)
            dgf_ref[...] = jnp.zeros_like(dgf_ref)
            dgm_ref[...] = jnp.zeros_like(dgm_ref)

        x1v = x_ref[...]
        r1 = lax.rsqrt(jnp.mean(x1v * x1v, axis=-1, keepdims=True) + NORM_EPS)
        xh1 = x1v * r1
        h2b = (xh1 * gm_ref[...]).astype(BF16)
        h2_ref[...] = h2b
        acc = jnp.zeros((tm, d), F32)
        for c in range(nfc):
            a = jnp.maximum(_dot(h2b, w1_ref[c]), 0.0)
            relu_ref[:, fc * c:fc * (c + 1)] = a
            hs = (a * a).astype(BF16)
            hsq_ref[:, fc * c:fc * (c + 1)] = hs
            acc = acc + _dot(hs, w2_ref[c])
        x2 = x1v + acc
        r2 = lax.rsqrt(jnp.mean(x2 * x2, axis=-1, keepdims=True) + NORM_EPS)
        xh2 = x2 * r2
        out = xh2 * gf_ref[...]
        row = i * tm + lax.broadcasted_iota(jnp.int32, (tm, 1), 0)
        valid = jnp.logical_and(row >= N_META, row < n_valid)
        diff = jnp.where(valid, out - tg_ref[...], 0.0)
        loss_ref[...] += 0.5 * jnp.sum(jnp.sum(diff * diff, axis=-1, keepdims=True) * (1.0 / d))
        dout = diff * (1.0 / d)
        dgf_ref[...] += jnp.sum(dout * xh2, axis=0, keepdims=True)
        dxh2 = dout * gf_ref[...]
        dx2 = r2 * (dxh2 - xh2 * jnp.mean(dxh2 * xh2, axis=-1, keepdims=True))
        dx2b = dx2.astype(BF16)
        dx2b_ref[...] = dx2b
        dh2 = jnp.zeros((tm, d), F32)
        for c in range(nfc):
            dhs = _dot_nt(dx2b, w2_ref[c])
            da = (dhs * (2.0 * relu_ref[:, fc * c:fc * (c + 1)])).astype(BF16)
            da_ref[:, fc * c:fc * (c + 1)] = da
            dh2 = dh2 + _dot_nt(da, w1_ref[c])
        dgm_ref[...] += jnp.sum(dh2 * xh1, axis=0, keepdims=True)
        dxh1 = dh2 * gm_ref[...]
        dx1_ref[...] = dx2 + r1 * (dxh1 - xh1 * jnp.mean(dxh1 * xh1, axis=-1, keepdims=True))

    return _pcall(
        body, name="mlp_loss_fwd_bwd", grid=(t // tm,),
        in_specs=[_row_spec(tm, d), _row_spec(tm, d), _full_spec((1, d)), _full_spec((1, d)), _ANY, _ANY],
        out_specs=[_row_spec(tm, d), _full_spec((8, 128)), _full_spec((1, d)), _full_spec((1, d)),
                   _row_spec(tm, d), _row_spec(tm, dff), _row_spec(tm, dff), _row_spec(tm, d)],
        out_shape=[jax.ShapeDtypeStruct((t, d), F32), jax.ShapeDtypeStruct((8, 128), F32),
                   jax.ShapeDtypeStruct((1, d), F32), jax.ShapeDtypeStruct((1, d), F32),
                   jax.ShapeDtypeStruct((t, d), BF16), jax.ShapeDtypeStruct((t, dff), BF16),
                   jax.ShapeDtypeStruct((t, dff), BF16), jax.ShapeDtypeStruct((t, d), BF16)],
        scratch=[pltpu.VMEM((nfc, d, fc), BF16), pltpu.VMEM((nfc, fc, d), BF16), pltpu.VMEM((tm, dff), F32),
                 pltpu.SemaphoreType.DMA((2,))],
        vmem=V7X_VMEM_LIMIT,
    )(x1, target, g_mlp, g_fin, w1, w2)


def _mixer_out_bwd(dx1, u, yf, yb, yt_attn, gates, ssm_d, w_glu, b_glu, p_ssm, p_attn, w_out):
    t, d = dx1.shape
    tm = ROW_TILE_BWD
    du = d // 2

    def body(dx_ref, u_ref, yf_ref, yb_ref, yt_ref, gt_ref, d_ref, wg_ref, bg_ref, ps_ref, pa_ref, wo_ref,
             dyb_ref, dud_ref, dyat_ref, dgates_ref, zb_ref, dglb_ref, ysb_ref, dasb_ref, daab_ref,
             mgb_ref, dxb_ref, dd_ref, dbg_ref):
        i = pl.program_id(0)

        @pl.when(i == 0)
        def _():
            dd_ref[...] = jnp.zeros_like(dd_ref)
            dbg_ref[...] = jnp.zeros_like(dbg_ref)

        uv = u_ref[...]
        y, z, sg, y_ssm, a_ssm, a_attn, s_ssm, s_attn, merged = _mixer_values(
            uv, yf_ref[...], yb_ref[...], yt_ref[...], gt_ref[...], d_ref, wg_ref, bg_ref, ps_ref, pa_ref, d)
        dxb = dx_ref[...].astype(BF16)
        dxb_ref[...] = dxb
        mgb_ref[...] = merged.astype(BF16)
        dmerged = _dot_nt(dxb, wo_ref[...])
        dgates_ref[:, 0:d] = (dmerged * a_ssm * (s_ssm * (1.0 - s_ssm))).astype(BF16)
        dgates_ref[:, d:2 * d] = (dmerged * a_attn * (s_attn * (1.0 - s_attn))).astype(BF16)
        da_ssm = (dmerged * s_ssm).astype(BF16)
        da_attn = (dmerged * s_attn).astype(BF16)
        dasb_ref[...] = da_ssm
        daab_ref[...] = da_attn
        ysb_ref[...] = y_ssm.astype(BF16)
        dy_ssm = _dot_nt(da_ssm, ps_ref[...])
        dyat_ref[...] = _dot_nt(pa_ref[...], da_attn).astype(BF16)
        dgl = dy_ssm * z * (sg * (1.0 - sg))
        dglb = dgl.astype(BF16)
        dglb_ref[...] = dglb
        zb_ref[...] = z.astype(BF16)
        dbg_ref[...] += jnp.sum(dgl, axis=0, keepdims=True)
        dz = dy_ssm * sg + _dot_nt(dglb, wg_ref[...])
        dy = dz * _gelu_grad(y)
        dyb_ref[...] = dy.astype(BF16)
        dd_ref[...] += jnp.sum(dy * uv, axis=0, keepdims=True)
        dud_ref[...] = dy * d_ref[...]

    bf = lambda w: jax.ShapeDtypeStruct((t, w), BF16)
    return _pcall(
        body, name="mixer_out_bwd", grid=(t // tm,),
        in_specs=[_row_spec(tm, d), _row_spec(tm, du), _row_spec(tm, du), _row_spec(tm, du),
                  pl.BlockSpec((d, tm), lambda i: (0, i)),
                  _row_spec(tm, 2 * d), _full_spec((1, du)), _full_spec((du, du)), _full_spec((1, du)),
                  _full_spec((du, d)), _full_spec((d, d)), _full_spec((d, d))],
        out_specs=[_row_spec(tm, du), _row_spec(tm, du), pl.BlockSpec((d, tm), lambda i: (0, i)),
                   _row_spec(tm, 2 * d), _row_spec(tm, du), _row_spec(tm, du), _row_spec(tm, du), _row_spec(tm, d),
                   _row_spec(tm, d), _row_spec(tm, d), _row_spec(tm, d), _full_spec((1, du)), _full_spec((1, du))],
        out_shape=[bf(du), jax.ShapeDtypeStruct((t, du), F32), jax.ShapeDtypeStruct((d, t), BF16), bf(2 * d),
                   bf(du), bf(du), bf(du), bf(d), bf(d), bf(d), bf(d),
                   jax.ShapeDtypeStruct((1, du), F32), jax.ShapeDtypeStruct((1, du), F32)],
        vmem=V7X_VMEM_LIMIT,
    )(dx1, u, yf, yb, yt_attn, gates, ssm_d, w_glu, b_glu, p_ssm, p_attn, w_out)


def _in_proj_bwd(x0, dx1, dud, duf, dub, qraw, kraw, dq, dk, dv, dgates, g_mix, w_in, qg, kg, cos, sin):
    t, d = x0.shape
    tm = ROW_TILE_BWD
    du, dkw = d // 2, d // 4
    nh, nkv = d // HEAD_DIM, d // HEAD_DIM // KV_REP
    bw = w_in.shape[-1]
    o_q, o_k, o_v, o_g = du, du + d, du + d + dkw, 2 * d

    def body(x_ref, dx1_ref, dud_ref, duf_ref, dub_ref, qraw_ref, kraw_ref, dq_ref, dk_ref, dv_ref, dgt_ref,
             g_ref, w_hbm, qg_ref, kg_ref, c_ref, s_ref,
             dx0_ref, dproj_ref, dgm_ref, dqg_ref, dkg_ref,
             w_ref, kv_ref, sem):
        i = pl.program_id(0)
        _load_once(i, [(w_hbm, w_ref)], sem)

        @pl.when(i == 0)
        def _():
            dgm_ref[...] = jnp.zeros_like(dgm_ref)
            dqg_ref[...] = jnp.zeros_like(dqg_ref)
            dkg_ref[...] = jnp.zeros_like(dkg_ref)

        lane = lax.broadcasted_iota(jnp.int32, (tm, 128), 1)
        lo = lane < HEAD_DIM
        even = (lane & 1) == 0
        c = c_ref[...]
        s = s_ref[...]

        def norm_rope_bwd(dout, raw, g128):
            sq = raw * raw
            ms_lo = jnp.sum(jnp.where(lo, sq, 0.0), axis=-1, keepdims=True) * (1.0 / HEAD_DIM)
            ms_hi = jnp.sum(jnp.where(lo, 0.0, sq), axis=-1, keepdims=True) * (1.0 / HEAD_DIM)
            rr = jnp.where(lo, lax.rsqrt(ms_lo + NORM_EPS), lax.rsqrt(ms_hi + NORM_EPS))
            xh = raw * rr
            dqn = dout * c + _swap_pairs(dout * s, even)
            dg = jnp.sum(dqn * xh, axis=0, keepdims=True)
            tt = dqn * g128
            pr = tt * xh
            mu_lo = jnp.sum(jnp.where(lo, pr, 0.0), axis=-1, keepdims=True) * (1.0 / HEAD_DIM)
            mu_hi = jnp.sum(jnp.where(lo, 0.0, pr), axis=-1, keepdims=True) * (1.0 / HEAD_DIM)
            return rr * (tt - xh * jnp.where(lo, mu_lo, mu_hi)), dg

        dub_tot = (dud_ref[...] + duf_ref[...]) + dub_ref[...]
        dproj_ref[:, 0:du] = dub_tot.astype(BF16)
        dqg = jnp.zeros((1, 128), F32)
        for a in range(nh // 2):
            sl = slice(128 * a, 128 * (a + 1))
            draw, dg = norm_rope_bwd(dq_ref[:, sl] * ATTN_SCALE, qraw_ref[:, sl], qg_ref[...])
            dqg = dqg + dg
            dproj_ref[:, o_q + 128 * a:o_q + 128 * (a + 1)] = draw.astype(BF16)
        dqg_ref[...] += dqg
        for hh in range(nkv):
            kv_ref[:, HEAD_DIM * hh:HEAD_DIM * (hh + 1)] = dk_ref[hh, :, 0:HEAD_DIM]
        dkg = jnp.zeros((1, 128), F32)
        for a in range(nkv // 2):
            sl = slice(128 * a, 128 * (a + 1))
            draw, dg = norm_rope_bwd(kv_ref[:, sl], kraw_ref[:, sl], kg_ref[...])
            dkg = dkg + dg
            dproj_ref[:, o_k + 128 * a:o_k + 128 * (a + 1)] = draw.astype(BF16)
        dkg_ref[...] += dkg
        for hh in range(nkv):
            kv_ref[:, HEAD_DIM * hh:HEAD_DIM * (hh + 1)] = dv_ref[hh]
        dproj_ref[:, o_v:o_g] = kv_ref[...].astype(BF16)
        dproj_ref[:, o_g:4 * d] = dgt_ref[...]
        dh = jnp.zeros((tm, d), F32)
        for blk in range(N_DEV):
            dh = dh + _dot_nt(dproj_ref[:, bw * blk:bw * (blk + 1)], w_ref[blk])
        x = x_ref[...]
        r = lax.rsqrt(jnp.mean(x * x, axis=-1, keepdims=True) + NORM_EPS)
        xh0 = x * r
        dgm_ref[...] += jnp.sum(dh * xh0, axis=0, keepdims=True)
        dxh = dh * g_ref[...]
        dx0_ref[...] = dx1_ref[...] + r * (dxh - xh0 * jnp.mean(dxh * xh0, axis=-1, keepdims=True))

    return _pcall(
        body, name="in_proj_bwd", grid=(t // tm,),
        in_specs=[_row_spec(tm, d), _row_spec(tm, d), _row_spec(tm, du), _row_spec(tm, du), _row_spec(tm, du),
                  _row_spec(tm, d), _row_spec(tm, dkw), _row_spec(tm, d),
                  pl.BlockSpec((nkv, tm, 128), lambda i: (0, i, 0)), _heads_spec(nkv, tm),
                  _row_spec(tm, 2 * d), _full_spec((1, d)), _ANY, _full_spec((1, 128)), _full_spec((1, 128)),
                  _row_spec(tm, 128), _row_spec(tm, 128)],
        out_specs=[_row_spec(tm, d), _row_spec(tm, 4 * d), _full_spec((1, d)), _full_spec((1, 128)),
                   _full_spec((1, 128))],
        out_shape=[jax.ShapeDtypeStruct((t, d), F32), jax.ShapeDtypeStruct((t, 4 * d), BF16),
                   jax.ShapeDtypeStruct((1, d), F32), jax.ShapeDtypeStruct((1, 128), F32),
                   jax.ShapeDtypeStruct((1, 128), F32)],
        scratch=[pltpu.VMEM((N_DEV, d, bw), BF16), pltpu.VMEM((tm, dkw), F32), pltpu.SemaphoreType.DMA((1,))],
        vmem=V7X_VMEM_LIMIT,
    )(x0, dx1, dud, duf, dub, qraw, kraw, dq, dk, dv, dgates, g_mix, w_in, qg, kg, cos, sin)


def _attn_fwd(qat, ka, vta):
    nh, _, t = qat.shape
    nkv = ka.shape[0]
    rep = nh // nkv
    hd = HEAD_DIM
    vr = vta.shape[1]
    tq = tk = KV_TILE

    def body(qt_ref, k_ref, vt_ref, ot_ref, lse_ref, m_scr, acc_scr):
        j = pl.program_id(2)

        @pl.when(j == 0)
        def _():
            m_scr[...] = jnp.full(m_scr.shape, NEG_BIG, F32)
            acc_scr[...] = jnp.zeros_like(acc_scr)

        k = k_ref[0]
        vt = vt_ref[0]
        strips = [(r, c) for r in range(rep) for c in range(0, tq, QUERY_STRIP)]
        scores = lambda r, c: _dot(k, qt_ref[r, :, c:c + QUERY_STRIP])
        st_next = scores(*strips[0])
        for n, (r, c) in enumerate(strips):
            st = st_next
            if n + 1 < len(strips):
                st_next = scores(*strips[n + 1])
            cols = slice(c, c + QUERY_STRIP)
            m_prev = m_scr[r, :, cols]
            m_next = jnp.maximum(m_prev, jnp.max(st, axis=0, keepdims=True))
            pt = jnp.exp(st - m_next).astype(BF16)
            acc_scr[r, :, cols] = jnp.exp(m_prev - m_next) * acc_scr[r, :, cols] + _dot(vt, pt)
            m_scr[r, :, cols] = m_next

        @pl.when(j == pl.num_programs(2) - 1)
        def _():
            for r in range(rep):
                l = acc_scr[r, hd:hd + 1, :]
                ot_ref[hd * r:hd * (r + 1), :] = acc_scr[r, 0:hd, :] / l
                lse_ref[0, r:r + 1, :] = m_scr[r] + jnp.log(l)

    return _pcall(
        body, name="attn_fwd", grid=(nkv, t // tq, t // tk),
        in_specs=[pl.BlockSpec((rep, 128, tq), lambda g, i, j: (g, 0, i)),
                  pl.BlockSpec((1, tk, 128), lambda g, i, j: (g, j, 0)),
                  pl.BlockSpec((1, vr, tk), lambda g, i, j: (g, 0, j))],
        out_specs=[pl.BlockSpec((rep * hd, tq), lambda g, i, j: (g, i)),
                   pl.BlockSpec((1, rep, tq), lambda g, i, j: (g, 0, i))],
        out_shape=[jax.ShapeDtypeStruct((nh * hd, t), F32), jax.ShapeDtypeStruct((nkv, rep, t), F32)],
        scratch=[pltpu.VMEM((rep, 1, tq), F32), pltpu.VMEM((rep, vr, tq), F32)],
        vmem=V7X_VMEM_LIMIT,
    )(qat, ka, vta)


def _attn_bwd(qat, ka, kt, va, dot, ot, lse_row):
    nh, _, t = qat.shape
    nkv = ka.shape[0]
    rep = nh // nkv
    hd = HEAD_DIM
    tq = tk = KV_TILE

    def body(qt_ref, k_ref, kt_ref, v_ref, dot_ref, ot_ref, lse_ref, dk_ref, dv_ref, dqt_ref):
        j = pl.program_id(1)
        i = pl.program_id(2)

        @pl.when(jnp.logical_and(j == 0, i == 0))
        def _():
            dqt_ref[...] = jnp.zeros_like(dqt_ref)

        @pl.when(i == 0)
        def _():
            dk_ref[...] = jnp.zeros_like(dk_ref)
            dv_ref[...] = jnp.zeros_like(dv_ref)

        k = k_ref[0]
        kt = kt_ref[0]
        v = v_ref[0, :, 0:hd]
        cols = pl.ds(pl.multiple_of(i * tq, tq), tq)
        dk = jnp.zeros((tk, 128), F32)
        dv = jnp.zeros((tk, hd), F32)
        products = lambda r: (_dot(k, qt_ref[r]), _dot(v, dot_ref[hd * r:hd * (r + 1), :]))
        nxt = products(0)
        for r in range(rep):
            st, dpt = nxt
            if r + 1 < rep:
                nxt = products(r + 1)
            heads = slice(hd * r, hd * (r + 1))
            qt = qt_ref[r]
            dot_r = dot_ref[heads, :]
            delta = jnp.sum(dot_r.astype(F32) * ot_ref[heads, :], axis=0, keepdims=True)
            pt = jnp.exp(st - lse_ref[0, r:r + 1, :])
            dst = (pt * (dpt - delta)).astype(BF16)
            dv = dv + _dot_nt(pt.astype(BF16), dot_r)
            dk = dk + _dot_nt(dst, qt)
            dqt_ref[heads, cols] += _dot(kt, dst)
        dk_ref[0] += dk
        dv_ref[0] += dv

    return _pcall(
        body, name="attn_bwd", grid=(nkv, t // tk, t // tq),
        in_specs=[pl.BlockSpec((rep, 128, tq), lambda g, j, i: (g, 0, i)),
                  pl.BlockSpec((1, tk, 128), lambda g, j, i: (g, j, 0)),
                  pl.BlockSpec((1, hd, tk), lambda g, j, i: (g, 0, j)),
                  pl.BlockSpec((1, tk, 128), lambda g, j, i: (g, j, 0)),
                  pl.BlockSpec((rep * hd, tq), lambda g, j, i: (g, i)),
                  pl.BlockSpec((rep * hd, tq), lambda g, j, i: (g, i)),
                  pl.BlockSpec((1, rep, tq), lambda g, j, i: (g, 0, i))],
        out_specs=[pl.BlockSpec((1, tk, 128), lambda g, j, i: (g, j, 0)),
                   pl.BlockSpec((1, tk, hd), lambda g, j, i: (g, j, 0)),
                   pl.BlockSpec((rep * hd, t), lambda g, j, i: (g, 0))],
        out_shape=[jax.ShapeDtypeStruct((nkv, t, 128), F32), jax.ShapeDtypeStruct((nkv, t, hd), F32),
                   jax.ShapeDtypeStruct((nh * hd, t), F32)],
        vmem=V7X_VMEM_LIMIT,
    )(qat, ka, kt, va, dot, ot, lse_row)


def _riding_exchange(refs, exchange, n_in, n_out, first_step, last_step):
    if exchange is None:
        return refs
    x_ref, out_ref = refs[n_in], refs[n_in + 1 + n_out]
    sems = refs[-3:]

    @pl.when(first_step)
    def _():
        _start_all(*_exchange_copies(x_ref, out_ref, *sems, exchange[1]))

    @pl.when(last_step)
    def _():
        _wait_all(*_exchange_copies(x_ref, out_ref, *sems, exchange[1]))

    return refs[:n_in] + refs[n_in + 1:n_in + 1 + n_out] + refs[n_in + 2 + n_out:-3]


def _segmented_scan(src_re, src_im, dst_re, dst_im, lam_re, lam_im, pow_re, pow_im, carry_re, carry_im,
                    end_re, end_im, in_re, in_im, lanes, descending, conj):
    tc = src_re.shape[0]
    seg = tc // 8
    width = lanes.size
    sign = -1.0 if conj else 1.0
    rows_of = lambda q: pl.ds(8 * (seg - 1 - q if descending else q), 8)
    lr = jnp.broadcast_to(lam_re[:, lanes], (8, width))
    li = jnp.broadcast_to(sign * lam_im[:, lanes], (8, width))
    xr = jnp.zeros((8, width), F32)
    xi = jnp.zeros((8, width), F32)
    for q in range(seg):
        rows = rows_of(q)
        xr, xi = (lr * xr - li * xi) + src_re[rows, lanes], (lr * xi + li * xr) + src_im[rows, lanes]
        dst_re[rows, lanes] = xr
        dst_im[rows, lanes] = xi
    end_re[:, lanes] = xr
    end_im[:, lanes] = xi
    sr = pow_re[seg - 1:seg, lanes]
    si = sign * pow_im[seg - 1:seg, lanes]
    cr = carry_re[:, lanes]
    ci = carry_im[:, lanes]
    for s in range(8):
        se = 7 - s if descending else s
        in_re[se:se + 1, lanes] = cr
        in_im[se:se + 1, lanes] = ci
        cr, ci = (end_re[se:se + 1, lanes] + (sr * cr - si * ci)), (end_im[se:se + 1, lanes] + (sr * ci + si * cr))
    carry_re[:, lanes] = cr
    carry_im[:, lanes] = ci
    ir = in_re[:, lanes]
    ii = in_im[:, lanes]
    for q in range(seg):
        rows = rows_of(q)
        pr = pow_re[q:q + 1, lanes]
        pi = sign * pow_im[q:q + 1, lanes]
        dst_re[rows, lanes] = dst_re[rows, lanes] + (pr * ir - pi * ii)
        dst_im[rows, lanes] = dst_im[rows, lanes] + (pr * ii + pi * ir)


def _diag_tiles(gn):
    rows_per_tile = DIAG_TILE // (SSM_STATE // SSM_GROUP)
    return [(slice(rows_per_tile * j, rows_per_tile * (j + 1)), slice(DIAG_TILE * j, DIAG_TILE * (j + 1)))
            for j in range(gn // DIAG_TILE)]


def _ssm_scan_fwd(ub, lam_re, lam_im, pow_re, pow_im, bb_re, bb_im, cc_re, cc_im, exchange=None):
    t, w = ub.shape
    gn = lam_re.shape[-1]
    tc = ROW_TILE
    cl = min(gn, SCAN_LANES)
    nblk = t // tc
    tiles = _diag_tiles(gn)

    def body(*refs):
        first = jnp.logical_and(pl.program_id(0) == 0, pl.program_id(1) == 0)
        last = jnp.logical_and(pl.program_id(0) == 1, pl.program_id(1) == nblk - 1)
        (u_ref, lr_ref, li_ref, pr_ref, pi_ref, br_ref, bi_ref, cr_ref, ci_ref, y_ref, xr_ref, xi_ref,
         bur_scr, bui_scr, cr_scr, ci_scr, er_scr, ei_scr, nr_scr, ni_scr) = _riding_exchange(
             refs, exchange, 9, 3, first, last)

        @pl.when(pl.program_id(1) == 0)
        def _():
            cr_scr[...] = jnp.zeros_like(cr_scr)
            ci_scr[...] = jnp.zeros_like(ci_scr)

        for rows, lanes in tiles:
            u_j = u_ref[:, rows]
            bur_scr[:, lanes] = _dot(u_j, br_ref[0, rows, lanes])
            bui_scr[:, lanes] = _dot(u_j, bi_ref[0, rows, lanes])
        for descending in (False, True):
            @pl.when(pl.program_id(0) == int(descending))
            def _(descending=descending):
                for c0 in range(0, gn, cl):
                    _segmented_scan(bur_scr, bui_scr, xr_ref.at[0], xi_ref.at[0], lr_ref.at[0], li_ref.at[0],
                                    pr_ref.at[0], pi_ref.at[0], cr_scr, ci_scr, er_scr, ei_scr, nr_scr, ni_scr,
                                    pl.ds(c0, cl), descending, conj=False)
        for rows, lanes in tiles:
            y_ref[0, :, rows] = (_dot(xr_ref[0, :, lanes].astype(BF16), cr_ref[0, lanes, rows])
                                 - _dot(xi_ref[0, :, lanes].astype(BF16), ci_ref[0, lanes, rows]))

    blk = lambda dd, i: jnp.where(dd == 0, i, nblk - 1 - i)
    row = lambda width: pl.BlockSpec((1, tc, width), lambda dd, i: (dd, blk(dd, i), 0))
    per_dir = lambda a, b: pl.BlockSpec((1, a, b), lambda dd, i: (dd, 0, 0))
    extra = exchange is not None
    return _pcall(
        body, name="ssm_scan_fwd", grid=(2, nblk),
        in_specs=[pl.BlockSpec((tc, w), lambda dd, i: (blk(dd, i), 0)), per_dir(1, gn), per_dir(1, gn),
                  per_dir(tc // 8, gn), per_dir(tc // 8, gn),
                  per_dir(w, gn), per_dir(w, gn), per_dir(gn, w), per_dir(gn, w)] + [_ANY] * extra,
        out_specs=[row(w), row(gn), row(gn)] + [_ANY] * extra,
        out_shape=[jax.ShapeDtypeStruct((2, t, w), F32), jax.ShapeDtypeStruct((2, t, gn), F32),
                   jax.ShapeDtypeStruct((2, t, gn), F32)] + ([_exchange_out_shape(*exchange)] if extra else []),
        scratch=[pltpu.VMEM((tc, gn), F32), pltpu.VMEM((tc, gn), F32), pltpu.VMEM((1, gn), F32),
                 pltpu.VMEM((1, gn), F32)] + [pltpu.VMEM((8, gn), F32)] * 4 + _EXCHANGE_SEMS * extra,
        vmem=V7X_VMEM_LIMIT,
    )(ub, lam_re, lam_im, pow_re, pow_im, bb_re, bb_im, cc_re, cc_im, *([exchange[0]] if extra else []))


def _ssm_scan_bwd(dyb, ub, xs_re, xs_im, lam_re, lam_im, pow_re, pow_im, cct_re, cct_im, bbt_re, bbt_im,
                  exchange=None):
    t, w = dyb.shape
    gn = lam_re.shape[-1]
    tc = ROW_TILE
    cl = min(gn, SCAN_LANES)
    nblk = t // tc
    tiles = _diag_tiles(gn)

    def body(*refs):
        i = pl.program_id(1)
        first = jnp.logical_and(pl.program_id(0) == 0, i == 0)
        last = jnp.logical_and(pl.program_id(0) == 1, i == nblk - 1)
        (dy_ref, u_ref, xr_ref, xi_ref, hr_ref, hi_ref, lr_ref, li_ref, pr_ref, pi_ref, ctr_ref, cti_ref, btr_ref,
         bti_ref, du_ref, dlr_ref, dli_ref, dbr_ref, dbi_ref, dcr_ref, dci_ref,
         gxr_scr, gxi_scr, cr_scr, ci_scr, ar_scr, ai_scr, er_scr, ei_scr, nr_scr, ni_scr) = _riding_exchange(
             refs, exchange, 14, 7, first, last)

        @pl.when(i == 0)
        def _():
            for ref in (cr_scr, ci_scr, ar_scr, ai_scr, dbr_ref, dbi_ref, dcr_ref, dci_ref):
                ref[...] = jnp.zeros_like(ref)

        for rows, lanes in tiles:
            dy_j = dy_ref[:, rows]
            gxr_scr[:, lanes] = _dot(dy_j, ctr_ref[0, rows, lanes])
            gxi_scr[:, lanes] = -_dot(dy_j, cti_ref[0, rows, lanes])
        first_block = i == nblk - 1
        sublane = lax.broadcasted_iota(jnp.int32, (8, 1), 0)

        def lam_gradient(state_descending):
            for c0 in range(0, gn, 512):
                lanes = pl.ds(c0, 512)
                if state_descending:
                    cur, prev, edge, src = pl.ds(0, tc - 8), pl.ds(8, tc - 8), pl.ds(tc - 8, 8), pl.ds(0, 8)
                    halo_at, halo_row, shift = 7, 0, 7
                else:
                    cur, prev, edge, src = pl.ds(8, tc - 8), pl.ds(0, tc - 8), pl.ds(0, 8), pl.ds(tc - 8, 8)
                    halo_at, halo_row, shift = 0, 7, 1
                halo_r = jnp.where(first_block, 0.0, hr_ref[0, halo_row:halo_row + 1, lanes])
                halo_i = jnp.where(first_block, 0.0, hi_ref[0, halo_row:halo_row + 1, lanes])
                xer = jnp.where(sublane == halo_at, halo_r, pltpu.roll(xr_ref[0, src, lanes], shift, 0))
                xei = jnp.where(sublane == halo_at, halo_i, pltpu.roll(xi_ref[0, src, lanes], shift, 0))
                gr, gi = gxr_scr[cur, lanes], gxi_scr[cur, lanes]
                xpr, xpi = xr_ref[0, prev, lanes], xi_ref[0, prev, lanes]
                ger, gei = gxr_scr[edge, lanes], gxi_scr[edge, lanes]
                ar_scr[:, lanes] += (jnp.sum(gr * xpr + gi * xpi, axis=0, keepdims=True)
                                     + jnp.sum(ger * xer + gei * xei, axis=0, keepdims=True))
                ai_scr[:, lanes] += (jnp.sum(gi * xpr - gr * xpi, axis=0, keepdims=True)
                                     + jnp.sum(gei * xer - ger * xei, axis=0, keepdims=True))

        for descending in (True, False):
            @pl.when(pl.program_id(0) == int(not descending))
            def _(descending=descending):
                for c0 in range(0, gn, cl):
                    _segmented_scan(gxr_scr, gxi_scr, gxr_scr, gxi_scr, lr_ref.at[0], li_ref.at[0], pr_ref.at[0],
                                    pi_ref.at[0], cr_scr, ci_scr, er_scr, ei_scr, nr_scr, ni_scr, pl.ds(c0, cl),
                                    descending, conj=True)
                lam_gradient(state_descending=not descending)
        dlr_ref[0] = ar_scr[...]
        dli_ref[0] = ai_scr[...]
        for rows, lanes in tiles:
            grb = gxr_scr[:, lanes].astype(BF16)
            gib = gxi_scr[:, lanes].astype(BF16)
            du_ref[0, :, rows] = _dot(grb, btr_ref[0, lanes, rows]) + _dot(gib, bti_ref[0, lanes, rows])
            u_j = u_ref[:, rows]
            dy_j = dy_ref[:, rows]
            dbr_ref[0, rows, :] += _dot_tn(u_j, grb)
            dbi_ref[0, rows, :] += _dot_tn(u_j, gib)
            dcr_ref[0, rows, :] += _dot_tn(dy_j, xr_ref[0, :, lanes].astype(BF16))
            dci_ref[0, rows, :] -= _dot_tn(dy_j, xi_ref[0, :, lanes].astype(BF16))

    blk = lambda dd, i: jnp.where(dd == 0, nblk - 1 - i, i)
    rev = lambda width: pl.BlockSpec((1, tc, width), lambda dd, i: (dd, blk(dd, i), 0))
    halo_blk = lambda dd, i: jnp.where(dd == 0, jnp.maximum(blk(dd, i) * (tc // 8) - 1, 0),
                                       jnp.minimum((blk(dd, i) + 1) * (tc // 8), t // 8 - 1))
    halo = pl.BlockSpec((1, 8, gn), lambda dd, i: (dd, halo_blk(dd, i), 0))
    per_dir = lambda a, b: pl.BlockSpec((1, a, b), lambda dd, i: (dd, 0, 0))
    extra = exchange is not None
    return _pcall(
        body, name="ssm_scan_bwd", grid=(2, nblk),
        in_specs=[pl.BlockSpec((tc, w), lambda dd, i: (blk(dd, i), 0)),
                  pl.BlockSpec((tc, w), lambda dd, i: (blk(dd, i), 0)), rev(gn), rev(gn), halo, halo,
                  per_dir(1, gn), per_dir(1, gn), per_dir(tc // 8, gn), per_dir(tc // 8, gn),
                  per_dir(w, gn), per_dir(w, gn), per_dir(gn, w), per_dir(gn, w)]
        + [_ANY] * extra,
        out_specs=[rev(w), per_dir(1, gn), per_dir(1, gn)] + [per_dir(w, DIAG_TILE)] * 4 + [_ANY] * extra,
        out_shape=[jax.ShapeDtypeStruct((2, t, w), F32), jax.ShapeDtypeStruct((2, 1, gn), F32),
                   jax.ShapeDtypeStruct((2, 1, gn), F32)] + [jax.ShapeDtypeStruct((2, w, DIAG_TILE), F32)] * 4
        + ([_exchange_out_shape(*exchange)] if extra else []),
        scratch=[pltpu.VMEM((tc, gn), F32), pltpu.VMEM((tc, gn), F32)] + [pltpu.VMEM((1, gn), F32)] * 4
        + [pltpu.VMEM((8, gn), F32)] * 4 + _EXCHANGE_SEMS * extra,
        vmem=V7X_VMEM_LIMIT,
    )(dyb, ub, xs_re, xs_im, xs_re, xs_im, lam_re, lam_im, pow_re, pow_im, cct_re, cct_im, bbt_re, bbt_im,
      *([exchange[0]] if extra else []))


def _matmul_tn(a, b, name, a_is_transposed=False):
    nb, t, n = b.shape
    m = a.shape[1] if a_is_transposed else a.shape[2]
    shared = a.shape[0] == 1
    bm, bn, tk = min(m, 1024), min(n, 1024), KV_TILE

    def body(a_ref, b_ref, o_ref):
        @pl.when(pl.program_id(3) == 0)
        def _():
            o_ref[...] = jnp.zeros_like(o_ref)

        mul = _dot if a_is_transposed else _dot_tn
        o_ref[0] += mul(a_ref[0].astype(BF16), b_ref[0].astype(BF16))

    a_spec = (pl.BlockSpec((1, bm, tk), lambda z, i, j, k: (0 if shared else z, i, k)) if a_is_transposed else
              pl.BlockSpec((1, tk, bm), lambda z, i, j, k: (0 if shared else z, k, i)))
    return _pcall(
        body, name=name, grid=(nb, m // bm, n // bn, t // tk),
        in_specs=[a_spec,
                  pl.BlockSpec((1, tk, bn), lambda z, i, j, k: (z, k, j))],
        out_specs=pl.BlockSpec((1, bm, bn), lambda z, i, j, k: (z, i, j)),
        out_shape=jax.ShapeDtypeStruct((nb, m, n), F32), vmem=V7X_VMEM_LIMIT,
    )(a, b)


def _reduce_adamw(gparts, p, m, v, name):
    rows, width = p.shape
    tr = max(k for k in range(16, 513, 16) if rows % k == 0)

    def body(g_ref, p_ref, m_ref, v_ref, go_ref, d_ref, mo_ref, vo_ref):
        g = g_ref[0].astype(F32)
        for k in range(1, N_DEV):
            g = g + g_ref[k].astype(F32)
        go_ref[...] = g
        mm = ADAM_B1 * m_ref[...] + (1.0 - ADAM_B1) * g
        vv = ADAM_B2 * v_ref[...] + (1.0 - ADAM_B2) * (g * g)
        m_hat = mm / (1.0 - ADAM_B1 ** ADAM_STEP)
        v_hat = vv / (1.0 - ADAM_B2 ** ADAM_STEP)
        d_ref[...] = -ADAM_LR * (m_hat / (jnp.sqrt(v_hat) + ADAM_EPS) + ADAM_WD * p_ref[...])
        mo_ref[...] = mm
        vo_ref[...] = vv

    spec = pl.BlockSpec((tr, width), lambda i: (i, 0))
    out = jax.ShapeDtypeStruct((rows, width), F32)
    return _pcall(
        body, name=name, grid=(rows // tr,),
        in_specs=[pl.BlockSpec((N_DEV, tr, width), lambda i: (0, i, 0)), spec, spec, spec],
        out_specs=[spec, spec, spec, spec], out_shape=[out, out, out, out], vmem=V7X_VMEM_LIMIT,
    )(gparts, p, m, v)


def _peer(k):
    x, y, c = lax.axis_index("x"), lax.axis_index("y"), lax.axis_index("c")
    return (x ^ ((k >> 2) & 1), y ^ ((k >> 1) & 1), c ^ (k & 1))


def _my_index():
    return 4 * lax.axis_index("x") + 2 * lax.axis_index("y") + lax.axis_index("c")


def _exchange_copies(x_ref, out_ref, send_sems, recv_sems, local_sem, scatter, first_sem=0):
    me = _my_index()
    local = pltpu.make_async_copy(x_ref.at[me] if scatter else x_ref, out_ref.at[me], local_sem)
    copies = []
    for k in range(1, N_DEV):
        peer = _peer(k)
        src = x_ref.at[4 * peer[0] + 2 * peer[1] + peer[2]] if scatter else x_ref
        copies.append(pltpu.make_async_remote_copy(
            src_ref=src, dst_ref=out_ref.at[me], send_sem=send_sems.at[first_sem + k - 1],
            recv_sem=recv_sems.at[first_sem + k - 1], device_id=peer, device_id_type=pl.DeviceIdType.MESH))
    return local, copies


def _start_all(local, copies):
    local.start()
    for cp in copies:
        cp.start()


def _wait_all(local, copies):
    for cp in copies:
        cp.wait_recv()
    for cp in copies:
        cp.wait_send()
    local.wait()


def _exchange_out_shape(x, scatter):
    return jax.ShapeDtypeStruct((N_DEV,) + tuple(x.shape[1:] if scatter else x.shape), x.dtype)


_EXCHANGE_SEMS = [pltpu.SemaphoreType.DMA((N_DEV - 1,)), pltpu.SemaphoreType.DMA((N_DEV - 1,)),
                  pltpu.SemaphoreType.DMA(())]


def _exchange(ops, name):
    n = len(ops)

    def body(*refs):
        x_refs, out_refs = refs[:n], refs[n:2 * n]
        send_sems, recv_sems, local_sems = refs[2 * n:]
        started = []
        for q, (_, scatter) in enumerate(ops):
            local, copies = _exchange_copies(x_refs[q], out_refs[q], send_sems, recv_sems, local_sems.at[q],
                                             scatter, first_sem=q * (N_DEV - 1))
            _start_all(local, copies)
            started.append((local, copies))
        for local, copies in started:
            _wait_all(local, copies)

    return pl.pallas_call(
        body, name=name, in_specs=[_ANY] * n, out_specs=[_ANY] * n,
        out_shape=[_exchange_out_shape(x, scatter) for x, scatter in ops],
        scratch_shapes=[pltpu.SemaphoreType.DMA((n * (N_DEV - 1),)), pltpu.SemaphoreType.DMA((n * (N_DEV - 1),)),
                        pltpu.SemaphoreType.DMA((n,))],
    )(*[x for x, _ in ops])


def _to_shards(full, axis):
    r, c = full.shape
    if axis == 0:
        return full.reshape(N_DEV, r // N_DEV, c)
    return full.reshape(r, N_DEV, c // N_DEV).transpose(1, 0, 2)


def _from_shards(shards, axis):
    _, r, c = shards.shape
    if axis == 0:
        return shards.reshape(N_DEV * r, c)
    return shards.transpose(1, 0, 2).reshape(r, N_DEV * c)


def _pack_rows(parts, lead):
    flat = []
    for p in parts:
        p = p.reshape(p.shape[:lead] + (-1, PACK_W))
        pad = _round_up(p.shape[lead], 16) - p.shape[lead]
        flat.append(jnp.pad(p, [(0, 0)] * lead + [(0, pad), (0, 0)]) if pad else p)
    return jnp.concatenate(flat, axis=lead)


def _unpack_rows(packed, shapes):
    lead = packed.shape[:-2]
    out, off = [], 0
    for shp in shapes:
        rows = math.prod(shp) // PACK_W
        out.append(packed[..., off:off + rows, :].reshape(lead + tuple(shp)))
        off += _round_up(rows, 16)
    return out


def _pack_flat(parts):
    flat = jnp.concatenate([p.reshape(-1) for p in parts])
    n = flat.shape[0]
    flat = jnp.pad(flat, (0, _round_up(n, 16 * PACK_W) - n))
    return flat.reshape(-1, PACK_W)


def _unpack(packed, shapes):
    flat = packed.reshape(-1)
    out, off = [], 0
    for shp in shapes:
        n = math.prod(shp)
        out.append(flat[off:off + n].reshape(shp))
        off += n
    return out


def _ssm_discretize(a_re, a_im, log_dt, b_re, b_im):
    dt = jnp.exp(log_dt)[..., None]
    lam_re = jnp.minimum(a_re, EIG_RE_MAX)
    lam_im = a_im
    mag = jnp.exp(lam_re * dt)
    ang = lam_im * dt
    lb_re = mag * jnp.cos(ang)
    lb_im = mag * jnp.sin(ang)
    num_re = lb_re - 1.0
    num_im = lb_im
    den = lam_re * lam_re + lam_im * lam_im
    f_re = (num_re * lam_re + num_im * lam_im) / den
    f_im = (num_im * lam_re - num_re * lam_im) / den
    bb_re = f_re[..., None] * b_re - f_im[..., None] * b_im
    bb_im = f_re[..., None] * b_im + f_im[..., None] * b_re
    return lb_re, lb_im, bb_re, bb_im


def _ssm_powers(a_re, a_im, log_dt, count):
    dt = jnp.exp(log_dt)[:, None, :, None]
    k = jnp.arange(1, count + 1, dtype=F32)[None, :, None, None]
    mag = jnp.exp(k * (jnp.minimum(a_re, EIG_RE_MAX)[:, None] * dt))
    ang = k * (a_im[:, None] * dt)
    shape = (a_re.shape[0], count, -1)
    return (mag * jnp.cos(ang)).reshape(shape), (mag * jnp.sin(ang)).reshape(shape)


def _interleave(a, inverse=False):
    lead, (t, width) = a.shape[:-2], a.shape[-2:]
    seg = ROW_TILE // 8
    shape = lead + (t // ROW_TILE,) + ((seg, 8) if inverse else (8, seg)) + (width,)
    return jnp.swapaxes(a.reshape(shape), -3, -2).reshape(a.shape)


def _block_diag(blocks):
    two, g, a, b = blocks.shape
    tiled = jnp.tile(blocks.reshape(two, g * a, b), (1, 1, g))
    row_group = lax.broadcasted_iota(jnp.int32, (g * a, g * b), 0) // a
    col_group = lax.broadcasted_iota(jnp.int32, (g * a, g * b), 1) // b
    return jnp.where(row_group == col_group, tiled, 0.0).astype(BF16)


def _diag_blocks(tiles):
    two, w, _ = tiles.shape
    per = DIAG_TILE // SSM_STATE
    t6 = tiles.reshape(two, w // (per * SSM_GROUP), per, SSM_GROUP, per, SSM_STATE)
    return jnp.einsum("zjqpqn->zjqpn", t6).reshape(two, w // SSM_GROUP, SSM_GROUP, SSM_STATE)


def _rope_tables(t, n_valid):
    pos = jnp.arange(t)
    real = jnp.logical_and(pos >= N_META, pos < n_valid)
    idx = jnp.where(real, pos - N_META, 0)
    row_id = (idx // GRID_W).astype(F32)
    col_id = (idx % GRID_W).astype(F32)
    pairs_per_axis = HEAD_DIM // 4
    inv_freq = ROPE_THETA ** (-jnp.arange(pairs_per_axis, dtype=F32) / pairs_per_axis)
    ang = jnp.concatenate([row_id[:, None] * inv_freq, col_id[:, None] * inv_freq], axis=-1)
    ang = jnp.where(real[:, None], ang, 0.0)
    cos = jnp.repeat(jnp.cos(ang), 2, axis=-1)
    sin = jnp.sin(ang)
    sin = jnp.stack([-sin, sin], axis=-1).reshape(t, HEAD_DIM)
    return jnp.tile(cos, (1, 2)), jnp.tile(sin, (1, 2))


def _local_step(x, loss_target, big, small, comm=None):
    s_len, d = x.shape
    n_valid = s_len + N_META
    t = _round_up(n_valid, KV_TILE)
    du = d // 2
    groups = du // SSM_GROUP
    nh = d // HEAD_DIM
    nkv = nh // KV_REP
    pad = t - n_valid

    x0 = jnp.concatenate([big["meta_tokens"].astype(F32), x, jnp.zeros((pad, d), F32)], axis=0)
    tgt = jnp.concatenate([jnp.zeros((N_META, d), F32), loss_target, jnp.zeros((pad, d), F32)], axis=0)
    cos, sin = _rope_tables(t, n_valid)
    g_mix = small["norm_mix_g"].reshape(1, d)
    g_mlp = small["norm_mlp_g"].reshape(1, d)
    g_fin = small["norm_final_g"].reshape(1, d)
    qg = jnp.tile(small["q_norm_g"].reshape(1, HEAD_DIM), (1, 2))
    kg = jnp.tile(small["k_norm_g"].reshape(1, HEAD_DIM), (1, 2))
    ssm_d = small["ssm_d"].reshape(1, du)
    b_glu = small["b_glu"].reshape(1, du)

    ssm_in = tuple(small[n][0] for n in ("ssm_a_re", "ssm_a_im", "ssm_log_dt", "ssm_b_re", "ssm_b_im"))
    (lb_re, lb_im, bbar_re, bbar_im), disc_vjp = jax.vjp(_ssm_discretize, *ssm_in)
    lam_re = lb_re.reshape(2, 1, groups * SSM_STATE)
    lam_im = lb_im.reshape(2, 1, groups * SSM_STATE)
    pow_re, pow_im = _ssm_powers(*ssm_in[0:3], ROW_TILE // 8)
    bb_re = _block_diag(bbar_re.transpose(0, 1, 3, 2))
    bb_im = _block_diag(bbar_im.transpose(0, 1, 3, 2))
    c_re, c_im = small["ssm_c_re"][0], small["ssm_c_im"][0]
    cct_re = _block_diag(c_re)
    cct_im = _block_diag(c_im)
    cc_re = cct_re.transpose(0, 2, 1)
    cc_im = cct_im.transpose(0, 2, 1)
    bbt_re = bb_re.transpose(0, 2, 1)
    bbt_im = bb_im.transpose(0, 2, 1)
    scan_w = (lam_re, lam_im, pow_re, pow_im)

    h, u, ub, qraw, kraw, qa, ka, va, gates = _in_proj_fwd(x0, g_mix, big["w_in"], qg, kg, cos, sin, n_valid)
    ub = _interleave(ub)
    if comm is None:
        y2, xs_re, xs_im = _ssm_scan_fwd(ub, *scan_w, bb_re, bb_im, cc_re, cc_im)
    else:
        y2, xs_re, xs_im, late = _ssm_scan_fwd(ub, *scan_w, bb_re, bb_im, cc_re, cc_im,
                                               exchange=(comm["late_weights"], False))
        big = {**big, **comm["unpack_late_weights"](late)}
    y2 = _interleave(y2, inverse=True)
    yf, yb = y2[0], y2[1]
    qat = qa.transpose(0, 2, 1)
    kt = ka[:, :, 0:HEAD_DIM].transpose(0, 2, 1)
    vta = va[:, :, 0:VT_ROWS].transpose(0, 2, 1)
    yt_attn, lse = _attn_fwd(qat, ka, vta)
    mixer_w = (ssm_d, big["w_glu"], b_glu, big["w_ssm_proj"], big["w_attn_proj"], big["w_out"])
    x1 = _mixer_out_fwd(x0, u, yf, yb, yt_attn, gates, *mixer_w)

    dx1, loss8, dg_fin, dg_mlp, h2b, dab, hsqb, dx2b = _mlp_loss_fwd_bwd(
        x1, tgt, g_mlp, g_fin, big["w_mlp_in"], big["w_mlp_out"], n_valid)

    (dyb, dud, dyt_attn, dgates, zb, dglb, ysb, dasb, daab, mgb, dxb, d_ssm_d, d_b_glu) = _mixer_out_bwd(
        dx1, u, yf, yb, yt_attn, gates, *mixer_w)
    one = lambda a: a[None]
    grads = {}
    grads["w_glu"] = _matmul_tn(one(zb), one(dglb), "grad_w_glu")[0]
    grads["w_ssm_proj"] = _matmul_tn(one(ysb), one(dasb), "grad_w_ssm_proj")[0]
    grads["w_attn_proj"] = _matmul_tn(one(yt_attn), one(daab), "grad_w_attn_proj", a_is_transposed=True)[0]
    grads["w_out"] = _matmul_tn(one(mgb), one(dxb), "grad_w_out")[0]
    grads["w_mlp_in"] = _matmul_tn(one(h2b), one(dab), "grad_w_mlp_in")[0]
    grads["w_mlp_out"] = _matmul_tn(one(hsqb), one(dx2b), "grad_w_mlp_out")[0]
    dk, dv, dqt = _attn_bwd(qat, ka, kt, va, dyt_attn, yt_attn, lse)
    scan_args = (_interleave(dyb), ub, xs_re, xs_im, *scan_w, cct_re, cct_im, bbt_re, bbt_im)
    if comm is None:
        late_grad_parts = None
        du2, dlam_re, dlam_im, dbr, dbi, dcr, dci = _ssm_scan_bwd(*scan_args)
    else:
        du2, dlam_re, dlam_im, dbr, dbi, dcr, dci, late_grad_parts = _ssm_scan_bwd(
            *scan_args, exchange=(comm["pack_late_grads"](grads), True))
    du2 = _interleave(du2, inverse=True)
    dx0, dproj, dg_mix, dqg, dkg = _in_proj_bwd(x0, dx1, dud, du2[0], du2[1], qraw, kraw, dqt.T, dk, dv, dgates,
                                                g_mix, big["w_in"], qg, kg, cos, sin)

    grads["w_in"] = _matmul_tn(one(h), one(dproj), "grad_w_in")[0]
    grads["meta_tokens"] = dx0[0:N_META]
    dbb_re = _diag_blocks(dbr).transpose(0, 1, 3, 2)
    dbb_im = _diag_blocks(dbi).transpose(0, 1, 3, 2)
    dc_re, dc_im = _diag_blocks(dcr), _diag_blocks(dci)
    shape_gn = (2, groups, SSM_STATE)
    d_a_re, d_a_im, d_log_dt, d_b_re, d_b_im = disc_vjp(
        (dlam_re.reshape(shape_gn), dlam_im.reshape(shape_gn), dbb_re, dbb_im))
    grads.update({
        "norm_mix_g": dg_mix, "ssm_a_re": d_a_re[None], "ssm_a_im": d_a_im[None], "ssm_log_dt": d_log_dt[None],
        "ssm_b_re": d_b_re[None], "ssm_b_im": d_b_im[None], "ssm_c_re": dc_re[None], "ssm_c_im": dc_im[None],
        "ssm_d": d_ssm_d, "b_glu": d_b_glu,
        "q_norm_g": dqg[:, 0:HEAD_DIM] + dqg[:, HEAD_DIM:128], "k_norm_g": dkg[:, 0:HEAD_DIM] + dkg[:, HEAD_DIM:128],
        "norm_mlp_g": dg_mlp, "norm_final_g": dg_fin.reshape(d),
    })
    return loss8[0, 0], dx0[N_META:n_valid], grads, late_grad_parts


def kernel(x, meta_tokens, norm_mix_g, w_in, ssm_a_re, ssm_a_im, ssm_log_dt, ssm_b_re, ssm_b_im, ssm_c_re, ssm_c_im, ssm_d, w_glu, b_glu, q_norm_g, k_norm_g, w_ssm_proj, w_attn_proj, w_out, norm_mlp_g, w_mlp_in, w_mlp_out, norm_final_g, loss_target, m_meta_tokens, m_norm_mix_g, m_w_in, m_ssm_a_re, m_ssm_a_im, m_ssm_log_dt, m_ssm_b_re, m_ssm_b_im, m_ssm_c_re, m_ssm_c_im, m_ssm_d, m_w_glu, m_b_glu, m_q_norm_g, m_k_norm_g, m_w_ssm_proj, m_w_attn_proj, m_w_out, m_norm_mlp_g, m_w_mlp_in, m_w_mlp_out, m_norm_final_g, v_meta_tokens, v_norm_mix_g, v_w_in, v_ssm_a_re, v_ssm_a_im, v_ssm_log_dt, v_ssm_b_re, v_ssm_b_im, v_ssm_c_re, v_ssm_c_im, v_ssm_d, v_w_glu, v_b_glu, v_q_norm_g, v_k_norm_g, v_w_ssm_proj, v_w_attn_proj, v_w_out, v_norm_mlp_g, v_w_mlp_in, v_w_mlp_out, v_norm_final_g):
    w = dict(meta_tokens=meta_tokens, norm_mix_g=norm_mix_g, w_in=w_in, ssm_a_re=ssm_a_re, ssm_a_im=ssm_a_im, ssm_log_dt=ssm_log_dt, ssm_b_re=ssm_b_re, ssm_b_im=ssm_b_im, ssm_c_re=ssm_c_re, ssm_c_im=ssm_c_im, ssm_d=ssm_d, w_glu=w_glu, b_glu=b_glu, q_norm_g=q_norm_g, k_norm_g=k_norm_g, w_ssm_proj=w_ssm_proj, w_attn_proj=w_attn_proj, w_out=w_out, norm_mlp_g=norm_mlp_g, w_mlp_in=w_mlp_in, w_mlp_out=w_mlp_out, norm_final_g=norm_final_g)
    m = dict(meta_tokens=m_meta_tokens, norm_mix_g=m_norm_mix_g, w_in=m_w_in, ssm_a_re=m_ssm_a_re, ssm_a_im=m_ssm_a_im, ssm_log_dt=m_ssm_log_dt, ssm_b_re=m_ssm_b_re, ssm_b_im=m_ssm_b_im, ssm_c_re=m_ssm_c_re, ssm_c_im=m_ssm_c_im, ssm_d=m_ssm_d, w_glu=m_w_glu, b_glu=m_b_glu, q_norm_g=m_q_norm_g, k_norm_g=m_k_norm_g, w_ssm_proj=m_w_ssm_proj, w_attn_proj=m_w_attn_proj, w_out=m_w_out, norm_mlp_g=m_norm_mlp_g, w_mlp_in=m_w_mlp_in, w_mlp_out=m_w_mlp_out, norm_final_g=m_norm_final_g)
    v = dict(meta_tokens=v_meta_tokens, norm_mix_g=v_norm_mix_g, w_in=v_w_in, ssm_a_re=v_ssm_a_re, ssm_a_im=v_ssm_a_im, ssm_log_dt=v_ssm_log_dt, ssm_b_re=v_ssm_b_re, ssm_b_im=v_ssm_b_im, ssm_c_re=v_ssm_c_re, ssm_c_im=v_ssm_c_im, ssm_d=v_ssm_d, w_glu=v_w_glu, b_glu=v_b_glu, q_norm_g=v_q_norm_g, k_norm_g=v_k_norm_g, w_ssm_proj=v_w_ssm_proj, w_attn_proj=v_w_attn_proj, w_out=v_w_out, norm_mlp_g=v_norm_mlp_g, w_mlp_in=v_w_mlp_in, w_mlp_out=v_w_mlp_out, norm_final_g=v_norm_final_g)

    shard2d = {n: w[n].reshape(w[n].shape[-2:]) for n in BIG_WEIGHTS}
    big_shapes = [shard2d[n].shape for n in BIG_WEIGHTS]

    meta_hi = shard2d["meta_tokens"].astype(BF16)
    meta_res = shard2d["meta_tokens"] - meta_hi.astype(F32)
    meta_mid = meta_res.astype(BF16)
    meta_lo = (meta_res - meta_mid.astype(F32)).astype(BF16)
    shapes_of = lambda names: [shard2d[n].shape for n in names]

    def full_weights(names, shards):
        return {n: s if n in BLOCK_WEIGHTS else _from_shards(s, BIG_SHARD_AXIS[n]) for n, s in zip(names, shards)}

    early = _exchange([(_pack_rows([meta_hi, meta_mid, meta_lo, shard2d["w_in"].astype(BF16)], 0), False)],
                      "gather_early_weights")[0]
    shards = _unpack_rows(early, [meta_hi.shape] * 3 + shapes_of(EARLY_WEIGHTS[1:]))
    meta = [_from_shards(s, 1).astype(F32) for s in shards[0:3]]
    big = {"meta_tokens": (meta[0] + meta[1]) + meta[2], **full_weights(EARLY_WEIGHTS[1:], shards[3:])}
    small = {n: w[n] for n in SMALL_WEIGHTS}
    pack_grads = lambda names, grads: _pack_rows(
        [_to_shards(grads[n], BIG_SHARD_AXIS[n]) for n in names], 1).astype(BF16)
    comm = {
        "late_weights": _pack_rows([shard2d[n].astype(BF16) for n in LATE_WEIGHTS], 0),
        "unpack_late_weights": lambda g: full_weights(LATE_WEIGHTS, _unpack_rows(g, shapes_of(LATE_WEIGHTS))),
        "pack_late_grads": functools.partial(pack_grads, LATE_WEIGHTS),
    }

    loss, grad_x, grads, late_parts = _local_step(x[0], loss_target[0], big, small, comm)
    loss = lax.psum(loss, ("x", "y", "c"))

    early_parts, small_parts = _exchange([(pack_grads(EARLY_WEIGHTS, grads), True),
                                          (_pack_flat([grads[n] for n in SMALL_WEIGHTS]), False)],
                                         "exchange_last_grads")
    pk = lambda names, src: _pack_rows([src[n].reshape(shard2d[n].shape) for n in names], 0)
    pe, pl_ = functools.partial(pk, EARLY_WEIGHTS), functools.partial(pk, LATE_WEIGHTS)
    early_out = _reduce_adamw(early_parts, pe(w), pe(m), pe(v), "adamw_sharded_early")
    late_out = _reduce_adamw(late_parts, pl_(w), pl_(m), pl_(v), "adamw_sharded_late")
    small_shapes = [w[n].shape for n in SMALL_WEIGHTS]
    pf = lambda src: _pack_flat([src[n] for n in SMALL_WEIGHTS])
    small_out = _reduce_adamw(small_parts, pf(w), pf(m), pf(v), "adamw_replicated")

    results = []
    for kind in range(4):
        big_un = dict(zip(EARLY_WEIGHTS + LATE_WEIGHTS,
                          _unpack_rows(early_out[kind], shapes_of(EARLY_WEIGHTS))
                          + _unpack_rows(late_out[kind], shapes_of(LATE_WEIGHTS))))
        small_un = dict(zip(SMALL_WEIGHTS, _unpack(small_out[kind], small_shapes)))
        for n in ALL_WEIGHTS:
            results.append(big_un[n].reshape(w[n].shape) if n in big_un else small_un[n])
    return (loss, grad_x[None], *results)
```

```python
import functools
import math

import jax
import jax.numpy as jnp
from jax import lax
from jax.experimental import pallas as pl
from jax.experimental.pallas import tpu as pltpu

F32 = jnp.float32
BF16 = jnp.bfloat16

N_DEV = 8
N_META = 16
GRID_W = 64
SSM_GROUP = 16
SSM_STATE = 64
HEAD_DIM = 64
KV_REP = 4
ROPE_THETA = 10000.0
NORM_EPS = 1e-6
EIG_RE_MAX = -1e-4
ATTN_SCALE = HEAD_DIM ** -0.5

ADAM_LR = 0.001
ADAM_B1 = 0.9
ADAM_B2 = 0.999
ADAM_EPS = 1e-08
ADAM_WD = 0.01
ADAM_STEP = 10

ROW_TILE = 384
ROW_TILE_BWD = 256
QUERY_STRIP = 256
VT_ROWS = 80
MASK_BIAS = -1e30
SCAN_LANES = 512
DIAG_TILE = 256
KV_TILE = 768
PACK_W = 1024
V7X_VMEM_LIMIT = 56 * 1024 * 1024
NEG_BIG = -1e30

BIG_WEIGHTS = ("meta_tokens", "w_in", "w_glu", "w_ssm_proj", "w_attn_proj", "w_out", "w_mlp_in", "w_mlp_out")
BIG_SHARD_AXIS = {"meta_tokens": 1, "w_in": 1, "w_glu": 0, "w_ssm_proj": 1, "w_attn_proj": 0, "w_out": 0,
                  "w_mlp_in": 1, "w_mlp_out": 0}
BLOCK_WEIGHTS = ("w_in", "w_mlp_in", "w_mlp_out")
EARLY_WEIGHTS = ("meta_tokens", "w_in")
MIXER_WEIGHTS = ("w_glu", "w_ssm_proj", "w_attn_proj", "w_out")
MLP_WEIGHTS = ("w_mlp_in", "w_mlp_out")
LATE_WEIGHTS = MIXER_WEIGHTS + MLP_WEIGHTS
SMALL_WEIGHTS = ("norm_mix_g", "ssm_a_re", "ssm_a_im", "ssm_log_dt", "ssm_b_re", "ssm_b_im", "ssm_c_re",
                 "ssm_c_im", "ssm_d", "b_glu", "q_norm_g", "k_norm_g", "norm_mlp_g", "norm_final_g")
ALL_WEIGHTS = ("meta_tokens", "norm_mix_g", "w_in", "ssm_a_re", "ssm_a_im", "ssm_log_dt", "ssm_b_re", "ssm_b_im",
               "ssm_c_re", "ssm_c_im", "ssm_d", "w_glu", "b_glu", "q_norm_g", "k_norm_g", "w_ssm_proj",
               "w_attn_proj", "w_out", "norm_mlp_g", "w_mlp_in", "w_mlp_out", "norm_final_g")


def _round_up(n, m):
    return (n + m - 1) // m * m


def _pcall(body, *, name, grid, in_specs, out_specs, out_shape, scratch=(), vmem=None, **kw):
    params = pltpu.CompilerParams(dimension_semantics=("arbitrary",) * len(grid), vmem_limit_bytes=vmem)
    return pl.pallas_call(body, name=name, grid=grid, in_specs=in_specs, out_specs=out_specs, out_shape=out_shape,
                          scratch_shapes=list(scratch), compiler_params=params, **kw)


def _dot(a, b):
    return jnp.dot(a, b, preferred_element_type=F32)


def _dot_nt(a, b):
    return lax.dot_general(a, b, (((1,), (1,)), ((), ())), preferred_element_type=F32)


def _dot_tn(a, b):
    return lax.dot_general(a, b, (((0,), (0,)), ((), ())), preferred_element_type=F32)


def _full_spec(shape):
    nd = len(shape)
    return pl.BlockSpec(shape, lambda *_: (0,) * nd)


def _row_spec(tm, width):
    return pl.BlockSpec((tm, width), lambda i: (i, 0))


def _heads_spec(nh, tm):
    return pl.BlockSpec((nh, tm, HEAD_DIM), lambda i: (0, i, 0))


_ANY = pl.BlockSpec(memory_space=pl.ANY)


def _load_once(step, pairs, sem):
    @pl.when(step == 0)
    def _():
        copies = [pltpu.make_async_copy(src, dst, sem.at[k]) for k, (src, dst) in enumerate(pairs)]
        for cp in copies:
            cp.start()
        for cp in copies:
            cp.wait()


def _swap_pairs(x, even):
    n = x.shape[-1]
    return jnp.where(even, pltpu.roll(x, n - 1, 1), pltpu.roll(x, 1, 1))


def _gelu(y):
    return 0.5 * y * (1.0 + lax.erf(y * (1.0 / math.sqrt(2.0))))


def _gelu_grad(y):
    return 0.5 * (1.0 + lax.erf(y * (1.0 / math.sqrt(2.0)))) + y * jnp.exp(-0.5 * y * y) * (1.0 / math.sqrt(2.0 * math.pi))


def _in_proj_fwd(x0, g_mix, w_in, qg, kg, cos, sin, n_valid, exchange=None):
    t, d = x0.shape
    tm = ROW_TILE
    du, dk = d // 2, d // 4
    nh, nkv = d // HEAD_DIM, d // HEAD_DIM // KV_REP
    bw = w_in.shape[-1]
    assert bw == du and dk * 2 == bw

    def body(*refs):
        i = pl.program_id(0)
        (x_ref, g_ref, w_hbm, qg_ref, kg_ref, c_ref, s_ref,
         h_ref, u_ref, ub_ref, qraw_ref, kraw_ref, qa_ref, ka_ref, va_ref, gates_ref,
         w_ref, sem) = _riding_exchange(refs, exchange, 7, 9, i == 0, i == t // tm - 1)
        _load_once(i, [(w_hbm, w_ref)], sem)
        x = x_ref[...]
        r = lax.rsqrt(jnp.mean(x * x, axis=-1, keepdims=True) + NORM_EPS)
        h = ((x * r) * g_ref[...]).astype(BF16)
        h_ref[...] = h
        u = _dot(h, w_ref[0])
        u_ref[...] = u
        ub_ref[...] = u.astype(BF16)
        lane = lax.broadcasted_iota(jnp.int32, (tm, 128), 1)
        lo = lane < HEAD_DIM
        even = (lane & 1) == 0
        aug = lane == HEAD_DIM
        c = c_ref[...]
        s = s_ref[...]
        row = i * tm + lax.broadcasted_iota(jnp.int32, (tm, 1), 0)
        one = jnp.where(aug, 1.0, 0.0)
        key_bias = jnp.where(jnp.logical_and(aug, row >= n_valid), MASK_BIAS, 0.0)

        def norm_rope(blk, g128):
            sq = blk * blk
            ms_lo = jnp.sum(jnp.where(lo, sq, 0.0), axis=-1, keepdims=True) * (1.0 / HEAD_DIM)
            ms_hi = jnp.sum(jnp.where(lo, 0.0, sq), axis=-1, keepdims=True) * (1.0 / HEAD_DIM)
            rr = jnp.where(lo, lax.rsqrt(ms_lo + NORM_EPS), lax.rsqrt(ms_hi + NORM_EPS))
            qn = (blk * rr) * g128
            return qn * c + _swap_pairs(qn, even) * s

        def put_heads(ref, first, pair, extra):
            ref[first] = jnp.where(lo, pair, extra).astype(BF16)
            ref[first + 1] = jnp.where(lo, pltpu.roll(pair, HEAD_DIM, 1), extra).astype(BF16)

        for blk in range(2):
            qb = _dot(h, w_ref[1 + blk])
            qraw_ref[:, bw * blk:bw * (blk + 1)] = qb
            for a in range(bw // 128):
                put_heads(qa_ref, (bw // HEAD_DIM) * blk + 2 * a,
                          norm_rope(qb[:, 128 * a:128 * (a + 1)], qg_ref[...]) * ATTN_SCALE, one)
        kv = _dot(h, w_ref[3])
        kraw_ref[...] = kv[:, 0:dk]
        for a in range(nkv // 2):
            put_heads(ka_ref, 2 * a, norm_rope(kv[:, 128 * a:128 * (a + 1)], kg_ref[...]), key_bias)
            put_heads(va_ref, 2 * a, kv[:, dk + 128 * a:dk + 128 * (a + 1)], one)
        for blk in range(4):
            gates_ref[:, bw * blk:bw * (blk + 1)] = _dot(h, w_ref[4 + blk])

    heads = lambda n: pl.BlockSpec((n, tm, 128), lambda i: (0, i, 0))
    extra = exchange is not None
    return _pcall(
        body, name="in_proj_fwd", grid=(t // tm,),
        in_specs=[_row_spec(tm, d), _full_spec((1, d)), _ANY, _full_spec((1, 128)), _full_spec((1, 128)),
                  _row_spec(tm, 128), _row_spec(tm, 128)] + [_ANY] * extra,
        out_specs=[_row_spec(tm, d), _row_spec(tm, du), _row_spec(tm, du), _row_spec(tm, d), _row_spec(tm, dk),
                   heads(nh), heads(nkv), heads(nkv), _row_spec(tm, 2 * d)] + [_ANY] * extra,
        out_shape=[jax.ShapeDtypeStruct((t, d), BF16), jax.ShapeDtypeStruct((t, du), F32),
                   jax.ShapeDtypeStruct((t, du), BF16), jax.ShapeDtypeStruct((t, d), F32),
                   jax.ShapeDtypeStruct((t, dk), F32), jax.ShapeDtypeStruct((nh, t, 128), BF16),
                   jax.ShapeDtypeStruct((nkv, t, 128), BF16), jax.ShapeDtypeStruct((nkv, t, 128), BF16),
                   jax.ShapeDtypeStruct((t, 2 * d), F32)] + ([_exchange_out_shape(*exchange)] if extra else []),
        scratch=[pltpu.VMEM((N_DEV, d, bw), BF16), pltpu.SemaphoreType.DMA((1,))] + _EXCHANGE_SEMS * extra,
        vmem=V7X_VMEM_LIMIT,
    )(x0, g_mix, w_in, qg, kg, cos, sin, *([exchange[0]] if extra else []))


def _mixer_values(u, yf, yb, yt_attn, gates, d_ref, wg_ref, bg_ref, ps_ref, pa_ref, d):
    y = (u * d_ref[...] + yf) + yb
    z = _gelu(y)
    sg = jax.nn.sigmoid(_dot(z.astype(BF16), wg_ref[...]) + bg_ref[...])
    y_ssm = z * sg
    a_ssm = _dot(y_ssm.astype(BF16), ps_ref[...])
    a_attn = _dot_tn(yt_attn.astype(BF16), pa_ref[...])
    s_ssm = jax.nn.sigmoid(gates[:, 0:d])
    s_attn = jax.nn.sigmoid(gates[:, d:2 * d])
    merged = s_ssm * a_ssm + s_attn * a_attn
    return y, z, sg, y_ssm, a_ssm, a_attn, s_ssm, s_attn, merged


def _mixer_out_fwd(x0, u, yf, yb, y_attn, gates, ssm_d, w_glu, b_glu, p_ssm, p_attn, w_out):
    t, d = x0.shape
    tm = ROW_TILE
    du = d // 2

    def body(x_ref, u_ref, yf_ref, yb_ref, ya_ref, gt_ref, d_ref, wg_ref, bg_ref, ps_ref, pa_ref, wo_ref, x1_ref):
        vals = _mixer_values(u_ref[...], yf_ref[...], yb_ref[...], ya_ref[...], gt_ref[...],
                             d_ref, wg_ref, bg_ref, ps_ref, pa_ref, d)
        merged = vals[-1]
        x1_ref[...] = x_ref[...] + _dot(merged.astype(BF16), wo_ref[...])

    return _pcall(
        body, name="mixer_out_fwd", grid=(t // tm,),
        in_specs=[_row_spec(tm, d), _row_spec(tm, du), _row_spec(tm, du), _row_spec(tm, du),
                  pl.BlockSpec((d, tm), lambda i: (0, i)), _row_spec(tm, 2 * d), _full_spec((1, du)), _full_spec((du, du)), _full_spec((1, du)),
                  _full_spec((du, d)), _full_spec((d, d)), _full_spec((d, d))],
        out_specs=_row_spec(tm, d), out_shape=jax.ShapeDtypeStruct((t, d), F32), vmem=V7X_VMEM_LIMIT,
    )(x0, u, yf, yb, y_attn, gates, ssm_d, w_glu, b_glu, p_ssm, p_attn, w_out)


def _mlp_loss_fwd_bwd(x1, target, g_mlp, g_fin, w1, w2, n_valid):
    t, d = x1.shape
    tm = ROW_TILE_BWD
    dff = 4 * d
    nfc, _, fc = w1.shape

    def body(x_ref, tg_ref, gm_ref, gf_ref, w1_hbm, w2_hbm,
             dx1_ref, loss_ref, dgf_ref, dgm_ref, h2_ref, da_ref, hsq_ref, dx2b_ref,
             w1_ref, w2_ref, relu_ref, sem):
        i = pl.program_id(0)
        _load_once(i, [(w1_hbm, w1_ref), (w2_hbm, w2_ref)], sem)

        @pl.when(i == 0)
        def _():
            loss_ref[...] = jnp.zeros_like(loss_ref)
            dgf_ref[...] = jnp.zeros_like(dgf_ref)
            dgm_ref[...] = jnp.zeros_like(dgm_ref)

        x1v = x_ref[...]
        r1 = lax.rsqrt(jnp.mean(x1v * x1v, axis=-1, keepdims=True) + NORM_EPS)
        xh1 = x1v * r1
        h2b = (xh1 * gm_ref[...]).astype(BF16)
        h2_ref[...] = h2b
        acc = jnp.zeros((tm, d), F32)
        for c in range(nfc):
            a = jnp.maximum(_dot(h2b, w1_ref[c]), 0.0)
            relu_ref[:, fc * c:fc * (c + 1)] = a
            hs = (a * a).astype(BF16)
            hsq_ref[:, fc * c:fc * (c + 1)] = hs
            acc = acc + _dot(hs, w2_ref[c])
        x2 = x1v + acc
        r2 = lax.rsqrt(jnp.mean(x2 * x2, axis=-1, keepdims=True) + NORM_EPS)
        xh2 = x2 * r2
        out = xh2 * gf_ref[...]
        row = i * tm + lax.broadcasted_iota(jnp.int32, (tm, 1), 0)
        valid = jnp.logical_and(row >= N_META, row < n_valid)
        diff = jnp.where(valid, out - tg_ref[...], 0.0)
        loss_ref[...] += 0.5 * jnp.sum(jnp.sum(diff * diff, axis=-1, keepdims=True) * (1.0 / d))
        dout = diff * (1.0 / d)
        dgf_ref[...] += jnp.sum(dout * xh2, axis=0, keepdims=True)
        dxh2 = dout * gf_ref[...]
        dx2 = r2 * (dxh2 - xh2 * jnp.mean(dxh2 * xh2, axis=-1, keepdims=True))
        dx2b = dx2.astype(BF16)
        dx2b_ref[...] = dx2b
        dh2 = jnp.zeros((tm, d), F32)
        for c in range(nfc):
            dhs = _dot_nt(dx2b, w2_ref[c])
            da = (dhs * (2.0 * relu_ref[:, fc * c:fc * (c + 1)])).astype(BF16)
            da_ref[:, fc * c:fc * (c + 1)] = da
            dh2 = dh2 + _dot_nt(da, w1_ref[c])
        dgm_ref[...] += jnp.sum(dh2 * xh1, axis=0, keepdims=True)
        dxh1 = dh2 * gm_ref[...]
        dx1_ref[...] = dx2 + r1 * (dxh1 - xh1 * jnp.mean(dxh1 * xh1, axis=-1, keepdims=True))

    return _pcall(
        body, name="mlp_loss_fwd_bwd", grid=(t // tm,),
        in_specs=[_row_spec(tm, d), _row_spec(tm, d), _full_spec((1, d)), _full_spec((1, d)), _ANY, _ANY],
        out_specs=[_row_spec(tm, d), _full_spec((8, 128)), _full_spec((1, d)), _full_spec((1, d)),
                   _row_spec(tm, d), _row_spec(tm, dff), _row_spec(tm, dff), _row_spec(tm, d)],
        out_shape=[jax.ShapeDtypeStruct((t, d), F32), jax.ShapeDtypeStruct((8, 128), F32),
                   jax.ShapeDtypeStruct((1, d), F32), jax.ShapeDtypeStruct((1, d), F32),
                   jax.ShapeDtypeStruct((t, d), BF16), jax.ShapeDtypeStruct((t, dff), BF16),
                   jax.ShapeDtypeStruct((t, dff), BF16), jax.ShapeDtypeStruct((t, d), BF16)],
        scratch=[pltpu.VMEM((nfc, d, fc), BF16), pltpu.VMEM((nfc, fc, d), BF16), pltpu.VMEM((tm, dff), F32),
                 pltpu.SemaphoreType.DMA((2,))],
        vmem=V7X_VMEM_LIMIT,
    )(x1, target, g_mlp, g_fin, w1, w2)


def _mixer_out_bwd(dx1, u, yf, yb, yt_attn, gates, ssm_d, w_glu, b_glu, p_ssm, p_attn, w_out):
    t, d = dx1.shape
    tm = ROW_TILE_BWD
    du = d // 2

    def body(dx_ref, u_ref, yf_ref, yb_ref, yt_ref, gt_ref, d_ref, wg_ref, bg_ref, ps_ref, pa_ref, wo_ref,
             dyb_ref, dud_ref, dyat_ref, dgates_ref, zb_ref, dglb_ref, ysb_ref, dasb_ref, daab_ref,
             mgb_ref, dxb_ref, dd_ref, dbg_ref):
        i = pl.program_id(0)

        @pl.when(i == 0)
        def _():
            dd_ref[...] = jnp.zeros_like(dd_ref)
            dbg_ref[...] = jnp.zeros_like(dbg_ref)

        uv = u_ref[...]
        y, z, sg, y_ssm, a_ssm, a_attn, s_ssm, s_attn, merged = _mixer_values(
            uv, yf_ref[...], yb_ref[...], yt_ref[...], gt_ref[...], d_ref, wg_ref, bg_ref, ps_ref, pa_ref, d)
        dxb = dx_ref[...].astype(BF16)
        dxb_ref[...] = dxb
        mgb_ref[...] = merged.astype(BF16)
        dmerged = _dot_nt(dxb, wo_ref[...])
        dgates_ref[:, 0:d] = (dmerged * a_ssm * (s_ssm * (1.0 - s_ssm))).astype(BF16)
        dgates_ref[:, d:2 * d] = (dmerged * a_attn * (s_attn * (1.0 - s_attn))).astype(BF16)
        da_ssm = (dmerged * s_ssm).astype(BF16)
        da_attn = (dmerged * s_attn).astype(BF16)
        dasb_ref[...] = da_ssm
        daab_ref[...] = da_attn
        ysb_ref[...] = y_ssm.astype(BF16)
        dy_ssm = _dot_nt(da_ssm, ps_ref[...])
        dyat_ref[...] = _dot_nt(pa_ref[...], da_attn).astype(BF16)
        dgl = dy_ssm * z * (sg * (1.0 - sg))
        dglb = dgl.astype(BF16)
        dglb_ref[...] = dglb
        zb_ref[...] = z.astype(BF16)
        dbg_ref[...] += jnp.sum(dgl, axis=0, keepdims=True)
        dz = dy_ssm * sg + _dot_nt(dglb, wg_ref[...])
        dy = dz * _gelu_grad(y)
        dyb_ref[...] = dy.astype(BF16)
        dd_ref[...] += jnp.sum(dy * uv, axis=0, keepdims=True)
        dud_ref[...] = dy * d_ref[...]

    bf = lambda w: jax.ShapeDtypeStruct((t, w), BF16)
    return _pcall(
        body, name="mixer_out_bwd", grid=(t // tm,),
        in_specs=[_row_spec(tm, d), _row_spec(tm, du), _row_spec(tm, du), _row_spec(tm, du),
                  pl.BlockSpec((d, tm), lambda i: (0, i)),
                  _row_spec(tm, 2 * d), _full_spec((1, du)), _full_spec((du, du)), _full_spec((1, du)),
                  _full_spec((du, d)), _full_spec((d, d)), _full_spec((d, d))],
        out_specs=[_row_spec(tm, du), _row_spec(tm, du), pl.BlockSpec((d, tm), lambda i: (0, i)),
                   _row_spec(tm, 2 * d), _row_spec(tm, du), _row_spec(tm, du), _row_spec(tm, du), _row_spec(tm, d),
                   _row_spec(tm, d), _row_spec(tm, d), _row_spec(tm, d), _full_spec((1, du)), _full_spec((1, du))],
        out_shape=[bf(du), jax.ShapeDtypeStruct((t, du), F32), jax.ShapeDtypeStruct((d, t), BF16), bf(2 * d),
                   bf(du), bf(du), bf(du), bf(d), bf(d), bf(d), bf(d),
                   jax.ShapeDtypeStruct((1, du), F32), jax.ShapeDtypeStruct((1, du), F32)],
        vmem=V7X_VMEM_LIMIT,
    )(dx1, u, yf, yb, yt_attn, gates, ssm_d, w_glu, b_glu, p_ssm, p_attn, w_out)


def _in_proj_bwd(x0, dx1, dud, duf, dub, qraw, kraw, dq, dk, dv, dgates, g_mix, w_in, qg, kg, cos, sin):
    t, d = x0.shape
    tm = ROW_TILE_BWD
    du, dkw = d // 2, d // 4
    nh, nkv = d // HEAD_DIM, d // HEAD_DIM // KV_REP
    bw = w_in.shape[-1]
    o_q, o_k, o_v, o_g = du, du + d, du + d + dkw, 2 * d

    def body(x_ref, dx1_ref, dud_ref, duf_ref, dub_ref, qraw_ref, kraw_ref, dq_ref, dk_ref, dv_ref, dgt_ref,
             g_ref, w_hbm, qg_ref, kg_ref, c_ref, s_ref,
             dx0_ref, dproj_ref, dgm_ref, dqg_ref, dkg_ref,
             w_ref, kv_ref, sem):
        i = pl.program_id(0)
        _load_once(i, [(w_hbm, w_ref)], sem)

        @pl.when(i == 0)
        def _():
            dgm_ref[...] = jnp.zeros_like(dgm_ref)
            dqg_ref[...] = jnp.zeros_like(dqg_ref)
            dkg_ref[...] = jnp.zeros_like(dkg_ref)

        lane = lax.broadcasted_iota(jnp.int32, (tm, 128), 1)
        lo = lane < HEAD_DIM
        even = (lane & 1) == 0
        c = c_ref[...]
        s = s_ref[...]

        def norm_rope_bwd(dout, raw, g128):
            sq = raw * raw
            ms_lo = jnp.sum(jnp.where(lo, sq, 0.0), axis=-1, keepdims=True) * (1.0 / HEAD_DIM)
            ms_hi = jnp.sum(jnp.where(lo, 0.0, sq), axis=-1, keepdims=True) * (1.0 / HEAD_DIM)
            rr = jnp.where(lo, lax.rsqrt(ms_lo + NORM_EPS), lax.rsqrt(ms_hi + NORM_EPS))
            xh = raw * rr
            dqn = dout * c + _swap_pairs(dout * s, even)
            dg = jnp.sum(dqn * xh, axis=0, keepdims=True)
            tt = dqn * g128
            pr = tt * xh
            mu_lo = jnp.sum(jnp.where(lo, pr, 0.0), axis=-1, keepdims=True) * (1.0 / HEAD_DIM)
            mu_hi = jnp.sum(jnp.where(lo, 0.0, pr), axis=-1, keepdims=True) * (1.0 / HEAD_DIM)
            return rr * (tt - xh * jnp.where(lo, mu_lo, mu_hi)), dg

        dub_tot = (dud_ref[...] + duf_ref[...]) + dub_ref[...]
        dproj_ref[:, 0:du] = dub_tot.astype(BF16)
        dqg = jnp.zeros((1, 128), F32)
        for a in range(nh // 2):
            sl = slice(128 * a, 128 * (a + 1))
            draw, dg = norm_rope_bwd(dq_ref[:, sl] * ATTN_SCALE, qraw_ref[:, sl], qg_ref[...])
            dqg = dqg + dg
            dproj_ref[:, o_q + 128 * a:o_q + 128 * (a + 1)] = draw.astype(BF16)
        dqg_ref[...] += dqg
        for hh in range(nkv):
            kv_ref[:, HEAD_DIM * hh:HEAD_DIM * (hh + 1)] = dk_ref[hh, :, 0:HEAD_DIM]
        dkg = jnp.zeros((1, 128), F32)
        for a in range(nkv // 2):
            sl = slice(128 * a, 128 * (a + 1))
            draw, dg = norm_rope_bwd(kv_ref[:, sl], kraw_ref[:, sl], kg_ref[...])
            dkg = dkg + dg
            dproj_ref[:, o_k + 128 * a:o_k + 128 * (a + 1)] = draw.astype(BF16)
        dkg_ref[...] += dkg
        for hh in range(nkv):
            kv_ref[:, HEAD_DIM * hh:HEAD_DIM * (hh + 1)] = dv_ref[hh]
        dproj_ref[:, o_v:o_g] = kv_ref[...].astype(BF16)
        dproj_ref[:, o_g:4 * d] = dgt_ref[...]
        dh = jnp.zeros((tm, d), F32)
        for blk in range(N_DEV):
            dh = dh + _dot_nt(dproj_ref[:, bw * blk:bw * (blk + 1)], w_ref[blk])
        x = x_ref[...]
        r = lax.rsqrt(jnp.mean(x * x, axis=-1, keepdims=True) + NORM_EPS)
        xh0 = x * r
        dgm_ref[...] += jnp.sum(dh * xh0, axis=0, keepdims=True)
        dxh = dh * g_ref[...]
        dx0_ref[...] = dx1_ref[...] + r * (dxh - xh0 * jnp.mean(dxh * xh0, axis=-1, keepdims=True))

    return _pcall(
        body, name="in_proj_bwd", grid=(t // tm,),
        in_specs=[_row_spec(tm, d), _row_spec(tm, d), _row_spec(tm, du), _row_spec(tm, du), _row_spec(tm, du),
                  _row_spec(tm, d), _row_spec(tm, dkw), _row_spec(tm, d),
                  pl.BlockSpec((nkv, tm, 128), lambda i: (0, i, 0)), _heads_spec(nkv, tm),
                  _row_spec(tm, 2 * d), _full_spec((1, d)), _ANY, _full_spec((1, 128)), _full_spec((1, 128)),
                  _row_spec(tm, 128), _row_spec(tm, 128)],
        out_specs=[_row_spec(tm, d), _row_spec(tm, 4 * d), _full_spec((1, d)), _full_spec((1, 128)),
                   _full_spec((1, 128))],
        out_shape=[jax.ShapeDtypeStruct((t, d), F32), jax.ShapeDtypeStruct((t, 4 * d), BF16),
                   jax.ShapeDtypeStruct((1, d), F32), jax.ShapeDtypeStruct((1, 128), F32),
                   jax.ShapeDtypeStruct((1, 128), F32)],
        scratch=[pltpu.VMEM((N_DEV, d, bw), BF16), pltpu.VMEM((tm, dkw), F32), pltpu.SemaphoreType.DMA((1,))],
        vmem=V7X_VMEM_LIMIT,
    )(x0, dx1, dud, duf, dub, qraw, kraw, dq, dk, dv, dgates, g_mix, w_in, qg, kg, cos, sin)


def _attn_fwd(qat, ka, vta):
    nh, _, t = qat.shape
    nkv = ka.shape[0]
    rep = nh // nkv
    hd = HEAD_DIM
    vr = vta.shape[1]
    tq = tk = KV_TILE

    def body(qt_ref, k_ref, vt_ref, ot_ref, lse_ref, m_scr, acc_scr):
        j = pl.program_id(2)

        @pl.when(j == 0)
        def _():
            m_scr[...] = jnp.full(m_scr.shape, NEG_BIG, F32)
            acc_scr[...] = jnp.zeros_like(acc_scr)

        k = k_ref[0]
        vt = vt_ref[0]
        strips = [(r, c) for r in range(rep) for c in range(0, tq, QUERY_STRIP)]
        scores = lambda r, c: _dot(k, qt_ref[r, :, c:c + QUERY_STRIP])
        def add_values(r, cols, alpha, pt):
            acc_scr[r, :, cols] = alpha * acc_scr[r, :, cols] + _dot(vt, pt)

        ahead = [scores(*strips[0]), scores(*strips[1])]
        pending = None
        for n, (r, c) in enumerate(strips):
            st = ahead.pop(0)
            if n + 2 < len(strips):
                ahead.append(scores(*strips[n + 2]))
            cols = slice(c, c + QUERY_STRIP)
            m_prev = m_scr[r, :, cols]
            m_next = jnp.maximum(m_prev, jnp.max(st, axis=0, keepdims=True))
            pt = jnp.exp(st - m_next).astype(BF16)
            m_scr[r, :, cols] = m_next
            if pending is not None:
                add_values(*pending)
            pending = (r, cols, jnp.exp(m_prev - m_next), pt)
        add_values(*pending)

        @pl.when(j == pl.num_programs(2) - 1)
        def _():
            for r in range(rep):
                l = acc_scr[r, hd:hd + 1, :]
                ot_ref[hd * r:hd * (r + 1), :] = acc_scr[r, 0:hd, :] / l
                lse_ref[0, r:r + 1, :] = m_scr[r] + jnp.log(l)

    return _pcall(
        body, name="attn_fwd", grid=(nkv, t // tq, t // tk),
        in_specs=[pl.BlockSpec((rep, 128, tq), lambda g, i, j: (g, 0, i)),
                  pl.BlockSpec((1, tk, 128), lambda g, i, j: (g, j, 0)),
                  pl.BlockSpec((1, vr, tk), lambda g, i, j: (g, 0, j))],
        out_specs=[pl.BlockSpec((rep * hd, tq), lambda g, i, j: (g, i)),
                   pl.BlockSpec((1, rep, tq), lambda g, i, j: (g, 0, i))],
        out_shape=[jax.ShapeDtypeStruct((nh * hd, t), F32), jax.ShapeDtypeStruct((nkv, rep, t), F32)],
        scratch=[pltpu.VMEM((rep, 1, tq), F32), pltpu.VMEM((rep, vr, tq), F32)],
        vmem=V7X_VMEM_LIMIT,
    )(qat, ka, vta)


def _attn_bwd(qat, ka, kt, va, dot, ot, lse_row):
    nh, _, t = qat.shape
    nkv = ka.shape[0]
    rep = nh // nkv
    hd = HEAD_DIM
    tq = tk = KV_TILE

    def body(qt_ref, k_ref, kt_ref, v_ref, dot_ref, ot_ref, lse_ref, dk_ref, dv_ref, dqt_ref):
        j = pl.program_id(1)
        i = pl.program_id(2)

        @pl.when(jnp.logical_and(j == 0, i == 0))
        def _():
            dqt_ref[...] = jnp.zeros_like(dqt_ref)

        @pl.when(i == 0)
        def _():
            dk_ref[...] = jnp.zeros_like(dk_ref)
            dv_ref[...] = jnp.zeros_like(dv_ref)

        k = k_ref[0]
        kt = kt_ref[0]
        v = v_ref[0, :, 0:hd]
        cols = pl.ds(pl.multiple_of(i * tq, tq), tq)
        dk = jnp.zeros((tk, 128), F32)
        dv = jnp.zeros((tk, hd), F32)
        products = lambda r: (_dot(k, qt_ref[r]), _dot(v, dot_ref[hd * r:hd * (r + 1), :]))
        nxt = products(0)
        for r in range(rep):
            st, dpt = nxt
            if r + 1 < rep:
                nxt = products(r + 1)
            heads = slice(hd * r, hd * (r + 1))
            qt = qt_ref[r]
            dot_r = dot_ref[heads, :]
            delta = jnp.sum(dot_r.astype(F32) * ot_ref[heads, :], axis=0, keepdims=True)
            pt = jnp.exp(st - lse_ref[0, r:r + 1, :])
            dst = (pt * (dpt - delta)).astype(BF16)
            dv = dv + _dot_nt(pt.astype(BF16), dot_r)
            dk = dk + _dot_nt(dst, qt)
            dqt_ref[heads, cols] += _dot(kt, dst)
        dk_ref[0] += dk
        dv_ref[0] += dv

    return _pcall(
        body, name="attn_bwd", grid=(nkv, t // tk, t // tq),
        in_specs=[pl.BlockSpec((rep, 128, tq), lambda g, j, i: (g, 0, i)),
                  pl.BlockSpec((1, tk, 128), lambda g, j, i: (g, j, 0)),
                  pl.BlockSpec((1, hd, tk), lambda g, j, i: (g, 0, j)),
                  pl.BlockSpec((1, tk, 128), lambda g, j, i: (g, j, 0)),
                  pl.BlockSpec((rep * hd, tq), lambda g, j, i: (g, i)),
                  pl.BlockSpec((rep * hd, tq), lambda g, j, i: (g, i)),
                  pl.BlockSpec((1, rep, tq), lambda g, j, i: (g, 0, i))],
        out_specs=[pl.BlockSpec((1, tk, 128), lambda g, j, i: (g, j, 0)),
                   pl.BlockSpec((1, tk, hd), lambda g, j, i: (g, j, 0)),
                   pl.BlockSpec((rep * hd, t), lambda g, j, i: (g, 0))],
        out_shape=[jax.ShapeDtypeStruct((nkv, t, 128), F32), jax.ShapeDtypeStruct((nkv, t, hd), F32),
                   jax.ShapeDtypeStruct((nh * hd, t), F32)],
        vmem=V7X_VMEM_LIMIT,
    )(qat, ka, kt, va, dot, ot, lse_row)


def _riding_exchange(refs, exchange, n_in, n_out, first_step, last_step):
    if exchange is None:
        return refs
    x_ref, out_ref = refs[n_in], refs[n_in + 1 + n_out]
    sems = refs[-3:]

    @pl.when(first_step)
    def _():
        _start_all(*_exchange_copies(x_ref, out_ref, *sems, exchange[1]))

    @pl.when(last_step)
    def _():
        _wait_all(*_exchange_copies(x_ref, out_ref, *sems, exchange[1]))

    return refs[:n_in] + refs[n_in + 1:n_in + 1 + n_out] + refs[n_in + 2 + n_out:-3]


def _segmented_scan(src_re, src_im, dst_re, dst_im, lam_re, lam_im, pow_re, pow_im, carry_re, carry_im,
                    end_re, end_im, in_re, in_im, lanes, descending, conj):
    tc = src_re.shape[0]
    seg = tc // 8
    width = lanes.size
    sign = -1.0 if conj else 1.0
    rows_of = lambda q: pl.ds(8 * (seg - 1 - q if descending else q), 8)
    lr = jnp.broadcast_to(lam_re[:, lanes], (8, width))
    li = jnp.broadcast_to(sign * lam_im[:, lanes], (8, width))
    xr = jnp.zeros((8, width), F32)
    xi = jnp.zeros((8, width), F32)
    for q in range(seg):
        rows = rows_of(q)
        xr, xi = (lr * xr - li * xi) + src_re[rows, lanes], (lr * xi + li * xr) + src_im[rows, lanes]
        dst_re[rows, lanes] = xr
        dst_im[rows, lanes] = xi
    end_re[:, lanes] = xr
    end_im[:, lanes] = xi
    sr = pow_re[seg - 1:seg, lanes]
    si = sign * pow_im[seg - 1:seg, lanes]
    cr = carry_re[:, lanes]
    ci = carry_im[:, lanes]
    for s in range(8):
        se = 7 - s if descending else s
        in_re[se:se + 1, lanes] = cr
        in_im[se:se + 1, lanes] = ci
        cr, ci = (end_re[se:se + 1, lanes] + (sr * cr - si * ci)), (end_im[se:se + 1, lanes] + (sr * ci + si * cr))
    carry_re[:, lanes] = cr
    carry_im[:, lanes] = ci
    ir = in_re[:, lanes]
    ii = in_im[:, lanes]
    for q in range(seg):
        rows = rows_of(q)
        pr = pow_re[q:q + 1, lanes]
        pi = sign * pow_im[q:q + 1, lanes]
        dst_re[rows, lanes] = dst_re[rows, lanes] + (pr * ir - pi * ii)
        dst_im[rows, lanes] = dst_im[rows, lanes] + (pr * ii + pi * ir)


def _diag_tiles(gn):
    rows_per_tile = DIAG_TILE // (SSM_STATE // SSM_GROUP)
    return [(slice(rows_per_tile * j, rows_per_tile * (j + 1)), slice(DIAG_TILE * j, DIAG_TILE * (j + 1)))
            for j in range(gn // DIAG_TILE)]


def _ssm_scan_fwd(ub, lam_re, lam_im, pow_re, pow_im, bb_re, bb_im, cc_re, cc_im, exchange=None):
    t, w = ub.shape
    gn = lam_re.shape[-1]
    tc = ROW_TILE
    cl = min(gn, SCAN_LANES)
    nblk = t // tc
    tiles = _diag_tiles(gn)

    def body(*refs):
        first = jnp.logical_and(pl.program_id(0) == 0, pl.program_id(1) == 0)
        last = jnp.logical_and(pl.program_id(0) == 1, pl.program_id(1) == nblk - 1)
        (u_ref, lr_ref, li_ref, pr_ref, pi_ref, br_ref, bi_ref, cr_ref, ci_ref, y_ref, xr_ref, xi_ref,
         bur_scr, bui_scr, cr_scr, ci_scr, er_scr, ei_scr, nr_scr, ni_scr) = _riding_exchange(
             refs, exchange, 9, 3, first, last)

        @pl.when(pl.program_id(1) == 0)
        def _():
            cr_scr[...] = jnp.zeros_like(cr_scr)
            ci_scr[...] = jnp.zeros_like(ci_scr)

        for rows, lanes in tiles:
            u_j = u_ref[:, rows]
            bur_scr[:, lanes] = _dot(u_j, br_ref[0, rows, lanes])
            bui_scr[:, lanes] = _dot(u_j, bi_ref[0, rows, lanes])
        for descending in (False, True):
            @pl.when(pl.program_id(0) == int(descending))
            def _(descending=descending):
                for c0 in range(0, gn, cl):
                    _segmented_scan(bur_scr, bui_scr, xr_ref.at[0], xi_ref.at[0], lr_ref.at[0], li_ref.at[0],
                                    pr_ref.at[0], pi_ref.at[0], cr_scr, ci_scr, er_scr, ei_scr, nr_scr, ni_scr,
                                    pl.ds(c0, cl), descending, conj=False)
        for rows, lanes in tiles:
            y_ref[0, :, rows] = (_dot(xr_ref[0, :, lanes].astype(BF16), cr_ref[0, lanes, rows])
                                 - _dot(xi_ref[0, :, lanes].astype(BF16), ci_ref[0, lanes, rows]))

    blk = lambda dd, i: jnp.where(dd == 0, i, nblk - 1 - i)
    row = lambda width: pl.BlockSpec((1, tc, width), lambda dd, i: (dd, blk(dd, i), 0))
    per_dir = lambda a, b: pl.BlockSpec((1, a, b), lambda dd, i: (dd, 0, 0))
    extra = exchange is not None
    return _pcall(
        body, name="ssm_scan_fwd", grid=(2, nblk),
        in_specs=[pl.BlockSpec((tc, w), lambda dd, i: (blk(dd, i), 0)), per_dir(1, gn), per_dir(1, gn),
                  per_dir(tc // 8, gn), per_dir(tc // 8, gn),
                  per_dir(w, gn), per_dir(w, gn), per_dir(gn, w), per_dir(gn, w)] + [_ANY] * extra,
        out_specs=[row(w), row(gn), row(gn)] + [_ANY] * extra,
        out_shape=[jax.ShapeDtypeStruct((2, t, w), F32), jax.ShapeDtypeStruct((2, t, gn), F32),
                   jax.ShapeDtypeStruct((2, t, gn), F32)] + ([_exchange_out_shape(*exchange)] if extra else []),
        scratch=[pltpu.VMEM((tc, gn), F32), pltpu.VMEM((tc, gn), F32), pltpu.VMEM((1, gn), F32),
                 pltpu.VMEM((1, gn), F32)] + [pltpu.VMEM((8, gn), F32)] * 4 + _EXCHANGE_SEMS * extra,
        vmem=V7X_VMEM_LIMIT,
    )(ub, lam_re, lam_im, pow_re, pow_im, bb_re, bb_im, cc_re, cc_im, *([exchange[0]] if extra else []))


def _ssm_scan_bwd(dyb, ub, xs_re, xs_im, lam_re, lam_im, pow_re, pow_im, cct_re, cct_im, bbt_re, bbt_im,
                  exchange=None):
    t, w = dyb.shape
    gn = lam_re.shape[-1]
    tc = ROW_TILE
    cl = min(gn, SCAN_LANES)
    nblk = t // tc
    tiles = _diag_tiles(gn)

    def body(*refs):
        i = pl.program_id(1)
        first = jnp.logical_and(pl.program_id(0) == 0, i == 0)
        last = jnp.logical_and(pl.program_id(0) == 1, i == nblk - 1)
        (dy_ref, u_ref, xr_ref, xi_ref, hr_ref, hi_ref, lr_ref, li_ref, pr_ref, pi_ref, ctr_ref, cti_ref, btr_ref,
         bti_ref, du_ref, dlr_ref, dli_ref, dbr_ref, dbi_ref, dcr_ref, dci_ref,
         gxr_scr, gxi_scr, cr_scr, ci_scr, ar_scr, ai_scr, er_scr, ei_scr, nr_scr, ni_scr) = _riding_exchange(
             refs, exchange, 14, 7, first, last)

        @pl.when(i == 0)
        def _():
            for ref in (cr_scr, ci_scr, ar_scr, ai_scr, dbr_ref, dbi_ref, dcr_ref, dci_ref):
                ref[...] = jnp.zeros_like(ref)

        for rows, lanes in tiles:
            dy_j = dy_ref[:, rows]
            gxr_scr[:, lanes] = _dot(dy_j, ctr_ref[0, rows, lanes])
            gxi_scr[:, lanes] = -_dot(dy_j, cti_ref[0, rows, lanes])
        first_block = i == nblk - 1
        sublane = lax.broadcasted_iota(jnp.int32, (8, 1), 0)

        def lam_gradient(state_descending):
            for c0 in range(0, gn, 512):
                lanes = pl.ds(c0, 512)
                if state_descending:
                    cur, prev, edge, src = pl.ds(0, tc - 8), pl.ds(8, tc - 8), pl.ds(tc - 8, 8), pl.ds(0, 8)
                    halo_at, halo_row, shift = 7, 0, 7
                else:
                    cur, prev, edge, src = pl.ds(8, tc - 8), pl.ds(0, tc - 8), pl.ds(0, 8), pl.ds(tc - 8, 8)
                    halo_at, halo_row, shift = 0, 7, 1
                halo_r = jnp.where(first_block, 0.0, hr_ref[0, halo_row:halo_row + 1, lanes])
                halo_i = jnp.where(first_block, 0.0, hi_ref[0, halo_row:halo_row + 1, lanes])
                xer = jnp.where(sublane == halo_at, halo_r, pltpu.roll(xr_ref[0, src, lanes], shift, 0))
                xei = jnp.where(sublane == halo_at, halo_i, pltpu.roll(xi_ref[0, src, lanes], shift, 0))
                gr, gi = gxr_scr[cur, lanes], gxi_scr[cur, lanes]
                xpr, xpi = xr_ref[0, prev, lanes], xi_ref[0, prev, lanes]
                ger, gei = gxr_scr[edge, lanes], gxi_scr[edge, lanes]
                ar_scr[:, lanes] += (jnp.sum(gr * xpr + gi * xpi, axis=0, keepdims=True)
                                     + jnp.sum(ger * xer + gei * xei, axis=0, keepdims=True))
                ai_scr[:, lanes] += (jnp.sum(gi * xpr - gr * xpi, axis=0, keepdims=True)
                                     + jnp.sum(gei * xer - ger * xei, axis=0, keepdims=True))

        for descending in (True, False):
            @pl.when(pl.program_id(0) == int(not descending))
            def _(descending=descending):
                for c0 in range(0, gn, cl):
                    _segmented_scan(gxr_scr, gxi_scr, gxr_scr, gxi_scr, lr_ref.at[0], li_ref.at[0], pr_ref.at[0],
                                    pi_ref.at[0], cr_scr, ci_scr, er_scr, ei_scr, nr_scr, ni_scr, pl.ds(c0, cl),
                                    descending, conj=True)
                lam_gradient(state_descending=not descending)
        dlr_ref[0] = ar_scr[...]
        dli_ref[0] = ai_scr[...]
        for rows, lanes in tiles:
            grb = gxr_scr[:, lanes].astype(BF16)
            gib = gxi_scr[:, lanes].astype(BF16)
            du_ref[0, :, rows] = _dot(grb, btr_ref[0, lanes, rows]) + _dot(gib, bti_ref[0, lanes, rows])
            u_j = u_ref[:, rows]
            dy_j = dy_ref[:, rows]
            dbr_ref[0, rows, :] += _dot_tn(u_j, grb)
            dbi_ref[0, rows, :] += _dot_tn(u_j, gib)
            dcr_ref[0, rows, :] += _dot_tn(dy_j, xr_ref[0, :, lanes].astype(BF16))
            dci_ref[0, rows, :] -= _dot_tn(dy_j, xi_ref[0, :, lanes].astype(BF16))

    blk = lambda dd, i: jnp.where(dd == 0, nblk - 1 - i, i)
    rev = lambda width: pl.BlockSpec((1, tc, width), lambda dd, i: (dd, blk(dd, i), 0))
    halo_blk = lambda dd, i: jnp.where(dd == 0, jnp.maximum(blk(dd, i) * (tc // 8) - 1, 0),
                                       jnp.minimum((blk(dd, i) + 1) * (tc // 8), t // 8 - 1))
    halo = pl.BlockSpec((1, 8, gn), lambda dd, i: (dd, halo_blk(dd, i), 0))
    per_dir = lambda a, b: pl.BlockSpec((1, a, b), lambda dd, i: (dd, 0, 0))
    extra = exchange is not None
    return _pcall(
        body, name="ssm_scan_bwd", grid=(2, nblk),
        in_specs=[pl.BlockSpec((tc, w), lambda dd, i: (blk(dd, i), 0)),
                  pl.BlockSpec((tc, w), lambda dd, i: (blk(dd, i), 0)), rev(gn), rev(gn), halo, halo,
                  per_dir(1, gn), per_dir(1, gn), per_dir(tc // 8, gn), per_dir(tc // 8, gn),
                  per_dir(w, gn), per_dir(w, gn), per_dir(gn, w), per_dir(gn, w)]
        + [_ANY] * extra,
        out_specs=[rev(w), per_dir(1, gn), per_dir(1, gn)] + [per_dir(w, DIAG_TILE)] * 4 + [_ANY] * extra,
        out_shape=[jax.ShapeDtypeStruct((2, t, w), F32), jax.ShapeDtypeStruct((2, 1, gn), F32),
                   jax.ShapeDtypeStruct((2, 1, gn), F32)] + [jax.ShapeDtypeStruct((2, w, DIAG_TILE), F32)] * 4
        + ([_exchange_out_shape(*exchange)] if extra else []),
        scratch=[pltpu.VMEM((tc, gn), F32), pltpu.VMEM((tc, gn), F32)] + [pltpu.VMEM((1, gn), F32)] * 4
        + [pltpu.VMEM((8, gn), F32)] * 4 + _EXCHANGE_SEMS * extra,
        vmem=V7X_VMEM_LIMIT,
    )(dyb, ub, xs_re, xs_im, xs_re, xs_im, lam_re, lam_im, pow_re, pow_im, cct_re, cct_im, bbt_re, bbt_im,
      *([exchange[0]] if extra else []))


def _matmul_tn(a, b, name, a_is_transposed=False, exchange=None):
    t, n = b.shape
    m = a.shape[0] if a_is_transposed else a.shape[1]
    bm, bn, tk = min(m, 1024), min(n, 1024), KV_TILE
    grid = (m // bm, n // bn, t // tk)

    def body(*refs):
        at = lambda step: functools.reduce(jnp.logical_and, [pl.program_id(ax) == step[ax] for ax in range(3)])
        a_ref, b_ref, o_ref = _riding_exchange(refs, exchange, 2, 1, at((0, 0, 0)), at([g - 1 for g in grid]))

        @pl.when(pl.program_id(2) == 0)
        def _():
            o_ref[...] = jnp.zeros_like(o_ref)

        mul = _dot if a_is_transposed else _dot_tn
        o_ref[...] += mul(a_ref[...].astype(BF16), b_ref[...].astype(BF16))

    a_spec = (pl.BlockSpec((bm, tk), lambda i, j, k: (i, k)) if a_is_transposed else
              pl.BlockSpec((tk, bm), lambda i, j, k: (k, i)))
    extra = exchange is not None
    out = _pcall(
        body, name=name, grid=grid,
        in_specs=[a_spec, pl.BlockSpec((tk, bn), lambda i, j, k: (k, j))] + [_ANY] * extra,
        out_specs=[pl.BlockSpec((bm, bn), lambda i, j, k: (i, j))] + [_ANY] * extra,
        out_shape=[jax.ShapeDtypeStruct((m, n), F32)] + ([_exchange_out_shape(*exchange)] if extra else []),
        scratch=_EXCHANGE_SEMS * extra, vmem=V7X_VMEM_LIMIT,
    )(a, b, *([exchange[0]] if extra else []))
    return out if extra else out[0]


def _reduce_adamw(gparts, p, m, v, name):
    rows, width = p.shape
    tr = max(k for k in range(16, 513, 16) if rows % k == 0)

    def body(g_ref, p_ref, m_ref, v_ref, go_ref, d_ref, mo_ref, vo_ref):
        g = g_ref[0].astype(F32)
        for k in range(1, N_DEV):
            g = g + g_ref[k].astype(F32)
        go_ref[...] = g
        mm = ADAM_B1 * m_ref[...] + (1.0 - ADAM_B1) * g
        vv = ADAM_B2 * v_ref[...] + (1.0 - ADAM_B2) * (g * g)
        m_hat = mm / (1.0 - ADAM_B1 ** ADAM_STEP)
        v_hat = vv / (1.0 - ADAM_B2 ** ADAM_STEP)
        d_ref[...] = -ADAM_LR * (m_hat / (jnp.sqrt(v_hat) + ADAM_EPS) + ADAM_WD * p_ref[...])
        mo_ref[...] = mm
        vo_ref[...] = vv

    spec = pl.BlockSpec((tr, width), lambda i: (i, 0))
    out = jax.ShapeDtypeStruct((rows, width), F32)
    return _pcall(
        body, name=name, grid=(rows // tr,),
        in_specs=[pl.BlockSpec((N_DEV, tr, width), lambda i: (0, i, 0)), spec, spec, spec],
        out_specs=[spec, spec, spec, spec], out_shape=[out, out, out, out], vmem=V7X_VMEM_LIMIT,
    )(gparts, p, m, v)


def _peer(k):
    x, y, c = lax.axis_index("x"), lax.axis_index("y"), lax.axis_index("c")
    return (x ^ ((k >> 2) & 1), y ^ ((k >> 1) & 1), c ^ (k & 1))


def _my_index():
    return 4 * lax.axis_index("x") + 2 * lax.axis_index("y") + lax.axis_index("c")


def _exchange_copies(x_ref, out_ref, send_sems, recv_sems, local_sem, scatter, first_sem=0):
    me = _my_index()
    local = pltpu.make_async_copy(x_ref.at[me] if scatter else x_ref, out_ref.at[me], local_sem)
    copies = []
    for k in range(1, N_DEV):
        peer = _peer(k)
        src = x_ref.at[4 * peer[0] + 2 * peer[1] + peer[2]] if scatter else x_ref
        copies.append(pltpu.make_async_remote_copy(
            src_ref=src, dst_ref=out_ref.at[me], send_sem=send_sems.at[first_sem + k - 1],
            recv_sem=recv_sems.at[first_sem + k - 1], device_id=peer, device_id_type=pl.DeviceIdType.MESH))
    return local, copies


def _start_all(local, copies):
    local.start()
    for cp in copies:
        cp.start()


def _wait_all(local, copies):
    for cp in copies:
        cp.wait_recv()
    for cp in copies:
        cp.wait_send()
    local.wait()


def _exchange_out_shape(x, scatter):
    return jax.ShapeDtypeStruct((N_DEV,) + tuple(x.shape[1:] if scatter else x.shape), x.dtype)


_EXCHANGE_SEMS = [pltpu.SemaphoreType.DMA((N_DEV - 1,)), pltpu.SemaphoreType.DMA((N_DEV - 1,)),
                  pltpu.SemaphoreType.DMA(())]


def _exchange(ops, name):
    n = len(ops)

    def body(*refs):
        x_refs, out_refs = refs[:n], refs[n:2 * n]
        send_sems, recv_sems, local_sems = refs[2 * n:]
        started = []
        for q, (_, scatter) in enumerate(ops):
            local, copies = _exchange_copies(x_refs[q], out_refs[q], send_sems, recv_sems, local_sems.at[q],
                                             scatter, first_sem=q * (N_DEV - 1))
            _start_all(local, copies)
            started.append((local, copies))
        for local, copies in started:
            _wait_all(local, copies)

    return pl.pallas_call(
        body, name=name, in_specs=[_ANY] * n, out_specs=[_ANY] * n,
        out_shape=[_exchange_out_shape(x, scatter) for x, scatter in ops],
        scratch_shapes=[pltpu.SemaphoreType.DMA((n * (N_DEV - 1),)), pltpu.SemaphoreType.DMA((n * (N_DEV - 1),)),
                        pltpu.SemaphoreType.DMA((n,))],
    )(*[x for x, _ in ops])


def _to_shards(full, axis):
    r, c = full.shape
    if axis == 0:
        return full.reshape(N_DEV, r // N_DEV, c)
    return full.reshape(r, N_DEV, c // N_DEV).transpose(1, 0, 2)


def _from_shards(shards, axis):
    _, r, c = shards.shape
    if axis == 0:
        return shards.reshape(N_DEV * r, c)
    return shards.transpose(1, 0, 2).reshape(r, N_DEV * c)


def _pack_rows(parts, lead):
    flat = []
    for p in parts:
        p = p.reshape(p.shape[:lead] + (-1, PACK_W))
        pad = _round_up(p.shape[lead], 16) - p.shape[lead]
        flat.append(jnp.pad(p, [(0, 0)] * lead + [(0, pad), (0, 0)]) if pad else p)
    return jnp.concatenate(flat, axis=lead)


def _unpack_rows(packed, shapes):
    lead = packed.shape[:-2]
    out, off = [], 0
    for shp in shapes:
        rows = math.prod(shp) // PACK_W
        out.append(packed[..., off:off + rows, :].reshape(lead + tuple(shp)))
        off += _round_up(rows, 16)
    return out


def _pack_flat(parts):
    flat = jnp.concatenate([p.reshape(-1) for p in parts])
    n = flat.shape[0]
    flat = jnp.pad(flat, (0, _round_up(n, 16 * PACK_W) - n))
    return flat.reshape(-1, PACK_W)


def _unpack(packed, shapes):
    flat = packed.reshape(-1)
    out, off = [], 0
    for shp in shapes:
        n = math.prod(shp)
        out.append(flat[off:off + n].reshape(shp))
        off += n
    return out


def _ssm_discretize(a_re, a_im, log_dt, b_re, b_im):
    dt = jnp.exp(log_dt)[..., None]
    lam_re = jnp.minimum(a_re, EIG_RE_MAX)
    lam_im = a_im
    mag = jnp.exp(lam_re * dt)
    ang = lam_im * dt
    lb_re = mag * jnp.cos(ang)
    lb_im = mag * jnp.sin(ang)
    num_re = lb_re - 1.0
    num_im = lb_im
    den = lam_re * lam_re + lam_im * lam_im
    f_re = (num_re * lam_re + num_im * lam_im) / den
    f_im = (num_im * lam_re - num_re * lam_im) / den
    bb_re = f_re[..., None] * b_re - f_im[..., None] * b_im
    bb_im = f_re[..., None] * b_im + f_im[..., None] * b_re
    return lb_re, lb_im, bb_re, bb_im


def _ssm_powers(a_re, a_im, log_dt, count):
    dt = jnp.exp(log_dt)[:, None, :, None]
    k = jnp.arange(1, count + 1, dtype=F32)[None, :, None, None]
    mag = jnp.exp(k * (jnp.minimum(a_re, EIG_RE_MAX)[:, None] * dt))
    ang = k * (a_im[:, None] * dt)
    shape = (a_re.shape[0], count, -1)
    return (mag * jnp.cos(ang)).reshape(shape), (mag * jnp.sin(ang)).reshape(shape)


def _interleave(a, inverse=False):
    lead, (t, width) = a.shape[:-2], a.shape[-2:]
    seg = ROW_TILE // 8
    shape = lead + (t // ROW_TILE,) + ((seg, 8) if inverse else (8, seg)) + (width,)
    return jnp.swapaxes(a.reshape(shape), -3, -2).reshape(a.shape)


def _block_diag(blocks):
    two, g, a, b = blocks.shape
    tiled = jnp.tile(blocks.reshape(two, g * a, b), (1, 1, g))
    row_group = lax.broadcasted_iota(jnp.int32, (g * a, g * b), 0) // a
    col_group = lax.broadcasted_iota(jnp.int32, (g * a, g * b), 1) // b
    return jnp.where(row_group == col_group, tiled, 0.0).astype(BF16)


def _diag_blocks(tiles):
    two, w, _ = tiles.shape
    per = DIAG_TILE // SSM_STATE
    t6 = tiles.reshape(two, w // (per * SSM_GROUP), per, SSM_GROUP, per, SSM_STATE)
    return jnp.einsum("zjqpqn->zjqpn", t6).reshape(two, w // SSM_GROUP, SSM_GROUP, SSM_STATE)


def _rope_tables(t, n_valid):
    pos = jnp.arange(t)
    real = jnp.logical_and(pos >= N_META, pos < n_valid)
    idx = jnp.where(real, pos - N_META, 0)
    row_id = (idx // GRID_W).astype(F32)
    col_id = (idx % GRID_W).astype(F32)
    pairs_per_axis = HEAD_DIM // 4
    inv_freq = ROPE_THETA ** (-jnp.arange(pairs_per_axis, dtype=F32) / pairs_per_axis)
    ang = jnp.concatenate([row_id[:, None] * inv_freq, col_id[:, None] * inv_freq], axis=-1)
    ang = jnp.where(real[:, None], ang, 0.0)
    cos = jnp.repeat(jnp.cos(ang), 2, axis=-1)
    sin = jnp.sin(ang)
    sin = jnp.stack([-sin, sin], axis=-1).reshape(t, HEAD_DIM)
    return jnp.tile(cos, (1, 2)), jnp.tile(sin, (1, 2))


def _local_step(x, loss_target, big, small, comm=None):
    s_len, d = x.shape
    n_valid = s_len + N_META
    t = _round_up(n_valid, KV_TILE)
    du = d // 2
    groups = du // SSM_GROUP
    nh = d // HEAD_DIM
    nkv = nh // KV_REP
    pad = t - n_valid

    x0 = jnp.concatenate([big["meta_tokens"].astype(F32), x, jnp.zeros((pad, d), F32)], axis=0)
    tgt = jnp.concatenate([jnp.zeros((N_META, d), F32), loss_target, jnp.zeros((pad, d), F32)], axis=0)
    cos, sin = _rope_tables(t, n_valid)
    g_mix = small["norm_mix_g"].reshape(1, d)
    g_mlp = small["norm_mlp_g"].reshape(1, d)
    g_fin = small["norm_final_g"].reshape(1, d)
    qg = jnp.tile(small["q_norm_g"].reshape(1, HEAD_DIM), (1, 2))
    kg = jnp.tile(small["k_norm_g"].reshape(1, HEAD_DIM), (1, 2))
    ssm_d = small["ssm_d"].reshape(1, du)
    b_glu = small["b_glu"].reshape(1, du)

    ssm_in = tuple(small[n][0] for n in ("ssm_a_re", "ssm_a_im", "ssm_log_dt", "ssm_b_re", "ssm_b_im"))
    (lb_re, lb_im, bbar_re, bbar_im), disc_vjp = jax.vjp(_ssm_discretize, *ssm_in)
    lam_re = lb_re.reshape(2, 1, groups * SSM_STATE)
    lam_im = lb_im.reshape(2, 1, groups * SSM_STATE)
    pow_re, pow_im = _ssm_powers(*ssm_in[0:3], ROW_TILE // 8)
    bb_re = _block_diag(bbar_re.transpose(0, 1, 3, 2))
    bb_im = _block_diag(bbar_im.transpose(0, 1, 3, 2))
    c_re, c_im = small["ssm_c_re"][0], small["ssm_c_im"][0]
    cct_re = _block_diag(c_re)
    cct_im = _block_diag(c_im)
    cc_re = cct_re.transpose(0, 2, 1)
    cc_im = cct_im.transpose(0, 2, 1)
    bbt_re = bb_re.transpose(0, 2, 1)
    bbt_im = bb_im.transpose(0, 2, 1)
    scan_w = (lam_re, lam_im, pow_re, pow_im)

    in_proj_args = (x0, g_mix, big["w_in"], qg, kg, cos, sin, n_valid)
    if comm is None:
        h, u, ub, qraw, kraw, qa, ka, va, gates = _in_proj_fwd(*in_proj_args)
    else:
        h, u, ub, qraw, kraw, qa, ka, va, gates, got = _in_proj_fwd(
            *in_proj_args, exchange=(comm["pack_weights"](MIXER_WEIGHTS), False))
        big = {**big, **comm["unpack_weights"](MIXER_WEIGHTS, got)}
    ub = _interleave(ub)
    if comm is None:
        y2, xs_re, xs_im = _ssm_scan_fwd(ub, *scan_w, bb_re, bb_im, cc_re, cc_im)
    else:
        y2, xs_re, xs_im, got = _ssm_scan_fwd(ub, *scan_w, bb_re, bb_im, cc_re, cc_im,
                                              exchange=(comm["pack_weights"](MLP_WEIGHTS), False))
        big = {**big, **comm["unpack_weights"](MLP_WEIGHTS, got)}
    y2 = _interleave(y2, inverse=True)
    yf, yb = y2[0], y2[1]
    qat = qa.transpose(0, 2, 1)
    kt = ka[:, :, 0:HEAD_DIM].transpose(0, 2, 1)
    vta = va[:, :, 0:VT_ROWS].transpose(0, 2, 1)
    yt_attn, lse = _attn_fwd(qat, ka, vta)
    mixer_w = (ssm_d, big["w_glu"], b_glu, big["w_ssm_proj"], big["w_attn_proj"], big["w_out"])
    x1 = _mixer_out_fwd(x0, u, yf, yb, yt_attn, gates, *mixer_w)

    dx1, loss8, dg_fin, dg_mlp, h2b, dab, hsqb, dx2b = _mlp_loss_fwd_bwd(
        x1, tgt, g_mlp, g_fin, big["w_mlp_in"], big["w_mlp_out"], n_valid)

    (dyb, dud, dyt_attn, dgates, zb, dglb, ysb, dasb, daab, mgb, dxb, d_ssm_d, d_b_glu) = _mixer_out_bwd(
        dx1, u, yf, yb, yt_attn, gates, *mixer_w)
    grads = {}
    grads["w_glu"] = _matmul_tn(zb, dglb, "grad_w_glu")
    grads["w_ssm_proj"] = _matmul_tn(ysb, dasb, "grad_w_ssm_proj")
    grads["w_attn_proj"] = _matmul_tn(yt_attn, daab, "grad_w_attn_proj", a_is_transposed=True)
    grads["w_out"] = _matmul_tn(mgb, dxb, "grad_w_out")
    grads["w_mlp_in"] = _matmul_tn(h2b, dab, "grad_w_mlp_in")
    grads["w_mlp_out"] = _matmul_tn(hsqb, dx2b, "grad_w_mlp_out")
    dk, dv, dqt = _attn_bwd(qat, ka, kt, va, dyt_attn, yt_attn, lse)
    scan_args = (_interleave(dyb), ub, xs_re, xs_im, *scan_w, cct_re, cct_im, bbt_re, bbt_im)
    if comm is None:
        late_grad_parts = None
        du2, dlam_re, dlam_im, dbr, dbi, dcr, dci = _ssm_scan_bwd(*scan_args)
    else:
        du2, dlam_re, dlam_im, dbr, dbi, dcr, dci, late_grad_parts = _ssm_scan_bwd(
            *scan_args, exchange=(comm["pack_grads"](LATE_WEIGHTS, grads), True))
    du2 = _interleave(du2, inverse=True)
    dx0, dproj, dg_mix, dqg, dkg = _in_proj_bwd(x0, dx1, dud, du2[0], du2[1], qraw, kraw, dqt.T, dk, dv, dgates,
                                                g_mix, big["w_in"], qg, kg, cos, sin)

    grads["meta_tokens"] = dx0[0:N_META]
    dbb_re = _diag_blocks(dbr).transpose(0, 1, 3, 2)
    dbb_im = _diag_blocks(dbi).transpose(0, 1, 3, 2)
    dc_re, dc_im = _diag_blocks(dcr), _diag_blocks(dci)
    shape_gn = (2, groups, SSM_STATE)
    d_a_re, d_a_im, d_log_dt, d_b_re, d_b_im = disc_vjp(
        (dlam_re.reshape(shape_gn), dlam_im.reshape(shape_gn), dbb_re, dbb_im))
    grads.update({
        "norm_mix_g": dg_mix, "ssm_a_re": d_a_re[None], "ssm_a_im": d_a_im[None], "ssm_log_dt": d_log_dt[None],
        "ssm_b_re": d_b_re[None], "ssm_b_im": d_b_im[None], "ssm_c_re": dc_re[None], "ssm_c_im": dc_im[None],
        "ssm_d": d_ssm_d, "b_glu": d_b_glu,
        "q_norm_g": dqg[:, 0:HEAD_DIM] + dqg[:, HEAD_DIM:128], "k_norm_g": dkg[:, 0:HEAD_DIM] + dkg[:, HEAD_DIM:128],
        "norm_mlp_g": dg_mlp, "norm_final_g": dg_fin.reshape(d),
    })
    if comm is None:
        small_grad_parts = None
        grads["w_in"] = _matmul_tn(h, dproj, "grad_w_in")
    else:
        grads["w_in"], small_grad_parts = _matmul_tn(h, dproj, "grad_w_in",
                                                     exchange=(comm["pack_small_grads"](grads), False))
    return loss8[0, 0], dx0[N_META:n_valid], grads, late_grad_parts, small_grad_parts


def kernel(x, meta_tokens, norm_mix_g, w_in, ssm_a_re, ssm_a_im, ssm_log_dt, ssm_b_re, ssm_b_im, ssm_c_re, ssm_c_im, ssm_d, w_glu, b_glu, q_norm_g, k_norm_g, w_ssm_proj, w_attn_proj, w_out, norm_mlp_g, w_mlp_in, w_mlp_out, norm_final_g, loss_target, m_meta_tokens, m_norm_mix_g, m_w_in, m_ssm_a_re, m_ssm_a_im, m_ssm_log_dt, m_ssm_b_re, m_ssm_b_im, m_ssm_c_re, m_ssm_c_im, m_ssm_d, m_w_glu, m_b_glu, m_q_norm_g, m_k_norm_g, m_w_ssm_proj, m_w_attn_proj, m_w_out, m_norm_mlp_g, m_w_mlp_in, m_w_mlp_out, m_norm_final_g, v_meta_tokens, v_norm_mix_g, v_w_in, v_ssm_a_re, v_ssm_a_im, v_ssm_log_dt, v_ssm_b_re, v_ssm_b_im, v_ssm_c_re, v_ssm_c_im, v_ssm_d, v_w_glu, v_b_glu, v_q_norm_g, v_k_norm_g, v_w_ssm_proj, v_w_attn_proj, v_w_out, v_norm_mlp_g, v_w_mlp_in, v_w_mlp_out, v_norm_final_g):
    w = dict(meta_tokens=meta_tokens, norm_mix_g=norm_mix_g, w_in=w_in, ssm_a_re=ssm_a_re, ssm_a_im=ssm_a_im, ssm_log_dt=ssm_log_dt, ssm_b_re=ssm_b_re, ssm_b_im=ssm_b_im, ssm_c_re=ssm_c_re, ssm_c_im=ssm_c_im, ssm_d=ssm_d, w_glu=w_glu, b_glu=b_glu, q_norm_g=q_norm_g, k_norm_g=k_norm_g, w_ssm_proj=w_ssm_proj, w_attn_proj=w_attn_proj, w_out=w_out, norm_mlp_g=norm_mlp_g, w_mlp_in=w_mlp_in, w_mlp_out=w_mlp_out, norm_final_g=norm_final_g)
    m = dict(meta_tokens=m_meta_tokens, norm_mix_g=m_norm_mix_g, w_in=m_w_in, ssm_a_re=m_ssm_a_re, ssm_a_im=m_ssm_a_im, ssm_log_dt=m_ssm_log_dt, ssm_b_re=m_ssm_b_re, ssm_b_im=m_ssm_b_im, ssm_c_re=m_ssm_c_re, ssm_c_im=m_ssm_c_im, ssm_d=m_ssm_d, w_glu=m_w_glu, b_glu=m_b_glu, q_norm_g=m_q_norm_g, k_norm_g=m_k_norm_g, w_ssm_proj=m_w_ssm_proj, w_attn_proj=m_w_attn_proj, w_out=m_w_out, norm_mlp_g=m_norm_mlp_g, w_mlp_in=m_w_mlp_in, w_mlp_out=m_w_mlp_out, norm_final_g=m_norm_final_g)
    v = dict(meta_tokens=v_meta_tokens, norm_mix_g=v_norm_mix_g, w_in=v_w_in, ssm_a_re=v_ssm_a_re, ssm_a_im=v_ssm_a_im, ssm_log_dt=v_ssm_log_dt, ssm_b_re=v_ssm_b_re, ssm_b_im=v_ssm_b_im, ssm_c_re=v_ssm_c_re, ssm_c_im=v_ssm_c_im, ssm_d=v_ssm_d, w_glu=v_w_glu, b_glu=v_b_glu, q_norm_g=v_q_norm_g, k_norm_g=v_k_norm_g, w_ssm_proj=v_w_ssm_proj, w_attn_proj=v_w_attn_proj, w_out=v_w_out, norm_mlp_g=v_norm_mlp_g, w_mlp_in=v_w_mlp_in, w_mlp_out=v_w_mlp_out, norm_final_g=v_norm_final_g)

    shard2d = {n: w[n].reshape(w[n].shape[-2:]) for n in BIG_WEIGHTS}
    big_shapes = [shard2d[n].shape for n in BIG_WEIGHTS]

    meta_hi = shard2d["meta_tokens"].astype(BF16)
    meta_res = shard2d["meta_tokens"] - meta_hi.astype(F32)
    meta_mid = meta_res.astype(BF16)
    meta_lo = (meta_res - meta_mid.astype(F32)).astype(BF16)
    shapes_of = lambda names: [shard2d[n].shape for n in names]

    def full_weights(names, shards):
        return {n: s if n in BLOCK_WEIGHTS else _from_shards(s, BIG_SHARD_AXIS[n]) for n, s in zip(names, shards)}

    early = _exchange([(_pack_rows([meta_hi, meta_mid, meta_lo, shard2d["w_in"].astype(BF16)], 0), False)],
                      "gather_early_weights")[0]
    shards = _unpack_rows(early, [meta_hi.shape] * 3 + shapes_of(EARLY_WEIGHTS[1:]))
    meta = [_from_shards(s, 1).astype(F32) for s in shards[0:3]]
    big = {"meta_tokens": (meta[0] + meta[1]) + meta[2], **full_weights(EARLY_WEIGHTS[1:], shards[3:])}
    small = {n: w[n] for n in SMALL_WEIGHTS}
    pack_grads = lambda names, grads: _pack_rows(
        [_to_shards(grads[n], BIG_SHARD_AXIS[n]) for n in names], 1).astype(BF16)
    comm = {
        "pack_weights": lambda names: _pack_rows([shard2d[n].astype(BF16) for n in names], 0),
        "unpack_weights": lambda names, g: full_weights(names, _unpack_rows(g, shapes_of(names))),
        "pack_grads": pack_grads,
        "pack_small_grads": lambda grads: _pack_flat([grads[n] for n in SMALL_WEIGHTS]),
    }

    loss, grad_x, grads, late_parts, small_parts = _local_step(x[0], loss_target[0], big, small, comm)
    loss = lax.psum(loss, ("x", "y", "c"))

    early_parts = _exchange([(pack_grads(EARLY_WEIGHTS, grads), True)], "scatter_early_grads")[0]
    pk = lambda names, src: _pack_rows([src[n].reshape(shard2d[n].shape) for n in names], 0)
    pe, pl_ = functools.partial(pk, EARLY_WEIGHTS), functools.partial(pk, LATE_WEIGHTS)
    early_out = _reduce_adamw(early_parts, pe(w), pe(m), pe(v), "adamw_sharded_early")
    late_out = _reduce_adamw(late_parts, pl_(w), pl_(m), pl_(v), "adamw_sharded_late")
    small_shapes = [w[n].shape for n in SMALL_WEIGHTS]
    pf = lambda src: _pack_flat([src[n] for n in SMALL_WEIGHTS])
    small_out = _reduce_adamw(small_parts, pf(w), pf(m), pf(v), "adamw_replicated")

    results = []
    for kind in range(4):
        big_un = dict(zip(EARLY_WEIGHTS + LATE_WEIGHTS,
                          _unpack_rows(early_out[kind], shapes_of(EARLY_WEIGHTS))
                          + _unpack_rows(late_out[kind], shapes_of(LATE_WEIGHTS))))
        small_un = dict(zip(SMALL_WEIGHTS, _unpack(small_out[kind], small_shapes)))
        for n in ALL_WEIGHTS:
            results.append(big_un[n].reshape(w[n].shape) if n in big_un else small_un[n])
    return (loss, grad_x[None], *results)
```

```python
import functools
import math

import jax
import jax.numpy as jnp
from jax import lax
from jax.experimental import pallas as pl
from jax.experimental.pallas import tpu as pltpu

F32 = jnp.float32
BF16 = jnp.bfloat16

N_DEV = 8
N_META = 16
GRID_W = 64
SSM_GROUP = 16
SSM_STATE = 64
HEAD_DIM = 64
KV_REP = 4
ROPE_THETA = 10000.0
NORM_EPS = 1e-6
EIG_RE_MAX = -1e-4
ATTN_SCALE = HEAD_DIM ** -0.5

ADAM_LR = 0.001
ADAM_B1 = 0.9
ADAM_B2 = 0.999
ADAM_EPS = 1e-08
ADAM_WD = 0.01
ADAM_STEP = 10

ROW_TILE = 384
ROW_TILE_BWD = 256
QUERY_STRIP = 256
VT_ROWS = 80
MASK_BIAS = -1e30
SCAN_LANES = 512
DIAG_TILE = 256
KV_TILE = 768
PACK_W = 1024
V7X_VMEM_LIMIT = 56 * 1024 * 1024
NEG_BIG = -1e30

BIG_WEIGHTS = ("meta_tokens", "w_in", "w_glu", "w_ssm_proj", "w_attn_proj", "w_out", "w_mlp_in", "w_mlp_out")
BIG_SHARD_AXIS = {"meta_tokens": 1, "w_in": 1, "w_glu": 0, "w_ssm_proj": 1, "w_attn_proj": 0, "w_out": 0,
                  "w_mlp_in": 1, "w_mlp_out": 0}
BLOCK_WEIGHTS = ("w_in", "w_mlp_in", "w_mlp_out")
EARLY_WEIGHTS = ("meta_tokens", "w_in")
MIXER_WEIGHTS = ("w_glu", "w_ssm_proj", "w_attn_proj", "w_out")
MLP_WEIGHTS = ("w_mlp_in", "w_mlp_out")
LATE_WEIGHTS = MIXER_WEIGHTS + MLP_WEIGHTS
SMALL_WEIGHTS = ("norm_mix_g", "ssm_a_re", "ssm_a_im", "ssm_log_dt", "ssm_b_re", "ssm_b_im", "ssm_c_re",
                 "ssm_c_im", "ssm_d", "b_glu", "q_norm_g", "k_norm_g", "norm_mlp_g", "norm_final_g")
ALL_WEIGHTS = ("meta_tokens", "norm_mix_g", "w_in", "ssm_a_re", "ssm_a_im", "ssm_log_dt", "ssm_b_re", "ssm_b_im",
               "ssm_c_re", "ssm_c_im", "ssm_d", "w_glu", "b_glu", "q_norm_g", "k_norm_g", "w_ssm_proj",
               "w_attn_proj", "w_out", "norm_mlp_g", "w_mlp_in", "w_mlp_out", "norm_final_g")


def _round_up(n, m):
    return (n + m - 1) // m * m


def _pcall(body, *, name, grid, in_specs, out_specs, out_shape, scratch=(), vmem=None, **kw):
    params = pltpu.CompilerParams(dimension_semantics=("arbitrary",) * len(grid), vmem_limit_bytes=vmem)
    return pl.pallas_call(body, name=name, grid=grid, in_specs=in_specs, out_specs=out_specs, out_shape=out_shape,
                          scratch_shapes=list(scratch), compiler_params=params, **kw)


def _dot(a, b):
    return jnp.dot(a, b, preferred_element_type=F32)


def _dot_nt(a, b):
    return lax.dot_general(a, b, (((1,), (1,)), ((), ())), preferred_element_type=F32)


def _dot_tn(a, b):
    return lax.dot_general(a, b, (((0,), (0,)), ((), ())), preferred_element_type=F32)


def _full_spec(shape):
    nd = len(shape)
    return pl.BlockSpec(shape, lambda *_: (0,) * nd)


def _row_spec(tm, width):
    return pl.BlockSpec((tm, width), lambda i: (i, 0))


def _heads_spec(nh, tm):
    return pl.BlockSpec((nh, tm, HEAD_DIM), lambda i: (0, i, 0))


_ANY = pl.BlockSpec(memory_space=pl.ANY)


def _load_once(step, pairs, sem):
    @pl.when(step == 0)
    def _():
        copies = [pltpu.make_async_copy(src, dst, sem.at[k]) for k, (src, dst) in enumerate(pairs)]
        for cp in copies:
            cp.start()
        for cp in copies:
            cp.wait()


def _swap_pairs(x, even):
    n = x.shape[-1]
    return jnp.where(even, pltpu.roll(x, n - 1, 1), pltpu.roll(x, 1, 1))


def _gelu(y):
    return 0.5 * y * (1.0 + lax.erf(y * (1.0 / math.sqrt(2.0))))


def _gelu_grad(y):
    return 0.5 * (1.0 + lax.erf(y * (1.0 / math.sqrt(2.0)))) + y * jnp.exp(-0.5 * y * y) * (1.0 / math.sqrt(2.0 * math.pi))


def _in_proj_fwd(x0, g_mix, w_in, qg, kg, cos, sin, n_valid, exchange=None):
    t, d = x0.shape
    tm = ROW_TILE
    du, dk = d // 2, d // 4
    nh, nkv = d // HEAD_DIM, d // HEAD_DIM // KV_REP
    bw = w_in.shape[-1]
    assert bw == du and dk * 2 == bw

    def body(*refs):
        i = pl.program_id(0)
        (x_ref, g_ref, w_hbm, qg_ref, kg_ref, c_ref, s_ref,
         h_ref, u_ref, ub_ref, qraw_ref, kraw_ref, qat_ref, ka_ref, kt_ref, va_ref, vta_ref, gates_ref,
         w_ref, sem) = _riding_exchange(refs, exchange, 7, 11, i == 0, i == t // tm - 1)
        _load_once(i, [(w_hbm, w_ref)], sem)
        x = x_ref[...]
        r = lax.rsqrt(jnp.mean(x * x, axis=-1, keepdims=True) + NORM_EPS)
        h = ((x * r) * g_ref[...]).astype(BF16)
        h_ref[...] = h
        u = _dot(h, w_ref[0])
        u_ref[...] = u
        ub_ref[...] = u.astype(BF16)
        lane = lax.broadcasted_iota(jnp.int32, (tm, 128), 1)
        lo = lane < HEAD_DIM
        even = (lane & 1) == 0
        aug = lane == HEAD_DIM
        c = c_ref[...]
        s = s_ref[...]
        row = i * tm + lax.broadcasted_iota(jnp.int32, (tm, 1), 0)
        one = jnp.where(aug, 1.0, 0.0)
        key_bias = jnp.where(jnp.logical_and(aug, row >= n_valid), MASK_BIAS, 0.0)

        def norm_rope(blk, g128):
            sq = blk * blk
            ms_lo = jnp.sum(jnp.where(lo, sq, 0.0), axis=-1, keepdims=True) * (1.0 / HEAD_DIM)
            ms_hi = jnp.sum(jnp.where(lo, 0.0, sq), axis=-1, keepdims=True) * (1.0 / HEAD_DIM)
            rr = jnp.where(lo, lax.rsqrt(ms_lo + NORM_EPS), lax.rsqrt(ms_hi + NORM_EPS))
            qn = (blk * rr) * g128
            return qn * c + _swap_pairs(qn, even) * s

        def put_heads(rows_ref, cols_ref, first, pair, extra):
            for k, head in enumerate((pair, pltpu.roll(pair, HEAD_DIM, 1))):
                wide = jnp.where(lo, head, extra)
                if rows_ref is not None:
                    rows_ref[first + k] = wide.astype(BF16)
                cols_ref[first + k] = wide.T[0:cols_ref.shape[1], :].astype(BF16)

        for blk in range(2):
            qb = _dot(h, w_ref[1 + blk])
            qraw_ref[:, bw * blk:bw * (blk + 1)] = qb
            for a in range(bw // 128):
                put_heads(None, qat_ref, (bw // HEAD_DIM) * blk + 2 * a,
                          norm_rope(qb[:, 128 * a:128 * (a + 1)], qg_ref[...]) * ATTN_SCALE, one)
        kv = _dot(h, w_ref[3])
        kraw_ref[...] = kv[:, 0:dk]
        for a in range(nkv // 2):
            put_heads(ka_ref, kt_ref, 2 * a, norm_rope(kv[:, 128 * a:128 * (a + 1)], kg_ref[...]), key_bias)
            put_heads(va_ref, vta_ref, 2 * a, kv[:, dk + 128 * a:dk + 128 * (a + 1)], one)
        for blk in range(4):
            gates_ref[:, bw * blk:bw * (blk + 1)] = _dot(h, w_ref[4 + blk])

    heads = lambda n: pl.BlockSpec((n, tm, 128), lambda i: (0, i, 0))
    heads_t = lambda n, rows: pl.BlockSpec((n, rows, tm), lambda i: (0, 0, i))
    extra = exchange is not None
    return _pcall(
        body, name="in_proj_fwd", grid=(t // tm,),
        in_specs=[_row_spec(tm, d), _full_spec((1, d)), _ANY, _full_spec((1, 128)), _full_spec((1, 128)),
                  _row_spec(tm, 128), _row_spec(tm, 128)] + [_ANY] * extra,
        out_specs=[_row_spec(tm, d), _row_spec(tm, du), _row_spec(tm, du), _row_spec(tm, d), _row_spec(tm, dk),
                   heads_t(nh, 128), heads(nkv), heads_t(nkv, HEAD_DIM), heads(nkv), heads_t(nkv, VT_ROWS),
                   _row_spec(tm, 2 * d)] + [_ANY] * extra,
        out_shape=[jax.ShapeDtypeStruct((t, d), BF16), jax.ShapeDtypeStruct((t, du), F32),
                   jax.ShapeDtypeStruct((t, du), BF16), jax.ShapeDtypeStruct((t, d), F32),
                   jax.ShapeDtypeStruct((t, dk), F32), jax.ShapeDtypeStruct((nh, 128, t), BF16),
                   jax.ShapeDtypeStruct((nkv, t, 128), BF16), jax.ShapeDtypeStruct((nkv, HEAD_DIM, t), BF16),
                   jax.ShapeDtypeStruct((nkv, t, 128), BF16), jax.ShapeDtypeStruct((nkv, VT_ROWS, t), BF16),
                   jax.ShapeDtypeStruct((t, 2 * d), F32)] + ([_exchange_out_shape(*exchange)] if extra else []),
        scratch=[pltpu.VMEM((N_DEV, d, bw), BF16), pltpu.SemaphoreType.DMA((1,))] + _EXCHANGE_SEMS * extra,
        vmem=V7X_VMEM_LIMIT,
    )(x0, g_mix, w_in, qg, kg, cos, sin, *([exchange[0]] if extra else []))


def _mixer_values(u, yf, yb, yt_attn, gates, d_ref, wg_ref, bg_ref, ps_ref, pa_ref, d):
    y = (u * d_ref[...] + yf) + yb
    z = _gelu(y)
    sg = jax.nn.sigmoid(_dot(z.astype(BF16), wg_ref[...]) + bg_ref[...])
    y_ssm = z * sg
    a_ssm = _dot(y_ssm.astype(BF16), ps_ref[...])
    a_attn = _dot_tn(yt_attn.astype(BF16), pa_ref[...])
    s_ssm = jax.nn.sigmoid(gates[:, 0:d])
    s_attn = jax.nn.sigmoid(gates[:, d:2 * d])
    merged = s_ssm * a_ssm + s_attn * a_attn
    return y, z, sg, y_ssm, a_ssm, a_attn, s_ssm, s_attn, merged


def _mixer_out_fwd(x0, u, yf, yb, y_attn, gates, ssm_d, w_glu, b_glu, p_ssm, p_attn, w_out):
    t, d = x0.shape
    tm = ROW_TILE
    du = d // 2

    def body(x_ref, u_ref, yf_ref, yb_ref, ya_ref, gt_ref, d_ref, wg_ref, bg_ref, ps_ref, pa_ref, wo_ref, x1_ref):
        vals = _mixer_values(u_ref[...], yf_ref[...], yb_ref[...], ya_ref[...], gt_ref[...],
                             d_ref, wg_ref, bg_ref, ps_ref, pa_ref, d)
        merged = vals[-1]
        x1_ref[...] = x_ref[...] + _dot(merged.astype(BF16), wo_ref[...])

    return _pcall(
        body, name="mixer_out_fwd", grid=(t // tm,),
        in_specs=[_row_spec(tm, d), _row_spec(tm, du), _row_spec(tm, du), _row_spec(tm, du),
                  pl.BlockSpec((d, tm), lambda i: (0, i)), _row_spec(tm, 2 * d), _full_spec((1, du)), _full_spec((du, du)), _full_spec((1, du)),
                  _full_spec((du, d)), _full_spec((d, d)), _full_spec((d, d))],
        out_specs=_row_spec(tm, d), out_shape=jax.ShapeDtypeStruct((t, d), F32), vmem=V7X_VMEM_LIMIT,
    )(x0, u, yf, yb, y_attn, gates, ssm_d, w_glu, b_glu, p_ssm, p_attn, w_out)


def _mlp_loss_fwd_bwd(x1, target, g_mlp, g_fin, w1, w2, n_valid):
    t, d = x1.shape
    tm = ROW_TILE_BWD
    dff = 4 * d
    nfc, _, fc = w1.shape

    def body(x_ref, tg_ref, gm_ref, gf_ref, w1_hbm, w2_hbm,
             dx1_ref, loss_ref, dgf_ref, dgm_ref, h2_ref, da_ref, hsq_ref, dx2b_ref,
             w1_ref, w2_ref, relu_ref, sem):
        i = pl.program_id(0)
        _load_once(i, [(w1_hbm, w1_ref), (w2_hbm, w2_ref)], sem)

        @pl.when(i == 0)
        def _():
            loss_ref[...] = jnp.zeros_like(loss_ref)
            dgf_ref[...] = jnp.zeros_like(dgf_ref)
            dgm_ref[...] = jnp.zeros_like(dgm_ref)

        x1v = x_ref[...]
        r1 = lax.rsqrt(jnp.mean(x1v * x1v, axis=-1, keepdims=True) + NORM_EPS)
        xh1 = x1v * r1
        h2b = (xh1 * gm_ref[...]).astype(BF16)
        h2_ref[...] = h2b
        acc = jnp.zeros((tm, d), F32)
        for c in range(nfc):
            a = jnp.maximum(_dot(h2b, w1_ref[c]), 0.0)
            relu_ref[:, fc * c:fc * (c + 1)] = a
            hs = (a * a).astype(BF16)
            hsq_ref[:, fc * c:fc * (c + 1)] = hs
            acc = acc + _dot(hs, w2_ref[c])
        x2 = x1v + acc
        r2 = lax.rsqrt(jnp.mean(x2 * x2, axis=-1, keepdims=True) + NORM_EPS)
        xh2 = x2 * r2
        out = xh2 * gf_ref[...]
        row = i * tm + lax.broadcasted_iota(jnp.int32, (tm, 1), 0)
        valid = jnp.logical_and(row >= N_META, row < n_valid)
        diff = jnp.where(valid, out - tg_ref[...], 0.0)
        loss_ref[...] += 0.5 * jnp.sum(jnp.sum(diff * diff, axis=-1, keepdims=True) * (1.0 / d))
        dout = diff * (1.0 / d)
        dgf_ref[...] += jnp.sum(dout * xh2, axis=0, keepdims=True)
        dxh2 = dout * gf_ref[...]
        dx2 = r2 * (dxh2 - xh2 * jnp.mean(dxh2 * xh2, axis=-1, keepdims=True))
        dx2b = dx2.astype(BF16)
        dx2b_ref[...] = dx2b
        dh2 = jnp.zeros((tm, d), F32)
        for c in range(nfc):
            dhs = _dot_nt(dx2b, w2_ref[c])
            da = (dhs * (2.0 * relu_ref[:, fc * c:fc * (c + 1)])).astype(BF16)
            da_ref[:, fc * c:fc * (c + 1)] = da
            dh2 = dh2 + _dot_nt(da, w1_ref[c])
        dgm_ref[...] += jnp.sum(dh2 * xh1, axis=0, keepdims=True)
        dxh1 = dh2 * gm_ref[...]
        dx1_ref[...] = dx2 + r1 * (dxh1 - xh1 * jnp.mean(dxh1 * xh1, axis=-1, keepdims=True))

    return _pcall(
        body, name="mlp_loss_fwd_bwd", grid=(t // tm,),
        in_specs=[_row_spec(tm, d), _row_spec(tm, d), _full_spec((1, d)), _full_spec((1, d)), _ANY, _ANY],
        out_specs=[_row_spec(tm, d), _full_spec((8, 128)), _full_spec((1, d)), _full_spec((1, d)),
                   _row_spec(tm, d), _row_spec(tm, dff), _row_spec(tm, dff), _row_spec(tm, d)],
        out_shape=[jax.ShapeDtypeStruct((t, d), F32), jax.ShapeDtypeStruct((8, 128), F32),
                   jax.ShapeDtypeStruct((1, d), F32), jax.ShapeDtypeStruct((1, d), F32),
                   jax.ShapeDtypeStruct((t, d), BF16), jax.ShapeDtypeStruct((t, dff), BF16),
                   jax.ShapeDtypeStruct((t, dff), BF16), jax.ShapeDtypeStruct((t, d), BF16)],
        scratch=[pltpu.VMEM((nfc, d, fc), BF16), pltpu.VMEM((nfc, fc, d), BF16), pltpu.VMEM((tm, dff), F32),
                 pltpu.SemaphoreType.DMA((2,))],
        vmem=V7X_VMEM_LIMIT,
    )(x1, target, g_mlp, g_fin, w1, w2)


def _mixer_out_bwd(dx1, u, yf, yb, yt_attn, gates, ssm_d, w_glu, b_glu, p_ssm, p_attn, w_out):
    t, d = dx1.shape
    tm = ROW_TILE_BWD
    du = d // 2

    def body(dx_ref, u_ref, yf_ref, yb_ref, yt_ref, gt_ref, d_ref, wg_ref, bg_ref, ps_ref, pa_ref, wo_ref,
             dyb_ref, dud_ref, dyat_ref, dgates_ref, zb_ref, dglb_ref, ysb_ref, dasb_ref, daab_ref,
             mgb_ref, dxb_ref, dd_ref, dbg_ref):
        i = pl.program_id(0)

        @pl.when(i == 0)
        def _():
            dd_ref[...] = jnp.zeros_like(dd_ref)
            dbg_ref[...] = jnp.zeros_like(dbg_ref)

        uv = u_ref[...]
        y, z, sg, y_ssm, a_ssm, a_attn, s_ssm, s_attn, merged = _mixer_values(
            uv, yf_ref[...], yb_ref[...], yt_ref[...], gt_ref[...], d_ref, wg_ref, bg_ref, ps_ref, pa_ref, d)
        dxb = dx_ref[...].astype(BF16)
        dxb_ref[...] = dxb
        mgb_ref[...] = merged.astype(BF16)
        dmerged = _dot_nt(dxb, wo_ref[...])
        dgates_ref[:, 0:d] = (dmerged * a_ssm * (s_ssm * (1.0 - s_ssm))).astype(BF16)
        dgates_ref[:, d:2 * d] = (dmerged * a_attn * (s_attn * (1.0 - s_attn))).astype(BF16)
        da_ssm = (dmerged * s_ssm).astype(BF16)
        da_attn = (dmerged * s_attn).astype(BF16)
        dasb_ref[...] = da_ssm
        daab_ref[...] = da_attn
        ysb_ref[...] = y_ssm.astype(BF16)
        dy_ssm = _dot_nt(da_ssm, ps_ref[...])
        dyat_ref[...] = _dot_nt(pa_ref[...], da_attn).astype(BF16)
        dgl = dy_ssm * z * (sg * (1.0 - sg))
        dglb = dgl.astype(BF16)
        dglb_ref[...] = dglb
        zb_ref[...] = z.astype(BF16)
        dbg_ref[...] += jnp.sum(dgl, axis=0, keepdims=True)
        dz = dy_ssm * sg + _dot_nt(dglb, wg_ref[...])
        dy = dz * _gelu_grad(y)
        dyb_ref[...] = dy.astype(BF16)
        dd_ref[...] += jnp.sum(dy * uv, axis=0, keepdims=True)
        dud_ref[...] = dy * d_ref[...]

    bf = lambda w: jax.ShapeDtypeStruct((t, w), BF16)
    return _pcall(
        body, name="mixer_out_bwd", grid=(t // tm,),
        in_specs=[_row_spec(tm, d), _row_spec(tm, du), _row_spec(tm, du), _row_spec(tm, du),
                  pl.BlockSpec((d, tm), lambda i: (0, i)),
                  _row_spec(tm, 2 * d), _full_spec((1, du)), _full_spec((du, du)), _full_spec((1, du)),
                  _full_spec((du, d)), _full_spec((d, d)), _full_spec((d, d))],
        out_specs=[_row_spec(tm, du), _row_spec(tm, du), pl.BlockSpec((d, tm), lambda i: (0, i)),
                   _row_spec(tm, 2 * d), _row_spec(tm, du), _row_spec(tm, du), _row_spec(tm, du), _row_spec(tm, d),
                   _row_spec(tm, d), _row_spec(tm, d), _row_spec(tm, d), _full_spec((1, du)), _full_spec((1, du))],
        out_shape=[bf(du), jax.ShapeDtypeStruct((t, du), F32), jax.ShapeDtypeStruct((d, t), BF16), bf(2 * d),
                   bf(du), bf(du), bf(du), bf(d), bf(d), bf(d), bf(d),
                   jax.ShapeDtypeStruct((1, du), F32), jax.ShapeDtypeStruct((1, du), F32)],
        vmem=V7X_VMEM_LIMIT,
    )(dx1, u, yf, yb, yt_attn, gates, ssm_d, w_glu, b_glu, p_ssm, p_attn, w_out)


def _in_proj_bwd(x0, dx1, dud, duf, dub, qraw, kraw, dq, dk, dv, dgates, g_mix, w_in, qg, kg, cos, sin):
    t, d = x0.shape
    tm = ROW_TILE_BWD
    du, dkw = d // 2, d // 4
    nh, nkv = d // HEAD_DIM, d // HEAD_DIM // KV_REP
    bw = w_in.shape[-1]
    o_q, o_k, o_v, o_g = du, du + d, du + d + dkw, 2 * d

    def body(x_ref, dx1_ref, dud_ref, duf_ref, dub_ref, qraw_ref, kraw_ref, dq_ref, dk_ref, dv_ref, dgt_ref,
             g_ref, w_hbm, qg_ref, kg_ref, c_ref, s_ref,
             dx0_ref, dproj_ref, dgm_ref, dqg_ref, dkg_ref,
             w_ref, kv_ref, sem):
        i = pl.program_id(0)
        _load_once(i, [(w_hbm, w_ref)], sem)

        @pl.when(i == 0)
        def _():
            dgm_ref[...] = jnp.zeros_like(dgm_ref)
            dqg_ref[...] = jnp.zeros_like(dqg_ref)
            dkg_ref[...] = jnp.zeros_like(dkg_ref)

        lane = lax.broadcasted_iota(jnp.int32, (tm, 128), 1)
        lo = lane < HEAD_DIM
        even = (lane & 1) == 0
        c = c_ref[...]
        s = s_ref[...]

        def norm_rope_bwd(dout, raw, g128):
            sq = raw * raw
            ms_lo = jnp.sum(jnp.where(lo, sq, 0.0), axis=-1, keepdims=True) * (1.0 / HEAD_DIM)
            ms_hi = jnp.sum(jnp.where(lo, 0.0, sq), axis=-1, keepdims=True) * (1.0 / HEAD_DIM)
            rr = jnp.where(lo, lax.rsqrt(ms_lo + NORM_EPS), lax.rsqrt(ms_hi + NORM_EPS))
            xh = raw * rr
            dqn = dout * c + _swap_pairs(dout * s, even)
            dg = jnp.sum(dqn * xh, axis=0, keepdims=True)
            tt = dqn * g128
            pr = tt * xh
            mu_lo = jnp.sum(jnp.where(lo, pr, 0.0), axis=-1, keepdims=True) * (1.0 / HEAD_DIM)
            mu_hi = jnp.sum(jnp.where(lo, 0.0, pr), axis=-1, keepdims=True) * (1.0 / HEAD_DIM)
            return rr * (tt - xh * jnp.where(lo, mu_lo, mu_hi)), dg

        dub_tot = (dud_ref[...] + duf_ref[...]) + dub_ref[...]
        dproj_ref[:, 0:du] = dub_tot.astype(BF16)
        dqg = jnp.zeros((1, 128), F32)
        for a in range(nh // 2):
            sl = slice(128 * a, 128 * (a + 1))
            draw, dg = norm_rope_bwd(dq_ref[sl, :].T * ATTN_SCALE, qraw_ref[:, sl], qg_ref[...])
            dqg = dqg + dg
            dproj_ref[:, o_q + 128 * a:o_q + 128 * (a + 1)] = draw.astype(BF16)
        dqg_ref[...] += dqg
        for hh in range(nkv):
            kv_ref[:, HEAD_DIM * hh:HEAD_DIM * (hh + 1)] = dk_ref[hh, :, 0:HEAD_DIM]
        dkg = jnp.zeros((1, 128), F32)
        for a in range(nkv // 2):
            sl = slice(128 * a, 128 * (a + 1))
            draw, dg = norm_rope_bwd(kv_ref[:, sl], kraw_ref[:, sl], kg_ref[...])
            dkg = dkg + dg
            dproj_ref[:, o_k + 128 * a:o_k + 128 * (a + 1)] = draw.astype(BF16)
        dkg_ref[...] += dkg
        for hh in range(nkv):
            kv_ref[:, HEAD_DIM * hh:HEAD_DIM * (hh + 1)] = dv_ref[hh]
        dproj_ref[:, o_v:o_g] = kv_ref[...].astype(BF16)
        dproj_ref[:, o_g:4 * d] = dgt_ref[...]
        dh = jnp.zeros((tm, d), F32)
        for blk in range(N_DEV):
            dh = dh + _dot_nt(dproj_ref[:, bw * blk:bw * (blk + 1)], w_ref[blk])
        x = x_ref[...]
        r = lax.rsqrt(jnp.mean(x * x, axis=-1, keepdims=True) + NORM_EPS)
        xh0 = x * r
        dgm_ref[...] += jnp.sum(dh * xh0, axis=0, keepdims=True)
        dxh = dh * g_ref[...]
        dx0_ref[...] = dx1_ref[...] + r * (dxh - xh0 * jnp.mean(dxh * xh0, axis=-1, keepdims=True))

    return _pcall(
        body, name="in_proj_bwd", grid=(t // tm,),
        in_specs=[_row_spec(tm, d), _row_spec(tm, d), _row_spec(tm, du), _row_spec(tm, du), _row_spec(tm, du),
                  _row_spec(tm, d), _row_spec(tm, dkw), pl.BlockSpec((d, tm), lambda i: (0, i)),
                  pl.BlockSpec((nkv, tm, 128), lambda i: (0, i, 0)), _heads_spec(nkv, tm),
                  _row_spec(tm, 2 * d), _full_spec((1, d)), _ANY, _full_spec((1, 128)), _full_spec((1, 128)),
                  _row_spec(tm, 128), _row_spec(tm, 128)],
        out_specs=[_row_spec(tm, d), _row_spec(tm, 4 * d), _full_spec((1, d)), _full_spec((1, 128)),
                   _full_spec((1, 128))],
        out_shape=[jax.ShapeDtypeStruct((t, d), F32), jax.ShapeDtypeStruct((t, 4 * d), BF16),
                   jax.ShapeDtypeStruct((1, d), F32), jax.ShapeDtypeStruct((1, 128), F32),
                   jax.ShapeDtypeStruct((1, 128), F32)],
        scratch=[pltpu.VMEM((N_DEV, d, bw), BF16), pltpu.VMEM((tm, dkw), F32), pltpu.SemaphoreType.DMA((1,))],
        vmem=V7X_VMEM_LIMIT,
    )(x0, dx1, dud, duf, dub, qraw, kraw, dq, dk, dv, dgates, g_mix, w_in, qg, kg, cos, sin)


def _attn_fwd(qat, ka, vta):
    nh, _, t = qat.shape
    nkv = ka.shape[0]
    rep = nh // nkv
    hd = HEAD_DIM
    vr = vta.shape[1]
    tq = tk = KV_TILE

    def body(qt_ref, k_ref, vt_ref, ot_ref, lse_ref, m_scr, acc_scr):
        j = pl.program_id(2)

        @pl.when(j == 0)
        def _():
            m_scr[...] = jnp.full(m_scr.shape, NEG_BIG, F32)
            acc_scr[...] = jnp.zeros_like(acc_scr)

        k = k_ref[0]
        vt = vt_ref[0]
        strips = [(r, c) for r in range(rep) for c in range(0, tq, QUERY_STRIP)]
        scores = lambda r, c: _dot(k, qt_ref[r, :, c:c + QUERY_STRIP])
        def add_values(r, cols, alpha, pt):
            acc_scr[r, :, cols] = alpha * acc_scr[r, :, cols] + _dot(vt, pt)

        ahead = [scores(*strips[0]), scores(*strips[1])]
        pending = None
        for n, (r, c) in enumerate(strips):
            st = ahead.pop(0)
            if n + 2 < len(strips):
                ahead.append(scores(*strips[n + 2]))
            cols = slice(c, c + QUERY_STRIP)
            m_prev = m_scr[r, :, cols]
            m_next = jnp.maximum(m_prev, jnp.max(st, axis=0, keepdims=True))
            pt = jnp.exp(st - m_next).astype(BF16)
            m_scr[r, :, cols] = m_next
            if pending is not None:
                add_values(*pending)
            pending = (r, cols, jnp.exp(m_prev - m_next), pt)
        add_values(*pending)

        @pl.when(j == pl.num_programs(2) - 1)
        def _():
            for r in range(rep):
                l = acc_scr[r, hd:hd + 1, :]
                ot_ref[hd * r:hd * (r + 1), :] = acc_scr[r, 0:hd, :] / l
                lse_ref[0, r:r + 1, :] = m_scr[r] + jnp.log(l)

    return _pcall(
        body, name="attn_fwd", grid=(nkv, t // tq, t // tk),
        in_specs=[pl.BlockSpec((rep, 128, tq), lambda g, i, j: (g, 0, i)),
                  pl.BlockSpec((1, tk, 128), lambda g, i, j: (g, j, 0)),
                  pl.BlockSpec((1, vr, tk), lambda g, i, j: (g, 0, j))],
        out_specs=[pl.BlockSpec((rep * hd, tq), lambda g, i, j: (g, i)),
                   pl.BlockSpec((1, rep, tq), lambda g, i, j: (g, 0, i))],
        out_shape=[jax.ShapeDtypeStruct((nh * hd, t), F32), jax.ShapeDtypeStruct((nkv, rep, t), F32)],
        scratch=[pltpu.VMEM((rep, 1, tq), F32), pltpu.VMEM((rep, vr, tq), F32)],
        vmem=V7X_VMEM_LIMIT,
    )(qat, ka, vta)


def _attn_bwd(qat, ka, kt, va, dot, ot, lse_row):
    nh, _, t = qat.shape
    nkv = ka.shape[0]
    rep = nh // nkv
    hd = HEAD_DIM
    tq = tk = KV_TILE

    def body(qt_ref, k_ref, kt_ref, v_ref, dot_ref, ot_ref, lse_ref, dk_ref, dv_ref, dqt_ref):
        j = pl.program_id(1)
        i = pl.program_id(2)

        @pl.when(jnp.logical_and(j == 0, i == 0))
        def _():
            dqt_ref[...] = jnp.zeros_like(dqt_ref)

        @pl.when(i == 0)
        def _():
            dk_ref[...] = jnp.zeros_like(dk_ref)
            dv_ref[...] = jnp.zeros_like(dv_ref)

        k = k_ref[0]
        kt = kt_ref[0]
        v = v_ref[0, :, 0:hd]
        cols = pl.ds(pl.multiple_of(i * tq, tq), tq)
        dk = jnp.zeros((tk, 128), F32)
        dv = jnp.zeros((tk, hd), F32)
        products = lambda r: (_dot(k, qt_ref[r]), _dot(v, dot_ref[hd * r:hd * (r + 1), :]))
        nxt = products(0)
        for r in range(rep):
            st, dpt = nxt
            if r + 1 < rep:
                nxt = products(r + 1)
            heads = slice(hd * r, hd * (r + 1))
            qt = qt_ref[r]
            dot_r = dot_ref[heads, :]
            delta = jnp.sum(dot_r.astype(F32) * ot_ref[heads, :], axis=0, keepdims=True)
            pt = jnp.exp(st - lse_ref[0, r:r + 1, :])
            dst = (pt * (dpt - delta)).astype(BF16)
            dv = dv + _dot_nt(pt.astype(BF16), dot_r)
            dk = dk + _dot_nt(dst, qt)
            dqt_ref[heads, cols] += _dot(kt, dst)
        dk_ref[0] += dk
        dv_ref[0] += dv

    return _pcall(
        body, name="attn_bwd", grid=(nkv, t // tk, t // tq),
        in_specs=[pl.BlockSpec((rep, 128, tq), lambda g, j, i: (g, 0, i)),
                  pl.BlockSpec((1, tk, 128), lambda g, j, i: (g, j, 0)),
                  pl.BlockSpec((1, hd, tk), lambda g, j, i: (g, 0, j)),
                  pl.BlockSpec((1, tk, 128), lambda g, j, i: (g, j, 0)),
                  pl.BlockSpec((rep * hd, tq), lambda g, j, i: (g, i)),
                  pl.BlockSpec((rep * hd, tq), lambda g, j, i: (g, i)),
                  pl.BlockSpec((1, rep, tq), lambda g, j, i: (g, 0, i))],
        out_specs=[pl.BlockSpec((1, tk, 128), lambda g, j, i: (g, j, 0)),
                   pl.BlockSpec((1, tk, hd), lambda g, j, i: (g, j, 0)),
                   pl.BlockSpec((rep * hd, t), lambda g, j, i: (g, 0))],
        out_shape=[jax.ShapeDtypeStruct((nkv, t, 128), F32), jax.ShapeDtypeStruct((nkv, t, hd), F32),
                   jax.ShapeDtypeStruct((nh * hd, t), F32)],
        vmem=V7X_VMEM_LIMIT,
    )(qat, ka, kt, va, dot, ot, lse_row)


def _riding_exchange(refs, exchange, n_in, n_out, first_step, last_step):
    if exchange is None:
        return refs
    x_ref, out_ref = refs[n_in], refs[n_in + 1 + n_out]
    sems = refs[-3:]

    @pl.when(first_step)
    def _():
        _start_all(*_exchange_copies(x_ref, out_ref, *sems, exchange[1]))

    @pl.when(last_step)
    def _():
        _wait_all(*_exchange_copies(x_ref, out_ref, *sems, exchange[1]))

    return refs[:n_in] + refs[n_in + 1:n_in + 1 + n_out] + refs[n_in + 2 + n_out:-3]


def _segmented_scan(src_re, src_im, dst_re, dst_im, lam_re, lam_im, pow_re, pow_im, carry_re, carry_im,
                    end_re, end_im, in_re, in_im, lanes, descending, conj):
    tc = src_re.shape[0]
    seg = tc // 8
    width = lanes.size
    sign = -1.0 if conj else 1.0
    rows_of = lambda q: pl.ds(8 * (seg - 1 - q if descending else q), 8)
    lr = jnp.broadcast_to(lam_re[:, lanes], (8, width))
    li = jnp.broadcast_to(sign * lam_im[:, lanes], (8, width))
    xr = jnp.zeros((8, width), F32)
    xi = jnp.zeros((8, width), F32)
    for q in range(seg):
        rows = rows_of(q)
        xr, xi = (lr * xr - li * xi) + src_re[rows, lanes], (lr * xi + li * xr) + src_im[rows, lanes]
        dst_re[rows, lanes] = xr
        dst_im[rows, lanes] = xi
    end_re[:, lanes] = xr
    end_im[:, lanes] = xi
    sr = pow_re[seg - 1:seg, lanes]
    si = sign * pow_im[seg - 1:seg, lanes]
    cr = carry_re[:, lanes]
    ci = carry_im[:, lanes]
    for s in range(8):
        se = 7 - s if descending else s
        in_re[se:se + 1, lanes] = cr
        in_im[se:se + 1, lanes] = ci
        cr, ci = (end_re[se:se + 1, lanes] + (sr * cr - si * ci)), (end_im[se:se + 1, lanes] + (sr * ci + si * cr))
    carry_re[:, lanes] = cr
    carry_im[:, lanes] = ci
    ir = in_re[:, lanes]
    ii = in_im[:, lanes]
    for q in range(seg):
        rows = rows_of(q)
        pr = pow_re[q:q + 1, lanes]
        pi = sign * pow_im[q:q + 1, lanes]
        dst_re[rows, lanes] = dst_re[rows, lanes] + (pr * ir - pi * ii)
        dst_im[rows, lanes] = dst_im[rows, lanes] + (pr * ii + pi * ir)


def _diag_tiles(gn):
    rows_per_tile = DIAG_TILE // (SSM_STATE // SSM_GROUP)
    return [(slice(rows_per_tile * j, rows_per_tile * (j + 1)), slice(DIAG_TILE * j, DIAG_TILE * (j + 1)))
            for j in range(gn // DIAG_TILE)]


def _ssm_scan_fwd(ub, lam_re, lam_im, pow_re, pow_im, bb_re, bb_im, cc_re, cc_im, exchange=None):
    t, w = ub.shape
    gn = lam_re.shape[-1]
    tc = ROW_TILE
    cl = min(gn, SCAN_LANES)
    nblk = t // tc
    tiles = _diag_tiles(gn)

    def body(*refs):
        first = jnp.logical_and(pl.program_id(0) == 0, pl.program_id(1) == 0)
        last = jnp.logical_and(pl.program_id(0) == 1, pl.program_id(1) == nblk - 1)
        (u_ref, lr_ref, li_ref, pr_ref, pi_ref, br_ref, bi_ref, cr_ref, ci_ref, y_ref, xr_ref, xi_ref,
         bur_scr, bui_scr, cr_scr, ci_scr, er_scr, ei_scr, nr_scr, ni_scr) = _riding_exchange(
             refs, exchange, 9, 3, first, last)

        @pl.when(pl.program_id(1) == 0)
        def _():
            cr_scr[...] = jnp.zeros_like(cr_scr)
            ci_scr[...] = jnp.zeros_like(ci_scr)

        for rows, lanes in tiles:
            u_j = u_ref[:, rows]
            bur_scr[:, lanes] = _dot(u_j, br_ref[0, rows, lanes])
            bui_scr[:, lanes] = _dot(u_j, bi_ref[0, rows, lanes])
        for descending in (False, True):
            @pl.when(pl.program_id(0) == int(descending))
            def _(descending=descending):
                for c0 in range(0, gn, cl):
                    _segmented_scan(bur_scr, bui_scr, xr_ref.at[0], xi_ref.at[0], lr_ref.at[0], li_ref.at[0],
                                    pr_ref.at[0], pi_ref.at[0], cr_scr, ci_scr, er_scr, ei_scr, nr_scr, ni_scr,
                                    pl.ds(c0, cl), descending, conj=False)
        for rows, lanes in tiles:
            y_ref[0, :, rows] = (_dot(xr_ref[0, :, lanes].astype(BF16), cr_ref[0, lanes, rows])
                                 - _dot(xi_ref[0, :, lanes].astype(BF16), ci_ref[0, lanes, rows]))

    blk = lambda dd, i: jnp.where(dd == 0, i, nblk - 1 - i)
    row = lambda width: pl.BlockSpec((1, tc, width), lambda dd, i: (dd, blk(dd, i), 0))
    per_dir = lambda a, b: pl.BlockSpec((1, a, b), lambda dd, i: (dd, 0, 0))
    extra = exchange is not None
    return _pcall(
        body, name="ssm_scan_fwd", grid=(2, nblk),
        in_specs=[pl.BlockSpec((tc, w), lambda dd, i: (blk(dd, i), 0)), per_dir(1, gn), per_dir(1, gn),
                  per_dir(tc // 8, gn), per_dir(tc // 8, gn),
                  per_dir(w, gn), per_dir(w, gn), per_dir(gn, w), per_dir(gn, w)] + [_ANY] * extra,
        out_specs=[row(w), row(gn), row(gn)] + [_ANY] * extra,
        out_shape=[jax.ShapeDtypeStruct((2, t, w), F32), jax.ShapeDtypeStruct((2, t, gn), F32),
                   jax.ShapeDtypeStruct((2, t, gn), F32)] + ([_exchange_out_shape(*exchange)] if extra else []),
        scratch=[pltpu.VMEM((tc, gn), F32), pltpu.VMEM((tc, gn), F32), pltpu.VMEM((1, gn), F32),
                 pltpu.VMEM((1, gn), F32)] + [pltpu.VMEM((8, gn), F32)] * 4 + _EXCHANGE_SEMS * extra,
        vmem=V7X_VMEM_LIMIT,
    )(ub, lam_re, lam_im, pow_re, pow_im, bb_re, bb_im, cc_re, cc_im, *([exchange[0]] if extra else []))


def _ssm_scan_bwd(dyb, ub, xs_re, xs_im, lam_re, lam_im, pow_re, pow_im, cct_re, cct_im, bbt_re, bbt_im,
                  exchange=None):
    t, w = dyb.shape
    gn = lam_re.shape[-1]
    tc = ROW_TILE
    cl = min(gn, SCAN_LANES)
    nblk = t // tc
    tiles = _diag_tiles(gn)

    def body(*refs):
        i = pl.program_id(1)
        first = jnp.logical_and(pl.program_id(0) == 0, i == 0)
        last = jnp.logical_and(pl.program_id(0) == 1, i == nblk - 1)
        (dy_ref, u_ref, xr_ref, xi_ref, hr_ref, hi_ref, lr_ref, li_ref, pr_ref, pi_ref, ctr_ref, cti_ref, btr_ref,
         bti_ref, du_ref, dlr_ref, dli_ref, dbr_ref, dbi_ref, dcr_ref, dci_ref,
         gxr_scr, gxi_scr, cr_scr, ci_scr, ar_scr, ai_scr, er_scr, ei_scr, nr_scr, ni_scr) = _riding_exchange(
             refs, exchange, 14, 7, first, last)

        @pl.when(i == 0)
        def _():
            for ref in (cr_scr, ci_scr, ar_scr, ai_scr, dbr_ref, dbi_ref, dcr_ref, dci_ref):
                ref[...] = jnp.zeros_like(ref)

        for rows, lanes in tiles:
            dy_j = dy_ref[:, rows]
            gxr_scr[:, lanes] = _dot(dy_j, ctr_ref[0, rows, lanes])
            gxi_scr[:, lanes] = -_dot(dy_j, cti_ref[0, rows, lanes])
        first_block = i == nblk - 1
        sublane = lax.broadcasted_iota(jnp.int32, (8, 1), 0)

        def lam_gradient(state_descending):
            for c0 in range(0, gn, 512):
                lanes = pl.ds(c0, 512)
                if state_descending:
                    cur, prev, edge, src = pl.ds(0, tc - 8), pl.ds(8, tc - 8), pl.ds(tc - 8, 8), pl.ds(0, 8)
                    halo_at, halo_row, shift = 7, 0, 7
                else:
                    cur, prev, edge, src = pl.ds(8, tc - 8), pl.ds(0, tc - 8), pl.ds(0, 8), pl.ds(tc - 8, 8)
                    halo_at, halo_row, shift = 0, 7, 1
                halo_r = jnp.where(first_block, 0.0, hr_ref[0, halo_row:halo_row + 1, lanes])
                halo_i = jnp.where(first_block, 0.0, hi_ref[0, halo_row:halo_row + 1, lanes])
                xer = jnp.where(sublane == halo_at, halo_r, pltpu.roll(xr_ref[0, src, lanes], shift, 0))
                xei = jnp.where(sublane == halo_at, halo_i, pltpu.roll(xi_ref[0, src, lanes], shift, 0))
                gr, gi = gxr_scr[cur, lanes], gxi_scr[cur, lanes]
                xpr, xpi = xr_ref[0, prev, lanes], xi_ref[0, prev, lanes]
                ger, gei = gxr_scr[edge, lanes], gxi_scr[edge, lanes]
                ar_scr[:, lanes] += (jnp.sum(gr * xpr + gi * xpi, axis=0, keepdims=True)
                                     + jnp.sum(ger * xer + gei * xei, axis=0, keepdims=True))
                ai_scr[:, lanes] += (jnp.sum(gi * xpr - gr * xpi, axis=0, keepdims=True)
                                     + jnp.sum(gei * xer - ger * xei, axis=0, keepdims=True))

        for descending in (True, False):
            @pl.when(pl.program_id(0) == int(not descending))
            def _(descending=descending):
                for c0 in range(0, gn, cl):
                    _segmented_scan(gxr_scr, gxi_scr, gxr_scr, gxi_scr, lr_ref.at[0], li_ref.at[0], pr_ref.at[0],
                                    pi_ref.at[0], cr_scr, ci_scr, er_scr, ei_scr, nr_scr, ni_scr, pl.ds(c0, cl),
                                    descending, conj=True)
                lam_gradient(state_descending=not descending)
        dlr_ref[0] = ar_scr[...]
        dli_ref[0] = ai_scr[...]
        for rows, lanes in tiles:
            grb = gxr_scr[:, lanes].astype(BF16)
            gib = gxi_scr[:, lanes].astype(BF16)
            du_ref[0, :, rows] = _dot(grb, btr_ref[0, lanes, rows]) + _dot(gib, bti_ref[0, lanes, rows])
            u_j = u_ref[:, rows]
            dy_j = dy_ref[:, rows]
            dbr_ref[0, rows, :] += _dot_tn(u_j, grb)
            dbi_ref[0, rows, :] += _dot_tn(u_j, gib)
            dcr_ref[0, rows, :] += _dot_tn(dy_j, xr_ref[0, :, lanes].astype(BF16))
            dci_ref[0, rows, :] -= _dot_tn(dy_j, xi_ref[0, :, lanes].astype(BF16))

    blk = lambda dd, i: jnp.where(dd == 0, nblk - 1 - i, i)
    rev = lambda width: pl.BlockSpec((1, tc, width), lambda dd, i: (dd, blk(dd, i), 0))
    halo_blk = lambda dd, i: jnp.where(dd == 0, jnp.maximum(blk(dd, i) * (tc // 8) - 1, 0),
                                       jnp.minimum((blk(dd, i) + 1) * (tc // 8), t // 8 - 1))
    halo = pl.BlockSpec((1, 8, gn), lambda dd, i: (dd, halo_blk(dd, i), 0))
    per_dir = lambda a, b: pl.BlockSpec((1, a, b), lambda dd, i: (dd, 0, 0))
    extra = exchange is not None
    return _pcall(
        body, name="ssm_scan_bwd", grid=(2, nblk),
        in_specs=[pl.BlockSpec((tc, w), lambda dd, i: (blk(dd, i), 0)),
                  pl.BlockSpec((tc, w), lambda dd, i: (blk(dd, i), 0)), rev(gn), rev(gn), halo, halo,
                  per_dir(1, gn), per_dir(1, gn), per_dir(tc // 8, gn), per_dir(tc // 8, gn),
                  per_dir(w, gn), per_dir(w, gn), per_dir(gn, w), per_dir(gn, w)]
        + [_ANY] * extra,
        out_specs=[rev(w), per_dir(1, gn), per_dir(1, gn)] + [per_dir(w, DIAG_TILE)] * 4 + [_ANY] * extra,
        out_shape=[jax.ShapeDtypeStruct((2, t, w), F32), jax.ShapeDtypeStruct((2, 1, gn), F32),
                   jax.ShapeDtypeStruct((2, 1, gn), F32)] + [jax.ShapeDtypeStruct((2, w, DIAG_TILE), F32)] * 4
        + ([_exchange_out_shape(*exchange)] if extra else []),
        scratch=[pltpu.VMEM((tc, gn), F32), pltpu.VMEM((tc, gn), F32)] + [pltpu.VMEM((1, gn), F32)] * 4
        + [pltpu.VMEM((8, gn), F32)] * 4 + _EXCHANGE_SEMS * extra,
        vmem=V7X_VMEM_LIMIT,
    )(dyb, ub, xs_re, xs_im, xs_re, xs_im, lam_re, lam_im, pow_re, pow_im, cct_re, cct_im, bbt_re, bbt_im,
      *([exchange[0]] if extra else []))


def _matmul_tn(a, b, name, a_is_transposed=False, exchange=None):
    t, n = b.shape
    m = a.shape[0] if a_is_transposed else a.shape[1]
    bm, bn, tk = min(m, 1024), min(n, 1024), KV_TILE
    grid = (m // bm, n // bn, t // tk)

    def body(*refs):
        at = lambda step: functools.reduce(jnp.logical_and, [pl.program_id(ax) == step[ax] for ax in range(3)])
        a_ref, b_ref, o_ref = _riding_exchange(refs, exchange, 2, 1, at((0, 0, 0)), at([g - 1 for g in grid]))

        @pl.when(pl.program_id(2) == 0)
        def _():
            o_ref[...] = jnp.zeros_like(o_ref)

        mul = _dot if a_is_transposed else _dot_tn
        o_ref[...] += mul(a_ref[...].astype(BF16), b_ref[...].astype(BF16))

    a_spec = (pl.BlockSpec((bm, tk), lambda i, j, k: (i, k)) if a_is_transposed else
              pl.BlockSpec((tk, bm), lambda i, j, k: (k, i)))
    extra = exchange is not None
    out = _pcall(
        body, name=name, grid=grid,
        in_specs=[a_spec, pl.BlockSpec((tk, bn), lambda i, j, k: (k, j))] + [_ANY] * extra,
        out_specs=[pl.BlockSpec((bm, bn), lambda i, j, k: (i, j))] + [_ANY] * extra,
        out_shape=[jax.ShapeDtypeStruct((m, n), F32)] + ([_exchange_out_shape(*exchange)] if extra else []),
        scratch=_EXCHANGE_SEMS * extra, vmem=V7X_VMEM_LIMIT,
    )(a, b, *([exchange[0]] if extra else []))
    return out if extra else out[0]


def _reduce_adamw(gparts, p, m, v, name, exchange=None):
    rows, width = p.shape
    tr = max(k for k in range(16, 513, 16) if rows % k == 0)

    def body(*refs):
        i = pl.program_id(0)
        g_ref, p_ref, m_ref, v_ref, go_ref, d_ref, mo_ref, vo_ref = _riding_exchange(
            refs, exchange, 4, 4, i == 0, i == rows // tr - 1)
        g = g_ref[0].astype(F32)
        for k in range(1, N_DEV):
            g = g + g_ref[k].astype(F32)
        go_ref[...] = g
        mm = ADAM_B1 * m_ref[...] + (1.0 - ADAM_B1) * g
        vv = ADAM_B2 * v_ref[...] + (1.0 - ADAM_B2) * (g * g)
        m_hat = mm / (1.0 - ADAM_B1 ** ADAM_STEP)
        v_hat = vv / (1.0 - ADAM_B2 ** ADAM_STEP)
        d_ref[...] = -ADAM_LR * (m_hat / (jnp.sqrt(v_hat) + ADAM_EPS) + ADAM_WD * p_ref[...])
        mo_ref[...] = mm
        vo_ref[...] = vv

    spec = pl.BlockSpec((tr, width), lambda i: (i, 0))
    out = jax.ShapeDtypeStruct((rows, width), F32)
    extra = exchange is not None
    return _pcall(
        body, name=name, grid=(rows // tr,),
        in_specs=[pl.BlockSpec((N_DEV, tr, width), lambda i: (0, i, 0)), spec, spec, spec] + [_ANY] * extra,
        out_specs=[spec, spec, spec, spec] + [_ANY] * extra,
        out_shape=[out, out, out, out] + ([_exchange_out_shape(*exchange)] if extra else []),
        scratch=_EXCHANGE_SEMS * extra, vmem=V7X_VMEM_LIMIT,
    )(gparts, p, m, v, *([exchange[0]] if extra else []))


def _peer(k):
    x, y, c = lax.axis_index("x"), lax.axis_index("y"), lax.axis_index("c")
    return (x ^ ((k >> 2) & 1), y ^ ((k >> 1) & 1), c ^ (k & 1))


def _my_index():
    return 4 * lax.axis_index("x") + 2 * lax.axis_index("y") + lax.axis_index("c")


def _exchange_copies(x_ref, out_ref, send_sems, recv_sems, local_sem, scatter, first_sem=0):
    me = _my_index()
    local = pltpu.make_async_copy(x_ref.at[me] if scatter else x_ref, out_ref.at[me], local_sem)
    copies = []
    for k in range(1, N_DEV):
        peer = _peer(k)
        src = x_ref.at[4 * peer[0] + 2 * peer[1] + peer[2]] if scatter else x_ref
        copies.append(pltpu.make_async_remote_copy(
            src_ref=src, dst_ref=out_ref.at[me], send_sem=send_sems.at[first_sem + k - 1],
            recv_sem=recv_sems.at[first_sem + k - 1], device_id=peer, device_id_type=pl.DeviceIdType.MESH))
    return local, copies


def _start_all(local, copies):
    local.start()
    for cp in copies:
        cp.start()


def _wait_all(local, copies):
    for cp in copies:
        cp.wait_recv()
    for cp in copies:
        cp.wait_send()
    local.wait()


def _exchange_out_shape(x, scatter):
    return jax.ShapeDtypeStruct((N_DEV,) + tuple(x.shape[1:] if scatter else x.shape), x.dtype)


_EXCHANGE_SEMS = [pltpu.SemaphoreType.DMA((N_DEV - 1,)), pltpu.SemaphoreType.DMA((N_DEV - 1,)),
                  pltpu.SemaphoreType.DMA(())]


def _exchange(ops, name):
    n = len(ops)

    def body(*refs):
        x_refs, out_refs = refs[:n], refs[n:2 * n]
        send_sems, recv_sems, local_sems = refs[2 * n:]
        started = []
        for q, (_, scatter) in enumerate(ops):
            local, copies = _exchange_copies(x_refs[q], out_refs[q], send_sems, recv_sems, local_sems.at[q],
                                             scatter, first_sem=q * (N_DEV - 1))
            _start_all(local, copies)
            started.append((local, copies))
        for local, copies in started:
            _wait_all(local, copies)

    return pl.pallas_call(
        body, name=name, in_specs=[_ANY] * n, out_specs=[_ANY] * n,
        out_shape=[_exchange_out_shape(x, scatter) for x, scatter in ops],
        scratch_shapes=[pltpu.SemaphoreType.DMA((n * (N_DEV - 1),)), pltpu.SemaphoreType.DMA((n * (N_DEV - 1),)),
                        pltpu.SemaphoreType.DMA((n,))],
    )(*[x for x, _ in ops])


def _to_shards(full, axis):
    r, c = full.shape
    if axis == 0:
        return full.reshape(N_DEV, r // N_DEV, c)
    return full.reshape(r, N_DEV, c // N_DEV).transpose(1, 0, 2)


def _from_shards(shards, axis):
    _, r, c = shards.shape
    if axis == 0:
        return shards.reshape(N_DEV * r, c)
    return shards.transpose(1, 0, 2).reshape(r, N_DEV * c)


def _pack_rows(parts, lead):
    flat = []
    for p in parts:
        p = p.reshape(p.shape[:lead] + (-1, PACK_W))
        pad = _round_up(p.shape[lead], 16) - p.shape[lead]
        flat.append(jnp.pad(p, [(0, 0)] * lead + [(0, pad), (0, 0)]) if pad else p)
    return jnp.concatenate(flat, axis=lead)


def _unpack_rows(packed, shapes):
    lead = packed.shape[:-2]
    out, off = [], 0
    for shp in shapes:
        rows = math.prod(shp) // PACK_W
        out.append(packed[..., off:off + rows, :].reshape(lead + tuple(shp)))
        off += _round_up(rows, 16)
    return out


def _pack_flat(parts):
    flat = jnp.concatenate([p.reshape(-1) for p in parts])
    n = flat.shape[0]
    flat = jnp.pad(flat, (0, _round_up(n, 16 * PACK_W) - n))
    return flat.reshape(-1, PACK_W)


def _unpack(packed, shapes):
    flat = packed.reshape(-1)
    out, off = [], 0
    for shp in shapes:
        n = math.prod(shp)
        out.append(flat[off:off + n].reshape(shp))
        off += n
    return out


def _ssm_discretize(a_re, a_im, log_dt, b_re, b_im):
    dt = jnp.exp(log_dt)[..., None]
    lam_re = jnp.minimum(a_re, EIG_RE_MAX)
    lam_im = a_im
    mag = jnp.exp(lam_re * dt)
    ang = lam_im * dt
    lb_re = mag * jnp.cos(ang)
    lb_im = mag * jnp.sin(ang)
    num_re = lb_re - 1.0
    num_im = lb_im
    den = lam_re * lam_re + lam_im * lam_im
    f_re = (num_re * lam_re + num_im * lam_im) / den
    f_im = (num_im * lam_re - num_re * lam_im) / den
    bb_re = f_re[..., None] * b_re - f_im[..., None] * b_im
    bb_im = f_re[..., None] * b_im + f_im[..., None] * b_re
    return lb_re, lb_im, bb_re, bb_im


def _ssm_powers(a_re, a_im, log_dt, count):
    dt = jnp.exp(log_dt)[:, None, :, None]
    k = jnp.arange(1, count + 1, dtype=F32)[None, :, None, None]
    mag = jnp.exp(k * (jnp.minimum(a_re, EIG_RE_MAX)[:, None] * dt))
    ang = k * (a_im[:, None] * dt)
    shape = (a_re.shape[0], count, -1)
    return (mag * jnp.cos(ang)).reshape(shape), (mag * jnp.sin(ang)).reshape(shape)


def _interleave(a, inverse=False):
    lead, (t, width) = a.shape[:-2], a.shape[-2:]
    seg = ROW_TILE // 8
    shape = lead + (t // ROW_TILE,) + ((seg, 8) if inverse else (8, seg)) + (width,)
    return jnp.swapaxes(a.reshape(shape), -3, -2).reshape(a.shape)


def _block_diag(blocks):
    two, g, a, b = blocks.shape
    tiled = jnp.tile(blocks.reshape(two, g * a, b), (1, 1, g))
    row_group = lax.broadcasted_iota(jnp.int32, (g * a, g * b), 0) // a
    col_group = lax.broadcasted_iota(jnp.int32, (g * a, g * b), 1) // b
    return jnp.where(row_group == col_group, tiled, 0.0).astype(BF16)


def _diag_blocks(tiles):
    two, w, _ = tiles.shape
    per = DIAG_TILE // SSM_STATE
    t6 = tiles.reshape(two, w // (per * SSM_GROUP), per, SSM_GROUP, per, SSM_STATE)
    return jnp.einsum("zjqpqn->zjqpn", t6).reshape(two, w // SSM_GROUP, SSM_GROUP, SSM_STATE)


def _rope_tables(t, n_valid):
    pos = jnp.arange(t)
    real = jnp.logical_and(pos >= N_META, pos < n_valid)
    idx = jnp.where(real, pos - N_META, 0)
    row_id = (idx // GRID_W).astype(F32)
    col_id = (idx % GRID_W).astype(F32)
    pairs_per_axis = HEAD_DIM // 4
    inv_freq = ROPE_THETA ** (-jnp.arange(pairs_per_axis, dtype=F32) / pairs_per_axis)
    ang = jnp.concatenate([row_id[:, None] * inv_freq, col_id[:, None] * inv_freq], axis=-1)
    ang = jnp.where(real[:, None], ang, 0.0)
    cos = jnp.repeat(jnp.cos(ang), 2, axis=-1)
    sin = jnp.sin(ang)
    sin = jnp.stack([-sin, sin], axis=-1).reshape(t, HEAD_DIM)
    return jnp.tile(cos, (1, 2)), jnp.tile(sin, (1, 2))


def _local_step(x, loss_target, big, small, comm=None):
    s_len, d = x.shape
    n_valid = s_len + N_META
    t = _round_up(n_valid, KV_TILE)
    du = d // 2
    groups = du // SSM_GROUP
    nh = d // HEAD_DIM
    nkv = nh // KV_REP
    pad = t - n_valid

    x0 = jnp.concatenate([big["meta_tokens"].astype(F32), x, jnp.zeros((pad, d), F32)], axis=0)
    tgt = jnp.concatenate([jnp.zeros((N_META, d), F32), loss_target, jnp.zeros((pad, d), F32)], axis=0)
    cos, sin = _rope_tables(t, n_valid)
    g_mix = small["norm_mix_g"].reshape(1, d)
    g_mlp = small["norm_mlp_g"].reshape(1, d)
    g_fin = small["norm_final_g"].reshape(1, d)
    qg = jnp.tile(small["q_norm_g"].reshape(1, HEAD_DIM), (1, 2))
    kg = jnp.tile(small["k_norm_g"].reshape(1, HEAD_DIM), (1, 2))
    ssm_d = small["ssm_d"].reshape(1, du)
    b_glu = small["b_glu"].reshape(1, du)

    ssm_in = tuple(small[n][0] for n in ("ssm_a_re", "ssm_a_im", "ssm_log_dt", "ssm_b_re", "ssm_b_im"))
    (lb_re, lb_im, bbar_re, bbar_im), disc_vjp = jax.vjp(_ssm_discretize, *ssm_in)
    lam_re = lb_re.reshape(2, 1, groups * SSM_STATE)
    lam_im = lb_im.reshape(2, 1, groups * SSM_STATE)
    pow_re, pow_im = _ssm_powers(*ssm_in[0:3], ROW_TILE // 8)
    bb_re = _block_diag(bbar_re.transpose(0, 1, 3, 2))
    bb_im = _block_diag(bbar_im.transpose(0, 1, 3, 2))
    c_re, c_im = small["ssm_c_re"][0], small["ssm_c_im"][0]
    cct_re = _block_diag(c_re)
    cct_im = _block_diag(c_im)
    cc_re = cct_re.transpose(0, 2, 1)
    cc_im = cct_im.transpose(0, 2, 1)
    bbt_re = bb_re.transpose(0, 2, 1)
    bbt_im = bb_im.transpose(0, 2, 1)
    scan_w = (lam_re, lam_im, pow_re, pow_im)

    in_proj_args = (x0, g_mix, big["w_in"], qg, kg, cos, sin, n_valid)
    if comm is None:
        h, u, ub, qraw, kraw, qat, ka, kt, va, vta, gates = _in_proj_fwd(*in_proj_args)
    else:
        h, u, ub, qraw, kraw, qat, ka, kt, va, vta, gates, got = _in_proj_fwd(
            *in_proj_args, exchange=(comm["pack_weights"](MIXER_WEIGHTS), False))
        big = {**big, **comm["unpack_weights"](MIXER_WEIGHTS, got)}
    ub = _interleave(ub)
    if comm is None:
        y2, xs_re, xs_im = _ssm_scan_fwd(ub, *scan_w, bb_re, bb_im, cc_re, cc_im)
    else:
        y2, xs_re, xs_im, got = _ssm_scan_fwd(ub, *scan_w, bb_re, bb_im, cc_re, cc_im,
                                              exchange=(comm["pack_weights"](MLP_WEIGHTS), False))
        big = {**big, **comm["unpack_weights"](MLP_WEIGHTS, got)}
    y2 = _interleave(y2, inverse=True)
    yf, yb = y2[0], y2[1]
    yt_attn, lse = _attn_fwd(qat, ka, vta)
    mixer_w = (ssm_d, big["w_glu"], b_glu, big["w_ssm_proj"], big["w_attn_proj"], big["w_out"])
    x1 = _mixer_out_fwd(x0, u, yf, yb, yt_attn, gates, *mixer_w)

    dx1, loss8, dg_fin, dg_mlp, h2b, dab, hsqb, dx2b = _mlp_loss_fwd_bwd(
        x1, tgt, g_mlp, g_fin, big["w_mlp_in"], big["w_mlp_out"], n_valid)

    (dyb, dud, dyt_attn, dgates, zb, dglb, ysb, dasb, daab, mgb, dxb, d_ssm_d, d_b_glu) = _mixer_out_bwd(
        dx1, u, yf, yb, yt_attn, gates, *mixer_w)
    grads = {}
    grads["w_glu"] = _matmul_tn(zb, dglb, "grad_w_glu")
    grads["w_ssm_proj"] = _matmul_tn(ysb, dasb, "grad_w_ssm_proj")
    grads["w_attn_proj"] = _matmul_tn(yt_attn, daab, "grad_w_attn_proj", a_is_transposed=True)
    grads["w_out"] = _matmul_tn(mgb, dxb, "grad_w_out")
    grads["w_mlp_in"] = _matmul_tn(h2b, dab, "grad_w_mlp_in")
    grads["w_mlp_out"] = _matmul_tn(hsqb, dx2b, "grad_w_mlp_out")
    dk, dv, dqt = _attn_bwd(qat, ka, kt, va, dyt_attn, yt_attn, lse)
    scan_args = (_interleave(dyb), ub, xs_re, xs_im, *scan_w, cct_re, cct_im, bbt_re, bbt_im)
    if comm is None:
        late_grad_parts = None
        du2, dlam_re, dlam_im, dbr, dbi, dcr, dci = _ssm_scan_bwd(*scan_args)
    else:
        du2, dlam_re, dlam_im, dbr, dbi, dcr, dci, late_grad_parts = _ssm_scan_bwd(
            *scan_args, exchange=(comm["pack_grads"](LATE_WEIGHTS, grads), True))
    du2 = _interleave(du2, inverse=True)
    dx0, dproj, dg_mix, dqg, dkg = _in_proj_bwd(x0, dx1, dud, du2[0], du2[1], qraw, kraw, dqt, dk, dv, dgates,
                                                g_mix, big["w_in"], qg, kg, cos, sin)

    grads["meta_tokens"] = dx0[0:N_META]
    dbb_re = _diag_blocks(dbr).transpose(0, 1, 3, 2)
    dbb_im = _diag_blocks(dbi).transpose(0, 1, 3, 2)
    dc_re, dc_im = _diag_blocks(dcr), _diag_blocks(dci)
    shape_gn = (2, groups, SSM_STATE)
    d_a_re, d_a_im, d_log_dt, d_b_re, d_b_im = disc_vjp(
        (dlam_re.reshape(shape_gn), dlam_im.reshape(shape_gn), dbb_re, dbb_im))
    grads.update({
        "norm_mix_g": dg_mix, "ssm_a_re": d_a_re[None], "ssm_a_im": d_a_im[None], "ssm_log_dt": d_log_dt[None],
        "ssm_b_re": d_b_re[None], "ssm_b_im": d_b_im[None], "ssm_c_re": dc_re[None], "ssm_c_im": dc_im[None],
        "ssm_d": d_ssm_d, "b_glu": d_b_glu,
        "q_norm_g": dqg[:, 0:HEAD_DIM] + dqg[:, HEAD_DIM:128], "k_norm_g": dkg[:, 0:HEAD_DIM] + dkg[:, HEAD_DIM:128],
        "norm_mlp_g": dg_mlp, "norm_final_g": dg_fin.reshape(d),
    })
    if comm is None:
        small_grad_parts = None
        grads["w_in"] = _matmul_tn(h, dproj, "grad_w_in")
    else:
        grads["w_in"], small_grad_parts = _matmul_tn(h, dproj, "grad_w_in",
                                                     exchange=(comm["pack_small_grads"](grads), False))
    return loss8[0, 0], dx0[N_META:n_valid], grads, late_grad_parts, small_grad_parts


def kernel(x, meta_tokens, norm_mix_g, w_in, ssm_a_re, ssm_a_im, ssm_log_dt, ssm_b_re, ssm_b_im, ssm_c_re, ssm_c_im, ssm_d, w_glu, b_glu, q_norm_g, k_norm_g, w_ssm_proj, w_attn_proj, w_out, norm_mlp_g, w_mlp_in, w_mlp_out, norm_final_g, loss_target, m_meta_tokens, m_norm_mix_g, m_w_in, m_ssm_a_re, m_ssm_a_im, m_ssm_log_dt, m_ssm_b_re, m_ssm_b_im, m_ssm_c_re, m_ssm_c_im, m_ssm_d, m_w_glu, m_b_glu, m_q_norm_g, m_k_norm_g, m_w_ssm_proj, m_w_attn_proj, m_w_out, m_norm_mlp_g, m_w_mlp_in, m_w_mlp_out, m_norm_final_g, v_meta_tokens, v_norm_mix_g, v_w_in, v_ssm_a_re, v_ssm_a_im, v_ssm_log_dt, v_ssm_b_re, v_ssm_b_im, v_ssm_c_re, v_ssm_c_im, v_ssm_d, v_w_glu, v_b_glu, v_q_norm_g, v_k_norm_g, v_w_ssm_proj, v_w_attn_proj, v_w_out, v_norm_mlp_g, v_w_mlp_in, v_w_mlp_out, v_norm_final_g):
    w = dict(meta_tokens=meta_tokens, norm_mix_g=norm_mix_g, w_in=w_in, ssm_a_re=ssm_a_re, ssm_a_im=ssm_a_im, ssm_log_dt=ssm_log_dt, ssm_b_re=ssm_b_re, ssm_b_im=ssm_b_im, ssm_c_re=ssm_c_re, ssm_c_im=ssm_c_im, ssm_d=ssm_d, w_glu=w_glu, b_glu=b_glu, q_norm_g=q_norm_g, k_norm_g=k_norm_g, w_ssm_proj=w_ssm_proj, w_attn_proj=w_attn_proj, w_out=w_out, norm_mlp_g=norm_mlp_g, w_mlp_in=w_mlp_in, w_mlp_out=w_mlp_out, norm_final_g=norm_final_g)
    m = dict(meta_tokens=m_meta_tokens, norm_mix_g=m_norm_mix_g, w_in=m_w_in, ssm_a_re=m_ssm_a_re, ssm_a_im=m_ssm_a_im, ssm_log_dt=m_ssm_log_dt, ssm_b_re=m_ssm_b_re, ssm_b_im=m_ssm_b_im, ssm_c_re=m_ssm_c_re, ssm_c_im=m_ssm_c_im, ssm_d=m_ssm_d, w_glu=m_w_glu, b_glu=m_b_glu, q_norm_g=m_q_norm_g, k_norm_g=m_k_norm_g, w_ssm_proj=m_w_ssm_proj, w_attn_proj=m_w_attn_proj, w_out=m_w_out, norm_mlp_g=m_norm_mlp_g, w_mlp_in=m_w_mlp_in, w_mlp_out=m_w_mlp_out, norm_final_g=m_norm_final_g)
    v = dict(meta_tokens=v_meta_tokens, norm_mix_g=v_norm_mix_g, w_in=v_w_in, ssm_a_re=v_ssm_a_re, ssm_a_im=v_ssm_a_im, ssm_log_dt=v_ssm_log_dt, ssm_b_re=v_ssm_b_re, ssm_b_im=v_ssm_b_im, ssm_c_re=v_ssm_c_re, ssm_c_im=v_ssm_c_im, ssm_d=v_ssm_d, w_glu=v_w_glu, b_glu=v_b_glu, q_norm_g=v_q_norm_g, k_norm_g=v_k_norm_g, w_ssm_proj=v_w_ssm_proj, w_attn_proj=v_w_attn_proj, w_out=v_w_out, norm_mlp_g=v_norm_mlp_g, w_mlp_in=v_w_mlp_in, w_mlp_out=v_w_mlp_out, norm_final_g=v_norm_final_g)

    shard2d = {n: w[n].reshape(w[n].shape[-2:]) for n in BIG_WEIGHTS}
    big_shapes = [shard2d[n].shape for n in BIG_WEIGHTS]

    meta_hi = shard2d["meta_tokens"].astype(BF16)
    meta_res = shard2d["meta_tokens"] - meta_hi.astype(F32)
    meta_mid = meta_res.astype(BF16)
    meta_lo = (meta_res - meta_mid.astype(F32)).astype(BF16)
    shapes_of = lambda names: [shard2d[n].shape for n in names]

    def full_weights(names, shards):
        return {n: s if n in BLOCK_WEIGHTS else _from_shards(s, BIG_SHARD_AXIS[n]) for n, s in zip(names, shards)}

    early = _exchange([(_pack_rows([meta_hi, meta_mid, meta_lo, shard2d["w_in"].astype(BF16)], 0), False)],
                      "gather_early_weights")[0]
    shards = _unpack_rows(early, [meta_hi.shape] * 3 + shapes_of(EARLY_WEIGHTS[1:]))
    meta = [_from_shards(s, 1).astype(F32) for s in shards[0:3]]
    big = {"meta_tokens": (meta[0] + meta[1]) + meta[2], **full_weights(EARLY_WEIGHTS[1:], shards[3:])}
    small = {n: w[n] for n in SMALL_WEIGHTS}
    pack_grads = lambda names, grads: _pack_rows(
        [_to_shards(grads[n], BIG_SHARD_AXIS[n]) for n in names], 1).astype(BF16)
    comm = {
        "pack_weights": lambda names: _pack_rows([shard2d[n].astype(BF16) for n in names], 0),
        "unpack_weights": lambda names, g: full_weights(names, _unpack_rows(g, shapes_of(names))),
        "pack_grads": pack_grads,
        "pack_small_grads": lambda grads: _pack_flat([grads[n] for n in SMALL_WEIGHTS]),
    }

    loss, grad_x, grads, late_parts, small_parts = _local_step(x[0], loss_target[0], big, small, comm)
    loss = lax.psum(loss, ("x", "y", "c"))

    pk = lambda names, src: _pack_rows([src[n].reshape(shard2d[n].shape) for n in names], 0)
    pe, pl_ = functools.partial(pk, EARLY_WEIGHTS), functools.partial(pk, LATE_WEIGHTS)
    *late_out, early_parts = _reduce_adamw(late_parts, pl_(w), pl_(m), pl_(v), "adamw_sharded_late",
                                           exchange=(pack_grads(EARLY_WEIGHTS, grads), True))
    early_out = _reduce_adamw(early_parts, pe(w), pe(m), pe(v), "adamw_sharded_early")
    small_shapes = [w[n].shape for n in SMALL_WEIGHTS]
    pf = lambda src: _pack_flat([src[n] for n in SMALL_WEIGHTS])
    small_out = _reduce_adamw(small_parts, pf(w), pf(m), pf(v), "adamw_replicated")

    results = []
    for kind in range(4):
        big_un = dict(zip(EARLY_WEIGHTS + LATE_WEIGHTS,
                          _unpack_rows(early_out[kind], shapes_of(EARLY_WEIGHTS))
                          + _unpack_rows(late_out[kind], shapes_of(LATE_WEIGHTS))))
        small_un = dict(zip(SMALL_WEIGHTS, _unpack(small_out[kind], small_shapes)))
        for n in ALL_WEIGHTS:
            results.append(big_un[n].reshape(w[n].shape) if n in big_un else small_un[n])
    return (loss, grad_x[None], *results)
```

```python
import functools
import math

import jax
import jax.numpy as jnp
from jax import lax
from jax.experimental import pallas as pl
from jax.experimental.pallas import tpu as pltpu

F32 = jnp.float32
BF16 = jnp.bfloat16

N_DEV = 8
N_META = 16
GRID_W = 64
SSM_GROUP = 16
SSM_STATE = 64
HEAD_DIM = 64
KV_REP = 4
ROPE_THETA = 10000.0
NORM_EPS = 1e-6
EIG_RE_MAX = -1e-4
ATTN_SCALE = HEAD_DIM ** -0.5

ADAM_LR = 0.001
ADAM_B1 = 0.9
ADAM_B2 = 0.999
ADAM_EPS = 1e-08
ADAM_WD = 0.01
ADAM_STEP = 10

ROW_TILE = 384
ROW_TILE_BWD = 256
QUERY_STRIP = 256
ONE_PASS_SLACK = 60.0
VT_ROWS = 80
MASK_BIAS = -1e30
SCAN_LANES = 512
DIAG_TILE = 256
KV_TILE = 768
PACK_W = 1024
V7X_VMEM_LIMIT = 56 * 1024 * 1024
NEG_BIG = -1e30

BIG_WEIGHTS = ("meta_tokens", "w_in", "w_glu", "w_ssm_proj", "w_attn_proj", "w_out", "w_mlp_in", "w_mlp_out")
BIG_SHARD_AXIS = {"meta_tokens": 1, "w_in": 1, "w_glu": 0, "w_ssm_proj": 1, "w_attn_proj": 0, "w_out": 0,
                  "w_mlp_in": 1, "w_mlp_out": 0}
BLOCK_WEIGHTS = ("w_in", "w_mlp_in", "w_mlp_out")
EARLY_WEIGHTS = ("meta_tokens", "w_in")
MIXER_WEIGHTS = ("w_glu", "w_ssm_proj", "w_attn_proj", "w_out")
MLP_WEIGHTS = ("w_mlp_in", "w_mlp_out")
LATE_WEIGHTS = MIXER_WEIGHTS + MLP_WEIGHTS
SMALL_WEIGHTS = ("norm_mix_g", "ssm_a_re", "ssm_a_im", "ssm_log_dt", "ssm_b_re", "ssm_b_im", "ssm_c_re",
                 "ssm_c_im", "ssm_d", "b_glu", "q_norm_g", "k_norm_g", "norm_mlp_g", "norm_final_g")
ALL_WEIGHTS = ("meta_tokens", "norm_mix_g", "w_in", "ssm_a_re", "ssm_a_im", "ssm_log_dt", "ssm_b_re", "ssm_b_im",
               "ssm_c_re", "ssm_c_im", "ssm_d", "w_glu", "b_glu", "q_norm_g", "k_norm_g", "w_ssm_proj",
               "w_attn_proj", "w_out", "norm_mlp_g", "w_mlp_in", "w_mlp_out", "norm_final_g")


def _round_up(n, m):
    return (n + m - 1) // m * m


def _pcall(body, *, name, grid, in_specs, out_specs, out_shape, scratch=(), vmem=None, **kw):
    params = pltpu.CompilerParams(dimension_semantics=("arbitrary",) * len(grid), vmem_limit_bytes=vmem)
    return pl.pallas_call(body, name=name, grid=grid, in_specs=in_specs, out_specs=out_specs, out_shape=out_shape,
                          scratch_shapes=list(scratch), compiler_params=params, **kw)


def _dot(a, b):
    return jnp.dot(a, b, preferred_element_type=F32)


def _dot_nt(a, b):
    return lax.dot_general(a, b, (((1,), (1,)), ((), ())), preferred_element_type=F32)


def _dot_tn(a, b):
    return lax.dot_general(a, b, (((0,), (0,)), ((), ())), preferred_element_type=F32)


def _full_spec(shape):
    nd = len(shape)
    return pl.BlockSpec(shape, lambda *_: (0,) * nd)


def _row_spec(tm, width):
    return pl.BlockSpec((tm, width), lambda i: (i, 0))


def _heads_spec(nh, tm):
    return pl.BlockSpec((nh, tm, HEAD_DIM), lambda i: (0, i, 0))


_ANY = pl.BlockSpec(memory_space=pl.ANY)


def _load_once(step, pairs, sem):
    @pl.when(step == 0)
    def _():
        copies = [pltpu.make_async_copy(src, dst, sem.at[k]) for k, (src, dst) in enumerate(pairs)]
        for cp in copies:
            cp.start()
        for cp in copies:
            cp.wait()


def _swap_pairs(x, even):
    n = x.shape[-1]
    return jnp.where(even, pltpu.roll(x, n - 1, 1), pltpu.roll(x, 1, 1))


def _gelu(y):
    return 0.5 * y * (1.0 + lax.erf(y * (1.0 / math.sqrt(2.0))))


def _gelu_grad(y):
    return 0.5 * (1.0 + lax.erf(y * (1.0 / math.sqrt(2.0)))) + y * jnp.exp(-0.5 * y * y) * (1.0 / math.sqrt(2.0 * math.pi))


def _in_proj_fwd(x0, g_mix, w_in, qg, kg, cos, sin, n_valid, exchange=None):
    t, d = x0.shape
    tm = ROW_TILE
    du, dk = d // 2, d // 4
    nh, nkv = d // HEAD_DIM, d // HEAD_DIM // KV_REP
    bw = w_in.shape[-1]
    assert bw == du and dk * 2 == bw

    def body(*refs):
        i = pl.program_id(0)
        (x_ref, g_ref, w_hbm, qg_ref, kg_ref, c_ref, s_ref,
         h_ref, u_ref, ub_ref, qraw_ref, kraw_ref, qat_ref, ka_ref, kt_ref, va_ref, vta_ref, gates_ref,
         w_ref, sem) = _riding_exchange(refs, exchange, 7, 11, i == 0, i == t // tm - 1)
        _load_once(i, [(w_hbm, w_ref)], sem)
        x = x_ref[...]
        r = lax.rsqrt(jnp.mean(x * x, axis=-1, keepdims=True) + NORM_EPS)
        h = ((x * r) * g_ref[...]).astype(BF16)
        h_ref[...] = h
        u = _dot(h, w_ref[0])
        u_ref[...] = u
        ub_ref[...] = u.astype(BF16)
        lane = lax.broadcasted_iota(jnp.int32, (tm, 128), 1)
        lo = lane < HEAD_DIM
        even = (lane & 1) == 0
        aug = lane == HEAD_DIM
        c = c_ref[...]
        s = s_ref[...]
        row = i * tm + lax.broadcasted_iota(jnp.int32, (tm, 1), 0)
        one = jnp.where(aug, 1.0, 0.0)
        key_bias = jnp.where(jnp.logical_and(aug, row >= n_valid), MASK_BIAS, 0.0)

        def norm_rope(blk, g128):
            sq = blk * blk
            ms_lo = jnp.sum(jnp.where(lo, sq, 0.0), axis=-1, keepdims=True) * (1.0 / HEAD_DIM)
            ms_hi = jnp.sum(jnp.where(lo, 0.0, sq), axis=-1, keepdims=True) * (1.0 / HEAD_DIM)
            rr = jnp.where(lo, lax.rsqrt(ms_lo + NORM_EPS), lax.rsqrt(ms_hi + NORM_EPS))
            qn = (blk * rr) * g128
            return qn * c + _swap_pairs(qn, even) * s

        def put_heads(rows_ref, cols_ref, first, pair, extra):
            for k, head in enumerate((pair, pltpu.roll(pair, HEAD_DIM, 1))):
                wide = jnp.where(lo, head, extra)
                if rows_ref is not None:
                    rows_ref[first + k] = wide.astype(BF16)
                cols_ref[first + k] = wide.T[0:cols_ref.shape[1], :].astype(BF16)

        for blk in range(2):
            qb = _dot(h, w_ref[1 + blk])
            qraw_ref[:, bw * blk:bw * (blk + 1)] = qb
            for a in range(bw // 128):
                put_heads(None, qat_ref, (bw // HEAD_DIM) * blk + 2 * a,
                          norm_rope(qb[:, 128 * a:128 * (a + 1)], qg_ref[...]) * ATTN_SCALE, one)
        kv = _dot(h, w_ref[3])
        kraw_ref[...] = kv[:, 0:dk]
        for a in range(nkv // 2):
            put_heads(ka_ref, kt_ref, 2 * a, norm_rope(kv[:, 128 * a:128 * (a + 1)], kg_ref[...]), key_bias)
            put_heads(va_ref, vta_ref, 2 * a, kv[:, dk + 128 * a:dk + 128 * (a + 1)], one)
        for blk in range(4):
            gates_ref[:, bw * blk:bw * (blk + 1)] = _dot(h, w_ref[4 + blk])

    heads = lambda n: pl.BlockSpec((n, tm, 128), lambda i: (0, i, 0))
    heads_t = lambda n, rows: pl.BlockSpec((n, rows, tm), lambda i: (0, 0, i))
    extra = exchange is not None
    return _pcall(
        body, name="in_proj_fwd", grid=(t // tm,),
        in_specs=[_row_spec(tm, d), _full_spec((1, d)), _ANY, _full_spec((1, 128)), _full_spec((1, 128)),
                  _row_spec(tm, 128), _row_spec(tm, 128)] + [_ANY] * extra,
        out_specs=[_row_spec(tm, d), _row_spec(tm, du), _row_spec(tm, du), _row_spec(tm, d), _row_spec(tm, dk),
                   heads_t(nh, 128), heads(nkv), heads_t(nkv, HEAD_DIM), heads(nkv), heads_t(nkv, VT_ROWS),
                   _row_spec(tm, 2 * d)] + [_ANY] * extra,
        out_shape=[jax.ShapeDtypeStruct((t, d), BF16), jax.ShapeDtypeStruct((t, du), F32),
                   jax.ShapeDtypeStruct((t, du), BF16), jax.ShapeDtypeStruct((t, d), F32),
                   jax.ShapeDtypeStruct((t, dk), F32), jax.ShapeDtypeStruct((nh, 128, t), BF16),
                   jax.ShapeDtypeStruct((nkv, t, 128), BF16), jax.ShapeDtypeStruct((nkv, HEAD_DIM, t), BF16),
                   jax.ShapeDtypeStruct((nkv, t, 128), BF16), jax.ShapeDtypeStruct((nkv, VT_ROWS, t), BF16),
                   jax.ShapeDtypeStruct((t, 2 * d), F32)] + ([_exchange_out_shape(*exchange)] if extra else []),
        scratch=[pltpu.VMEM((N_DEV, d, bw), BF16), pltpu.SemaphoreType.DMA((1,))] + _EXCHANGE_SEMS * extra,
        vmem=V7X_VMEM_LIMIT,
    )(x0, g_mix, w_in, qg, kg, cos, sin, *([exchange[0]] if extra else []))


def _mixer_values(u, yf, yb, yt_attn, gates, d_ref, wg_ref, bg_ref, ps_ref, pa_ref, d):
    y = (u * d_ref[...] + yf) + yb
    z = _gelu(y)
    sg = jax.nn.sigmoid(_dot(z.astype(BF16), wg_ref[...]) + bg_ref[...])
    y_ssm = z * sg
    a_ssm = _dot(y_ssm.astype(BF16), ps_ref[...])
    a_attn = _dot_tn(yt_attn.astype(BF16), pa_ref[...])
    s_ssm = jax.nn.sigmoid(gates[:, 0:d])
    s_attn = jax.nn.sigmoid(gates[:, d:2 * d])
    merged = s_ssm * a_ssm + s_attn * a_attn
    return y, z, sg, y_ssm, a_ssm, a_attn, s_ssm, s_attn, merged


def _mixer_out_fwd(x0, u, yf, yb, y_attn, gates, ssm_d, w_glu, b_glu, p_ssm, p_attn, w_out):
    t, d = x0.shape
    tm = ROW_TILE
    du = d // 2

    def body(x_ref, u_ref, yf_ref, yb_ref, ya_ref, gt_ref, d_ref, wg_ref, bg_ref, ps_ref, pa_ref, wo_ref, x1_ref):
        vals = _mixer_values(u_ref[...], yf_ref[...], yb_ref[...], ya_ref[...], gt_ref[...],
                             d_ref, wg_ref, bg_ref, ps_ref, pa_ref, d)
        merged = vals[-1]
        x1_ref[...] = x_ref[...] + _dot(merged.astype(BF16), wo_ref[...])

    return _pcall(
        body, name="mixer_out_fwd", grid=(t // tm,),
        in_specs=[_row_spec(tm, d), _row_spec(tm, du), _row_spec(tm, du), _row_spec(tm, du),
                  pl.BlockSpec((d, tm), lambda i: (0, i)), _row_spec(tm, 2 * d), _full_spec((1, du)), _full_spec((du, du)), _full_spec((1, du)),
                  _full_spec((du, d)), _full_spec((d, d)), _full_spec((d, d))],
        out_specs=_row_spec(tm, d), out_shape=jax.ShapeDtypeStruct((t, d), F32), vmem=V7X_VMEM_LIMIT,
    )(x0, u, yf, yb, y_attn, gates, ssm_d, w_glu, b_glu, p_ssm, p_attn, w_out)


def _mlp_loss_fwd_bwd(x1, target, g_mlp, g_fin, w1, w2, n_valid):
    t, d = x1.shape
    tm = ROW_TILE_BWD
    dff = 4 * d
    nfc, _, fc = w1.shape

    def body(x_ref, tg_ref, gm_ref, gf_ref, w1_hbm, w2_hbm,
             dx1_ref, loss_ref, dgf_ref, dgm_ref, h2_ref, da_ref, hsq_ref, dx2b_ref,
             w1_ref, w2_ref, relu_ref, sem):
        i = pl.program_id(0)
        _load_once(i, [(w1_hbm, w1_ref), (w2_hbm, w2_ref)], sem)

        @pl.when(i == 0)
        def _():
            loss_ref[...] = jnp.zeros_like(loss_ref)
            dgf_ref[...] = jnp.zeros_like(dgf_ref)
            dgm_ref[...] = jnp.zeros_like(dgm_ref)

        x1v = x_ref[...]
        r1 = lax.rsqrt(jnp.mean(x1v * x1v, axis=-1, keepdims=True) + NORM_EPS)
        xh1 = x1v * r1
        h2b = (xh1 * gm_ref[...]).astype(BF16)
        h2_ref[...] = h2b
        acc = jnp.zeros((tm, d), F32)
        for c in range(nfc):
            a = jnp.maximum(_dot(h2b, w1_ref[c]), 0.0)
            relu_ref[:, fc * c:fc * (c + 1)] = a
            hs = (a * a).astype(BF16)
            hsq_ref[:, fc * c:fc * (c + 1)] = hs
            acc = acc + _dot(hs, w2_ref[c])
        x2 = x1v + acc
        r2 = lax.rsqrt(jnp.mean(x2 * x2, axis=-1, keepdims=True) + NORM_EPS)
        xh2 = x2 * r2
        out = xh2 * gf_ref[...]
        row = i * tm + lax.broadcasted_iota(jnp.int32, (tm, 1), 0)
        valid = jnp.logical_and(row >= N_META, row < n_valid)
        diff = jnp.where(valid, out - tg_ref[...], 0.0)
        loss_ref[...] += 0.5 * jnp.sum(jnp.sum(diff * diff, axis=-1, keepdims=True) * (1.0 / d))
        dout = diff * (1.0 / d)
        dgf_ref[...] += jnp.sum(dout * xh2, axis=0, keepdims=True)
        dxh2 = dout * gf_ref[...]
        dx2 = r2 * (dxh2 - xh2 * jnp.mean(dxh2 * xh2, axis=-1, keepdims=True))
        dx2b = dx2.astype(BF16)
        dx2b_ref[...] = dx2b
        dh2 = jnp.zeros((tm, d), F32)
        for c in range(nfc):
            dhs = _dot_nt(dx2b, w2_ref[c])
            da = (dhs * (2.0 * relu_ref[:, fc * c:fc * (c + 1)])).astype(BF16)
            da_ref[:, fc * c:fc * (c + 1)] = da
            dh2 = dh2 + _dot_nt(da, w1_ref[c])
        dgm_ref[...] += jnp.sum(dh2 * xh1, axis=0, keepdims=True)
        dxh1 = dh2 * gm_ref[...]
        dx1_ref[...] = dx2 + r1 * (dxh1 - xh1 * jnp.mean(dxh1 * xh1, axis=-1, keepdims=True))

    return _pcall(
        body, name="mlp_loss_fwd_bwd", grid=(t // tm,),
        in_specs=[_row_spec(tm, d), _row_spec(tm, d), _full_spec((1, d)), _full_spec((1, d)), _ANY, _ANY],
        out_specs=[_row_spec(tm, d), _full_spec((8, 128)), _full_spec((1, d)), _full_spec((1, d)),
                   _row_spec(tm, d), _row_spec(tm, dff), _row_spec(tm, dff), _row_spec(tm, d)],
        out_shape=[jax.ShapeDtypeStruct((t, d), F32), jax.ShapeDtypeStruct((8, 128), F32),
                   jax.ShapeDtypeStruct((1, d), F32), jax.ShapeDtypeStruct((1, d), F32),
                   jax.ShapeDtypeStruct((t, d), BF16), jax.ShapeDtypeStruct((t, dff), BF16),
                   jax.ShapeDtypeStruct((t, dff), BF16), jax.ShapeDtypeStruct((t, d), BF16)],
        scratch=[pltpu.VMEM((nfc, d, fc), BF16), pltpu.VMEM((nfc, fc, d), BF16), pltpu.VMEM((tm, dff), F32),
                 pltpu.SemaphoreType.DMA((2,))],
        vmem=V7X_VMEM_LIMIT,
    )(x1, target, g_mlp, g_fin, w1, w2)


def _mixer_out_bwd(dx1, u, yf, yb, yt_attn, gates, ssm_d, w_glu, b_glu, p_ssm, p_attn, w_out):
    t, d = dx1.shape
    tm = ROW_TILE_BWD
    du = d // 2

    def body(dx_ref, u_ref, yf_ref, yb_ref, yt_ref, gt_ref, d_ref, wg_ref, bg_ref, ps_ref, pa_ref, wo_ref,
             dyb_ref, dud_ref, dyat_ref, dgates_ref, zb_ref, dglb_ref, ysb_ref, dasb_ref, daab_ref,
             mgb_ref, dxb_ref, dd_ref, dbg_ref):
        i = pl.program_id(0)

        @pl.when(i == 0)
        def _():
            dd_ref[...] = jnp.zeros_like(dd_ref)
            dbg_ref[...] = jnp.zeros_like(dbg_ref)

        uv = u_ref[...]
        y, z, sg, y_ssm, a_ssm, a_attn, s_ssm, s_attn, merged = _mixer_values(
            uv, yf_ref[...], yb_ref[...], yt_ref[...], gt_ref[...], d_ref, wg_ref, bg_ref, ps_ref, pa_ref, d)
        dxb = dx_ref[...].astype(BF16)
        dxb_ref[...] = dxb
        mgb_ref[...] = merged.astype(BF16)
        dmerged = _dot_nt(dxb, wo_ref[...])
        dgates_ref[:, 0:d] = (dmerged * a_ssm * (s_ssm * (1.0 - s_ssm))).astype(BF16)
        dgates_ref[:, d:2 * d] = (dmerged * a_attn * (s_attn * (1.0 - s_attn))).astype(BF16)
        da_ssm = (dmerged * s_ssm).astype(BF16)
        da_attn = (dmerged * s_attn).astype(BF16)
        dasb_ref[...] = da_ssm
        daab_ref[...] = da_attn
        ysb_ref[...] = y_ssm.astype(BF16)
        dy_ssm = _dot_nt(da_ssm, ps_ref[...])
        dyat_ref[...] = _dot_nt(pa_ref[...], da_attn).astype(BF16)
        dgl = dy_ssm * z * (sg * (1.0 - sg))
        dglb = dgl.astype(BF16)
        dglb_ref[...] = dglb
        zb_ref[...] = z.astype(BF16)
        dbg_ref[...] += jnp.sum(dgl, axis=0, keepdims=True)
        dz = dy_ssm * sg + _dot_nt(dglb, wg_ref[...])
        dy = dz * _gelu_grad(y)
        dyb_ref[...] = dy.astype(BF16)
        dd_ref[...] += jnp.sum(dy * uv, axis=0, keepdims=True)
        dud_ref[...] = dy * d_ref[...]

    bf = lambda w: jax.ShapeDtypeStruct((t, w), BF16)
    return _pcall(
        body, name="mixer_out_bwd", grid=(t // tm,),
        in_specs=[_row_spec(tm, d), _row_spec(tm, du), _row_spec(tm, du), _row_spec(tm, du),
                  pl.BlockSpec((d, tm), lambda i: (0, i)),
                  _row_spec(tm, 2 * d), _full_spec((1, du)), _full_spec((du, du)), _full_spec((1, du)),
                  _full_spec((du, d)), _full_spec((d, d)), _full_spec((d, d))],
        out_specs=[_row_spec(tm, du), _row_spec(tm, du), pl.BlockSpec((d, tm), lambda i: (0, i)),
                   _row_spec(tm, 2 * d), _row_spec(tm, du), _row_spec(tm, du), _row_spec(tm, du), _row_spec(tm, d),
                   _row_spec(tm, d), _row_spec(tm, d), _row_spec(tm, d), _full_spec((1, du)), _full_spec((1, du))],
        out_shape=[bf(du), jax.ShapeDtypeStruct((t, du), F32), jax.ShapeDtypeStruct((d, t), BF16), bf(2 * d),
                   bf(du), bf(du), bf(du), bf(d), bf(d), bf(d), bf(d),
                   jax.ShapeDtypeStruct((1, du), F32), jax.ShapeDtypeStruct((1, du), F32)],
        vmem=V7X_VMEM_LIMIT,
    )(dx1, u, yf, yb, yt_attn, gates, ssm_d, w_glu, b_glu, p_ssm, p_attn, w_out)


def _in_proj_bwd(x0, dx1, dud, duf, dub, qraw, kraw, dq, dk, dv, dgates, g_mix, w_in, qg, kg, cos, sin):
    t, d = x0.shape
    tm = ROW_TILE_BWD
    du, dkw = d // 2, d // 4
    nh, nkv = d // HEAD_DIM, d // HEAD_DIM // KV_REP
    bw = w_in.shape[-1]
    o_q, o_k, o_v, o_g = du, du + d, du + d + dkw, 2 * d

    def body(x_ref, dx1_ref, dud_ref, duf_ref, dub_ref, qraw_ref, kraw_ref, dq_ref, dk_ref, dv_ref, dgt_ref,
             g_ref, w_hbm, qg_ref, kg_ref, c_ref, s_ref,
             dx0_ref, dproj_ref, dgm_ref, dqg_ref, dkg_ref,
             w_ref, kv_ref, sem):
        i = pl.program_id(0)
        _load_once(i, [(w_hbm, w_ref)], sem)

        @pl.when(i == 0)
        def _():
            dgm_ref[...] = jnp.zeros_like(dgm_ref)
            dqg_ref[...] = jnp.zeros_like(dqg_ref)
            dkg_ref[...] = jnp.zeros_like(dkg_ref)

        lane = lax.broadcasted_iota(jnp.int32, (tm, 128), 1)
        lo = lane < HEAD_DIM
        even = (lane & 1) == 0
        c = c_ref[...]
        s = s_ref[...]

        def norm_rope_bwd(dout, raw, g128):
            sq = raw * raw
            ms_lo = jnp.sum(jnp.where(lo, sq, 0.0), axis=-1, keepdims=True) * (1.0 / HEAD_DIM)
            ms_hi = jnp.sum(jnp.where(lo, 0.0, sq), axis=-1, keepdims=True) * (1.0 / HEAD_DIM)
            rr = jnp.where(lo, lax.rsqrt(ms_lo + NORM_EPS), lax.rsqrt(ms_hi + NORM_EPS))
            xh = raw * rr
            dqn = dout * c + _swap_pairs(dout * s, even)
            dg = jnp.sum(dqn * xh, axis=0, keepdims=True)
            tt = dqn * g128
            pr = tt * xh
            mu_lo = jnp.sum(jnp.where(lo, pr, 0.0), axis=-1, keepdims=True) * (1.0 / HEAD_DIM)
            mu_hi = jnp.sum(jnp.where(lo, 0.0, pr), axis=-1, keepdims=True) * (1.0 / HEAD_DIM)
            return rr * (tt - xh * jnp.where(lo, mu_lo, mu_hi)), dg

        dub_tot = (dud_ref[...] + duf_ref[...]) + dub_ref[...]
        dproj_ref[:, 0:du] = dub_tot.astype(BF16)
        dqg = jnp.zeros((1, 128), F32)
        for a in range(nh // 2):
            sl = slice(128 * a, 128 * (a + 1))
            draw, dg = norm_rope_bwd(dq_ref[sl, :].T * ATTN_SCALE, qraw_ref[:, sl], qg_ref[...])
            dqg = dqg + dg
            dproj_ref[:, o_q + 128 * a:o_q + 128 * (a + 1)] = draw.astype(BF16)
        dqg_ref[...] += dqg
        for hh in range(nkv):
            kv_ref[:, HEAD_DIM * hh:HEAD_DIM * (hh + 1)] = dk_ref[hh, :, 0:HEAD_DIM]
        dkg = jnp.zeros((1, 128), F32)
        for a in range(nkv // 2):
            sl = slice(128 * a, 128 * (a + 1))
            draw, dg = norm_rope_bwd(kv_ref[:, sl], kraw_ref[:, sl], kg_ref[...])
            dkg = dkg + dg
            dproj_ref[:, o_k + 128 * a:o_k + 128 * (a + 1)] = draw.astype(BF16)
        dkg_ref[...] += dkg
        for hh in range(nkv):
            kv_ref[:, HEAD_DIM * hh:HEAD_DIM * (hh + 1)] = dv_ref[hh]
        dproj_ref[:, o_v:o_g] = kv_ref[...].astype(BF16)
        dproj_ref[:, o_g:4 * d] = dgt_ref[...]
        dh = jnp.zeros((tm, d), F32)
        for blk in range(N_DEV):
            dh = dh + _dot_nt(dproj_ref[:, bw * blk:bw * (blk + 1)], w_ref[blk])
        x = x_ref[...]
        r = lax.rsqrt(jnp.mean(x * x, axis=-1, keepdims=True) + NORM_EPS)
        xh0 = x * r
        dgm_ref[...] += jnp.sum(dh * xh0, axis=0, keepdims=True)
        dxh = dh * g_ref[...]
        dx0_ref[...] = dx1_ref[...] + r * (dxh - xh0 * jnp.mean(dxh * xh0, axis=-1, keepdims=True))

    return _pcall(
        body, name="in_proj_bwd", grid=(t // tm,),
        in_specs=[_row_spec(tm, d), _row_spec(tm, d), _row_spec(tm, du), _row_spec(tm, du), _row_spec(tm, du),
                  _row_spec(tm, d), _row_spec(tm, dkw), pl.BlockSpec((d, tm), lambda i: (0, i)),
                  pl.BlockSpec((nkv, tm, 128), lambda i: (0, i, 0)), _heads_spec(nkv, tm),
                  _row_spec(tm, 2 * d), _full_spec((1, d)), _ANY, _full_spec((1, 128)), _full_spec((1, 128)),
                  _row_spec(tm, 128), _row_spec(tm, 128)],
        out_specs=[_row_spec(tm, d), _row_spec(tm, 4 * d), _full_spec((1, d)), _full_spec((1, 128)),
                   _full_spec((1, 128))],
        out_shape=[jax.ShapeDtypeStruct((t, d), F32), jax.ShapeDtypeStruct((t, 4 * d), BF16),
                   jax.ShapeDtypeStruct((1, d), F32), jax.ShapeDtypeStruct((1, 128), F32),
                   jax.ShapeDtypeStruct((1, 128), F32)],
        scratch=[pltpu.VMEM((N_DEV, d, bw), BF16), pltpu.VMEM((tm, dkw), F32), pltpu.SemaphoreType.DMA((1,))],
        vmem=V7X_VMEM_LIMIT,
    )(x0, dx1, dud, duf, dub, qraw, kraw, dq, dk, dv, dgates, g_mix, w_in, qg, kg, cos, sin)


def _attn_fwd(qat, ka, vta):
    nh, _, t = qat.shape
    nkv = ka.shape[0]
    rep = nh // nkv
    hd = HEAD_DIM
    vr = vta.shape[1]
    tq = tk = KV_TILE

    def body(qt_ref, k_ref, vt_ref, ot_ref, lse_ref, m_scr, acc_scr, excess_scr):
        j = pl.program_id(2)
        src = j % 2
        dst = 1 - src

        @pl.when(j == 0)
        def _():
            m_scr[0] = jnp.full(m_scr.shape[1:], NEG_BIG, F32)
            acc_scr[0] = jnp.zeros(acc_scr.shape[1:], F32)
            excess_scr[...] = jnp.full(excess_scr.shape, -NEG_BIG, F32)

        k = k_ref[0]
        vt = vt_ref[0]
        strips = [(r, c) for r in range(rep) for c in range(0, tq, QUERY_STRIP)]
        scores = lambda r, c: _dot(k, qt_ref[r, :, c:c + QUERY_STRIP])

        def sweep(one_pass):
            def add_values(r, cols, before, after, pt):
                acc = acc_scr[src, r, :, cols]
                acc_scr[dst, r, :, cols] = after * ((acc if before is None else before * acc) + _dot(vt, pt))

            ahead = [scores(*strips[0]), scores(*strips[1])]
            pending = None
            excess = jnp.full((1, QUERY_STRIP), NEG_BIG, F32)
            for n, (r, c) in enumerate(strips):
                st = ahead.pop(0)
                if n + 2 < len(strips):
                    ahead.append(scores(*strips[n + 2]))
                cols = slice(c, c + QUERY_STRIP)
                m_prev = m_scr[src, r, :, cols]
                if one_pass:
                    pt = jnp.exp(st - m_prev).astype(BF16)
                    tile_max = jnp.max(st, axis=0, keepdims=True)
                    m_next = jnp.maximum(m_prev, tile_max)
                    excess = jnp.maximum(excess, tile_max - m_prev)
                    factors = (None, jnp.exp(m_prev - m_next))
                else:
                    m_next = jnp.maximum(m_prev, jnp.max(st, axis=0, keepdims=True))
                    pt = jnp.exp(st - m_next).astype(BF16)
                    factors = (jnp.exp(m_prev - m_next), 1.0)
                m_scr[dst, r, :, cols] = m_next
                if pending is not None:
                    add_values(*pending)
                pending = (r, cols, *factors, pt)
            add_values(*pending)
            return excess

        @pl.when(j > 0)
        def _():
            excess_scr[...] = sweep(one_pass=True)

        @pl.when(jnp.max(excess_scr[...]) > ONE_PASS_SLACK)
        def _():
            sweep(one_pass=False)

        @pl.when(j == pl.num_programs(2) - 1)
        def _():
            for r in range(rep):
                l = acc_scr[dst, r, hd:hd + 1, :]
                ot_ref[hd * r:hd * (r + 1), :] = acc_scr[dst, r, 0:hd, :] / l
                lse_ref[0, r:r + 1, :] = m_scr[dst, r] + jnp.log(l)

    return _pcall(
        body, name="attn_fwd", grid=(nkv, t // tq, t // tk),
        in_specs=[pl.BlockSpec((rep, 128, tq), lambda g, i, j: (g, 0, i)),
                  pl.BlockSpec((1, tk, 128), lambda g, i, j: (g, j, 0)),
                  pl.BlockSpec((1, vr, tk), lambda g, i, j: (g, 0, j))],
        out_specs=[pl.BlockSpec((rep * hd, tq), lambda g, i, j: (g, i)),
                   pl.BlockSpec((1, rep, tq), lambda g, i, j: (g, 0, i))],
        out_shape=[jax.ShapeDtypeStruct((nh * hd, t), F32), jax.ShapeDtypeStruct((nkv, rep, t), F32)],
        scratch=[pltpu.VMEM((2, rep, 1, tq), F32), pltpu.VMEM((2, rep, vr, tq), F32),
                 pltpu.VMEM((1, QUERY_STRIP), F32)],
        vmem=V7X_VMEM_LIMIT,
    )(qat, ka, vta)


def _attn_bwd(qat, ka, kt, va, dot, ot, lse_row):
    nh, _, t = qat.shape
    nkv = ka.shape[0]
    rep = nh // nkv
    hd = HEAD_DIM
    tq = tk = KV_TILE

    def body(qt_ref, k_ref, kt_ref, v_ref, dot_ref, ot_ref, lse_ref, dk_ref, dv_ref, dqt_ref):
        j = pl.program_id(1)
        i = pl.program_id(2)

        @pl.when(jnp.logical_and(j == 0, i == 0))
        def _():
            dqt_ref[...] = jnp.zeros_like(dqt_ref)

        @pl.when(i == 0)
        def _():
            dk_ref[...] = jnp.zeros_like(dk_ref)
            dv_ref[...] = jnp.zeros_like(dv_ref)

        k = k_ref[0]
        kt = kt_ref[0]
        v = v_ref[0, :, 0:hd]
        cols = pl.ds(pl.multiple_of(i * tq, tq), tq)
        dk = jnp.zeros((tk, 128), F32)
        dv = jnp.zeros((tk, hd), F32)
        products = lambda r: (_dot(k, qt_ref[r]), _dot(v, dot_ref[hd * r:hd * (r + 1), :]))
        nxt = products(0)
        for r in range(rep):
            st, dpt = nxt
            if r + 1 < rep:
                nxt = products(r + 1)
            heads = slice(hd * r, hd * (r + 1))
            qt = qt_ref[r]
            dot_r = dot_ref[heads, :]
            delta = jnp.sum(dot_r.astype(F32) * ot_ref[heads, :], axis=0, keepdims=True)
            pt = jnp.exp(st - lse_ref[0, r:r + 1, :])
            dst = (pt * (dpt - delta)).astype(BF16)
            dv = dv + _dot_nt(pt.astype(BF16), dot_r)
            dk = dk + _dot_nt(dst, qt)
            dqt_ref[heads, cols] += _dot(kt, dst)
        dk_ref[0] += dk
        dv_ref[0] += dv

    return _pcall(
        body, name="attn_bwd", grid=(nkv, t // tk, t // tq),
        in_specs=[pl.BlockSpec((rep, 128, tq), lambda g, j, i: (g, 0, i)),
                  pl.BlockSpec((1, tk, 128), lambda g, j, i: (g, j, 0)),
                  pl.BlockSpec((1, hd, tk), lambda g, j, i: (g, 0, j)),
                  pl.BlockSpec((1, tk, 128), lambda g, j, i: (g, j, 0)),
                  pl.BlockSpec((rep * hd, tq), lambda g, j, i: (g, i)),
                  pl.BlockSpec((rep * hd, tq), lambda g, j, i: (g, i)),
                  pl.BlockSpec((1, rep, tq), lambda g, j, i: (g, 0, i))],
        out_specs=[pl.BlockSpec((1, tk, 128), lambda g, j, i: (g, j, 0)),
                   pl.BlockSpec((1, tk, hd), lambda g, j, i: (g, j, 0)),
                   pl.BlockSpec((rep * hd, t), lambda g, j, i: (g, 0))],
        out_shape=[jax.ShapeDtypeStruct((nkv, t, 128), F32), jax.ShapeDtypeStruct((nkv, t, hd), F32),
                   jax.ShapeDtypeStruct((nh * hd, t), F32)],
        vmem=V7X_VMEM_LIMIT,
    )(qat, ka, kt, va, dot, ot, lse_row)


def _riding_exchange(refs, exchange, n_in, n_out, first_step, last_step):
    if exchange is None:
        return refs
    x_ref, out_ref = refs[n_in], refs[n_in + 1 + n_out]
    sems = refs[-3:]

    @pl.when(first_step)
    def _():
        _start_all(*_exchange_copies(x_ref, out_ref, *sems, exchange[1]))

    @pl.when(last_step)
    def _():
        _wait_all(*_exchange_copies(x_ref, out_ref, *sems, exchange[1]))

    return refs[:n_in] + refs[n_in + 1:n_in + 1 + n_out] + refs[n_in + 2 + n_out:-3]


def _segmented_scan(src_re, src_im, dst_re, dst_im, lam_re, lam_im, pow_re, pow_im, carry_re, carry_im,
                    end_re, end_im, in_re, in_im, lanes, descending, conj):
    tc = src_re.shape[0]
    seg = tc // 8
    width = lanes.size
    sign = -1.0 if conj else 1.0
    rows_of = lambda q: pl.ds(8 * (seg - 1 - q if descending else q), 8)
    lr = jnp.broadcast_to(lam_re[:, lanes], (8, width))
    li = jnp.broadcast_to(sign * lam_im[:, lanes], (8, width))
    xr = jnp.zeros((8, width), F32)
    xi = jnp.zeros((8, width), F32)
    for q in range(seg):
        rows = rows_of(q)
        xr, xi = (lr * xr - li * xi) + src_re[rows, lanes], (lr * xi + li * xr) + src_im[rows, lanes]
        dst_re[rows, lanes] = xr
        dst_im[rows, lanes] = xi
    end_re[:, lanes] = xr
    end_im[:, lanes] = xi
    sr = pow_re[seg - 1:seg, lanes]
    si = sign * pow_im[seg - 1:seg, lanes]
    cr = carry_re[:, lanes]
    ci = carry_im[:, lanes]
    for s in range(8):
        se = 7 - s if descending else s
        in_re[se:se + 1, lanes] = cr
        in_im[se:se + 1, lanes] = ci
        cr, ci = (end_re[se:se + 1, lanes] + (sr * cr - si * ci)), (end_im[se:se + 1, lanes] + (sr * ci + si * cr))
    carry_re[:, lanes] = cr
    carry_im[:, lanes] = ci
    ir = in_re[:, lanes]
    ii = in_im[:, lanes]
    for q in range(seg):
        rows = rows_of(q)
        pr = pow_re[q:q + 1, lanes]
        pi = sign * pow_im[q:q + 1, lanes]
        dst_re[rows, lanes] = dst_re[rows, lanes] + (pr * ir - pi * ii)
        dst_im[rows, lanes] = dst_im[rows, lanes] + (pr * ii + pi * ir)


def _diag_tiles(gn):
    rows_per_tile = DIAG_TILE // (SSM_STATE // SSM_GROUP)
    return [(slice(rows_per_tile * j, rows_per_tile * (j + 1)), slice(DIAG_TILE * j, DIAG_TILE * (j + 1)))
            for j in range(gn // DIAG_TILE)]


def _ssm_scan_fwd(ub, lam_re, lam_im, pow_re, pow_im, bb_re, bb_im, cc_re, cc_im, exchange=None):
    t, w = ub.shape
    gn = lam_re.shape[-1]
    tc = ROW_TILE
    cl = min(gn, SCAN_LANES)
    nblk = t // tc
    tiles = _diag_tiles(gn)

    def body(*refs):
        first = jnp.logical_and(pl.program_id(0) == 0, pl.program_id(1) == 0)
        last = jnp.logical_and(pl.program_id(0) == 1, pl.program_id(1) == nblk - 1)
        (u_ref, lr_ref, li_ref, pr_ref, pi_ref, br_ref, bi_ref, cr_ref, ci_ref, y_ref, xr_ref, xi_ref,
         bur_scr, bui_scr, cr_scr, ci_scr, er_scr, ei_scr, nr_scr, ni_scr) = _riding_exchange(
             refs, exchange, 9, 3, first, last)

        @pl.when(pl.program_id(1) == 0)
        def _():
            cr_scr[...] = jnp.zeros_like(cr_scr)
            ci_scr[...] = jnp.zeros_like(ci_scr)

        for rows, lanes in tiles:
            u_j = u_ref[:, rows]
            bur_scr[:, lanes] = _dot(u_j, br_ref[0, rows, lanes])
            bui_scr[:, lanes] = _dot(u_j, bi_ref[0, rows, lanes])
        for descending in (False, True):
            @pl.when(pl.program_id(0) == int(descending))
            def _(descending=descending):
                for c0 in range(0, gn, cl):
                    _segmented_scan(bur_scr, bui_scr, xr_ref.at[0], xi_ref.at[0], lr_ref.at[0], li_ref.at[0],
                                    pr_ref.at[0], pi_ref.at[0], cr_scr, ci_scr, er_scr, ei_scr, nr_scr, ni_scr,
                                    pl.ds(c0, cl), descending, conj=False)
        for rows, lanes in tiles:
            y_ref[0, :, rows] = (_dot(xr_ref[0, :, lanes].astype(BF16), cr_ref[0, lanes, rows])
                                 - _dot(xi_ref[0, :, lanes].astype(BF16), ci_ref[0, lanes, rows]))

    blk = lambda dd, i: jnp.where(dd == 0, i, nblk - 1 - i)
    row = lambda width: pl.BlockSpec((1, tc, width), lambda dd, i: (dd, blk(dd, i), 0))
    per_dir = lambda a, b: pl.BlockSpec((1, a, b), lambda dd, i: (dd, 0, 0))
    extra = exchange is not None
    return _pcall(
        body, name="ssm_scan_fwd", grid=(2, nblk),
        in_specs=[pl.BlockSpec((tc, w), lambda dd, i: (blk(dd, i), 0)), per_dir(1, gn), per_dir(1, gn),
                  per_dir(tc // 8, gn), per_dir(tc // 8, gn),
                  per_dir(w, gn), per_dir(w, gn), per_dir(gn, w), per_dir(gn, w)] + [_ANY] * extra,
        out_specs=[row(w), row(gn), row(gn)] + [_ANY] * extra,
        out_shape=[jax.ShapeDtypeStruct((2, t, w), F32), jax.ShapeDtypeStruct((2, t, gn), F32),
                   jax.ShapeDtypeStruct((2, t, gn), F32)] + ([_exchange_out_shape(*exchange)] if extra else []),
        scratch=[pltpu.VMEM((tc, gn), F32), pltpu.VMEM((tc, gn), F32), pltpu.VMEM((1, gn), F32),
                 pltpu.VMEM((1, gn), F32)] + [pltpu.VMEM((8, gn), F32)] * 4 + _EXCHANGE_SEMS * extra,
        vmem=V7X_VMEM_LIMIT,
    )(ub, lam_re, lam_im, pow_re, pow_im, bb_re, bb_im, cc_re, cc_im, *([exchange[0]] if extra else []))


def _ssm_scan_bwd(dyb, ub, xs_re, xs_im, lam_re, lam_im, pow_re, pow_im, cct_re, cct_im, bbt_re, bbt_im,
                  exchange=None):
    t, w = dyb.shape
    gn = lam_re.shape[-1]
    tc = ROW_TILE
    cl = min(gn, SCAN_LANES)
    nblk = t // tc
    tiles = _diag_tiles(gn)

    def body(*refs):
        i = pl.program_id(1)
        first = jnp.logical_and(pl.program_id(0) == 0, i == 0)
        last = jnp.logical_and(pl.program_id(0) == 1, i == nblk - 1)
        (dy_ref, u_ref, xr_ref, xi_ref, hr_ref, hi_ref, lr_ref, li_ref, pr_ref, pi_ref, ctr_ref, cti_ref, btr_ref,
         bti_ref, du_ref, dlr_ref, dli_ref, dbr_ref, dbi_ref, dcr_ref, dci_ref,
         gxr_scr, gxi_scr, cr_scr, ci_scr, ar_scr, ai_scr, er_scr, ei_scr, nr_scr, ni_scr) = _riding_exchange(
             refs, exchange, 14, 7, first, last)

        @pl.when(i == 0)
        def _():
            for ref in (cr_scr, ci_scr, ar_scr, ai_scr, dbr_ref, dbi_ref, dcr_ref, dci_ref):
                ref[...] = jnp.zeros_like(ref)

        for rows, lanes in tiles:
            dy_j = dy_ref[:, rows]
            gxr_scr[:, lanes] = _dot(dy_j, ctr_ref[0, rows, lanes])
            gxi_scr[:, lanes] = -_dot(dy_j, cti_ref[0, rows, lanes])
        first_block = i == nblk - 1
        sublane = lax.broadcasted_iota(jnp.int32, (8, 1), 0)

        def lam_gradient(state_descending):
            for c0 in range(0, gn, 512):
                lanes = pl.ds(c0, 512)
                if state_descending:
                    cur, prev, edge, src = pl.ds(0, tc - 8), pl.ds(8, tc - 8), pl.ds(tc - 8, 8), pl.ds(0, 8)
                    halo_at, halo_row, shift = 7, 0, 7
                else:
                    cur, prev, edge, src = pl.ds(8, tc - 8), pl.ds(0, tc - 8), pl.ds(0, 8), pl.ds(tc - 8, 8)
                    halo_at, halo_row, shift = 0, 7, 1
                halo_r = jnp.where(first_block, 0.0, hr_ref[0, halo_row:halo_row + 1, lanes])
                halo_i = jnp.where(first_block, 0.0, hi_ref[0, halo_row:halo_row + 1, lanes])
                xer = jnp.where(sublane == halo_at, halo_r, pltpu.roll(xr_ref[0, src, lanes], shift, 0))
                xei = jnp.where(sublane == halo_at, halo_i, pltpu.roll(xi_ref[0, src, lanes], shift, 0))
                gr, gi = gxr_scr[cur, lanes], gxi_scr[cur, lanes]
                xpr, xpi = xr_ref[0, prev, lanes], xi_ref[0, prev, lanes]
                ger, gei = gxr_scr[edge, lanes], gxi_scr[edge, lanes]
                ar_scr[:, lanes] += (jnp.sum(gr * xpr + gi * xpi, axis=0, keepdims=True)
                                     + jnp.sum(ger * xer + gei * xei, axis=0, keepdims=True))
                ai_scr[:, lanes] += (jnp.sum(gi * xpr - gr * xpi, axis=0, keepdims=True)
                                     + jnp.sum(gei * xer - ger * xei, axis=0, keepdims=True))

        for descending in (True, False):
            @pl.when(pl.program_id(0) == int(not descending))
            def _(descending=descending):
                for c0 in range(0, gn, cl):
                    _segmented_scan(gxr_scr, gxi_scr, gxr_scr, gxi_scr, lr_ref.at[0], li_ref.at[0], pr_ref.at[0],
                                    pi_ref.at[0], cr_scr, ci_scr, er_scr, ei_scr, nr_scr, ni_scr, pl.ds(c0, cl),
                                    descending, conj=True)
                lam_gradient(state_descending=not descending)
        dlr_ref[0] = ar_scr[...]
        dli_ref[0] = ai_scr[...]
        for rows, lanes in tiles:
            grb = gxr_scr[:, lanes].astype(BF16)
            gib = gxi_scr[:, lanes].astype(BF16)
            du_ref[0, :, rows] = _dot(grb, btr_ref[0, lanes, rows]) + _dot(gib, bti_ref[0, lanes, rows])
            u_j = u_ref[:, rows]
            dy_j = dy_ref[:, rows]
            dbr_ref[0, rows, :] += _dot_tn(u_j, grb)
            dbi_ref[0, rows, :] += _dot_tn(u_j, gib)
            dcr_ref[0, rows, :] += _dot_tn(dy_j, xr_ref[0, :, lanes].astype(BF16))
            dci_ref[0, rows, :] -= _dot_tn(dy_j, xi_ref[0, :, lanes].astype(BF16))

    blk = lambda dd, i: jnp.where(dd == 0, nblk - 1 - i, i)
    rev = lambda width: pl.BlockSpec((1, tc, width), lambda dd, i: (dd, blk(dd, i), 0))
    halo_blk = lambda dd, i: jnp.where(dd == 0, jnp.maximum(blk(dd, i) * (tc // 8) - 1, 0),
                                       jnp.minimum((blk(dd, i) + 1) * (tc // 8), t // 8 - 1))
    halo = pl.BlockSpec((1, 8, gn), lambda dd, i: (dd, halo_blk(dd, i), 0))
    per_dir = lambda a, b: pl.BlockSpec((1, a, b), lambda dd, i: (dd, 0, 0))
    extra = exchange is not None
    return _pcall(
        body, name="ssm_scan_bwd", grid=(2, nblk),
        in_specs=[pl.BlockSpec((tc, w), lambda dd, i: (blk(dd, i), 0)),
                  pl.BlockSpec((tc, w), lambda dd, i: (blk(dd, i), 0)), rev(gn), rev(gn), halo, halo,
                  per_dir(1, gn), per_dir(1, gn), per_dir(tc // 8, gn), per_dir(tc // 8, gn),
                  per_dir(w, gn), per_dir(w, gn), per_dir(gn, w), per_dir(gn, w)]
        + [_ANY] * extra,
        out_specs=[rev(w), per_dir(1, gn), per_dir(1, gn)] + [per_dir(w, DIAG_TILE)] * 4 + [_ANY] * extra,
        out_shape=[jax.ShapeDtypeStruct((2, t, w), F32), jax.ShapeDtypeStruct((2, 1, gn), F32),
                   jax.ShapeDtypeStruct((2, 1, gn), F32)] + [jax.ShapeDtypeStruct((2, w, DIAG_TILE), F32)] * 4
        + ([_exchange_out_shape(*exchange)] if extra else []),
        scratch=[pltpu.VMEM((tc, gn), F32), pltpu.VMEM((tc, gn), F32)] + [pltpu.VMEM((1, gn), F32)] * 4
        + [pltpu.VMEM((8, gn), F32)] * 4 + _EXCHANGE_SEMS * extra,
        vmem=V7X_VMEM_LIMIT,
    )(dyb, ub, xs_re, xs_im, xs_re, xs_im, lam_re, lam_im, pow_re, pow_im, cct_re, cct_im, bbt_re, bbt_im,
      *([exchange[0]] if extra else []))


def _matmul_tn(a, b, name, a_is_transposed=False, exchange=None):
    t, n = b.shape
    m = a.shape[0] if a_is_transposed else a.shape[1]
    bm, bn, tk = min(m, 1024), min(n, 1024), KV_TILE
    grid = (m // bm, n // bn, t // tk)

    def body(*refs):
        at = lambda step: functools.reduce(jnp.logical_and, [pl.program_id(ax) == step[ax] for ax in range(3)])
        a_ref, b_ref, o_ref = _riding_exchange(refs, exchange, 2, 1, at((0, 0, 0)), at([g - 1 for g in grid]))

        @pl.when(pl.program_id(2) == 0)
        def _():
            o_ref[...] = jnp.zeros_like(o_ref)

        mul = _dot if a_is_transposed else _dot_tn
        o_ref[...] += mul(a_ref[...].astype(BF16), b_ref[...].astype(BF16))

    a_spec = (pl.BlockSpec((bm, tk), lambda i, j, k: (i, k)) if a_is_transposed else
              pl.BlockSpec((tk, bm), lambda i, j, k: (k, i)))
    extra = exchange is not None
    out = _pcall(
        body, name=name, grid=grid,
        in_specs=[a_spec, pl.BlockSpec((tk, bn), lambda i, j, k: (k, j))] + [_ANY] * extra,
        out_specs=[pl.BlockSpec((bm, bn), lambda i, j, k: (i, j))] + [_ANY] * extra,
        out_shape=[jax.ShapeDtypeStruct((m, n), F32)] + ([_exchange_out_shape(*exchange)] if extra else []),
        scratch=_EXCHANGE_SEMS * extra, vmem=V7X_VMEM_LIMIT,
    )(a, b, *([exchange[0]] if extra else []))
    return out if extra else out[0]


def _reduce_adamw(gparts, p, m, v, name, exchange=None):
    rows, width = p.shape
    tr = max(k for k in range(16, 513, 16) if rows % k == 0)

    def body(*refs):
        i = pl.program_id(0)
        g_ref, p_ref, m_ref, v_ref, go_ref, d_ref, mo_ref, vo_ref = _riding_exchange(
            refs, exchange, 4, 4, i == 0, i == rows // tr - 1)
        g = g_ref[0].astype(F32)
        for k in range(1, N_DEV):
            g = g + g_ref[k].astype(F32)
        go_ref[...] = g
        mm = ADAM_B1 * m_ref[...] + (1.0 - ADAM_B1) * g
        vv = ADAM_B2 * v_ref[...] + (1.0 - ADAM_B2) * (g * g)
        m_hat = mm / (1.0 - ADAM_B1 ** ADAM_STEP)
        v_hat = vv / (1.0 - ADAM_B2 ** ADAM_STEP)
        d_ref[...] = -ADAM_LR * (m_hat / (jnp.sqrt(v_hat) + ADAM_EPS) + ADAM_WD * p_ref[...])
        mo_ref[...] = mm
        vo_ref[...] = vv

    spec = pl.BlockSpec((tr, width), lambda i: (i, 0))
    out = jax.ShapeDtypeStruct((rows, width), F32)
    extra = exchange is not None
    return _pcall(
        body, name=name, grid=(rows // tr,),
        in_specs=[pl.BlockSpec((N_DEV, tr, width), lambda i: (0, i, 0)), spec, spec, spec] + [_ANY] * extra,
        out_specs=[spec, spec, spec, spec] + [_ANY] * extra,
        out_shape=[out, out, out, out] + ([_exchange_out_shape(*exchange)] if extra else []),
        scratch=_EXCHANGE_SEMS * extra, vmem=V7X_VMEM_LIMIT,
    )(gparts, p, m, v, *([exchange[0]] if extra else []))


def _peer(k):
    x, y, c = lax.axis_index("x"), lax.axis_index("y"), lax.axis_index("c")
    return (x ^ ((k >> 2) & 1), y ^ ((k >> 1) & 1), c ^ (k & 1))


def _my_index():
    return 4 * lax.axis_index("x") + 2 * lax.axis_index("y") + lax.axis_index("c")


def _exchange_copies(x_ref, out_ref, send_sems, recv_sems, local_sem, scatter, first_sem=0):
    me = _my_index()
    local = pltpu.make_async_copy(x_ref.at[me] if scatter else x_ref, out_ref.at[me], local_sem)
    copies = []
    for k in range(1, N_DEV):
        peer = _peer(k)
        src = x_ref.at[4 * peer[0] + 2 * peer[1] + peer[2]] if scatter else x_ref
        copies.append(pltpu.make_async_remote_copy(
            src_ref=src, dst_ref=out_ref.at[me], send_sem=send_sems.at[first_sem + k - 1],
            recv_sem=recv_sems.at[first_sem + k - 1], device_id=peer, device_id_type=pl.DeviceIdType.MESH))
    return local, copies


def _start_all(local, copies):
    local.start()
    for cp in copies:
        cp.start()


def _wait_all(local, copies):
    for cp in copies:
        cp.wait_recv()
    for cp in copies:
        cp.wait_send()
    local.wait()


def _exchange_out_shape(x, scatter):
    return jax.ShapeDtypeStruct((N_DEV,) + tuple(x.shape[1:] if scatter else x.shape), x.dtype)


_EXCHANGE_SEMS = [pltpu.SemaphoreType.DMA((N_DEV - 1,)), pltpu.SemaphoreType.DMA((N_DEV - 1,)),
                  pltpu.SemaphoreType.DMA(())]


def _exchange(ops, name):
    n = len(ops)

    def body(*refs):
        x_refs, out_refs = refs[:n], refs[n:2 * n]
        send_sems, recv_sems, local_sems = refs[2 * n:]
        started = []
        for q, (_, scatter) in enumerate(ops):
            local, copies = _exchange_copies(x_refs[q], out_refs[q], send_sems, recv_sems, local_sems.at[q],
                                             scatter, first_sem=q * (N_DEV - 1))
            _start_all(local, copies)
            started.append((local, copies))
        for local, copies in started:
            _wait_all(local, copies)

    return pl.pallas_call(
        body, name=name, in_specs=[_ANY] * n, out_specs=[_ANY] * n,
        out_shape=[_exchange_out_shape(x, scatter) for x, scatter in ops],
        scratch_shapes=[pltpu.SemaphoreType.DMA((n * (N_DEV - 1),)), pltpu.SemaphoreType.DMA((n * (N_DEV - 1),)),
                        pltpu.SemaphoreType.DMA((n,))],
    )(*[x for x, _ in ops])


def _to_shards(full, axis):
    r, c = full.shape
    if axis == 0:
        return full.reshape(N_DEV, r // N_DEV, c)
    return full.reshape(r, N_DEV, c // N_DEV).transpose(1, 0, 2)


def _from_shards(shards, axis):
    _, r, c = shards.shape
    if axis == 0:
        return shards.reshape(N_DEV * r, c)
    return shards.transpose(1, 0, 2).reshape(r, N_DEV * c)


def _pack_rows(parts, lead):
    flat = []
    for p in parts:
        p = p.reshape(p.shape[:lead] + (-1, PACK_W))
        pad = _round_up(p.shape[lead], 16) - p.shape[lead]
        flat.append(jnp.pad(p, [(0, 0)] * lead + [(0, pad), (0, 0)]) if pad else p)
    return jnp.concatenate(flat, axis=lead)


def _unpack_rows(packed, shapes):
    lead = packed.shape[:-2]
    out, off = [], 0
    for shp in shapes:
        rows = math.prod(shp) // PACK_W
        out.append(packed[..., off:off + rows, :].reshape(lead + tuple(shp)))
        off += _round_up(rows, 16)
    return out


def _pack_flat(parts):
    flat = jnp.concatenate([p.reshape(-1) for p in parts])
    n = flat.shape[0]
    flat = jnp.pad(flat, (0, _round_up(n, 16 * PACK_W) - n))
    return flat.reshape(-1, PACK_W)


def _unpack(packed, shapes):
    flat = packed.reshape(-1)
    out, off = [], 0
    for shp in shapes:
        n = math.prod(shp)
        out.append(flat[off:off + n].reshape(shp))
        off += n
    return out


def _ssm_discretize(a_re, a_im, log_dt, b_re, b_im):
    dt = jnp.exp(log_dt)[..., None]
    lam_re = jnp.minimum(a_re, EIG_RE_MAX)
    lam_im = a_im
    mag = jnp.exp(lam_re * dt)
    ang = lam_im * dt
    lb_re = mag * jnp.cos(ang)
    lb_im = mag * jnp.sin(ang)
    num_re = lb_re - 1.0
    num_im = lb_im
    den = lam_re * lam_re + lam_im * lam_im
    f_re = (num_re * lam_re + num_im * lam_im) / den
    f_im = (num_im * lam_re - num_re * lam_im) / den
    bb_re = f_re[..., None] * b_re - f_im[..., None] * b_im
    bb_im = f_re[..., None] * b_im + f_im[..., None] * b_re
    return lb_re, lb_im, bb_re, bb_im


def _ssm_powers(a_re, a_im, log_dt, count):
    dt = jnp.exp(log_dt)[:, None, :, None]
    k = jnp.arange(1, count + 1, dtype=F32)[None, :, None, None]
    mag = jnp.exp(k * (jnp.minimum(a_re, EIG_RE_MAX)[:, None] * dt))
    ang = k * (a_im[:, None] * dt)
    shape = (a_re.shape[0], count, -1)
    return (mag * jnp.cos(ang)).reshape(shape), (mag * jnp.sin(ang)).reshape(shape)


def _interleave(a, inverse=False):
    lead, (t, width) = a.shape[:-2], a.shape[-2:]
    seg = ROW_TILE // 8
    shape = lead + (t // ROW_TILE,) + ((seg, 8) if inverse else (8, seg)) + (width,)
    return jnp.swapaxes(a.reshape(shape), -3, -2).reshape(a.shape)


def _block_diag(blocks):
    two, g, a, b = blocks.shape
    tiled = jnp.tile(blocks.reshape(two, g * a, b), (1, 1, g))
    row_group = lax.broadcasted_iota(jnp.int32, (g * a, g * b), 0) // a
    col_group = lax.broadcasted_iota(jnp.int32, (g * a, g * b), 1) // b
    return jnp.where(row_group == col_group, tiled, 0.0).astype(BF16)


def _diag_blocks(tiles):
    two, w, _ = tiles.shape
    per = DIAG_TILE // SSM_STATE
    t6 = tiles.reshape(two, w // (per * SSM_GROUP), per, SSM_GROUP, per, SSM_STATE)
    return jnp.einsum("zjqpqn->zjqpn", t6).reshape(two, w // SSM_GROUP, SSM_GROUP, SSM_STATE)


def _rope_tables(t, n_valid):
    pos = jnp.arange(t)
    real = jnp.logical_and(pos >= N_META, pos < n_valid)
    idx = jnp.where(real, pos - N_META, 0)
    row_id = (idx // GRID_W).astype(F32)
    col_id = (idx % GRID_W).astype(F32)
    pairs_per_axis = HEAD_DIM // 4
    inv_freq = ROPE_THETA ** (-jnp.arange(pairs_per_axis, dtype=F32) / pairs_per_axis)
    ang = jnp.concatenate([row_id[:, None] * inv_freq, col_id[:, None] * inv_freq], axis=-1)
    ang = jnp.where(real[:, None], ang, 0.0)
    cos = jnp.repeat(jnp.cos(ang), 2, axis=-1)
    sin = jnp.sin(ang)
    sin = jnp.stack([-sin, sin], axis=-1).reshape(t, HEAD_DIM)
    return jnp.tile(cos, (1, 2)), jnp.tile(sin, (1, 2))


def _local_step(x, loss_target, big, small, comm=None):
    s_len, d = x.shape
    n_valid = s_len + N_META
    t = _round_up(n_valid, KV_TILE)
    du = d // 2
    groups = du // SSM_GROUP
    nh = d // HEAD_DIM
    nkv = nh // KV_REP
    pad = t - n_valid

    x0 = jnp.concatenate([big["meta_tokens"].astype(F32), x, jnp.zeros((pad, d), F32)], axis=0)
    tgt = jnp.concatenate([jnp.zeros((N_META, d), F32), loss_target, jnp.zeros((pad, d), F32)], axis=0)
    cos, sin = _rope_tables(t, n_valid)
    g_mix = small["norm_mix_g"].reshape(1, d)
    g_mlp = small["norm_mlp_g"].reshape(1, d)
    g_fin = small["norm_final_g"].reshape(1, d)
    qg = jnp.tile(small["q_norm_g"].reshape(1, HEAD_DIM), (1, 2))
    kg = jnp.tile(small["k_norm_g"].reshape(1, HEAD_DIM), (1, 2))
    ssm_d = small["ssm_d"].reshape(1, du)
    b_glu = small["b_glu"].reshape(1, du)

    ssm_in = tuple(small[n][0] for n in ("ssm_a_re", "ssm_a_im", "ssm_log_dt", "ssm_b_re", "ssm_b_im"))
    (lb_re, lb_im, bbar_re, bbar_im), disc_vjp = jax.vjp(_ssm_discretize, *ssm_in)
    lam_re = lb_re.reshape(2, 1, groups * SSM_STATE)
    lam_im = lb_im.reshape(2, 1, groups * SSM_STATE)
    pow_re, pow_im = _ssm_powers(*ssm_in[0:3], ROW_TILE // 8)
    bb_re = _block_diag(bbar_re.transpose(0, 1, 3, 2))
    bb_im = _block_diag(bbar_im.transpose(0, 1, 3, 2))
    c_re, c_im = small["ssm_c_re"][0], small["ssm_c_im"][0]
    cct_re = _block_diag(c_re)
    cct_im = _block_diag(c_im)
    cc_re = cct_re.transpose(0, 2, 1)
    cc_im = cct_im.transpose(0, 2, 1)
    bbt_re = bb_re.transpose(0, 2, 1)
    bbt_im = bb_im.transpose(0, 2, 1)
    scan_w = (lam_re, lam_im, pow_re, pow_im)

    in_proj_args = (x0, g_mix, big["w_in"], qg, kg, cos, sin, n_valid)
    if comm is None:
        h, u, ub, qraw, kraw, qat, ka, kt, va, vta, gates = _in_proj_fwd(*in_proj_args)
    else:
        h, u, ub, qraw, kraw, qat, ka, kt, va, vta, gates, got = _in_proj_fwd(
            *in_proj_args, exchange=(comm["pack_weights"](MIXER_WEIGHTS), False))
        big = {**big, **comm["unpack_weights"](MIXER_WEIGHTS, got)}
    ub = _interleave(ub)
    if comm is None:
        y2, xs_re, xs_im = _ssm_scan_fwd(ub, *scan_w, bb_re, bb_im, cc_re, cc_im)
    else:
        y2, xs_re, xs_im, got = _ssm_scan_fwd(ub, *scan_w, bb_re, bb_im, cc_re, cc_im,
                                              exchange=(comm["pack_weights"](MLP_WEIGHTS), False))
        big = {**big, **comm["unpack_weights"](MLP_WEIGHTS, got)}
    y2 = _interleave(y2, inverse=True)
    yf, yb = y2[0], y2[1]
    yt_attn, lse = _attn_fwd(qat, ka, vta)
    mixer_w = (ssm_d, big["w_glu"], b_glu, big["w_ssm_proj"], big["w_attn_proj"], big["w_out"])
    x1 = _mixer_out_fwd(x0, u, yf, yb, yt_attn, gates, *mixer_w)

    dx1, loss8, dg_fin, dg_mlp, h2b, dab, hsqb, dx2b = _mlp_loss_fwd_bwd(
        x1, tgt, g_mlp, g_fin, big["w_mlp_in"], big["w_mlp_out"], n_valid)

    (dyb, dud, dyt_attn, dgates, zb, dglb, ysb, dasb, daab, mgb, dxb, d_ssm_d, d_b_glu) = _mixer_out_bwd(
        dx1, u, yf, yb, yt_attn, gates, *mixer_w)
    grads = {}
    grads["w_glu"] = _matmul_tn(zb, dglb, "grad_w_glu")
    grads["w_ssm_proj"] = _matmul_tn(ysb, dasb, "grad_w_ssm_proj")
    grads["w_attn_proj"] = _matmul_tn(yt_attn, daab, "grad_w_attn_proj", a_is_transposed=True)
    grads["w_out"] = _matmul_tn(mgb, dxb, "grad_w_out")
    grads["w_mlp_in"] = _matmul_tn(h2b, dab, "grad_w_mlp_in")
    grads["w_mlp_out"] = _matmul_tn(hsqb, dx2b, "grad_w_mlp_out")
    dk, dv, dqt = _attn_bwd(qat, ka, kt, va, dyt_attn, yt_attn, lse)
    scan_args = (_interleave(dyb), ub, xs_re, xs_im, *scan_w, cct_re, cct_im, bbt_re, bbt_im)
    if comm is None:
        late_grad_parts = None
        du2, dlam_re, dlam_im, dbr, dbi, dcr, dci = _ssm_scan_bwd(*scan_args)
    else:
        du2, dlam_re, dlam_im, dbr, dbi, dcr, dci, late_grad_parts = _ssm_scan_bwd(
            *scan_args, exchange=(comm["pack_grads"](LATE_WEIGHTS, grads), True))
    du2 = _interleave(du2, inverse=True)
    dx0, dproj, dg_mix, dqg, dkg = _in_proj_bwd(x0, dx1, dud, du2[0], du2[1], qraw, kraw, dqt, dk, dv, dgates,
                                                g_mix, big["w_in"], qg, kg, cos, sin)

    grads["meta_tokens"] = dx0[0:N_META]
    dbb_re = _diag_blocks(dbr).transpose(0, 1, 3, 2)
    dbb_im = _diag_blocks(dbi).transpose(0, 1, 3, 2)
    dc_re, dc_im = _diag_blocks(dcr), _diag_blocks(dci)
    shape_gn = (2, groups, SSM_STATE)
    d_a_re, d_a_im, d_log_dt, d_b_re, d_b_im = disc_vjp(
        (dlam_re.reshape(shape_gn), dlam_im.reshape(shape_gn), dbb_re, dbb_im))
    grads.update({
        "norm_mix_g": dg_mix, "ssm_a_re": d_a_re[None], "ssm_a_im": d_a_im[None], "ssm_log_dt": d_log_dt[None],
        "ssm_b_re": d_b_re[None], "ssm_b_im": d_b_im[None], "ssm_c_re": dc_re[None], "ssm_c_im": dc_im[None],
        "ssm_d": d_ssm_d, "b_glu": d_b_glu,
        "q_norm_g": dqg[:, 0:HEAD_DIM] + dqg[:, HEAD_DIM:128], "k_norm_g": dkg[:, 0:HEAD_DIM] + dkg[:, HEAD_DIM:128],
        "norm_mlp_g": dg_mlp, "norm_final_g": dg_fin.reshape(d),
    })
    if comm is None:
        small_grad_parts = None
        grads["w_in"] = _matmul_tn(h, dproj, "grad_w_in")
    else:
        grads["w_in"], small_grad_parts = _matmul_tn(h, dproj, "grad_w_in",
                                                     exchange=(comm["pack_small_grads"](grads), False))
    return loss8[0, 0], dx0[N_META:n_valid], grads, late_grad_parts, small_grad_parts


def kernel(x, meta_tokens, norm_mix_g, w_in, ssm_a_re, ssm_a_im, ssm_log_dt, ssm_b_re, ssm_b_im, ssm_c_re, ssm_c_im, ssm_d, w_glu, b_glu, q_norm_g, k_norm_g, w_ssm_proj, w_attn_proj, w_out, norm_mlp_g, w_mlp_in, w_mlp_out, norm_final_g, loss_target, m_meta_tokens, m_norm_mix_g, m_w_in, m_ssm_a_re, m_ssm_a_im, m_ssm_log_dt, m_ssm_b_re, m_ssm_b_im, m_ssm_c_re, m_ssm_c_im, m_ssm_d, m_w_glu, m_b_glu, m_q_norm_g, m_k_norm_g, m_w_ssm_proj, m_w_attn_proj, m_w_out, m_norm_mlp_g, m_w_mlp_in, m_w_mlp_out, m_norm_final_g, v_meta_tokens, v_norm_mix_g, v_w_in, v_ssm_a_re, v_ssm_a_im, v_ssm_log_dt, v_ssm_b_re, v_ssm_b_im, v_ssm_c_re, v_ssm_c_im, v_ssm_d, v_w_glu, v_b_glu, v_q_norm_g, v_k_norm_g, v_w_ssm_proj, v_w_attn_proj, v_w_out, v_norm_mlp_g, v_w_mlp_in, v_w_mlp_out, v_norm_final_g):
    w = dict(meta_tokens=meta_tokens, norm_mix_g=norm_mix_g, w_in=w_in, ssm_a_re=ssm_a_re, ssm_a_im=ssm_a_im, ssm_log_dt=ssm_log_dt, ssm_b_re=ssm_b_re, ssm_b_im=ssm_b_im, ssm_c_re=ssm_c_re, ssm_c_im=ssm_c_im, ssm_d=ssm_d, w_glu=w_glu, b_glu=b_glu, q_norm_g=q_norm_g, k_norm_g=k_norm_g, w_ssm_proj=w_ssm_proj, w_attn_proj=w_attn_proj, w_out=w_out, norm_mlp_g=norm_mlp_g, w_mlp_in=w_mlp_in, w_mlp_out=w_mlp_out, norm_final_g=norm_final_g)
    m = dict(meta_tokens=m_meta_tokens, norm_mix_g=m_norm_mix_g, w_in=m_w_in, ssm_a_re=m_ssm_a_re, ssm_a_im=m_ssm_a_im, ssm_log_dt=m_ssm_log_dt, ssm_b_re=m_ssm_b_re, ssm_b_im=m_ssm_b_im, ssm_c_re=m_ssm_c_re, ssm_c_im=m_ssm_c_im, ssm_d=m_ssm_d, w_glu=m_w_glu, b_glu=m_b_glu, q_norm_g=m_q_norm_g, k_norm_g=m_k_norm_g, w_ssm_proj=m_w_ssm_proj, w_attn_proj=m_w_attn_proj, w_out=m_w_out, norm_mlp_g=m_norm_mlp_g, w_mlp_in=m_w_mlp_in, w_mlp_out=m_w_mlp_out, norm_final_g=m_norm_final_g)
    v = dict(meta_tokens=v_meta_tokens, norm_mix_g=v_norm_mix_g, w_in=v_w_in, ssm_a_re=v_ssm_a_re, ssm_a_im=v_ssm_a_im, ssm_log_dt=v_ssm_log_dt, ssm_b_re=v_ssm_b_re, ssm_b_im=v_ssm_b_im, ssm_c_re=v_ssm_c_re, ssm_c_im=v_ssm_c_im, ssm_d=v_ssm_d, w_glu=v_w_glu, b_glu=v_b_glu, q_norm_g=v_q_norm_g, k_norm_g=v_k_norm_g, w_ssm_proj=v_w_ssm_proj, w_attn_proj=v_w_attn_proj, w_out=v_w_out, norm_mlp_g=v_norm_mlp_g, w_mlp_in=v_w_mlp_in, w_mlp_out=v_w_mlp_out, norm_final_g=v_norm_final_g)

    shard2d = {n: w[n].reshape(w[n].shape[-2:]) for n in BIG_WEIGHTS}
    big_shapes = [shard2d[n].shape for n in BIG_WEIGHTS]

    meta_hi = shard2d["meta_tokens"].astype(BF16)
    meta_res = shard2d["meta_tokens"] - meta_hi.astype(F32)
    meta_mid = meta_res.astype(BF16)
    meta_lo = (meta_res - meta_mid.astype(F32)).astype(BF16)
    shapes_of = lambda names: [shard2d[n].shape for n in names]

    def full_weights(names, shards):
        return {n: s if n in BLOCK_WEIGHTS else _from_shards(s, BIG_SHARD_AXIS[n]) for n, s in zip(names, shards)}

    early = _exchange([(_pack_rows([meta_hi, meta_mid, meta_lo, shard2d["w_in"].astype(BF16)], 0), False)],
                      "gather_early_weights")[0]
    shards = _unpack_rows(early, [meta_hi.shape] * 3 + shapes_of(EARLY_WEIGHTS[1:]))
    meta = [_from_shards(s, 1).astype(F32) for s in shards[0:3]]
    big = {"meta_tokens": (meta[0] + meta[1]) + meta[2], **full_weights(EARLY_WEIGHTS[1:], shards[3:])}
    small = {n: w[n] for n in SMALL_WEIGHTS}
    pack_grads = lambda names, grads: _pack_rows(
        [_to_shards(grads[n], BIG_SHARD_AXIS[n]) for n in names], 1).astype(BF16)
    comm = {
        "pack_weights": lambda names: _pack_rows([shard2d[n].astype(BF16) for n in names], 0),
        "unpack_weights": lambda names, g: full_weights(names, _unpack_rows(g, shapes_of(names))),
        "pack_grads": pack_grads,
        "pack_small_grads": lambda grads: _pack_flat([grads[n] for n in SMALL_WEIGHTS]),
    }

    loss, grad_x, grads, late_parts, small_parts = _local_step(x[0], loss_target[0], big, small, comm)
    loss = lax.psum(loss, ("x", "y", "c"))

    pk = lambda names, src: _pack_rows([src[n].reshape(shard2d[n].shape) for n in names], 0)
    pe, pl_ = functools.partial(pk, EARLY_WEIGHTS), functools.partial(pk, LATE_WEIGHTS)
    *late_out, early_parts = _reduce_adamw(late_parts, pl_(w), pl_(m), pl_(v), "adamw_sharded_late",
                                           exchange=(pack_grads(EARLY_WEIGHTS, grads), True))
    early_out = _reduce_adamw(early_parts, pe(w), pe(m), pe(v), "adamw_sharded_early")
    small_shapes = [w[n].shape for n in SMALL_WEIGHTS]
    pf = lambda src: _pack_flat([src[n] for n in SMALL_WEIGHTS])
    small_out = _reduce_adamw(small_parts, pf(w), pf(m), pf(v), "adamw_replicated")

    results = []
    for kind in range(4):
        big_un = dict(zip(EARLY_WEIGHTS + LATE_WEIGHTS,
                          _unpack_rows(early_out[kind], shapes_of(EARLY_WEIGHTS))
                          + _unpack_rows(late_out[kind], shapes_of(LATE_WEIGHTS))))
        small_un = dict(zip(SMALL_WEIGHTS, _unpack(small_out[kind], small_shapes)))
        for n in ALL_WEIGHTS:
            results.append(big_un[n].reshape(w[n].shape) if n in big_un else small_un[n])
    return (loss, grad_x[None], *results)
```

```python
import functools
import math

import jax
import jax.numpy as jnp
from jax import lax
from jax.experimental import pallas as pl
from jax.experimental.pallas import tpu as pltpu

F32 = jnp.float32
BF16 = jnp.bfloat16

N_DEV = 8
N_META = 16
GRID_W = 64
SSM_GROUP = 16
SSM_STATE = 64
HEAD_DIM = 64
KV_REP = 4
ROPE_THETA = 10000.0
NORM_EPS = 1e-6
EIG_RE_MAX = -1e-4
ATTN_SCALE = HEAD_DIM ** -0.5

ADAM_LR = 0.001
ADAM_B1 = 0.9
ADAM_B2 = 0.999
ADAM_EPS = 1e-08
ADAM_WD = 0.01
ADAM_STEP = 10

ROW_TILE = 384
ROW_TILE_BWD = 384
QUERY_STRIP = 256
ONE_PASS_SLACK = 60.0
VT_ROWS = 80
MASK_BIAS = -1e30
SCAN_LANES = 512
DIAG_TILE = 256
KV_TILE = 768
PACK_W = 1024
V7X_VMEM_LIMIT = 56 * 1024 * 1024
NEG_BIG = -1e30

BIG_WEIGHTS = ("meta_tokens", "w_in", "w_glu", "w_ssm_proj", "w_attn_proj", "w_out", "w_mlp_in", "w_mlp_out")
BIG_SHARD_AXIS = {"meta_tokens": 1, "w_in": 1, "w_glu": 0, "w_ssm_proj": 1, "w_attn_proj": 0, "w_out": 0,
                  "w_mlp_in": 1, "w_mlp_out": 0}
BLOCK_WEIGHTS = ("w_in", "w_mlp_in", "w_mlp_out")
EARLY_WEIGHTS = ("meta_tokens", "w_in")
MIXER_WEIGHTS = ("w_glu", "w_ssm_proj", "w_attn_proj", "w_out")
MLP_WEIGHTS = ("w_mlp_in", "w_mlp_out")
LATE_WEIGHTS = MIXER_WEIGHTS + MLP_WEIGHTS
SMALL_WEIGHTS = ("norm_mix_g", "ssm_a_re", "ssm_a_im", "ssm_log_dt", "ssm_b_re", "ssm_b_im", "ssm_c_re",
                 "ssm_c_im", "ssm_d", "b_glu", "q_norm_g", "k_norm_g", "norm_mlp_g", "norm_final_g")
ALL_WEIGHTS = ("meta_tokens", "norm_mix_g", "w_in", "ssm_a_re", "ssm_a_im", "ssm_log_dt", "ssm_b_re", "ssm_b_im",
               "ssm_c_re", "ssm_c_im", "ssm_d", "w_glu", "b_glu", "q_norm_g", "k_norm_g", "w_ssm_proj",
               "w_attn_proj", "w_out", "norm_mlp_g", "w_mlp_in", "w_mlp_out", "norm_final_g")


def _round_up(n, m):
    return (n + m - 1) // m * m


def _pcall(body, *, name, grid, in_specs, out_specs, out_shape, scratch=(), vmem=None, **kw):
    params = pltpu.CompilerParams(dimension_semantics=("arbitrary",) * len(grid), vmem_limit_bytes=vmem)
    return pl.pallas_call(body, name=name, grid=grid, in_specs=in_specs, out_specs=out_specs, out_shape=out_shape,
                          scratch_shapes=list(scratch), compiler_params=params, **kw)


def _dot(a, b):
    return jnp.dot(a, b, preferred_element_type=F32)


def _dot_nt(a, b):
    return lax.dot_general(a, b, (((1,), (1,)), ((), ())), preferred_element_type=F32)


def _dot_tn(a, b):
    return lax.dot_general(a, b, (((0,), (0,)), ((), ())), preferred_element_type=F32)


def _full_spec(shape):
    nd = len(shape)
    return pl.BlockSpec(shape, lambda *_: (0,) * nd)


def _row_spec(tm, width):
    return pl.BlockSpec((tm, width), lambda i: (i, 0))


def _heads_spec(nh, tm):
    return pl.BlockSpec((nh, tm, HEAD_DIM), lambda i: (0, i, 0))


_ANY = pl.BlockSpec(memory_space=pl.ANY)


def _load_once(step, pairs, sem):
    @pl.when(step == 0)
    def _():
        copies = [pltpu.make_async_copy(src, dst, sem.at[k]) for k, (src, dst) in enumerate(pairs)]
        for cp in copies:
            cp.start()
        for cp in copies:
            cp.wait()


def _swap_pairs(x, even):
    n = x.shape[-1]
    return jnp.where(even, pltpu.roll(x, n - 1, 1), pltpu.roll(x, 1, 1))


def _gelu(y):
    return 0.5 * y * (1.0 + lax.erf(y * (1.0 / math.sqrt(2.0))))


def _gelu_grad(y):
    return 0.5 * (1.0 + lax.erf(y * (1.0 / math.sqrt(2.0)))) + y * jnp.exp(-0.5 * y * y) * (1.0 / math.sqrt(2.0 * math.pi))


def _in_proj_fwd(x0, g_mix, w_in, qg, kg, cos, sin, n_valid, exchange=None):
    t, d = x0.shape
    tm = ROW_TILE
    du, dk = d // 2, d // 4
    nh, nkv = d // HEAD_DIM, d // HEAD_DIM // KV_REP
    bw = w_in.shape[-1]
    assert bw == du and dk * 2 == bw

    def body(*refs):
        i = pl.program_id(0)
        (x_ref, g_ref, w_hbm, qg_ref, kg_ref, c_ref, s_ref,
         h_ref, u_ref, ub_ref, qraw_ref, kraw_ref, qat_ref, ka_ref, kt_ref, va_ref, vta_ref, gates_ref,
         w_ref, sem) = _riding_exchange(refs, exchange, 7, 11, i == 0, i == t // tm - 1)
        _load_once(i, [(w_hbm, w_ref)], sem)
        x = x_ref[...]
        r = lax.rsqrt(jnp.mean(x * x, axis=-1, keepdims=True) + NORM_EPS)
        h = ((x * r) * g_ref[...]).astype(BF16)
        h_ref[...] = h
        u = _dot(h, w_ref[0])
        u_ref[...] = u
        ub_ref[...] = u.astype(BF16)
        lane = lax.broadcasted_iota(jnp.int32, (tm, 128), 1)
        lo = lane < HEAD_DIM
        even = (lane & 1) == 0
        aug = lane == HEAD_DIM
        c = c_ref[...]
        s = s_ref[...]
        row = i * tm + lax.broadcasted_iota(jnp.int32, (tm, 1), 0)
        one = jnp.where(aug, 1.0, 0.0)
        key_bias = jnp.where(jnp.logical_and(aug, row >= n_valid), MASK_BIAS, 0.0)

        def norm_rope(blk, g128):
            sq = blk * blk
            ms_lo = jnp.sum(jnp.where(lo, sq, 0.0), axis=-1, keepdims=True) * (1.0 / HEAD_DIM)
            ms_hi = jnp.sum(jnp.where(lo, 0.0, sq), axis=-1, keepdims=True) * (1.0 / HEAD_DIM)
            rr = jnp.where(lo, lax.rsqrt(ms_lo + NORM_EPS), lax.rsqrt(ms_hi + NORM_EPS))
            qn = (blk * rr) * g128
            return qn * c + _swap_pairs(qn, even) * s

        def put_heads(rows_ref, cols_ref, first, pair, extra):
            for k, head in enumerate((pair, pltpu.roll(pair, HEAD_DIM, 1))):
                wide = jnp.where(lo, head, extra)
                if rows_ref is not None:
                    rows_ref[first + k] = wide.astype(BF16)
                cols_ref[first + k] = wide.T[0:cols_ref.shape[1], :].astype(BF16)

        for blk in range(2):
            qb = _dot(h, w_ref[1 + blk])
            qraw_ref[:, bw * blk:bw * (blk + 1)] = qb
            for a in range(bw // 128):
                put_heads(None, qat_ref, (bw // HEAD_DIM) * blk + 2 * a,
                          norm_rope(qb[:, 128 * a:128 * (a + 1)], qg_ref[...]) * ATTN_SCALE, one)
        kv = _dot(h, w_ref[3])
        kraw_ref[...] = kv[:, 0:dk]
        for a in range(nkv // 2):
            put_heads(ka_ref, kt_ref, 2 * a, norm_rope(kv[:, 128 * a:128 * (a + 1)], kg_ref[...]), key_bias)
            put_heads(va_ref, vta_ref, 2 * a, kv[:, dk + 128 * a:dk + 128 * (a + 1)], one)
        for blk in range(4):
            gates_ref[:, bw * blk:bw * (blk + 1)] = _dot(h, w_ref[4 + blk])

    heads = lambda n: pl.BlockSpec((n, tm, 128), lambda i: (0, i, 0))
    heads_t = lambda n, rows: pl.BlockSpec((n, rows, tm), lambda i: (0, 0, i))
    extra = exchange is not None
    return _pcall(
        body, name="in_proj_fwd", grid=(t // tm,),
        in_specs=[_row_spec(tm, d), _full_spec((1, d)), _ANY, _full_spec((1, 128)), _full_spec((1, 128)),
                  _row_spec(tm, 128), _row_spec(tm, 128)] + [_ANY] * extra,
        out_specs=[_row_spec(tm, d), _row_spec(tm, du), _row_spec(tm, du), _row_spec(tm, d), _row_spec(tm, dk),
                   heads_t(nh, 128), heads(nkv), heads_t(nkv, HEAD_DIM), heads(nkv), heads_t(nkv, VT_ROWS),
                   _row_spec(tm, 2 * d)] + [_ANY] * extra,
        out_shape=[jax.ShapeDtypeStruct((t, d), BF16), jax.ShapeDtypeStruct((t, du), F32),
                   jax.ShapeDtypeStruct((t, du), BF16), jax.ShapeDtypeStruct((t, d), F32),
                   jax.ShapeDtypeStruct((t, dk), F32), jax.ShapeDtypeStruct((nh, 128, t), BF16),
                   jax.ShapeDtypeStruct((nkv, t, 128), BF16), jax.ShapeDtypeStruct((nkv, HEAD_DIM, t), BF16),
                   jax.ShapeDtypeStruct((nkv, t, 128), BF16), jax.ShapeDtypeStruct((nkv, VT_ROWS, t), BF16),
                   jax.ShapeDtypeStruct((t, 2 * d), F32)] + ([_exchange_out_shape(*exchange)] if extra else []),
        scratch=[pltpu.VMEM((N_DEV, d, bw), BF16), pltpu.SemaphoreType.DMA((1,))] + _EXCHANGE_SEMS * extra,
        vmem=V7X_VMEM_LIMIT,
    )(x0, g_mix, w_in, qg, kg, cos, sin, *([exchange[0]] if extra else []))


def _mixer_values(u, yf, yb, yt_attn, gates, d_ref, wg_ref, bg_ref, ps_ref, pa_ref, d):
    y = (u * d_ref[...] + yf) + yb
    z = _gelu(y)
    sg = jax.nn.sigmoid(_dot(z.astype(BF16), wg_ref[...]) + bg_ref[...])
    y_ssm = z * sg
    a_ssm = _dot(y_ssm.astype(BF16), ps_ref[...])
    a_attn = _dot_tn(yt_attn.astype(BF16), pa_ref[...])
    s_ssm = jax.nn.sigmoid(gates[:, 0:d])
    s_attn = jax.nn.sigmoid(gates[:, d:2 * d])
    merged = s_ssm * a_ssm + s_attn * a_attn
    return y, z, sg, y_ssm, a_ssm, a_attn, s_ssm, s_attn, merged


def _mixer_out_fwd(x0, u, yf, yb, y_attn, gates, ssm_d, w_glu, b_glu, p_ssm, p_attn, w_out):
    t, d = x0.shape
    tm = ROW_TILE
    du = d // 2

    def body(x_ref, u_ref, yf_ref, yb_ref, ya_ref, gt_ref, d_ref, wg_ref, bg_ref, ps_ref, pa_ref, wo_ref, x1_ref):
        vals = _mixer_values(u_ref[...], yf_ref[...], yb_ref[...], ya_ref[...], gt_ref[...],
                             d_ref, wg_ref, bg_ref, ps_ref, pa_ref, d)
        merged = vals[-1]
        x1_ref[...] = x_ref[...] + _dot(merged.astype(BF16), wo_ref[...])

    return _pcall(
        body, name="mixer_out_fwd", grid=(t // tm,),
        in_specs=[_row_spec(tm, d), _row_spec(tm, du), _row_spec(tm, du), _row_spec(tm, du),
                  pl.BlockSpec((d, tm), lambda i: (0, i)), _row_spec(tm, 2 * d), _full_spec((1, du)), _full_spec((du, du)), _full_spec((1, du)),
                  _full_spec((du, d)), _full_spec((d, d)), _full_spec((d, d))],
        out_specs=_row_spec(tm, d), out_shape=jax.ShapeDtypeStruct((t, d), F32), vmem=V7X_VMEM_LIMIT,
    )(x0, u, yf, yb, y_attn, gates, ssm_d, w_glu, b_glu, p_ssm, p_attn, w_out)


def _mlp_loss_fwd_bwd(x1, target, g_mlp, g_fin, w1, w2, n_valid):
    t, d = x1.shape
    tm = ROW_TILE_BWD
    dff = 4 * d
    nfc, _, fc = w1.shape

    def body(x_ref, tg_ref, gm_ref, gf_ref, w1_hbm, w2_hbm,
             dx1_ref, loss_ref, dgf_ref, dgm_ref, h2_ref, da_ref, hsq_ref, dx2b_ref,
             w1_ref, w2_ref, relu_ref, sem):
        i = pl.program_id(0)
        _load_once(i, [(w1_hbm, w1_ref), (w2_hbm, w2_ref)], sem)

        @pl.when(i == 0)
        def _():
            loss_ref[...] = jnp.zeros_like(loss_ref)
            dgf_ref[...] = jnp.zeros_like(dgf_ref)
            dgm_ref[...] = jnp.zeros_like(dgm_ref)

        x1v = x_ref[...]
        r1 = lax.rsqrt(jnp.mean(x1v * x1v, axis=-1, keepdims=True) + NORM_EPS)
        xh1 = x1v * r1
        h2b = (xh1 * gm_ref[...]).astype(BF16)
        h2_ref[...] = h2b
        acc = jnp.zeros((tm, d), F32)
        for c in range(nfc):
            a = jnp.maximum(_dot(h2b, w1_ref[c]), 0.0)
            relu_ref[:, fc * c:fc * (c + 1)] = a
            hs = (a * a).astype(BF16)
            hsq_ref[:, fc * c:fc * (c + 1)] = hs
            acc = acc + _dot(hs, w2_ref[c])
        x2 = x1v + acc
        r2 = lax.rsqrt(jnp.mean(x2 * x2, axis=-1, keepdims=True) + NORM_EPS)
        xh2 = x2 * r2
        out = xh2 * gf_ref[...]
        row = i * tm + lax.broadcasted_iota(jnp.int32, (tm, 1), 0)
        valid = jnp.logical_and(row >= N_META, row < n_valid)
        diff = jnp.where(valid, out - tg_ref[...], 0.0)
        loss_ref[...] += 0.5 * jnp.sum(jnp.sum(diff * diff, axis=-1, keepdims=True) * (1.0 / d))
        dout = diff * (1.0 / d)
        dgf_ref[...] += jnp.sum(dout * xh2, axis=0, keepdims=True)
        dxh2 = dout * gf_ref[...]
        dx2 = r2 * (dxh2 - xh2 * jnp.mean(dxh2 * xh2, axis=-1, keepdims=True))
        dx2b = dx2.astype(BF16)
        dx2b_ref[...] = dx2b
        dh2 = jnp.zeros((tm, d), F32)
        for c in range(nfc):
            dhs = _dot_nt(dx2b, w2_ref[c])
            da = (dhs * (2.0 * relu_ref[:, fc * c:fc * (c + 1)])).astype(BF16)
            da_ref[:, fc * c:fc * (c + 1)] = da
            dh2 = dh2 + _dot_nt(da, w1_ref[c])
        dgm_ref[...] += jnp.sum(dh2 * xh1, axis=0, keepdims=True)
        dxh1 = dh2 * gm_ref[...]
        dx1_ref[...] = dx2 + r1 * (dxh1 - xh1 * jnp.mean(dxh1 * xh1, axis=-1, keepdims=True))

    return _pcall(
        body, name="mlp_loss_fwd_bwd", grid=(t // tm,),
        in_specs=[_row_spec(tm, d), _row_spec(tm, d), _full_spec((1, d)), _full_spec((1, d)), _ANY, _ANY],
        out_specs=[_row_spec(tm, d), _full_spec((8, 128)), _full_spec((1, d)), _full_spec((1, d)),
                   _row_spec(tm, d), _row_spec(tm, dff), _row_spec(tm, dff), _row_spec(tm, d)],
        out_shape=[jax.ShapeDtypeStruct((t, d), F32), jax.ShapeDtypeStruct((8, 128), F32),
                   jax.ShapeDtypeStruct((1, d), F32), jax.ShapeDtypeStruct((1, d), F32),
                   jax.ShapeDtypeStruct((t, d), BF16), jax.ShapeDtypeStruct((t, dff), BF16),
                   jax.ShapeDtypeStruct((t, dff), BF16), jax.ShapeDtypeStruct((t, d), BF16)],
        scratch=[pltpu.VMEM((nfc, d, fc), BF16), pltpu.VMEM((nfc, fc, d), BF16), pltpu.VMEM((tm, dff), F32),
                 pltpu.SemaphoreType.DMA((2,))],
        vmem=V7X_VMEM_LIMIT,
    )(x1, target, g_mlp, g_fin, w1, w2)


def _mixer_out_bwd(dx1, u, yf, yb, yt_attn, gates, ssm_d, w_glu, b_glu, p_ssm, p_attn, w_out):
    t, d = dx1.shape
    tm = ROW_TILE_BWD
    du = d // 2

    def body(dx_ref, u_ref, yf_ref, yb_ref, yt_ref, gt_ref, d_ref, wg_ref, bg_ref, ps_ref, pa_ref, wo_ref,
             dyb_ref, dud_ref, dyat_ref, dgates_ref, zb_ref, dglb_ref, ysb_ref, dasb_ref, daab_ref,
             mgb_ref, dxb_ref, dd_ref, dbg_ref):
        i = pl.program_id(0)

        @pl.when(i == 0)
        def _():
            dd_ref[...] = jnp.zeros_like(dd_ref)
            dbg_ref[...] = jnp.zeros_like(dbg_ref)

        uv = u_ref[...]
        y, z, sg, y_ssm, a_ssm, a_attn, s_ssm, s_attn, merged = _mixer_values(
            uv, yf_ref[...], yb_ref[...], yt_ref[...], gt_ref[...], d_ref, wg_ref, bg_ref, ps_ref, pa_ref, d)
        dxb = dx_ref[...].astype(BF16)
        dxb_ref[...] = dxb
        mgb_ref[...] = merged.astype(BF16)
        dmerged = _dot_nt(dxb, wo_ref[...])
        dgates_ref[:, 0:d] = (dmerged * a_ssm * (s_ssm * (1.0 - s_ssm))).astype(BF16)
        dgates_ref[:, d:2 * d] = (dmerged * a_attn * (s_attn * (1.0 - s_attn))).astype(BF16)
        da_ssm = (dmerged * s_ssm).astype(BF16)
        da_attn = (dmerged * s_attn).astype(BF16)
        dasb_ref[...] = da_ssm
        daab_ref[...] = da_attn
        ysb_ref[...] = y_ssm.astype(BF16)
        dy_ssm = _dot_nt(da_ssm, ps_ref[...])
        dyat_ref[...] = _dot_nt(pa_ref[...], da_attn).astype(BF16)
        dgl = dy_ssm * z * (sg * (1.0 - sg))
        dglb = dgl.astype(BF16)
        dglb_ref[...] = dglb
        zb_ref[...] = z.astype(BF16)
        dbg_ref[...] += jnp.sum(dgl, axis=0, keepdims=True)
        dz = dy_ssm * sg + _dot_nt(dglb, wg_ref[...])
        dy = dz * _gelu_grad(y)
        dyb_ref[...] = dy.astype(BF16)
        dd_ref[...] += jnp.sum(dy * uv, axis=0, keepdims=True)
        dud_ref[...] = dy * d_ref[...]

    bf = lambda w: jax.ShapeDtypeStruct((t, w), BF16)
    return _pcall(
        body, name="mixer_out_bwd", grid=(t // tm,),
        in_specs=[_row_spec(tm, d), _row_spec(tm, du), _row_spec(tm, du), _row_spec(tm, du),
                  pl.BlockSpec((d, tm), lambda i: (0, i)),
                  _row_spec(tm, 2 * d), _full_spec((1, du)), _full_spec((du, du)), _full_spec((1, du)),
                  _full_spec((du, d)), _full_spec((d, d)), _full_spec((d, d))],
        out_specs=[_row_spec(tm, du), _row_spec(tm, du), pl.BlockSpec((d, tm), lambda i: (0, i)),
                   _row_spec(tm, 2 * d), _row_spec(tm, du), _row_spec(tm, du), _row_spec(tm, du), _row_spec(tm, d),
                   _row_spec(tm, d), _row_spec(tm, d), _row_spec(tm, d), _full_spec((1, du)), _full_spec((1, du))],
        out_shape=[bf(du), jax.ShapeDtypeStruct((t, du), F32), jax.ShapeDtypeStruct((d, t), BF16), bf(2 * d),
                   bf(du), bf(du), bf(du), bf(d), bf(d), bf(d), bf(d),
                   jax.ShapeDtypeStruct((1, du), F32), jax.ShapeDtypeStruct((1, du), F32)],
        vmem=V7X_VMEM_LIMIT,
    )(dx1, u, yf, yb, yt_attn, gates, ssm_d, w_glu, b_glu, p_ssm, p_attn, w_out)


def _in_proj_bwd(x0, dx1, dud, duf, dub, qraw, kraw, dq, dk, dv, dgates, g_mix, w_in, qg, kg, cos, sin):
    t, d = x0.shape
    tm = ROW_TILE_BWD
    du, dkw = d // 2, d // 4
    nh, nkv = d // HEAD_DIM, d // HEAD_DIM // KV_REP
    bw = w_in.shape[-1]
    o_q, o_k, o_v, o_g = du, du + d, du + d + dkw, 2 * d

    def body(x_ref, dx1_ref, dud_ref, duf_ref, dub_ref, qraw_ref, kraw_ref, dq_ref, dk_ref, dv_ref, dgt_ref,
             g_ref, w_hbm, qg_ref, kg_ref, c_ref, s_ref,
             dx0_ref, dproj_ref, dgm_ref, dqg_ref, dkg_ref,
             w_ref, kv_ref, sem):
        i = pl.program_id(0)
        _load_once(i, [(w_hbm, w_ref)], sem)

        @pl.when(i == 0)
        def _():
            dgm_ref[...] = jnp.zeros_like(dgm_ref)
            dqg_ref[...] = jnp.zeros_like(dqg_ref)
            dkg_ref[...] = jnp.zeros_like(dkg_ref)

        lane = lax.broadcasted_iota(jnp.int32, (tm, 128), 1)
        lo = lane < HEAD_DIM
        even = (lane & 1) == 0
        c = c_ref[...]
        s = s_ref[...]

        def norm_rope_bwd(dout, raw, g128):
            sq = raw * raw
            ms_lo = jnp.sum(jnp.where(lo, sq, 0.0), axis=-1, keepdims=True) * (1.0 / HEAD_DIM)
            ms_hi = jnp.sum(jnp.where(lo, 0.0, sq), axis=-1, keepdims=True) * (1.0 / HEAD_DIM)
            rr = jnp.where(lo, lax.rsqrt(ms_lo + NORM_EPS), lax.rsqrt(ms_hi + NORM_EPS))
            xh = raw * rr
            dqn = dout * c + _swap_pairs(dout * s, even)
            dg = jnp.sum(dqn * xh, axis=0, keepdims=True)
            tt = dqn * g128
            pr = tt * xh
            mu_lo = jnp.sum(jnp.where(lo, pr, 0.0), axis=-1, keepdims=True) * (1.0 / HEAD_DIM)
            mu_hi = jnp.sum(jnp.where(lo, 0.0, pr), axis=-1, keepdims=True) * (1.0 / HEAD_DIM)
            return rr * (tt - xh * jnp.where(lo, mu_lo, mu_hi)), dg

        dub_tot = (dud_ref[...] + duf_ref[...]) + dub_ref[...]
        dproj_ref[:, 0:du] = dub_tot.astype(BF16)
        dqg = jnp.zeros((1, 128), F32)
        for a in range(nh // 2):
            sl = slice(128 * a, 128 * (a + 1))
            draw, dg = norm_rope_bwd(dq_ref[sl, :].T * ATTN_SCALE, qraw_ref[:, sl], qg_ref[...])
            dqg = dqg + dg
            dproj_ref[:, o_q + 128 * a:o_q + 128 * (a + 1)] = draw.astype(BF16)
        dqg_ref[...] += dqg
        for hh in range(nkv):
            kv_ref[:, HEAD_DIM * hh:HEAD_DIM * (hh + 1)] = dk_ref[hh, :, 0:HEAD_DIM]
        dkg = jnp.zeros((1, 128), F32)
        for a in range(nkv // 2):
            sl = slice(128 * a, 128 * (a + 1))
            draw, dg = norm_rope_bwd(kv_ref[:, sl], kraw_ref[:, sl], kg_ref[...])
            dkg = dkg + dg
            dproj_ref[:, o_k + 128 * a:o_k + 128 * (a + 1)] = draw.astype(BF16)
        dkg_ref[...] += dkg
        for hh in range(nkv):
            kv_ref[:, HEAD_DIM * hh:HEAD_DIM * (hh + 1)] = dv_ref[hh]
        dproj_ref[:, o_v:o_g] = kv_ref[...].astype(BF16)
        dproj_ref[:, o_g:4 * d] = dgt_ref[...]
        dh = jnp.zeros((tm, d), F32)
        for blk in range(N_DEV):
            dh = dh + _dot_nt(dproj_ref[:, bw * blk:bw * (blk + 1)], w_ref[blk])
        x = x_ref[...]
        r = lax.rsqrt(jnp.mean(x * x, axis=-1, keepdims=True) + NORM_EPS)
        xh0 = x * r
        dgm_ref[...] += jnp.sum(dh * xh0, axis=0, keepdims=True)
        dxh = dh * g_ref[...]
        dx0_ref[...] = dx1_ref[...] + r * (dxh - xh0 * jnp.mean(dxh * xh0, axis=-1, keepdims=True))

    return _pcall(
        body, name="in_proj_bwd", grid=(t // tm,),
        in_specs=[_row_spec(tm, d), _row_spec(tm, d), _row_spec(tm, du), _row_spec(tm, du), _row_spec(tm, du),
                  _row_spec(tm, d), _row_spec(tm, dkw), pl.BlockSpec((d, tm), lambda i: (0, i)),
                  pl.BlockSpec((nkv, tm, 128), lambda i: (0, i, 0)), _heads_spec(nkv, tm),
                  _row_spec(tm, 2 * d), _full_spec((1, d)), _ANY, _full_spec((1, 128)), _full_spec((1, 128)),
                  _row_spec(tm, 128), _row_spec(tm, 128)],
        out_specs=[_row_spec(tm, d), _row_spec(tm, 4 * d), _full_spec((1, d)), _full_spec((1, 128)),
                   _full_spec((1, 128))],
        out_shape=[jax.ShapeDtypeStruct((t, d), F32), jax.ShapeDtypeStruct((t, 4 * d), BF16),
                   jax.ShapeDtypeStruct((1, d), F32), jax.ShapeDtypeStruct((1, 128), F32),
                   jax.ShapeDtypeStruct((1, 128), F32)],
        scratch=[pltpu.VMEM((N_DEV, d, bw), BF16), pltpu.VMEM((tm, dkw), F32), pltpu.SemaphoreType.DMA((1,))],
        vmem=V7X_VMEM_LIMIT,
    )(x0, dx1, dud, duf, dub, qraw, kraw, dq, dk, dv, dgates, g_mix, w_in, qg, kg, cos, sin)


def _attn_fwd(qat, ka, vta):
    nh, _, t = qat.shape
    nkv = ka.shape[0]
    rep = nh // nkv
    hd = HEAD_DIM
    vr = vta.shape[1]
    tq = tk = KV_TILE

    def body(qt_ref, k_ref, vt_ref, ot_ref, lse_ref, m_scr, acc_scr, excess_scr):
        j = pl.program_id(2)
        src = j % 2
        dst = 1 - src

        @pl.when(j == 0)
        def _():
            m_scr[0] = jnp.full(m_scr.shape[1:], NEG_BIG, F32)
            acc_scr[0] = jnp.zeros(acc_scr.shape[1:], F32)
            excess_scr[...] = jnp.full(excess_scr.shape, -NEG_BIG, F32)

        k = k_ref[0]
        vt = vt_ref[0]
        strips = [(r, c) for r in range(rep) for c in range(0, tq, QUERY_STRIP)]
        scores = lambda r, c: _dot(k, qt_ref[r, :, c:c + QUERY_STRIP])

        def sweep(one_pass):
            def add_values(r, cols, before, after, pt):
                acc = acc_scr[src, r, :, cols]
                acc_scr[dst, r, :, cols] = after * ((acc if before is None else before * acc) + _dot(vt, pt))

            ahead = [scores(*strips[0]), scores(*strips[1])]
            pending = None
            excess = jnp.full((1, QUERY_STRIP), NEG_BIG, F32)
            for n, (r, c) in enumerate(strips):
                st = ahead.pop(0)
                if n + 2 < len(strips):
                    ahead.append(scores(*strips[n + 2]))
                cols = slice(c, c + QUERY_STRIP)
                m_prev = m_scr[src, r, :, cols]
                if one_pass:
                    pt = jnp.exp(st - m_prev).astype(BF16)
                    tile_max = jnp.max(st, axis=0, keepdims=True)
                    m_next = jnp.maximum(m_prev, tile_max)
                    excess = jnp.maximum(excess, tile_max - m_prev)
                    factors = (None, jnp.exp(m_prev - m_next))
                else:
                    m_next = jnp.maximum(m_prev, jnp.max(st, axis=0, keepdims=True))
                    pt = jnp.exp(st - m_next).astype(BF16)
                    factors = (jnp.exp(m_prev - m_next), 1.0)
                m_scr[dst, r, :, cols] = m_next
                if pending is not None:
                    add_values(*pending)
                pending = (r, cols, *factors, pt)
            add_values(*pending)
            return excess

        @pl.when(j > 0)
        def _():
            excess_scr[...] = sweep(one_pass=True)

        @pl.when(jnp.max(excess_scr[...]) > ONE_PASS_SLACK)
        def _():
            sweep(one_pass=False)

        @pl.when(j == pl.num_programs(2) - 1)
        def _():
            for r in range(rep):
                l = acc_scr[dst, r, hd:hd + 1, :]
                ot_ref[hd * r:hd * (r + 1), :] = acc_scr[dst, r, 0:hd, :] / l
                lse_ref[0, r:r + 1, :] = m_scr[dst, r] + jnp.log(l)

    return _pcall(
        body, name="attn_fwd", grid=(nkv, t // tq, t // tk),
        in_specs=[pl.BlockSpec((rep, 128, tq), lambda g, i, j: (g, 0, i)),
                  pl.BlockSpec((1, tk, 128), lambda g, i, j: (g, j, 0)),
                  pl.BlockSpec((1, vr, tk), lambda g, i, j: (g, 0, j))],
        out_specs=[pl.BlockSpec((rep * hd, tq), lambda g, i, j: (g, i)),
                   pl.BlockSpec((1, rep, tq), lambda g, i, j: (g, 0, i))],
        out_shape=[jax.ShapeDtypeStruct((nh * hd, t), F32), jax.ShapeDtypeStruct((nkv, rep, t), F32)],
        scratch=[pltpu.VMEM((2, rep, 1, tq), F32), pltpu.VMEM((2, rep, vr, tq), F32),
                 pltpu.VMEM((1, QUERY_STRIP), F32)],
        vmem=V7X_VMEM_LIMIT,
    )(qat, ka, vta)


def _attn_bwd(qat, ka, kt, va, dot, ot, lse_row):
    nh, _, t = qat.shape
    nkv = ka.shape[0]
    rep = nh // nkv
    hd = HEAD_DIM
    tq = tk = KV_TILE

    def body(qt_ref, k_ref, kt_ref, v_ref, dot_ref, ot_ref, lse_ref, dk_ref, dv_ref, dqt_ref):
        j = pl.program_id(1)
        i = pl.program_id(2)

        @pl.when(jnp.logical_and(j == 0, i == 0))
        def _():
            dqt_ref[...] = jnp.zeros_like(dqt_ref)

        @pl.when(i == 0)
        def _():
            dk_ref[...] = jnp.zeros_like(dk_ref)
            dv_ref[...] = jnp.zeros_like(dv_ref)

        k = k_ref[0]
        kt = kt_ref[0]
        v = v_ref[0, :, 0:hd]
        cols = pl.ds(pl.multiple_of(i * tq, tq), tq)
        dk = jnp.zeros((tk, 128), F32)
        dv = jnp.zeros((tk, hd), F32)
        products = lambda r: (_dot(k, qt_ref[r]), _dot(v, dot_ref[hd * r:hd * (r + 1), :]))
        nxt = products(0)
        for r in range(rep):
            st, dpt = nxt
            if r + 1 < rep:
                nxt = products(r + 1)
            heads = slice(hd * r, hd * (r + 1))
            qt = qt_ref[r]
            dot_r = dot_ref[heads, :]
            delta = jnp.sum(dot_r.astype(F32) * ot_ref[heads, :], axis=0, keepdims=True)
            pt = jnp.exp(st - lse_ref[0, r:r + 1, :])
            dst = (pt * (dpt - delta)).astype(BF16)
            dv = dv + _dot_nt(pt.astype(BF16), dot_r)
            dk = dk + _dot_nt(dst, qt)
            dqt_ref[heads, cols] += _dot(kt, dst)
        dk_ref[0] += dk
        dv_ref[0] += dv

    return _pcall(
        body, name="attn_bwd", grid=(nkv, t // tk, t // tq),
        in_specs=[pl.BlockSpec((rep, 128, tq), lambda g, j, i: (g, 0, i)),
                  pl.BlockSpec((1, tk, 128), lambda g, j, i: (g, j, 0)),
                  pl.BlockSpec((1, hd, tk), lambda g, j, i: (g, 0, j)),
                  pl.BlockSpec((1, tk, 128), lambda g, j, i: (g, j, 0)),
                  pl.BlockSpec((rep * hd, tq), lambda g, j, i: (g, i)),
                  pl.BlockSpec((rep * hd, tq), lambda g, j, i: (g, i)),
                  pl.BlockSpec((1, rep, tq), lambda g, j, i: (g, 0, i))],
        out_specs=[pl.BlockSpec((1, tk, 128), lambda g, j, i: (g, j, 0)),
                   pl.BlockSpec((1, tk, hd), lambda g, j, i: (g, j, 0)),
                   pl.BlockSpec((rep * hd, t), lambda g, j, i: (g, 0))],
        out_shape=[jax.ShapeDtypeStruct((nkv, t, 128), F32), jax.ShapeDtypeStruct((nkv, t, hd), F32),
                   jax.ShapeDtypeStruct((nh * hd, t), F32)],
        vmem=V7X_VMEM_LIMIT,
    )(qat, ka, kt, va, dot, ot, lse_row)


def _riding_exchange(refs, exchange, n_in, n_out, first_step, last_step):
    if exchange is None:
        return refs
    x_ref, out_ref = refs[n_in], refs[n_in + 1 + n_out]
    sems = refs[-3:]

    @pl.when(first_step)
    def _():
        _start_all(*_exchange_copies(x_ref, out_ref, *sems, exchange[1]))

    @pl.when(last_step)
    def _():
        _wait_all(*_exchange_copies(x_ref, out_ref, *sems, exchange[1]))

    return refs[:n_in] + refs[n_in + 1:n_in + 1 + n_out] + refs[n_in + 2 + n_out:-3]


def _segmented_scan(src_re, src_im, dst_re, dst_im, lam_re, lam_im, pow_re, pow_im, carry_re, carry_im,
                    end_re, end_im, in_re, in_im, lanes, descending, conj):
    tc = src_re.shape[0]
    seg = tc // 8
    width = lanes.size
    sign = -1.0 if conj else 1.0
    rows_of = lambda q: pl.ds(8 * (seg - 1 - q if descending else q), 8)
    lr = jnp.broadcast_to(lam_re[:, lanes], (8, width))
    li = jnp.broadcast_to(sign * lam_im[:, lanes], (8, width))
    xr = jnp.zeros((8, width), F32)
    xi = jnp.zeros((8, width), F32)
    for q in range(seg):
        rows = rows_of(q)
        xr, xi = (lr * xr - li * xi) + src_re[rows, lanes], (lr * xi + li * xr) + src_im[rows, lanes]
        dst_re[rows, lanes] = xr
        dst_im[rows, lanes] = xi
    end_re[:, lanes] = xr
    end_im[:, lanes] = xi
    sr = pow_re[seg - 1:seg, lanes]
    si = sign * pow_im[seg - 1:seg, lanes]
    cr = carry_re[:, lanes]
    ci = carry_im[:, lanes]
    for s in range(8):
        se = 7 - s if descending else s
        in_re[se:se + 1, lanes] = cr
        in_im[se:se + 1, lanes] = ci
        cr, ci = (end_re[se:se + 1, lanes] + (sr * cr - si * ci)), (end_im[se:se + 1, lanes] + (sr * ci + si * cr))
    carry_re[:, lanes] = cr
    carry_im[:, lanes] = ci
    ir = in_re[:, lanes]
    ii = in_im[:, lanes]
    for q in range(seg):
        rows = rows_of(q)
        pr = pow_re[q:q + 1, lanes]
        pi = sign * pow_im[q:q + 1, lanes]
        dst_re[rows, lanes] = dst_re[rows, lanes] + (pr * ir - pi * ii)
        dst_im[rows, lanes] = dst_im[rows, lanes] + (pr * ii + pi * ir)


def _diag_tiles(gn):
    rows_per_tile = DIAG_TILE // (SSM_STATE // SSM_GROUP)
    return [(slice(rows_per_tile * j, rows_per_tile * (j + 1)), slice(DIAG_TILE * j, DIAG_TILE * (j + 1)))
            for j in range(gn // DIAG_TILE)]


def _ssm_scan_fwd(ub, lam_re, lam_im, pow_re, pow_im, bb_re, bb_im, cc_re, cc_im, exchange=None):
    t, w = ub.shape
    gn = lam_re.shape[-1]
    tc = ROW_TILE
    cl = min(gn, SCAN_LANES)
    nblk = t // tc
    tiles = _diag_tiles(gn)

    def body(*refs):
        first = jnp.logical_and(pl.program_id(0) == 0, pl.program_id(1) == 0)
        last = jnp.logical_and(pl.program_id(0) == 1, pl.program_id(1) == nblk - 1)
        (u_ref, lr_ref, li_ref, pr_ref, pi_ref, br_ref, bi_ref, cr_ref, ci_ref, y_ref, xr_ref, xi_ref,
         bur_scr, bui_scr, cr_scr, ci_scr, er_scr, ei_scr, nr_scr, ni_scr) = _riding_exchange(
             refs, exchange, 9, 3, first, last)

        @pl.when(pl.program_id(1) == 0)
        def _():
            cr_scr[...] = jnp.zeros_like(cr_scr)
            ci_scr[...] = jnp.zeros_like(ci_scr)

        for rows, lanes in tiles:
            u_j = u_ref[:, rows]
            bur_scr[:, lanes] = _dot(u_j, br_ref[0, rows, lanes])
            bui_scr[:, lanes] = _dot(u_j, bi_ref[0, rows, lanes])
        for descending in (False, True):
            @pl.when(pl.program_id(0) == int(descending))
            def _(descending=descending):
                for c0 in range(0, gn, cl):
                    _segmented_scan(bur_scr, bui_scr, xr_ref.at[0], xi_ref.at[0], lr_ref.at[0], li_ref.at[0],
                                    pr_ref.at[0], pi_ref.at[0], cr_scr, ci_scr, er_scr, ei_scr, nr_scr, ni_scr,
                                    pl.ds(c0, cl), descending, conj=False)
        for rows, lanes in tiles:
            y_ref[0, :, rows] = (_dot(xr_ref[0, :, lanes].astype(BF16), cr_ref[0, lanes, rows])
                                 - _dot(xi_ref[0, :, lanes].astype(BF16), ci_ref[0, lanes, rows]))

    blk = lambda dd, i: jnp.where(dd == 0, i, nblk - 1 - i)
    row = lambda width: pl.BlockSpec((1, tc, width), lambda dd, i: (dd, blk(dd, i), 0))
    per_dir = lambda a, b: pl.BlockSpec((1, a, b), lambda dd, i: (dd, 0, 0))
    extra = exchange is not None
    return _pcall(
        body, name="ssm_scan_fwd", grid=(2, nblk),
        in_specs=[pl.BlockSpec((tc, w), lambda dd, i: (blk(dd, i), 0)), per_dir(1, gn), per_dir(1, gn),
                  per_dir(tc // 8, gn), per_dir(tc // 8, gn),
                  per_dir(w, gn), per_dir(w, gn), per_dir(gn, w), per_dir(gn, w)] + [_ANY] * extra,
        out_specs=[row(w), row(gn), row(gn)] + [_ANY] * extra,
        out_shape=[jax.ShapeDtypeStruct((2, t, w), F32), jax.ShapeDtypeStruct((2, t, gn), F32),
                   jax.ShapeDtypeStruct((2, t, gn), F32)] + ([_exchange_out_shape(*exchange)] if extra else []),
        scratch=[pltpu.VMEM((tc, gn), F32), pltpu.VMEM((tc, gn), F32), pltpu.VMEM((1, gn), F32),
                 pltpu.VMEM((1, gn), F32)] + [pltpu.VMEM((8, gn), F32)] * 4 + _EXCHANGE_SEMS * extra,
        vmem=V7X_VMEM_LIMIT,
    )(ub, lam_re, lam_im, pow_re, pow_im, bb_re, bb_im, cc_re, cc_im, *([exchange[0]] if extra else []))


def _ssm_scan_bwd(dyb, ub, xs_re, xs_im, lam_re, lam_im, pow_re, pow_im, cct_re, cct_im, bbt_re, bbt_im,
                  exchange=None):
    t, w = dyb.shape
    gn = lam_re.shape[-1]
    tc = ROW_TILE
    cl = min(gn, SCAN_LANES)
    nblk = t // tc
    tiles = _diag_tiles(gn)

    def body(*refs):
        i = pl.program_id(1)
        first = jnp.logical_and(pl.program_id(0) == 0, i == 0)
        last = jnp.logical_and(pl.program_id(0) == 1, i == nblk - 1)
        (dy_ref, u_ref, xr_ref, xi_ref, hr_ref, hi_ref, lr_ref, li_ref, pr_ref, pi_ref, ctr_ref, cti_ref, btr_ref,
         bti_ref, du_ref, dlr_ref, dli_ref, dbr_ref, dbi_ref, dcr_ref, dci_ref,
         gxr_scr, gxi_scr, cr_scr, ci_scr, ar_scr, ai_scr, er_scr, ei_scr, nr_scr, ni_scr) = _riding_exchange(
             refs, exchange, 14, 7, first, last)

        @pl.when(i == 0)
        def _():
            for ref in (cr_scr, ci_scr, ar_scr, ai_scr, dbr_ref, dbi_ref, dcr_ref, dci_ref):
                ref[...] = jnp.zeros_like(ref)

        for rows, lanes in tiles:
            dy_j = dy_ref[:, rows]
            gxr_scr[:, lanes] = _dot(dy_j, ctr_ref[0, rows, lanes])
            gxi_scr[:, lanes] = -_dot(dy_j, cti_ref[0, rows, lanes])
        first_block = i == nblk - 1
        sublane = lax.broadcasted_iota(jnp.int32, (8, 1), 0)

        def lam_gradient(state_descending):
            for c0 in range(0, gn, 512):
                lanes = pl.ds(c0, 512)
                if state_descending:
                    cur, prev, edge, src = pl.ds(0, tc - 8), pl.ds(8, tc - 8), pl.ds(tc - 8, 8), pl.ds(0, 8)
                    halo_at, halo_row, shift = 7, 0, 7
                else:
                    cur, prev, edge, src = pl.ds(8, tc - 8), pl.ds(0, tc - 8), pl.ds(0, 8), pl.ds(tc - 8, 8)
                    halo_at, halo_row, shift = 0, 7, 1
                halo_r = jnp.where(first_block, 0.0, hr_ref[0, halo_row:halo_row + 1, lanes])
                halo_i = jnp.where(first_block, 0.0, hi_ref[0, halo_row:halo_row + 1, lanes])
                xer = jnp.where(sublane == halo_at, halo_r, pltpu.roll(xr_ref[0, src, lanes], shift, 0))
                xei = jnp.where(sublane == halo_at, halo_i, pltpu.roll(xi_ref[0, src, lanes], shift, 0))
                gr, gi = gxr_scr[cur, lanes], gxi_scr[cur, lanes]
                xpr, xpi = xr_ref[0, prev, lanes], xi_ref[0, prev, lanes]
                ger, gei = gxr_scr[edge, lanes], gxi_scr[edge, lanes]
                ar_scr[:, lanes] += (jnp.sum(gr * xpr + gi * xpi, axis=0, keepdims=True)
                                     + jnp.sum(ger * xer + gei * xei, axis=0, keepdims=True))
                ai_scr[:, lanes] += (jnp.sum(gi * xpr - gr * xpi, axis=0, keepdims=True)
                                     + jnp.sum(gei * xer - ger * xei, axis=0, keepdims=True))

        for descending in (True, False):
            @pl.when(pl.program_id(0) == int(not descending))
            def _(descending=descending):
                for c0 in range(0, gn, cl):
                    _segmented_scan(gxr_scr, gxi_scr, gxr_scr, gxi_scr, lr_ref.at[0], li_ref.at[0], pr_ref.at[0],
                                    pi_ref.at[0], cr_scr, ci_scr, er_scr, ei_scr, nr_scr, ni_scr, pl.ds(c0, cl),
                                    descending, conj=True)
                lam_gradient(state_descending=not descending)
        dlr_ref[0] = ar_scr[...]
        dli_ref[0] = ai_scr[...]
        for rows, lanes in tiles:
            grb = gxr_scr[:, lanes].astype(BF16)
            gib = gxi_scr[:, lanes].astype(BF16)
            du_ref[0, :, rows] = _dot(grb, btr_ref[0, lanes, rows]) + _dot(gib, bti_ref[0, lanes, rows])
            u_j = u_ref[:, rows]
            dy_j = dy_ref[:, rows]
            dbr_ref[0, rows, :] += _dot_tn(u_j, grb)
            dbi_ref[0, rows, :] += _dot_tn(u_j, gib)
            dcr_ref[0, rows, :] += _dot_tn(dy_j, xr_ref[0, :, lanes].astype(BF16))
            dci_ref[0, rows, :] -= _dot_tn(dy_j, xi_ref[0, :, lanes].astype(BF16))

    blk = lambda dd, i: jnp.where(dd == 0, nblk - 1 - i, i)
    rev = lambda width: pl.BlockSpec((1, tc, width), lambda dd, i: (dd, blk(dd, i), 0))
    halo_blk = lambda dd, i: jnp.where(dd == 0, jnp.maximum(blk(dd, i) * (tc // 8) - 1, 0),
                                       jnp.minimum((blk(dd, i) + 1) * (tc // 8), t // 8 - 1))
    halo = pl.BlockSpec((1, 8, gn), lambda dd, i: (dd, halo_blk(dd, i), 0))
    per_dir = lambda a, b: pl.BlockSpec((1, a, b), lambda dd, i: (dd, 0, 0))
    extra = exchange is not None
    return _pcall(
        body, name="ssm_scan_bwd", grid=(2, nblk),
        in_specs=[pl.BlockSpec((tc, w), lambda dd, i: (blk(dd, i), 0)),
                  pl.BlockSpec((tc, w), lambda dd, i: (blk(dd, i), 0)), rev(gn), rev(gn), halo, halo,
                  per_dir(1, gn), per_dir(1, gn), per_dir(tc // 8, gn), per_dir(tc // 8, gn),
                  per_dir(w, gn), per_dir(w, gn), per_dir(gn, w), per_dir(gn, w)]
        + [_ANY] * extra,
        out_specs=[rev(w), per_dir(1, gn), per_dir(1, gn)] + [per_dir(w, DIAG_TILE)] * 4 + [_ANY] * extra,
        out_shape=[jax.ShapeDtypeStruct((2, t, w), F32), jax.ShapeDtypeStruct((2, 1, gn), F32),
                   jax.ShapeDtypeStruct((2, 1, gn), F32)] + [jax.ShapeDtypeStruct((2, w, DIAG_TILE), F32)] * 4
        + ([_exchange_out_shape(*exchange)] if extra else []),
        scratch=[pltpu.VMEM((tc, gn), F32), pltpu.VMEM((tc, gn), F32)] + [pltpu.VMEM((1, gn), F32)] * 4
        + [pltpu.VMEM((8, gn), F32)] * 4 + _EXCHANGE_SEMS * extra,
        vmem=V7X_VMEM_LIMIT,
    )(dyb, ub, xs_re, xs_im, xs_re, xs_im, lam_re, lam_im, pow_re, pow_im, cct_re, cct_im, bbt_re, bbt_im,
      *([exchange[0]] if extra else []))


def _matmul_tn(a, b, name, a_is_transposed=False, exchange=None):
    t, n = b.shape
    m = a.shape[0] if a_is_transposed else a.shape[1]
    bm, bn, tk = min(m, 1024), min(n, 1024), KV_TILE
    grid = (m // bm, n // bn, t // tk)

    def body(*refs):
        at = lambda step: functools.reduce(jnp.logical_and, [pl.program_id(ax) == step[ax] for ax in range(3)])
        a_ref, b_ref, o_ref = _riding_exchange(refs, exchange, 2, 1, at((0, 0, 0)), at([g - 1 for g in grid]))

        @pl.when(pl.program_id(2) == 0)
        def _():
            o_ref[...] = jnp.zeros_like(o_ref)

        mul = _dot if a_is_transposed else _dot_tn
        o_ref[...] += mul(a_ref[...].astype(BF16), b_ref[...].astype(BF16))

    a_spec = (pl.BlockSpec((bm, tk), lambda i, j, k: (i, k)) if a_is_transposed else
              pl.BlockSpec((tk, bm), lambda i, j, k: (k, i)))
    extra = exchange is not None
    out = _pcall(
        body, name=name, grid=grid,
        in_specs=[a_spec, pl.BlockSpec((tk, bn), lambda i, j, k: (k, j))] + [_ANY] * extra,
        out_specs=[pl.BlockSpec((bm, bn), lambda i, j, k: (i, j))] + [_ANY] * extra,
        out_shape=[jax.ShapeDtypeStruct((m, n), F32)] + ([_exchange_out_shape(*exchange)] if extra else []),
        scratch=_EXCHANGE_SEMS * extra, vmem=V7X_VMEM_LIMIT,
    )(a, b, *([exchange[0]] if extra else []))
    return out if extra else out[0]


def _reduce_adamw(gparts, p, m, v, name, exchange=None):
    rows, width = p.shape
    tr = max(k for k in range(16, 513, 16) if rows % k == 0)

    def body(*refs):
        i = pl.program_id(0)
        g_ref, p_ref, m_ref, v_ref, go_ref, d_ref, mo_ref, vo_ref = _riding_exchange(
            refs, exchange, 4, 4, i == 0, i == rows // tr - 1)
        g = g_ref[0].astype(F32)
        for k in range(1, N_DEV):
            g = g + g_ref[k].astype(F32)
        go_ref[...] = g
        mm = ADAM_B1 * m_ref[...] + (1.0 - ADAM_B1) * g
        vv = ADAM_B2 * v_ref[...] + (1.0 - ADAM_B2) * (g * g)
        m_hat = mm / (1.0 - ADAM_B1 ** ADAM_STEP)
        v_hat = vv / (1.0 - ADAM_B2 ** ADAM_STEP)
        d_ref[...] = -ADAM_LR * (m_hat / (jnp.sqrt(v_hat) + ADAM_EPS) + ADAM_WD * p_ref[...])
        mo_ref[...] = mm
        vo_ref[...] = vv

    spec = pl.BlockSpec((tr, width), lambda i: (i, 0))
    out = jax.ShapeDtypeStruct((rows, width), F32)
    extra = exchange is not None
    return _pcall(
        body, name=name, grid=(rows // tr,),
        in_specs=[pl.BlockSpec((N_DEV, tr, width), lambda i: (0, i, 0)), spec, spec, spec] + [_ANY] * extra,
        out_specs=[spec, spec, spec, spec] + [_ANY] * extra,
        out_shape=[out, out, out, out] + ([_exchange_out_shape(*exchange)] if extra else []),
        scratch=_EXCHANGE_SEMS * extra, vmem=V7X_VMEM_LIMIT,
    )(gparts, p, m, v, *([exchange[0]] if extra else []))


def _peer(k):
    x, y, c = lax.axis_index("x"), lax.axis_index("y"), lax.axis_index("c")
    return (x ^ ((k >> 2) & 1), y ^ ((k >> 1) & 1), c ^ (k & 1))


def _my_index():
    return 4 * lax.axis_index("x") + 2 * lax.axis_index("y") + lax.axis_index("c")


def _exchange_copies(x_ref, out_ref, send_sems, recv_sems, local_sem, scatter, first_sem=0):
    me = _my_index()
    local = pltpu.make_async_copy(x_ref.at[me] if scatter else x_ref, out_ref.at[me], local_sem)
    copies = []
    for k in range(1, N_DEV):
        peer = _peer(k)
        src = x_ref.at[4 * peer[0] + 2 * peer[1] + peer[2]] if scatter else x_ref
        copies.append(pltpu.make_async_remote_copy(
            src_ref=src, dst_ref=out_ref.at[me], send_sem=send_sems.at[first_sem + k - 1],
            recv_sem=recv_sems.at[first_sem + k - 1], device_id=peer, device_id_type=pl.DeviceIdType.MESH))
    return local, copies


def _start_all(local, copies):
    local.start()
    for cp in copies:
        cp.start()


def _wait_all(local, copies):
    for cp in copies:
        cp.wait_recv()
    for cp in copies:
        cp.wait_send()
    local.wait()


def _exchange_out_shape(x, scatter):
    return jax.ShapeDtypeStruct((N_DEV,) + tuple(x.shape[1:] if scatter else x.shape), x.dtype)


_EXCHANGE_SEMS = [pltpu.SemaphoreType.DMA((N_DEV - 1,)), pltpu.SemaphoreType.DMA((N_DEV - 1,)),
                  pltpu.SemaphoreType.DMA(())]


def _exchange(ops, name):
    n = len(ops)

    def body(*refs):
        x_refs, out_refs = refs[:n], refs[n:2 * n]
        send_sems, recv_sems, local_sems = refs[2 * n:]
        started = []
        for q, (_, scatter) in enumerate(ops):
            local, copies = _exchange_copies(x_refs[q], out_refs[q], send_sems, recv_sems, local_sems.at[q],
                                             scatter, first_sem=q * (N_DEV - 1))
            _start_all(local, copies)
            started.append((local, copies))
        for local, copies in started:
            _wait_all(local, copies)

    return pl.pallas_call(
        body, name=name, in_specs=[_ANY] * n, out_specs=[_ANY] * n,
        out_shape=[_exchange_out_shape(x, scatter) for x, scatter in ops],
        scratch_shapes=[pltpu.SemaphoreType.DMA((n * (N_DEV - 1),)), pltpu.SemaphoreType.DMA((n * (N_DEV - 1),)),
                        pltpu.SemaphoreType.DMA((n,))],
    )(*[x for x, _ in ops])


def _to_shards(full, axis):
    r, c = full.shape
    if axis == 0:
        return full.reshape(N_DEV, r // N_DEV, c)
    return full.reshape(r, N_DEV, c // N_DEV).transpose(1, 0, 2)


def _from_shards(shards, axis):
    _, r, c = shards.shape
    if axis == 0:
        return shards.reshape(N_DEV * r, c)
    return shards.transpose(1, 0, 2).reshape(r, N_DEV * c)


def _pack_rows(parts, lead):
    flat = []
    for p in parts:
        p = p.reshape(p.shape[:lead] + (-1, PACK_W))
        pad = _round_up(p.shape[lead], 16) - p.shape[lead]
        flat.append(jnp.pad(p, [(0, 0)] * lead + [(0, pad), (0, 0)]) if pad else p)
    return jnp.concatenate(flat, axis=lead)


def _unpack_rows(packed, shapes):
    lead = packed.shape[:-2]
    out, off = [], 0
    for shp in shapes:
        rows = math.prod(shp) // PACK_W
        out.append(packed[..., off:off + rows, :].reshape(lead + tuple(shp)))
        off += _round_up(rows, 16)
    return out


def _pack_flat(parts):
    flat = jnp.concatenate([p.reshape(-1) for p in parts])
    n = flat.shape[0]
    flat = jnp.pad(flat, (0, _round_up(n, 16 * PACK_W) - n))
    return flat.reshape(-1, PACK_W)


def _unpack(packed, shapes):
    flat = packed.reshape(-1)
    out, off = [], 0
    for shp in shapes:
        n = math.prod(shp)
        out.append(flat[off:off + n].reshape(shp))
        off += n
    return out


def _ssm_discretize(a_re, a_im, log_dt, b_re, b_im):
    dt = jnp.exp(log_dt)[..., None]
    lam_re = jnp.minimum(a_re, EIG_RE_MAX)
    lam_im = a_im
    mag = jnp.exp(lam_re * dt)
    ang = lam_im * dt
    lb_re = mag * jnp.cos(ang)
    lb_im = mag * jnp.sin(ang)
    num_re = lb_re - 1.0
    num_im = lb_im
    den = lam_re * lam_re + lam_im * lam_im
    f_re = (num_re * lam_re + num_im * lam_im) / den
    f_im = (num_im * lam_re - num_re * lam_im) / den
    bb_re = f_re[..., None] * b_re - f_im[..., None] * b_im
    bb_im = f_re[..., None] * b_im + f_im[..., None] * b_re
    return lb_re, lb_im, bb_re, bb_im


def _ssm_powers(a_re, a_im, log_dt, count):
    dt = jnp.exp(log_dt)[:, None, :, None]
    k = jnp.arange(1, count + 1, dtype=F32)[None, :, None, None]
    mag = jnp.exp(k * (jnp.minimum(a_re, EIG_RE_MAX)[:, None] * dt))
    ang = k * (a_im[:, None] * dt)
    shape = (a_re.shape[0], count, -1)
    return (mag * jnp.cos(ang)).reshape(shape), (mag * jnp.sin(ang)).reshape(shape)


def _interleave(a, inverse=False):
    lead, (t, width) = a.shape[:-2], a.shape[-2:]
    seg = ROW_TILE // 8
    shape = lead + (t // ROW_TILE,) + ((seg, 8) if inverse else (8, seg)) + (width,)
    return jnp.swapaxes(a.reshape(shape), -3, -2).reshape(a.shape)


def _block_diag(blocks):
    two, g, a, b = blocks.shape
    tiled = jnp.tile(blocks.reshape(two, g * a, b), (1, 1, g))
    row_group = lax.broadcasted_iota(jnp.int32, (g * a, g * b), 0) // a
    col_group = lax.broadcasted_iota(jnp.int32, (g * a, g * b), 1) // b
    return jnp.where(row_group == col_group, tiled, 0.0).astype(BF16)


def _diag_blocks(tiles):
    two, w, _ = tiles.shape
    per = DIAG_TILE // SSM_STATE
    t6 = tiles.reshape(two, w // (per * SSM_GROUP), per, SSM_GROUP, per, SSM_STATE)
    return jnp.einsum("zjqpqn->zjqpn", t6).reshape(two, w // SSM_GROUP, SSM_GROUP, SSM_STATE)


def _rope_tables(t, n_valid):
    pos = jnp.arange(t)
    real = jnp.logical_and(pos >= N_META, pos < n_valid)
    idx = jnp.where(real, pos - N_META, 0)
    row_id = (idx // GRID_W).astype(F32)
    col_id = (idx % GRID_W).astype(F32)
    pairs_per_axis = HEAD_DIM // 4
    inv_freq = ROPE_THETA ** (-jnp.arange(pairs_per_axis, dtype=F32) / pairs_per_axis)
    ang = jnp.concatenate([row_id[:, None] * inv_freq, col_id[:, None] * inv_freq], axis=-1)
    ang = jnp.where(real[:, None], ang, 0.0)
    cos = jnp.repeat(jnp.cos(ang), 2, axis=-1)
    sin = jnp.sin(ang)
    sin = jnp.stack([-sin, sin], axis=-1).reshape(t, HEAD_DIM)
    return jnp.tile(cos, (1, 2)), jnp.tile(sin, (1, 2))


def _local_step(x, loss_target, big, small, comm=None):
    s_len, d = x.shape
    n_valid = s_len + N_META
    t = _round_up(n_valid, KV_TILE)
    du = d // 2
    groups = du // SSM_GROUP
    nh = d // HEAD_DIM
    nkv = nh // KV_REP
    pad = t - n_valid

    x0 = jnp.concatenate([big["meta_tokens"].astype(F32), x, jnp.zeros((pad, d), F32)], axis=0)
    tgt = jnp.concatenate([jnp.zeros((N_META, d), F32), loss_target, jnp.zeros((pad, d), F32)], axis=0)
    cos, sin = _rope_tables(t, n_valid)
    g_mix = small["norm_mix_g"].reshape(1, d)
    g_mlp = small["norm_mlp_g"].reshape(1, d)
    g_fin = small["norm_final_g"].reshape(1, d)
    qg = jnp.tile(small["q_norm_g"].reshape(1, HEAD_DIM), (1, 2))
    kg = jnp.tile(small["k_norm_g"].reshape(1, HEAD_DIM), (1, 2))
    ssm_d = small["ssm_d"].reshape(1, du)
    b_glu = small["b_glu"].reshape(1, du)

    ssm_in = tuple(small[n][0] for n in ("ssm_a_re", "ssm_a_im", "ssm_log_dt", "ssm_b_re", "ssm_b_im"))
    (lb_re, lb_im, bbar_re, bbar_im), disc_vjp = jax.vjp(_ssm_discretize, *ssm_in)
    lam_re = lb_re.reshape(2, 1, groups * SSM_STATE)
    lam_im = lb_im.reshape(2, 1, groups * SSM_STATE)
    pow_re, pow_im = _ssm_powers(*ssm_in[0:3], ROW_TILE // 8)
    bb_re = _block_diag(bbar_re.transpose(0, 1, 3, 2))
    bb_im = _block_diag(bbar_im.transpose(0, 1, 3, 2))
    c_re, c_im = small["ssm_c_re"][0], small["ssm_c_im"][0]
    cct_re = _block_diag(c_re)
    cct_im = _block_diag(c_im)
    cc_re = cct_re.transpose(0, 2, 1)
    cc_im = cct_im.transpose(0, 2, 1)
    bbt_re = bb_re.transpose(0, 2, 1)
    bbt_im = bb_im.transpose(0, 2, 1)
    scan_w = (lam_re, lam_im, pow_re, pow_im)

    in_proj_args = (x0, g_mix, big["w_in"], qg, kg, cos, sin, n_valid)
    if comm is None:
        h, u, ub, qraw, kraw, qat, ka, kt, va, vta, gates = _in_proj_fwd(*in_proj_args)
    else:
        h, u, ub, qraw, kraw, qat, ka, kt, va, vta, gates, got = _in_proj_fwd(
            *in_proj_args, exchange=(comm["pack_weights"](MIXER_WEIGHTS), False))
        big = {**big, **comm["unpack_weights"](MIXER_WEIGHTS, got)}
    ub = _interleave(ub)
    if comm is None:
        y2, xs_re, xs_im = _ssm_scan_fwd(ub, *scan_w, bb_re, bb_im, cc_re, cc_im)
    else:
        y2, xs_re, xs_im, got = _ssm_scan_fwd(ub, *scan_w, bb_re, bb_im, cc_re, cc_im,
                                              exchange=(comm["pack_weights"](MLP_WEIGHTS), False))
        big = {**big, **comm["unpack_weights"](MLP_WEIGHTS, got)}
    y2 = _interleave(y2, inverse=True)
    yf, yb = y2[0], y2[1]
    yt_attn, lse = _attn_fwd(qat, ka, vta)
    mixer_w = (ssm_d, big["w_glu"], b_glu, big["w_ssm_proj"], big["w_attn_proj"], big["w_out"])
    x1 = _mixer_out_fwd(x0, u, yf, yb, yt_attn, gates, *mixer_w)

    dx1, loss8, dg_fin, dg_mlp, h2b, dab, hsqb, dx2b = _mlp_loss_fwd_bwd(
        x1, tgt, g_mlp, g_fin, big["w_mlp_in"], big["w_mlp_out"], n_valid)

    (dyb, dud, dyt_attn, dgates, zb, dglb, ysb, dasb, daab, mgb, dxb, d_ssm_d, d_b_glu) = _mixer_out_bwd(
        dx1, u, yf, yb, yt_attn, gates, *mixer_w)
    grads = {}
    grads["w_glu"] = _matmul_tn(zb, dglb, "grad_w_glu")
    grads["w_ssm_proj"] = _matmul_tn(ysb, dasb, "grad_w_ssm_proj")
    grads["w_attn_proj"] = _matmul_tn(yt_attn, daab, "grad_w_attn_proj", a_is_transposed=True)
    grads["w_out"] = _matmul_tn(mgb, dxb, "grad_w_out")
    grads["w_mlp_in"] = _matmul_tn(h2b, dab, "grad_w_mlp_in")
    grads["w_mlp_out"] = _matmul_tn(hsqb, dx2b, "grad_w_mlp_out")
    dk, dv, dqt = _attn_bwd(qat, ka, kt, va, dyt_attn, yt_attn, lse)
    scan_args = (_interleave(dyb), ub, xs_re, xs_im, *scan_w, cct_re, cct_im, bbt_re, bbt_im)
    if comm is None:
        late_grad_parts = None
        du2, dlam_re, dlam_im, dbr, dbi, dcr, dci = _ssm_scan_bwd(*scan_args)
    else:
        du2, dlam_re, dlam_im, dbr, dbi, dcr, dci, late_grad_parts = _ssm_scan_bwd(
            *scan_args, exchange=(comm["pack_grads"](LATE_WEIGHTS, grads), True))
    du2 = _interleave(du2, inverse=True)
    dx0, dproj, dg_mix, dqg, dkg = _in_proj_bwd(x0, dx1, dud, du2[0], du2[1], qraw, kraw, dqt, dk, dv, dgates,
                                                g_mix, big["w_in"], qg, kg, cos, sin)

    grads["meta_tokens"] = dx0[0:N_META]
    dbb_re = _diag_blocks(dbr).transpose(0, 1, 3, 2)
    dbb_im = _diag_blocks(dbi).transpose(0, 1, 3, 2)
    dc_re, dc_im = _diag_blocks(dcr), _diag_blocks(dci)
    shape_gn = (2, groups, SSM_STATE)
    d_a_re, d_a_im, d_log_dt, d_b_re, d_b_im = disc_vjp(
        (dlam_re.reshape(shape_gn), dlam_im.reshape(shape_gn), dbb_re, dbb_im))
    grads.update({
        "norm_mix_g": dg_mix, "ssm_a_re": d_a_re[None], "ssm_a_im": d_a_im[None], "ssm_log_dt": d_log_dt[None],
        "ssm_b_re": d_b_re[None], "ssm_b_im": d_b_im[None], "ssm_c_re": dc_re[None], "ssm_c_im": dc_im[None],
        "ssm_d": d_ssm_d, "b_glu": d_b_glu,
        "q_norm_g": dqg[:, 0:HEAD_DIM] + dqg[:, HEAD_DIM:128], "k_norm_g": dkg[:, 0:HEAD_DIM] + dkg[:, HEAD_DIM:128],
        "norm_mlp_g": dg_mlp, "norm_final_g": dg_fin.reshape(d),
    })
    if comm is None:
        small_grad_parts = None
        grads["w_in"] = _matmul_tn(h, dproj, "grad_w_in")
    else:
        grads["w_in"], small_grad_parts = _matmul_tn(h, dproj, "grad_w_in",
                                                     exchange=(comm["pack_small_grads"](grads), False))
    return loss8[0, 0], dx0[N_META:n_valid], grads, late_grad_parts, small_grad_parts


def kernel(x, meta_tokens, norm_mix_g, w_in, ssm_a_re, ssm_a_im, ssm_log_dt, ssm_b_re, ssm_b_im, ssm_c_re, ssm_c_im, ssm_d, w_glu, b_glu, q_norm_g, k_norm_g, w_ssm_proj, w_attn_proj, w_out, norm_mlp_g, w_mlp_in, w_mlp_out, norm_final_g, loss_target, m_meta_tokens, m_norm_mix_g, m_w_in, m_ssm_a_re, m_ssm_a_im, m_ssm_log_dt, m_ssm_b_re, m_ssm_b_im, m_ssm_c_re, m_ssm_c_im, m_ssm_d, m_w_glu, m_b_glu, m_q_norm_g, m_k_norm_g, m_w_ssm_proj, m_w_attn_proj, m_w_out, m_norm_mlp_g, m_w_mlp_in, m_w_mlp_out, m_norm_final_g, v_meta_tokens, v_norm_mix_g, v_w_in, v_ssm_a_re, v_ssm_a_im, v_ssm_log_dt, v_ssm_b_re, v_ssm_b_im, v_ssm_c_re, v_ssm_c_im, v_ssm_d, v_w_glu, v_b_glu, v_q_norm_g, v_k_norm_g, v_w_ssm_proj, v_w_attn_proj, v_w_out, v_norm_mlp_g, v_w_mlp_in, v_w_mlp_out, v_norm_final_g):
    w = dict(meta_tokens=meta_tokens, norm_mix_g=norm_mix_g, w_in=w_in, ssm_a_re=ssm_a_re, ssm_a_im=ssm_a_im, ssm_log_dt=ssm_log_dt, ssm_b_re=ssm_b_re, ssm_b_im=ssm_b_im, ssm_c_re=ssm_c_re, ssm_c_im=ssm_c_im, ssm_d=ssm_d, w_glu=w_glu, b_glu=b_glu, q_norm_g=q_norm_g, k_norm_g=k_norm_g, w_ssm_proj=w_ssm_proj, w_attn_proj=w_attn_proj, w_out=w_out, norm_mlp_g=norm_mlp_g, w_mlp_in=w_mlp_in, w_mlp_out=w_mlp_out, norm_final_g=norm_final_g)
    m = dict(meta_tokens=m_meta_tokens, norm_mix_g=m_norm_mix_g, w_in=m_w_in, ssm_a_re=m_ssm_a_re, ssm_a_im=m_ssm_a_im, ssm_log_dt=m_ssm_log_dt, ssm_b_re=m_ssm_b_re, ssm_b_im=m_ssm_b_im, ssm_c_re=m_ssm_c_re, ssm_c_im=m_ssm_c_im, ssm_d=m_ssm_d, w_glu=m_w_glu, b_glu=m_b_glu, q_norm_g=m_q_norm_g, k_norm_g=m_k_norm_g, w_ssm_proj=m_w_ssm_proj, w_attn_proj=m_w_attn_proj, w_out=m_w_out, norm_mlp_g=m_norm_mlp_g, w_mlp_in=m_w_mlp_in, w_mlp_out=m_w_mlp_out, norm_final_g=m_norm_final_g)
    v = dict(meta_tokens=v_meta_tokens, norm_mix_g=v_norm_mix_g, w_in=v_w_in, ssm_a_re=v_ssm_a_re, ssm_a_im=v_ssm_a_im, ssm_log_dt=v_ssm_log_dt, ssm_b_re=v_ssm_b_re, ssm_b_im=v_ssm_b_im, ssm_c_re=v_ssm_c_re, ssm_c_im=v_ssm_c_im, ssm_d=v_ssm_d, w_glu=v_w_glu, b_glu=v_b_glu, q_norm_g=v_q_norm_g, k_norm_g=v_k_norm_g, w_ssm_proj=v_w_ssm_proj, w_attn_proj=v_w_attn_proj, w_out=v_w_out, norm_mlp_g=v_norm_mlp_g, w_mlp_in=v_w_mlp_in, w_mlp_out=v_w_mlp_out, norm_final_g=v_norm_final_g)

    shard2d = {n: w[n].reshape(w[n].shape[-2:]) for n in BIG_WEIGHTS}
    big_shapes = [shard2d[n].shape for n in BIG_WEIGHTS]

    meta_hi = shard2d["meta_tokens"].astype(BF16)
    meta_res = shard2d["meta_tokens"] - meta_hi.astype(F32)
    meta_mid = meta_res.astype(BF16)
    meta_lo = (meta_res - meta_mid.astype(F32)).astype(BF16)
    shapes_of = lambda names: [shard2d[n].shape for n in names]

    def full_weights(names, shards):
        return {n: s if n in BLOCK_WEIGHTS else _from_shards(s, BIG_SHARD_AXIS[n]) for n, s in zip(names, shards)}

    early = _exchange([(_pack_rows([meta_hi, meta_mid, meta_lo, shard2d["w_in"].astype(BF16)], 0), False)],
                      "gather_early_weights")[0]
    shards = _unpack_rows(early, [meta_hi.shape] * 3 + shapes_of(EARLY_WEIGHTS[1:]))
    meta = [_from_shards(s, 1).astype(F32) for s in shards[0:3]]
    big = {"meta_tokens": (meta[0] + meta[1]) + meta[2], **full_weights(EARLY_WEIGHTS[1:], shards[3:])}
    small = {n: w[n] for n in SMALL_WEIGHTS}
    pack_grads = lambda names, grads: _pack_rows(
        [_to_shards(grads[n], BIG_SHARD_AXIS[n]) for n in names], 1).astype(BF16)
    comm = {
        "pack_weights": lambda names: _pack_rows([shard2d[n].astype(BF16) for n in names], 0),
        "unpack_weights": lambda names, g: full_weights(names, _unpack_rows(g, shapes_of(names))),
        "pack_grads": pack_grads,
        "pack_small_grads": lambda grads: _pack_flat([grads[n] for n in SMALL_WEIGHTS]),
    }

    loss, grad_x, grads, late_parts, small_parts = _local_step(x[0], loss_target[0], big, small, comm)
    loss = lax.psum(loss, ("x", "y", "c"))

    pk = lambda names, src: _pack_rows([src[n].reshape(shard2d[n].shape) for n in names], 0)
    pe, pl_ = functools.partial(pk, EARLY_WEIGHTS), functools.partial(pk, LATE_WEIGHTS)
    *late_out, early_parts = _reduce_adamw(late_parts, pl_(w), pl_(m), pl_(v), "adamw_sharded_late",
                                           exchange=(pack_grads(EARLY_WEIGHTS, grads), True))
    early_out = _reduce_adamw(early_parts, pe(w), pe(m), pe(v), "adamw_sharded_early")
    small_shapes = [w[n].shape for n in SMALL_WEIGHTS]
    pf = lambda src: _pack_flat([src[n] for n in SMALL_WEIGHTS])
    small_out = _reduce_adamw(small_parts, pf(w), pf(m), pf(v), "adamw_replicated")

    results = []
    for kind in range(4):
        big_un = dict(zip(EARLY_WEIGHTS + LATE_WEIGHTS,
                          _unpack_rows(early_out[kind], shapes_of(EARLY_WEIGHTS))
                          + _unpack_rows(late_out[kind], shapes_of(LATE_WEIGHTS))))
        small_un = dict(zip(SMALL_WEIGHTS, _unpack(small_out[kind], small_shapes)))
        for n in ALL_WEIGHTS:
            results.append(big_un[n].reshape(w[n].shape) if n in big_un else small_un[n])
    return (loss, grad_x[None], *results)
```

```python
import functools
import math

import jax
import jax.numpy as jnp
from jax import lax
from jax.experimental import pallas as pl
from jax.experimental.pallas import tpu as pltpu

F32 = jnp.float32
BF16 = jnp.bfloat16

N_DEV = 8
N_META = 16
GRID_W = 64
SSM_GROUP = 16
SSM_STATE = 64
HEAD_DIM = 64
KV_REP = 4
ROPE_THETA = 10000.0
NORM_EPS = 1e-6
EIG_RE_MAX = -1e-4
ATTN_SCALE = HEAD_DIM ** -0.5

ADAM_LR = 0.001
ADAM_B1 = 0.9
ADAM_B2 = 0.999
ADAM_EPS = 1e-08
ADAM_WD = 0.01
ADAM_STEP = 10

ROW_TILE = 384
ROW_TILE_BWD = 384
QUERY_STRIP = 256
ONE_PASS_SLACK = 60.0
VT_ROWS = 80
MASK_BIAS = -1e30
SCAN_LANES = 512
DIAG_TILE = 256
KV_TILE = 768
PACK_W = 1024
V7X_VMEM_LIMIT = 56 * 1024 * 1024
NEG_BIG = -1e30

BIG_WEIGHTS = ("meta_tokens", "w_in", "w_glu", "w_ssm_proj", "w_attn_proj", "w_out", "w_mlp_in", "w_mlp_out")
BIG_SHARD_AXIS = {"meta_tokens": 1, "w_in": 1, "w_glu": 0, "w_ssm_proj": 1, "w_attn_proj": 0, "w_out": 0,
                  "w_mlp_in": 1, "w_mlp_out": 0}
BLOCK_WEIGHTS = ("w_in", "w_mlp_in", "w_mlp_out")
EARLY_WEIGHTS = ("meta_tokens", "w_in")
MIXER_WEIGHTS = ("w_glu", "w_ssm_proj", "w_attn_proj", "w_out")
MLP_WEIGHTS = ("w_mlp_in", "w_mlp_out")
LATE_WEIGHTS = MIXER_WEIGHTS + MLP_WEIGHTS
SMALL_WEIGHTS = ("norm_mix_g", "ssm_a_re", "ssm_a_im", "ssm_log_dt", "ssm_b_re", "ssm_b_im", "ssm_c_re",
                 "ssm_c_im", "ssm_d", "b_glu", "q_norm_g", "k_norm_g", "norm_mlp_g", "norm_final_g")
ALL_WEIGHTS = ("meta_tokens", "norm_mix_g", "w_in", "ssm_a_re", "ssm_a_im", "ssm_log_dt", "ssm_b_re", "ssm_b_im",
               "ssm_c_re", "ssm_c_im", "ssm_d", "w_glu", "b_glu", "q_norm_g", "k_norm_g", "w_ssm_proj",
               "w_attn_proj", "w_out", "norm_mlp_g", "w_mlp_in", "w_mlp_out", "norm_final_g")


def _round_up(n, m):
    return (n + m - 1) // m * m


def _pcall(body, *, name, grid, in_specs, out_specs, out_shape, scratch=(), vmem=None, **kw):
    params = pltpu.CompilerParams(dimension_semantics=("arbitrary",) * len(grid), vmem_limit_bytes=vmem)
    return pl.pallas_call(body, name=name, grid=grid, in_specs=in_specs, out_specs=out_specs, out_shape=out_shape,
                          scratch_shapes=list(scratch), compiler_params=params, **kw)


def _dot(a, b):
    return jnp.dot(a, b, preferred_element_type=F32)


def _dot_nt(a, b):
    return lax.dot_general(a, b, (((1,), (1,)), ((), ())), preferred_element_type=F32)


def _dot_tn(a, b):
    return lax.dot_general(a, b, (((0,), (0,)), ((), ())), preferred_element_type=F32)


def _full_spec(shape):
    nd = len(shape)
    return pl.BlockSpec(shape, lambda *_: (0,) * nd)


def _row_spec(tm, width):
    return pl.BlockSpec((tm, width), lambda i: (i, 0))


def _heads_spec(nh, tm):
    return pl.BlockSpec((nh, tm, HEAD_DIM), lambda i: (0, i, 0))


_ANY = pl.BlockSpec(memory_space=pl.ANY)


def _load_once(step, pairs, sem):
    @pl.when(step == 0)
    def _():
        copies = [pltpu.make_async_copy(src, dst, sem.at[k]) for k, (src, dst) in enumerate(pairs)]
        for cp in copies:
            cp.start()
        for cp in copies:
            cp.wait()


def _swap_pairs(x, even):
    n = x.shape[-1]
    return jnp.where(even, pltpu.roll(x, n - 1, 1), pltpu.roll(x, 1, 1))


def _gelu(y):
    return 0.5 * y * (1.0 + lax.erf(y * (1.0 / math.sqrt(2.0))))


def _gelu_grad(y):
    return 0.5 * (1.0 + lax.erf(y * (1.0 / math.sqrt(2.0)))) + y * jnp.exp(-0.5 * y * y) * (1.0 / math.sqrt(2.0 * math.pi))


def _in_proj_fwd(x0, g_mix, w_in, qg, kg, cos, sin, n_valid, exchange=None):
    t, d = x0.shape
    tm = ROW_TILE
    du, dk = d // 2, d // 4
    nh, nkv = d // HEAD_DIM, d // HEAD_DIM // KV_REP
    bw = w_in.shape[-1]
    assert bw == du and dk * 2 == bw

    def body(*refs):
        i = pl.program_id(0)
        (x_ref, g_ref, w_hbm, qg_ref, kg_ref, c_ref, s_ref,
         h_ref, u_ref, ub_ref, qraw_ref, kraw_ref, qat_ref, ka_ref, kt_ref, va_ref, vta_ref, gates_ref,
         w_ref, sem) = _riding_exchange(refs, exchange, 7, 11, i == 0, i == t // tm - 1)
        _load_once(i, [(w_hbm, w_ref)], sem)
        x = x_ref[...]
        r = lax.rsqrt(jnp.mean(x * x, axis=-1, keepdims=True) + NORM_EPS)
        h = ((x * r) * g_ref[...]).astype(BF16)
        h_ref[...] = h
        u = _dot(h, w_ref[0])
        u_ref[...] = u
        ub_ref[...] = u.astype(BF16)
        lane = lax.broadcasted_iota(jnp.int32, (tm, 128), 1)
        lo = lane < HEAD_DIM
        even = (lane & 1) == 0
        aug = lane == HEAD_DIM
        c = c_ref[...]
        s = s_ref[...]
        row = i * tm + lax.broadcasted_iota(jnp.int32, (tm, 1), 0)
        one = jnp.where(aug, 1.0, 0.0)
        key_bias = jnp.where(jnp.logical_and(aug, row >= n_valid), MASK_BIAS, 0.0)

        def norm_rope(blk, g128):
            sq = blk * blk
            ms_lo = jnp.sum(jnp.where(lo, sq, 0.0), axis=-1, keepdims=True) * (1.0 / HEAD_DIM)
            ms_hi = jnp.sum(jnp.where(lo, 0.0, sq), axis=-1, keepdims=True) * (1.0 / HEAD_DIM)
            rr = jnp.where(lo, lax.rsqrt(ms_lo + NORM_EPS), lax.rsqrt(ms_hi + NORM_EPS))
            qn = (blk * rr) * g128
            return qn * c + _swap_pairs(qn, even) * s

        def put_heads(rows_ref, cols_ref, first, pair, extra):
            for k, head in enumerate((pair, pltpu.roll(pair, HEAD_DIM, 1))):
                wide = jnp.where(lo, head, extra)
                if rows_ref is not None:
                    rows_ref[first + k] = wide.astype(BF16)
                cols_ref[first + k] = wide.T[0:cols_ref.shape[1], :].astype(BF16)

        for blk in range(2):
            qb = _dot(h, w_ref[1 + blk])
            qraw_ref[:, bw * blk:bw * (blk + 1)] = qb
            for a in range(bw // 128):
                put_heads(None, qat_ref, (bw // HEAD_DIM) * blk + 2 * a,
                          norm_rope(qb[:, 128 * a:128 * (a + 1)], qg_ref[...]) * ATTN_SCALE, one)
        kv = _dot(h, w_ref[3])
        kraw_ref[...] = kv[:, 0:dk]
        for a in range(nkv // 2):
            put_heads(ka_ref, kt_ref, 2 * a, norm_rope(kv[:, 128 * a:128 * (a + 1)], kg_ref[...]), key_bias)
            put_heads(va_ref, vta_ref, 2 * a, kv[:, dk + 128 * a:dk + 128 * (a + 1)], one)
        for blk in range(4):
            gates_ref[:, bw * blk:bw * (blk + 1)] = _dot(h, w_ref[4 + blk])

    heads = lambda n: pl.BlockSpec((n, tm, 128), lambda i: (0, i, 0))
    heads_t = lambda n, rows: pl.BlockSpec((n, rows, tm), lambda i: (0, 0, i))
    extra = exchange is not None
    return _pcall(
        body, name="in_proj_fwd", grid=(t // tm,),
        in_specs=[_row_spec(tm, d), _full_spec((1, d)), _ANY, _full_spec((1, 128)), _full_spec((1, 128)),
                  _row_spec(tm, 128), _row_spec(tm, 128)] + [_ANY] * extra,
        out_specs=[_row_spec(tm, d), _row_spec(tm, du), _row_spec(tm, du), _row_spec(tm, d), _row_spec(tm, dk),
                   heads_t(nh, 128), heads(nkv), heads_t(nkv, HEAD_DIM), heads(nkv), heads_t(nkv, VT_ROWS),
                   _row_spec(tm, 2 * d)] + [_ANY] * extra,
        out_shape=[jax.ShapeDtypeStruct((t, d), BF16), jax.ShapeDtypeStruct((t, du), F32),
                   jax.ShapeDtypeStruct((t, du), BF16), jax.ShapeDtypeStruct((t, d), F32),
                   jax.ShapeDtypeStruct((t, dk), F32), jax.ShapeDtypeStruct((nh, 128, t), BF16),
                   jax.ShapeDtypeStruct((nkv, t, 128), BF16), jax.ShapeDtypeStruct((nkv, HEAD_DIM, t), BF16),
                   jax.ShapeDtypeStruct((nkv, t, 128), BF16), jax.ShapeDtypeStruct((nkv, VT_ROWS, t), BF16),
                   jax.ShapeDtypeStruct((t, 2 * d), F32)] + ([_exchange_out_shape(*exchange)] if extra else []),
        scratch=[pltpu.VMEM((N_DEV, d, bw), BF16), pltpu.SemaphoreType.DMA((1,))] + _EXCHANGE_SEMS * extra,
        vmem=V7X_VMEM_LIMIT,
    )(x0, g_mix, w_in, qg, kg, cos, sin, *([exchange[0]] if extra else []))


def _mixer_values(u, yf, yb, yt_attn, gates, d_ref, wg_ref, bg_ref, ps_ref, pa_ref, d):
    y = (u * d_ref[...] + yf) + yb
    z = _gelu(y)
    sg = jax.nn.sigmoid(_dot(z.astype(BF16), wg_ref[...]) + bg_ref[...])
    y_ssm = z * sg
    a_ssm = _dot(y_ssm.astype(BF16), ps_ref[...])
    a_attn = _dot_tn(yt_attn.astype(BF16), pa_ref[...])
    s_ssm = jax.nn.sigmoid(gates[:, 0:d])
    s_attn = jax.nn.sigmoid(gates[:, d:2 * d])
    merged = s_ssm * a_ssm + s_attn * a_attn
    return y, z, sg, y_ssm, a_ssm, a_attn, s_ssm, s_attn, merged


def _mixer_out_fwd(x0, u, yf, yb, y_attn, gates, ssm_d, w_glu, b_glu, p_ssm, p_attn, w_out):
    t, d = x0.shape
    tm = ROW_TILE
    du = d // 2

    def body(x_ref, u_ref, yf_ref, yb_ref, ya_ref, gt_ref, d_ref, wg_ref, bg_ref, ps_ref, pa_ref, wo_ref, x1_ref):
        vals = _mixer_values(u_ref[...], yf_ref[...], yb_ref[...], ya_ref[...], gt_ref[...],
                             d_ref, wg_ref, bg_ref, ps_ref, pa_ref, d)
        merged = vals[-1]
        x1_ref[...] = x_ref[...] + _dot(merged.astype(BF16), wo_ref[...])

    return _pcall(
        body, name="mixer_out_fwd", grid=(t // tm,),
        in_specs=[_row_spec(tm, d), _row_spec(tm, du), _row_spec(tm, du), _row_spec(tm, du),
                  pl.BlockSpec((d, tm), lambda i: (0, i)), _row_spec(tm, 2 * d), _full_spec((1, du)), _full_spec((du, du)), _full_spec((1, du)),
                  _full_spec((du, d)), _full_spec((d, d)), _full_spec((d, d))],
        out_specs=_row_spec(tm, d), out_shape=jax.ShapeDtypeStruct((t, d), F32), vmem=V7X_VMEM_LIMIT,
    )(x0, u, yf, yb, y_attn, gates, ssm_d, w_glu, b_glu, p_ssm, p_attn, w_out)


def _mlp_loss_fwd_bwd(x1, target, g_mlp, g_fin, w1, w2, n_valid):
    t, d = x1.shape
    tm = ROW_TILE_BWD
    dff = 4 * d
    nfc, _, fc = w1.shape

    def body(x_ref, tg_ref, gm_ref, gf_ref, w1_hbm, w2_hbm,
             dx1_ref, loss_ref, dgf_ref, dgm_ref, h2_ref, da_ref, hsq_ref, dx2b_ref,
             w1_ref, w2_ref, relu_ref, sem):
        i = pl.program_id(0)
        _load_once(i, [(w1_hbm, w1_ref), (w2_hbm, w2_ref)], sem)

        @pl.when(i == 0)
        def _():
            loss_ref[...] = jnp.zeros_like(loss_ref)
            dgf_ref[...] = jnp.zeros_like(dgf_ref)
            dgm_ref[...] = jnp.zeros_like(dgm_ref)

        x1v = x_ref[...]
        r1 = lax.rsqrt(jnp.mean(x1v * x1v, axis=-1, keepdims=True) + NORM_EPS)
        xh1 = x1v * r1
        h2b = (xh1 * gm_ref[...]).astype(BF16)
        h2_ref[...] = h2b
        acc = jnp.zeros((tm, d), F32)
        for c in range(nfc):
            a = jnp.maximum(_dot(h2b, w1_ref[c]), 0.0)
            relu_ref[:, fc * c:fc * (c + 1)] = a
            hs = (a * a).astype(BF16)
            hsq_ref[:, fc * c:fc * (c + 1)] = hs
            acc = acc + _dot(hs, w2_ref[c])
        x2 = x1v + acc
        r2 = lax.rsqrt(jnp.mean(x2 * x2, axis=-1, keepdims=True) + NORM_EPS)
        xh2 = x2 * r2
        out = xh2 * gf_ref[...]
        row = i * tm + lax.broadcasted_iota(jnp.int32, (tm, 1), 0)
        valid = jnp.logical_and(row >= N_META, row < n_valid)
        diff = jnp.where(valid, out - tg_ref[...], 0.0)
        loss_ref[...] += 0.5 * jnp.sum(jnp.sum(diff * diff, axis=-1, keepdims=True) * (1.0 / d))
        dout = diff * (1.0 / d)
        dgf_ref[...] += jnp.sum(dout * xh2, axis=0, keepdims=True)
        dxh2 = dout * gf_ref[...]
        dx2 = r2 * (dxh2 - xh2 * jnp.mean(dxh2 * xh2, axis=-1, keepdims=True))
        dx2b = dx2.astype(BF16)
        dx2b_ref[...] = dx2b
        dh2 = jnp.zeros((tm, d), F32)
        for c in range(nfc):
            dhs = _dot_nt(dx2b, w2_ref[c])
            da = (dhs * (2.0 * relu_ref[:, fc * c:fc * (c + 1)])).astype(BF16)
            da_ref[:, fc * c:fc * (c + 1)] = da
            dh2 = dh2 + _dot_nt(da, w1_ref[c])
        dgm_ref[...] += jnp.sum(dh2 * xh1, axis=0, keepdims=True)
        dxh1 = dh2 * gm_ref[...]
        dx1_ref[...] = dx2 + r1 * (dxh1 - xh1 * jnp.mean(dxh1 * xh1, axis=-1, keepdims=True))

    return _pcall(
        body, name="mlp_loss_fwd_bwd", grid=(t // tm,),
        in_specs=[_row_spec(tm, d), _row_spec(tm, d), _full_spec((1, d)), _full_spec((1, d)), _ANY, _ANY],
        out_specs=[_row_spec(tm, d), _full_spec((8, 128)), _full_spec((1, d)), _full_spec((1, d)),
                   _row_spec(tm, d), _row_spec(tm, dff), _row_spec(tm, dff), _row_spec(tm, d)],
        out_shape=[jax.ShapeDtypeStruct((t, d), F32), jax.ShapeDtypeStruct((8, 128), F32),
                   jax.ShapeDtypeStruct((1, d), F32), jax.ShapeDtypeStruct((1, d), F32),
                   jax.ShapeDtypeStruct((t, d), BF16), jax.ShapeDtypeStruct((t, dff), BF16),
                   jax.ShapeDtypeStruct((t, dff), BF16), jax.ShapeDtypeStruct((t, d), BF16)],
        scratch=[pltpu.VMEM((nfc, d, fc), BF16), pltpu.VMEM((nfc, fc, d), BF16), pltpu.VMEM((tm, dff), F32),
                 pltpu.SemaphoreType.DMA((2,))],
        vmem=V7X_VMEM_LIMIT,
    )(x1, target, g_mlp, g_fin, w1, w2)


def _mixer_out_bwd(dx1, u, yf, yb, yt_attn, gates, ssm_d, w_glu, b_glu, p_ssm, p_attn, w_out):
    t, d = dx1.shape
    tm = ROW_TILE_BWD
    du = d // 2

    def body(dx_ref, u_ref, yf_ref, yb_ref, yt_ref, gt_ref, d_ref, wg_ref, bg_ref, ps_ref, pa_ref, wo_ref,
             dyb_ref, dud_ref, dyat_ref, dgates_ref, zb_ref, dglb_ref, ysb_ref, dasb_ref, daab_ref,
             mgb_ref, dxb_ref, dd_ref, dbg_ref):
        i = pl.program_id(0)

        @pl.when(i == 0)
        def _():
            dd_ref[...] = jnp.zeros_like(dd_ref)
            dbg_ref[...] = jnp.zeros_like(dbg_ref)

        uv = u_ref[...]
        y, z, sg, y_ssm, a_ssm, a_attn, s_ssm, s_attn, merged = _mixer_values(
            uv, yf_ref[...], yb_ref[...], yt_ref[...], gt_ref[...], d_ref, wg_ref, bg_ref, ps_ref, pa_ref, d)
        dxb = dx_ref[...].astype(BF16)
        dxb_ref[...] = dxb
        mgb_ref[...] = merged.astype(BF16)
        dmerged = _dot_nt(dxb, wo_ref[...])
        dgates_ref[:, 0:d] = (dmerged * a_ssm * (s_ssm * (1.0 - s_ssm))).astype(BF16)
        dgates_ref[:, d:2 * d] = (dmerged * a_attn * (s_attn * (1.0 - s_attn))).astype(BF16)
        da_ssm = (dmerged * s_ssm).astype(BF16)
        da_attn = (dmerged * s_attn).astype(BF16)
        dasb_ref[...] = da_ssm
        daab_ref[...] = da_attn
        ysb_ref[...] = y_ssm.astype(BF16)
        dy_ssm = _dot_nt(da_ssm, ps_ref[...])
        dyat_ref[...] = _dot_nt(pa_ref[...], da_attn).astype(BF16)
        dgl = dy_ssm * z * (sg * (1.0 - sg))
        dglb = dgl.astype(BF16)
        dglb_ref[...] = dglb
        zb_ref[...] = z.astype(BF16)
        dbg_ref[...] += jnp.sum(dgl, axis=0, keepdims=True)
        dz = dy_ssm * sg + _dot_nt(dglb, wg_ref[...])
        dy = dz * _gelu_grad(y)
        dyb_ref[...] = dy.astype(BF16)
        dd_ref[...] += jnp.sum(dy * uv, axis=0, keepdims=True)
        dud_ref[...] = dy * d_ref[...]

    bf = lambda w: jax.ShapeDtypeStruct((t, w), BF16)
    return _pcall(
        body, name="mixer_out_bwd", grid=(t // tm,),
        in_specs=[_row_spec(tm, d), _row_spec(tm, du), _row_spec(tm, du), _row_spec(tm, du),
                  pl.BlockSpec((d, tm), lambda i: (0, i)),
                  _row_spec(tm, 2 * d), _full_spec((1, du)), _full_spec((du, du)), _full_spec((1, du)),
                  _full_spec((du, d)), _full_spec((d, d)), _full_spec((d, d))],
        out_specs=[_row_spec(tm, du), _row_spec(tm, du), pl.BlockSpec((d, tm), lambda i: (0, i)),
                   _row_spec(tm, 2 * d), _row_spec(tm, du), _row_spec(tm, du), _row_spec(tm, du), _row_spec(tm, d),
                   _row_spec(tm, d), _row_spec(tm, d), _row_spec(tm, d), _full_spec((1, du)), _full_spec((1, du))],
        out_shape=[bf(du), jax.ShapeDtypeStruct((t, du), F32), jax.ShapeDtypeStruct((d, t), BF16), bf(2 * d),
                   bf(du), bf(du), bf(du), bf(d), bf(d), bf(d), bf(d),
                   jax.ShapeDtypeStruct((1, du), F32), jax.ShapeDtypeStruct((1, du), F32)],
        vmem=V7X_VMEM_LIMIT,
    )(dx1, u, yf, yb, yt_attn, gates, ssm_d, w_glu, b_glu, p_ssm, p_attn, w_out)


def _in_proj_bwd(x0, dx1, dud, duf, dub, qraw, kraw, dq, dk, dv, dgates, g_mix, w_in, qg, kg, cos, sin):
    t, d = x0.shape
    tm = ROW_TILE_BWD
    du, dkw = d // 2, d // 4
    nh, nkv = d // HEAD_DIM, d // HEAD_DIM // KV_REP
    bw = w_in.shape[-1]
    o_q, o_k, o_v, o_g = du, du + d, du + d + dkw, 2 * d

    def body(x_ref, dx1_ref, dud_ref, duf_ref, dub_ref, qraw_ref, kraw_ref, dq_ref, dk_ref, dv_ref, dgt_ref,
             g_ref, w_hbm, qg_ref, kg_ref, c_ref, s_ref,
             dx0_ref, dproj_ref, dgm_ref, dqg_ref, dkg_ref,
             w_ref, kv_ref, sem):
        i = pl.program_id(0)
        _load_once(i, [(w_hbm, w_ref)], sem)

        @pl.when(i == 0)
        def _():
            dgm_ref[...] = jnp.zeros_like(dgm_ref)
            dqg_ref[...] = jnp.zeros_like(dqg_ref)
            dkg_ref[...] = jnp.zeros_like(dkg_ref)

        lane = lax.broadcasted_iota(jnp.int32, (tm, 128), 1)
        lo = lane < HEAD_DIM
        even = (lane & 1) == 0
        c = c_ref[...]
        s = s_ref[...]

        def norm_rope_bwd(dout, raw, g128):
            sq = raw * raw
            ms_lo = jnp.sum(jnp.where(lo, sq, 0.0), axis=-1, keepdims=True) * (1.0 / HEAD_DIM)
            ms_hi = jnp.sum(jnp.where(lo, 0.0, sq), axis=-1, keepdims=True) * (1.0 / HEAD_DIM)
            rr = jnp.where(lo, lax.rsqrt(ms_lo + NORM_EPS), lax.rsqrt(ms_hi + NORM_EPS))
            xh = raw * rr
            dqn = dout * c + _swap_pairs(dout * s, even)
            dg = jnp.sum(dqn * xh, axis=0, keepdims=True)
            tt = dqn * g128
            pr = tt * xh
            mu_lo = jnp.sum(jnp.where(lo, pr, 0.0), axis=-1, keepdims=True) * (1.0 / HEAD_DIM)
            mu_hi = jnp.sum(jnp.where(lo, 0.0, pr), axis=-1, keepdims=True) * (1.0 / HEAD_DIM)
            return rr * (tt - xh * jnp.where(lo, mu_lo, mu_hi)), dg

        dub_tot = (dud_ref[...] + duf_ref[...]) + dub_ref[...]
        dproj_ref[:, 0:du] = dub_tot.astype(BF16)
        dqg = jnp.zeros((1, 128), F32)
        for a in range(nh // 2):
            sl = slice(128 * a, 128 * (a + 1))
            draw, dg = norm_rope_bwd(dq_ref[sl, :].T * ATTN_SCALE, qraw_ref[:, sl], qg_ref[...])
            dqg = dqg + dg
            dproj_ref[:, o_q + 128 * a:o_q + 128 * (a + 1)] = draw.astype(BF16)
        dqg_ref[...] += dqg
        for hh in range(nkv):
            kv_ref[:, HEAD_DIM * hh:HEAD_DIM * (hh + 1)] = dk_ref[hh, :, 0:HEAD_DIM]
        dkg = jnp.zeros((1, 128), F32)
        for a in range(nkv // 2):
            sl = slice(128 * a, 128 * (a + 1))
            draw, dg = norm_rope_bwd(kv_ref[:, sl], kraw_ref[:, sl], kg_ref[...])
            dkg = dkg + dg
            dproj_ref[:, o_k + 128 * a:o_k + 128 * (a + 1)] = draw.astype(BF16)
        dkg_ref[...] += dkg
        for hh in range(nkv):
            kv_ref[:, HEAD_DIM * hh:HEAD_DIM * (hh + 1)] = dv_ref[hh]
        dproj_ref[:, o_v:o_g] = kv_ref[...].astype(BF16)
        dproj_ref[:, o_g:4 * d] = dgt_ref[...]
        dh = jnp.zeros((tm, d), F32)
        for blk in range(N_DEV):
            dh = dh + _dot_nt(dproj_ref[:, bw * blk:bw * (blk + 1)], w_ref[blk])
        x = x_ref[...]
        r = lax.rsqrt(jnp.mean(x * x, axis=-1, keepdims=True) + NORM_EPS)
        xh0 = x * r
        dgm_ref[...] += jnp.sum(dh * xh0, axis=0, keepdims=True)
        dxh = dh * g_ref[...]
        dx0_ref[...] = dx1_ref[...] + r * (dxh - xh0 * jnp.mean(dxh * xh0, axis=-1, keepdims=True))

    return _pcall(
        body, name="in_proj_bwd", grid=(t // tm,),
        in_specs=[_row_spec(tm, d), _row_spec(tm, d), _row_spec(tm, du), _row_spec(tm, du), _row_spec(tm, du),
                  _row_spec(tm, d), _row_spec(tm, dkw), pl.BlockSpec((d, tm), lambda i: (0, i)),
                  pl.BlockSpec((nkv, tm, 128), lambda i: (0, i, 0)), _heads_spec(nkv, tm),
                  _row_spec(tm, 2 * d), _full_spec((1, d)), _ANY, _full_spec((1, 128)), _full_spec((1, 128)),
                  _row_spec(tm, 128), _row_spec(tm, 128)],
        out_specs=[_row_spec(tm, d), _row_spec(tm, 4 * d), _full_spec((1, d)), _full_spec((1, 128)),
                   _full_spec((1, 128))],
        out_shape=[jax.ShapeDtypeStruct((t, d), F32), jax.ShapeDtypeStruct((t, 4 * d), BF16),
                   jax.ShapeDtypeStruct((1, d), F32), jax.ShapeDtypeStruct((1, 128), F32),
                   jax.ShapeDtypeStruct((1, 128), F32)],
        scratch=[pltpu.VMEM((N_DEV, d, bw), BF16), pltpu.VMEM((tm, dkw), F32), pltpu.SemaphoreType.DMA((1,))],
        vmem=V7X_VMEM_LIMIT,
    )(x0, dx1, dud, duf, dub, qraw, kraw, dq, dk, dv, dgates, g_mix, w_in, qg, kg, cos, sin)


def _attn_fwd(qat, ka, vta):
    nh, _, t = qat.shape
    nkv = ka.shape[0]
    rep = nh // nkv
    hd = HEAD_DIM
    vr = vta.shape[1]
    tq = tk = KV_TILE

    def body(qt_ref, k_ref, vt_ref, ot_ref, lse_ref, m_scr, acc_scr, excess_scr):
        j = pl.program_id(2)
        src = j % 2
        dst = 1 - src

        @pl.when(j == 0)
        def _():
            m_scr[0] = jnp.full(m_scr.shape[1:], NEG_BIG, F32)
            acc_scr[0] = jnp.zeros(acc_scr.shape[1:], F32)
            excess_scr[...] = jnp.full(excess_scr.shape, -NEG_BIG, F32)

        k = k_ref[0]
        vt = vt_ref[0]
        strips = [(r, c) for r in range(rep) for c in range(0, tq, QUERY_STRIP)]
        scores = lambda r, c: _dot(k, qt_ref[r, :, c:c + QUERY_STRIP])

        def sweep(one_pass):
            def add_values(r, cols, before, after, pt):
                acc = acc_scr[src, r, :, cols]
                acc_scr[dst, r, :, cols] = after * ((acc if before is None else before * acc) + _dot(vt, pt))

            ahead = [scores(*strips[0]), scores(*strips[1])]
            pending = None
            excess = jnp.full((1, QUERY_STRIP), NEG_BIG, F32)
            for n, (r, c) in enumerate(strips):
                st = ahead.pop(0)
                if n + 2 < len(strips):
                    ahead.append(scores(*strips[n + 2]))
                cols = slice(c, c + QUERY_STRIP)
                m_prev = m_scr[src, r, :, cols]
                if one_pass:
                    pt = jnp.exp(st - m_prev).astype(BF16)
                    tile_max = jnp.max(st, axis=0, keepdims=True)
                    m_next = jnp.maximum(m_prev, tile_max)
                    excess = jnp.maximum(excess, tile_max - m_prev)
                    factors = (None, jnp.exp(m_prev - m_next))
                else:
                    m_next = jnp.maximum(m_prev, jnp.max(st, axis=0, keepdims=True))
                    pt = jnp.exp(st - m_next).astype(BF16)
                    factors = (jnp.exp(m_prev - m_next), 1.0)
                m_scr[dst, r, :, cols] = m_next
                if pending is not None:
                    add_values(*pending)
                pending = (r, cols, *factors, pt)
            add_values(*pending)
            return excess

        @pl.when(j > 0)
        def _():
            excess_scr[...] = sweep(one_pass=True)

        @pl.when(jnp.max(excess_scr[...]) > ONE_PASS_SLACK)
        def _():
            sweep(one_pass=False)

        @pl.when(j == pl.num_programs(2) - 1)
        def _():
            for r in range(rep):
                l = acc_scr[dst, r, hd:hd + 1, :]
                ot_ref[hd * r:hd * (r + 1), :] = acc_scr[dst, r, 0:hd, :] / l
                lse_ref[0, r:r + 1, :] = m_scr[dst, r] + jnp.log(l)

    return _pcall(
        body, name="attn_fwd", grid=(nkv, t // tq, t // tk),
        in_specs=[pl.BlockSpec((rep, 128, tq), lambda g, i, j: (g, 0, i)),
                  pl.BlockSpec((1, tk, 128), lambda g, i, j: (g, j, 0)),
                  pl.BlockSpec((1, vr, tk), lambda g, i, j: (g, 0, j))],
        out_specs=[pl.BlockSpec((rep * hd, tq), lambda g, i, j: (g, i)),
                   pl.BlockSpec((1, rep, tq), lambda g, i, j: (g, 0, i))],
        out_shape=[jax.ShapeDtypeStruct((nh * hd, t), F32), jax.ShapeDtypeStruct((nkv, rep, t), F32)],
        scratch=[pltpu.VMEM((2, rep, 1, tq), F32), pltpu.VMEM((2, rep, vr, tq), F32),
                 pltpu.VMEM((1, QUERY_STRIP), F32)],
        vmem=V7X_VMEM_LIMIT,
    )(qat, ka, vta)


def _attn_bwd(qat, ka, kt, va, dot, ot, lse_row):
    nh, _, t = qat.shape
    nkv = ka.shape[0]
    rep = nh // nkv
    hd = HEAD_DIM
    tq = tk = KV_TILE

    def body(qt_ref, k_ref, kt_ref, v_ref, dot_ref, ot_ref, lse_ref, dk_ref, dv_ref, dqt_ref):
        j = pl.program_id(1)
        i = pl.program_id(2)

        @pl.when(jnp.logical_and(j == 0, i == 0))
        def _():
            dqt_ref[...] = jnp.zeros_like(dqt_ref)

        @pl.when(i == 0)
        def _():
            dk_ref[...] = jnp.zeros_like(dk_ref)
            dv_ref[...] = jnp.zeros_like(dv_ref)

        k = k_ref[0]
        kt = kt_ref[0]
        v = v_ref[0, :, 0:hd]
        cols = pl.ds(pl.multiple_of(i * tq, tq), tq)
        dk = jnp.zeros((tk, 128), F32)
        dv = jnp.zeros((tk, hd), F32)
        products = lambda r: (_dot(k, qt_ref[r]), _dot(v, dot_ref[hd * r:hd * (r + 1), :]))
        nxt = products(0)
        for r in range(rep):
            st, dpt = nxt
            if r + 1 < rep:
                nxt = products(r + 1)
            heads = slice(hd * r, hd * (r + 1))
            qt = qt_ref[r]
            dot_r = dot_ref[heads, :]
            delta = jnp.sum(dot_r.astype(F32) * ot_ref[heads, :], axis=0, keepdims=True)
            pt = jnp.exp(st - lse_ref[0, r:r + 1, :])
            dst = (pt * (dpt - delta)).astype(BF16)
            dv = dv + _dot_nt(pt.astype(BF16), dot_r)
            dk = dk + _dot_nt(dst, qt)
            dqt_ref[heads, cols] += _dot(kt, dst)
        dk_ref[0] += dk
        dv_ref[0] += dv

    return _pcall(
        body, name="attn_bwd", grid=(nkv, t // tk, t // tq),
        in_specs=[pl.BlockSpec((rep, 128, tq), lambda g, j, i: (g, 0, i)),
                  pl.BlockSpec((1, tk, 128), lambda g, j, i: (g, j, 0)),
                  pl.BlockSpec((1, hd, tk), lambda g, j, i: (g, 0, j)),
                  pl.BlockSpec((1, tk, 128), lambda g, j, i: (g, j, 0)),
                  pl.BlockSpec((rep * hd, tq), lambda g, j, i: (g, i)),
                  pl.BlockSpec((rep * hd, tq), lambda g, j, i: (g, i)),
                  pl.BlockSpec((1, rep, tq), lambda g, j, i: (g, 0, i))],
        out_specs=[pl.BlockSpec((1, tk, 128), lambda g, j, i: (g, j, 0)),
                   pl.BlockSpec((1, tk, hd), lambda g, j, i: (g, j, 0)),
                   pl.BlockSpec((rep * hd, t), lambda g, j, i: (g, 0))],
        out_shape=[jax.ShapeDtypeStruct((nkv, t, 128), F32), jax.ShapeDtypeStruct((nkv, t, hd), F32),
                   jax.ShapeDtypeStruct((nh * hd, t), F32)],
        vmem=V7X_VMEM_LIMIT,
    )(qat, ka, kt, va, dot, ot, lse_row)


def _riding_exchange(refs, exchange, n_in, n_out, first_step, last_step):
    if exchange is None:
        return refs
    x_ref, out_ref = refs[n_in], refs[n_in + 1 + n_out]
    sems = refs[-3:]

    @pl.when(first_step)
    def _():
        _start_all(*_exchange_copies(x_ref, out_ref, *sems, exchange[1]))

    @pl.when(last_step)
    def _():
        _wait_all(*_exchange_copies(x_ref, out_ref, *sems, exchange[1]))

    return refs[:n_in] + refs[n_in + 1:n_in + 1 + n_out] + refs[n_in + 2 + n_out:-3]


def _segmented_scan(src_re, src_im, dst_re, dst_im, lam_re, lam_im, pow_re, pow_im, carry_re, carry_im,
                    end_re, end_im, in_re, in_im, lanes, descending, conj):
    tc = src_re.shape[0]
    seg = tc // 8
    width = lanes.size
    sign = -1.0 if conj else 1.0
    rows_of = lambda q: pl.ds(8 * (seg - 1 - q if descending else q), 8)
    lr = jnp.broadcast_to(lam_re[:, lanes], (8, width))
    li = jnp.broadcast_to(sign * lam_im[:, lanes], (8, width))
    xr = jnp.zeros((8, width), F32)
    xi = jnp.zeros((8, width), F32)
    for q in range(seg):
        rows = rows_of(q)
        xr, xi = (lr * xr - li * xi) + src_re[rows, lanes], (lr * xi + li * xr) + src_im[rows, lanes]
        dst_re[rows, lanes] = xr
        dst_im[rows, lanes] = xi
    end_re[:, lanes] = xr
    end_im[:, lanes] = xi
    sr = pow_re[seg - 1:seg, lanes]
    si = sign * pow_im[seg - 1:seg, lanes]
    cr = carry_re[:, lanes]
    ci = carry_im[:, lanes]
    for s in range(8):
        se = 7 - s if descending else s
        in_re[se:se + 1, lanes] = cr
        in_im[se:se + 1, lanes] = ci
        cr, ci = (end_re[se:se + 1, lanes] + (sr * cr - si * ci)), (end_im[se:se + 1, lanes] + (sr * ci + si * cr))
    carry_re[:, lanes] = cr
    carry_im[:, lanes] = ci
    ir = in_re[:, lanes]
    ii = in_im[:, lanes]
    for q in range(seg):
        rows = rows_of(q)
        pr = pow_re[q:q + 1, lanes]
        pi = sign * pow_im[q:q + 1, lanes]
        dst_re[rows, lanes] = dst_re[rows, lanes] + (pr * ir - pi * ii)
        dst_im[rows, lanes] = dst_im[rows, lanes] + (pr * ii + pi * ir)


def _diag_tiles(gn):
    rows_per_tile = DIAG_TILE // (SSM_STATE // SSM_GROUP)
    return [(slice(rows_per_tile * j, rows_per_tile * (j + 1)), slice(DIAG_TILE * j, DIAG_TILE * (j + 1)))
            for j in range(gn // DIAG_TILE)]


def _ssm_scan_fwd(ub, lam_re, lam_im, pow_re, pow_im, bb_re, bb_im, cc_re, cc_im, exchange=None):
    t, w = ub.shape
    gn = lam_re.shape[-1]
    tc = ROW_TILE
    cl = min(gn, SCAN_LANES)
    nblk = t // tc
    tiles = _diag_tiles(gn)

    def body(*refs):
        first = jnp.logical_and(pl.program_id(0) == 0, pl.program_id(1) == 0)
        last = jnp.logical_and(pl.program_id(0) == 1, pl.program_id(1) == nblk - 1)
        (u_ref, lr_ref, li_ref, pr_ref, pi_ref, br_ref, bi_ref, cr_ref, ci_ref, y_ref, xr_ref, xi_ref,
         bur_scr, bui_scr, cr_scr, ci_scr, er_scr, ei_scr, nr_scr, ni_scr) = _riding_exchange(
             refs, exchange, 9, 3, first, last)

        @pl.when(pl.program_id(1) == 0)
        def _():
            cr_scr[...] = jnp.zeros_like(cr_scr)
            ci_scr[...] = jnp.zeros_like(ci_scr)

        for rows, lanes in tiles:
            u_j = u_ref[:, rows]
            bur_scr[:, lanes] = _dot(u_j, br_ref[0, rows, lanes])
            bui_scr[:, lanes] = _dot(u_j, bi_ref[0, rows, lanes])
        for descending in (False, True):
            @pl.when(pl.program_id(0) == int(descending))
            def _(descending=descending):
                for c0 in range(0, gn, cl):
                    _segmented_scan(bur_scr, bui_scr, xr_ref.at[0], xi_ref.at[0], lr_ref.at[0], li_ref.at[0],
                                    pr_ref.at[0], pi_ref.at[0], cr_scr, ci_scr, er_scr, ei_scr, nr_scr, ni_scr,
                                    pl.ds(c0, cl), descending, conj=False)
        for rows, lanes in tiles:
            y_ref[0, :, rows] = (_dot(xr_ref[0, :, lanes].astype(BF16), cr_ref[0, lanes, rows])
                                 - _dot(xi_ref[0, :, lanes].astype(BF16), ci_ref[0, lanes, rows]))

    blk = lambda dd, i: jnp.where(dd == 0, i, nblk - 1 - i)
    row = lambda width: pl.BlockSpec((1, tc, width), lambda dd, i: (dd, blk(dd, i), 0))
    per_dir = lambda a, b: pl.BlockSpec((1, a, b), lambda dd, i: (dd, 0, 0))
    extra = exchange is not None
    return _pcall(
        body, name="ssm_scan_fwd", grid=(2, nblk),
        in_specs=[pl.BlockSpec((tc, w), lambda dd, i: (blk(dd, i), 0)), per_dir(1, gn), per_dir(1, gn),
                  per_dir(tc // 8, gn), per_dir(tc // 8, gn),
                  per_dir(w, gn), per_dir(w, gn), per_dir(gn, w), per_dir(gn, w)] + [_ANY] * extra,
        out_specs=[row(w), row(gn), row(gn)] + [_ANY] * extra,
        out_shape=[jax.ShapeDtypeStruct((2, t, w), F32), jax.ShapeDtypeStruct((2, t, gn), F32),
                   jax.ShapeDtypeStruct((2, t, gn), F32)] + ([_exchange_out_shape(*exchange)] if extra else []),
        scratch=[pltpu.VMEM((tc, gn), F32), pltpu.VMEM((tc, gn), F32), pltpu.VMEM((1, gn), F32),
                 pltpu.VMEM((1, gn), F32)] + [pltpu.VMEM((8, gn), F32)] * 4 + _EXCHANGE_SEMS * extra,
        vmem=V7X_VMEM_LIMIT,
    )(ub, lam_re, lam_im, pow_re, pow_im, bb_re, bb_im, cc_re, cc_im, *([exchange[0]] if extra else []))


def _ssm_scan_bwd(dyb, ub, xs_re, xs_im, lam_re, lam_im, pow_re, pow_im, cct_re, cct_im, bbt_re, bbt_im,
                  exchange=None):
    t, w = dyb.shape
    gn = lam_re.shape[-1]
    tc = ROW_TILE
    cl = min(gn, SCAN_LANES)
    nblk = t // tc
    tiles = _diag_tiles(gn)

    def body(*refs):
        i = pl.program_id(1)
        first = jnp.logical_and(pl.program_id(0) == 0, i == 0)
        last = jnp.logical_and(pl.program_id(0) == 1, i == nblk - 1)
        (dy_ref, u_ref, xr_ref, xi_ref, hr_ref, hi_ref, lr_ref, li_ref, pr_ref, pi_ref, ctr_ref, cti_ref, btr_ref,
         bti_ref, du_ref, dlr_ref, dli_ref, dbr_ref, dbi_ref, dcr_ref, dci_ref,
         gxr_scr, gxi_scr, cr_scr, ci_scr, ar_scr, ai_scr, er_scr, ei_scr, nr_scr, ni_scr) = _riding_exchange(
             refs, exchange, 14, 7, first, last)

        @pl.when(i == 0)
        def _():
            for ref in (cr_scr, ci_scr, ar_scr, ai_scr, dbr_ref, dbi_ref, dcr_ref, dci_ref):
                ref[...] = jnp.zeros_like(ref)

        for rows, lanes in tiles:
            dy_j = dy_ref[:, rows]
            gxr_scr[:, lanes] = _dot(dy_j, ctr_ref[0, rows, lanes])
            gxi_scr[:, lanes] = -_dot(dy_j, cti_ref[0, rows, lanes])
        first_block = i == nblk - 1
        sublane = lax.broadcasted_iota(jnp.int32, (8, 1), 0)

        def lam_gradient(state_descending):
            for c0 in range(0, gn, 512):
                lanes = pl.ds(c0, 512)
                if state_descending:
                    cur, prev, edge, src = pl.ds(0, tc - 8), pl.ds(8, tc - 8), pl.ds(tc - 8, 8), pl.ds(0, 8)
                    halo_at, halo_row, shift = 7, 0, 7
                else:
                    cur, prev, edge, src = pl.ds(8, tc - 8), pl.ds(0, tc - 8), pl.ds(0, 8), pl.ds(tc - 8, 8)
                    halo_at, halo_row, shift = 0, 7, 1
                halo_r = jnp.where(first_block, 0.0, hr_ref[0, halo_row:halo_row + 1, lanes])
                halo_i = jnp.where(first_block, 0.0, hi_ref[0, halo_row:halo_row + 1, lanes])
                xer = jnp.where(sublane == halo_at, halo_r, pltpu.roll(xr_ref[0, src, lanes], shift, 0))
                xei = jnp.where(sublane == halo_at, halo_i, pltpu.roll(xi_ref[0, src, lanes], shift, 0))
                gr, gi = gxr_scr[cur, lanes], gxi_scr[cur, lanes]
                xpr, xpi = xr_ref[0, prev, lanes], xi_ref[0, prev, lanes]
                ger, gei = gxr_scr[edge, lanes], gxi_scr[edge, lanes]
                ar_scr[:, lanes] += (jnp.sum(gr * xpr + gi * xpi, axis=0, keepdims=True)
                                     + jnp.sum(ger * xer + gei * xei, axis=0, keepdims=True))
                ai_scr[:, lanes] += (jnp.sum(gi * xpr - gr * xpi, axis=0, keepdims=True)
                                     + jnp.sum(gei * xer - ger * xei, axis=0, keepdims=True))

        for descending in (True, False):
            @pl.when(pl.program_id(0) == int(not descending))
            def _(descending=descending):
                for c0 in range(0, gn, cl):
                    _segmented_scan(gxr_scr, gxi_scr, gxr_scr, gxi_scr, lr_ref.at[0], li_ref.at[0], pr_ref.at[0],
                                    pi_ref.at[0], cr_scr, ci_scr, er_scr, ei_scr, nr_scr, ni_scr, pl.ds(c0, cl),
                                    descending, conj=True)
                lam_gradient(state_descending=not descending)
        dlr_ref[0] = ar_scr[...]
        dli_ref[0] = ai_scr[...]
        for rows, lanes in tiles:
            grb = gxr_scr[:, lanes].astype(BF16)
            gib = gxi_scr[:, lanes].astype(BF16)
            du_ref[0, :, rows] = _dot(grb, btr_ref[0, lanes, rows]) + _dot(gib, bti_ref[0, lanes, rows])
            u_j = u_ref[:, rows]
            dy_j = dy_ref[:, rows]
            dbr_ref[0, rows, :] += _dot_tn(u_j, grb)
            dbi_ref[0, rows, :] += _dot_tn(u_j, gib)
            dcr_ref[0, rows, :] += _dot_tn(dy_j, xr_ref[0, :, lanes].astype(BF16))
            dci_ref[0, rows, :] -= _dot_tn(dy_j, xi_ref[0, :, lanes].astype(BF16))

    blk = lambda dd, i: jnp.where(dd == 0, nblk - 1 - i, i)
    rev = lambda width: pl.BlockSpec((1, tc, width), lambda dd, i: (dd, blk(dd, i), 0))
    halo_blk = lambda dd, i: jnp.where(dd == 0, jnp.maximum(blk(dd, i) * (tc // 8) - 1, 0),
                                       jnp.minimum((blk(dd, i) + 1) * (tc // 8), t // 8 - 1))
    halo = pl.BlockSpec((1, 8, gn), lambda dd, i: (dd, halo_blk(dd, i), 0))
    per_dir = lambda a, b: pl.BlockSpec((1, a, b), lambda dd, i: (dd, 0, 0))
    extra = exchange is not None
    return _pcall(
        body, name="ssm_scan_bwd", grid=(2, nblk),
        in_specs=[pl.BlockSpec((tc, w), lambda dd, i: (blk(dd, i), 0)),
                  pl.BlockSpec((tc, w), lambda dd, i: (blk(dd, i), 0)), rev(gn), rev(gn), halo, halo,
                  per_dir(1, gn), per_dir(1, gn), per_dir(tc // 8, gn), per_dir(tc // 8, gn),
                  per_dir(w, gn), per_dir(w, gn), per_dir(gn, w), per_dir(gn, w)]
        + [_ANY] * extra,
        out_specs=[rev(w), per_dir(1, gn), per_dir(1, gn)] + [per_dir(w, DIAG_TILE)] * 4 + [_ANY] * extra,
        out_shape=[jax.ShapeDtypeStruct((2, t, w), F32), jax.ShapeDtypeStruct((2, 1, gn), F32),
                   jax.ShapeDtypeStruct((2, 1, gn), F32)] + [jax.ShapeDtypeStruct((2, w, DIAG_TILE), F32)] * 4
        + ([_exchange_out_shape(*exchange)] if extra else []),
        scratch=[pltpu.VMEM((tc, gn), F32), pltpu.VMEM((tc, gn), F32)] + [pltpu.VMEM((1, gn), F32)] * 4
        + [pltpu.VMEM((8, gn), F32)] * 4 + _EXCHANGE_SEMS * extra,
        vmem=V7X_VMEM_LIMIT,
    )(dyb, ub, xs_re, xs_im, xs_re, xs_im, lam_re, lam_im, pow_re, pow_im, cct_re, cct_im, bbt_re, bbt_im,
      *([exchange[0]] if extra else []))


def _matmul_tn(a, b, name, a_is_transposed=False, exchange=None):
    t, n = b.shape
    m = a.shape[0] if a_is_transposed else a.shape[1]
    bm, bn, tk = min(m, 1024), min(n, 1024), KV_TILE
    grid = (m // bm, n // bn, t // tk)

    def body(*refs):
        at = lambda step: functools.reduce(jnp.logical_and, [pl.program_id(ax) == step[ax] for ax in range(3)])
        a_ref, b_ref, o_ref = _riding_exchange(refs, exchange, 2, 1, at((0, 0, 0)), at([g - 1 for g in grid]))

        @pl.when(pl.program_id(2) == 0)
        def _():
            o_ref[...] = jnp.zeros_like(o_ref)

        mul = _dot if a_is_transposed else _dot_tn
        o_ref[...] += mul(a_ref[...].astype(BF16), b_ref[...].astype(BF16))

    a_spec = (pl.BlockSpec((bm, tk), lambda i, j, k: (i, k)) if a_is_transposed else
              pl.BlockSpec((tk, bm), lambda i, j, k: (k, i)))
    extra = exchange is not None
    out = _pcall(
        body, name=name, grid=grid,
        in_specs=[a_spec, pl.BlockSpec((tk, bn), lambda i, j, k: (k, j))] + [_ANY] * extra,
        out_specs=[pl.BlockSpec((bm, bn), lambda i, j, k: (i, j))] + [_ANY] * extra,
        out_shape=[jax.ShapeDtypeStruct((m, n), F32)] + ([_exchange_out_shape(*exchange)] if extra else []),
        scratch=_EXCHANGE_SEMS * extra, vmem=V7X_VMEM_LIMIT,
    )(a, b, *([exchange[0]] if extra else []))
    return out if extra else out[0]


def _reduce_adamw(gparts, p, m, v, name, exchange=None):
    rows, width = p.shape
    tr = max(k for k in range(16, 513, 16) if rows % k == 0)

    def body(*refs):
        i = pl.program_id(0)
        g_ref, p_ref, m_ref, v_ref, go_ref, d_ref, mo_ref, vo_ref = _riding_exchange(
            refs, exchange, 4, 4, i == 0, i == rows // tr - 1)
        g = g_ref[0].astype(F32)
        for k in range(1, N_DEV):
            g = g + g_ref[k].astype(F32)
        go_ref[...] = g
        mm = ADAM_B1 * m_ref[...] + (1.0 - ADAM_B1) * g
        vv = ADAM_B2 * v_ref[...] + (1.0 - ADAM_B2) * (g * g)
        m_hat = mm / (1.0 - ADAM_B1 ** ADAM_STEP)
        v_hat = vv / (1.0 - ADAM_B2 ** ADAM_STEP)
        d_ref[...] = -ADAM_LR * (m_hat / (jnp.sqrt(v_hat) + ADAM_EPS) + ADAM_WD * p_ref[...])
        mo_ref[...] = mm
        vo_ref[...] = vv

    spec = pl.BlockSpec((tr, width), lambda i: (i, 0))
    out = jax.ShapeDtypeStruct((rows, width), F32)
    extra = exchange is not None
    return _pcall(
        body, name=name, grid=(rows // tr,),
        in_specs=[pl.BlockSpec((N_DEV, tr, width), lambda i: (0, i, 0)), spec, spec, spec] + [_ANY] * extra,
        out_specs=[spec, spec, spec, spec] + [_ANY] * extra,
        out_shape=[out, out, out, out] + ([_exchange_out_shape(*exchange)] if extra else []),
        scratch=_EXCHANGE_SEMS * extra, vmem=V7X_VMEM_LIMIT,
    )(gparts, p, m, v, *([exchange[0]] if extra else []))


def _peer(k):
    x, y, c = lax.axis_index("x"), lax.axis_index("y"), lax.axis_index("c")
    return (x ^ ((k >> 2) & 1), y ^ ((k >> 1) & 1), c ^ (k & 1))


def _my_index():
    return 4 * lax.axis_index("x") + 2 * lax.axis_index("y") + lax.axis_index("c")


def _exchange_copies(x_ref, out_ref, send_sems, recv_sems, local_sem, scatter, first_sem=0):
    me = _my_index()
    local = pltpu.make_async_copy(x_ref.at[me] if scatter else x_ref, out_ref.at[me], local_sem)
    copies = []
    for k in range(1, N_DEV):
        peer = _peer(k)
        src = x_ref.at[4 * peer[0] + 2 * peer[1] + peer[2]] if scatter else x_ref
        copies.append(pltpu.make_async_remote_copy(
            src_ref=src, dst_ref=out_ref.at[me], send_sem=send_sems.at[first_sem + k - 1],
            recv_sem=recv_sems.at[first_sem + k - 1], device_id=peer, device_id_type=pl.DeviceIdType.MESH))
    return local, copies


def _start_all(local, copies):
    local.start()
    for cp in copies:
        cp.start()


def _wait_all(local, copies):
    for cp in copies:
        cp.wait_recv()
    for cp in copies:
        cp.wait_send()
    local.wait()


def _exchange_out_shape(x, scatter):
    return jax.ShapeDtypeStruct((N_DEV,) + tuple(x.shape[1:] if scatter else x.shape), x.dtype)


_EXCHANGE_SEMS = [pltpu.SemaphoreType.DMA((N_DEV - 1,)), pltpu.SemaphoreType.DMA((N_DEV - 1,)),
                  pltpu.SemaphoreType.DMA(())]


def _gather_two_level(x, name):
    def body(x_ref, out_ref, send_sems, recv_sems, local_sem):
        x, y, c = lax.axis_index("x"), lax.axis_index("y"), lax.axis_index("c")
        me, sibling = (x, y, c), (x, y, 1 - c)
        chips = [(1 - x, y), (x, 1 - y), (1 - x, 1 - y)]

        def copy(k, block, to, src=None):
            slot = out_ref.at[4 * block[0] + 2 * block[1] + block[2]]
            return pltpu.make_async_remote_copy(src_ref=slot if src is None else src, dst_ref=slot,
                                                send_sem=send_sems.at[k], recv_sem=recv_sems.at[k], device_id=to,
                                                device_id_type=pl.DeviceIdType.MESH)

        local = pltpu.make_async_copy(x_ref, out_ref.at[_my_index()], local_sem)
        local.start()
        first = [copy(0, me, sibling, src=x_ref)]
        first += [copy(1 + j, me, (*chip, c), src=x_ref) for j, chip in enumerate(chips)]
        for cp in first:
            cp.start()
        passed = [copy(4 + j, (*chip, c), sibling) for j, chip in enumerate(chips)]
        for j, chip in enumerate(chips):
            copy(1 + j, (*chip, c), me).wait_recv()
            passed[j].start()
        copy(0, sibling, me).wait_recv()
        for j, chip in enumerate(chips):
            copy(4 + j, (*chip, 1 - c), me).wait_recv()
        for cp in first + passed:
            cp.wait_send()
        local.wait()

    return pl.pallas_call(
        body, name=name, in_specs=[_ANY], out_specs=_ANY, out_shape=_exchange_out_shape(x, False),
        scratch_shapes=_EXCHANGE_SEMS,
    )(x)


def _to_shards(full, axis):
    r, c = full.shape
    if axis == 0:
        return full.reshape(N_DEV, r // N_DEV, c)
    return full.reshape(r, N_DEV, c // N_DEV).transpose(1, 0, 2)


def _from_shards(shards, axis):
    _, r, c = shards.shape
    if axis == 0:
        return shards.reshape(N_DEV * r, c)
    return shards.transpose(1, 0, 2).reshape(r, N_DEV * c)


def _pack_rows(parts, lead):
    flat = []
    for p in parts:
        p = p.reshape(p.shape[:lead] + (-1, PACK_W))
        pad = _round_up(p.shape[lead], 16) - p.shape[lead]
        flat.append(jnp.pad(p, [(0, 0)] * lead + [(0, pad), (0, 0)]) if pad else p)
    return jnp.concatenate(flat, axis=lead)


def _unpack_rows(packed, shapes):
    lead = packed.shape[:-2]
    out, off = [], 0
    for shp in shapes:
        rows = math.prod(shp) // PACK_W
        out.append(packed[..., off:off + rows, :].reshape(lead + tuple(shp)))
        off += _round_up(rows, 16)
    return out


def _pack_flat(parts):
    flat = jnp.concatenate([p.reshape(-1) for p in parts])
    n = flat.shape[0]
    flat = jnp.pad(flat, (0, _round_up(n, 16 * PACK_W) - n))
    return flat.reshape(-1, PACK_W)


def _unpack(packed, shapes):
    flat = packed.reshape(-1)
    out, off = [], 0
    for shp in shapes:
        n = math.prod(shp)
        out.append(flat[off:off + n].reshape(shp))
        off += n
    return out


def _ssm_discretize(a_re, a_im, log_dt, b_re, b_im):
    dt = jnp.exp(log_dt)[..., None]
    lam_re = jnp.minimum(a_re, EIG_RE_MAX)
    lam_im = a_im
    mag = jnp.exp(lam_re * dt)
    ang = lam_im * dt
    lb_re = mag * jnp.cos(ang)
    lb_im = mag * jnp.sin(ang)
    num_re = lb_re - 1.0
    num_im = lb_im
    den = lam_re * lam_re + lam_im * lam_im
    f_re = (num_re * lam_re + num_im * lam_im) / den
    f_im = (num_im * lam_re - num_re * lam_im) / den
    bb_re = f_re[..., None] * b_re - f_im[..., None] * b_im
    bb_im = f_re[..., None] * b_im + f_im[..., None] * b_re
    return lb_re, lb_im, bb_re, bb_im


def _ssm_powers(a_re, a_im, log_dt, count):
    dt = jnp.exp(log_dt)[:, None, :, None]
    k = jnp.arange(1, count + 1, dtype=F32)[None, :, None, None]
    mag = jnp.exp(k * (jnp.minimum(a_re, EIG_RE_MAX)[:, None] * dt))
    ang = k * (a_im[:, None] * dt)
    shape = (a_re.shape[0], count, -1)
    return (mag * jnp.cos(ang)).reshape(shape), (mag * jnp.sin(ang)).reshape(shape)


def _interleave(a, inverse=False):
    lead, (t, width) = a.shape[:-2], a.shape[-2:]
    seg = ROW_TILE // 8
    shape = lead + (t // ROW_TILE,) + ((seg, 8) if inverse else (8, seg)) + (width,)
    return jnp.swapaxes(a.reshape(shape), -3, -2).reshape(a.shape)


def _block_diag(blocks):
    two, g, a, b = blocks.shape
    tiled = jnp.tile(blocks.reshape(two, g * a, b), (1, 1, g))
    row_group = lax.broadcasted_iota(jnp.int32, (g * a, g * b), 0) // a
    col_group = lax.broadcasted_iota(jnp.int32, (g * a, g * b), 1) // b
    return jnp.where(row_group == col_group, tiled, 0.0).astype(BF16)


def _diag_blocks(tiles):
    two, w, _ = tiles.shape
    per = DIAG_TILE // SSM_STATE
    t6 = tiles.reshape(two, w // (per * SSM_GROUP), per, SSM_GROUP, per, SSM_STATE)
    return jnp.einsum("zjqpqn->zjqpn", t6).reshape(two, w // SSM_GROUP, SSM_GROUP, SSM_STATE)


def _rope_tables(t, n_valid):
    pos = jnp.arange(t)
    real = jnp.logical_and(pos >= N_META, pos < n_valid)
    idx = jnp.where(real, pos - N_META, 0)
    row_id = (idx // GRID_W).astype(F32)
    col_id = (idx % GRID_W).astype(F32)
    pairs_per_axis = HEAD_DIM // 4
    inv_freq = ROPE_THETA ** (-jnp.arange(pairs_per_axis, dtype=F32) / pairs_per_axis)
    ang = jnp.concatenate([row_id[:, None] * inv_freq, col_id[:, None] * inv_freq], axis=-1)
    ang = jnp.where(real[:, None], ang, 0.0)
    cos = jnp.repeat(jnp.cos(ang), 2, axis=-1)
    sin = jnp.sin(ang)
    sin = jnp.stack([-sin, sin], axis=-1).reshape(t, HEAD_DIM)
    return jnp.tile(cos, (1, 2)), jnp.tile(sin, (1, 2))


def _local_step(x, loss_target, big, small, comm=None):
    s_len, d = x.shape
    n_valid = s_len + N_META
    t = _round_up(n_valid, KV_TILE)
    du = d // 2
    groups = du // SSM_GROUP
    pad = t - n_valid

    x0 = jnp.concatenate([big["meta_tokens"].astype(F32), x, jnp.zeros((pad, d), F32)], axis=0)
    tgt = jnp.concatenate([jnp.zeros((N_META, d), F32), loss_target, jnp.zeros((pad, d), F32)], axis=0)
    cos, sin = _rope_tables(t, n_valid)
    g_mix = small["norm_mix_g"].reshape(1, d)
    g_mlp = small["norm_mlp_g"].reshape(1, d)
    g_fin = small["norm_final_g"].reshape(1, d)
    qg = jnp.tile(small["q_norm_g"].reshape(1, HEAD_DIM), (1, 2))
    kg = jnp.tile(small["k_norm_g"].reshape(1, HEAD_DIM), (1, 2))
    ssm_d = small["ssm_d"].reshape(1, du)
    b_glu = small["b_glu"].reshape(1, du)

    ssm_in = tuple(small[n][0] for n in ("ssm_a_re", "ssm_a_im", "ssm_log_dt", "ssm_b_re", "ssm_b_im"))
    (lb_re, lb_im, bbar_re, bbar_im), disc_vjp = jax.vjp(_ssm_discretize, *ssm_in)
    lam_re = lb_re.reshape(2, 1, groups * SSM_STATE)
    lam_im = lb_im.reshape(2, 1, groups * SSM_STATE)
    pow_re, pow_im = _ssm_powers(*ssm_in[0:3], ROW_TILE // 8)
    bb_re = _block_diag(bbar_re.transpose(0, 1, 3, 2))
    bb_im = _block_diag(bbar_im.transpose(0, 1, 3, 2))
    c_re, c_im = small["ssm_c_re"][0], small["ssm_c_im"][0]
    cct_re = _block_diag(c_re)
    cct_im = _block_diag(c_im)
    cc_re = cct_re.transpose(0, 2, 1)
    cc_im = cct_im.transpose(0, 2, 1)
    bbt_re = bb_re.transpose(0, 2, 1)
    bbt_im = bb_im.transpose(0, 2, 1)
    scan_w = (lam_re, lam_im, pow_re, pow_im)

    in_proj_args = (x0, g_mix, big["w_in"], qg, kg, cos, sin, n_valid)
    if comm is None:
        h, u, ub, qraw, kraw, qat, ka, kt, va, vta, gates = _in_proj_fwd(*in_proj_args)
    else:
        h, u, ub, qraw, kraw, qat, ka, kt, va, vta, gates, got = _in_proj_fwd(
            *in_proj_args, exchange=(comm["pack_weights"](MIXER_WEIGHTS), False))
        big = {**big, **comm["unpack_weights"](MIXER_WEIGHTS, got)}
    ub = _interleave(ub)
    if comm is None:
        y2, xs_re, xs_im = _ssm_scan_fwd(ub, *scan_w, bb_re, bb_im, cc_re, cc_im)
    else:
        y2, xs_re, xs_im, got = _ssm_scan_fwd(ub, *scan_w, bb_re, bb_im, cc_re, cc_im,
                                              exchange=(comm["pack_weights"](MLP_WEIGHTS), False))
        big = {**big, **comm["unpack_weights"](MLP_WEIGHTS, got)}
    y2 = _interleave(y2, inverse=True)
    yf, yb = y2[0], y2[1]
    yt_attn, lse = _attn_fwd(qat, ka, vta)
    mixer_w = (ssm_d, big["w_glu"], b_glu, big["w_ssm_proj"], big["w_attn_proj"], big["w_out"])
    x1 = _mixer_out_fwd(x0, u, yf, yb, yt_attn, gates, *mixer_w)

    dx1, loss8, dg_fin, dg_mlp, h2b, dab, hsqb, dx2b = _mlp_loss_fwd_bwd(
        x1, tgt, g_mlp, g_fin, big["w_mlp_in"], big["w_mlp_out"], n_valid)

    (dyb, dud, dyt_attn, dgates, zb, dglb, ysb, dasb, daab, mgb, dxb, d_ssm_d, d_b_glu) = _mixer_out_bwd(
        dx1, u, yf, yb, yt_attn, gates, *mixer_w)
    grads = {}
    grads["w_glu"] = _matmul_tn(zb, dglb, "grad_w_glu")
    grads["w_ssm_proj"] = _matmul_tn(ysb, dasb, "grad_w_ssm_proj")
    grads["w_attn_proj"] = _matmul_tn(yt_attn, daab, "grad_w_attn_proj", a_is_transposed=True)
    grads["w_out"] = _matmul_tn(mgb, dxb, "grad_w_out")
    grads["w_mlp_in"] = _matmul_tn(h2b, dab, "grad_w_mlp_in")
    grads["w_mlp_out"] = _matmul_tn(hsqb, dx2b, "grad_w_mlp_out")
    dk, dv, dqt = _attn_bwd(qat, ka, kt, va, dyt_attn, yt_attn, lse)
    scan_args = (_interleave(dyb), ub, xs_re, xs_im, *scan_w, cct_re, cct_im, bbt_re, bbt_im)
    if comm is None:
        late_grad_parts = None
        du2, dlam_re, dlam_im, dbr, dbi, dcr, dci = _ssm_scan_bwd(*scan_args)
    else:
        du2, dlam_re, dlam_im, dbr, dbi, dcr, dci, late_grad_parts = _ssm_scan_bwd(
            *scan_args, exchange=(comm["pack_grads"](LATE_WEIGHTS, grads), True))
    du2 = _interleave(du2, inverse=True)
    dx0, dproj, dg_mix, dqg, dkg = _in_proj_bwd(x0, dx1, dud, du2[0], du2[1], qraw, kraw, dqt, dk, dv, dgates,
                                                g_mix, big["w_in"], qg, kg, cos, sin)

    grads["meta_tokens"] = dx0[0:N_META]
    dbb_re = _diag_blocks(dbr).transpose(0, 1, 3, 2)
    dbb_im = _diag_blocks(dbi).transpose(0, 1, 3, 2)
    dc_re, dc_im = _diag_blocks(dcr), _diag_blocks(dci)
    shape_gn = (2, groups, SSM_STATE)
    d_a_re, d_a_im, d_log_dt, d_b_re, d_b_im = disc_vjp(
        (dlam_re.reshape(shape_gn), dlam_im.reshape(shape_gn), dbb_re, dbb_im))
    grads.update({
        "norm_mix_g": dg_mix, "ssm_a_re": d_a_re[None], "ssm_a_im": d_a_im[None], "ssm_log_dt": d_log_dt[None],
        "ssm_b_re": d_b_re[None], "ssm_b_im": d_b_im[None], "ssm_c_re": dc_re[None], "ssm_c_im": dc_im[None],
        "ssm_d": d_ssm_d, "b_glu": d_b_glu,
        "q_norm_g": dqg[:, 0:HEAD_DIM] + dqg[:, HEAD_DIM:128], "k_norm_g": dkg[:, 0:HEAD_DIM] + dkg[:, HEAD_DIM:128],
        "norm_mlp_g": dg_mlp, "norm_final_g": dg_fin.reshape(d),
    })
    if comm is None:
        small_grad_parts = None
        grads["w_in"] = _matmul_tn(h, dproj, "grad_w_in")
    else:
        grads["w_in"], small_grad_parts = _matmul_tn(h, dproj, "grad_w_in",
                                                     exchange=(comm["pack_small_grads"](grads), False))
    return loss8[0, 0], dx0[N_META:n_valid], grads, late_grad_parts, small_grad_parts


def kernel(x, meta_tokens, norm_mix_g, w_in, ssm_a_re, ssm_a_im, ssm_log_dt, ssm_b_re, ssm_b_im, ssm_c_re, ssm_c_im, ssm_d, w_glu, b_glu, q_norm_g, k_norm_g, w_ssm_proj, w_attn_proj, w_out, norm_mlp_g, w_mlp_in, w_mlp_out, norm_final_g, loss_target, m_meta_tokens, m_norm_mix_g, m_w_in, m_ssm_a_re, m_ssm_a_im, m_ssm_log_dt, m_ssm_b_re, m_ssm_b_im, m_ssm_c_re, m_ssm_c_im, m_ssm_d, m_w_glu, m_b_glu, m_q_norm_g, m_k_norm_g, m_w_ssm_proj, m_w_attn_proj, m_w_out, m_norm_mlp_g, m_w_mlp_in, m_w_mlp_out, m_norm_final_g, v_meta_tokens, v_norm_mix_g, v_w_in, v_ssm_a_re, v_ssm_a_im, v_ssm_log_dt, v_ssm_b_re, v_ssm_b_im, v_ssm_c_re, v_ssm_c_im, v_ssm_d, v_w_glu, v_b_glu, v_q_norm_g, v_k_norm_g, v_w_ssm_proj, v_w_attn_proj, v_w_out, v_norm_mlp_g, v_w_mlp_in, v_w_mlp_out, v_norm_final_g):
    w = dict(meta_tokens=meta_tokens, norm_mix_g=norm_mix_g, w_in=w_in, ssm_a_re=ssm_a_re, ssm_a_im=ssm_a_im, ssm_log_dt=ssm_log_dt, ssm_b_re=ssm_b_re, ssm_b_im=ssm_b_im, ssm_c_re=ssm_c_re, ssm_c_im=ssm_c_im, ssm_d=ssm_d, w_glu=w_glu, b_glu=b_glu, q_norm_g=q_norm_g, k_norm_g=k_norm_g, w_ssm_proj=w_ssm_proj, w_attn_proj=w_attn_proj, w_out=w_out, norm_mlp_g=norm_mlp_g, w_mlp_in=w_mlp_in, w_mlp_out=w_mlp_out, norm_final_g=norm_final_g)
    m = dict(meta_tokens=m_meta_tokens, norm_mix_g=m_norm_mix_g, w_in=m_w_in, ssm_a_re=m_ssm_a_re, ssm_a_im=m_ssm_a_im, ssm_log_dt=m_ssm_log_dt, ssm_b_re=m_ssm_b_re, ssm_b_im=m_ssm_b_im, ssm_c_re=m_ssm_c_re, ssm_c_im=m_ssm_c_im, ssm_d=m_ssm_d, w_glu=m_w_glu, b_glu=m_b_glu, q_norm_g=m_q_norm_g, k_norm_g=m_k_norm_g, w_ssm_proj=m_w_ssm_proj, w_attn_proj=m_w_attn_proj, w_out=m_w_out, norm_mlp_g=m_norm_mlp_g, w_mlp_in=m_w_mlp_in, w_mlp_out=m_w_mlp_out, norm_final_g=m_norm_final_g)
    v = dict(meta_tokens=v_meta_tokens, norm_mix_g=v_norm_mix_g, w_in=v_w_in, ssm_a_re=v_ssm_a_re, ssm_a_im=v_ssm_a_im, ssm_log_dt=v_ssm_log_dt, ssm_b_re=v_ssm_b_re, ssm_b_im=v_ssm_b_im, ssm_c_re=v_ssm_c_re, ssm_c_im=v_ssm_c_im, ssm_d=v_ssm_d, w_glu=v_w_glu, b_glu=v_b_glu, q_norm_g=v_q_norm_g, k_norm_g=v_k_norm_g, w_ssm_proj=v_w_ssm_proj, w_attn_proj=v_w_attn_proj, w_out=v_w_out, norm_mlp_g=v_norm_mlp_g, w_mlp_in=v_w_mlp_in, w_mlp_out=v_w_mlp_out, norm_final_g=v_norm_final_g)

    shard2d = {n: w[n].reshape(w[n].shape[-2:]) for n in BIG_WEIGHTS}

    meta_hi = shard2d["meta_tokens"].astype(BF16)
    meta_res = shard2d["meta_tokens"] - meta_hi.astype(F32)
    meta_mid = meta_res.astype(BF16)
    meta_lo = (meta_res - meta_mid.astype(F32)).astype(BF16)
    shapes_of = lambda names: [shard2d[n].shape for n in names]

    def full_weights(names, shards):
        return {n: s if n in BLOCK_WEIGHTS else _from_shards(s, BIG_SHARD_AXIS[n]) for n, s in zip(names, shards)}

    early = _gather_two_level(_pack_rows([meta_hi, meta_mid, meta_lo, shard2d["w_in"].astype(BF16)], 0),
                              "gather_early_weights")
    shards = _unpack_rows(early, [meta_hi.shape] * 3 + shapes_of(EARLY_WEIGHTS[1:]))
    meta = [_from_shards(s, 1).astype(F32) for s in shards[0:3]]
    big = {"meta_tokens": (meta[0] + meta[1]) + meta[2], **full_weights(EARLY_WEIGHTS[1:], shards[3:])}
    small = {n: w[n] for n in SMALL_WEIGHTS}
    pack_grads = lambda names, grads: _pack_rows(
        [_to_shards(grads[n], BIG_SHARD_AXIS[n]) for n in names], 1).astype(BF16)
    comm = {
        "pack_weights": lambda names: _pack_rows([shard2d[n].astype(BF16) for n in names], 0),
        "unpack_weights": lambda names, g: full_weights(names, _unpack_rows(g, shapes_of(names))),
        "pack_grads": pack_grads,
        "pack_small_grads": lambda grads: _pack_flat([grads[n] for n in SMALL_WEIGHTS]),
    }

    loss, grad_x, grads, late_parts, small_parts = _local_step(x[0], loss_target[0], big, small, comm)
    loss = lax.psum(loss, ("x", "y", "c"))

    pk = lambda names, src: _pack_rows([src[n].reshape(shard2d[n].shape) for n in names], 0)
    pe, pl_ = functools.partial(pk, EARLY_WEIGHTS), functools.partial(pk, LATE_WEIGHTS)
    *late_out, early_parts = _reduce_adamw(late_parts, pl_(w), pl_(m), pl_(v), "adamw_sharded_late",
                                           exchange=(pack_grads(EARLY_WEIGHTS, grads), True))
    early_out = _reduce_adamw(early_parts, pe(w), pe(m), pe(v), "adamw_sharded_early")
    small_shapes = [w[n].shape for n in SMALL_WEIGHTS]
    pf = lambda src: _pack_flat([src[n] for n in SMALL_WEIGHTS])
    small_out = _reduce_adamw(small_parts, pf(w), pf(m), pf(v), "adamw_replicated")

    results = []
    for kind in range(4):
        big_un = dict(zip(EARLY_WEIGHTS + LATE_WEIGHTS,
                          _unpack_rows(early_out[kind], shapes_of(EARLY_WEIGHTS))
                          + _unpack_rows(late_out[kind], shapes_of(LATE_WEIGHTS))))
        small_un = dict(zip(SMALL_WEIGHTS, _unpack(small_out[kind], small_shapes)))
        for n in ALL_WEIGHTS:
            results.append(big_un[n].reshape(w[n].shape) if n in big_un else small_un[n])
    return (loss, grad_x[None], *results)
```

```python
import functools
import math

import jax
import jax.numpy as jnp
from jax import lax
from jax.experimental import pallas as pl
from jax.experimental.pallas import tpu as pltpu

F32 = jnp.float32
BF16 = jnp.bfloat16

N_DEV = 8
N_META = 16
GRID_W = 64
SSM_GROUP = 16
SSM_STATE = 64
HEAD_DIM = 64
KV_REP = 4
ROPE_THETA = 10000.0
NORM_EPS = 1e-6
EIG_RE_MAX = -1e-4
ATTN_SCALE = HEAD_DIM ** -0.5

ADAM_LR = 0.001
ADAM_B1 = 0.9
ADAM_B2 = 0.999
ADAM_EPS = 1e-08
ADAM_WD = 0.01
ADAM_STEP = 10

ROW_TILE = 384
ROW_TILE_BWD = 384
QUERY_STRIP = 256
ONE_PASS_SLACK = 60.0
VT_ROWS = 80
MASK_BIAS = -1e30
SCAN_LANES = 512
DIAG_TILE = 256
KV_TILE = 768
PACK_W = 1024
V7X_VMEM_LIMIT = 56 * 1024 * 1024
NEG_BIG = -1e30

BIG_WEIGHTS = ("meta_tokens", "w_in", "w_glu", "w_ssm_proj", "w_attn_proj", "w_out", "w_mlp_in", "w_mlp_out")
BIG_SHARD_AXIS = {"meta_tokens": 1, "w_in": 1, "w_glu": 0, "w_ssm_proj": 1, "w_attn_proj": 0, "w_out": 0,
                  "w_mlp_in": 1, "w_mlp_out": 0}
BLOCK_WEIGHTS = ("w_in", "w_mlp_in", "w_mlp_out")
EARLY_WEIGHTS = ("meta_tokens", "w_in")
MIXER_WEIGHTS = ("w_glu", "w_ssm_proj", "w_attn_proj", "w_out")
MLP_WEIGHTS = ("w_mlp_in", "w_mlp_out")
LATE_WEIGHTS = MIXER_WEIGHTS + MLP_WEIGHTS
SMALL_WEIGHTS = ("norm_mix_g", "ssm_a_re", "ssm_a_im", "ssm_log_dt", "ssm_b_re", "ssm_b_im", "ssm_c_re",
                 "ssm_c_im", "ssm_d", "b_glu", "q_norm_g", "k_norm_g", "norm_mlp_g", "norm_final_g")
ALL_WEIGHTS = ("meta_tokens", "norm_mix_g", "w_in", "ssm_a_re", "ssm_a_im", "ssm_log_dt", "ssm_b_re", "ssm_b_im",
               "ssm_c_re", "ssm_c_im", "ssm_d", "w_glu", "b_glu", "q_norm_g", "k_norm_g", "w_ssm_proj",
               "w_attn_proj", "w_out", "norm_mlp_g", "w_mlp_in", "w_mlp_out", "norm_final_g")


def _round_up(n, m):
    return (n + m - 1) // m * m


def _pcall(body, *, name, grid, in_specs, out_specs, out_shape, scratch=(), vmem=None, **kw):
    params = pltpu.CompilerParams(dimension_semantics=("arbitrary",) * len(grid), vmem_limit_bytes=vmem)
    return pl.pallas_call(body, name=name, grid=grid, in_specs=in_specs, out_specs=out_specs, out_shape=out_shape,
                          scratch_shapes=list(scratch), compiler_params=params, **kw)


def _dot(a, b):
    return jnp.dot(a, b, preferred_element_type=F32)


def _dot_nt(a, b):
    return lax.dot_general(a, b, (((1,), (1,)), ((), ())), preferred_element_type=F32)


def _dot_tn(a, b):
    return lax.dot_general(a, b, (((0,), (0,)), ((), ())), preferred_element_type=F32)


def _full_spec(shape):
    nd = len(shape)
    return pl.BlockSpec(shape, lambda *_: (0,) * nd)


def _row_spec(tm, width):
    return pl.BlockSpec((tm, width), lambda i: (i, 0))


def _heads_spec(nh, tm):
    return pl.BlockSpec((nh, tm, HEAD_DIM), lambda i: (0, i, 0))


_ANY = pl.BlockSpec(memory_space=pl.ANY)


def _load_once(step, pairs, sem):
    @pl.when(step == 0)
    def _():
        copies = [pltpu.make_async_copy(src, dst, sem.at[k]) for k, (src, dst) in enumerate(pairs)]
        for cp in copies:
            cp.start()
        for cp in copies:
            cp.wait()


def _swap_pairs(x, even):
    n = x.shape[-1]
    return jnp.where(even, pltpu.roll(x, n - 1, 1), pltpu.roll(x, 1, 1))


def _head_mean(v):
    row = lax.broadcasted_iota(jnp.int32, (128, 128), 0) // HEAD_DIM
    col = lax.broadcasted_iota(jnp.int32, (128, 128), 1) // HEAD_DIM
    ones = jnp.where(row == col, 1.0, 0.0).astype(BF16)
    hi = v.astype(BF16)
    lo = (v - hi.astype(F32)).astype(BF16)
    return (_dot(hi, ones) + _dot(lo, ones)) * (1.0 / HEAD_DIM)


def _gelu(y):
    return 0.5 * y * (1.0 + lax.erf(y * (1.0 / math.sqrt(2.0))))


def _gelu_grad(y):
    return 0.5 * (1.0 + lax.erf(y * (1.0 / math.sqrt(2.0)))) + y * jnp.exp(-0.5 * y * y) * (1.0 / math.sqrt(2.0 * math.pi))


def _in_proj_fwd(x0, g_mix, w_in, qg, kg, cos, sin, n_valid, exchange=None):
    t, d = x0.shape
    tm = ROW_TILE
    du, dk = d // 2, d // 4
    nh, nkv = d // HEAD_DIM, d // HEAD_DIM // KV_REP
    bw = w_in.shape[-1]
    assert bw == du and dk * 2 == bw

    def body(*refs):
        i = pl.program_id(0)
        (x_ref, g_ref, w_hbm, qg_ref, kg_ref, c_ref, s_ref,
         h_ref, u_ref, ub_ref, qraw_ref, kraw_ref, qat_ref, ka_ref, kt_ref, va_ref, vta_ref, gates_ref,
         w_ref, sem) = _riding_exchange(refs, exchange, 7, 11, i == 0, i == t // tm - 1)
        _load_once(i, [(w_hbm, w_ref)], sem)
        x = x_ref[...]
        r = lax.rsqrt(jnp.mean(x * x, axis=-1, keepdims=True) + NORM_EPS)
        h = ((x * r) * g_ref[...]).astype(BF16)
        h_ref[...] = h
        u = _dot(h, w_ref[0])
        u_ref[...] = u
        ub_ref[...] = u.astype(BF16)
        lane = lax.broadcasted_iota(jnp.int32, (tm, 128), 1)
        lo = lane < HEAD_DIM
        even = (lane & 1) == 0
        aug = lane == HEAD_DIM
        c = c_ref[...]
        s = s_ref[...]
        row = i * tm + lax.broadcasted_iota(jnp.int32, (tm, 1), 0)
        one = jnp.where(aug, 1.0, 0.0)
        key_bias = jnp.where(jnp.logical_and(aug, row >= n_valid), MASK_BIAS, 0.0)

        def norm_rope(blk, g128):
            rr = lax.rsqrt(_head_mean(blk * blk) + NORM_EPS)
            qn = (blk * rr) * g128
            return qn * c + _swap_pairs(qn, even) * s

        def put_heads(rows_ref, cols_ref, first, pair, extra):
            for k, head in enumerate((pair, pltpu.roll(pair, HEAD_DIM, 1))):
                wide = jnp.where(lo, head, extra)
                if rows_ref is not None:
                    rows_ref[first + k] = wide.astype(BF16)
                cols_ref[first + k] = wide.T[0:cols_ref.shape[1], :].astype(BF16)

        for blk in range(2):
            qb = _dot(h, w_ref[1 + blk])
            qraw_ref[:, bw * blk:bw * (blk + 1)] = qb
            for a in range(bw // 128):
                put_heads(None, qat_ref, (bw // HEAD_DIM) * blk + 2 * a,
                          norm_rope(qb[:, 128 * a:128 * (a + 1)], qg_ref[...]) * ATTN_SCALE, one)
        kv = _dot(h, w_ref[3])
        kraw_ref[...] = kv[:, 0:dk]
        for a in range(nkv // 2):
            put_heads(ka_ref, kt_ref, 2 * a, norm_rope(kv[:, 128 * a:128 * (a + 1)], kg_ref[...]), key_bias)
            put_heads(va_ref, vta_ref, 2 * a, kv[:, dk + 128 * a:dk + 128 * (a + 1)], one)
        for blk in range(4):
            gates_ref[:, bw * blk:bw * (blk + 1)] = _dot(h, w_ref[4 + blk])

    heads = lambda n: pl.BlockSpec((n, tm, 128), lambda i: (0, i, 0))
    heads_t = lambda n, rows: pl.BlockSpec((n, rows, tm), lambda i: (0, 0, i))
    extra = exchange is not None
    return _pcall(
        body, name="in_proj_fwd", grid=(t // tm,),
        in_specs=[_row_spec(tm, d), _full_spec((1, d)), _ANY, _full_spec((1, 128)), _full_spec((1, 128)),
                  _row_spec(tm, 128), _row_spec(tm, 128)] + [_ANY] * extra,
        out_specs=[_row_spec(tm, d), _row_spec(tm, du), _row_spec(tm, du), _row_spec(tm, d), _row_spec(tm, dk),
                   heads_t(nh, 128), heads(nkv), heads_t(nkv, HEAD_DIM), heads(nkv), heads_t(nkv, VT_ROWS),
                   _row_spec(tm, 2 * d)] + [_ANY] * extra,
        out_shape=[jax.ShapeDtypeStruct((t, d), BF16), jax.ShapeDtypeStruct((t, du), F32),
                   jax.ShapeDtypeStruct((t, du), BF16), jax.ShapeDtypeStruct((t, d), F32),
                   jax.ShapeDtypeStruct((t, dk), F32), jax.ShapeDtypeStruct((nh, 128, t), BF16),
                   jax.ShapeDtypeStruct((nkv, t, 128), BF16), jax.ShapeDtypeStruct((nkv, HEAD_DIM, t), BF16),
                   jax.ShapeDtypeStruct((nkv, t, 128), BF16), jax.ShapeDtypeStruct((nkv, VT_ROWS, t), BF16),
                   jax.ShapeDtypeStruct((t, 2 * d), F32)] + ([_exchange_out_shape(*exchange)] if extra else []),
        scratch=[pltpu.VMEM((N_DEV, d, bw), BF16), pltpu.SemaphoreType.DMA((1,))] + _EXCHANGE_SEMS * extra,
        vmem=V7X_VMEM_LIMIT,
    )(x0, g_mix, w_in, qg, kg, cos, sin, *([exchange[0]] if extra else []))


def _mixer_values(u, yf, yb, yt_attn, gates, d_ref, wg_ref, bg_ref, ps_ref, pa_ref, d):
    y = (u * d_ref[...] + yf) + yb
    z = _gelu(y)
    sg = jax.nn.sigmoid(_dot(z.astype(BF16), wg_ref[...]) + bg_ref[...])
    y_ssm = z * sg
    a_ssm = _dot(y_ssm.astype(BF16), ps_ref[...])
    a_attn = _dot_tn(yt_attn.astype(BF16), pa_ref[...])
    s_ssm = jax.nn.sigmoid(gates[:, 0:d])
    s_attn = jax.nn.sigmoid(gates[:, d:2 * d])
    merged = s_ssm * a_ssm + s_attn * a_attn
    return y, z, sg, y_ssm, a_ssm, a_attn, s_ssm, s_attn, merged


def _mixer_out_fwd(x0, u, yf, yb, y_attn, gates, ssm_d, w_glu, b_glu, p_ssm, p_attn, w_out):
    t, d = x0.shape
    tm = ROW_TILE
    du = d // 2

    def body(x_ref, u_ref, yf_ref, yb_ref, ya_ref, gt_ref, d_ref, wg_ref, bg_ref, ps_ref, pa_ref, wo_ref, x1_ref):
        vals = _mixer_values(u_ref[...], yf_ref[...], yb_ref[...], ya_ref[...], gt_ref[...],
                             d_ref, wg_ref, bg_ref, ps_ref, pa_ref, d)
        merged = vals[-1]
        x1_ref[...] = x_ref[...] + _dot(merged.astype(BF16), wo_ref[...])

    return _pcall(
        body, name="mixer_out_fwd", grid=(t // tm,),
        in_specs=[_row_spec(tm, d), _row_spec(tm, du), _row_spec(tm, du), _row_spec(tm, du),
                  pl.BlockSpec((d, tm), lambda i: (0, i)), _row_spec(tm, 2 * d), _full_spec((1, du)), _full_spec((du, du)), _full_spec((1, du)),
                  _full_spec((du, d)), _full_spec((d, d)), _full_spec((d, d))],
        out_specs=_row_spec(tm, d), out_shape=jax.ShapeDtypeStruct((t, d), F32), vmem=V7X_VMEM_LIMIT,
    )(x0, u, yf, yb, y_attn, gates, ssm_d, w_glu, b_glu, p_ssm, p_attn, w_out)


def _mlp_loss_fwd_bwd(x1, target, g_mlp, g_fin, w1, w2, n_valid):
    t, d = x1.shape
    tm = ROW_TILE_BWD
    dff = 4 * d
    nfc, _, fc = w1.shape

    def body(x_ref, tg_ref, gm_ref, gf_ref, w1_hbm, w2_hbm,
             dx1_ref, loss_ref, dgf_ref, dgm_ref, h2_ref, da_ref, hsq_ref, dx2b_ref,
             w1_ref, w2_ref, relu_ref, sem):
        i = pl.program_id(0)
        _load_once(i, [(w1_hbm, w1_ref), (w2_hbm, w2_ref)], sem)

        @pl.when(i == 0)
        def _():
            loss_ref[...] = jnp.zeros_like(loss_ref)
            dgf_ref[...] = jnp.zeros_like(dgf_ref)
            dgm_ref[...] = jnp.zeros_like(dgm_ref)

        x1v = x_ref[...]
        r1 = lax.rsqrt(jnp.mean(x1v * x1v, axis=-1, keepdims=True) + NORM_EPS)
        xh1 = x1v * r1
        h2b = (xh1 * gm_ref[...]).astype(BF16)
        h2_ref[...] = h2b
        acc = jnp.zeros((tm, d), F32)
        for c in range(nfc):
            a = jnp.maximum(_dot(h2b, w1_ref[c]), 0.0)
            relu_ref[:, fc * c:fc * (c + 1)] = a
            hs = (a * a).astype(BF16)
            hsq_ref[:, fc * c:fc * (c + 1)] = hs
            acc = acc + _dot(hs, w2_ref[c])
        x2 = x1v + acc
        r2 = lax.rsqrt(jnp.mean(x2 * x2, axis=-1, keepdims=True) + NORM_EPS)
        xh2 = x2 * r2
        out = xh2 * gf_ref[...]
        row = i * tm + lax.broadcasted_iota(jnp.int32, (tm, 1), 0)
        valid = jnp.logical_and(row >= N_META, row < n_valid)
        diff = jnp.where(valid, out - tg_ref[...], 0.0)
        loss_ref[...] += 0.5 * jnp.sum(jnp.sum(diff * diff, axis=-1, keepdims=True) * (1.0 / d))
        dout = diff * (1.0 / d)
        dgf_ref[...] += jnp.sum(dout * xh2, axis=0, keepdims=True)
        dxh2 = dout * gf_ref[...]
        dx2 = r2 * (dxh2 - xh2 * jnp.mean(dxh2 * xh2, axis=-1, keepdims=True))
        dx2b = dx2.astype(BF16)
        dx2b_ref[...] = dx2b
        dh2 = jnp.zeros((tm, d), F32)
        for c in range(nfc):
            dhs = _dot_nt(dx2b, w2_ref[c])
            da = (dhs * (2.0 * relu_ref[:, fc * c:fc * (c + 1)])).astype(BF16)
            da_ref[:, fc * c:fc * (c + 1)] = da
            dh2 = dh2 + _dot_nt(da, w1_ref[c])
        dgm_ref[...] += jnp.sum(dh2 * xh1, axis=0, keepdims=True)
        dxh1 = dh2 * gm_ref[...]
        dx1_ref[...] = dx2 + r1 * (dxh1 - xh1 * jnp.mean(dxh1 * xh1, axis=-1, keepdims=True))

    return _pcall(
        body, name="mlp_loss_fwd_bwd", grid=(t // tm,),
        in_specs=[_row_spec(tm, d), _row_spec(tm, d), _full_spec((1, d)), _full_spec((1, d)), _ANY, _ANY],
        out_specs=[_row_spec(tm, d), _full_spec((8, 128)), _full_spec((1, d)), _full_spec((1, d)),
                   _row_spec(tm, d), _row_spec(tm, dff), _row_spec(tm, dff), _row_spec(tm, d)],
        out_shape=[jax.ShapeDtypeStruct((t, d), F32), jax.ShapeDtypeStruct((8, 128), F32),
                   jax.ShapeDtypeStruct((1, d), F32), jax.ShapeDtypeStruct((1, d), F32),
                   jax.ShapeDtypeStruct((t, d), BF16), jax.ShapeDtypeStruct((t, dff), BF16),
                   jax.ShapeDtypeStruct((t, dff), BF16), jax.ShapeDtypeStruct((t, d), BF16)],
        scratch=[pltpu.VMEM((nfc, d, fc), BF16), pltpu.VMEM((nfc, fc, d), BF16), pltpu.VMEM((tm, dff), F32),
                 pltpu.SemaphoreType.DMA((2,))],
        vmem=V7X_VMEM_LIMIT,
    )(x1, target, g_mlp, g_fin, w1, w2)


def _mixer_out_bwd(dx1, u, yf, yb, yt_attn, gates, ssm_d, w_glu, b_glu, p_ssm, p_attn, w_out):
    t, d = dx1.shape
    tm = ROW_TILE_BWD
    du = d // 2

    def body(dx_ref, u_ref, yf_ref, yb_ref, yt_ref, gt_ref, d_ref, wg_ref, bg_ref, ps_ref, pa_ref, wo_ref,
             dyb_ref, dud_ref, dyat_ref, dgates_ref, zb_ref, dglb_ref, ysb_ref, dasb_ref, daab_ref,
             mgb_ref, dxb_ref, dd_ref, dbg_ref):
        i = pl.program_id(0)

        @pl.when(i == 0)
        def _():
            dd_ref[...] = jnp.zeros_like(dd_ref)
            dbg_ref[...] = jnp.zeros_like(dbg_ref)

        uv = u_ref[...]
        y, z, sg, y_ssm, a_ssm, a_attn, s_ssm, s_attn, merged = _mixer_values(
            uv, yf_ref[...], yb_ref[...], yt_ref[...], gt_ref[...], d_ref, wg_ref, bg_ref, ps_ref, pa_ref, d)
        dxb = dx_ref[...].astype(BF16)
        dxb_ref[...] = dxb
        mgb_ref[...] = merged.astype(BF16)
        dmerged = _dot_nt(dxb, wo_ref[...])
        dgates_ref[:, 0:d] = (dmerged * a_ssm * (s_ssm * (1.0 - s_ssm))).astype(BF16)
        dgates_ref[:, d:2 * d] = (dmerged * a_attn * (s_attn * (1.0 - s_attn))).astype(BF16)
        da_ssm = (dmerged * s_ssm).astype(BF16)
        da_attn = (dmerged * s_attn).astype(BF16)
        dasb_ref[...] = da_ssm
        daab_ref[...] = da_attn
        ysb_ref[...] = y_ssm.astype(BF16)
        dy_ssm = _dot_nt(da_ssm, ps_ref[...])
        dyat_ref[...] = _dot_nt(pa_ref[...], da_attn).astype(BF16)
        dgl = dy_ssm * z * (sg * (1.0 - sg))
        dglb = dgl.astype(BF16)
        dglb_ref[...] = dglb
        zb_ref[...] = z.astype(BF16)
        dbg_ref[...] += jnp.sum(dgl, axis=0, keepdims=True)
        dz = dy_ssm * sg + _dot_nt(dglb, wg_ref[...])
        dy = dz * _gelu_grad(y)
        dyb_ref[...] = dy.astype(BF16)
        dd_ref[...] += jnp.sum(dy * uv, axis=0, keepdims=True)
        dud_ref[...] = dy * d_ref[...]

    bf = lambda w: jax.ShapeDtypeStruct((t, w), BF16)
    return _pcall(
        body, name="mixer_out_bwd", grid=(t // tm,),
        in_specs=[_row_spec(tm, d), _row_spec(tm, du), _row_spec(tm, du), _row_spec(tm, du),
                  pl.BlockSpec((d, tm), lambda i: (0, i)),
                  _row_spec(tm, 2 * d), _full_spec((1, du)), _full_spec((du, du)), _full_spec((1, du)),
                  _full_spec((du, d)), _full_spec((d, d)), _full_spec((d, d))],
        out_specs=[_row_spec(tm, du), _row_spec(tm, du), pl.BlockSpec((d, tm), lambda i: (0, i)),
                   _row_spec(tm, 2 * d), _row_spec(tm, du), _row_spec(tm, du), _row_spec(tm, du), _row_spec(tm, d),
                   _row_spec(tm, d), _row_spec(tm, d), _row_spec(tm, d), _full_spec((1, du)), _full_spec((1, du))],
        out_shape=[bf(du), jax.ShapeDtypeStruct((t, du), F32), jax.ShapeDtypeStruct((d, t), BF16), bf(2 * d),
                   bf(du), bf(du), bf(du), bf(d), bf(d), bf(d), bf(d),
                   jax.ShapeDtypeStruct((1, du), F32), jax.ShapeDtypeStruct((1, du), F32)],
        vmem=V7X_VMEM_LIMIT,
    )(dx1, u, yf, yb, yt_attn, gates, ssm_d, w_glu, b_glu, p_ssm, p_attn, w_out)


def _in_proj_bwd(x0, dx1, dud, duf, dub, qraw, kraw, dq, dk, dv, dgates, g_mix, w_in, qg, kg, cos, sin):
    t, d = x0.shape
    tm = ROW_TILE_BWD
    du, dkw = d // 2, d // 4
    nh, nkv = d // HEAD_DIM, d // HEAD_DIM // KV_REP
    bw = w_in.shape[-1]
    o_q, o_k, o_v, o_g = du, du + d, du + d + dkw, 2 * d

    def body(x_ref, dx1_ref, dud_ref, duf_ref, dub_ref, qraw_ref, kraw_ref, dq_ref, dk_ref, dv_ref, dgt_ref,
             g_ref, w_hbm, qg_ref, kg_ref, c_ref, s_ref,
             dx0_ref, dproj_ref, dgm_ref, dqg_ref, dkg_ref,
             w_ref, kv_ref, sem):
        i = pl.program_id(0)
        _load_once(i, [(w_hbm, w_ref)], sem)

        @pl.when(i == 0)
        def _():
            dgm_ref[...] = jnp.zeros_like(dgm_ref)
            dqg_ref[...] = jnp.zeros_like(dqg_ref)
            dkg_ref[...] = jnp.zeros_like(dkg_ref)

        lane = lax.broadcasted_iota(jnp.int32, (tm, 128), 1)
        lo = lane < HEAD_DIM
        even = (lane & 1) == 0
        c = c_ref[...]
        s = s_ref[...]

        def norm_rope_bwd(dout, raw, g128):
            rr = lax.rsqrt(_head_mean(raw * raw) + NORM_EPS)
            xh = raw * rr
            dqn = dout * c + _swap_pairs(dout * s, even)
            dg = jnp.sum(dqn * xh, axis=0, keepdims=True)
            tt = dqn * g128
            return rr * (tt - xh * _head_mean(tt * xh)), dg

        dub_tot = (dud_ref[...] + duf_ref[...]) + dub_ref[...]
        dproj_ref[:, 0:du] = dub_tot.astype(BF16)
        dqg = jnp.zeros((1, 128), F32)
        for a in range(nh // 2):
            sl = slice(128 * a, 128 * (a + 1))
            draw, dg = norm_rope_bwd(dq_ref[sl, :].T * ATTN_SCALE, qraw_ref[:, sl], qg_ref[...])
            dqg = dqg + dg
            dproj_ref[:, o_q + 128 * a:o_q + 128 * (a + 1)] = draw.astype(BF16)
        dqg_ref[...] += dqg
        for hh in range(nkv):
            kv_ref[:, HEAD_DIM * hh:HEAD_DIM * (hh + 1)] = dk_ref[hh, :, 0:HEAD_DIM]
        dkg = jnp.zeros((1, 128), F32)
        for a in range(nkv // 2):
            sl = slice(128 * a, 128 * (a + 1))
            draw, dg = norm_rope_bwd(kv_ref[:, sl], kraw_ref[:, sl], kg_ref[...])
            dkg = dkg + dg
            dproj_ref[:, o_k + 128 * a:o_k + 128 * (a + 1)] = draw.astype(BF16)
        dkg_ref[...] += dkg
        for hh in range(nkv):
            kv_ref[:, HEAD_DIM * hh:HEAD_DIM * (hh + 1)] = dv_ref[hh]
        dproj_ref[:, o_v:o_g] = kv_ref[...].astype(BF16)
        dproj_ref[:, o_g:4 * d] = dgt_ref[...]
        dh = jnp.zeros((tm, d), F32)
        for blk in range(N_DEV):
            dh = dh + _dot_nt(dproj_ref[:, bw * blk:bw * (blk + 1)], w_ref[blk])
        x = x_ref[...]
        r = lax.rsqrt(jnp.mean(x * x, axis=-1, keepdims=True) + NORM_EPS)
        xh0 = x * r
        dgm_ref[...] += jnp.sum(dh * xh0, axis=0, keepdims=True)
        dxh = dh * g_ref[...]
        dx0_ref[...] = dx1_ref[...] + r * (dxh - xh0 * jnp.mean(dxh * xh0, axis=-1, keepdims=True))

    return _pcall(
        body, name="in_proj_bwd", grid=(t // tm,),
        in_specs=[_row_spec(tm, d), _row_spec(tm, d), _row_spec(tm, du), _row_spec(tm, du), _row_spec(tm, du),
                  _row_spec(tm, d), _row_spec(tm, dkw), pl.BlockSpec((d, tm), lambda i: (0, i)),
                  pl.BlockSpec((nkv, tm, 128), lambda i: (0, i, 0)), _heads_spec(nkv, tm),
                  _row_spec(tm, 2 * d), _full_spec((1, d)), _ANY, _full_spec((1, 128)), _full_spec((1, 128)),
                  _row_spec(tm, 128), _row_spec(tm, 128)],
        out_specs=[_row_spec(tm, d), _row_spec(tm, 4 * d), _full_spec((1, d)), _full_spec((1, 128)),
                   _full_spec((1, 128))],
        out_shape=[jax.ShapeDtypeStruct((t, d), F32), jax.ShapeDtypeStruct((t, 4 * d), BF16),
                   jax.ShapeDtypeStruct((1, d), F32), jax.ShapeDtypeStruct((1, 128), F32),
                   jax.ShapeDtypeStruct((1, 128), F32)],
        scratch=[pltpu.VMEM((N_DEV, d, bw), BF16), pltpu.VMEM((tm, dkw), F32), pltpu.SemaphoreType.DMA((1,))],
        vmem=V7X_VMEM_LIMIT,
    )(x0, dx1, dud, duf, dub, qraw, kraw, dq, dk, dv, dgates, g_mix, w_in, qg, kg, cos, sin)


def _attn_fwd(qat, ka, vta):
    nh, _, t = qat.shape
    nkv = ka.shape[0]
    rep = nh // nkv
    hd = HEAD_DIM
    vr = vta.shape[1]
    tq = tk = KV_TILE

    def body(qt_ref, k_ref, vt_ref, ot_ref, lse_ref, m_scr, acc_scr, excess_scr):
        j = pl.program_id(2)
        src = j % 2
        dst = 1 - src

        @pl.when(j == 0)
        def _():
            m_scr[0] = jnp.full(m_scr.shape[1:], NEG_BIG, F32)
            acc_scr[0] = jnp.zeros(acc_scr.shape[1:], F32)
            excess_scr[...] = jnp.full(excess_scr.shape, -NEG_BIG, F32)

        k = k_ref[0]
        vt = vt_ref[0]
        strips = [(r, c) for r in range(rep) for c in range(0, tq, QUERY_STRIP)]
        scores = lambda r, c: _dot(k, qt_ref[r, :, c:c + QUERY_STRIP])

        def sweep(one_pass):
            def add_values(r, cols, before, after, pt):
                acc = acc_scr[src, r, :, cols]
                acc_scr[dst, r, :, cols] = after * ((acc if before is None else before * acc) + _dot(vt, pt))

            ahead = [scores(*strips[0]), scores(*strips[1])]
            pending = None
            excess = jnp.full((1, QUERY_STRIP), NEG_BIG, F32)
            for n, (r, c) in enumerate(strips):
                st = ahead.pop(0)
                if n + 2 < len(strips):
                    ahead.append(scores(*strips[n + 2]))
                cols = slice(c, c + QUERY_STRIP)
                m_prev = m_scr[src, r, :, cols]
                if one_pass:
                    pt = jnp.exp(st - m_prev).astype(BF16)
                    tile_max = jnp.max(st, axis=0, keepdims=True)
                    m_next = jnp.maximum(m_prev, tile_max)
                    excess = jnp.maximum(excess, tile_max - m_prev)
                    factors = (None, jnp.exp(m_prev - m_next))
                else:
                    m_next = jnp.maximum(m_prev, jnp.max(st, axis=0, keepdims=True))
                    pt = jnp.exp(st - m_next).astype(BF16)
                    factors = (jnp.exp(m_prev - m_next), 1.0)
                m_scr[dst, r, :, cols] = m_next
                if pending is not None:
                    add_values(*pending)
                pending = (r, cols, *factors, pt)
            add_values(*pending)
            return excess

        @pl.when(j > 0)
        def _():
            excess_scr[...] = sweep(one_pass=True)

        @pl.when(jnp.max(excess_scr[...]) > ONE_PASS_SLACK)
        def _():
            sweep(one_pass=False)

        @pl.when(j == pl.num_programs(2) - 1)
        def _():
            for r in range(rep):
                l = acc_scr[dst, r, hd:hd + 1, :]
                ot_ref[hd * r:hd * (r + 1), :] = acc_scr[dst, r, 0:hd, :] / l
                lse_ref[0, r:r + 1, :] = m_scr[dst, r] + jnp.log(l)

    return _pcall(
        body, name="attn_fwd", grid=(nkv, t // tq, t // tk),
        in_specs=[pl.BlockSpec((rep, 128, tq), lambda g, i, j: (g, 0, i)),
                  pl.BlockSpec((1, tk, 128), lambda g, i, j: (g, j, 0)),
                  pl.BlockSpec((1, vr, tk), lambda g, i, j: (g, 0, j))],
        out_specs=[pl.BlockSpec((rep * hd, tq), lambda g, i, j: (g, i)),
                   pl.BlockSpec((1, rep, tq), lambda g, i, j: (g, 0, i))],
        out_shape=[jax.ShapeDtypeStruct((nh * hd, t), F32), jax.ShapeDtypeStruct((nkv, rep, t), F32)],
        scratch=[pltpu.VMEM((2, rep, 1, tq), F32), pltpu.VMEM((2, rep, vr, tq), F32),
                 pltpu.VMEM((1, QUERY_STRIP), F32)],
        vmem=V7X_VMEM_LIMIT,
    )(qat, ka, vta)


def _attn_bwd(qat, ka, kt, va, dot, ot, lse_row):
    nh, _, t = qat.shape
    nkv = ka.shape[0]
    rep = nh // nkv
    hd = HEAD_DIM
    tq = tk = KV_TILE

    def body(qt_ref, k_ref, kt_ref, v_ref, dot_ref, ot_ref, lse_ref, dk_ref, dv_ref, dqt_ref):
        j = pl.program_id(1)
        i = pl.program_id(2)

        @pl.when(jnp.logical_and(j == 0, i == 0))
        def _():
            dqt_ref[...] = jnp.zeros_like(dqt_ref)

        @pl.when(i == 0)
        def _():
            dk_ref[...] = jnp.zeros_like(dk_ref)
            dv_ref[...] = jnp.zeros_like(dv_ref)

        k = k_ref[0]
        kt = kt_ref[0]
        v = v_ref[0, :, 0:hd]
        cols = pl.ds(pl.multiple_of(i * tq, tq), tq)
        dk = jnp.zeros((tk, 128), F32)
        dv = jnp.zeros((tk, hd), F32)
        products = lambda r: (_dot(k, qt_ref[r]), _dot(v, dot_ref[hd * r:hd * (r + 1), :]))
        nxt = products(0)
        for r in range(rep):
            st, dpt = nxt
            if r + 1 < rep:
                nxt = products(r + 1)
            heads = slice(hd * r, hd * (r + 1))
            qt = qt_ref[r]
            dot_r = dot_ref[heads, :]
            delta = jnp.sum(dot_r.astype(F32) * ot_ref[heads, :], axis=0, keepdims=True)
            pt = jnp.exp(st - lse_ref[0, r:r + 1, :])
            dst = (pt * (dpt - delta)).astype(BF16)
            dv = dv + _dot_nt(pt.astype(BF16), dot_r)
            dk = dk + _dot_nt(dst, qt)
            dqt_ref[heads, cols] += _dot(kt, dst)
        dk_ref[0] += dk
        dv_ref[0] += dv

    return _pcall(
        body, name="attn_bwd", grid=(nkv, t // tk, t // tq),
        in_specs=[pl.BlockSpec((rep, 128, tq), lambda g, j, i: (g, 0, i)),
                  pl.BlockSpec((1, tk, 128), lambda g, j, i: (g, j, 0)),
                  pl.BlockSpec((1, hd, tk), lambda g, j, i: (g, 0, j)),
                  pl.BlockSpec((1, tk, 128), lambda g, j, i: (g, j, 0)),
                  pl.BlockSpec((rep * hd, tq), lambda g, j, i: (g, i)),
                  pl.BlockSpec((rep * hd, tq), lambda g, j, i: (g, i)),
                  pl.BlockSpec((1, rep, tq), lambda g, j, i: (g, 0, i))],
        out_specs=[pl.BlockSpec((1, tk, 128), lambda g, j, i: (g, j, 0)),
                   pl.BlockSpec((1, tk, hd), lambda g, j, i: (g, j, 0)),
                   pl.BlockSpec((rep * hd, t), lambda g, j, i: (g, 0))],
        out_shape=[jax.ShapeDtypeStruct((nkv, t, 128), F32), jax.ShapeDtypeStruct((nkv, t, hd), F32),
                   jax.ShapeDtypeStruct((nh * hd, t), F32)],
        vmem=V7X_VMEM_LIMIT,
    )(qat, ka, kt, va, dot, ot, lse_row)


def _riding_exchange(refs, exchange, n_in, n_out, first_step, last_step):
    if exchange is None:
        return refs
    x_ref, out_ref = refs[n_in], refs[n_in + 1 + n_out]
    sems = refs[-3:]

    @pl.when(first_step)
    def _():
        _start_all(*_exchange_copies(x_ref, out_ref, *sems, exchange[1]))

    @pl.when(last_step)
    def _():
        _wait_all(*_exchange_copies(x_ref, out_ref, *sems, exchange[1]))

    return refs[:n_in] + refs[n_in + 1:n_in + 1 + n_out] + refs[n_in + 2 + n_out:-3]


def _segmented_scan(src_re, src_im, dst_re, dst_im, lam_re, lam_im, pow_re, pow_im, carry_re, carry_im,
                    end_re, end_im, in_re, in_im, lanes, descending, conj):
    tc = src_re.shape[0]
    seg = tc // 8
    width = lanes.size
    sign = -1.0 if conj else 1.0
    rows_of = lambda q: pl.ds(8 * (seg - 1 - q if descending else q), 8)
    lr = jnp.broadcast_to(lam_re[:, lanes], (8, width))
    li = jnp.broadcast_to(sign * lam_im[:, lanes], (8, width))
    xr = jnp.zeros((8, width), F32)
    xi = jnp.zeros((8, width), F32)
    for q in range(seg):
        rows = rows_of(q)
        xr, xi = (lr * xr - li * xi) + src_re[rows, lanes], (lr * xi + li * xr) + src_im[rows, lanes]
        dst_re[rows, lanes] = xr
        dst_im[rows, lanes] = xi
    end_re[:, lanes] = xr
    end_im[:, lanes] = xi
    sr = pow_re[seg - 1:seg, lanes]
    si = sign * pow_im[seg - 1:seg, lanes]
    cr = carry_re[:, lanes]
    ci = carry_im[:, lanes]
    for s in range(8):
        se = 7 - s if descending else s
        in_re[se:se + 1, lanes] = cr
        in_im[se:se + 1, lanes] = ci
        cr, ci = (end_re[se:se + 1, lanes] + (sr * cr - si * ci)), (end_im[se:se + 1, lanes] + (sr * ci + si * cr))
    carry_re[:, lanes] = cr
    carry_im[:, lanes] = ci
    ir = in_re[:, lanes]
    ii = in_im[:, lanes]
    for q in range(seg):
        rows = rows_of(q)
        pr = pow_re[q:q + 1, lanes]
        pi = sign * pow_im[q:q + 1, lanes]
        dst_re[rows, lanes] = dst_re[rows, lanes] + (pr * ir - pi * ii)
        dst_im[rows, lanes] = dst_im[rows, lanes] + (pr * ii + pi * ir)


def _diag_tiles(gn):
    rows_per_tile = DIAG_TILE // (SSM_STATE // SSM_GROUP)
    return [(slice(rows_per_tile * j, rows_per_tile * (j + 1)), slice(DIAG_TILE * j, DIAG_TILE * (j + 1)))
            for j in range(gn // DIAG_TILE)]


def _ssm_scan_fwd(ub, lam_re, lam_im, pow_re, pow_im, bb_re, bb_im, cc_re, cc_im, exchange=None):
    t, w = ub.shape
    gn = lam_re.shape[-1]
    tc = ROW_TILE
    cl = min(gn, SCAN_LANES)
    nblk = t // tc
    tiles = _diag_tiles(gn)

    def body(*refs):
        first = jnp.logical_and(pl.program_id(0) == 0, pl.program_id(1) == 0)
        last = jnp.logical_and(pl.program_id(0) == 1, pl.program_id(1) == nblk - 1)
        (u_ref, lr_ref, li_ref, pr_ref, pi_ref, br_ref, bi_ref, cr_ref, ci_ref, y_ref, xr_ref, xi_ref,
         bur_scr, bui_scr, cr_scr, ci_scr, er_scr, ei_scr, nr_scr, ni_scr) = _riding_exchange(
             refs, exchange, 9, 3, first, last)

        @pl.when(pl.program_id(1) == 0)
        def _():
            cr_scr[...] = jnp.zeros_like(cr_scr)
            ci_scr[...] = jnp.zeros_like(ci_scr)

        for rows, lanes in tiles:
            u_j = u_ref[:, rows]
            bur_scr[:, lanes] = _dot(u_j, br_ref[0, rows, lanes])
            bui_scr[:, lanes] = _dot(u_j, bi_ref[0, rows, lanes])
        for descending in (False, True):
            @pl.when(pl.program_id(0) == int(descending))
            def _(descending=descending):
                for c0 in range(0, gn, cl):
                    _segmented_scan(bur_scr, bui_scr, xr_ref.at[0], xi_ref.at[0], lr_ref.at[0], li_ref.at[0],
                                    pr_ref.at[0], pi_ref.at[0], cr_scr, ci_scr, er_scr, ei_scr, nr_scr, ni_scr,
                                    pl.ds(c0, cl), descending, conj=False)
        for rows, lanes in tiles:
            y_ref[0, :, rows] = (_dot(xr_ref[0, :, lanes].astype(BF16), cr_ref[0, lanes, rows])
                                 - _dot(xi_ref[0, :, lanes].astype(BF16), ci_ref[0, lanes, rows]))

    blk = lambda dd, i: jnp.where(dd == 0, i, nblk - 1 - i)
    row = lambda width: pl.BlockSpec((1, tc, width), lambda dd, i: (dd, blk(dd, i), 0))
    per_dir = lambda a, b: pl.BlockSpec((1, a, b), lambda dd, i: (dd, 0, 0))
    extra = exchange is not None
    return _pcall(
        body, name="ssm_scan_fwd", grid=(2, nblk),
        in_specs=[pl.BlockSpec((tc, w), lambda dd, i: (blk(dd, i), 0)), per_dir(1, gn), per_dir(1, gn),
                  per_dir(tc // 8, gn), per_dir(tc // 8, gn),
                  per_dir(w, gn), per_dir(w, gn), per_dir(gn, w), per_dir(gn, w)] + [_ANY] * extra,
        out_specs=[row(w), row(gn), row(gn)] + [_ANY] * extra,
        out_shape=[jax.ShapeDtypeStruct((2, t, w), F32), jax.ShapeDtypeStruct((2, t, gn), F32),
                   jax.ShapeDtypeStruct((2, t, gn), F32)] + ([_exchange_out_shape(*exchange)] if extra else []),
        scratch=[pltpu.VMEM((tc, gn), F32), pltpu.VMEM((tc, gn), F32), pltpu.VMEM((1, gn), F32),
                 pltpu.VMEM((1, gn), F32)] + [pltpu.VMEM((8, gn), F32)] * 4 + _EXCHANGE_SEMS * extra,
        vmem=V7X_VMEM_LIMIT,
    )(ub, lam_re, lam_im, pow_re, pow_im, bb_re, bb_im, cc_re, cc_im, *([exchange[0]] if extra else []))


def _ssm_scan_bwd(dyb, ub, xs_re, xs_im, lam_re, lam_im, pow_re, pow_im, cct_re, cct_im, bbt_re, bbt_im,
                  exchange=None):
    t, w = dyb.shape
    gn = lam_re.shape[-1]
    tc = ROW_TILE
    cl = min(gn, SCAN_LANES)
    nblk = t // tc
    tiles = _diag_tiles(gn)

    def body(*refs):
        i = pl.program_id(1)
        first = jnp.logical_and(pl.program_id(0) == 0, i == 0)
        last = jnp.logical_and(pl.program_id(0) == 1, i == nblk - 1)
        (dy_ref, u_ref, xr_ref, xi_ref, hr_ref, hi_ref, lr_ref, li_ref, pr_ref, pi_ref, ctr_ref, cti_ref, btr_ref,
         bti_ref, du_ref, dlr_ref, dli_ref, dbr_ref, dbi_ref, dcr_ref, dci_ref,
         gxr_scr, gxi_scr, cr_scr, ci_scr, ar_scr, ai_scr, er_scr, ei_scr, nr_scr, ni_scr) = _riding_exchange(
             refs, exchange, 14, 7, first, last)

        @pl.when(i == 0)
        def _():
            for ref in (cr_scr, ci_scr, ar_scr, ai_scr, dbr_ref, dbi_ref, dcr_ref, dci_ref):
                ref[...] = jnp.zeros_like(ref)

        for rows, lanes in tiles:
            dy_j = dy_ref[:, rows]
            gxr_scr[:, lanes] = _dot(dy_j, ctr_ref[0, rows, lanes])
            gxi_scr[:, lanes] = -_dot(dy_j, cti_ref[0, rows, lanes])
        first_block = i == nblk - 1
        sublane = lax.broadcasted_iota(jnp.int32, (8, 1), 0)

        def lam_gradient(state_descending):
            for c0 in range(0, gn, 512):
                lanes = pl.ds(c0, 512)
                if state_descending:
                    cur, prev, edge, src = pl.ds(0, tc - 8), pl.ds(8, tc - 8), pl.ds(tc - 8, 8), pl.ds(0, 8)
                    halo_at, halo_row, shift = 7, 0, 7
                else:
                    cur, prev, edge, src = pl.ds(8, tc - 8), pl.ds(0, tc - 8), pl.ds(0, 8), pl.ds(tc - 8, 8)
                    halo_at, halo_row, shift = 0, 7, 1
                halo_r = jnp.where(first_block, 0.0, hr_ref[0, halo_row:halo_row + 1, lanes])
                halo_i = jnp.where(first_block, 0.0, hi_ref[0, halo_row:halo_row + 1, lanes])
                xer = jnp.where(sublane == halo_at, halo_r, pltpu.roll(xr_ref[0, src, lanes], shift, 0))
                xei = jnp.where(sublane == halo_at, halo_i, pltpu.roll(xi_ref[0, src, lanes], shift, 0))
                gr, gi = gxr_scr[cur, lanes], gxi_scr[cur, lanes]
                xpr, xpi = xr_ref[0, prev, lanes], xi_ref[0, prev, lanes]
                ger, gei = gxr_scr[edge, lanes], gxi_scr[edge, lanes]
                ar_scr[:, lanes] += (jnp.sum(gr * xpr + gi * xpi, axis=0, keepdims=True)
                                     + jnp.sum(ger * xer + gei * xei, axis=0, keepdims=True))
                ai_scr[:, lanes] += (jnp.sum(gi * xpr - gr * xpi, axis=0, keepdims=True)
                                     + jnp.sum(gei * xer - ger * xei, axis=0, keepdims=True))

        for descending in (True, False):
            @pl.when(pl.program_id(0) == int(not descending))
            def _(descending=descending):
                for c0 in range(0, gn, cl):
                    _segmented_scan(gxr_scr, gxi_scr, gxr_scr, gxi_scr, lr_ref.at[0], li_ref.at[0], pr_ref.at[0],
                                    pi_ref.at[0], cr_scr, ci_scr, er_scr, ei_scr, nr_scr, ni_scr, pl.ds(c0, cl),
                                    descending, conj=True)
                lam_gradient(state_descending=not descending)
        dlr_ref[0] = ar_scr[...]
        dli_ref[0] = ai_scr[...]
        for rows, lanes in tiles:
            grb = gxr_scr[:, lanes].astype(BF16)
            gib = gxi_scr[:, lanes].astype(BF16)
            du_ref[0, :, rows] = _dot(grb, btr_ref[0, lanes, rows]) + _dot(gib, bti_ref[0, lanes, rows])
            u_j = u_ref[:, rows]
            dy_j = dy_ref[:, rows]
            dbr_ref[0, rows, :] += _dot_tn(u_j, grb)
            dbi_ref[0, rows, :] += _dot_tn(u_j, gib)
            dcr_ref[0, rows, :] += _dot_tn(dy_j, xr_ref[0, :, lanes].astype(BF16))
            dci_ref[0, rows, :] -= _dot_tn(dy_j, xi_ref[0, :, lanes].astype(BF16))

    blk = lambda dd, i: jnp.where(dd == 0, nblk - 1 - i, i)
    rev = lambda width: pl.BlockSpec((1, tc, width), lambda dd, i: (dd, blk(dd, i), 0))
    halo_blk = lambda dd, i: jnp.where(dd == 0, jnp.maximum(blk(dd, i) * (tc // 8) - 1, 0),
                                       jnp.minimum((blk(dd, i) + 1) * (tc // 8), t // 8 - 1))
    halo = pl.BlockSpec((1, 8, gn), lambda dd, i: (dd, halo_blk(dd, i), 0))
    per_dir = lambda a, b: pl.BlockSpec((1, a, b), lambda dd, i: (dd, 0, 0))
    extra = exchange is not None
    return _pcall(
        body, name="ssm_scan_bwd", grid=(2, nblk),
        in_specs=[pl.BlockSpec((tc, w), lambda dd, i: (blk(dd, i), 0)),
                  pl.BlockSpec((tc, w), lambda dd, i: (blk(dd, i), 0)), rev(gn), rev(gn), halo, halo,
                  per_dir(1, gn), per_dir(1, gn), per_dir(tc // 8, gn), per_dir(tc // 8, gn),
                  per_dir(w, gn), per_dir(w, gn), per_dir(gn, w), per_dir(gn, w)]
        + [_ANY] * extra,
        out_specs=[rev(w), per_dir(1, gn), per_dir(1, gn)] + [per_dir(w, DIAG_TILE)] * 4 + [_ANY] * extra,
        out_shape=[jax.ShapeDtypeStruct((2, t, w), F32), jax.ShapeDtypeStruct((2, 1, gn), F32),
                   jax.ShapeDtypeStruct((2, 1, gn), F32)] + [jax.ShapeDtypeStruct((2, w, DIAG_TILE), F32)] * 4
        + ([_exchange_out_shape(*exchange)] if extra else []),
        scratch=[pltpu.VMEM((tc, gn), F32), pltpu.VMEM((tc, gn), F32)] + [pltpu.VMEM((1, gn), F32)] * 4
        + [pltpu.VMEM((8, gn), F32)] * 4 + _EXCHANGE_SEMS * extra,
        vmem=V7X_VMEM_LIMIT,
    )(dyb, ub, xs_re, xs_im, xs_re, xs_im, lam_re, lam_im, pow_re, pow_im, cct_re, cct_im, bbt_re, bbt_im,
      *([exchange[0]] if extra else []))


def _matmul_tn(a, b, name, a_is_transposed=False, exchange=None):
    t, n = b.shape
    m = a.shape[0] if a_is_transposed else a.shape[1]
    bm, bn, tk = min(m, 1024), min(n, 1024), KV_TILE
    grid = (m // bm, n // bn, t // tk)

    def body(*refs):
        at = lambda step: functools.reduce(jnp.logical_and, [pl.program_id(ax) == step[ax] for ax in range(3)])
        a_ref, b_ref, o_ref = _riding_exchange(refs, exchange, 2, 1, at((0, 0, 0)), at([g - 1 for g in grid]))

        @pl.when(pl.program_id(2) == 0)
        def _():
            o_ref[...] = jnp.zeros_like(o_ref)

        mul = _dot if a_is_transposed else _dot_tn
        o_ref[...] += mul(a_ref[...].astype(BF16), b_ref[...].astype(BF16))

    a_spec = (pl.BlockSpec((bm, tk), lambda i, j, k: (i, k)) if a_is_transposed else
              pl.BlockSpec((tk, bm), lambda i, j, k: (k, i)))
    extra = exchange is not None
    out = _pcall(
        body, name=name, grid=grid,
        in_specs=[a_spec, pl.BlockSpec((tk, bn), lambda i, j, k: (k, j))] + [_ANY] * extra,
        out_specs=[pl.BlockSpec((bm, bn), lambda i, j, k: (i, j))] + [_ANY] * extra,
        out_shape=[jax.ShapeDtypeStruct((m, n), F32)] + ([_exchange_out_shape(*exchange)] if extra else []),
        scratch=_EXCHANGE_SEMS * extra, vmem=V7X_VMEM_LIMIT,
    )(a, b, *([exchange[0]] if extra else []))
    return out if extra else out[0]


def _reduce_adamw(gparts, p, m, v, name, exchange=None):
    rows, width = p.shape
    tr = max(k for k in range(16, 513, 16) if rows % k == 0)

    def body(*refs):
        i = pl.program_id(0)
        g_ref, p_ref, m_ref, v_ref, go_ref, d_ref, mo_ref, vo_ref = _riding_exchange(
            refs, exchange, 4, 4, i == 0, i == rows // tr - 1)
        g = g_ref[0].astype(F32)
        for k in range(1, N_DEV):
            g = g + g_ref[k].astype(F32)
        go_ref[...] = g
        mm = ADAM_B1 * m_ref[...] + (1.0 - ADAM_B1) * g
        vv = ADAM_B2 * v_ref[...] + (1.0 - ADAM_B2) * (g * g)
        m_hat = mm / (1.0 - ADAM_B1 ** ADAM_STEP)
        v_hat = vv / (1.0 - ADAM_B2 ** ADAM_STEP)
        d_ref[...] = -ADAM_LR * (m_hat / (jnp.sqrt(v_hat) + ADAM_EPS) + ADAM_WD * p_ref[...])
        mo_ref[...] = mm
        vo_ref[...] = vv

    spec = pl.BlockSpec((tr, width), lambda i: (i, 0))
    out = jax.ShapeDtypeStruct((rows, width), F32)
    extra = exchange is not None
    return _pcall(
        body, name=name, grid=(rows // tr,),
        in_specs=[pl.BlockSpec((N_DEV, tr, width), lambda i: (0, i, 0)), spec, spec, spec] + [_ANY] * extra,
        out_specs=[spec, spec, spec, spec] + [_ANY] * extra,
        out_shape=[out, out, out, out] + ([_exchange_out_shape(*exchange)] if extra else []),
        scratch=_EXCHANGE_SEMS * extra, vmem=V7X_VMEM_LIMIT,
    )(gparts, p, m, v, *([exchange[0]] if extra else []))


def _peer(k):
    x, y, c = lax.axis_index("x"), lax.axis_index("y"), lax.axis_index("c")
    return (x ^ ((k >> 2) & 1), y ^ ((k >> 1) & 1), c ^ (k & 1))


def _my_index():
    return 4 * lax.axis_index("x") + 2 * lax.axis_index("y") + lax.axis_index("c")


def _exchange_copies(x_ref, out_ref, send_sems, recv_sems, local_sem, scatter, first_sem=0):
    me = _my_index()
    local = pltpu.make_async_copy(x_ref.at[me] if scatter else x_ref, out_ref.at[me], local_sem)
    copies = []
    for k in range(1, N_DEV):
        peer = _peer(k)
        src = x_ref.at[4 * peer[0] + 2 * peer[1] + peer[2]] if scatter else x_ref
        copies.append(pltpu.make_async_remote_copy(
            src_ref=src, dst_ref=out_ref.at[me], send_sem=send_sems.at[first_sem + k - 1],
            recv_sem=recv_sems.at[first_sem + k - 1], device_id=peer, device_id_type=pl.DeviceIdType.MESH))
    return local, copies


def _start_all(local, copies):
    local.start()
    for cp in copies:
        cp.start()


def _wait_all(local, copies):
    for cp in copies:
        cp.wait_recv()
    for cp in copies:
        cp.wait_send()
    local.wait()


def _exchange_out_shape(x, scatter):
    return jax.ShapeDtypeStruct((N_DEV,) + tuple(x.shape[1:] if scatter else x.shape), x.dtype)


_EXCHANGE_SEMS = [pltpu.SemaphoreType.DMA((N_DEV - 1,)), pltpu.SemaphoreType.DMA((N_DEV - 1,)),
                  pltpu.SemaphoreType.DMA(())]


def _gather_two_level(x, name):
    def body(x_ref, out_ref, send_sems, recv_sems, local_sem):
        x, y, c = lax.axis_index("x"), lax.axis_index("y"), lax.axis_index("c")
        me, sibling = (x, y, c), (x, y, 1 - c)
        chips = [(1 - x, y), (x, 1 - y), (1 - x, 1 - y)]

        def copy(k, block, to, src=None):
            slot = out_ref.at[4 * block[0] + 2 * block[1] + block[2]]
            return pltpu.make_async_remote_copy(src_ref=slot if src is None else src, dst_ref=slot,
                                                send_sem=send_sems.at[k], recv_sem=recv_sems.at[k], device_id=to,
                                                device_id_type=pl.DeviceIdType.MESH)

        local = pltpu.make_async_copy(x_ref, out_ref.at[_my_index()], local_sem)
        local.start()
        first = [copy(0, me, sibling, src=x_ref)]
        first += [copy(1 + j, me, (*chip, c), src=x_ref) for j, chip in enumerate(chips)]
        for cp in first:
            cp.start()
        passed = [copy(4 + j, (*chip, c), sibling) for j, chip in enumerate(chips)]
        for j, chip in enumerate(chips):
            copy(1 + j, (*chip, c), me).wait_recv()
            passed[j].start()
        copy(0, sibling, me).wait_recv()
        for j, chip in enumerate(chips):
            copy(4 + j, (*chip, 1 - c), me).wait_recv()
        for cp in first + passed:
            cp.wait_send()
        local.wait()

    return pl.pallas_call(
        body, name=name, in_specs=[_ANY], out_specs=_ANY, out_shape=_exchange_out_shape(x, False),
        scratch_shapes=_EXCHANGE_SEMS,
    )(x)


def _to_shards(full, axis):
    r, c = full.shape
    if axis == 0:
        return full.reshape(N_DEV, r // N_DEV, c)
    return full.reshape(r, N_DEV, c // N_DEV).transpose(1, 0, 2)


def _from_shards(shards, axis):
    _, r, c = shards.shape
    if axis == 0:
        return shards.reshape(N_DEV * r, c)
    return shards.transpose(1, 0, 2).reshape(r, N_DEV * c)


def _pack_rows(parts, lead):
    flat = []
    for p in parts:
        p = p.reshape(p.shape[:lead] + (-1, PACK_W))
        pad = _round_up(p.shape[lead], 16) - p.shape[lead]
        flat.append(jnp.pad(p, [(0, 0)] * lead + [(0, pad), (0, 0)]) if pad else p)
    return jnp.concatenate(flat, axis=lead)


def _unpack_rows(packed, shapes):
    lead = packed.shape[:-2]
    out, off = [], 0
    for shp in shapes:
        rows = math.prod(shp) // PACK_W
        out.append(packed[..., off:off + rows, :].reshape(lead + tuple(shp)))
        off += _round_up(rows, 16)
    return out


def _pack_flat(parts):
    flat = jnp.concatenate([p.reshape(-1) for p in parts])
    n = flat.shape[0]
    flat = jnp.pad(flat, (0, _round_up(n, 16 * PACK_W) - n))
    return flat.reshape(-1, PACK_W)


def _unpack(packed, shapes):
    flat = packed.reshape(-1)
    out, off = [], 0
    for shp in shapes:
        n = math.prod(shp)
        out.append(flat[off:off + n].reshape(shp))
        off += n
    return out


def _ssm_discretize(a_re, a_im, log_dt, b_re, b_im):
    dt = jnp.exp(log_dt)[..., None]
    lam_re = jnp.minimum(a_re, EIG_RE_MAX)
    lam_im = a_im
    mag = jnp.exp(lam_re * dt)
    ang = lam_im * dt
    lb_re = mag * jnp.cos(ang)
    lb_im = mag * jnp.sin(ang)
    num_re = lb_re - 1.0
    num_im = lb_im
    den = lam_re * lam_re + lam_im * lam_im
    f_re = (num_re * lam_re + num_im * lam_im) / den
    f_im = (num_im * lam_re - num_re * lam_im) / den
    bb_re = f_re[..., None] * b_re - f_im[..., None] * b_im
    bb_im = f_re[..., None] * b_im + f_im[..., None] * b_re
    return lb_re, lb_im, bb_re, bb_im


def _ssm_powers(a_re, a_im, log_dt, count):
    dt = jnp.exp(log_dt)[:, None, :, None]
    k = jnp.arange(1, count + 1, dtype=F32)[None, :, None, None]
    mag = jnp.exp(k * (jnp.minimum(a_re, EIG_RE_MAX)[:, None] * dt))
    ang = k * (a_im[:, None] * dt)
    shape = (a_re.shape[0], count, -1)
    return (mag * jnp.cos(ang)).reshape(shape), (mag * jnp.sin(ang)).reshape(shape)


def _interleave(a, inverse=False):
    lead, (t, width) = a.shape[:-2], a.shape[-2:]
    seg = ROW_TILE // 8
    shape = lead + (t // ROW_TILE,) + ((seg, 8) if inverse else (8, seg)) + (width,)
    return jnp.swapaxes(a.reshape(shape), -3, -2).reshape(a.shape)


def _block_diag(blocks):
    two, g, a, b = blocks.shape
    tiled = jnp.tile(blocks.reshape(two, g * a, b), (1, 1, g))
    row_group = lax.broadcasted_iota(jnp.int32, (g * a, g * b), 0) // a
    col_group = lax.broadcasted_iota(jnp.int32, (g * a, g * b), 1) // b
    return jnp.where(row_group == col_group, tiled, 0.0).astype(BF16)


def _diag_blocks(tiles):
    two, w, _ = tiles.shape
    per = DIAG_TILE // SSM_STATE
    t6 = tiles.reshape(two, w // (per * SSM_GROUP), per, SSM_GROUP, per, SSM_STATE)
    return jnp.einsum("zjqpqn->zjqpn", t6).reshape(two, w // SSM_GROUP, SSM_GROUP, SSM_STATE)


def _rope_tables(t, n_valid):
    pos = jnp.arange(t)
    real = jnp.logical_and(pos >= N_META, pos < n_valid)
    idx = jnp.where(real, pos - N_META, 0)
    row_id = (idx // GRID_W).astype(F32)
    col_id = (idx % GRID_W).astype(F32)
    pairs_per_axis = HEAD_DIM // 4
    inv_freq = ROPE_THETA ** (-jnp.arange(pairs_per_axis, dtype=F32) / pairs_per_axis)
    ang = jnp.concatenate([row_id[:, None] * inv_freq, col_id[:, None] * inv_freq], axis=-1)
    ang = jnp.where(real[:, None], ang, 0.0)
    cos = jnp.repeat(jnp.cos(ang), 2, axis=-1)
    sin = jnp.sin(ang)
    sin = jnp.stack([-sin, sin], axis=-1).reshape(t, HEAD_DIM)
    return jnp.tile(cos, (1, 2)), jnp.tile(sin, (1, 2))


def _local_step(x, loss_target, big, small, comm=None):
    s_len, d = x.shape
    n_valid = s_len + N_META
    t = _round_up(n_valid, KV_TILE)
    du = d // 2
    groups = du // SSM_GROUP
    pad = t - n_valid

    x0 = jnp.concatenate([big["meta_tokens"].astype(F32), x, jnp.zeros((pad, d), F32)], axis=0)
    tgt = jnp.concatenate([jnp.zeros((N_META, d), F32), loss_target, jnp.zeros((pad, d), F32)], axis=0)
    cos, sin = _rope_tables(t, n_valid)
    g_mix = small["norm_mix_g"].reshape(1, d)
    g_mlp = small["norm_mlp_g"].reshape(1, d)
    g_fin = small["norm_final_g"].reshape(1, d)
    qg = jnp.tile(small["q_norm_g"].reshape(1, HEAD_DIM), (1, 2))
    kg = jnp.tile(small["k_norm_g"].reshape(1, HEAD_DIM), (1, 2))
    ssm_d = small["ssm_d"].reshape(1, du)
    b_glu = small["b_glu"].reshape(1, du)

    ssm_in = tuple(small[n][0] for n in ("ssm_a_re", "ssm_a_im", "ssm_log_dt", "ssm_b_re", "ssm_b_im"))
    (lb_re, lb_im, bbar_re, bbar_im), disc_vjp = jax.vjp(_ssm_discretize, *ssm_in)
    lam_re = lb_re.reshape(2, 1, groups * SSM_STATE)
    lam_im = lb_im.reshape(2, 1, groups * SSM_STATE)
    pow_re, pow_im = _ssm_powers(*ssm_in[0:3], ROW_TILE // 8)
    bb_re = _block_diag(bbar_re.transpose(0, 1, 3, 2))
    bb_im = _block_diag(bbar_im.transpose(0, 1, 3, 2))
    c_re, c_im = small["ssm_c_re"][0], small["ssm_c_im"][0]
    cct_re = _block_diag(c_re)
    cct_im = _block_diag(c_im)
    cc_re = cct_re.transpose(0, 2, 1)
    cc_im = cct_im.transpose(0, 2, 1)
    bbt_re = bb_re.transpose(0, 2, 1)
    bbt_im = bb_im.transpose(0, 2, 1)
    scan_w = (lam_re, lam_im, pow_re, pow_im)

    in_proj_args = (x0, g_mix, big["w_in"], qg, kg, cos, sin, n_valid)
    if comm is None:
        h, u, ub, qraw, kraw, qat, ka, kt, va, vta, gates = _in_proj_fwd(*in_proj_args)
    else:
        h, u, ub, qraw, kraw, qat, ka, kt, va, vta, gates, got = _in_proj_fwd(
            *in_proj_args, exchange=(comm["pack_weights"](MIXER_WEIGHTS), False))
        big = {**big, **comm["unpack_weights"](MIXER_WEIGHTS, got)}
    ub = _interleave(ub)
    if comm is None:
        y2, xs_re, xs_im = _ssm_scan_fwd(ub, *scan_w, bb_re, bb_im, cc_re, cc_im)
    else:
        y2, xs_re, xs_im, got = _ssm_scan_fwd(ub, *scan_w, bb_re, bb_im, cc_re, cc_im,
                                              exchange=(comm["pack_weights"](MLP_WEIGHTS), False))
        big = {**big, **comm["unpack_weights"](MLP_WEIGHTS, got)}
    y2 = _interleave(y2, inverse=True)
    yf, yb = y2[0], y2[1]
    yt_attn, lse = _attn_fwd(qat, ka, vta)
    mixer_w = (ssm_d, big["w_glu"], b_glu, big["w_ssm_proj"], big["w_attn_proj"], big["w_out"])
    x1 = _mixer_out_fwd(x0, u, yf, yb, yt_attn, gates, *mixer_w)

    dx1, loss8, dg_fin, dg_mlp, h2b, dab, hsqb, dx2b = _mlp_loss_fwd_bwd(
        x1, tgt, g_mlp, g_fin, big["w_mlp_in"], big["w_mlp_out"], n_valid)

    (dyb, dud, dyt_attn, dgates, zb, dglb, ysb, dasb, daab, mgb, dxb, d_ssm_d, d_b_glu) = _mixer_out_bwd(
        dx1, u, yf, yb, yt_attn, gates, *mixer_w)
    grads = {}
    grads["w_glu"] = _matmul_tn(zb, dglb, "grad_w_glu")
    grads["w_ssm_proj"] = _matmul_tn(ysb, dasb, "grad_w_ssm_proj")
    grads["w_attn_proj"] = _matmul_tn(yt_attn, daab, "grad_w_attn_proj", a_is_transposed=True)
    grads["w_out"] = _matmul_tn(mgb, dxb, "grad_w_out")
    grads["w_mlp_in"] = _matmul_tn(h2b, dab, "grad_w_mlp_in")
    grads["w_mlp_out"] = _matmul_tn(hsqb, dx2b, "grad_w_mlp_out")
    dk, dv, dqt = _attn_bwd(qat, ka, kt, va, dyt_attn, yt_attn, lse)
    scan_args = (_interleave(dyb), ub, xs_re, xs_im, *scan_w, cct_re, cct_im, bbt_re, bbt_im)
    if comm is None:
        late_grad_parts = None
        du2, dlam_re, dlam_im, dbr, dbi, dcr, dci = _ssm_scan_bwd(*scan_args)
    else:
        du2, dlam_re, dlam_im, dbr, dbi, dcr, dci, late_grad_parts = _ssm_scan_bwd(
            *scan_args, exchange=(comm["pack_grads"](LATE_WEIGHTS, grads), True))
    du2 = _interleave(du2, inverse=True)
    dx0, dproj, dg_mix, dqg, dkg = _in_proj_bwd(x0, dx1, dud, du2[0], du2[1], qraw, kraw, dqt, dk, dv, dgates,
                                                g_mix, big["w_in"], qg, kg, cos, sin)

    grads["meta_tokens"] = dx0[0:N_META]
    dbb_re = _diag_blocks(dbr).transpose(0, 1, 3, 2)
    dbb_im = _diag_blocks(dbi).transpose(0, 1, 3, 2)
    dc_re, dc_im = _diag_blocks(dcr), _diag_blocks(dci)
    shape_gn = (2, groups, SSM_STATE)
    d_a_re, d_a_im, d_log_dt, d_b_re, d_b_im = disc_vjp(
        (dlam_re.reshape(shape_gn), dlam_im.reshape(shape_gn), dbb_re, dbb_im))
    grads.update({
        "norm_mix_g": dg_mix, "ssm_a_re": d_a_re[None], "ssm_a_im": d_a_im[None], "ssm_log_dt": d_log_dt[None],
        "ssm_b_re": d_b_re[None], "ssm_b_im": d_b_im[None], "ssm_c_re": dc_re[None], "ssm_c_im": dc_im[None],
        "ssm_d": d_ssm_d, "b_glu": d_b_glu,
        "q_norm_g": dqg[:, 0:HEAD_DIM] + dqg[:, HEAD_DIM:128], "k_norm_g": dkg[:, 0:HEAD_DIM] + dkg[:, HEAD_DIM:128],
        "norm_mlp_g": dg_mlp, "norm_final_g": dg_fin.reshape(d),
    })
    if comm is None:
        small_grad_parts = None
        grads["w_in"] = _matmul_tn(h, dproj, "grad_w_in")
    else:
        grads["w_in"], small_grad_parts = _matmul_tn(h, dproj, "grad_w_in",
                                                     exchange=(comm["pack_small_grads"](grads), False))
    return loss8[0, 0], dx0[N_META:n_valid], grads, late_grad_parts, small_grad_parts


def kernel(x, meta_tokens, norm_mix_g, w_in, ssm_a_re, ssm_a_im, ssm_log_dt, ssm_b_re, ssm_b_im, ssm_c_re, ssm_c_im, ssm_d, w_glu, b_glu, q_norm_g, k_norm_g, w_ssm_proj, w_attn_proj, w_out, norm_mlp_g, w_mlp_in, w_mlp_out, norm_final_g, loss_target, m_meta_tokens, m_norm_mix_g, m_w_in, m_ssm_a_re, m_ssm_a_im, m_ssm_log_dt, m_ssm_b_re, m_ssm_b_im, m_ssm_c_re, m_ssm_c_im, m_ssm_d, m_w_glu, m_b_glu, m_q_norm_g, m_k_norm_g, m_w_ssm_proj, m_w_attn_proj, m_w_out, m_norm_mlp_g, m_w_mlp_in, m_w_mlp_out, m_norm_final_g, v_meta_tokens, v_norm_mix_g, v_w_in, v_ssm_a_re, v_ssm_a_im, v_ssm_log_dt, v_ssm_b_re, v_ssm_b_im, v_ssm_c_re, v_ssm_c_im, v_ssm_d, v_w_glu, v_b_glu, v_q_norm_g, v_k_norm_g, v_w_ssm_proj, v_w_attn_proj, v_w_out, v_norm_mlp_g, v_w_mlp_in, v_w_mlp_out, v_norm_final_g):
    w = dict(meta_tokens=meta_tokens, norm_mix_g=norm_mix_g, w_in=w_in, ssm_a_re=ssm_a_re, ssm_a_im=ssm_a_im, ssm_log_dt=ssm_log_dt, ssm_b_re=ssm_b_re, ssm_b_im=ssm_b_im, ssm_c_re=ssm_c_re, ssm_c_im=ssm_c_im, ssm_d=ssm_d, w_glu=w_glu, b_glu=b_glu, q_norm_g=q_norm_g, k_norm_g=k_norm_g, w_ssm_proj=w_ssm_proj, w_attn_proj=w_attn_proj, w_out=w_out, norm_mlp_g=norm_mlp_g, w_mlp_in=w_mlp_in, w_mlp_out=w_mlp_out, norm_final_g=norm_final_g)
    m = dict(meta_tokens=m_meta_tokens, norm_mix_g=m_norm_mix_g, w_in=m_w_in, ssm_a_re=m_ssm_a_re, ssm_a_im=m_ssm_a_im, ssm_log_dt=m_ssm_log_dt, ssm_b_re=m_ssm_b_re, ssm_b_im=m_ssm_b_im, ssm_c_re=m_ssm_c_re, ssm_c_im=m_ssm_c_im, ssm_d=m_ssm_d, w_glu=m_w_glu, b_glu=m_b_glu, q_norm_g=m_q_norm_g, k_norm_g=m_k_norm_g, w_ssm_proj=m_w_ssm_proj, w_attn_proj=m_w_attn_proj, w_out=m_w_out, norm_mlp_g=m_norm_mlp_g, w_mlp_in=m_w_mlp_in, w_mlp_out=m_w_mlp_out, norm_final_g=m_norm_final_g)
    v = dict(meta_tokens=v_meta_tokens, norm_mix_g=v_norm_mix_g, w_in=v_w_in, ssm_a_re=v_ssm_a_re, ssm_a_im=v_ssm_a_im, ssm_log_dt=v_ssm_log_dt, ssm_b_re=v_ssm_b_re, ssm_b_im=v_ssm_b_im, ssm_c_re=v_ssm_c_re, ssm_c_im=v_ssm_c_im, ssm_d=v_ssm_d, w_glu=v_w_glu, b_glu=v_b_glu, q_norm_g=v_q_norm_g, k_norm_g=v_k_norm_g, w_ssm_proj=v_w_ssm_proj, w_attn_proj=v_w_attn_proj, w_out=v_w_out, norm_mlp_g=v_norm_mlp_g, w_mlp_in=v_w_mlp_in, w_mlp_out=v_w_mlp_out, norm_final_g=v_norm_final_g)

    shard2d = {n: w[n].reshape(w[n].shape[-2:]) for n in BIG_WEIGHTS}

    meta_hi = shard2d["meta_tokens"].astype(BF16)
    meta_res = shard2d["meta_tokens"] - meta_hi.astype(F32)
    meta_mid = meta_res.astype(BF16)
    meta_lo = (meta_res - meta_mid.astype(F32)).astype(BF16)
    shapes_of = lambda names: [shard2d[n].shape for n in names]

    def full_weights(names, shards):
        return {n: s if n in BLOCK_WEIGHTS else _from_shards(s, BIG_SHARD_AXIS[n]) for n, s in zip(names, shards)}

    early = _gather_two_level(_pack_rows([meta_hi, meta_mid, meta_lo, shard2d["w_in"].astype(BF16)], 0),
                              "gather_early_weights")
    shards = _unpack_rows(early, [meta_hi.shape] * 3 + shapes_of(EARLY_WEIGHTS[1:]))
    meta = [_from_shards(s, 1).astype(F32) for s in shards[0:3]]
    big = {"meta_tokens": (meta[0] + meta[1]) + meta[2], **full_weights(EARLY_WEIGHTS[1:], shards[3:])}
    small = {n: w[n] for n in SMALL_WEIGHTS}
    pack_grads = lambda names, grads: _pack_rows(
        [_to_shards(grads[n], BIG_SHARD_AXIS[n]) for n in names], 1).astype(BF16)
    comm = {
        "pack_weights": lambda names: _pack_rows([shard2d[n].astype(BF16) for n in names], 0),
        "unpack_weights": lambda names, g: full_weights(names, _unpack_rows(g, shapes_of(names))),
        "pack_grads": pack_grads,
        "pack_small_grads": lambda grads: _pack_flat([grads[n] for n in SMALL_WEIGHTS]),
    }

    loss, grad_x, grads, late_parts, small_parts = _local_step(x[0], loss_target[0], big, small, comm)
    loss = lax.psum(loss, ("x", "y", "c"))

    pk = lambda names, src: _pack_rows([src[n].reshape(shard2d[n].shape) for n in names], 0)
    pe, pl_ = functools.partial(pk, EARLY_WEIGHTS), functools.partial(pk, LATE_WEIGHTS)
    *late_out, early_parts = _reduce_adamw(late_parts, pl_(w), pl_(m), pl_(v), "adamw_sharded_late",
                                           exchange=(pack_grads(EARLY_WEIGHTS, grads), True))
    early_out = _reduce_adamw(early_parts, pe(w), pe(m), pe(v), "adamw_sharded_early")
    small_shapes = [w[n].shape for n in SMALL_WEIGHTS]
    pf = lambda src: _pack_flat([src[n] for n in SMALL_WEIGHTS])
    small_out = _reduce_adamw(small_parts, pf(w), pf(m), pf(v), "adamw_replicated")

    results = []
    for kind in range(4):
        big_un = dict(zip(EARLY_WEIGHTS + LATE_WEIGHTS,
                          _unpack_rows(early_out[kind], shapes_of(EARLY_WEIGHTS))
                          + _unpack_rows(late_out[kind], shapes_of(LATE_WEIGHTS))))
        small_un = dict(zip(SMALL_WEIGHTS, _unpack(small_out[kind], small_shapes)))
        for n in ALL_WEIGHTS:
            results.append(big_un[n].reshape(w[n].shape) if n in big_un else small_un[n])
    return (loss, grad_x[None], *results)
```

```python
import functools
import math

import jax
import jax.numpy as jnp
from jax import lax
from jax.experimental import pallas as pl
from jax.experimental.pallas import tpu as pltpu

F32 = jnp.float32
BF16 = jnp.bfloat16

N_DEV = 8
N_META = 16
GRID_W = 64
SSM_GROUP = 16
SSM_STATE = 64
HEAD_DIM = 64
KV_REP = 4
ROPE_THETA = 10000.0
NORM_EPS = 1e-6
EIG_RE_MAX = -1e-4
ATTN_SCALE = HEAD_DIM ** -0.5

ADAM_LR = 0.001
ADAM_B1 = 0.9
ADAM_B2 = 0.999
ADAM_EPS = 1e-08
ADAM_WD = 0.01
ADAM_STEP = 10

ROW_TILE = 384
ROW_TILE_BWD = 384
QUERY_STRIP = 256
ONE_PASS_SLACK = 60.0
VT_ROWS = 80
MASK_BIAS = -1e30
SCAN_LANES = 512
DIAG_TILE = 256
KV_TILE = 768
PACK_W = 1024
V7X_VMEM_LIMIT = 56 * 1024 * 1024
NEG_BIG = -1e30

BIG_WEIGHTS = ("meta_tokens", "w_in", "w_glu", "w_ssm_proj", "w_attn_proj", "w_out", "w_mlp_in", "w_mlp_out")
BIG_SHARD_AXIS = {"meta_tokens": 1, "w_in": 1, "w_glu": 0, "w_ssm_proj": 1, "w_attn_proj": 0, "w_out": 0,
                  "w_mlp_in": 1, "w_mlp_out": 0}
BLOCK_WEIGHTS = ("w_in", "w_mlp_in", "w_mlp_out")
EARLY_WEIGHTS = ("meta_tokens", "w_in")
MIXER_WEIGHTS = ("w_glu", "w_ssm_proj", "w_attn_proj", "w_out")
MLP_WEIGHTS = ("w_mlp_in", "w_mlp_out")
LATE_WEIGHTS = MIXER_WEIGHTS + MLP_WEIGHTS
SMALL_WEIGHTS = ("norm_mix_g", "ssm_a_re", "ssm_a_im", "ssm_log_dt", "ssm_b_re", "ssm_b_im", "ssm_c_re",
                 "ssm_c_im", "ssm_d", "b_glu", "q_norm_g", "k_norm_g", "norm_mlp_g", "norm_final_g")
ALL_WEIGHTS = ("meta_tokens", "norm_mix_g", "w_in", "ssm_a_re", "ssm_a_im", "ssm_log_dt", "ssm_b_re", "ssm_b_im",
               "ssm_c_re", "ssm_c_im", "ssm_d", "w_glu", "b_glu", "q_norm_g", "k_norm_g", "w_ssm_proj",
               "w_attn_proj", "w_out", "norm_mlp_g", "w_mlp_in", "w_mlp_out", "norm_final_g")


def _round_up(n, m):
    return (n + m - 1) // m * m


def _pcall(body, *, name, grid, in_specs, out_specs, out_shape, scratch=(), vmem=None, **kw):
    params = pltpu.CompilerParams(dimension_semantics=("arbitrary",) * len(grid), vmem_limit_bytes=vmem)
    return pl.pallas_call(body, name=name, grid=grid, in_specs=in_specs, out_specs=out_specs, out_shape=out_shape,
                          scratch_shapes=list(scratch), compiler_params=params, **kw)


def _dot(a, b):
    return jnp.dot(a, b, preferred_element_type=F32)


def _dot_nt(a, b):
    return lax.dot_general(a, b, (((1,), (1,)), ((), ())), preferred_element_type=F32)


def _dot_tn(a, b):
    return lax.dot_general(a, b, (((0,), (0,)), ((), ())), preferred_element_type=F32)


def _full_spec(shape):
    nd = len(shape)
    return pl.BlockSpec(shape, lambda *_: (0,) * nd)


def _row_spec(tm, width):
    return pl.BlockSpec((tm, width), lambda i: (i, 0))


def _heads_spec(nh, tm):
    return pl.BlockSpec((nh, tm, HEAD_DIM), lambda i: (0, i, 0))


_ANY = pl.BlockSpec(memory_space=pl.ANY)


def _load_once(step, pairs, sem):
    @pl.when(step == 0)
    def _():
        copies = [pltpu.make_async_copy(src, dst, sem.at[k]) for k, (src, dst) in enumerate(pairs)]
        for cp in copies:
            cp.start()
        for cp in copies:
            cp.wait()


def _swap_pairs(x, even):
    n = x.shape[-1]
    return jnp.where(even, pltpu.roll(x, n - 1, 1), pltpu.roll(x, 1, 1))


def _head_mean(v):
    row = lax.broadcasted_iota(jnp.int32, (128, 128), 0) // HEAD_DIM
    col = lax.broadcasted_iota(jnp.int32, (128, 128), 1) // HEAD_DIM
    ones = jnp.where(row == col, 1.0, 0.0).astype(BF16)
    hi = v.astype(BF16)
    lo = (v - hi.astype(F32)).astype(BF16)
    return (_dot(hi, ones) + _dot(lo, ones)) * (1.0 / HEAD_DIM)


def _gelu(y):
    return 0.5 * y * (1.0 + lax.erf(y * (1.0 / math.sqrt(2.0))))


def _gelu_grad(y):
    return 0.5 * (1.0 + lax.erf(y * (1.0 / math.sqrt(2.0)))) + y * jnp.exp(-0.5 * y * y) * (1.0 / math.sqrt(2.0 * math.pi))


def _in_proj_fwd(x0, g_mix, w_in, qg, kg, cos, sin, n_valid, exchange=None):
    t, d = x0.shape
    tm = ROW_TILE
    du, dk = d // 2, d // 4
    nh, nkv = d // HEAD_DIM, d // HEAD_DIM // KV_REP
    bw = w_in.shape[-1]
    assert bw == du and dk * 2 == bw

    def body(*refs):
        i = pl.program_id(0)
        (x_ref, g_ref, w_hbm, qg_ref, kg_ref, c_ref, s_ref,
         h_ref, u_ref, ub_ref, qraw_ref, kraw_ref, qat_ref, ka_ref, kt_ref, va_ref, vta_ref, gates_ref,
         w_ref, sem) = _riding_exchange(refs, exchange, 7, 11, i == 0, i == t // tm - 1)
        _load_once(i, [(w_hbm, w_ref)], sem)
        x = x_ref[...]
        r = lax.rsqrt(jnp.mean(x * x, axis=-1, keepdims=True) + NORM_EPS)
        h = ((x * r) * g_ref[...]).astype(BF16)
        h_ref[...] = h
        u = _dot(h, w_ref[0])
        u_ref[...] = u
        ub_ref[...] = u.astype(BF16)
        lane = lax.broadcasted_iota(jnp.int32, (tm, 128), 1)
        lo = lane < HEAD_DIM
        even = (lane & 1) == 0
        aug = lane == HEAD_DIM
        c = c_ref[...]
        s = s_ref[...]
        row = i * tm + lax.broadcasted_iota(jnp.int32, (tm, 1), 0)
        one = jnp.where(aug, 1.0, 0.0)
        key_bias = jnp.where(jnp.logical_and(aug, row >= n_valid), MASK_BIAS, 0.0)

        def norm_rope(blk, g128):
            rr = lax.rsqrt(_head_mean(blk * blk) + NORM_EPS)
            qn = (blk * rr) * g128
            return qn * c + _swap_pairs(qn, even) * s

        def put_heads(rows_ref, cols_ref, first, pair, extra):
            for k, head in enumerate((pair, pltpu.roll(pair, HEAD_DIM, 1))):
                wide = jnp.where(lo, head, extra)
                if rows_ref is not None:
                    rows_ref[first + k] = wide.astype(BF16)
                cols_ref[first + k] = wide.T[0:cols_ref.shape[1], :].astype(BF16)

        for blk in range(2):
            qb = _dot(h, w_ref[1 + blk])
            qraw_ref[:, bw * blk:bw * (blk + 1)] = qb
            for a in range(bw // 128):
                put_heads(None, qat_ref, (bw // HEAD_DIM) * blk + 2 * a,
                          norm_rope(qb[:, 128 * a:128 * (a + 1)], qg_ref[...]) * ATTN_SCALE, one)
        kv = _dot(h, w_ref[3])
        kraw_ref[...] = kv[:, 0:dk]
        for a in range(nkv // 2):
            put_heads(ka_ref, kt_ref, 2 * a, norm_rope(kv[:, 128 * a:128 * (a + 1)], kg_ref[...]), key_bias)
            put_heads(va_ref, vta_ref, 2 * a, kv[:, dk + 128 * a:dk + 128 * (a + 1)], one)
        for blk in range(4):
            gates_ref[:, bw * blk:bw * (blk + 1)] = _dot(h, w_ref[4 + blk])

    heads = lambda n: pl.BlockSpec((n, tm, 128), lambda i: (0, i, 0))
    heads_t = lambda n, rows: pl.BlockSpec((n, rows, tm), lambda i: (0, 0, i))
    extra = exchange is not None
    return _pcall(
        body, name="in_proj_fwd", grid=(t // tm,),
        in_specs=[_row_spec(tm, d), _full_spec((1, d)), _ANY, _full_spec((1, 128)), _full_spec((1, 128)),
                  _row_spec(tm, 128), _row_spec(tm, 128)] + [_ANY] * extra,
        out_specs=[_row_spec(tm, d), _row_spec(tm, du), _row_spec(tm, du), _row_spec(tm, d), _row_spec(tm, dk),
                   heads_t(nh, 128), heads(nkv), heads_t(nkv, HEAD_DIM), heads(nkv), heads_t(nkv, VT_ROWS),
                   _row_spec(tm, 2 * d)] + [_ANY] * extra,
        out_shape=[jax.ShapeDtypeStruct((t, d), BF16), jax.ShapeDtypeStruct((t, du), F32),
                   jax.ShapeDtypeStruct((t, du), BF16), jax.ShapeDtypeStruct((t, d), F32),
                   jax.ShapeDtypeStruct((t, dk), F32), jax.ShapeDtypeStruct((nh, 128, t), BF16),
                   jax.ShapeDtypeStruct((nkv, t, 128), BF16), jax.ShapeDtypeStruct((nkv, HEAD_DIM, t), BF16),
                   jax.ShapeDtypeStruct((nkv, t, 128), BF16), jax.ShapeDtypeStruct((nkv, VT_ROWS, t), BF16),
                   jax.ShapeDtypeStruct((t, 2 * d), F32)] + ([_exchange_out_shape(*exchange)] if extra else []),
        scratch=[pltpu.VMEM((N_DEV, d, bw), BF16), pltpu.SemaphoreType.DMA((1,))] + _EXCHANGE_SEMS * extra,
        vmem=V7X_VMEM_LIMIT,
    )(x0, g_mix, w_in, qg, kg, cos, sin, *([exchange[0]] if extra else []))


def _mixer_values(u, yf, yb, yt_attn, gates, d_ref, wg_ref, bg_ref, ps_ref, pa_ref, d):
    y = (u * d_ref[...] + yf) + yb
    z = _gelu(y)
    sg = jax.nn.sigmoid(_dot(z.astype(BF16), wg_ref[...]) + bg_ref[...])
    y_ssm = z * sg
    a_ssm = _dot(y_ssm.astype(BF16), ps_ref[...])
    a_attn = _dot_tn(yt_attn.astype(BF16), pa_ref[...])
    s_ssm = jax.nn.sigmoid(gates[:, 0:d])
    s_attn = jax.nn.sigmoid(gates[:, d:2 * d])
    merged = s_ssm * a_ssm + s_attn * a_attn
    return y, z, sg, y_ssm, a_ssm, a_attn, s_ssm, s_attn, merged


def _mixer_out_fwd(x0, u, yf, yb, y_attn, gates, ssm_d, w_glu, b_glu, p_ssm, p_attn, w_out):
    t, d = x0.shape
    tm = ROW_TILE
    du = d // 2

    def body(x_ref, u_ref, yf_ref, yb_ref, ya_ref, gt_ref, d_ref, wg_ref, bg_ref, ps_ref, pa_ref, wo_ref, x1_ref):
        vals = _mixer_values(u_ref[...], yf_ref[...], yb_ref[...], ya_ref[...], gt_ref[...],
                             d_ref, wg_ref, bg_ref, ps_ref, pa_ref, d)
        merged = vals[-1]
        x1_ref[...] = x_ref[...] + _dot(merged.astype(BF16), wo_ref[...])

    return _pcall(
        body, name="mixer_out_fwd", grid=(t // tm,),
        in_specs=[_row_spec(tm, d), _row_spec(tm, du), _row_spec(tm, du), _row_spec(tm, du),
                  pl.BlockSpec((d, tm), lambda i: (0, i)), _row_spec(tm, 2 * d), _full_spec((1, du)), _full_spec((du, du)), _full_spec((1, du)),
                  _full_spec((du, d)), _full_spec((d, d)), _full_spec((d, d))],
        out_specs=_row_spec(tm, d), out_shape=jax.ShapeDtypeStruct((t, d), F32), vmem=V7X_VMEM_LIMIT,
    )(x0, u, yf, yb, y_attn, gates, ssm_d, w_glu, b_glu, p_ssm, p_attn, w_out)


def _mlp_loss_fwd_bwd(x1, target, g_mlp, g_fin, w1, w2, n_valid):
    t, d = x1.shape
    tm = ROW_TILE_BWD
    dff = 4 * d
    nfc, _, fc = w1.shape

    def body(x_ref, tg_ref, gm_ref, gf_ref, w1_hbm, w2_hbm,
             dx1_ref, loss_ref, dgf_ref, dgm_ref, h2_ref, da_ref, hsq_ref, dx2b_ref,
             w1_ref, w2_ref, relu_ref, sem):
        i = pl.program_id(0)
        _load_once(i, [(w1_hbm, w1_ref), (w2_hbm, w2_ref)], sem)

        @pl.when(i == 0)
        def _():
            loss_ref[...] = jnp.zeros_like(loss_ref)
            dgf_ref[...] = jnp.zeros_like(dgf_ref)
            dgm_ref[...] = jnp.zeros_like(dgm_ref)

        x1v = x_ref[...]
        r1 = lax.rsqrt(jnp.mean(x1v * x1v, axis=-1, keepdims=True) + NORM_EPS)
        xh1 = x1v * r1
        h2b = (xh1 * gm_ref[...]).astype(BF16)
        h2_ref[...] = h2b
        acc = jnp.zeros((tm, d), F32)
        for c in range(nfc):
            a = jnp.maximum(_dot(h2b, w1_ref[c]), 0.0)
            relu_ref[:, fc * c:fc * (c + 1)] = a
            hs = (a * a).astype(BF16)
            hsq_ref[:, fc * c:fc * (c + 1)] = hs
            acc = acc + _dot(hs, w2_ref[c])
        x2 = x1v + acc
        r2 = lax.rsqrt(jnp.mean(x2 * x2, axis=-1, keepdims=True) + NORM_EPS)
        xh2 = x2 * r2
        out = xh2 * gf_ref[...]
        row = i * tm + lax.broadcasted_iota(jnp.int32, (tm, 1), 0)
        valid = jnp.logical_and(row >= N_META, row < n_valid)
        diff = jnp.where(valid, out - tg_ref[...], 0.0)
        loss_ref[...] += 0.5 * jnp.sum(jnp.sum(diff * diff, axis=-1, keepdims=True) * (1.0 / d))
        dout = diff * (1.0 / d)
        dgf_ref[...] += jnp.sum(dout * xh2, axis=0, keepdims=True)
        dxh2 = dout * gf_ref[...]
        dx2 = r2 * (dxh2 - xh2 * jnp.mean(dxh2 * xh2, axis=-1, keepdims=True))
        dx2b = dx2.astype(BF16)
        dx2b_ref[...] = dx2b
        dh2 = jnp.zeros((tm, d), F32)
        for c in range(nfc):
            dhs = _dot_nt(dx2b, w2_ref[c])
            da = (dhs * (2.0 * relu_ref[:, fc * c:fc * (c + 1)])).astype(BF16)
            da_ref[:, fc * c:fc * (c + 1)] = da
            dh2 = dh2 + _dot_nt(da, w1_ref[c])
        dgm_ref[...] += jnp.sum(dh2 * xh1, axis=0, keepdims=True)
        dxh1 = dh2 * gm_ref[...]
        dx1_ref[...] = dx2 + r1 * (dxh1 - xh1 * jnp.mean(dxh1 * xh1, axis=-1, keepdims=True))

    return _pcall(
        body, name="mlp_loss_fwd_bwd", grid=(t // tm,),
        in_specs=[_row_spec(tm, d), _row_spec(tm, d), _full_spec((1, d)), _full_spec((1, d)), _ANY, _ANY],
        out_specs=[_row_spec(tm, d), _full_spec((8, 128)), _full_spec((1, d)), _full_spec((1, d)),
                   _row_spec(tm, d), _row_spec(tm, dff), _row_spec(tm, dff), _row_spec(tm, d)],
        out_shape=[jax.ShapeDtypeStruct((t, d), F32), jax.ShapeDtypeStruct((8, 128), F32),
                   jax.ShapeDtypeStruct((1, d), F32), jax.ShapeDtypeStruct((1, d), F32),
                   jax.ShapeDtypeStruct((t, d), BF16), jax.ShapeDtypeStruct((t, dff), BF16),
                   jax.ShapeDtypeStruct((t, dff), BF16), jax.ShapeDtypeStruct((t, d), BF16)],
        scratch=[pltpu.VMEM((nfc, d, fc), BF16), pltpu.VMEM((nfc, fc, d), BF16), pltpu.VMEM((tm, dff), F32),
                 pltpu.SemaphoreType.DMA((2,))],
        vmem=V7X_VMEM_LIMIT,
    )(x1, target, g_mlp, g_fin, w1, w2)


def _mixer_out_bwd(dx1, u, yf, yb, yt_attn, gates, ssm_d, w_glu, b_glu, p_ssm, p_attn, w_out):
    t, d = dx1.shape
    tm = ROW_TILE_BWD
    du = d // 2
    last = t // tm - 1

    def body(dx_ref, u_ref, yf_ref, yb_ref, yt_ref, gt_ref, d_ref, wg_ref, bg_ref, ps_ref, pa_ref, wo_ref,
             dyb_ref, dud_ref, dyat_ref, dgates_ref, dd_ref, dbg_ref, dwg_hbm, dps_hbm, dpa_hbm, dwo_hbm,
             dwg_scr, dps_scr, dpa_scr, dwo_scr, sem):
        i = pl.program_id(0)
        accumulators = (dd_ref, dbg_ref, dwg_scr, dps_scr, dpa_scr, dwo_scr)

        @pl.when(i == 0)
        def _():
            for ref in accumulators:
                ref[...] = jnp.zeros_like(ref)

        uv = u_ref[...]
        y, z, sg, y_ssm, a_ssm, a_attn, s_ssm, s_attn, merged = _mixer_values(
            uv, yf_ref[...], yb_ref[...], yt_ref[...], gt_ref[...], d_ref, wg_ref, bg_ref, ps_ref, pa_ref, d)
        dxb = dx_ref[...].astype(BF16)
        dwo_scr[...] += _dot_tn(merged.astype(BF16), dxb)
        dmerged = _dot_nt(dxb, wo_ref[...])
        dgates_ref[:, 0:d] = (dmerged * a_ssm * (s_ssm * (1.0 - s_ssm))).astype(BF16)
        dgates_ref[:, d:2 * d] = (dmerged * a_attn * (s_attn * (1.0 - s_attn))).astype(BF16)
        da_ssm = (dmerged * s_ssm).astype(BF16)
        da_attn = (dmerged * s_attn).astype(BF16)
        dps_scr[...] += _dot_tn(y_ssm.astype(BF16), da_ssm)
        dpa_scr[...] += _dot(yt_ref[...].astype(BF16), da_attn)
        dy_ssm = _dot_nt(da_ssm, ps_ref[...])
        dyat_ref[...] = _dot_nt(pa_ref[...], da_attn).astype(BF16)
        dgl = dy_ssm * z * (sg * (1.0 - sg))
        dglb = dgl.astype(BF16)
        dwg_scr[...] += _dot_tn(z.astype(BF16), dglb)
        dbg_ref[...] += jnp.sum(dgl, axis=0, keepdims=True)
        dz = dy_ssm * sg + _dot_nt(dglb, wg_ref[...])
        dy = dz * _gelu_grad(y)
        dyb_ref[...] = dy.astype(BF16)
        dd_ref[...] += jnp.sum(dy * uv, axis=0, keepdims=True)
        dud_ref[...] = dy * d_ref[...]

        @pl.when(i == last)
        def _():
            pairs = ((dwg_scr, dwg_hbm), (dps_scr, dps_hbm), (dpa_scr, dpa_hbm), (dwo_scr, dwo_hbm))
            copies = [pltpu.make_async_copy(src, dst, sem.at[k]) for k, (src, dst) in enumerate(pairs)]
            for cp in copies:
                cp.start()
            for cp in copies:
                cp.wait()

    bf = lambda w: jax.ShapeDtypeStruct((t, w), BF16)
    f32 = lambda *shape: jax.ShapeDtypeStruct(shape, F32)
    return _pcall(
        body, name="mixer_out_bwd", grid=(t // tm,),
        in_specs=[_row_spec(tm, d), _row_spec(tm, du), _row_spec(tm, du), _row_spec(tm, du),
                  pl.BlockSpec((d, tm), lambda i: (0, i)),
                  _row_spec(tm, 2 * d), _full_spec((1, du)), _full_spec((du, du)), _full_spec((1, du)),
                  _full_spec((du, d)), _full_spec((d, d)), _full_spec((d, d))],
        out_specs=[_row_spec(tm, du), _row_spec(tm, du), pl.BlockSpec((d, tm), lambda i: (0, i)),
                   _row_spec(tm, 2 * d), _full_spec((1, du)), _full_spec((1, du)), _ANY, _ANY, _ANY, _ANY],
        out_shape=[bf(du), f32(t, du), jax.ShapeDtypeStruct((d, t), BF16), bf(2 * d), f32(1, du), f32(1, du),
                   f32(du, du), f32(du, d), f32(d, d), f32(d, d)],
        scratch=[pltpu.VMEM((du, du), F32), pltpu.VMEM((du, d), F32), pltpu.VMEM((d, d), F32),
                 pltpu.VMEM((d, d), F32), pltpu.SemaphoreType.DMA((4,))],
        vmem=V7X_VMEM_LIMIT,
    )(dx1, u, yf, yb, yt_attn, gates, ssm_d, w_glu, b_glu, p_ssm, p_attn, w_out)


def _in_proj_bwd(x0, dx1, dud, duf, dub, qraw, kraw, dq, dk, dv, dgates, g_mix, w_in, qg, kg, cos, sin):
    t, d = x0.shape
    tm = ROW_TILE_BWD
    du, dkw = d // 2, d // 4
    nh, nkv = d // HEAD_DIM, d // HEAD_DIM // KV_REP
    bw = w_in.shape[-1]
    o_q, o_k, o_v, o_g = du, du + d, du + d + dkw, 2 * d

    def body(x_ref, dx1_ref, dud_ref, duf_ref, dub_ref, qraw_ref, kraw_ref, dq_ref, dk_ref, dv_ref, dgt_ref,
             g_ref, w_hbm, qg_ref, kg_ref, c_ref, s_ref,
             dx0_ref, dproj_ref, dgm_ref, dqg_ref, dkg_ref,
             w_ref, kv_ref, sem):
        i = pl.program_id(0)
        _load_once(i, [(w_hbm, w_ref)], sem)

        @pl.when(i == 0)
        def _():
            dgm_ref[...] = jnp.zeros_like(dgm_ref)
            dqg_ref[...] = jnp.zeros_like(dqg_ref)
            dkg_ref[...] = jnp.zeros_like(dkg_ref)

        lane = lax.broadcasted_iota(jnp.int32, (tm, 128), 1)
        lo = lane < HEAD_DIM
        even = (lane & 1) == 0
        c = c_ref[...]
        s = s_ref[...]

        def norm_rope_bwd(dout, raw, g128):
            rr = lax.rsqrt(_head_mean(raw * raw) + NORM_EPS)
            xh = raw * rr
            dqn = dout * c + _swap_pairs(dout * s, even)
            dg = jnp.sum(dqn * xh, axis=0, keepdims=True)
            tt = dqn * g128
            return rr * (tt - xh * _head_mean(tt * xh)), dg

        dub_tot = (dud_ref[...] + duf_ref[...]) + dub_ref[...]
        dproj_ref[:, 0:du] = dub_tot.astype(BF16)
        dqg = jnp.zeros((1, 128), F32)
        for a in range(nh // 2):
            sl = slice(128 * a, 128 * (a + 1))
            draw, dg = norm_rope_bwd(dq_ref[sl, :].T * ATTN_SCALE, qraw_ref[:, sl], qg_ref[...])
            dqg = dqg + dg
            dproj_ref[:, o_q + 128 * a:o_q + 128 * (a + 1)] = draw.astype(BF16)
        dqg_ref[...] += dqg
        for hh in range(nkv):
            kv_ref[:, HEAD_DIM * hh:HEAD_DIM * (hh + 1)] = dk_ref[hh, :, 0:HEAD_DIM]
        dkg = jnp.zeros((1, 128), F32)
        for a in range(nkv // 2):
            sl = slice(128 * a, 128 * (a + 1))
            draw, dg = norm_rope_bwd(kv_ref[:, sl], kraw_ref[:, sl], kg_ref[...])
            dkg = dkg + dg
            dproj_ref[:, o_k + 128 * a:o_k + 128 * (a + 1)] = draw.astype(BF16)
        dkg_ref[...] += dkg
        for hh in range(nkv):
            kv_ref[:, HEAD_DIM * hh:HEAD_DIM * (hh + 1)] = dv_ref[hh]
        dproj_ref[:, o_v:o_g] = kv_ref[...].astype(BF16)
        dproj_ref[:, o_g:4 * d] = dgt_ref[...]
        dh = jnp.zeros((tm, d), F32)
        for blk in range(N_DEV):
            dh = dh + _dot_nt(dproj_ref[:, bw * blk:bw * (blk + 1)], w_ref[blk])
        x = x_ref[...]
        r = lax.rsqrt(jnp.mean(x * x, axis=-1, keepdims=True) + NORM_EPS)
        xh0 = x * r
        dgm_ref[...] += jnp.sum(dh * xh0, axis=0, keepdims=True)
        dxh = dh * g_ref[...]
        dx0_ref[...] = dx1_ref[...] + r * (dxh - xh0 * jnp.mean(dxh * xh0, axis=-1, keepdims=True))

    return _pcall(
        body, name="in_proj_bwd", grid=(t // tm,),
        in_specs=[_row_spec(tm, d), _row_spec(tm, d), _row_spec(tm, du), _row_spec(tm, du), _row_spec(tm, du),
                  _row_spec(tm, d), _row_spec(tm, dkw), pl.BlockSpec((d, tm), lambda i: (0, i)),
                  pl.BlockSpec((nkv, tm, 128), lambda i: (0, i, 0)), _heads_spec(nkv, tm),
                  _row_spec(tm, 2 * d), _full_spec((1, d)), _ANY, _full_spec((1, 128)), _full_spec((1, 128)),
                  _row_spec(tm, 128), _row_spec(tm, 128)],
        out_specs=[_row_spec(tm, d), _row_spec(tm, 4 * d), _full_spec((1, d)), _full_spec((1, 128)),
                   _full_spec((1, 128))],
        out_shape=[jax.ShapeDtypeStruct((t, d), F32), jax.ShapeDtypeStruct((t, 4 * d), BF16),
                   jax.ShapeDtypeStruct((1, d), F32), jax.ShapeDtypeStruct((1, 128), F32),
                   jax.ShapeDtypeStruct((1, 128), F32)],
        scratch=[pltpu.VMEM((N_DEV, d, bw), BF16), pltpu.VMEM((tm, dkw), F32), pltpu.SemaphoreType.DMA((1,))],
        vmem=V7X_VMEM_LIMIT,
    )(x0, dx1, dud, duf, dub, qraw, kraw, dq, dk, dv, dgates, g_mix, w_in, qg, kg, cos, sin)


def _attn_fwd(qat, ka, vta):
    nh, _, t = qat.shape
    nkv = ka.shape[0]
    rep = nh // nkv
    hd = HEAD_DIM
    vr = vta.shape[1]
    tq = tk = KV_TILE

    def body(qt_ref, k_ref, vt_ref, ot_ref, lse_ref, m_scr, acc_scr, excess_scr):
        j = pl.program_id(2)
        src = j % 2
        dst = 1 - src

        @pl.when(j == 0)
        def _():
            m_scr[0] = jnp.full(m_scr.shape[1:], NEG_BIG, F32)
            acc_scr[0] = jnp.zeros(acc_scr.shape[1:], F32)
            excess_scr[...] = jnp.full(excess_scr.shape, -NEG_BIG, F32)

        k = k_ref[0]
        vt = vt_ref[0]
        strips = [(r, c) for r in range(rep) for c in range(0, tq, QUERY_STRIP)]
        scores = lambda r, c: _dot(k, qt_ref[r, :, c:c + QUERY_STRIP])

        def sweep(one_pass):
            def add_values(r, cols, before, after, pt):
                acc = acc_scr[src, r, :, cols]
                acc_scr[dst, r, :, cols] = after * ((acc if before is None else before * acc) + _dot(vt, pt))

            ahead = [scores(*strips[0]), scores(*strips[1])]
            pending = None
            excess = jnp.full((1, QUERY_STRIP), NEG_BIG, F32)
            for n, (r, c) in enumerate(strips):
                st = ahead.pop(0)
                if n + 2 < len(strips):
                    ahead.append(scores(*strips[n + 2]))
                cols = slice(c, c + QUERY_STRIP)
                m_prev = m_scr[src, r, :, cols]
                if one_pass:
                    pt = jnp.exp(st - m_prev).astype(BF16)
                    tile_max = jnp.max(st, axis=0, keepdims=True)
                    m_next = jnp.maximum(m_prev, tile_max)
                    excess = jnp.maximum(excess, tile_max - m_prev)
                    factors = (None, jnp.exp(m_prev - m_next))
                else:
                    m_next = jnp.maximum(m_prev, jnp.max(st, axis=0, keepdims=True))
                    pt = jnp.exp(st - m_next).astype(BF16)
                    factors = (jnp.exp(m_prev - m_next), 1.0)
                m_scr[dst, r, :, cols] = m_next
                if pending is not None:
                    add_values(*pending)
                pending = (r, cols, *factors, pt)
            add_values(*pending)
            return excess

        @pl.when(j > 0)
        def _():
            excess_scr[...] = sweep(one_pass=True)

        @pl.when(jnp.max(excess_scr[...]) > ONE_PASS_SLACK)
        def _():
            sweep(one_pass=False)

        @pl.when(j == pl.num_programs(2) - 1)
        def _():
            for r in range(rep):
                l = acc_scr[dst, r, hd:hd + 1, :]
                ot_ref[hd * r:hd * (r + 1), :] = acc_scr[dst, r, 0:hd, :] / l
                lse_ref[0, r:r + 1, :] = m_scr[dst, r] + jnp.log(l)

    return _pcall(
        body, name="attn_fwd", grid=(nkv, t // tq, t // tk),
        in_specs=[pl.BlockSpec((rep, 128, tq), lambda g, i, j: (g, 0, i)),
                  pl.BlockSpec((1, tk, 128), lambda g, i, j: (g, j, 0)),
                  pl.BlockSpec((1, vr, tk), lambda g, i, j: (g, 0, j))],
        out_specs=[pl.BlockSpec((rep * hd, tq), lambda g, i, j: (g, i)),
                   pl.BlockSpec((1, rep, tq), lambda g, i, j: (g, 0, i))],
        out_shape=[jax.ShapeDtypeStruct((nh * hd, t), F32), jax.ShapeDtypeStruct((nkv, rep, t), F32)],
        scratch=[pltpu.VMEM((2, rep, 1, tq), F32), pltpu.VMEM((2, rep, vr, tq), F32),
                 pltpu.VMEM((1, QUERY_STRIP), F32)],
        vmem=V7X_VMEM_LIMIT,
    )(qat, ka, vta)


def _attn_bwd(qat, ka, kt, va, dot, ot, lse_row):
    nh, _, t = qat.shape
    nkv = ka.shape[0]
    rep = nh // nkv
    hd = HEAD_DIM
    tq = tk = KV_TILE

    def body(qt_ref, k_ref, kt_ref, v_ref, dot_ref, ot_ref, lse_ref, dk_ref, dv_ref, dqt_ref):
        j = pl.program_id(1)
        i = pl.program_id(2)

        @pl.when(jnp.logical_and(j == 0, i == 0))
        def _():
            dqt_ref[...] = jnp.zeros_like(dqt_ref)

        @pl.when(i == 0)
        def _():
            dk_ref[...] = jnp.zeros_like(dk_ref)
            dv_ref[...] = jnp.zeros_like(dv_ref)

        k = k_ref[0]
        kt = kt_ref[0]
        v = v_ref[0, :, 0:hd]
        cols = pl.ds(pl.multiple_of(i * tq, tq), tq)
        dk = jnp.zeros((tk, 128), F32)
        dv = jnp.zeros((tk, hd), F32)
        products = lambda r: (_dot(k, qt_ref[r]), _dot(v, dot_ref[hd * r:hd * (r + 1), :]))
        nxt = products(0)
        for r in range(rep):
            st, dpt = nxt
            if r + 1 < rep:
                nxt = products(r + 1)
            heads = slice(hd * r, hd * (r + 1))
            qt = qt_ref[r]
            dot_r = dot_ref[heads, :]
            delta = jnp.sum(dot_r.astype(F32) * ot_ref[heads, :], axis=0, keepdims=True)
            pt = jnp.exp(st - lse_ref[0, r:r + 1, :])
            dst = (pt * (dpt - delta)).astype(BF16)
            dv = dv + _dot_nt(pt.astype(BF16), dot_r)
            dk = dk + _dot_nt(dst, qt)
            dqt_ref[heads, cols] += _dot(kt, dst)
        dk_ref[0] += dk
        dv_ref[0] += dv

    return _pcall(
        body, name="attn_bwd", grid=(nkv, t // tk, t // tq),
        in_specs=[pl.BlockSpec((rep, 128, tq), lambda g, j, i: (g, 0, i)),
                  pl.BlockSpec((1, tk, 128), lambda g, j, i: (g, j, 0)),
                  pl.BlockSpec((1, hd, tk), lambda g, j, i: (g, 0, j)),
                  pl.BlockSpec((1, tk, 128), lambda g, j, i: (g, j, 0)),
                  pl.BlockSpec((rep * hd, tq), lambda g, j, i: (g, i)),
                  pl.BlockSpec((rep * hd, tq), lambda g, j, i: (g, i)),
                  pl.BlockSpec((1, rep, tq), lambda g, j, i: (g, 0, i))],
        out_specs=[pl.BlockSpec((1, tk, 128), lambda g, j, i: (g, j, 0)),
                   pl.BlockSpec((1, tk, hd), lambda g, j, i: (g, j, 0)),
                   pl.BlockSpec((rep * hd, t), lambda g, j, i: (g, 0))],
        out_shape=[jax.ShapeDtypeStruct((nkv, t, 128), F32), jax.ShapeDtypeStruct((nkv, t, hd), F32),
                   jax.ShapeDtypeStruct((nh * hd, t), F32)],
        vmem=V7X_VMEM_LIMIT,
    )(qat, ka, kt, va, dot, ot, lse_row)


def _riding_exchange(refs, exchange, n_in, n_out, first_step, last_step):
    if exchange is None:
        return refs
    x_ref, out_ref = refs[n_in], refs[n_in + 1 + n_out]
    sems = refs[-3:]

    @pl.when(first_step)
    def _():
        _start_all(*_exchange_copies(x_ref, out_ref, *sems, exchange[1]))

    @pl.when(last_step)
    def _():
        _wait_all(*_exchange_copies(x_ref, out_ref, *sems, exchange[1]))

    return refs[:n_in] + refs[n_in + 1:n_in + 1 + n_out] + refs[n_in + 2 + n_out:-3]


def _segmented_scan(src_re, src_im, dst_re, dst_im, lam_re, lam_im, pow_re, pow_im, carry_re, carry_im,
                    end_re, end_im, in_re, in_im, lanes, descending, conj):
    tc = src_re.shape[0]
    seg = tc // 8
    width = lanes.size
    sign = -1.0 if conj else 1.0
    rows_of = lambda q: pl.ds(8 * (seg - 1 - q if descending else q), 8)
    lr = jnp.broadcast_to(lam_re[:, lanes], (8, width))
    li = jnp.broadcast_to(sign * lam_im[:, lanes], (8, width))
    xr = jnp.zeros((8, width), F32)
    xi = jnp.zeros((8, width), F32)
    for q in range(seg):
        rows = rows_of(q)
        xr, xi = (lr * xr - li * xi) + src_re[rows, lanes], (lr * xi + li * xr) + src_im[rows, lanes]
        dst_re[rows, lanes] = xr
        dst_im[rows, lanes] = xi
    end_re[:, lanes] = xr
    end_im[:, lanes] = xi
    sr = pow_re[seg - 1:seg, lanes]
    si = sign * pow_im[seg - 1:seg, lanes]
    cr = carry_re[:, lanes]
    ci = carry_im[:, lanes]
    for s in range(8):
        se = 7 - s if descending else s
        in_re[se:se + 1, lanes] = cr
        in_im[se:se + 1, lanes] = ci
        cr, ci = (end_re[se:se + 1, lanes] + (sr * cr - si * ci)), (end_im[se:se + 1, lanes] + (sr * ci + si * cr))
    carry_re[:, lanes] = cr
    carry_im[:, lanes] = ci
    ir = in_re[:, lanes]
    ii = in_im[:, lanes]
    for q in range(seg):
        rows = rows_of(q)
        pr = pow_re[q:q + 1, lanes]
        pi = sign * pow_im[q:q + 1, lanes]
        dst_re[rows, lanes] = dst_re[rows, lanes] + (pr * ir - pi * ii)
        dst_im[rows, lanes] = dst_im[rows, lanes] + (pr * ii + pi * ir)


def _diag_tiles(gn):
    rows_per_tile = DIAG_TILE // (SSM_STATE // SSM_GROUP)
    return [(slice(rows_per_tile * j, rows_per_tile * (j + 1)), slice(DIAG_TILE * j, DIAG_TILE * (j + 1)))
            for j in range(gn // DIAG_TILE)]


def _ssm_scan_fwd(ub, lam_re, lam_im, pow_re, pow_im, bb_re, bb_im, cc_re, cc_im, exchange=None):
    t, w = ub.shape
    gn = lam_re.shape[-1]
    tc = ROW_TILE
    cl = min(gn, SCAN_LANES)
    nblk = t // tc
    tiles = _diag_tiles(gn)

    def body(*refs):
        first = jnp.logical_and(pl.program_id(0) == 0, pl.program_id(1) == 0)
        last = jnp.logical_and(pl.program_id(0) == 1, pl.program_id(1) == nblk - 1)
        (u_ref, lr_ref, li_ref, pr_ref, pi_ref, br_ref, bi_ref, cr_ref, ci_ref, y_ref, xr_ref, xi_ref,
         bur_scr, bui_scr, cr_scr, ci_scr, er_scr, ei_scr, nr_scr, ni_scr) = _riding_exchange(
             refs, exchange, 9, 3, first, last)

        @pl.when(pl.program_id(1) == 0)
        def _():
            cr_scr[...] = jnp.zeros_like(cr_scr)
            ci_scr[...] = jnp.zeros_like(ci_scr)

        for rows, lanes in tiles:
            u_j = u_ref[:, rows]
            bur_scr[:, lanes] = _dot(u_j, br_ref[0, rows, lanes])
            bui_scr[:, lanes] = _dot(u_j, bi_ref[0, rows, lanes])
        for descending in (False, True):
            @pl.when(pl.program_id(0) == int(descending))
            def _(descending=descending):
                for c0 in range(0, gn, cl):
                    _segmented_scan(bur_scr, bui_scr, xr_ref.at[0], xi_ref.at[0], lr_ref.at[0], li_ref.at[0],
                                    pr_ref.at[0], pi_ref.at[0], cr_scr, ci_scr, er_scr, ei_scr, nr_scr, ni_scr,
                                    pl.ds(c0, cl), descending, conj=False)
        for rows, lanes in tiles:
            y_ref[0, :, rows] = (_dot(xr_ref[0, :, lanes].astype(BF16), cr_ref[0, lanes, rows])
                                 - _dot(xi_ref[0, :, lanes].astype(BF16), ci_ref[0, lanes, rows]))

    blk = lambda dd, i: jnp.where(dd == 0, i, nblk - 1 - i)
    row = lambda width: pl.BlockSpec((1, tc, width), lambda dd, i: (dd, blk(dd, i), 0))
    per_dir = lambda a, b: pl.BlockSpec((1, a, b), lambda dd, i: (dd, 0, 0))
    extra = exchange is not None
    return _pcall(
        body, name="ssm_scan_fwd", grid=(2, nblk),
        in_specs=[pl.BlockSpec((tc, w), lambda dd, i: (blk(dd, i), 0)), per_dir(1, gn), per_dir(1, gn),
                  per_dir(tc // 8, gn), per_dir(tc // 8, gn),
                  per_dir(w, gn), per_dir(w, gn), per_dir(gn, w), per_dir(gn, w)] + [_ANY] * extra,
        out_specs=[row(w), row(gn), row(gn)] + [_ANY] * extra,
        out_shape=[jax.ShapeDtypeStruct((2, t, w), F32), jax.ShapeDtypeStruct((2, t, gn), F32),
                   jax.ShapeDtypeStruct((2, t, gn), F32)] + ([_exchange_out_shape(*exchange)] if extra else []),
        scratch=[pltpu.VMEM((tc, gn), F32), pltpu.VMEM((tc, gn), F32), pltpu.VMEM((1, gn), F32),
                 pltpu.VMEM((1, gn), F32)] + [pltpu.VMEM((8, gn), F32)] * 4 + _EXCHANGE_SEMS * extra,
        vmem=V7X_VMEM_LIMIT,
    )(ub, lam_re, lam_im, pow_re, pow_im, bb_re, bb_im, cc_re, cc_im, *([exchange[0]] if extra else []))


def _ssm_scan_bwd(dyb, ub, xs_re, xs_im, lam_re, lam_im, pow_re, pow_im, cct_re, cct_im, bbt_re, bbt_im,
                  exchange=None):
    t, w = dyb.shape
    gn = lam_re.shape[-1]
    tc = ROW_TILE
    cl = min(gn, SCAN_LANES)
    nblk = t // tc
    tiles = _diag_tiles(gn)

    def body(*refs):
        i = pl.program_id(1)
        first = jnp.logical_and(pl.program_id(0) == 0, i == 0)
        last = jnp.logical_and(pl.program_id(0) == 1, i == nblk - 1)
        (dy_ref, u_ref, xr_ref, xi_ref, hr_ref, hi_ref, lr_ref, li_ref, pr_ref, pi_ref, ctr_ref, cti_ref, btr_ref,
         bti_ref, du_ref, dlr_ref, dli_ref, dbr_ref, dbi_ref, dcr_ref, dci_ref,
         gxr_scr, gxi_scr, cr_scr, ci_scr, ar_scr, ai_scr, er_scr, ei_scr, nr_scr, ni_scr) = _riding_exchange(
             refs, exchange, 14, 7, first, last)

        @pl.when(i == 0)
        def _():
            for ref in (cr_scr, ci_scr, ar_scr, ai_scr, dbr_ref, dbi_ref, dcr_ref, dci_ref):
                ref[...] = jnp.zeros_like(ref)

        for rows, lanes in tiles:
            dy_j = dy_ref[:, rows]
            gxr_scr[:, lanes] = _dot(dy_j, ctr_ref[0, rows, lanes])
            gxi_scr[:, lanes] = -_dot(dy_j, cti_ref[0, rows, lanes])
        first_block = i == nblk - 1
        sublane = lax.broadcasted_iota(jnp.int32, (8, 1), 0)

        def lam_gradient(state_descending):
            for c0 in range(0, gn, 512):
                lanes = pl.ds(c0, 512)
                if state_descending:
                    cur, prev, edge, src = pl.ds(0, tc - 8), pl.ds(8, tc - 8), pl.ds(tc - 8, 8), pl.ds(0, 8)
                    halo_at, halo_row, shift = 7, 0, 7
                else:
                    cur, prev, edge, src = pl.ds(8, tc - 8), pl.ds(0, tc - 8), pl.ds(0, 8), pl.ds(tc - 8, 8)
                    halo_at, halo_row, shift = 0, 7, 1
                halo_r = jnp.where(first_block, 0.0, hr_ref[0, halo_row:halo_row + 1, lanes])
                halo_i = jnp.where(first_block, 0.0, hi_ref[0, halo_row:halo_row + 1, lanes])
                xer = jnp.where(sublane == halo_at, halo_r, pltpu.roll(xr_ref[0, src, lanes], shift, 0))
                xei = jnp.where(sublane == halo_at, halo_i, pltpu.roll(xi_ref[0, src, lanes], shift, 0))
                gr, gi = gxr_scr[cur, lanes], gxi_scr[cur, lanes]
                xpr, xpi = xr_ref[0, prev, lanes], xi_ref[0, prev, lanes]
                ger, gei = gxr_scr[edge, lanes], gxi_scr[edge, lanes]
                ar_scr[:, lanes] += (jnp.sum(gr * xpr + gi * xpi, axis=0, keepdims=True)
                                     + jnp.sum(ger * xer + gei * xei, axis=0, keepdims=True))
                ai_scr[:, lanes] += (jnp.sum(gi * xpr - gr * xpi, axis=0, keepdims=True)
                                     + jnp.sum(gei * xer - ger * xei, axis=0, keepdims=True))

        for descending in (True, False):
            @pl.when(pl.program_id(0) == int(not descending))
            def _(descending=descending):
                for c0 in range(0, gn, cl):
                    _segmented_scan(gxr_scr, gxi_scr, gxr_scr, gxi_scr, lr_ref.at[0], li_ref.at[0], pr_ref.at[0],
                                    pi_ref.at[0], cr_scr, ci_scr, er_scr, ei_scr, nr_scr, ni_scr, pl.ds(c0, cl),
                                    descending, conj=True)
                lam_gradient(state_descending=not descending)
        dlr_ref[0] = ar_scr[...]
        dli_ref[0] = ai_scr[...]
        for rows, lanes in tiles:
            grb = gxr_scr[:, lanes].astype(BF16)
            gib = gxi_scr[:, lanes].astype(BF16)
            du_ref[0, :, rows] = _dot(grb, btr_ref[0, lanes, rows]) + _dot(gib, bti_ref[0, lanes, rows])
            u_j = u_ref[:, rows]
            dy_j = dy_ref[:, rows]
            dbr_ref[0, rows, :] += _dot_tn(u_j, grb)
            dbi_ref[0, rows, :] += _dot_tn(u_j, gib)
            dcr_ref[0, rows, :] += _dot_tn(dy_j, xr_ref[0, :, lanes].astype(BF16))
            dci_ref[0, rows, :] -= _dot_tn(dy_j, xi_ref[0, :, lanes].astype(BF16))

    blk = lambda dd, i: jnp.where(dd == 0, nblk - 1 - i, i)
    rev = lambda width: pl.BlockSpec((1, tc, width), lambda dd, i: (dd, blk(dd, i), 0))
    halo_blk = lambda dd, i: jnp.where(dd == 0, jnp.maximum(blk(dd, i) * (tc // 8) - 1, 0),
                                       jnp.minimum((blk(dd, i) + 1) * (tc // 8), t // 8 - 1))
    halo = pl.BlockSpec((1, 8, gn), lambda dd, i: (dd, halo_blk(dd, i), 0))
    per_dir = lambda a, b: pl.BlockSpec((1, a, b), lambda dd, i: (dd, 0, 0))
    extra = exchange is not None
    return _pcall(
        body, name="ssm_scan_bwd", grid=(2, nblk),
        in_specs=[pl.BlockSpec((tc, w), lambda dd, i: (blk(dd, i), 0)),
                  pl.BlockSpec((tc, w), lambda dd, i: (blk(dd, i), 0)), rev(gn), rev(gn), halo, halo,
                  per_dir(1, gn), per_dir(1, gn), per_dir(tc // 8, gn), per_dir(tc // 8, gn),
                  per_dir(w, gn), per_dir(w, gn), per_dir(gn, w), per_dir(gn, w)]
        + [_ANY] * extra,
        out_specs=[rev(w), per_dir(1, gn), per_dir(1, gn)] + [per_dir(w, DIAG_TILE)] * 4 + [_ANY] * extra,
        out_shape=[jax.ShapeDtypeStruct((2, t, w), F32), jax.ShapeDtypeStruct((2, 1, gn), F32),
                   jax.ShapeDtypeStruct((2, 1, gn), F32)] + [jax.ShapeDtypeStruct((2, w, DIAG_TILE), F32)] * 4
        + ([_exchange_out_shape(*exchange)] if extra else []),
        scratch=[pltpu.VMEM((tc, gn), F32), pltpu.VMEM((tc, gn), F32)] + [pltpu.VMEM((1, gn), F32)] * 4
        + [pltpu.VMEM((8, gn), F32)] * 4 + _EXCHANGE_SEMS * extra,
        vmem=V7X_VMEM_LIMIT,
    )(dyb, ub, xs_re, xs_im, xs_re, xs_im, lam_re, lam_im, pow_re, pow_im, cct_re, cct_im, bbt_re, bbt_im,
      *([exchange[0]] if extra else []))


def _matmul_tn(a, b, name, a_is_transposed=False, exchange=None):
    t, n = b.shape
    m = a.shape[0] if a_is_transposed else a.shape[1]
    bm, bn, tk = min(m, 1024), min(n, 1024), KV_TILE
    grid = (m // bm, n // bn, t // tk)

    def body(*refs):
        at = lambda step: functools.reduce(jnp.logical_and, [pl.program_id(ax) == step[ax] for ax in range(3)])
        a_ref, b_ref, o_ref = _riding_exchange(refs, exchange, 2, 1, at((0, 0, 0)), at([g - 1 for g in grid]))

        @pl.when(pl.program_id(2) == 0)
        def _():
            o_ref[...] = jnp.zeros_like(o_ref)

        mul = _dot if a_is_transposed else _dot_tn
        o_ref[...] += mul(a_ref[...].astype(BF16), b_ref[...].astype(BF16))

    a_spec = (pl.BlockSpec((bm, tk), lambda i, j, k: (i, k)) if a_is_transposed else
              pl.BlockSpec((tk, bm), lambda i, j, k: (k, i)))
    extra = exchange is not None
    out = _pcall(
        body, name=name, grid=grid,
        in_specs=[a_spec, pl.BlockSpec((tk, bn), lambda i, j, k: (k, j))] + [_ANY] * extra,
        out_specs=[pl.BlockSpec((bm, bn), lambda i, j, k: (i, j))] + [_ANY] * extra,
        out_shape=[jax.ShapeDtypeStruct((m, n), F32)] + ([_exchange_out_shape(*exchange)] if extra else []),
        scratch=_EXCHANGE_SEMS * extra, vmem=V7X_VMEM_LIMIT,
    )(a, b, *([exchange[0]] if extra else []))
    return out if extra else out[0]


def _reduce_adamw(gparts, p, m, v, name, exchange=None):
    rows, width = p.shape
    tr = max(k for k in range(16, 513, 16) if rows % k == 0)

    def body(*refs):
        i = pl.program_id(0)
        g_ref, p_ref, m_ref, v_ref, go_ref, d_ref, mo_ref, vo_ref = _riding_exchange(
            refs, exchange, 4, 4, i == 0, i == rows // tr - 1)
        g = g_ref[0].astype(F32)
        for k in range(1, N_DEV):
            g = g + g_ref[k].astype(F32)
        go_ref[...] = g
        mm = ADAM_B1 * m_ref[...] + (1.0 - ADAM_B1) * g
        vv = ADAM_B2 * v_ref[...] + (1.0 - ADAM_B2) * (g * g)
        m_hat = mm / (1.0 - ADAM_B1 ** ADAM_STEP)
        v_hat = vv / (1.0 - ADAM_B2 ** ADAM_STEP)
        d_ref[...] = -ADAM_LR * (m_hat / (jnp.sqrt(v_hat) + ADAM_EPS) + ADAM_WD * p_ref[...])
        mo_ref[...] = mm
        vo_ref[...] = vv

    spec = pl.BlockSpec((tr, width), lambda i: (i, 0))
    out = jax.ShapeDtypeStruct((rows, width), F32)
    extra = exchange is not None
    return _pcall(
        body, name=name, grid=(rows // tr,),
        in_specs=[pl.BlockSpec((N_DEV, tr, width), lambda i: (0, i, 0)), spec, spec, spec] + [_ANY] * extra,
        out_specs=[spec, spec, spec, spec] + [_ANY] * extra,
        out_shape=[out, out, out, out] + ([_exchange_out_shape(*exchange)] if extra else []),
        scratch=_EXCHANGE_SEMS * extra, vmem=V7X_VMEM_LIMIT,
    )(gparts, p, m, v, *([exchange[0]] if extra else []))


def _peer(k):
    x, y, c = lax.axis_index("x"), lax.axis_index("y"), lax.axis_index("c")
    return (x ^ ((k >> 2) & 1), y ^ ((k >> 1) & 1), c ^ (k & 1))


def _my_index():
    return 4 * lax.axis_index("x") + 2 * lax.axis_index("y") + lax.axis_index("c")


def _exchange_copies(x_ref, out_ref, send_sems, recv_sems, local_sem, scatter, first_sem=0):
    me = _my_index()
    local = pltpu.make_async_copy(x_ref.at[me] if scatter else x_ref, out_ref.at[me], local_sem)
    copies = []
    for k in range(1, N_DEV):
        peer = _peer(k)
        src = x_ref.at[4 * peer[0] + 2 * peer[1] + peer[2]] if scatter else x_ref
        copies.append(pltpu.make_async_remote_copy(
            src_ref=src, dst_ref=out_ref.at[me], send_sem=send_sems.at[first_sem + k - 1],
            recv_sem=recv_sems.at[first_sem + k - 1], device_id=peer, device_id_type=pl.DeviceIdType.MESH))
    return local, copies


def _start_all(local, copies):
    local.start()
    for cp in copies:
        cp.start()


def _wait_all(local, copies):
    for cp in copies:
        cp.wait_recv()
    for cp in copies:
        cp.wait_send()
    local.wait()


def _exchange_out_shape(x, scatter):
    return jax.ShapeDtypeStruct((N_DEV,) + tuple(x.shape[1:] if scatter else x.shape), x.dtype)


_EXCHANGE_SEMS = [pltpu.SemaphoreType.DMA((N_DEV - 1,)), pltpu.SemaphoreType.DMA((N_DEV - 1,)),
                  pltpu.SemaphoreType.DMA(())]


def _gather_two_level(x, name):
    def body(x_ref, out_ref, send_sems, recv_sems, local_sem):
        x, y, c = lax.axis_index("x"), lax.axis_index("y"), lax.axis_index("c")
        me, sibling = (x, y, c), (x, y, 1 - c)
        chips = [(1 - x, y), (x, 1 - y), (1 - x, 1 - y)]

        def copy(k, block, to, src=None):
            slot = out_ref.at[4 * block[0] + 2 * block[1] + block[2]]
            return pltpu.make_async_remote_copy(src_ref=slot if src is None else src, dst_ref=slot,
                                                send_sem=send_sems.at[k], recv_sem=recv_sems.at[k], device_id=to,
                                                device_id_type=pl.DeviceIdType.MESH)

        local = pltpu.make_async_copy(x_ref, out_ref.at[_my_index()], local_sem)
        local.start()
        first = [copy(0, me, sibling, src=x_ref)]
        first += [copy(1 + j, me, (*chip, c), src=x_ref) for j, chip in enumerate(chips)]
        for cp in first:
            cp.start()
        passed = [copy(4 + j, (*chip, c), sibling) for j, chip in enumerate(chips)]
        for j, chip in enumerate(chips):
            copy(1 + j, (*chip, c), me).wait_recv()
            passed[j].start()
        copy(0, sibling, me).wait_recv()
        for j, chip in enumerate(chips):
            copy(4 + j, (*chip, 1 - c), me).wait_recv()
        for cp in first + passed:
            cp.wait_send()
        local.wait()

    return pl.pallas_call(
        body, name=name, in_specs=[_ANY], out_specs=_ANY, out_shape=_exchange_out_shape(x, False),
        scratch_shapes=_EXCHANGE_SEMS,
    )(x)


def _to_shards(full, axis):
    r, c = full.shape
    if axis == 0:
        return full.reshape(N_DEV, r // N_DEV, c)
    return full.reshape(r, N_DEV, c // N_DEV).transpose(1, 0, 2)


def _from_shards(shards, axis):
    _, r, c = shards.shape
    if axis == 0:
        return shards.reshape(N_DEV * r, c)
    return shards.transpose(1, 0, 2).reshape(r, N_DEV * c)


def _pack_rows(parts, lead):
    flat = []
    for p in parts:
        p = p.reshape(p.shape[:lead] + (-1, PACK_W))
        pad = _round_up(p.shape[lead], 16) - p.shape[lead]
        flat.append(jnp.pad(p, [(0, 0)] * lead + [(0, pad), (0, 0)]) if pad else p)
    return jnp.concatenate(flat, axis=lead)


def _unpack_rows(packed, shapes):
    lead = packed.shape[:-2]
    out, off = [], 0
    for shp in shapes:
        rows = math.prod(shp) // PACK_W
        out.append(packed[..., off:off + rows, :].reshape(lead + tuple(shp)))
        off += _round_up(rows, 16)
    return out


def _pack_flat(parts):
    flat = jnp.concatenate([p.reshape(-1) for p in parts])
    n = flat.shape[0]
    flat = jnp.pad(flat, (0, _round_up(n, 16 * PACK_W) - n))
    return flat.reshape(-1, PACK_W)


def _unpack(packed, shapes):
    flat = packed.reshape(-1)
    out, off = [], 0
    for shp in shapes:
        n = math.prod(shp)
        out.append(flat[off:off + n].reshape(shp))
        off += n
    return out


def _ssm_discretize(a_re, a_im, log_dt, b_re, b_im):
    dt = jnp.exp(log_dt)[..., None]
    lam_re = jnp.minimum(a_re, EIG_RE_MAX)
    lam_im = a_im
    mag = jnp.exp(lam_re * dt)
    ang = lam_im * dt
    lb_re = mag * jnp.cos(ang)
    lb_im = mag * jnp.sin(ang)
    num_re = lb_re - 1.0
    num_im = lb_im
    den = lam_re * lam_re + lam_im * lam_im
    f_re = (num_re * lam_re + num_im * lam_im) / den
    f_im = (num_im * lam_re - num_re * lam_im) / den
    bb_re = f_re[..., None] * b_re - f_im[..., None] * b_im
    bb_im = f_re[..., None] * b_im + f_im[..., None] * b_re
    return lb_re, lb_im, bb_re, bb_im


def _ssm_powers(a_re, a_im, log_dt, count):
    dt = jnp.exp(log_dt)[:, None, :, None]
    k = jnp.arange(1, count + 1, dtype=F32)[None, :, None, None]
    mag = jnp.exp(k * (jnp.minimum(a_re, EIG_RE_MAX)[:, None] * dt))
    ang = k * (a_im[:, None] * dt)
    shape = (a_re.shape[0], count, -1)
    return (mag * jnp.cos(ang)).reshape(shape), (mag * jnp.sin(ang)).reshape(shape)


def _interleave(a, inverse=False):
    lead, (t, width) = a.shape[:-2], a.shape[-2:]
    seg = ROW_TILE // 8
    shape = lead + (t // ROW_TILE,) + ((seg, 8) if inverse else (8, seg)) + (width,)
    return jnp.swapaxes(a.reshape(shape), -3, -2).reshape(a.shape)


def _block_diag(blocks):
    two, g, a, b = blocks.shape
    tiled = jnp.tile(blocks.reshape(two, g * a, b), (1, 1, g))
    row_group = lax.broadcasted_iota(jnp.int32, (g * a, g * b), 0) // a
    col_group = lax.broadcasted_iota(jnp.int32, (g * a, g * b), 1) // b
    return jnp.where(row_group == col_group, tiled, 0.0).astype(BF16)


def _diag_blocks(tiles):
    two, w, _ = tiles.shape
    per = DIAG_TILE // SSM_STATE
    t6 = tiles.reshape(two, w // (per * SSM_GROUP), per, SSM_GROUP, per, SSM_STATE)
    return jnp.einsum("zjqpqn->zjqpn", t6).reshape(two, w // SSM_GROUP, SSM_GROUP, SSM_STATE)


def _rope_tables(t, n_valid):
    pos = jnp.arange(t)
    real = jnp.logical_and(pos >= N_META, pos < n_valid)
    idx = jnp.where(real, pos - N_META, 0)
    row_id = (idx // GRID_W).astype(F32)
    col_id = (idx % GRID_W).astype(F32)
    pairs_per_axis = HEAD_DIM // 4
    inv_freq = ROPE_THETA ** (-jnp.arange(pairs_per_axis, dtype=F32) / pairs_per_axis)
    ang = jnp.concatenate([row_id[:, None] * inv_freq, col_id[:, None] * inv_freq], axis=-1)
    ang = jnp.where(real[:, None], ang, 0.0)
    cos = jnp.repeat(jnp.cos(ang), 2, axis=-1)
    sin = jnp.sin(ang)
    sin = jnp.stack([-sin, sin], axis=-1).reshape(t, HEAD_DIM)
    return jnp.tile(cos, (1, 2)), jnp.tile(sin, (1, 2))


def _local_step(x, loss_target, big, small, comm=None):
    s_len, d = x.shape
    n_valid = s_len + N_META
    t = _round_up(n_valid, KV_TILE)
    du = d // 2
    groups = du // SSM_GROUP
    pad = t - n_valid

    x0 = jnp.concatenate([big["meta_tokens"].astype(F32), x, jnp.zeros((pad, d), F32)], axis=0)
    tgt = jnp.concatenate([jnp.zeros((N_META, d), F32), loss_target, jnp.zeros((pad, d), F32)], axis=0)
    cos, sin = _rope_tables(t, n_valid)
    g_mix = small["norm_mix_g"].reshape(1, d)
    g_mlp = small["norm_mlp_g"].reshape(1, d)
    g_fin = small["norm_final_g"].reshape(1, d)
    qg = jnp.tile(small["q_norm_g"].reshape(1, HEAD_DIM), (1, 2))
    kg = jnp.tile(small["k_norm_g"].reshape(1, HEAD_DIM), (1, 2))
    ssm_d = small["ssm_d"].reshape(1, du)
    b_glu = small["b_glu"].reshape(1, du)

    ssm_in = tuple(small[n][0] for n in ("ssm_a_re", "ssm_a_im", "ssm_log_dt", "ssm_b_re", "ssm_b_im"))
    (lb_re, lb_im, bbar_re, bbar_im), disc_vjp = jax.vjp(_ssm_discretize, *ssm_in)
    lam_re = lb_re.reshape(2, 1, groups * SSM_STATE)
    lam_im = lb_im.reshape(2, 1, groups * SSM_STATE)
    pow_re, pow_im = _ssm_powers(*ssm_in[0:3], ROW_TILE // 8)
    bb_re = _block_diag(bbar_re.transpose(0, 1, 3, 2))
    bb_im = _block_diag(bbar_im.transpose(0, 1, 3, 2))
    c_re, c_im = small["ssm_c_re"][0], small["ssm_c_im"][0]
    cct_re = _block_diag(c_re)
    cct_im = _block_diag(c_im)
    cc_re = cct_re.transpose(0, 2, 1)
    cc_im = cct_im.transpose(0, 2, 1)
    bbt_re = bb_re.transpose(0, 2, 1)
    bbt_im = bb_im.transpose(0, 2, 1)
    scan_w = (lam_re, lam_im, pow_re, pow_im)

    in_proj_args = (x0, g_mix, big["w_in"], qg, kg, cos, sin, n_valid)
    if comm is None:
        h, u, ub, qraw, kraw, qat, ka, kt, va, vta, gates = _in_proj_fwd(*in_proj_args)
    else:
        h, u, ub, qraw, kraw, qat, ka, kt, va, vta, gates, got = _in_proj_fwd(
            *in_proj_args, exchange=(comm["pack_weights"](MIXER_WEIGHTS), False))
        big = {**big, **comm["unpack_weights"](MIXER_WEIGHTS, got)}
    ub = _interleave(ub)
    if comm is None:
        y2, xs_re, xs_im = _ssm_scan_fwd(ub, *scan_w, bb_re, bb_im, cc_re, cc_im)
    else:
        y2, xs_re, xs_im, got = _ssm_scan_fwd(ub, *scan_w, bb_re, bb_im, cc_re, cc_im,
                                              exchange=(comm["pack_weights"](MLP_WEIGHTS), False))
        big = {**big, **comm["unpack_weights"](MLP_WEIGHTS, got)}
    y2 = _interleave(y2, inverse=True)
    yf, yb = y2[0], y2[1]
    yt_attn, lse = _attn_fwd(qat, ka, vta)
    mixer_w = (ssm_d, big["w_glu"], b_glu, big["w_ssm_proj"], big["w_attn_proj"], big["w_out"])
    x1 = _mixer_out_fwd(x0, u, yf, yb, yt_attn, gates, *mixer_w)

    dx1, loss8, dg_fin, dg_mlp, h2b, dab, hsqb, dx2b = _mlp_loss_fwd_bwd(
        x1, tgt, g_mlp, g_fin, big["w_mlp_in"], big["w_mlp_out"], n_valid)

    grads = {}
    (dyb, dud, dyt_attn, dgates, d_ssm_d, d_b_glu, grads["w_glu"], grads["w_ssm_proj"], grads["w_attn_proj"],
     grads["w_out"]) = _mixer_out_bwd(dx1, u, yf, yb, yt_attn, gates, *mixer_w)
    grads["w_mlp_in"] = _matmul_tn(h2b, dab, "grad_w_mlp_in")
    grads["w_mlp_out"] = _matmul_tn(hsqb, dx2b, "grad_w_mlp_out")
    dk, dv, dqt = _attn_bwd(qat, ka, kt, va, dyt_attn, yt_attn, lse)
    scan_args = (_interleave(dyb), ub, xs_re, xs_im, *scan_w, cct_re, cct_im, bbt_re, bbt_im)
    if comm is None:
        late_grad_parts = None
        du2, dlam_re, dlam_im, dbr, dbi, dcr, dci = _ssm_scan_bwd(*scan_args)
    else:
        du2, dlam_re, dlam_im, dbr, dbi, dcr, dci, late_grad_parts = _ssm_scan_bwd(
            *scan_args, exchange=(comm["pack_grads"](LATE_WEIGHTS, grads), True))
    du2 = _interleave(du2, inverse=True)
    dx0, dproj, dg_mix, dqg, dkg = _in_proj_bwd(x0, dx1, dud, du2[0], du2[1], qraw, kraw, dqt, dk, dv, dgates,
                                                g_mix, big["w_in"], qg, kg, cos, sin)

    grads["meta_tokens"] = dx0[0:N_META]
    dbb_re = _diag_blocks(dbr).transpose(0, 1, 3, 2)
    dbb_im = _diag_blocks(dbi).transpose(0, 1, 3, 2)
    dc_re, dc_im = _diag_blocks(dcr), _diag_blocks(dci)
    shape_gn = (2, groups, SSM_STATE)
    d_a_re, d_a_im, d_log_dt, d_b_re, d_b_im = disc_vjp(
        (dlam_re.reshape(shape_gn), dlam_im.reshape(shape_gn), dbb_re, dbb_im))
    grads.update({
        "norm_mix_g": dg_mix, "ssm_a_re": d_a_re[None], "ssm_a_im": d_a_im[None], "ssm_log_dt": d_log_dt[None],
        "ssm_b_re": d_b_re[None], "ssm_b_im": d_b_im[None], "ssm_c_re": dc_re[None], "ssm_c_im": dc_im[None],
        "ssm_d": d_ssm_d, "b_glu": d_b_glu,
        "q_norm_g": dqg[:, 0:HEAD_DIM] + dqg[:, HEAD_DIM:128], "k_norm_g": dkg[:, 0:HEAD_DIM] + dkg[:, HEAD_DIM:128],
        "norm_mlp_g": dg_mlp, "norm_final_g": dg_fin.reshape(d),
    })
    if comm is None:
        small_grad_parts = None
        grads["w_in"] = _matmul_tn(h, dproj, "grad_w_in")
    else:
        grads["w_in"], small_grad_parts = _matmul_tn(h, dproj, "grad_w_in",
                                                     exchange=(comm["pack_small_grads"](grads), False))
    return loss8[0, 0], dx0[N_META:n_valid], grads, late_grad_parts, small_grad_parts


def kernel(x, meta_tokens, norm_mix_g, w_in, ssm_a_re, ssm_a_im, ssm_log_dt, ssm_b_re, ssm_b_im, ssm_c_re, ssm_c_im, ssm_d, w_glu, b_glu, q_norm_g, k_norm_g, w_ssm_proj, w_attn_proj, w_out, norm_mlp_g, w_mlp_in, w_mlp_out, norm_final_g, loss_target, m_meta_tokens, m_norm_mix_g, m_w_in, m_ssm_a_re, m_ssm_a_im, m_ssm_log_dt, m_ssm_b_re, m_ssm_b_im, m_ssm_c_re, m_ssm_c_im, m_ssm_d, m_w_glu, m_b_glu, m_q_norm_g, m_k_norm_g, m_w_ssm_proj, m_w_attn_proj, m_w_out, m_norm_mlp_g, m_w_mlp_in, m_w_mlp_out, m_norm_final_g, v_meta_tokens, v_norm_mix_g, v_w_in, v_ssm_a_re, v_ssm_a_im, v_ssm_log_dt, v_ssm_b_re, v_ssm_b_im, v_ssm_c_re, v_ssm_c_im, v_ssm_d, v_w_glu, v_b_glu, v_q_norm_g, v_k_norm_g, v_w_ssm_proj, v_w_attn_proj, v_w_out, v_norm_mlp_g, v_w_mlp_in, v_w_mlp_out, v_norm_final_g):
    w = dict(meta_tokens=meta_tokens, norm_mix_g=norm_mix_g, w_in=w_in, ssm_a_re=ssm_a_re, ssm_a_im=ssm_a_im, ssm_log_dt=ssm_log_dt, ssm_b_re=ssm_b_re, ssm_b_im=ssm_b_im, ssm_c_re=ssm_c_re, ssm_c_im=ssm_c_im, ssm_d=ssm_d, w_glu=w_glu, b_glu=b_glu, q_norm_g=q_norm_g, k_norm_g=k_norm_g, w_ssm_proj=w_ssm_proj, w_attn_proj=w_attn_proj, w_out=w_out, norm_mlp_g=norm_mlp_g, w_mlp_in=w_mlp_in, w_mlp_out=w_mlp_out, norm_final_g=norm_final_g)
    m = dict(meta_tokens=m_meta_tokens, norm_mix_g=m_norm_mix_g, w_in=m_w_in, ssm_a_re=m_ssm_a_re, ssm_a_im=m_ssm_a_im, ssm_log_dt=m_ssm_log_dt, ssm_b_re=m_ssm_b_re, ssm_b_im=m_ssm_b_im, ssm_c_re=m_ssm_c_re, ssm_c_im=m_ssm_c_im, ssm_d=m_ssm_d, w_glu=m_w_glu, b_glu=m_b_glu, q_norm_g=m_q_norm_g, k_norm_g=m_k_norm_g, w_ssm_proj=m_w_ssm_proj, w_attn_proj=m_w_attn_proj, w_out=m_w_out, norm_mlp_g=m_norm_mlp_g, w_mlp_in=m_w_mlp_in, w_mlp_out=m_w_mlp_out, norm_final_g=m_norm_final_g)
    v = dict(meta_tokens=v_meta_tokens, norm_mix_g=v_norm_mix_g, w_in=v_w_in, ssm_a_re=v_ssm_a_re, ssm_a_im=v_ssm_a_im, ssm_log_dt=v_ssm_log_dt, ssm_b_re=v_ssm_b_re, ssm_b_im=v_ssm_b_im, ssm_c_re=v_ssm_c_re, ssm_c_im=v_ssm_c_im, ssm_d=v_ssm_d, w_glu=v_w_glu, b_glu=v_b_glu, q_norm_g=v_q_norm_g, k_norm_g=v_k_norm_g, w_ssm_proj=v_w_ssm_proj, w_attn_proj=v_w_attn_proj, w_out=v_w_out, norm_mlp_g=v_norm_mlp_g, w_mlp_in=v_w_mlp_in, w_mlp_out=v_w_mlp_out, norm_final_g=v_norm_final_g)

    shard2d = {n: w[n].reshape(w[n].shape[-2:]) for n in BIG_WEIGHTS}

    meta_hi = shard2d["meta_tokens"].astype(BF16)
    meta_res = shard2d["meta_tokens"] - meta_hi.astype(F32)
    meta_mid = meta_res.astype(BF16)
    meta_lo = (meta_res - meta_mid.astype(F32)).astype(BF16)
    shapes_of = lambda names: [shard2d[n].shape for n in names]

    def full_weights(names, shards):
        return {n: s if n in BLOCK_WEIGHTS else _from_shards(s, BIG_SHARD_AXIS[n]) for n, s in zip(names, shards)}

    early = _gather_two_level(_pack_rows([meta_hi, meta_mid, meta_lo, shard2d["w_in"].astype(BF16)], 0),
                              "gather_early_weights")
    shards = _unpack_rows(early, [meta_hi.shape] * 3 + shapes_of(EARLY_WEIGHTS[1:]))
    meta = [_from_shards(s, 1).astype(F32) for s in shards[0:3]]
    big = {"meta_tokens": (meta[0] + meta[1]) + meta[2], **full_weights(EARLY_WEIGHTS[1:], shards[3:])}
    small = {n: w[n] for n in SMALL_WEIGHTS}
    pack_grads = lambda names, grads: _pack_rows(
        [_to_shards(grads[n], BIG_SHARD_AXIS[n]) for n in names], 1).astype(BF16)
    comm = {
        "pack_weights": lambda names: _pack_rows([shard2d[n].astype(BF16) for n in names], 0),
        "unpack_weights": lambda names, g: full_weights(names, _unpack_rows(g, shapes_of(names))),
        "pack_grads": pack_grads,
        "pack_small_grads": lambda grads: _pack_flat([grads[n] for n in SMALL_WEIGHTS]),
    }

    loss, grad_x, grads, late_parts, small_parts = _local_step(x[0], loss_target[0], big, small, comm)
    loss = lax.psum(loss, ("x", "y", "c"))

    pk = lambda names, src: _pack_rows([src[n].reshape(shard2d[n].shape) for n in names], 0)
    pe, pl_ = functools.partial(pk, EARLY_WEIGHTS), functools.partial(pk, LATE_WEIGHTS)
    *late_out, early_parts = _reduce_adamw(late_parts, pl_(w), pl_(m), pl_(v), "adamw_sharded_late",
                                           exchange=(pack_grads(EARLY_WEIGHTS, grads), True))
    early_out = _reduce_adamw(early_parts, pe(w), pe(m), pe(v), "adamw_sharded_early")
    small_shapes = [w[n].shape for n in SMALL_WEIGHTS]
    pf = lambda src: _pack_flat([src[n] for n in SMALL_WEIGHTS])
    small_out = _reduce_adamw(small_parts, pf(w), pf(m), pf(v), "adamw_replicated")

    results = []
    for kind in range(4):
        big_un = dict(zip(EARLY_WEIGHTS + LATE_WEIGHTS,
                          _unpack_rows(early_out[kind], shapes_of(EARLY_WEIGHTS))
                          + _unpack_rows(late_out[kind], shapes_of(LATE_WEIGHTS))))
        small_un = dict(zip(SMALL_WEIGHTS, _unpack(small_out[kind], small_shapes)))
        for n in ALL_WEIGHTS:
            results.append(big_un[n].reshape(w[n].shape) if n in big_un else small_un[n])
    return (loss, grad_x[None], *results)
```

```python
import functools
import math

import jax
import jax.numpy as jnp
from jax import lax
from jax.experimental import pallas as pl
from jax.experimental.pallas import tpu as pltpu

F32 = jnp.float32
BF16 = jnp.bfloat16

N_DEV = 8
N_META = 16
GRID_W = 64
SSM_GROUP = 16
SSM_STATE = 64
HEAD_DIM = 64
KV_REP = 4
ROPE_THETA = 10000.0
NORM_EPS = 1e-6
EIG_RE_MAX = -1e-4
ATTN_SCALE = HEAD_DIM ** -0.5

ADAM_LR = 0.001
ADAM_B1 = 0.9
ADAM_B2 = 0.999
ADAM_EPS = 1e-08
ADAM_WD = 0.01
ADAM_STEP = 10

ROW_TILE = 384
ROW_TILE_BWD = 384
QUERY_STRIP = 256
ONE_PASS_SLACK = 60.0
VT_ROWS = 80
MASK_BIAS = -1e30
SCAN_LANES = 512
DIAG_TILE = 256
KV_TILE = 768
PACK_W = 1024
V7X_VMEM_LIMIT = 56 * 1024 * 1024
NEG_BIG = -1e30

BIG_WEIGHTS = ("meta_tokens", "w_in", "w_glu", "w_ssm_proj", "w_attn_proj", "w_out", "w_mlp_in", "w_mlp_out")
BIG_SHARD_AXIS = {"meta_tokens": 1, "w_in": 1, "w_glu": 0, "w_ssm_proj": 1, "w_attn_proj": 0, "w_out": 0,
                  "w_mlp_in": 1, "w_mlp_out": 0}
BLOCK_WEIGHTS = ("w_in", "w_mlp_in", "w_mlp_out")
EARLY_WEIGHTS = ("meta_tokens", "w_in")
MIXER_WEIGHTS = ("w_glu", "w_ssm_proj", "w_attn_proj", "w_out")
MLP_WEIGHTS = ("w_mlp_in", "w_mlp_out")
LATE_WEIGHTS = MIXER_WEIGHTS + MLP_WEIGHTS
SMALL_WEIGHTS = ("norm_mix_g", "ssm_a_re", "ssm_a_im", "ssm_log_dt", "ssm_b_re", "ssm_b_im", "ssm_c_re",
                 "ssm_c_im", "ssm_d", "b_glu", "q_norm_g", "k_norm_g", "norm_mlp_g", "norm_final_g")
ALL_WEIGHTS = ("meta_tokens", "norm_mix_g", "w_in", "ssm_a_re", "ssm_a_im", "ssm_log_dt", "ssm_b_re", "ssm_b_im",
               "ssm_c_re", "ssm_c_im", "ssm_d", "w_glu", "b_glu", "q_norm_g", "k_norm_g", "w_ssm_proj",
               "w_attn_proj", "w_out", "norm_mlp_g", "w_mlp_in", "w_mlp_out", "norm_final_g")


def _round_up(n, m):
    return (n + m - 1) // m * m


def _pcall(body, *, name, grid, in_specs, out_specs, out_shape, scratch=(), vmem=None, **kw):
    params = pltpu.CompilerParams(dimension_semantics=("arbitrary",) * len(grid), vmem_limit_bytes=vmem)
    return pl.pallas_call(body, name=name, grid=grid, in_specs=in_specs, out_specs=out_specs, out_shape=out_shape,
                          scratch_shapes=list(scratch), compiler_params=params, **kw)


def _dot(a, b):
    return jnp.dot(a, b, preferred_element_type=F32)


def _dot_nt(a, b):
    return lax.dot_general(a, b, (((1,), (1,)), ((), ())), preferred_element_type=F32)


def _dot_tn(a, b):
    return lax.dot_general(a, b, (((0,), (0,)), ((), ())), preferred_element_type=F32)


def _full_spec(shape):
    nd = len(shape)
    return pl.BlockSpec(shape, lambda *_: (0,) * nd)


def _row_spec(tm, width):
    return pl.BlockSpec((tm, width), lambda i: (i, 0))


def _heads_spec(nh, tm):
    return pl.BlockSpec((nh, tm, HEAD_DIM), lambda i: (0, i, 0))


_ANY = pl.BlockSpec(memory_space=pl.ANY)


def _load_once(step, pairs, sem):
    @pl.when(step == 0)
    def _():
        copies = [pltpu.make_async_copy(src, dst, sem.at[k]) for k, (src, dst) in enumerate(pairs)]
        for cp in copies:
            cp.start()
        for cp in copies:
            cp.wait()


def _swap_pairs(x, even):
    n = x.shape[-1]
    return jnp.where(even, pltpu.roll(x, n - 1, 1), pltpu.roll(x, 1, 1))


def _head_mean(v):
    row = lax.broadcasted_iota(jnp.int32, (128, 128), 0) // HEAD_DIM
    col = lax.broadcasted_iota(jnp.int32, (128, 128), 1) // HEAD_DIM
    ones = jnp.where(row == col, 1.0, 0.0).astype(BF16)
    hi = v.astype(BF16)
    lo = (v - hi.astype(F32)).astype(BF16)
    return (_dot(hi, ones) + _dot(lo, ones)) * (1.0 / HEAD_DIM)


def _gelu(y):
    return 0.5 * y * (1.0 + lax.erf(y * (1.0 / math.sqrt(2.0))))


def _gelu_grad(y):
    return 0.5 * (1.0 + lax.erf(y * (1.0 / math.sqrt(2.0)))) + y * jnp.exp(-0.5 * y * y) * (1.0 / math.sqrt(2.0 * math.pi))


def _in_proj_fwd(x0, g_mix, w_in, qg, kg, cos, sin, n_valid, exchange=None):
    t, d = x0.shape
    tm = ROW_TILE
    du, dk = d // 2, d // 4
    nh, nkv = d // HEAD_DIM, d // HEAD_DIM // KV_REP
    bw = w_in.shape[-1]
    assert bw == du and dk * 2 == bw

    def body(*refs):
        i = pl.program_id(0)
        (x_ref, g_ref, w_hbm, qg_ref, kg_ref, c_ref, s_ref,
         h_ref, u_ref, ub_ref, qraw_ref, kraw_ref, qat_ref, ka_ref, kt_ref, va_ref, vta_ref, gates_ref,
         w_ref, sem) = _riding_exchange(refs, exchange, 7, 11, i == 0, i == t // tm - 1)
        _load_once(i, [(w_hbm, w_ref)], sem)
        x = x_ref[...]
        r = lax.rsqrt(jnp.mean(x * x, axis=-1, keepdims=True) + NORM_EPS)
        h = ((x * r) * g_ref[...]).astype(BF16)
        h_ref[...] = h
        u = _dot(h, w_ref[0])
        u_ref[...] = u
        ub_ref[...] = u.astype(BF16)
        lane = lax.broadcasted_iota(jnp.int32, (tm, 128), 1)
        lo = lane < HEAD_DIM
        even = (lane & 1) == 0
        aug = lane == HEAD_DIM
        c = c_ref[...]
        s = s_ref[...]
        row = i * tm + lax.broadcasted_iota(jnp.int32, (tm, 1), 0)
        one = jnp.where(aug, 1.0, 0.0)
        key_bias = jnp.where(jnp.logical_and(aug, row >= n_valid), MASK_BIAS, 0.0)

        def norm_rope(blk, g128):
            rr = lax.rsqrt(_head_mean(blk * blk) + NORM_EPS)
            qn = (blk * rr) * g128
            return qn * c + _swap_pairs(qn, even) * s

        def put_heads(rows_ref, cols_ref, first, pair, extra):
            for k, head in enumerate((pair, pltpu.roll(pair, HEAD_DIM, 1))):
                wide = jnp.where(lo, head, extra)
                if rows_ref is not None:
                    rows_ref[first + k] = wide.astype(BF16)
                cols_ref[first + k] = wide.T[0:cols_ref.shape[1], :].astype(BF16)

        for blk in range(2):
            qb = _dot(h, w_ref[1 + blk])
            qraw_ref[:, bw * blk:bw * (blk + 1)] = qb
            for a in range(bw // 128):
                put_heads(None, qat_ref, (bw // HEAD_DIM) * blk + 2 * a,
                          norm_rope(qb[:, 128 * a:128 * (a + 1)], qg_ref[...]) * ATTN_SCALE, one)
        kv = _dot(h, w_ref[3])
        kraw_ref[...] = kv[:, 0:dk]
        for a in range(nkv // 2):
            put_heads(ka_ref, kt_ref, 2 * a, norm_rope(kv[:, 128 * a:128 * (a + 1)], kg_ref[...]), key_bias)
            put_heads(va_ref, vta_ref, 2 * a, kv[:, dk + 128 * a:dk + 128 * (a + 1)], one)
        for blk in range(4):
            gates_ref[:, bw * blk:bw * (blk + 1)] = _dot(h, w_ref[4 + blk])

    heads = lambda n: pl.BlockSpec((n, tm, 128), lambda i: (0, i, 0))
    heads_t = lambda n, rows: pl.BlockSpec((n, rows, tm), lambda i: (0, 0, i))
    extra = exchange is not None
    return _pcall(
        body, name="in_proj_fwd", grid=(t // tm,),
        in_specs=[_row_spec(tm, d), _full_spec((1, d)), _ANY, _full_spec((1, 128)), _full_spec((1, 128)),
                  _row_spec(tm, 128), _row_spec(tm, 128)] + [_ANY] * extra,
        out_specs=[_row_spec(tm, d), _row_spec(tm, du), _row_spec(tm, du), _row_spec(tm, d), _row_spec(tm, dk),
                   heads_t(nh, 128), heads(nkv), heads_t(nkv, HEAD_DIM), heads(nkv), heads_t(nkv, VT_ROWS),
                   _row_spec(tm, 2 * d)] + [_ANY] * extra,
        out_shape=[jax.ShapeDtypeStruct((t, d), BF16), jax.ShapeDtypeStruct((t, du), F32),
                   jax.ShapeDtypeStruct((t, du), BF16), jax.ShapeDtypeStruct((t, d), F32),
                   jax.ShapeDtypeStruct((t, dk), F32), jax.ShapeDtypeStruct((nh, 128, t), BF16),
                   jax.ShapeDtypeStruct((nkv, t, 128), BF16), jax.ShapeDtypeStruct((nkv, HEAD_DIM, t), BF16),
                   jax.ShapeDtypeStruct((nkv, t, 128), BF16), jax.ShapeDtypeStruct((nkv, VT_ROWS, t), BF16),
                   jax.ShapeDtypeStruct((t, 2 * d), F32)] + ([_exchange_out_shape(*exchange)] if extra else []),
        scratch=[pltpu.VMEM((N_DEV, d, bw), BF16), pltpu.SemaphoreType.DMA((1,))] + _EXCHANGE_SEMS * extra,
        vmem=V7X_VMEM_LIMIT,
    )(x0, g_mix, w_in, qg, kg, cos, sin, *([exchange[0]] if extra else []))


def _mixer_values(u, yf, yb, yt_attn, gates, d_ref, wg_ref, bg_ref, ps_ref, pa_ref, d):
    y = (u * d_ref[...] + yf) + yb
    z = _gelu(y)
    sg = jax.nn.sigmoid(_dot(z.astype(BF16), wg_ref[...]) + bg_ref[...])
    y_ssm = z * sg
    a_ssm = _dot(y_ssm.astype(BF16), ps_ref[...])
    a_attn = _dot_tn(yt_attn.astype(BF16), pa_ref[...])
    s_ssm = jax.nn.sigmoid(gates[:, 0:d])
    s_attn = jax.nn.sigmoid(gates[:, d:2 * d])
    merged = s_ssm * a_ssm + s_attn * a_attn
    return y, z, sg, y_ssm, a_ssm, a_attn, s_ssm, s_attn, merged


def _mixer_out_fwd(x0, u, yf, yb, y_attn, gates, ssm_d, w_glu, b_glu, p_ssm, p_attn, w_out):
    t, d = x0.shape
    tm = ROW_TILE
    du = d // 2

    def body(x_ref, u_ref, yf_ref, yb_ref, ya_ref, gt_ref, d_ref, wg_ref, bg_ref, ps_ref, pa_ref, wo_ref, x1_ref):
        vals = _mixer_values(u_ref[...], yf_ref[...], yb_ref[...], ya_ref[...], gt_ref[...],
                             d_ref, wg_ref, bg_ref, ps_ref, pa_ref, d)
        merged = vals[-1]
        x1_ref[...] = x_ref[...] + _dot(merged.astype(BF16), wo_ref[...])

    return _pcall(
        body, name="mixer_out_fwd", grid=(t // tm,),
        in_specs=[_row_spec(tm, d), _row_spec(tm, du), _row_spec(tm, du), _row_spec(tm, du),
                  pl.BlockSpec((d, tm), lambda i: (0, i)), _row_spec(tm, 2 * d), _full_spec((1, du)), _full_spec((du, du)), _full_spec((1, du)),
                  _full_spec((du, d)), _full_spec((d, d)), _full_spec((d, d))],
        out_specs=_row_spec(tm, d), out_shape=jax.ShapeDtypeStruct((t, d), F32), vmem=V7X_VMEM_LIMIT,
    )(x0, u, yf, yb, y_attn, gates, ssm_d, w_glu, b_glu, p_ssm, p_attn, w_out)


def _mlp_loss_fwd_bwd(x1, target, g_mlp, g_fin, w1, w2, n_valid):
    t, d = x1.shape
    tm = ROW_TILE_BWD
    dff = 4 * d
    nfc, _, fc = w1.shape

    def body(x_ref, tg_ref, gm_ref, gf_ref, w1_hbm, w2_hbm,
             dx1_ref, loss_ref, dgf_ref, dgm_ref, h2_ref, da_ref, hsq_ref, dx2b_ref,
             w1_ref, w2_ref, relu_ref, sem):
        i = pl.program_id(0)
        _load_once(i, [(w1_hbm, w1_ref), (w2_hbm, w2_ref)], sem)

        @pl.when(i == 0)
        def _():
            loss_ref[...] = jnp.zeros_like(loss_ref)
            dgf_ref[...] = jnp.zeros_like(dgf_ref)
            dgm_ref[...] = jnp.zeros_like(dgm_ref)

        x1v = x_ref[...]
        r1 = lax.rsqrt(jnp.mean(x1v * x1v, axis=-1, keepdims=True) + NORM_EPS)
        xh1 = x1v * r1
        h2b = (xh1 * gm_ref[...]).astype(BF16)
        h2_ref[...] = h2b
        acc = jnp.zeros((tm, d), F32)
        for c in range(nfc):
            a = jnp.maximum(_dot(h2b, w1_ref[c]), 0.0)
            relu_ref[:, fc * c:fc * (c + 1)] = a
            hs = (a * a).astype(BF16)
            hsq_ref[:, fc * c:fc * (c + 1)] = hs
            acc = acc + _dot(hs, w2_ref[c])
        x2 = x1v + acc
        r2 = lax.rsqrt(jnp.mean(x2 * x2, axis=-1, keepdims=True) + NORM_EPS)
        xh2 = x2 * r2
        out = xh2 * gf_ref[...]
        row = i * tm + lax.broadcasted_iota(jnp.int32, (tm, 1), 0)
        valid = jnp.logical_and(row >= N_META, row < n_valid)
        diff = jnp.where(valid, out - tg_ref[...], 0.0)
        loss_ref[...] += 0.5 * jnp.sum(jnp.sum(diff * diff, axis=-1, keepdims=True) * (1.0 / d))
        dout = diff * (1.0 / d)
        dgf_ref[...] += jnp.sum(dout * xh2, axis=0, keepdims=True)
        dxh2 = dout * gf_ref[...]
        dx2 = r2 * (dxh2 - xh2 * jnp.mean(dxh2 * xh2, axis=-1, keepdims=True))
        dx2b = dx2.astype(BF16)
        dx2b_ref[...] = dx2b
        dh2 = jnp.zeros((tm, d), F32)
        for c in range(nfc):
            dhs = _dot_nt(dx2b, w2_ref[c])
            da = (dhs * (2.0 * relu_ref[:, fc * c:fc * (c + 1)])).astype(BF16)
            da_ref[:, fc * c:fc * (c + 1)] = da
            dh2 = dh2 + _dot_nt(da, w1_ref[c])
        dgm_ref[...] += jnp.sum(dh2 * xh1, axis=0, keepdims=True)
        dxh1 = dh2 * gm_ref[...]
        dx1_ref[...] = dx2 + r1 * (dxh1 - xh1 * jnp.mean(dxh1 * xh1, axis=-1, keepdims=True))

    return _pcall(
        body, name="mlp_loss_fwd_bwd", grid=(t // tm,),
        in_specs=[_row_spec(tm, d), _row_spec(tm, d), _full_spec((1, d)), _full_spec((1, d)), _ANY, _ANY],
        out_specs=[_row_spec(tm, d), _full_spec((8, 128)), _full_spec((1, d)), _full_spec((1, d)),
                   _row_spec(tm, d), _row_spec(tm, dff), _row_spec(tm, dff), _row_spec(tm, d)],
        out_shape=[jax.ShapeDtypeStruct((t, d), F32), jax.ShapeDtypeStruct((8, 128), F32),
                   jax.ShapeDtypeStruct((1, d), F32), jax.ShapeDtypeStruct((1, d), F32),
                   jax.ShapeDtypeStruct((t, d), BF16), jax.ShapeDtypeStruct((t, dff), BF16),
                   jax.ShapeDtypeStruct((t, dff), BF16), jax.ShapeDtypeStruct((t, d), BF16)],
        scratch=[pltpu.VMEM((nfc, d, fc), BF16), pltpu.VMEM((nfc, fc, d), BF16), pltpu.VMEM((tm, dff), F32),
                 pltpu.SemaphoreType.DMA((2,))],
        vmem=V7X_VMEM_LIMIT,
    )(x1, target, g_mlp, g_fin, w1, w2)


def _mixer_out_bwd(dx1, u, yf, yb, yt_attn, gates, ssm_d, w_glu, b_glu, p_ssm, p_attn, w_out):
    t, d = dx1.shape
    tm = ROW_TILE_BWD
    du = d // 2
    last = t // tm - 1

    def body(dx_ref, u_ref, yf_ref, yb_ref, yt_ref, gt_ref, d_ref, wg_ref, bg_ref, ps_ref, pa_ref, wo_ref,
             dyb_ref, dud_ref, dyat_ref, dgates_ref, dd_ref, dbg_ref, dwg_hbm, dps_hbm, dpa_hbm, dwo_hbm,
             dwg_scr, dps_scr, dpa_scr, dwo_scr, sem):
        i = pl.program_id(0)
        accumulators = (dd_ref, dbg_ref, dwg_scr, dps_scr, dpa_scr, dwo_scr)

        @pl.when(i == 0)
        def _():
            for ref in accumulators:
                ref[...] = jnp.zeros_like(ref)

        uv = u_ref[...]
        y, z, sg, y_ssm, a_ssm, a_attn, s_ssm, s_attn, merged = _mixer_values(
            uv, yf_ref[...], yb_ref[...], yt_ref[...], gt_ref[...], d_ref, wg_ref, bg_ref, ps_ref, pa_ref, d)
        dxb = dx_ref[...].astype(BF16)
        dwo_scr[...] += _dot_tn(merged.astype(BF16), dxb)
        dmerged = _dot_nt(dxb, wo_ref[...])
        dgates_ref[:, 0:d] = (dmerged * a_ssm * (s_ssm * (1.0 - s_ssm))).astype(BF16)
        dgates_ref[:, d:2 * d] = (dmerged * a_attn * (s_attn * (1.0 - s_attn))).astype(BF16)
        da_ssm = (dmerged * s_ssm).astype(BF16)
        da_attn = (dmerged * s_attn).astype(BF16)
        dps_scr[...] += _dot_tn(y_ssm.astype(BF16), da_ssm)
        dpa_scr[...] += _dot(yt_ref[...].astype(BF16), da_attn)
        dy_ssm = _dot_nt(da_ssm, ps_ref[...])
        dyat_ref[...] = _dot_nt(pa_ref[...], da_attn).astype(BF16)
        dgl = dy_ssm * z * (sg * (1.0 - sg))
        dglb = dgl.astype(BF16)
        dwg_scr[...] += _dot_tn(z.astype(BF16), dglb)
        dbg_ref[...] += jnp.sum(dgl, axis=0, keepdims=True)
        dz = dy_ssm * sg + _dot_nt(dglb, wg_ref[...])
        dy = dz * _gelu_grad(y)
        dyb_ref[...] = dy.astype(BF16)
        dd_ref[...] += jnp.sum(dy * uv, axis=0, keepdims=True)
        dud_ref[...] = dy * d_ref[...]

        @pl.when(i == last)
        def _():
            pairs = ((dwg_scr, dwg_hbm), (dps_scr, dps_hbm), (dpa_scr, dpa_hbm), (dwo_scr, dwo_hbm))
            copies = [pltpu.make_async_copy(src, dst, sem.at[k]) for k, (src, dst) in enumerate(pairs)]
            for cp in copies:
                cp.start()
            for cp in copies:
                cp.wait()

    bf = lambda w: jax.ShapeDtypeStruct((t, w), BF16)
    f32 = lambda *shape: jax.ShapeDtypeStruct(shape, F32)
    return _pcall(
        body, name="mixer_out_bwd", grid=(t // tm,),
        in_specs=[_row_spec(tm, d), _row_spec(tm, du), _row_spec(tm, du), _row_spec(tm, du),
                  pl.BlockSpec((d, tm), lambda i: (0, i)),
                  _row_spec(tm, 2 * d), _full_spec((1, du)), _full_spec((du, du)), _full_spec((1, du)),
                  _full_spec((du, d)), _full_spec((d, d)), _full_spec((d, d))],
        out_specs=[_row_spec(tm, du), _row_spec(tm, du), pl.BlockSpec((d, tm), lambda i: (0, i)),
                   _row_spec(tm, 2 * d), _full_spec((1, du)), _full_spec((1, du)), _ANY, _ANY, _ANY, _ANY],
        out_shape=[bf(du), f32(t, du), jax.ShapeDtypeStruct((d, t), BF16), bf(2 * d), f32(1, du), f32(1, du),
                   f32(du, du), f32(du, d), f32(d, d), f32(d, d)],
        scratch=[pltpu.VMEM((du, du), F32), pltpu.VMEM((du, d), F32), pltpu.VMEM((d, d), F32),
                 pltpu.VMEM((d, d), F32), pltpu.SemaphoreType.DMA((4,))],
        vmem=V7X_VMEM_LIMIT,
    )(dx1, u, yf, yb, yt_attn, gates, ssm_d, w_glu, b_glu, p_ssm, p_attn, w_out)


def _in_proj_bwd(x0, dx1, dud, duf, dub, qraw, kraw, dq, dk, dv, dgates, g_mix, w_in, qg, kg, cos, sin):
    t, d = x0.shape
    tm = ROW_TILE_BWD
    du, dkw = d // 2, d // 4
    nh, nkv = d // HEAD_DIM, d // HEAD_DIM // KV_REP
    bw = w_in.shape[-1]
    o_q, o_k, o_v, o_g = du, du + d, du + d + dkw, 2 * d

    def body(x_ref, dx1_ref, dud_ref, duf_ref, dub_ref, qraw_ref, kraw_ref, dq_ref, dk_ref, dv_ref, dgt_ref,
             g_ref, w_hbm, qg_ref, kg_ref, c_ref, s_ref,
             dx0_ref, dproj_ref, dgm_ref, dqg_ref, dkg_ref,
             w_ref, kv_ref, sem):
        i = pl.program_id(0)
        _load_once(i, [(w_hbm, w_ref)], sem)

        @pl.when(i == 0)
        def _():
            dgm_ref[...] = jnp.zeros_like(dgm_ref)
            dqg_ref[...] = jnp.zeros_like(dqg_ref)
            dkg_ref[...] = jnp.zeros_like(dkg_ref)

        lane = lax.broadcasted_iota(jnp.int32, (tm, 128), 1)
        lo = lane < HEAD_DIM
        even = (lane & 1) == 0
        c = c_ref[...]
        s = s_ref[...]

        def norm_rope_bwd(dout, raw, g128):
            rr = lax.rsqrt(_head_mean(raw * raw) + NORM_EPS)
            xh = raw * rr
            dqn = dout * c + _swap_pairs(dout * s, even)
            dg = jnp.sum(dqn * xh, axis=0, keepdims=True)
            tt = dqn * g128
            return rr * (tt - xh * _head_mean(tt * xh)), dg

        dub_tot = (dud_ref[...] + duf_ref[...]) + dub_ref[...]
        dproj_ref[:, 0:du] = dub_tot.astype(BF16)
        dqg = jnp.zeros((1, 128), F32)
        for a in range(nh // 2):
            sl = slice(128 * a, 128 * (a + 1))
            draw, dg = norm_rope_bwd(dq_ref[sl, :].T * ATTN_SCALE, qraw_ref[:, sl], qg_ref[...])
            dqg = dqg + dg
            dproj_ref[:, o_q + 128 * a:o_q + 128 * (a + 1)] = draw.astype(BF16)
        dqg_ref[...] += dqg
        for hh in range(nkv):
            kv_ref[:, HEAD_DIM * hh:HEAD_DIM * (hh + 1)] = dk_ref[hh, :, 0:HEAD_DIM]
        dkg = jnp.zeros((1, 128), F32)
        for a in range(nkv // 2):
            sl = slice(128 * a, 128 * (a + 1))
            draw, dg = norm_rope_bwd(kv_ref[:, sl], kraw_ref[:, sl], kg_ref[...])
            dkg = dkg + dg
            dproj_ref[:, o_k + 128 * a:o_k + 128 * (a + 1)] = draw.astype(BF16)
        dkg_ref[...] += dkg
        for hh in range(nkv):
            kv_ref[:, HEAD_DIM * hh:HEAD_DIM * (hh + 1)] = dv_ref[hh]
        dproj_ref[:, o_v:o_g] = kv_ref[...].astype(BF16)
        dproj_ref[:, o_g:4 * d] = dgt_ref[...]
        dh = jnp.zeros((tm, d), F32)
        for blk in range(N_DEV):
            dh = dh + _dot_nt(dproj_ref[:, bw * blk:bw * (blk + 1)], w_ref[blk])
        x = x_ref[...]
        r = lax.rsqrt(jnp.mean(x * x, axis=-1, keepdims=True) + NORM_EPS)
        xh0 = x * r
        dgm_ref[...] += jnp.sum(dh * xh0, axis=0, keepdims=True)
        dxh = dh * g_ref[...]
        dx0_ref[...] = dx1_ref[...] + r * (dxh - xh0 * jnp.mean(dxh * xh0, axis=-1, keepdims=True))

    return _pcall(
        body, name="in_proj_bwd", grid=(t // tm,),
        in_specs=[_row_spec(tm, d), _row_spec(tm, d), _row_spec(tm, du), _row_spec(tm, du), _row_spec(tm, du),
                  _row_spec(tm, d), _row_spec(tm, dkw), pl.BlockSpec((d, tm), lambda i: (0, i)),
                  pl.BlockSpec((nkv, tm, 128), lambda i: (0, i, 0)), _heads_spec(nkv, tm),
                  _row_spec(tm, 2 * d), _full_spec((1, d)), _ANY, _full_spec((1, 128)), _full_spec((1, 128)),
                  _row_spec(tm, 128), _row_spec(tm, 128)],
        out_specs=[_row_spec(tm, d), _row_spec(tm, 4 * d), _full_spec((1, d)), _full_spec((1, 128)),
                   _full_spec((1, 128))],
        out_shape=[jax.ShapeDtypeStruct((t, d), F32), jax.ShapeDtypeStruct((t, 4 * d), BF16),
                   jax.ShapeDtypeStruct((1, d), F32), jax.ShapeDtypeStruct((1, 128), F32),
                   jax.ShapeDtypeStruct((1, 128), F32)],
        scratch=[pltpu.VMEM((N_DEV, d, bw), BF16), pltpu.VMEM((tm, dkw), F32), pltpu.SemaphoreType.DMA((1,))],
        vmem=V7X_VMEM_LIMIT,
    )(x0, dx1, dud, duf, dub, qraw, kraw, dq, dk, dv, dgates, g_mix, w_in, qg, kg, cos, sin)


def _attn_fwd(qat, ka, vta):
    nh, _, t = qat.shape
    nkv = ka.shape[0]
    rep = nh // nkv
    hd = HEAD_DIM
    vr = vta.shape[1]
    tq = tk = KV_TILE

    def body(qt_ref, k_ref, vt_ref, ot_ref, otb_ref, lse_ref, m_scr, acc_scr, excess_scr):
        j = pl.program_id(2)
        src = j % 2
        dst = 1 - src

        @pl.when(j == 0)
        def _():
            m_scr[0] = jnp.full(m_scr.shape[1:], NEG_BIG, F32)
            acc_scr[0] = jnp.zeros(acc_scr.shape[1:], F32)
            excess_scr[...] = jnp.full(excess_scr.shape, -NEG_BIG, F32)

        k = k_ref[0]
        vt = vt_ref[0]
        strips = [(r, c) for r in range(rep) for c in range(0, tq, QUERY_STRIP)]
        scores = lambda r, c: _dot(k, qt_ref[r, :, c:c + QUERY_STRIP])

        def sweep(one_pass):
            def add_values(r, cols, before, after, pt):
                acc = acc_scr[src, r, :, cols]
                acc_scr[dst, r, :, cols] = after * ((acc if before is None else before * acc) + _dot(vt, pt))

            ahead = [scores(*strips[0]), scores(*strips[1])]
            pending = None
            excess = jnp.full((1, QUERY_STRIP), NEG_BIG, F32)
            for n, (r, c) in enumerate(strips):
                st = ahead.pop(0)
                if n + 2 < len(strips):
                    ahead.append(scores(*strips[n + 2]))
                cols = slice(c, c + QUERY_STRIP)
                m_prev = m_scr[src, r, :, cols]
                if one_pass:
                    pt = jnp.exp(st - m_prev).astype(BF16)
                    tile_max = jnp.max(st, axis=0, keepdims=True)
                    m_next = jnp.maximum(m_prev, tile_max)
                    excess = jnp.maximum(excess, tile_max - m_prev)
                    factors = (None, jnp.exp(m_prev - m_next))
                else:
                    m_next = jnp.maximum(m_prev, jnp.max(st, axis=0, keepdims=True))
                    pt = jnp.exp(st - m_next).astype(BF16)
                    factors = (jnp.exp(m_prev - m_next), 1.0)
                m_scr[dst, r, :, cols] = m_next
                if pending is not None:
                    add_values(*pending)
                pending = (r, cols, *factors, pt)
            add_values(*pending)
            return excess

        @pl.when(j > 0)
        def _():
            excess_scr[...] = sweep(one_pass=True)

        @pl.when(jnp.max(excess_scr[...]) > ONE_PASS_SLACK)
        def _():
            sweep(one_pass=False)

        @pl.when(j == pl.num_programs(2) - 1)
        def _():
            for r in range(rep):
                l = acc_scr[dst, r, hd:hd + 1, :]
                out = acc_scr[dst, r, 0:hd, :] / l
                ot_ref[hd * r:hd * (r + 1), :] = out
                otb_ref[hd * r:hd * (r + 1), :] = out.astype(BF16)
                lse_ref[0, r:r + 1, :] = m_scr[dst, r] + jnp.log(l)

    return _pcall(
        body, name="attn_fwd", grid=(nkv, t // tq, t // tk),
        in_specs=[pl.BlockSpec((rep, 128, tq), lambda g, i, j: (g, 0, i)),
                  pl.BlockSpec((1, tk, 128), lambda g, i, j: (g, j, 0)),
                  pl.BlockSpec((1, vr, tk), lambda g, i, j: (g, 0, j))],
        out_specs=[pl.BlockSpec((rep * hd, tq), lambda g, i, j: (g, i)),
                   pl.BlockSpec((rep * hd, tq), lambda g, i, j: (g, i)),
                   pl.BlockSpec((1, rep, tq), lambda g, i, j: (g, 0, i))],
        out_shape=[jax.ShapeDtypeStruct((nh * hd, t), F32), jax.ShapeDtypeStruct((nh * hd, t), BF16),
                   jax.ShapeDtypeStruct((nkv, rep, t), F32)],
        scratch=[pltpu.VMEM((2, rep, 1, tq), F32), pltpu.VMEM((2, rep, vr, tq), F32),
                 pltpu.VMEM((1, QUERY_STRIP), F32)],
        vmem=V7X_VMEM_LIMIT,
    )(qat, ka, vta)


def _attn_bwd(qat, ka, kt, va, dot, ot, lse_row):
    nh, _, t = qat.shape
    nkv = ka.shape[0]
    rep = nh // nkv
    hd = HEAD_DIM
    tq = tk = KV_TILE

    def body(qt_ref, k_ref, kt_ref, v_ref, dot_ref, ot_ref, lse_ref, dk_ref, dv_ref, dqt_ref):
        j = pl.program_id(1)
        i = pl.program_id(2)

        @pl.when(jnp.logical_and(j == 0, i == 0))
        def _():
            dqt_ref[...] = jnp.zeros_like(dqt_ref)

        @pl.when(i == 0)
        def _():
            dk_ref[...] = jnp.zeros_like(dk_ref)
            dv_ref[...] = jnp.zeros_like(dv_ref)

        k = k_ref[0]
        kt = kt_ref[0]
        v = v_ref[0, :, 0:hd]
        cols = pl.ds(pl.multiple_of(i * tq, tq), tq)
        dk = jnp.zeros((tk, 128), F32)
        dv = jnp.zeros((tk, hd), F32)
        products = lambda r: (_dot(k, qt_ref[r]), _dot(v, dot_ref[hd * r:hd * (r + 1), :]))
        nxt = products(0)
        for r in range(rep):
            st, dpt = nxt
            if r + 1 < rep:
                nxt = products(r + 1)
            heads = slice(hd * r, hd * (r + 1))
            qt = qt_ref[r]
            dot_r = dot_ref[heads, :]
            delta = jnp.sum(dot_r.astype(F32) * ot_ref[heads, :], axis=0, keepdims=True)
            pt = jnp.exp(st - lse_ref[0, r:r + 1, :])
            dst = (pt * (dpt - delta)).astype(BF16)
            dv = dv + _dot_nt(pt.astype(BF16), dot_r)
            dk = dk + _dot_nt(dst, qt)
            dqt_ref[heads, cols] += _dot(kt, dst)
        dk_ref[0] += dk
        dv_ref[0] += dv

    return _pcall(
        body, name="attn_bwd", grid=(nkv, t // tk, t // tq),
        in_specs=[pl.BlockSpec((rep, 128, tq), lambda g, j, i: (g, 0, i)),
                  pl.BlockSpec((1, tk, 128), lambda g, j, i: (g, j, 0)),
                  pl.BlockSpec((1, hd, tk), lambda g, j, i: (g, 0, j)),
                  pl.BlockSpec((1, tk, 128), lambda g, j, i: (g, j, 0)),
                  pl.BlockSpec((rep * hd, tq), lambda g, j, i: (g, i)),
                  pl.BlockSpec((rep * hd, tq), lambda g, j, i: (g, i)),
                  pl.BlockSpec((1, rep, tq), lambda g, j, i: (g, 0, i))],
        out_specs=[pl.BlockSpec((1, tk, 128), lambda g, j, i: (g, j, 0)),
                   pl.BlockSpec((1, tk, hd), lambda g, j, i: (g, j, 0)),
                   pl.BlockSpec((rep * hd, t), lambda g, j, i: (g, 0))],
        out_shape=[jax.ShapeDtypeStruct((nkv, t, 128), F32), jax.ShapeDtypeStruct((nkv, t, hd), F32),
                   jax.ShapeDtypeStruct((nh * hd, t), F32)],
        vmem=V7X_VMEM_LIMIT,
    )(qat, ka, kt, va, dot, ot, lse_row)


def _riding_exchange(refs, exchange, n_in, n_out, first_step, last_step):
    if exchange is None:
        return refs
    x_ref, out_ref = refs[n_in], refs[n_in + 1 + n_out]
    sems = refs[-3:]

    @pl.when(first_step)
    def _():
        _start_all(*_exchange_copies(x_ref, out_ref, *sems, exchange[1]))

    @pl.when(last_step)
    def _():
        _wait_all(*_exchange_copies(x_ref, out_ref, *sems, exchange[1]))

    return refs[:n_in] + refs[n_in + 1:n_in + 1 + n_out] + refs[n_in + 2 + n_out:-3]


def _segmented_scan(src_re, src_im, dst_re, dst_im, lam_re, lam_im, pow_re, pow_im, carry_re, carry_im,
                    end_re, end_im, in_re, in_im, lanes, descending, conj):
    tc = src_re.shape[0]
    seg = tc // 8
    width = lanes.size
    sign = -1.0 if conj else 1.0
    rows_of = lambda q: pl.ds(8 * (seg - 1 - q if descending else q), 8)
    lr = jnp.broadcast_to(lam_re[:, lanes], (8, width))
    li = jnp.broadcast_to(sign * lam_im[:, lanes], (8, width))
    xr = jnp.zeros((8, width), F32)
    xi = jnp.zeros((8, width), F32)
    for q in range(seg):
        rows = rows_of(q)
        xr, xi = (lr * xr - li * xi) + src_re[rows, lanes], (lr * xi + li * xr) + src_im[rows, lanes]
        dst_re[rows, lanes] = xr
        dst_im[rows, lanes] = xi
    end_re[:, lanes] = xr
    end_im[:, lanes] = xi
    sr = pow_re[seg - 1:seg, lanes]
    si = sign * pow_im[seg - 1:seg, lanes]
    cr = carry_re[:, lanes]
    ci = carry_im[:, lanes]
    for s in range(8):
        se = 7 - s if descending else s
        in_re[se:se + 1, lanes] = cr
        in_im[se:se + 1, lanes] = ci
        cr, ci = (end_re[se:se + 1, lanes] + (sr * cr - si * ci)), (end_im[se:se + 1, lanes] + (sr * ci + si * cr))
    carry_re[:, lanes] = cr
    carry_im[:, lanes] = ci
    ir = in_re[:, lanes]
    ii = in_im[:, lanes]
    for q in range(seg):
        rows = rows_of(q)
        pr = pow_re[q:q + 1, lanes]
        pi = sign * pow_im[q:q + 1, lanes]
        dst_re[rows, lanes] = dst_re[rows, lanes] + (pr * ir - pi * ii)
        dst_im[rows, lanes] = dst_im[rows, lanes] + (pr * ii + pi * ir)


def _diag_tiles(gn):
    rows_per_tile = DIAG_TILE // (SSM_STATE // SSM_GROUP)
    return [(slice(rows_per_tile * j, rows_per_tile * (j + 1)), slice(DIAG_TILE * j, DIAG_TILE * (j + 1)))
            for j in range(gn // DIAG_TILE)]


def _ssm_scan_fwd(ub, lam_re, lam_im, pow_re, pow_im, bb_re, bb_im, cc_re, cc_im, exchange=None):
    t, w = ub.shape
    gn = lam_re.shape[-1]
    tc = ROW_TILE
    cl = min(gn, SCAN_LANES)
    nblk = t // tc
    tiles = _diag_tiles(gn)

    def body(*refs):
        first = jnp.logical_and(pl.program_id(0) == 0, pl.program_id(1) == 0)
        last = jnp.logical_and(pl.program_id(0) == 1, pl.program_id(1) == nblk - 1)
        (u_ref, lr_ref, li_ref, pr_ref, pi_ref, br_ref, bi_ref, cr_ref, ci_ref, y_ref, xr_ref, xi_ref,
         bur_scr, bui_scr, cr_scr, ci_scr, er_scr, ei_scr, nr_scr, ni_scr) = _riding_exchange(
             refs, exchange, 9, 3, first, last)

        @pl.when(pl.program_id(1) == 0)
        def _():
            cr_scr[...] = jnp.zeros_like(cr_scr)
            ci_scr[...] = jnp.zeros_like(ci_scr)

        for rows, lanes in tiles:
            u_j = u_ref[:, rows]
            bur_scr[:, lanes] = _dot(u_j, br_ref[0, rows, lanes])
            bui_scr[:, lanes] = _dot(u_j, bi_ref[0, rows, lanes])
        for descending in (False, True):
            @pl.when(pl.program_id(0) == int(descending))
            def _(descending=descending):
                for c0 in range(0, gn, cl):
                    _segmented_scan(bur_scr, bui_scr, xr_ref.at[0], xi_ref.at[0], lr_ref.at[0], li_ref.at[0],
                                    pr_ref.at[0], pi_ref.at[0], cr_scr, ci_scr, er_scr, ei_scr, nr_scr, ni_scr,
                                    pl.ds(c0, cl), descending, conj=False)
        for rows, lanes in tiles:
            y_ref[0, :, rows] = (_dot(xr_ref[0, :, lanes].astype(BF16), cr_ref[0, lanes, rows])
                                 - _dot(xi_ref[0, :, lanes].astype(BF16), ci_ref[0, lanes, rows]))

    blk = lambda dd, i: jnp.where(dd == 0, i, nblk - 1 - i)
    row = lambda width: pl.BlockSpec((1, tc, width), lambda dd, i: (dd, blk(dd, i), 0))
    per_dir = lambda a, b: pl.BlockSpec((1, a, b), lambda dd, i: (dd, 0, 0))
    extra = exchange is not None
    return _pcall(
        body, name="ssm_scan_fwd", grid=(2, nblk),
        in_specs=[pl.BlockSpec((tc, w), lambda dd, i: (blk(dd, i), 0)), per_dir(1, gn), per_dir(1, gn),
                  per_dir(tc // 8, gn), per_dir(tc // 8, gn),
                  per_dir(w, gn), per_dir(w, gn), per_dir(gn, w), per_dir(gn, w)] + [_ANY] * extra,
        out_specs=[row(w), row(gn), row(gn)] + [_ANY] * extra,
        out_shape=[jax.ShapeDtypeStruct((2, t, w), F32), jax.ShapeDtypeStruct((2, t, gn), F32),
                   jax.ShapeDtypeStruct((2, t, gn), F32)] + ([_exchange_out_shape(*exchange)] if extra else []),
        scratch=[pltpu.VMEM((tc, gn), F32), pltpu.VMEM((tc, gn), F32), pltpu.VMEM((1, gn), F32),
                 pltpu.VMEM((1, gn), F32)] + [pltpu.VMEM((8, gn), F32)] * 4 + _EXCHANGE_SEMS * extra,
        vmem=V7X_VMEM_LIMIT,
    )(ub, lam_re, lam_im, pow_re, pow_im, bb_re, bb_im, cc_re, cc_im, *([exchange[0]] if extra else []))


def _ssm_scan_bwd(dyb, ub, xs_re, xs_im, lam_re, lam_im, pow_re, pow_im, cct_re, cct_im, bbt_re, bbt_im,
                  exchange=None):
    t, w = dyb.shape
    gn = lam_re.shape[-1]
    tc = ROW_TILE
    cl = min(gn, SCAN_LANES)
    nblk = t // tc
    tiles = _diag_tiles(gn)

    def body(*refs):
        i = pl.program_id(1)
        first = jnp.logical_and(pl.program_id(0) == 0, i == 0)
        last = jnp.logical_and(pl.program_id(0) == 1, i == nblk - 1)
        (dy_ref, u_ref, xr_ref, xi_ref, hr_ref, hi_ref, lr_ref, li_ref, pr_ref, pi_ref, ctr_ref, cti_ref, btr_ref,
         bti_ref, du_ref, dlr_ref, dli_ref, dbr_ref, dbi_ref, dcr_ref, dci_ref,
         gxr_scr, gxi_scr, cr_scr, ci_scr, ar_scr, ai_scr, er_scr, ei_scr, nr_scr, ni_scr) = _riding_exchange(
             refs, exchange, 14, 7, first, last)

        @pl.when(i == 0)
        def _():
            for ref in (cr_scr, ci_scr, ar_scr, ai_scr, dbr_ref, dbi_ref, dcr_ref, dci_ref):
                ref[...] = jnp.zeros_like(ref)

        for rows, lanes in tiles:
            dy_j = dy_ref[:, rows]
            gxr_scr[:, lanes] = _dot(dy_j, ctr_ref[0, rows, lanes])
            gxi_scr[:, lanes] = -_dot(dy_j, cti_ref[0, rows, lanes])
        first_block = i == nblk - 1
        sublane = lax.broadcasted_iota(jnp.int32, (8, 1), 0)

        def lam_gradient(state_descending):
            for c0 in range(0, gn, 512):
                lanes = pl.ds(c0, 512)
                if state_descending:
                    cur, prev, edge, src = pl.ds(0, tc - 8), pl.ds(8, tc - 8), pl.ds(tc - 8, 8), pl.ds(0, 8)
                    halo_at, halo_row, shift = 7, 0, 7
                else:
                    cur, prev, edge, src = pl.ds(8, tc - 8), pl.ds(0, tc - 8), pl.ds(0, 8), pl.ds(tc - 8, 8)
                    halo_at, halo_row, shift = 0, 7, 1
                halo_r = jnp.where(first_block, 0.0, hr_ref[0, halo_row:halo_row + 1, lanes])
                halo_i = jnp.where(first_block, 0.0, hi_ref[0, halo_row:halo_row + 1, lanes])
                xer = jnp.where(sublane == halo_at, halo_r, pltpu.roll(xr_ref[0, src, lanes], shift, 0))
                xei = jnp.where(sublane == halo_at, halo_i, pltpu.roll(xi_ref[0, src, lanes], shift, 0))
                gr, gi = gxr_scr[cur, lanes], gxi_scr[cur, lanes]
                xpr, xpi = xr_ref[0, prev, lanes], xi_ref[0, prev, lanes]
                ger, gei = gxr_scr[edge, lanes], gxi_scr[edge, lanes]
                ar_scr[:, lanes] += (jnp.sum(gr * xpr + gi * xpi, axis=0, keepdims=True)
                                     + jnp.sum(ger * xer + gei * xei, axis=0, keepdims=True))
                ai_scr[:, lanes] += (jnp.sum(gi * xpr - gr * xpi, axis=0, keepdims=True)
                                     + jnp.sum(gei * xer - ger * xei, axis=0, keepdims=True))

        for descending in (True, False):
            @pl.when(pl.program_id(0) == int(not descending))
            def _(descending=descending):
                for c0 in range(0, gn, cl):
                    _segmented_scan(gxr_scr, gxi_scr, gxr_scr, gxi_scr, lr_ref.at[0], li_ref.at[0], pr_ref.at[0],
                                    pi_ref.at[0], cr_scr, ci_scr, er_scr, ei_scr, nr_scr, ni_scr, pl.ds(c0, cl),
                                    descending, conj=True)
                lam_gradient(state_descending=not descending)
        dlr_ref[0] = ar_scr[...]
        dli_ref[0] = ai_scr[...]
        for rows, lanes in tiles:
            grb = gxr_scr[:, lanes].astype(BF16)
            gib = gxi_scr[:, lanes].astype(BF16)
            du_ref[0, :, rows] = _dot(grb, btr_ref[0, lanes, rows]) + _dot(gib, bti_ref[0, lanes, rows])
            u_j = u_ref[:, rows]
            dy_j = dy_ref[:, rows]
            dbr_ref[0, rows, :] += _dot_tn(u_j, grb)
            dbi_ref[0, rows, :] += _dot_tn(u_j, gib)
            dcr_ref[0, rows, :] += _dot_tn(dy_j, xr_ref[0, :, lanes].astype(BF16))
            dci_ref[0, rows, :] -= _dot_tn(dy_j, xi_ref[0, :, lanes].astype(BF16))

    blk = lambda dd, i: jnp.where(dd == 0, nblk - 1 - i, i)
    rev = lambda width: pl.BlockSpec((1, tc, width), lambda dd, i: (dd, blk(dd, i), 0))
    halo_blk = lambda dd, i: jnp.where(dd == 0, jnp.maximum(blk(dd, i) * (tc // 8) - 1, 0),
                                       jnp.minimum((blk(dd, i) + 1) * (tc // 8), t // 8 - 1))
    halo = pl.BlockSpec((1, 8, gn), lambda dd, i: (dd, halo_blk(dd, i), 0))
    per_dir = lambda a, b: pl.BlockSpec((1, a, b), lambda dd, i: (dd, 0, 0))
    extra = exchange is not None
    return _pcall(
        body, name="ssm_scan_bwd", grid=(2, nblk),
        in_specs=[pl.BlockSpec((tc, w), lambda dd, i: (blk(dd, i), 0)),
                  pl.BlockSpec((tc, w), lambda dd, i: (blk(dd, i), 0)), rev(gn), rev(gn), halo, halo,
                  per_dir(1, gn), per_dir(1, gn), per_dir(tc // 8, gn), per_dir(tc // 8, gn),
                  per_dir(w, gn), per_dir(w, gn), per_dir(gn, w), per_dir(gn, w)]
        + [_ANY] * extra,
        out_specs=[rev(w), per_dir(1, gn), per_dir(1, gn)] + [per_dir(w, DIAG_TILE)] * 4 + [_ANY] * extra,
        out_shape=[jax.ShapeDtypeStruct((2, t, w), F32), jax.ShapeDtypeStruct((2, 1, gn), F32),
                   jax.ShapeDtypeStruct((2, 1, gn), F32)] + [jax.ShapeDtypeStruct((2, w, DIAG_TILE), F32)] * 4
        + ([_exchange_out_shape(*exchange)] if extra else []),
        scratch=[pltpu.VMEM((tc, gn), F32), pltpu.VMEM((tc, gn), F32)] + [pltpu.VMEM((1, gn), F32)] * 4
        + [pltpu.VMEM((8, gn), F32)] * 4 + _EXCHANGE_SEMS * extra,
        vmem=V7X_VMEM_LIMIT,
    )(dyb, ub, xs_re, xs_im, xs_re, xs_im, lam_re, lam_im, pow_re, pow_im, cct_re, cct_im, bbt_re, bbt_im,
      *([exchange[0]] if extra else []))


def _matmul_tn(a, b, name, a_is_transposed=False, exchange=None):
    t, n = b.shape
    m = a.shape[0] if a_is_transposed else a.shape[1]
    bm, bn, tk = min(m, 1024), min(n, 1024), KV_TILE
    grid = (m // bm, n // bn, t // tk)

    def body(*refs):
        at = lambda step: functools.reduce(jnp.logical_and, [pl.program_id(ax) == step[ax] for ax in range(3)])
        a_ref, b_ref, o_ref = _riding_exchange(refs, exchange, 2, 1, at((0, 0, 0)), at([g - 1 for g in grid]))

        @pl.when(pl.program_id(2) == 0)
        def _():
            o_ref[...] = jnp.zeros_like(o_ref)

        mul = _dot if a_is_transposed else _dot_tn
        o_ref[...] += mul(a_ref[...].astype(BF16), b_ref[...].astype(BF16))

    a_spec = (pl.BlockSpec((bm, tk), lambda i, j, k: (i, k)) if a_is_transposed else
              pl.BlockSpec((tk, bm), lambda i, j, k: (k, i)))
    extra = exchange is not None
    out = _pcall(
        body, name=name, grid=grid,
        in_specs=[a_spec, pl.BlockSpec((tk, bn), lambda i, j, k: (k, j))] + [_ANY] * extra,
        out_specs=[pl.BlockSpec((bm, bn), lambda i, j, k: (i, j))] + [_ANY] * extra,
        out_shape=[jax.ShapeDtypeStruct((m, n), F32)] + ([_exchange_out_shape(*exchange)] if extra else []),
        scratch=_EXCHANGE_SEMS * extra, vmem=V7X_VMEM_LIMIT,
    )(a, b, *([exchange[0]] if extra else []))
    return out if extra else out[0]


def _reduce_adamw(gparts, p, m, v, name, exchange=None):
    rows, width = p.shape
    tr = max(k for k in range(16, 513, 16) if rows % k == 0)

    def body(*refs):
        i = pl.program_id(0)
        g_ref, p_ref, m_ref, v_ref, go_ref, d_ref, mo_ref, vo_ref = _riding_exchange(
            refs, exchange, 4, 4, i == 0, i == rows // tr - 1)
        g = g_ref[0].astype(F32)
        for k in range(1, N_DEV):
            g = g + g_ref[k].astype(F32)
        go_ref[...] = g
        mm = ADAM_B1 * m_ref[...] + (1.0 - ADAM_B1) * g
        vv = ADAM_B2 * v_ref[...] + (1.0 - ADAM_B2) * (g * g)
        m_hat = mm / (1.0 - ADAM_B1 ** ADAM_STEP)
        v_hat = vv / (1.0 - ADAM_B2 ** ADAM_STEP)
        d_ref[...] = -ADAM_LR * (m_hat / (jnp.sqrt(v_hat) + ADAM_EPS) + ADAM_WD * p_ref[...])
        mo_ref[...] = mm
        vo_ref[...] = vv

    spec = pl.BlockSpec((tr, width), lambda i: (i, 0))
    out = jax.ShapeDtypeStruct((rows, width), F32)
    extra = exchange is not None
    return _pcall(
        body, name=name, grid=(rows // tr,),
        in_specs=[pl.BlockSpec((N_DEV, tr, width), lambda i: (0, i, 0)), spec, spec, spec] + [_ANY] * extra,
        out_specs=[spec, spec, spec, spec] + [_ANY] * extra,
        out_shape=[out, out, out, out] + ([_exchange_out_shape(*exchange)] if extra else []),
        scratch=_EXCHANGE_SEMS * extra, vmem=V7X_VMEM_LIMIT,
    )(gparts, p, m, v, *([exchange[0]] if extra else []))


def _peer(k):
    x, y, c = lax.axis_index("x"), lax.axis_index("y"), lax.axis_index("c")
    return (x ^ ((k >> 2) & 1), y ^ ((k >> 1) & 1), c ^ (k & 1))


def _my_index():
    return 4 * lax.axis_index("x") + 2 * lax.axis_index("y") + lax.axis_index("c")


def _exchange_copies(x_ref, out_ref, send_sems, recv_sems, local_sem, scatter, first_sem=0):
    me = _my_index()
    local = pltpu.make_async_copy(x_ref.at[me] if scatter else x_ref, out_ref.at[me], local_sem)
    copies = []
    for k in range(1, N_DEV):
        peer = _peer(k)
        src = x_ref.at[4 * peer[0] + 2 * peer[1] + peer[2]] if scatter else x_ref
        copies.append(pltpu.make_async_remote_copy(
            src_ref=src, dst_ref=out_ref.at[me], send_sem=send_sems.at[first_sem + k - 1],
            recv_sem=recv_sems.at[first_sem + k - 1], device_id=peer, device_id_type=pl.DeviceIdType.MESH))
    return local, copies


def _start_all(local, copies):
    local.start()
    for cp in copies:
        cp.start()


def _wait_all(local, copies):
    for cp in copies:
        cp.wait_recv()
    for cp in copies:
        cp.wait_send()
    local.wait()


def _exchange_out_shape(x, scatter):
    return jax.ShapeDtypeStruct((N_DEV,) + tuple(x.shape[1:] if scatter else x.shape), x.dtype)


_EXCHANGE_SEMS = [pltpu.SemaphoreType.DMA((N_DEV - 1,)), pltpu.SemaphoreType.DMA((N_DEV - 1,)),
                  pltpu.SemaphoreType.DMA(())]


def _gather_two_level(x, name):
    def body(x_ref, out_ref, send_sems, recv_sems, local_sem):
        x, y, c = lax.axis_index("x"), lax.axis_index("y"), lax.axis_index("c")
        me, sibling = (x, y, c), (x, y, 1 - c)
        chips = [(1 - x, y), (x, 1 - y), (1 - x, 1 - y)]

        def copy(k, block, to, src=None):
            slot = out_ref.at[4 * block[0] + 2 * block[1] + block[2]]
            return pltpu.make_async_remote_copy(src_ref=slot if src is None else src, dst_ref=slot,
                                                send_sem=send_sems.at[k], recv_sem=recv_sems.at[k], device_id=to,
                                                device_id_type=pl.DeviceIdType.MESH)

        local = pltpu.make_async_copy(x_ref, out_ref.at[_my_index()], local_sem)
        local.start()
        first = [copy(0, me, sibling, src=x_ref)]
        first += [copy(1 + j, me, (*chip, c), src=x_ref) for j, chip in enumerate(chips)]
        for cp in first:
            cp.start()
        passed = [copy(4 + j, (*chip, c), sibling) for j, chip in enumerate(chips)]
        for j, chip in enumerate(chips):
            copy(1 + j, (*chip, c), me).wait_recv()
            passed[j].start()
        copy(0, sibling, me).wait_recv()
        for j, chip in enumerate(chips):
            copy(4 + j, (*chip, 1 - c), me).wait_recv()
        for cp in first + passed:
            cp.wait_send()
        local.wait()

    return pl.pallas_call(
        body, name=name, in_specs=[_ANY], out_specs=_ANY, out_shape=_exchange_out_shape(x, False),
        scratch_shapes=_EXCHANGE_SEMS,
    )(x)


def _to_shards(full, axis):
    r, c = full.shape
    if axis == 0:
        return full.reshape(N_DEV, r // N_DEV, c)
    return full.reshape(r, N_DEV, c // N_DEV).transpose(1, 0, 2)


def _from_shards(shards, axis):
    _, r, c = shards.shape
    if axis == 0:
        return shards.reshape(N_DEV * r, c)
    return shards.transpose(1, 0, 2).reshape(r, N_DEV * c)


def _pack_rows(parts, lead):
    flat = []
    for p in parts:
        p = p.reshape(p.shape[:lead] + (-1, PACK_W))
        pad = _round_up(p.shape[lead], 16) - p.shape[lead]
        flat.append(jnp.pad(p, [(0, 0)] * lead + [(0, pad), (0, 0)]) if pad else p)
    return jnp.concatenate(flat, axis=lead)


def _unpack_rows(packed, shapes):
    lead = packed.shape[:-2]
    out, off = [], 0
    for shp in shapes:
        rows = math.prod(shp) // PACK_W
        out.append(packed[..., off:off + rows, :].reshape(lead + tuple(shp)))
        off += _round_up(rows, 16)
    return out


def _pack_flat(parts):
    flat = jnp.concatenate([p.reshape(-1) for p in parts])
    n = flat.shape[0]
    flat = jnp.pad(flat, (0, _round_up(n, 16 * PACK_W) - n))
    return flat.reshape(-1, PACK_W)


def _unpack(packed, shapes):
    flat = packed.reshape(-1)
    out, off = [], 0
    for shp in shapes:
        n = math.prod(shp)
        out.append(flat[off:off + n].reshape(shp))
        off += n
    return out


def _ssm_discretize(a_re, a_im, log_dt, b_re, b_im):
    dt = jnp.exp(log_dt)[..., None]
    lam_re = jnp.minimum(a_re, EIG_RE_MAX)
    lam_im = a_im
    mag = jnp.exp(lam_re * dt)
    ang = lam_im * dt
    lb_re = mag * jnp.cos(ang)
    lb_im = mag * jnp.sin(ang)
    num_re = lb_re - 1.0
    num_im = lb_im
    den = lam_re * lam_re + lam_im * lam_im
    f_re = (num_re * lam_re + num_im * lam_im) / den
    f_im = (num_im * lam_re - num_re * lam_im) / den
    bb_re = f_re[..., None] * b_re - f_im[..., None] * b_im
    bb_im = f_re[..., None] * b_im + f_im[..., None] * b_re
    return lb_re, lb_im, bb_re, bb_im


def _ssm_powers(a_re, a_im, log_dt, count):
    dt = jnp.exp(log_dt)[:, None, :, None]
    k = jnp.arange(1, count + 1, dtype=F32)[None, :, None, None]
    mag = jnp.exp(k * (jnp.minimum(a_re, EIG_RE_MAX)[:, None] * dt))
    ang = k * (a_im[:, None] * dt)
    shape = (a_re.shape[0], count, -1)
    return (mag * jnp.cos(ang)).reshape(shape), (mag * jnp.sin(ang)).reshape(shape)


def _interleave(a, inverse=False):
    lead, (t, width) = a.shape[:-2], a.shape[-2:]
    seg = ROW_TILE // 8
    shape = lead + (t // ROW_TILE,) + ((seg, 8) if inverse else (8, seg)) + (width,)
    return jnp.swapaxes(a.reshape(shape), -3, -2).reshape(a.shape)


def _block_diag(blocks):
    two, g, a, b = blocks.shape
    tiled = jnp.tile(blocks.reshape(two, g * a, b), (1, 1, g))
    row_group = lax.broadcasted_iota(jnp.int32, (g * a, g * b), 0) // a
    col_group = lax.broadcasted_iota(jnp.int32, (g * a, g * b), 1) // b
    return jnp.where(row_group == col_group, tiled, 0.0).astype(BF16)


def _diag_blocks(tiles):
    two, w, _ = tiles.shape
    per = DIAG_TILE // SSM_STATE
    t6 = tiles.reshape(two, w // (per * SSM_GROUP), per, SSM_GROUP, per, SSM_STATE)
    return jnp.einsum("zjqpqn->zjqpn", t6).reshape(two, w // SSM_GROUP, SSM_GROUP, SSM_STATE)


def _rope_tables(t, n_valid):
    pos = jnp.arange(t)
    real = jnp.logical_and(pos >= N_META, pos < n_valid)
    idx = jnp.where(real, pos - N_META, 0)
    row_id = (idx // GRID_W).astype(F32)
    col_id = (idx % GRID_W).astype(F32)
    pairs_per_axis = HEAD_DIM // 4
    inv_freq = ROPE_THETA ** (-jnp.arange(pairs_per_axis, dtype=F32) / pairs_per_axis)
    ang = jnp.concatenate([row_id[:, None] * inv_freq, col_id[:, None] * inv_freq], axis=-1)
    ang = jnp.where(real[:, None], ang, 0.0)
    cos = jnp.repeat(jnp.cos(ang), 2, axis=-1)
    sin = jnp.sin(ang)
    sin = jnp.stack([-sin, sin], axis=-1).reshape(t, HEAD_DIM)
    return jnp.tile(cos, (1, 2)), jnp.tile(sin, (1, 2))


def _local_step(x, loss_target, big, small, comm=None):
    s_len, d = x.shape
    n_valid = s_len + N_META
    t = _round_up(n_valid, KV_TILE)
    du = d // 2
    groups = du // SSM_GROUP
    pad = t - n_valid

    x0 = jnp.concatenate([big["meta_tokens"].astype(F32), x, jnp.zeros((pad, d), F32)], axis=0)
    tgt = jnp.concatenate([jnp.zeros((N_META, d), F32), loss_target, jnp.zeros((pad, d), F32)], axis=0)
    cos, sin = _rope_tables(t, n_valid)
    g_mix = small["norm_mix_g"].reshape(1, d)
    g_mlp = small["norm_mlp_g"].reshape(1, d)
    g_fin = small["norm_final_g"].reshape(1, d)
    qg = jnp.tile(small["q_norm_g"].reshape(1, HEAD_DIM), (1, 2))
    kg = jnp.tile(small["k_norm_g"].reshape(1, HEAD_DIM), (1, 2))
    ssm_d = small["ssm_d"].reshape(1, du)
    b_glu = small["b_glu"].reshape(1, du)

    ssm_in = tuple(small[n][0] for n in ("ssm_a_re", "ssm_a_im", "ssm_log_dt", "ssm_b_re", "ssm_b_im"))
    (lb_re, lb_im, bbar_re, bbar_im), disc_vjp = jax.vjp(_ssm_discretize, *ssm_in)
    lam_re = lb_re.reshape(2, 1, groups * SSM_STATE)
    lam_im = lb_im.reshape(2, 1, groups * SSM_STATE)
    pow_re, pow_im = _ssm_powers(*ssm_in[0:3], ROW_TILE // 8)
    bb_re = _block_diag(bbar_re.transpose(0, 1, 3, 2))
    bb_im = _block_diag(bbar_im.transpose(0, 1, 3, 2))
    c_re, c_im = small["ssm_c_re"][0], small["ssm_c_im"][0]
    cct_re = _block_diag(c_re)
    cct_im = _block_diag(c_im)
    cc_re = cct_re.transpose(0, 2, 1)
    cc_im = cct_im.transpose(0, 2, 1)
    bbt_re = bb_re.transpose(0, 2, 1)
    bbt_im = bb_im.transpose(0, 2, 1)
    scan_w = (lam_re, lam_im, pow_re, pow_im)

    in_proj_args = (x0, g_mix, big["w_in"], qg, kg, cos, sin, n_valid)
    if comm is None:
        h, u, ub, qraw, kraw, qat, ka, kt, va, vta, gates = _in_proj_fwd(*in_proj_args)
    else:
        h, u, ub, qraw, kraw, qat, ka, kt, va, vta, gates, got = _in_proj_fwd(
            *in_proj_args, exchange=(comm["pack_weights"](MIXER_WEIGHTS), False))
        big = {**big, **comm["unpack_weights"](MIXER_WEIGHTS, got)}
    ub = _interleave(ub)
    if comm is None:
        y2, xs_re, xs_im = _ssm_scan_fwd(ub, *scan_w, bb_re, bb_im, cc_re, cc_im)
    else:
        y2, xs_re, xs_im, got = _ssm_scan_fwd(ub, *scan_w, bb_re, bb_im, cc_re, cc_im,
                                              exchange=(comm["pack_weights"](MLP_WEIGHTS), False))
        big = {**big, **comm["unpack_weights"](MLP_WEIGHTS, got)}
    y2 = _interleave(y2, inverse=True)
    yf, yb = y2[0], y2[1]
    yt_attn, ytb_attn, lse = _attn_fwd(qat, ka, vta)
    mixer_w = (ssm_d, big["w_glu"], b_glu, big["w_ssm_proj"], big["w_attn_proj"], big["w_out"])
    x1 = _mixer_out_fwd(x0, u, yf, yb, ytb_attn, gates, *mixer_w)

    dx1, loss8, dg_fin, dg_mlp, h2b, dab, hsqb, dx2b = _mlp_loss_fwd_bwd(
        x1, tgt, g_mlp, g_fin, big["w_mlp_in"], big["w_mlp_out"], n_valid)

    grads = {}
    (dyb, dud, dyt_attn, dgates, d_ssm_d, d_b_glu, grads["w_glu"], grads["w_ssm_proj"], grads["w_attn_proj"],
     grads["w_out"]) = _mixer_out_bwd(dx1, u, yf, yb, ytb_attn, gates, *mixer_w)
    grads["w_mlp_in"] = _matmul_tn(h2b, dab, "grad_w_mlp_in")
    grads["w_mlp_out"] = _matmul_tn(hsqb, dx2b, "grad_w_mlp_out")
    dk, dv, dqt = _attn_bwd(qat, ka, kt, va, dyt_attn, yt_attn, lse)
    scan_args = (_interleave(dyb), ub, xs_re, xs_im, *scan_w, cct_re, cct_im, bbt_re, bbt_im)
    if comm is None:
        late_grad_parts = None
        du2, dlam_re, dlam_im, dbr, dbi, dcr, dci = _ssm_scan_bwd(*scan_args)
    else:
        du2, dlam_re, dlam_im, dbr, dbi, dcr, dci, late_grad_parts = _ssm_scan_bwd(
            *scan_args, exchange=(comm["pack_grads"](LATE_WEIGHTS, grads), True))
    du2 = _interleave(du2, inverse=True)
    dx0, dproj, dg_mix, dqg, dkg = _in_proj_bwd(x0, dx1, dud, du2[0], du2[1], qraw, kraw, dqt, dk, dv, dgates,
                                                g_mix, big["w_in"], qg, kg, cos, sin)

    grads["meta_tokens"] = dx0[0:N_META]
    dbb_re = _diag_blocks(dbr).transpose(0, 1, 3, 2)
    dbb_im = _diag_blocks(dbi).transpose(0, 1, 3, 2)
    dc_re, dc_im = _diag_blocks(dcr), _diag_blocks(dci)
    shape_gn = (2, groups, SSM_STATE)
    d_a_re, d_a_im, d_log_dt, d_b_re, d_b_im = disc_vjp(
        (dlam_re.reshape(shape_gn), dlam_im.reshape(shape_gn), dbb_re, dbb_im))
    grads.update({
        "norm_mix_g": dg_mix, "ssm_a_re": d_a_re[None], "ssm_a_im": d_a_im[None], "ssm_log_dt": d_log_dt[None],
        "ssm_b_re": d_b_re[None], "ssm_b_im": d_b_im[None], "ssm_c_re": dc_re[None], "ssm_c_im": dc_im[None],
        "ssm_d": d_ssm_d, "b_glu": d_b_glu,
        "q_norm_g": dqg[:, 0:HEAD_DIM] + dqg[:, HEAD_DIM:128], "k_norm_g": dkg[:, 0:HEAD_DIM] + dkg[:, HEAD_DIM:128],
        "norm_mlp_g": dg_mlp, "norm_final_g": dg_fin.reshape(d),
    })
    if comm is None:
        small_grad_parts = None
        grads["w_in"] = _matmul_tn(h, dproj, "grad_w_in")
    else:
        grads["w_in"], small_grad_parts = _matmul_tn(h, dproj, "grad_w_in",
                                                     exchange=(comm["pack_small_grads"](grads), False))
    return loss8[0, 0], dx0[N_META:n_valid], grads, late_grad_parts, small_grad_parts


def kernel(x, meta_tokens, norm_mix_g, w_in, ssm_a_re, ssm_a_im, ssm_log_dt, ssm_b_re, ssm_b_im, ssm_c_re, ssm_c_im, ssm_d, w_glu, b_glu, q_norm_g, k_norm_g, w_ssm_proj, w_attn_proj, w_out, norm_mlp_g, w_mlp_in, w_mlp_out, norm_final_g, loss_target, m_meta_tokens, m_norm_mix_g, m_w_in, m_ssm_a_re, m_ssm_a_im, m_ssm_log_dt, m_ssm_b_re, m_ssm_b_im, m_ssm_c_re, m_ssm_c_im, m_ssm_d, m_w_glu, m_b_glu, m_q_norm_g, m_k_norm_g, m_w_ssm_proj, m_w_attn_proj, m_w_out, m_norm_mlp_g, m_w_mlp_in, m_w_mlp_out, m_norm_final_g, v_meta_tokens, v_norm_mix_g, v_w_in, v_ssm_a_re, v_ssm_a_im, v_ssm_log_dt, v_ssm_b_re, v_ssm_b_im, v_ssm_c_re, v_ssm_c_im, v_ssm_d, v_w_glu, v_b_glu, v_q_norm_g, v_k_norm_g, v_w_ssm_proj, v_w_attn_proj, v_w_out, v_norm_mlp_g, v_w_mlp_in, v_w_mlp_out, v_norm_final_g):
    w = dict(meta_tokens=meta_tokens, norm_mix_g=norm_mix_g, w_in=w_in, ssm_a_re=ssm_a_re, ssm_a_im=ssm_a_im, ssm_log_dt=ssm_log_dt, ssm_b_re=ssm_b_re, ssm_b_im=ssm_b_im, ssm_c_re=ssm_c_re, ssm_c_im=ssm_c_im, ssm_d=ssm_d, w_glu=w_glu, b_glu=b_glu, q_norm_g=q_norm_g, k_norm_g=k_norm_g, w_ssm_proj=w_ssm_proj, w_attn_proj=w_attn_proj, w_out=w_out, norm_mlp_g=norm_mlp_g, w_mlp_in=w_mlp_in, w_mlp_out=w_mlp_out, norm_final_g=norm_final_g)
    m = dict(meta_tokens=m_meta_tokens, norm_mix_g=m_norm_mix_g, w_in=m_w_in, ssm_a_re=m_ssm_a_re, ssm_a_im=m_ssm_a_im, ssm_log_dt=m_ssm_log_dt, ssm_b_re=m_ssm_b_re, ssm_b_im=m_ssm_b_im, ssm_c_re=m_ssm_c_re, ssm_c_im=m_ssm_c_im, ssm_d=m_ssm_d, w_glu=m_w_glu, b_glu=m_b_glu, q_norm_g=m_q_norm_g, k_norm_g=m_k_norm_g, w_ssm_proj=m_w_ssm_proj, w_attn_proj=m_w_attn_proj, w_out=m_w_out, norm_mlp_g=m_norm_mlp_g, w_mlp_in=m_w_mlp_in, w_mlp_out=m_w_mlp_out, norm_final_g=m_norm_final_g)
    v = dict(meta_tokens=v_meta_tokens, norm_mix_g=v_norm_mix_g, w_in=v_w_in, ssm_a_re=v_ssm_a_re, ssm_a_im=v_ssm_a_im, ssm_log_dt=v_ssm_log_dt, ssm_b_re=v_ssm_b_re, ssm_b_im=v_ssm_b_im, ssm_c_re=v_ssm_c_re, ssm_c_im=v_ssm_c_im, ssm_d=v_ssm_d, w_glu=v_w_glu, b_glu=v_b_glu, q_norm_g=v_q_norm_g, k_norm_g=v_k_norm_g, w_ssm_proj=v_w_ssm_proj, w_attn_proj=v_w_attn_proj, w_out=v_w_out, norm_mlp_g=v_norm_mlp_g, w_mlp_in=v_w_mlp_in, w_mlp_out=v_w_mlp_out, norm_final_g=v_norm_final_g)

    shard2d = {n: w[n].reshape(w[n].shape[-2:]) for n in BIG_WEIGHTS}

    meta_hi = shard2d["meta_tokens"].astype(BF16)
    meta_res = shard2d["meta_tokens"] - meta_hi.astype(F32)
    meta_mid = meta_res.astype(BF16)
    meta_lo = (meta_res - meta_mid.astype(F32)).astype(BF16)
    shapes_of = lambda names: [shard2d[n].shape for n in names]

    def full_weights(names, shards):
        return {n: s if n in BLOCK_WEIGHTS else _from_shards(s, BIG_SHARD_AXIS[n]) for n, s in zip(names, shards)}

    early = _gather_two_level(_pack_rows([meta_hi, meta_mid, meta_lo, shard2d["w_in"].astype(BF16)], 0),
                              "gather_early_weights")
    shards = _unpack_rows(early, [meta_hi.shape] * 3 + shapes_of(EARLY_WEIGHTS[1:]))
    meta = [_from_shards(s, 1).astype(F32) for s in shards[0:3]]
    big = {"meta_tokens": (meta[0] + meta[1]) + meta[2], **full_weights(EARLY_WEIGHTS[1:], shards[3:])}
    small = {n: w[n] for n in SMALL_WEIGHTS}
    pack_grads = lambda names, grads: _pack_rows(
        [_to_shards(grads[n], BIG_SHARD_AXIS[n]) for n in names], 1).astype(BF16)
    comm = {
        "pack_weights": lambda names: _pack_rows([shard2d[n].astype(BF16) for n in names], 0),
        "unpack_weights": lambda names, g: full_weights(names, _unpack_rows(g, shapes_of(names))),
        "pack_grads": pack_grads,
        "pack_small_grads": lambda grads: _pack_flat([grads[n] for n in SMALL_WEIGHTS]),
    }

    loss, grad_x, grads, late_parts, small_parts = _local_step(x[0], loss_target[0], big, small, comm)
    loss = lax.psum(loss, ("x", "y", "c"))

    pk = lambda names, src: _pack_rows([src[n].reshape(shard2d[n].shape) for n in names], 0)
    pe, pl_ = functools.partial(pk, EARLY_WEIGHTS), functools.partial(pk, LATE_WEIGHTS)
    *late_out, early_parts = _reduce_adamw(late_parts, pl_(w), pl_(m), pl_(v), "adamw_sharded_late",
                                           exchange=(pack_grads(EARLY_WEIGHTS, grads), True))
    early_out = _reduce_adamw(early_parts, pe(w), pe(m), pe(v), "adamw_sharded_early")
    small_shapes = [w[n].shape for n in SMALL_WEIGHTS]
    pf = lambda src: _pack_flat([src[n] for n in SMALL_WEIGHTS])
    small_out = _reduce_adamw(small_parts, pf(w), pf(m), pf(v), "adamw_replicated")

    results = []
    for kind in range(4):
        big_un = dict(zip(EARLY_WEIGHTS + LATE_WEIGHTS,
                          _unpack_rows(early_out[kind], shapes_of(EARLY_WEIGHTS))
                          + _unpack_rows(late_out[kind], shapes_of(LATE_WEIGHTS))))
        small_un = dict(zip(SMALL_WEIGHTS, _unpack(small_out[kind], small_shapes)))
        for n in ALL_WEIGHTS:
            results.append(big_un[n].reshape(w[n].shape) if n in big_un else small_un[n])
    return (loss, grad_x[None], *results)
```
